```python
import math, functools
import jax, jax.numpy as jnp
from jax import lax
import numpy as np

D_MODEL = 1024
BATCH = 8
SEQ = 8192
DEPTH = 2

N_MEM = 256
D_MAIN = D_MODEL
N_POOL_GROUPS = 4
POOL_WINDOWS = (2, 4, 8, 16)
POOL_GROUP = D_MAIN // N_POOL_GROUPS
FOX_HEADS = 16
FOX_HEAD_DIM = D_MAIN // FOX_HEADS
MEM_HEADS = 4
MEM_HEAD_DIM = 128
D_MEM = MEM_HEADS * MEM_HEAD_DIM
D_MIX = D_MAIN + D_MEM
D_IN = 2 * D_MIX
N_A = DEPTH // 2
N_B = DEPTH - N_A
Q_BLOCK = 128
ALPHA = (2 * DEPTH) ** 0.25
BETA = (8 * DEPTH) ** -0.25
LN_EPS = 1e-5

kernel_name = "yoco_pool_fox_memory_deepnorm"


def layer_norm(x, g, b):
    xf = x.astype(jnp.float32)
    mu = jnp.mean(xf, axis=-1, keepdims=True)
    var = jnp.mean(jnp.square(xf - mu), axis=-1, keepdims=True)
    y = (xf - mu) * lax.rsqrt(var + LN_EPS) * g.astype(jnp.float32) + b.astype(jnp.float32)
    return y.astype(x.dtype)


def causal_multiscale_pool(u, pool_w, pool_scale):
    B, S, _ = u.shape
    ug = u.reshape(B, S, N_POOL_GROUPS, POOL_GROUP)
    cs = jnp.cumsum(ug.astype(jnp.float32), axis=1)
    cs = jnp.pad(cs, ((0, 0), (1, 0), (0, 0), (0, 0)))
    t = jnp.arange(S)
    outs = []
    for gi, w in enumerate(POOL_WINDOWS):
        c = cs[:, :, gi]
        upper = c[:, 1:]
        lower = jnp.concatenate(
            [jnp.zeros((B, w - 1, POOL_GROUP), jnp.float32), c[:, :S + 1 - w]], axis=1)
        count = jnp.minimum(t + 1, w).astype(jnp.float32)[None, :, None]
        outs.append((upper - lower) / count - ug[:, :, gi].astype(jnp.float32))
    pm = jnp.stack(outs, axis=2).astype(u.dtype)
    mixed = jnp.einsum('bsgc,gcd->bsgd', pm, pool_w)
    return mixed.reshape(B, S, D_MAIN) * pool_scale


def shared_kv(x, w_kv_shared, b_forget):
    B, S, _ = x.shape
    h = x @ w_kv_shared
    k = h[..., :D_MAIN].reshape(B, S, FOX_HEADS, FOX_HEAD_DIM)
    v = h[..., D_MAIN:2 * D_MAIN].reshape(B, S, FOX_HEADS, FOX_HEAD_DIM)
    f_logit = h[..., 2 * D_MAIN:].astype(jnp.float32) + b_forget.astype(jnp.float32)
    log_f = jax.nn.log_sigmoid(f_logit)
    cum = jnp.cumsum(log_f, axis=1)
    return k, v, cum


def forgetting_attention(u, k, v, cum):
    B, S, _ = u.shape
    q = u.reshape(B, S, FOX_HEADS, FOX_HEAD_DIM)
    nb = S // Q_BLOCK
    qb = q.reshape(B, nb, Q_BLOCK, FOX_HEADS, FOX_HEAD_DIM).transpose(1, 0, 2, 3, 4)
    cb = cum.reshape(B, nb, Q_BLOCK, FOX_HEADS).transpose(1, 0, 3, 2)
    cum_k = cum.transpose(0, 2, 1)
    starts = jnp.arange(nb) * Q_BLOCK
    kpos = jnp.arange(S)
    scale = FOX_HEAD_DIM ** -0.5

    def block(args):
        qi, ci, s0 = args
        logits = jnp.einsum('bqhd,bkhd->bhqk', qi, k,
                            preferred_element_type=jnp.float32) * scale
        logits = logits + (ci[..., :, None] - cum_k[:, :, None, :])
        qpos = s0 + jnp.arange(Q_BLOCK)
        mask = kpos[None, :] <= qpos[:, None]
        logits = jnp.where(mask[None, None], logits, -jnp.inf)
        p = jax.nn.softmax(logits, axis=-1)
        return jnp.einsum('bhqk,bkhd->bqhd', p.astype(v.dtype), v)

    out = lax.map(block, (qb, cb, starts))
    return out.transpose(1, 0, 2, 3, 4).reshape(B, S, D_MAIN)


def memory_attention(q_mem, mem, w_mem_kv):
    B, S, _ = q_mem.shape
    M = mem.shape[1]
    mkv = mem @ w_mem_kv
    mk = mkv[..., :D_MEM].reshape(B, M, MEM_HEADS, MEM_HEAD_DIM)
    mv = mkv[..., D_MEM:].reshape(B, M, MEM_HEADS, MEM_HEAD_DIM)
    q = q_mem.reshape(B, S, MEM_HEADS, MEM_HEAD_DIM)
    logits = jnp.einsum('bshd,bmhd->bhsm', q, mk,
                        preferred_element_type=jnp.float32) * (MEM_HEAD_DIM ** -0.5)
    p = jax.nn.softmax(logits, axis=-1)
    return jnp.einsum('bhsm,bmhd->bshd', p.astype(mv.dtype), mv).reshape(B, S, D_MEM)


def mixer_sublayer(x, mem, w_in, w_mem_kv, w_out, main_fn):
    h = x @ w_in
    u_main = h[..., :D_MAIN]
    q_mem = h[..., D_MAIN:D_MIX]
    g_main = h[..., D_MIX:D_MIX + D_MAIN]
    g_mem = h[..., D_MIX + D_MAIN:]
    y_main = main_fn(u_main)
    y_mem = memory_attention(q_mem, mem, w_mem_kv)
    y = jnp.concatenate([y_main * jax.nn.silu(g_main), y_mem * jax.nn.silu(g_mem)], axis=-1)
    return y @ w_out


def _fwd_setup_inputs(seed: int = 0) -> dict:
    key = jax.random.key(seed)
    ks = jax.random.split(key, 12)
    x = jax.random.normal(ks[0], (BATCH, SEQ, D_MODEL), jnp.float32)
    mem = jax.random.normal(ks[1], (BATCH, N_MEM, D_MODEL), jnp.float32)
    w_in = jax.random.normal(ks[2], (DEPTH, D_MODEL, D_IN), jnp.float32) * D_MODEL ** -0.5
    w_mem_kv = jax.random.normal(ks[3], (DEPTH, D_MODEL, 2 * D_MEM), jnp.float32) * D_MODEL ** -0.5
    w_out = jax.random.normal(ks[4], (DEPTH, D_MIX, D_MODEL), jnp.float32) * (D_MIX ** -0.5 * BETA)
    ln_g = 1.0 + 0.02 * jax.random.normal(ks[5], (DEPTH, D_MODEL), jnp.float32)
    ln_b = 0.02 * jax.random.normal(ks[6], (DEPTH, D_MODEL), jnp.float32)
    pool_w = jax.random.normal(ks[7], (N_A, N_POOL_GROUPS, POOL_GROUP, POOL_GROUP), jnp.float32) * POOL_GROUP ** -0.5
    pool_scale = 1.0 + 0.1 * jax.random.normal(ks[8], (N_A, D_MAIN), jnp.float32)
    w_kv_shared = jax.random.normal(ks[9], (D_MODEL, 2 * D_MAIN + FOX_HEADS), jnp.float32) * D_MODEL ** -0.5
    b_forget = jax.random.uniform(ks[10], (FOX_HEADS,), jnp.float32, 1.0, 5.0)
    return {"x": x, "mem": mem, "w_in": w_in, "w_mem_kv": w_mem_kv, "w_out": w_out,
            "ln_g": ln_g, "ln_b": ln_b, "pool_w": pool_w, "pool_scale": pool_scale,
            "w_kv_shared": w_kv_shared, "b_forget": b_forget}


def _fwd_reference(x, mem, w_in, w_mem_kv, w_out, ln_g, ln_b, pool_w, pool_scale,
              w_kv_shared, b_forget):
    k_sh = v_sh = cum_sh = None
    for layer in range(DEPTH):
        if layer < N_A:
            main_fn = functools.partial(causal_multiscale_pool,
                                        pool_w=pool_w[layer], pool_scale=pool_scale[layer])
        else:
            if layer == N_A:
                k_sh, v_sh, cum_sh = shared_kv(x, w_kv_shared, b_forget)
            main_fn = functools.partial(forgetting_attention, k=k_sh, v=v_sh, cum=cum_sh)
        y = mixer_sublayer(x, mem, w_in[layer], w_mem_kv[layer], w_out[layer], main_fn)
        x = layer_norm(ALPHA * x + y, ln_g[layer], ln_b[layer])
    return x


import jax as _jax
import jax.numpy as _jnp

TWIN_FORMAT = 'train_step'
FWD_PARAMS = ['x', 'mem', 'w_in', 'w_mem_kv', 'w_out', 'ln_g', 'ln_b', 'pool_w', 'pool_scale', 'w_kv_shared', 'b_forget']
TWIN_WEIGHTS = ['w_in', 'w_mem_kv', 'w_out', 'ln_g', 'ln_b', 'pool_w', 'pool_scale', 'w_kv_shared', 'b_forget']
TWIN_DIFF_INPUT = 'x'
TWIN_INPUTS = ['x', 'mem', 'w_in', 'w_mem_kv', 'w_out', 'ln_g', 'ln_b', 'pool_w', 'pool_scale', 'w_kv_shared', 'b_forget', 'loss_target', 'm_w_in', 'm_w_mem_kv', 'm_w_out', 'm_ln_g', 'm_ln_b', 'm_pool_w', 'm_pool_scale', 'm_w_kv_shared', 'm_b_forget', 'v_w_in', 'v_w_mem_kv', 'v_w_out', 'v_ln_g', 'v_ln_b', 'v_pool_w', 'v_pool_scale', 'v_w_kv_shared', 'v_b_forget']
TWIN_OUTPUTS = ['loss', 'grad_x', 'grad_w_in', 'grad_w_mem_kv', 'grad_w_out', 'grad_ln_g', 'grad_ln_b', 'grad_pool_w', 'grad_pool_scale', 'grad_w_kv_shared', 'grad_b_forget', 'delta_w_in', 'delta_w_mem_kv', 'delta_w_out', 'delta_ln_g', 'delta_ln_b', 'delta_pool_w', 'delta_pool_scale', 'delta_w_kv_shared', 'delta_b_forget', 'new_m_w_in', 'new_m_w_mem_kv', 'new_m_w_out', 'new_m_ln_g', 'new_m_ln_b', 'new_m_pool_w', 'new_m_pool_scale', 'new_m_w_kv_shared', 'new_m_b_forget', 'new_v_w_in', 'new_v_w_mem_kv', 'new_v_w_out', 'new_v_ln_g', 'new_v_ln_b', 'new_v_pool_w', 'new_v_pool_scale', 'new_v_w_kv_shared', 'new_v_b_forget']
TWIN_LEAF_KINDS = {'loss': 'loss', 'grad_x': 'grad_x', 'grad_w_in': 'grad_w', 'grad_w_mem_kv': 'grad_w', 'grad_w_out': 'grad_w', 'grad_ln_g': 'grad_w', 'grad_ln_b': 'grad_w', 'grad_pool_w': 'grad_w', 'grad_pool_scale': 'grad_w', 'grad_w_kv_shared': 'grad_w', 'grad_b_forget': 'grad_w', 'delta_w_in': 'delta_w', 'delta_w_mem_kv': 'delta_w', 'delta_w_out': 'delta_w', 'delta_ln_g': 'delta_w', 'delta_ln_b': 'delta_w', 'delta_pool_w': 'delta_w', 'delta_pool_scale': 'delta_w', 'delta_w_kv_shared': 'delta_w', 'delta_b_forget': 'delta_w', 'new_m_w_in': 'new_m', 'new_m_w_mem_kv': 'new_m', 'new_m_w_out': 'new_m', 'new_m_ln_g': 'new_m', 'new_m_ln_b': 'new_m', 'new_m_pool_w': 'new_m', 'new_m_pool_scale': 'new_m', 'new_m_w_kv_shared': 'new_m', 'new_m_b_forget': 'new_m', 'new_v_w_in': 'new_v', 'new_v_w_mem_kv': 'new_v', 'new_v_w_out': 'new_v', 'new_v_ln_g': 'new_v', 'new_v_ln_b': 'new_v', 'new_v_pool_w': 'new_v', 'new_v_pool_scale': 'new_v', 'new_v_w_kv_shared': 'new_v', 'new_v_b_forget': 'new_v'}


def _forward(args):
    return _fwd_reference(*[args[k] for k in FWD_PARAMS])


def _output_shape():
    def fwd():
        inp = _fwd_setup_inputs(0)
        return _fwd_reference(*[inp[k] for k in FWD_PARAMS])
    out = _jax.eval_shape(fwd)
    return out.shape, out.dtype

N_MICROBATCH = 1
ADAM_LR = 0.001
ADAM_B1 = 0.9
ADAM_B2 = 0.999
ADAM_EPS = 1e-08
ADAM_WD = 0.01
ADAM_STEP = 10
PER_EXAMPLE_BATCH_AXIS = {'x': 0, 'mem': 0, 'loss_target': 0}
SHARED_INPUTS = []
_WEIGHT_DTYPES = {'w_in': _jnp.float32, 'w_mem_kv': _jnp.float32, 'w_out': _jnp.float32, 'ln_g': _jnp.float32, 'ln_b': _jnp.float32, 'pool_w': _jnp.float32, 'pool_scale': _jnp.float32, 'w_kv_shared': _jnp.float32, 'b_forget': _jnp.float32}
MOMENT_SCALE = {'w_in': 2.354812e-02, 'w_mem_kv': 4.568884e-03, 'w_out': 5.753931e-02, 'ln_g': 4.530745e+01, 'ln_b': 1.201595e+00, 'pool_w': 3.782407e-02, 'pool_scale': 3.788478e-02, 'w_kv_shared': 1.338846e-02, 'b_forget': 7.860108e-02}


def _to_microbatches(a, axis):
    t = _jnp.moveaxis(a, axis, 0)
    t = t.reshape((N_MICROBATCH, t.shape[0] // N_MICROBATCH) + t.shape[1:])
    return _jnp.moveaxis(t, 1, axis + 1)


def setup_inputs(seed: int = 0) -> dict:
    inp = _fwd_setup_inputs(seed)
    key = _jax.random.fold_in(_jax.random.key(seed), 7919)
    shape, _ = _output_shape()
    out = dict(inp)
    out["loss_target"] = _jax.random.normal(_jax.random.fold_in(key, 0), shape, _jnp.float32)
    for i, name in enumerate(TWIN_WEIGHTS):
        w = inp[name].astype(_jnp.float32)
        if MOMENT_SCALE is None:
            s = _jnp.sqrt(_jnp.mean(_jnp.square(w)) + 1e-30)
        else:
            s = MOMENT_SCALE[name]
        km, kv = _jax.random.split(_jax.random.fold_in(key, i + 1))
        out[name] = w
        out["m_" + name] = s * _jax.random.normal(km, w.shape, _jnp.float32)
        out["v_" + name] = (s * s) * _jax.random.uniform(kv, w.shape, _jnp.float32, 0.5, 1.5)
    if N_MICROBATCH > 1:
        for name, axis in PER_EXAMPLE_BATCH_AXIS.items():
            out[name] = _to_microbatches(out[name], axis)
    return {'x': out['x'], 'mem': out['mem'], 'w_in': out['w_in'], 'w_mem_kv': out['w_mem_kv'], 'w_out': out['w_out'], 'ln_g': out['ln_g'], 'ln_b': out['ln_b'], 'pool_w': out['pool_w'], 'pool_scale': out['pool_scale'], 'w_kv_shared': out['w_kv_shared'], 'b_forget': out['b_forget'], 'loss_target': out['loss_target'], 'm_w_in': out['m_w_in'], 'm_w_mem_kv': out['m_w_mem_kv'], 'm_w_out': out['m_w_out'], 'm_ln_g': out['m_ln_g'], 'm_ln_b': out['m_ln_b'], 'm_pool_w': out['m_pool_w'], 'm_pool_scale': out['m_pool_scale'], 'm_w_kv_shared': out['m_w_kv_shared'], 'm_b_forget': out['m_b_forget'], 'v_w_in': out['v_w_in'], 'v_w_mem_kv': out['v_w_mem_kv'], 'v_w_out': out['v_w_out'], 'v_ln_g': out['v_ln_g'], 'v_ln_b': out['v_ln_b'], 'v_pool_w': out['v_pool_w'], 'v_pool_scale': out['v_pool_scale'], 'v_w_kv_shared': out['v_w_kv_shared'], 'v_b_forget': out['v_b_forget']}


def _loss(weights, diff, rest, loss_target):
    with _jax.named_scope("forward"):
        args = {**rest, TWIN_DIFF_INPUT: diff, **{k: w.astype(_WEIGHT_DTYPES[k]) for k, w in weights.items()}}
        y = _forward(args)
    with _jax.named_scope("loss_head"):
        err = _jnp.square(y.astype(_jnp.float32) - loss_target)
        return 0.5 * _jnp.sum(_jnp.mean(err, axis=-1)) if err.ndim else 0.5 * err


def _adamw(w, g, m, v):
    m = ADAM_B1 * m + (1.0 - ADAM_B1) * g
    v = ADAM_B2 * v + (1.0 - ADAM_B2) * _jnp.square(g)
    m_hat = m / (1.0 - ADAM_B1 ** ADAM_STEP)
    v_hat = v / (1.0 - ADAM_B2 ** ADAM_STEP)
    delta = -ADAM_LR * (m_hat / (_jnp.sqrt(v_hat) + ADAM_EPS) + ADAM_WD * w)
    return delta, m, v


def reference(x, mem, w_in, w_mem_kv, w_out, ln_g, ln_b, pool_w, pool_scale, w_kv_shared, b_forget, loss_target, m_w_in, m_w_mem_kv, m_w_out, m_ln_g, m_ln_b, m_pool_w, m_pool_scale, m_w_kv_shared, m_b_forget, v_w_in, v_w_mem_kv, v_w_out, v_ln_g, v_ln_b, v_pool_w, v_pool_scale, v_w_kv_shared, v_b_forget):
    given = dict(x=x, mem=mem, w_in=w_in, w_mem_kv=w_mem_kv, w_out=w_out, ln_g=ln_g, ln_b=ln_b, pool_w=pool_w, pool_scale=pool_scale, w_kv_shared=w_kv_shared, b_forget=b_forget, loss_target=loss_target, m_w_in=m_w_in, m_w_mem_kv=m_w_mem_kv, m_w_out=m_w_out, m_ln_g=m_ln_g, m_ln_b=m_ln_b, m_pool_w=m_pool_w, m_pool_scale=m_pool_scale, m_w_kv_shared=m_w_kv_shared, m_b_forget=m_b_forget, v_w_in=v_w_in, v_w_mem_kv=v_w_mem_kv, v_w_out=v_w_out, v_ln_g=v_ln_g, v_ln_b=v_ln_b, v_pool_w=v_pool_w, v_pool_scale=v_pool_scale, v_w_kv_shared=v_w_kv_shared, v_b_forget=v_b_forget)
    weights = {n: given[n] for n in TWIN_WEIGHTS}
    shared = {n: given[n] for n in SHARED_INPUTS}
    per_example = {n: given[n] for n in ['x', 'mem']}
    grad_fn = _jax.value_and_grad(_loss, argnums=(0, 1))

    def one_microbatch(ex, loss_target):
        ex = dict(ex)
        diff = ex.pop(TWIN_DIFF_INPUT)
        return grad_fn(weights, diff, {**shared, **ex}, loss_target)

    if N_MICROBATCH == 1:
        loss, (grad_w, grad_x) = one_microbatch(per_example, given["loss_target"])
    else:
        def body(carry, xs):
            loss_sum, grad_sum = carry
            l_k, (gw_k, gx_k) = one_microbatch(xs[0], xs[1])
            with _jax.named_scope("update"):
                return (loss_sum + l_k, _jax.tree.map(_jnp.add, grad_sum, gw_k)), gx_k

        init = (_jnp.zeros((), _jnp.float32), _jax.tree.map(_jnp.zeros_like, weights))
        (loss, grad_w), grad_x = _jax.lax.scan(body, init, (per_example, given["loss_target"]))
    with _jax.named_scope("update"):
        delta_w, new_m, new_v = {}, {}, {}
        for n in TWIN_WEIGHTS:
            delta_w[n], new_m[n], new_v[n] = _adamw(weights[n], grad_w[n], given["m_" + n], given["v_" + n])
    return (loss, grad_x, *[grad_w[n] for n in TWIN_WEIGHTS], *[delta_w[n] for n in TWIN_WEIGHTS],
            *[new_m[n] for n in TWIN_WEIGHTS], *[new_v[n] for n in TWIN_WEIGHTS])
```

```python
import functools

import jax
import jax.numpy as jnp
from jax import lax
from jax.experimental import pallas as pl
from jax.experimental.pallas import tpu as pltpu

F32 = jnp.float32
BF16 = jnp.bfloat16

D_MODEL = 1024
D_MAIN = 1024
D_MEM = 512
D_MIX = D_MAIN + D_MEM
D_IN = 2 * D_MIX
N_MEM = 256
MEM_HEADS = 4
MEM_HEAD_DIM = 128
FOX_HEADS = 16
FOX_HEAD_DIM = 64
FOX_SCALE = 0.125
POOL_WINDOWS = (2, 4, 8, 16)
POOL_GROUP = 256
POOL_HALO = 16
ALPHA = 4.0 ** 0.25
LN_EPS = 1e-5
LANES = 128
N_CHIPS = 4

ADAM_LR = 0.001
ADAM_B1 = 0.9
ADAM_B2 = 0.999
ADAM_EPS = 1e-08
ADAM_WD = 0.01
ADAM_STEP = 10

VMEM_LIMIT = 56 * 1024 * 1024

ROWS_W_IN = 2 * D_MODEL * (D_IN // N_CHIPS) // 1024
ROWS_W_MKV = 2 * (D_MODEL // N_CHIPS) * 2 * D_MEM // 1024
ROWS_W_OUT = 2 * (D_MIX // N_CHIPS) * D_MODEL // 1024
ROWS_POOL_W = 4 * (POOL_GROUP // N_CHIPS) * POOL_GROUP // 1024
KV_COLS = 2 * D_MAIN + FOX_HEADS
KV_SHARD = KV_COLS // N_CHIPS
ROWS_W_KV = 528
ROWS_SMALL = 16
OFF_W_IN = 0
OFF_W_MKV = OFF_W_IN + ROWS_W_IN
OFF_W_OUT = OFF_W_MKV + ROWS_W_MKV
OFF_POOL_W = OFF_W_OUT + ROWS_W_OUT
OFF_W_KV = OFF_POOL_W + ROWS_POOL_W
OFF_POOL_S = OFF_W_KV + ROWS_W_KV
OFF_LN_G = OFF_POOL_S + ROWS_SMALL
OFF_LN_B = OFF_LN_G + ROWS_SMALL
OFF_BF = OFF_LN_B + ROWS_SMALL
PACK_ROWS = 3584
HALF_ROWS = PACK_ROWS // 2
PACK_TILE = 256
ROWS_PAD = PACK_ROWS - (OFF_BF + ROWS_SMALL)


def _dot(a, b):
    return jnp.dot(a, b, preferred_element_type=F32)


def _dot_nt(a, b):
    return lax.dot_general(a, b, (((1,), (1,)), ((), ())), preferred_element_type=F32)


def _dot_tn(a, b):
    return lax.dot_general(a, b, (((0,), (0,)), ((), ())), preferred_element_type=F32)


def _params(n_axes=1):
    return pltpu.CompilerParams(dimension_semantics=("arbitrary",) * n_axes,
                                vmem_limit_bytes=VMEM_LIMIT)


def _const_spec(shape):
    zeros = (0,) * len(shape)
    return pl.BlockSpec(shape, lambda *_: zeros, pipeline_mode=pl.Buffered(1))


def _split3(x):
    hi = x.astype(BF16)
    r = x - hi.astype(F32)
    mid = r.astype(BF16)
    lo = (r - mid.astype(F32)).astype(BF16)
    return hi, mid, lo


def _linear_fwd(x, w, *, tm, name):
    S, K = x.shape
    N = w.shape[1]
    nc = 512 if N % 512 == 0 else N

    def body(x_ref, w_ref, o_ref):
        xb = x_ref[...].astype(BF16)
        for n0 in range(0, N, nc):
            o_ref[:, n0:n0 + nc] = _dot(xb, w_ref[:, n0:n0 + nc]).astype(BF16)

    return pl.pallas_call(
        body, name=name, grid=(S // tm,),
        in_specs=[pl.BlockSpec((tm, K), lambda i: (i, 0)), _const_spec((K, N))],
        out_specs=pl.BlockSpec((tm, N), lambda i: (i, 0)),
        out_shape=jax.ShapeDtypeStruct((S, N), BF16),
        compiler_params=_params(),
    )(x, w)


def _kv_proj(x1, wk, wv, wf, bf_row, *, tm):
    S = x1.shape[0]

    def body(x_ref, wk_ref, wv_ref, wf_ref, b_ref, k_ref, v_ref, fl_ref, cum_ref, carry_ref):
        i = pl.program_id(0)

        @pl.when(i == 0)
        def _():
            carry_ref[...] = jnp.zeros_like(carry_ref)

        xb = x_ref[...].astype(BF16)
        for n0 in range(0, D_MAIN, 512):
            k_ref[:, n0:n0 + 512] = _dot(xb, wk_ref[:, n0:n0 + 512]).astype(BF16)
            v_ref[:, n0:n0 + 512] = _dot(xb, wv_ref[:, n0:n0 + 512]).astype(BF16)
        fl = _dot(xb, wf_ref[...]) + b_ref[...]
        fl_ref[...] = fl
        log_f = jnp.minimum(fl, 0.0) - jnp.log1p(jnp.exp(-jnp.abs(fl)))
        r = lax.broadcasted_iota(jnp.int32, (tm, tm), 0)
        c = lax.broadcasted_iota(jnp.int32, (tm, tm), 1)
        tri = jnp.where(c <= r, 1.0, 0.0).astype(BF16)
        hi, mid, lo = _split3(log_f)
        cum = (_dot(tri, hi) + _dot(tri, mid)) + _dot(tri, lo) + carry_ref[0:1, :]
        cum_ref[...] = cum
        carry_ref[0:1, :] = cum[tm - 1:tm, :]

    return pl.pallas_call(
        body, name="kv_proj", grid=(S // tm,),
        in_specs=[pl.BlockSpec((tm, D_MODEL), lambda i: (i, 0)),
                  _const_spec((D_MODEL, D_MAIN)), _const_spec((D_MODEL, D_MAIN)),
                  _const_spec((D_MODEL, LANES)), _const_spec((1, LANES))],
        out_specs=[pl.BlockSpec((tm, D_MAIN), lambda i: (i, 0)),
                   pl.BlockSpec((tm, D_MAIN), lambda i: (i, 0)),
                   pl.BlockSpec((tm, LANES), lambda i: (i, 0)),
                   pl.BlockSpec((tm, LANES), lambda i: (i, 0))],
        out_shape=[jax.ShapeDtypeStruct((S, D_MAIN), BF16), jax.ShapeDtypeStruct((S, D_MAIN), BF16),
                   jax.ShapeDtypeStruct((S, LANES), F32), jax.ShapeDtypeStruct((S, LANES), F32)],
        scratch_shapes=[pltpu.VMEM((8, LANES), F32)],
        compiler_params=_params(),
    )(x1, wk, wv, wf, bf_row)


def _gate_bwd(dcq, dck, fl, *, tm):
    S = fl.shape[0]
    n = S // tm

    def body(dcq_ref, dck_ref, fl_ref, df_ref, db_ref, carry_ref):
        i = pl.program_id(0)

        @pl.when(i == 0)
        def _():
            carry_ref[...] = jnp.zeros_like(carry_ref)
            db_ref[...] = jnp.zeros_like(db_ref)

        dcum = dcq_ref[...] - dck_ref[...]
        r = lax.broadcasted_iota(jnp.int32, (tm, tm), 0)
        c = lax.broadcasted_iota(jnp.int32, (tm, tm), 1)
        tri = jnp.where(c >= r, 1.0, 0.0).astype(BF16)
        hi, mid, lo = _split3(dcum)
        rev = (_dot(tri, hi) + _dot(tri, mid)) + _dot(tri, lo) + carry_ref[0:1, :]
        carry_ref[0:1, :] = rev[0:1, :]
        fl_v = fl_ref[...]
        df = rev * (1.0 / (1.0 + jnp.exp(fl_v)))
        df_ref[...] = df
        db_ref[...] += jnp.sum(df, axis=0, keepdims=True)

    return pl.pallas_call(
        body, name="gate_bwd", grid=(n,),
        in_specs=[pl.BlockSpec((tm, LANES), lambda i: (n - 1 - i, 0)),
                  pl.BlockSpec((tm, LANES), lambda i: (n - 1 - i, 0)),
                  pl.BlockSpec((tm, LANES), lambda i: (n - 1 - i, 0))],
        out_specs=[pl.BlockSpec((tm, LANES), lambda i: (n - 1 - i, 0)),
                   pl.BlockSpec((1, LANES), lambda i: (0, 0))],
        out_shape=[jax.ShapeDtypeStruct((S, LANES), F32), jax.ShapeDtypeStruct((1, LANES), F32)],
        scratch_shapes=[pltpu.VMEM((8, LANES), F32)],
        compiler_params=_params(),
    )(dcq, dck, fl)


def _silu_and_grad(g):
    sg = 1.0 / (1.0 + jnp.exp(-g))
    return g * sg, sg * (1.0 + g * (1.0 - sg))


def _mix(kind, mode, *, h, xres=None, mkv, w_out, ln_g, ln_b=None, pool_w=None, pool_scale=None,
         ymain=None, target=None, z=None, dy=None, tm):
    S = h.shape[0]
    n = S // tm
    pool = kind == "pool"
    bwd = mode == "bwd"
    loss_head = bwd and not pool
    rev = pool and bwd
    mem_scale = MEM_HEAD_DIM ** -0.5

    def t_of(i):
        return (n - 1 - i) if rev else i

    row = lambda i: (t_of(i), 0)
    names, arrays, specs = [], [], []

    def add(name, arr, spec):
        names.append(name)
        arrays.append(arr)
        specs.append(spec)

    add("h", h, pl.BlockSpec((tm, D_IN), row))
    if pool:
        hb = tm // POOL_HALO
        add("halo", h, pl.BlockSpec((POOL_HALO, D_MAIN), lambda i: (jnp.maximum(t_of(i) * hb - 1, 0), 0)))
        add("pool_w", pool_w, _const_spec((4, POOL_GROUP, POOL_GROUP)))
        add("pool_scale", pool_scale, _const_spec((1, D_MAIN)))
    else:
        add("ymain", ymain, pl.BlockSpec((tm, D_MAIN), row))
    add("mkv", mkv, _const_spec((N_MEM, 2 * D_MEM)))
    add("w_out", w_out, _const_spec((D_MIX, D_MODEL)))
    add("ln_g", ln_g, _const_spec((1, D_MODEL)))
    if not (pool and bwd):
        add("xres", xres, pl.BlockSpec((tm, D_MODEL), row))
        add("ln_b", ln_b, _const_spec((1, D_MODEL)))
    if loss_head:
        add("target", target, pl.BlockSpec((tm, D_MODEL), row))
    if pool and bwd:
        add("z", z, pl.BlockSpec((tm, D_MODEL), row))
        add("dy", dy, pl.BlockSpec((tm, D_MODEL), row))

    onames, oshapes, ospecs = [], [], []

    def add_out(name, shape, dtype, spec):
        onames.append(name)
        oshapes.append(jax.ShapeDtypeStruct(shape, dtype))
        ospecs.append(spec)

    const2 = lambda i: (0, 0)
    if not bwd:
        add_out("z", (S, D_MODEL), F32, pl.BlockSpec((tm, D_MODEL), row))
        add_out("xout", (S, D_MODEL), F32, pl.BlockSpec((tm, D_MODEL), row))
    else:
        add_out("dz", (S, D_MODEL), F32, pl.BlockSpec((tm, D_MODEL), row))
        add_out("dmain", (S, D_MAIN), BF16, pl.BlockSpec((tm, D_MAIN), row))
        add_out("drest", (S, D_IN - D_MAIN), BF16, pl.BlockSpec((tm, D_IN - D_MAIN), row))
        add_out("dw_out", (D_MIX, D_MODEL), F32, pl.BlockSpec((D_MIX, D_MODEL), const2))
        add_out("dmkv", (N_MEM, 2 * D_MEM), F32, pl.BlockSpec((N_MEM, 2 * D_MEM), const2))
        add_out("dln_g", (1, D_MODEL), F32, pl.BlockSpec((1, D_MODEL), const2))
        add_out("dln_b", (1, D_MODEL), F32, pl.BlockSpec((1, D_MODEL), const2))
        if pool:
            add_out("dpool_w", (4, POOL_GROUP, POOL_GROUP), F32,
                    pl.BlockSpec((4, POOL_GROUP, POOL_GROUP), lambda i: (0, 0, 0)))
            add_out("dpool_scale", (1, D_MAIN), F32, pl.BlockSpec((1, D_MAIN), const2))
        else:
            add_out("loss", (1, D_MODEL), F32, pl.BlockSpec((1, D_MODEL), const2))
            add_out("drow", (FOX_HEADS, S), F32, pl.BlockSpec((FOX_HEADS, tm), lambda i: (0, i)))

    scratch = [pltpu.VMEM((tm, D_MIX), BF16),
               pltpu.VMEM((tm, D_MEM), F32)]
    if pool:
        scratch.append(pltpu.VMEM((tm + 2 * POOL_HALO, D_MAIN), F32))
    if rev:
        scratch.append(pltpu.VMEM((POOL_HALO, D_MAIN), F32))
    n_in, n_out = len(names), len(onames)

    def body(*refs):
        R = dict(zip(names, refs[:n_in]))
        O = dict(zip(onames, refs[n_in:n_in + n_out]))
        sc = refs[n_in + n_out:]
        yc_ref, ymem_ref = sc[0], sc[1]
        ext_ref = sc[2] if pool else None
        carry_ref = sc[3] if rev else None
        i = pl.program_id(0)
        t = t_of(i)
        h_ref = R["h"]
        gamma = R["ln_g"][...]

        if bwd:
            @pl.when(i == 0)
            def _():
                for nm in ("dw_out", "dmkv", "dln_g", "dln_b", "dpool_w", "dpool_scale", "loss"):
                    if nm in O:
                        O[nm][...] = jnp.zeros_like(O[nm])
                if rev:
                    carry_ref[...] = jnp.zeros_like(carry_ref)

        if pool:
            u = h_ref[:, 0:D_MAIN].astype(F32)
            halo = R["halo"][...].astype(F32)
            ext_ref[0:POOL_HALO, :] = jnp.where(t > 0, halo, 0.0)
            ext_ref[POOL_HALO:POOL_HALO + tm, :] = u
            tpos = t * tm + lax.broadcasted_iota(jnp.int32, (tm, 1), 0)
            pms, invcs = [], []
            for gi, w in enumerate(POOL_WINDOWS):
                cs = slice(gi * POOL_GROUP, (gi + 1) * POOL_GROUP)
                acc = ext_ref[POOL_HALO:POOL_HALO + tm, cs]
                for k in range(1, w):
                    acc = acc + ext_ref[POOL_HALO - k:POOL_HALO - k + tm, cs]
                invc = 1.0 / jnp.minimum(tpos + 1, w).astype(F32)
                pm = (acc * invc - u[:, cs]).astype(BF16)
                pms.append(pm)
                invcs.append(invc)
            mixed = [_dot(pms[gi], R["pool_w"][gi]) for gi in range(4)]
            ps = R["pool_scale"][...]
            y_main = [mixed[gi] * ps[:, gi * POOL_GROUP:(gi + 1) * POOL_GROUP] for gi in range(4)]
        else:
            y_main = [R["ymain"][:, gi * 256:(gi + 1) * 256].astype(F32) for gi in range(4)]

        probs = []
        for hd in range(MEM_HEADS):
            sl = slice(D_MAIN + hd * MEM_HEAD_DIM, D_MAIN + (hd + 1) * MEM_HEAD_DIM)
            ksl = slice(hd * MEM_HEAD_DIM, (hd + 1) * MEM_HEAD_DIM)
            vsl = slice(D_MEM + hd * MEM_HEAD_DIM, D_MEM + (hd + 1) * MEM_HEAD_DIM)
            s = _dot_nt(h_ref[:, sl], R["mkv"][:, ksl]) * mem_scale
            e = jnp.exp(s - jnp.max(s, axis=1, keepdims=True))
            p = e / jnp.sum(e, axis=1, keepdims=True)
            probs.append(p)
            ymem_ref[:, ksl] = _dot(p.astype(BF16), R["mkv"][:, vsl])

        g_off = D_MIX
        gate_d = []
        for gi in range(4):
            cs = slice(gi * 256, (gi + 1) * 256)
            gm = h_ref[:, g_off + gi * 256:g_off + (gi + 1) * 256].astype(F32)
            sv, sd = _silu_and_grad(gm)
            yc_ref[:, cs] = (y_main[gi] * sv).astype(BF16)
            gate_d.append((sv, sd))
        gq = h_ref[:, g_off + D_MAIN:D_IN].astype(F32)
        svq, sdq = _silu_and_grad(gq)
        yc_ref[:, D_MAIN:D_MIX] = (ymem_ref[...] * svq).astype(BF16)

        if pool and bwd:
            zt = R["z"][...]
        else:
            o = _dot(yc_ref[...], R["w_out"][...])
            zt = ALPHA * R["xres"][...] + o
        mu = jnp.mean(zt, axis=1, keepdims=True)
        zc = zt - mu
        var = jnp.mean(zc * zc, axis=1, keepdims=True)
        rstd = lax.rsqrt(var + LN_EPS)
        xhat = zc * rstd
        if not bwd:
            O["z"][...] = zt
            O["xout"][...] = xhat * gamma + R["ln_b"][...]
            return

        if loss_head:
            xo = xhat * gamma + R["ln_b"][...]
            err = xo - R["target"][...]
            O["loss"][...] += jnp.sum(err * err, axis=0, keepdims=True)
            dyt = err * (1.0 / D_MODEL)
        else:
            dyt = R["dy"][...]

        O["dln_g"][...] += jnp.sum(dyt * xhat, axis=0, keepdims=True)
        O["dln_b"][...] += jnp.sum(dyt, axis=0, keepdims=True)
        gdy = dyt * gamma
        m1 = jnp.mean(gdy, axis=1, keepdims=True)
        m2 = jnp.mean(gdy * xhat, axis=1, keepdims=True)
        dz = rstd * (gdy - m1 - xhat * m2)
        O["dz"][...] = dz
        dzb = dz.astype(BF16)

        for n0 in range(0, D_MIX, 512):
            O["dw_out"][n0:n0 + 512, :] += _dot_tn(yc_ref[:, n0:n0 + 512], dzb)
        dyc_mem = _dot_nt(dzb, R["w_out"][D_MAIN:D_MIX, :])

        O["drest"][:, D_MEM + D_MAIN:D_MEM + D_MAIN + D_MEM] = (dyc_mem * ymem_ref[...] * sdq).astype(BF16)
        dymem = dyc_mem * svq
        for hd in range(MEM_HEADS):
            sl = slice(D_MAIN + hd * MEM_HEAD_DIM, D_MAIN + (hd + 1) * MEM_HEAD_DIM)
            ksl = slice(hd * MEM_HEAD_DIM, (hd + 1) * MEM_HEAD_DIM)
            vsl = slice(D_MEM + hd * MEM_HEAD_DIM, D_MEM + (hd + 1) * MEM_HEAD_DIM)
            p = probs[hd]
            dyb = dymem[:, ksl].astype(BF16)
            dp = _dot_nt(dyb, R["mkv"][:, vsl])
            ds = p * (dp - jnp.sum(dp * p, axis=1, keepdims=True)) * mem_scale
            dsb = ds.astype(BF16)
            O["drest"][:, ksl] = _dot(dsb, R["mkv"][:, ksl]).astype(BF16)
            O["dmkv"][:, ksl] += _dot_tn(dsb, h_ref[:, sl])
            O["dmkv"][:, vsl] += _dot_tn(p.astype(BF16), dyb)

        dmain = []
        for gi in range(4):
            cs = slice(gi * 256, (gi + 1) * 256)
            dyc_g = _dot_nt(dzb, R["w_out"][cs, :])
            sv, sd = gate_d[gi]
            O["drest"][:, D_MEM + gi * 256:D_MEM + (gi + 1) * 256] = (dyc_g * y_main[gi] * sd).astype(BF16)
            dmain.append(dyc_g * sv)

        if not pool:
            prod = []
            for gi in range(4):
                cs = slice(gi * 256, (gi + 1) * 256)
                db16 = dmain[gi].astype(BF16)
                O["dmain"][:, cs] = db16
                prod.append(db16.astype(F32) * R["ymain"][:, cs].astype(F32))
            dcol = jnp.zeros((tm, LANES), F32)
            for gi in range(4):
                dr = lax.broadcasted_iota(jnp.int32, (256, LANES), 0)
                hc = lax.broadcasted_iota(jnp.int32, (256, LANES), 1)
                sel = jnp.where(jnp.right_shift(dr, 6) + gi * 4 == hc, 1.0, 0.0).astype(BF16)
                hi, mid, lo = _split3(prod[gi])
                dcol = dcol + ((_dot(hi, sel) + _dot(mid, sel)) + _dot(lo, sel))
            O["drow"][...] = dcol.T[0:FOX_HEADS, :]
            return

        ps = R["pool_scale"][...]
        dpm_list = []
        for gi in range(4):
            cs = slice(gi * 256, (gi + 1) * 256)
            O["dpool_scale"][:, cs] += jnp.sum(dmain[gi] * mixed[gi], axis=0, keepdims=True)
            dmix = (dmain[gi] * ps[:, cs]).astype(BF16)
            O["dpool_w"][gi] += _dot_tn(pms[gi], dmix)
            dpm = _dot_nt(dmix, R["pool_w"][gi])
            dpm_list.append(dpm)
            ext_ref[0:tm, cs] = dpm * invcs[gi]
        ext_ref[tm:tm + POOL_HALO, :] = carry_ref[...]
        carry_ref[...] = ext_ref[0:POOL_HALO, :]
        for gi, w in enumerate(POOL_WINDOWS):
            cs = slice(gi * 256, (gi + 1) * 256)
            acc = ext_ref[0:tm, cs]
            for k in range(1, w):
                acc = acc + ext_ref[k:k + tm, cs]
            O["dmain"][:, cs] = (acc - dpm_list[gi]).astype(BF16)

    outs = pl.pallas_call(
        body, name=f"mix_{kind}_{mode}", grid=(n,),
        in_specs=specs, out_specs=ospecs, out_shape=oshapes,
        scratch_shapes=scratch, compiler_params=_params(),
    )(*arrays)
    return dict(zip(onames, outs))


def _lin_bwd(xin, dhs, ws, res, *, tm, name):
    S, K = xin.shape
    nj = len(dhs)
    nr = len(res)
    widths = [w.shape[1] for w in ws]
    scales = [s for _, s in res]

    def body(*refs):
        x_ref = refs[0]
        dh_refs = refs[1:1 + nj]
        w_refs = refs[1 + nj:1 + 2 * nj]
        r_refs = refs[1 + 2 * nj:1 + 2 * nj + nr]
        dx_ref = refs[1 + 2 * nj + nr]
        dw_refs = refs[2 + 2 * nj + nr:]
        i = pl.program_id(0)

        @pl.when(i == 0)
        def _():
            for dw in dw_refs:
                dw[...] = jnp.zeros_like(dw)

        xb = x_ref[...].astype(BF16)
        dx = jnp.zeros((tm, K), F32)
        for r_ref, sc in zip(r_refs, scales):
            dx = dx + sc * r_ref[...]
        for j in range(nj):
            N = widths[j]
            nc = 512 if N % 512 == 0 else N
            for n0 in range(0, N, nc):
                dhb = dh_refs[j][:, n0:n0 + nc].astype(BF16)
                dx = dx + _dot_nt(dhb, w_refs[j][:, n0:n0 + nc])
                dw_refs[j][:, n0:n0 + nc] += _dot_tn(xb, dhb)
        dx_ref[...] = dx

    in_specs = [pl.BlockSpec((tm, K), lambda i: (i, 0))]
    in_specs += [pl.BlockSpec((tm, N), lambda i: (i, 0)) for N in widths]
    in_specs += [_const_spec((K, N)) for N in widths]
    in_specs += [pl.BlockSpec((tm, K), lambda i: (i, 0)) for _ in res]
    out_specs = [pl.BlockSpec((tm, K), lambda i: (i, 0))]
    out_specs += [pl.BlockSpec((K, N), lambda i: (0, 0)) for N in widths]
    out_shape = [jax.ShapeDtypeStruct((S, K), F32)]
    out_shape += [jax.ShapeDtypeStruct((K, N), F32) for N in widths]
    outs = pl.pallas_call(
        body, name=name, grid=(S // tm,),
        in_specs=in_specs, out_specs=out_specs, out_shape=out_shape,
        compiler_params=_params(),
    )(xin, *dhs, *ws, *[r for r, _ in res])
    return outs[0], list(outs[1:])


def _wgrad(xin, dh, *, name):
    M, K = xin.shape
    N = dh.shape[1]

    def body(x_ref, dh_ref, o_ref):
        o_ref[...] = _dot_tn(x_ref[...].astype(BF16), dh_ref[...].astype(BF16))

    return pl.pallas_call(
        body, name=name, out_shape=jax.ShapeDtypeStruct((K, N), F32),
        compiler_params=pltpu.CompilerParams(vmem_limit_bytes=VMEM_LIMIT),
    )(xin, dh)


def _rows_layout(a16, nt, tt):
    S = a16.shape[1]
    a = a16.reshape(8, 2, nt, tt).transpose(0, 2, 1, 3)
    return jnp.pad(a, ((0, 0), (0, 0), (0, 6), (0, 0)))


def _fox_fwd(h1, k, v, cum, ck_rows, *, tq):
    S = k.shape[0]
    nq = S // tq
    tk = tq

    def body(q_ref, k_ref, v_ref, cum_ref, ckr_ref, o_ref, lse_ref):
        hp = pl.program_id(0)
        qi = pl.program_id(1)
        lane = lax.broadcasted_iota(jnp.int32, (tq, LANES), 1)
        half0 = lane < FOX_HEAD_DIM
        q = q_ref[...] * jnp.asarray(FOX_SCALE, BF16)
        zero = jnp.zeros_like(q)
        cumc = cum_ref[...]
        rr = lax.broadcasted_iota(jnp.int32, (tq, tk), 0)
        cc = lax.broadcasted_iota(jnp.int32, (tq, tk), 1)
        results = []
        for hh in range(2):
            qm = jnp.where(half0, q, zero) if hh == 0 else jnp.where(half0, zero, q)
            cq = jnp.sum(jnp.where(lane == 2 * hp + hh, cumc, 0.0), axis=1, keepdims=True)

            def chunk(ki, carry, masked):
                m, l, acc = carry
                k0 = pl.multiple_of(ki * tk, tk)
                kc = k_ref[pl.ds(k0, tk), :]
                vc = v_ref[pl.ds(k0, tk), :]
                ck = ckr_ref[0, ki][hh:hh + 1, :]
                s = _dot_nt(qm, kc) + (cq - ck)
                if masked:
                    s = jnp.where(cc <= rr, s, -jnp.inf)
                m_new = jnp.maximum(m, jnp.max(s, axis=1, keepdims=True))
                p = jnp.exp(s - m_new)
                alpha = jnp.exp(m - m_new)
                l = alpha * l + jnp.sum(p, axis=1, keepdims=True)
                acc = alpha * acc + _dot(p.astype(BF16), vc)
                return m_new, l, acc

            init = (jnp.full((tq, 1), -jnp.inf, F32), jnp.zeros((tq, 1), F32),
                    jnp.zeros((tq, LANES), F32))
            carry = lax.fori_loop(0, qi, lambda ki, c: chunk(ki, c, False), init)
            m, l, acc = chunk(qi, carry, True)
            results.append((acc / l, m + jnp.log(l)))
        o_ref[...] = jnp.where(half0, results[0][0], results[1][0])
        lse_cols = jnp.where(lane == 0, results[0][1], 0.0) + jnp.where(lane == 1, results[1][1], 0.0)
        lse_ref[0, 0] = lse_cols.T[0:8, :]

    return pl.pallas_call(
        body, name="fox_fwd", grid=(8, nq),
        in_specs=[pl.BlockSpec((tq, LANES), lambda hp, qi: (qi, hp)),
                  pl.BlockSpec((S, LANES), lambda hp, qi: (0, hp)),
                  pl.BlockSpec((S, LANES), lambda hp, qi: (0, hp)),
                  pl.BlockSpec((tq, LANES), lambda hp, qi: (qi, 0)),
                  pl.BlockSpec((1, nq, 8, tk), lambda hp, qi: (hp, 0, 0, 0))],
        out_specs=[pl.BlockSpec((tq, LANES), lambda hp, qi: (qi, hp)),
                   pl.BlockSpec((1, 1, 8, tq), lambda hp, qi: (hp, qi, 0, 0))],
        out_shape=[jax.ShapeDtypeStruct((S, D_MAIN), F32),
                   jax.ShapeDtypeStruct((8, nq, 8, tq), F32)],
        compiler_params=_params(2),
    )(h1, k, v, cum, ck_rows)


def _fox_bwd(h1, k, v, do, cum, cq_rows, lse_rows, d_rows, *, tq):
    S = k.shape[0]
    nq = S // tq
    tk = tq

    def body(k_ref, v_ref, q_ref, do_ref, cum_ref, cqr_ref, lser_ref, dr_ref,
             dq_ref, dk_ref, dv_ref, dck_ref, dcq_ref):
        hp = pl.program_id(0)
        kj = pl.program_id(1)

        @pl.when(kj == 0)
        def _():
            dq_ref[...] = jnp.zeros_like(dq_ref)
            dcq_ref[...] = jnp.zeros_like(dcq_ref)

        lane = lax.broadcasted_iota(jnp.int32, (tk, LANES), 1)
        half0 = lane < FOX_HEAD_DIM
        kt = k_ref[...]
        vt = v_ref[...]
        zero = jnp.zeros_like(kt)
        cumc = cum_ref[...]
        rr = lax.broadcasted_iota(jnp.int32, (tk, tq), 0)
        cc = lax.broadcasted_iota(jnp.int32, (tk, tq), 1)
        dk_tot = jnp.zeros((tk, LANES), F32)
        dv_tot = jnp.zeros((tk, LANES), F32)
        dck_cols = jnp.zeros((tk, LANES), F32)
        for hh in range(2):
            mine = half0 if hh == 0 else jnp.logical_not(half0)
            km = jnp.where(mine, kt, zero)
            vm = jnp.where(mine, vt, zero)
            ck = jnp.sum(jnp.where(lane == 2 * hp + hh, cumc, 0.0), axis=1, keepdims=True)

            def chunk(qi, carry, masked):
                dk_acc, dv_acc, dck_acc = carry
                q0 = pl.multiple_of(qi * tq, tq)
                qc = q_ref[pl.ds(q0, tq), :]
                doc = do_ref[pl.ds(q0, tq), :]
                cq = cqr_ref[0, qi][hh:hh + 1, :]
                lse = lser_ref[0, qi][hh:hh + 1, :]
                dd = dr_ref[0, qi][hh:hh + 1, :]
                st = _dot_nt(km, qc) * FOX_SCALE + (cq - ck)
                pt = jnp.exp(st - lse)
                if masked:
                    pt = jnp.where(rr <= cc, pt, 0.0)
                dpt = _dot_nt(vm, doc)
                dst = pt * (dpt - dd)
                dsb = dst.astype(BF16)
                dv_acc = dv_acc + _dot(pt.astype(BF16), doc)
                dk_acc = dk_acc + _dot(dsb, qc)
                dq_ref[pl.ds(q0, tq), :] += _dot_tn(dsb, km) * FOX_SCALE
                dck_acc = dck_acc + jnp.sum(dst, axis=1, keepdims=True)
                dcq_ref[0, qi, hh:hh + 1, :] += jnp.sum(dst, axis=0, keepdims=True)
                return dk_acc, dv_acc, dck_acc

            init = (jnp.zeros((tk, LANES), F32), jnp.zeros((tk, LANES), F32), jnp.zeros((tk, 1), F32))
            carry = chunk(kj, init, True)
            dk_acc, dv_acc, dck_acc = lax.fori_loop(kj + 1, nq, lambda qi, c: chunk(qi, c, False), carry)
            dk_tot = dk_tot + jnp.where(mine, dk_acc * FOX_SCALE, 0.0)
            dv_tot = dv_tot + jnp.where(mine, dv_acc, 0.0)
            dck_cols = dck_cols + jnp.where(lane == hh, dck_acc, 0.0)
        dk_ref[...] = dk_tot.astype(BF16)
        dv_ref[...] = dv_tot.astype(BF16)
        dck_ref[0, 0] = dck_cols.T[0:8, :]

    return pl.pallas_call(
        body, name="fox_bwd", grid=(8, nq),
        in_specs=[pl.BlockSpec((tk, LANES), lambda hp, kj: (kj, hp)),
                  pl.BlockSpec((tk, LANES), lambda hp, kj: (kj, hp)),
                  pl.BlockSpec((S, LANES), lambda hp, kj: (0, hp)),
                  pl.BlockSpec((S, LANES), lambda hp, kj: (0, hp)),
                  pl.BlockSpec((tk, LANES), lambda hp, kj: (kj, 0)),
                  pl.BlockSpec((1, nq, 8, tq), lambda hp, kj: (hp, 0, 0, 0)),
                  pl.BlockSpec((1, nq, 8, tq), lambda hp, kj: (hp, 0, 0, 0)),
                  pl.BlockSpec((1, nq, 8, tq), lambda hp, kj: (hp, 0, 0, 0))],
        out_specs=[pl.BlockSpec((S, LANES), lambda hp, kj: (0, hp)),
                   pl.BlockSpec((tk, LANES), lambda hp, kj: (kj, hp)),
                   pl.BlockSpec((tk, LANES), lambda hp, kj: (kj, hp)),
                   pl.BlockSpec((1, 1, 8, tk), lambda hp, kj: (hp, kj, 0, 0)),
                   pl.BlockSpec((1, nq, 8, tq), lambda hp, kj: (hp, 0, 0, 0))],
        out_shape=[jax.ShapeDtypeStruct((S, D_MAIN), F32),
                   jax.ShapeDtypeStruct((S, D_MAIN), BF16),
                   jax.ShapeDtypeStruct((S, D_MAIN), BF16),
                   jax.ShapeDtypeStruct((8, nq, 8, tk), F32),
                   jax.ShapeDtypeStruct((8, nq, 8, tq), F32)],
        compiler_params=_params(2),
    )(k, v, h1, do, cum, cq_rows, lse_rows, d_rows)


def _adamw(w, g, m, v, *, name):
    Rr, C = w.shape
    tr = 256 if Rr % 256 == 0 else Rr
    c1 = 1.0 / (1.0 - ADAM_B1 ** ADAM_STEP)
    c2 = 1.0 / (1.0 - ADAM_B2 ** ADAM_STEP)

    def body(w_ref, g_ref, m_ref, v_ref, d_ref, nm_ref, nv_ref):
        gv = g_ref[...]
        nm = ADAM_B1 * m_ref[...] + (1.0 - ADAM_B1) * gv
        nv = ADAM_B2 * v_ref[...] + (1.0 - ADAM_B2) * (gv * gv)
        d_ref[...] = -ADAM_LR * ((nm * c1) / (jnp.sqrt(nv * c2) + ADAM_EPS) + ADAM_WD * w_ref[...])
        nm_ref[...] = nm
        nv_ref[...] = nv

    spec = pl.BlockSpec((tr, C), lambda i: (i, 0))
    sds = jax.ShapeDtypeStruct((Rr, C), F32)
    return pl.pallas_call(
        body, name=name, grid=(Rr // tr,),
        in_specs=[spec] * 4, out_specs=[spec] * 3, out_shape=[sds] * 3,
        compiler_params=_params(),
    )(w, g, m, v)


_ANY = pl.BlockSpec(memory_space=pl.ANY)
_MESH = pl.DeviceIdType.MESH


def _place():
    x, y, c = lax.axis_index("x"), lax.axis_index("y"), lax.axis_index("c")
    return x, y, c


def _all_gather_shards(pack):
    dt = pack.dtype

    def body(p_ref, out_ref, send_sems, recv_sems, local_sem):
        x, y, c = _place()
        sib = (x, y, 1 - c)
        chips = [(1 - x, y), (x, 1 - y), (1 - x, 1 - y)]
        me = 2 * x + y

        def copy(k, chip_idx, half, to, src=None):
            dst = out_ref.at[chip_idx, half]
            return pltpu.make_async_remote_copy(
                src_ref=dst if src is None else src, dst_ref=dst,
                send_sem=send_sems.at[k], recv_sem=recv_sems.at[k],
                device_id=to, device_id_type=_MESH)

        mine = pltpu.make_async_copy(p_ref, out_ref.at[me], local_sem)
        mine.start()
        first = [copy(j, me, c, (*chip, c), src=p_ref.at[c]) for j, chip in enumerate(chips)]
        for cp in first:
            cp.start()
        passed = [copy(3 + j, 2 * chip[0] + chip[1], c, sib) for j, chip in enumerate(chips)]
        for j, chip in enumerate(chips):
            copy(j, 2 * chip[0] + chip[1], c, sib).wait_recv()
            passed[j].start()
        for j, chip in enumerate(chips):
            copy(3 + j, 2 * chip[0] + chip[1], 1 - c, sib).wait_recv()
        for cp in first + passed:
            cp.wait_send()
        mine.wait()

    return pl.pallas_call(
        body, name="all_gather_shards",
        in_specs=[_ANY], out_specs=_ANY,
        out_shape=jax.ShapeDtypeStruct((N_CHIPS, 2, HALF_ROWS, 1024), dt),
        scratch_shapes=[pltpu.SemaphoreType.DMA((6,)), pltpu.SemaphoreType.DMA((6,)),
                        pltpu.SemaphoreType.DMA],
    )(pack)


def _send_half_to_sibling(gpack):
    def body(g_ref, out_ref, send_sem, recv_sem):
        x, y, c = _place()
        sib = (x, y, 1 - c)
        cps = [pltpu.make_async_remote_copy(
            src_ref=g_ref.at[j, 1 - c], dst_ref=out_ref.at[j],
            send_sem=send_sem.at[j], recv_sem=recv_sem.at[j],
            device_id=sib, device_id_type=_MESH) for j in range(N_CHIPS)]
        for cp in cps:
            cp.start()
        for cp in cps:
            cp.wait_recv()
        for cp in cps:
            cp.wait_send()

    return pl.pallas_call(
        body, name="pair_send",
        in_specs=[_ANY], out_specs=_ANY,
        out_shape=jax.ShapeDtypeStruct((N_CHIPS, HALF_ROWS, 1024), F32),
        scratch_shapes=[pltpu.SemaphoreType.DMA((N_CHIPS,)), pltpu.SemaphoreType.DMA((N_CHIPS,))],
    )(gpack)


def _pair_sum(gpack, recv, c_arr, *, tr):
    def body(c_ref, a_ref, b_ref, o_ref):
        o_ref[...] = (a_ref[...] + b_ref[...]).astype(BF16)

    grid_spec = pltpu.PrefetchScalarGridSpec(
        num_scalar_prefetch=1, grid=(N_CHIPS, HALF_ROWS // tr),
        in_specs=[pl.BlockSpec((None, None, tr, 1024), lambda j, i, c_ref: (j, c_ref[0], i, 0)),
                  pl.BlockSpec((None, tr, 1024), lambda j, i, c_ref: (j, i, 0))],
        out_specs=pl.BlockSpec((None, tr, 1024), lambda j, i, c_ref: (j, i, 0)))
    return pl.pallas_call(
        body, name="pair_sum", grid_spec=grid_spec,
        out_shape=jax.ShapeDtypeStruct((N_CHIPS, HALF_ROWS, 1024), BF16),
        compiler_params=_params(2),
    )(c_arr, gpack, recv)


def _scatter_pieces(psum):
    def body(p_ref, out_ref, send_sems, recv_sems, local_sem):
        x, y, c = _place()
        chips = [(1 - x, y), (x, 1 - y), (1 - x, 1 - y)]
        me = 2 * x + y
        mine = pltpu.make_async_copy(p_ref.at[me], out_ref.at[me], local_sem)
        mine.start()
        cps = []
        for j, chip in enumerate(chips):
            them = 2 * chip[0] + chip[1]
            cps.append(pltpu.make_async_remote_copy(
                src_ref=p_ref.at[them], dst_ref=out_ref.at[me],
                send_sem=send_sems.at[j], recv_sem=recv_sems.at[j],
                device_id=(*chip, c), device_id_type=_MESH))
        for cp in cps:
            cp.start()
        for cp in cps:
            cp.wait_recv()
        for cp in cps:
            cp.wait_send()
        mine.wait()

    return pl.pallas_call(
        body, name="scatter_pieces",
        in_specs=[_ANY], out_specs=_ANY,
        out_shape=jax.ShapeDtypeStruct((N_CHIPS, HALF_ROWS, 1024), BF16),
        scratch_shapes=[pltpu.SemaphoreType.DMA((3,)), pltpu.SemaphoreType.DMA((3,)),
                        pltpu.SemaphoreType.DMA],
    )(psum)


def _sum_pieces(pieces, *, tr):
    def body(p_ref, o_ref):
        acc = p_ref[0].astype(F32) + p_ref[1].astype(F32)
        acc = acc + p_ref[2].astype(F32)
        o_ref[...] = acc + p_ref[3].astype(F32)

    return pl.pallas_call(
        body, name="sum_pieces", grid=(HALF_ROWS // tr,),
        in_specs=[pl.BlockSpec((N_CHIPS, tr, 1024), lambda i: (0, i, 0))],
        out_specs=pl.BlockSpec((tr, 1024), lambda i: (i, 0)),
        out_shape=jax.ShapeDtypeStruct((HALF_ROWS, 1024), F32),
        compiler_params=_params(),
    )(pieces)


def _exchange_halves(total):
    def body(t_ref, out_ref, send_sem, recv_sem, local_sem):
        x, y, c = _place()
        sib = (x, y, 1 - c)
        mine = pltpu.make_async_copy(t_ref, out_ref.at[c], local_sem)
        mine.start()
        cp = pltpu.make_async_remote_copy(
            src_ref=t_ref, dst_ref=out_ref.at[c], send_sem=send_sem, recv_sem=recv_sem,
            device_id=sib, device_id_type=_MESH)
        cp.start()
        cp.wait_recv()
        cp.wait_send()
        mine.wait()

    return pl.pallas_call(
        body, name="exchange_halves",
        in_specs=[_ANY], out_specs=_ANY,
        out_shape=jax.ShapeDtypeStruct((2, HALF_ROWS, 1024), F32),
        scratch_shapes=[pltpu.SemaphoreType.DMA, pltpu.SemaphoreType.DMA, pltpu.SemaphoreType.DMA],
    )(total)


def _pad_rows(a, rows):
    return jnp.pad(a, ((0, rows - a.shape[0]), (0, 0)))


def _pack_weight_shard(w_in, w_mem_kv, w_out, pool_w, w_kv_shared, pool_scale):
    ps_bits = lax.bitcast_convert_type(pool_scale.reshape(-1), BF16).reshape(1, -1)
    ps_row = jnp.pad(ps_bits, ((0, 0), (0, 1024 - ps_bits.shape[1])))
    parts = [
        w_in.astype(BF16).reshape(ROWS_W_IN, 1024),
        w_mem_kv.astype(BF16).reshape(ROWS_W_MKV, 1024),
        w_out.astype(BF16).reshape(ROWS_W_OUT, 1024),
        pool_w.astype(BF16).reshape(ROWS_POOL_W, 1024),
        _pad_rows(w_kv_shared.astype(BF16).reshape(KV_SHARD, 1024), ROWS_W_KV),
        _pad_rows(ps_row, ROWS_SMALL),
        jnp.zeros((PACK_ROWS - OFF_LN_G, 1024), BF16),
    ]
    return jnp.concatenate(parts, axis=0).reshape(2, HALF_ROWS, 1024)


def _unpack_weights(g):
    w_in = g[:, OFF_W_IN:OFF_W_IN + ROWS_W_IN].reshape(4, 2, D_MODEL, D_IN // 4)
    w_in = w_in.transpose(1, 2, 0, 3).reshape(2, D_MODEL, D_IN)
    w_mkv = g[:, OFF_W_MKV:OFF_W_MKV + ROWS_W_MKV].reshape(4, 2, D_MODEL // 4, 2 * D_MEM)
    w_mkv = w_mkv.transpose(1, 0, 2, 3).reshape(2, D_MODEL, 2 * D_MEM)
    w_out = g[:, OFF_W_OUT:OFF_W_OUT + ROWS_W_OUT].reshape(4, 2, D_MIX // 4, D_MODEL)
    w_out = w_out.transpose(1, 0, 2, 3).reshape(2, D_MIX, D_MODEL)
    pool_w = g[:, OFF_POOL_W:OFF_POOL_W + ROWS_POOL_W].reshape(4, 4, POOL_GROUP // 4, POOL_GROUP)
    pool_w = pool_w.transpose(1, 0, 2, 3).reshape(4, POOL_GROUP, POOL_GROUP)
    w_kv = g[:, OFF_W_KV:OFF_W_KV + KV_SHARD].reshape(4, D_MODEL, KV_SHARD)
    w_kv = w_kv.transpose(1, 0, 2).reshape(D_MODEL, KV_COLS)
    ps_bits = g[:, OFF_POOL_S, 0:512].reshape(4, 256, 2)
    pool_scale = lax.bitcast_convert_type(ps_bits, F32).reshape(1, D_MAIN)
    return w_in, w_mkv, w_out, pool_w, w_kv, pool_scale


def _pack_grads(g_w_in, g_w_mkv, g_w_out, g_pool_w, g_w_kv, g_pool_scale, g_ln_g, g_ln_b, g_bf):
    def rep(a):
        a = _pad_rows(a, ROWS_SMALL)
        return jnp.broadcast_to(a[None], (4,) + a.shape)

    parts = [
        g_w_in.reshape(2, D_MODEL, 4, D_IN // 4).transpose(2, 0, 1, 3).reshape(4, ROWS_W_IN, 1024),
        g_w_mkv.reshape(2, 4, D_MODEL // 4, 2 * D_MEM).transpose(1, 0, 2, 3).reshape(4, ROWS_W_MKV, 1024),
        g_w_out.reshape(2, 4, D_MIX // 4, D_MODEL).transpose(1, 0, 2, 3).reshape(4, ROWS_W_OUT, 1024),
        g_pool_w.reshape(4, 4, POOL_GROUP // 4, POOL_GROUP).transpose(1, 0, 2, 3).reshape(4, ROWS_POOL_W, 1024),
        jnp.pad(g_w_kv.reshape(D_MODEL, 4, KV_SHARD).transpose(1, 0, 2).reshape(4, KV_SHARD, 1024),
                ((0, 0), (0, ROWS_W_KV - KV_SHARD), (0, 0))),
        jnp.pad(g_pool_scale.reshape(4, 1, 256), ((0, 0), (0, ROWS_SMALL - 1), (0, 1024 - 256))),
        rep(g_ln_g), rep(g_ln_b),
        rep(jnp.pad(g_bf.reshape(1, -1), ((0, 0), (0, 1024 - g_bf.shape[0])))),
        jnp.zeros((4, ROWS_PAD, 1024), F32),
    ]
    return jnp.concatenate(parts, axis=1).reshape(4, 2, HALF_ROWS, 1024)


def _local_step(x, mem, target, w_in, w_mkv, w_out, pool_w, pool_scale, w_kv, ln_g, ln_b, b_forget,
                *, tm=256, tq=512):
    S = x.shape[0]
    nq = S // tq
    g_rows = [ln_g[l:l + 1] for l in range(2)]
    b_rows = [ln_b[l:l + 1] for l in range(2)]
    wk, wv = w_kv[:, 0:D_MAIN], w_kv[:, D_MAIN:2 * D_MAIN]
    wf = jnp.pad(w_kv[:, 2 * D_MAIN:], ((0, 0), (0, LANES - FOX_HEADS)))
    bf_row = jnp.pad(b_forget.reshape(1, -1), ((0, 0), (0, LANES - FOX_HEADS)))

    mkv = [_linear_fwd(mem, w_mkv[l], tm=N_MEM, name=f"mem_kv{l}") for l in range(2)]

    h0 = _linear_fwd(x, w_in[0], tm=tm, name="in_proj0")
    f0 = _mix("pool", "fwd", h=h0, xres=x, mkv=mkv[0], w_out=w_out[0], ln_g=g_rows[0], ln_b=b_rows[0],
              pool_w=pool_w, pool_scale=pool_scale, tm=tm)
    z0, x1 = f0["z"], f0["xout"]
    k, v, fl, cum = _kv_proj(x1, wk, wv, wf, bf_row, tm=tm)
    h1 = _linear_fwd(x1, w_in[1], tm=tm, name="in_proj1")
    cum_rows = _rows_layout(cum[:, 0:FOX_HEADS].T, nq, tq)
    ymain1, lse_rows = _fox_fwd(h1, k, v, cum, cum_rows, tq=tq)

    b1 = _mix("fox", "bwd", h=h1, xres=x1, mkv=mkv[1], w_out=w_out[1], ln_g=g_rows[1], ln_b=b_rows[1],
              ymain=ymain1, target=target, tm=tm)
    d_rows = _rows_layout(b1["drow"], nq, tq)
    dq, dk, dv, dck_rows, dcq_rows = _fox_bwd(h1, k, v, b1["dmain"], cum, cum_rows, lse_rows, d_rows, tq=tq)

    def cols(rows):
        a = rows[:, :, 0:2, :].transpose(0, 2, 1, 3).reshape(FOX_HEADS, S).T
        return jnp.pad(a, ((0, 0), (0, LANES - FOX_HEADS)))

    df, dbf = _gate_bwd(cols(dcq_rows), cols(dck_rows), fl, tm=tm)

    dx1a, (dwu1, dwr1) = _lin_bwd(
        x1, [dq, b1["drest"]], [w_in[1][:, 0:D_MAIN], w_in[1][:, D_MAIN:]], [(b1["dz"], ALPHA)],
        tm=tm, name="in_proj1_bwd")
    dx1, (dwk, dwv, dwf) = _lin_bwd(x1, [dk, dv, df], [wk, wv, wf], [(dx1a, 1.0)], tm=tm, name="kv_proj_bwd")

    b0 = _mix("pool", "bwd", h=h0, mkv=mkv[0], w_out=w_out[0], ln_g=g_rows[0],
              pool_w=pool_w, pool_scale=pool_scale, z=z0, dy=dx1, tm=tm)
    dx, (dwu0, dwr0) = _lin_bwd(
        x, [b0["dmain"], b0["drest"]], [w_in[0][:, 0:D_MAIN], w_in[0][:, D_MAIN:]], [(b0["dz"], ALPHA)],
        tm=tm, name="in_proj0_bwd")
    dw_mkv = [_wgrad(mem, b["dmkv"], name=f"mem_kv{l}_bwd") for l, b in enumerate((b0, b1))]

    grads = dict(
        w_in=jnp.stack([jnp.concatenate([dwu0, dwr0], axis=1), jnp.concatenate([dwu1, dwr1], axis=1)]),
        w_mem_kv=jnp.stack(dw_mkv),
        w_out=jnp.stack([b0["dw_out"], b1["dw_out"]]),
        ln_g=jnp.concatenate([b0["dln_g"], b1["dln_g"]], axis=0),
        ln_b=jnp.concatenate([b0["dln_b"], b1["dln_b"]], axis=0),
        pool_w=b0["dpool_w"],
        pool_scale=b0["dpool_scale"],
        w_kv=jnp.concatenate([dwk, dwv, dwf[:, 0:FOX_HEADS]], axis=1),
        b_forget=dbf[0, 0:FOX_HEADS],
    )
    return b1["loss"], dx, grads


def kernel(x, mem, w_in, w_mem_kv, w_out, ln_g, ln_b, pool_w, pool_scale, w_kv_shared, b_forget, loss_target, m_w_in, m_w_mem_kv, m_w_out, m_ln_g, m_ln_b, m_pool_w, m_pool_scale, m_w_kv_shared, m_b_forget, v_w_in, v_w_mem_kv, v_w_out, v_ln_g, v_ln_b, v_pool_w, v_pool_scale, v_w_kv_shared, v_b_forget):
    c_arr = lax.axis_index("c").astype(jnp.int32).reshape(1)

    wpack = _pack_weight_shard(w_in, w_mem_kv, w_out, pool_w, w_kv_shared, pool_scale)
    gathered = _all_gather_shards(wpack).reshape(N_CHIPS, PACK_ROWS, 1024)
    fw_in, fw_mkv, fw_out, fpool_w, fw_kv, fpool_scale = _unpack_weights(gathered)

    loss_vec, dx, g = _local_step(x[0], mem[0], loss_target[0], fw_in, fw_mkv, fw_out, fpool_w,
                                  fpool_scale, fw_kv, ln_g, ln_b, b_forget)
    loss = lax.psum(0.5 / D_MODEL * jnp.sum(loss_vec), ("x", "y", "c"))

    gpack = _pack_grads(g["w_in"], g["w_mem_kv"], g["w_out"], g["pool_w"], g["w_kv"], g["pool_scale"],
                        g["ln_g"], g["ln_b"], g["b_forget"])
    from_sibling = _send_half_to_sibling(gpack)
    psum = _pair_sum(gpack, from_sibling, c_arr, tr=PACK_TILE)
    pieces = _scatter_pieces(psum)
    total = _sum_pieces(pieces, tr=PACK_TILE)
    shard = _exchange_halves(total).reshape(PACK_ROWS, 1024)

    g_w_in = shard[OFF_W_IN:OFF_W_IN + ROWS_W_IN].reshape(w_in.shape)
    g_w_mkv = shard[OFF_W_MKV:OFF_W_MKV + ROWS_W_MKV].reshape(w_mem_kv.shape)
    g_w_out = shard[OFF_W_OUT:OFF_W_OUT + ROWS_W_OUT].reshape(w_out.shape)
    g_pool_w = shard[OFF_POOL_W:OFF_POOL_W + ROWS_POOL_W].reshape(pool_w.shape)
    g_w_kv = shard[OFF_W_KV:OFF_W_KV + KV_SHARD].reshape(w_kv_shared.shape)
    g_pool_scale = shard[OFF_POOL_S:OFF_POOL_S + 1, 0:256].reshape(pool_scale.shape)
    g_ln_g = shard[OFF_LN_G:OFF_LN_G + 2]
    g_ln_b = shard[OFF_LN_B:OFF_LN_B + 2]
    g_bf = shard[OFF_BF, 0:FOX_HEADS]

    names = ["w_in", "w_mem_kv", "w_out", "ln_g", "ln_b", "pool_w", "pool_scale", "w_kv_shared", "b_forget"]
    ws = [w_in, w_mem_kv, w_out, ln_g, ln_b, pool_w, pool_scale, w_kv_shared, b_forget]
    gs = [g_w_in, g_w_mkv, g_w_out, g_ln_g, g_ln_b, g_pool_w, g_pool_scale, g_w_kv, g_bf]
    ms = [m_w_in, m_w_mem_kv, m_w_out, m_ln_g, m_ln_b, m_pool_w, m_pool_scale, m_w_kv_shared, m_b_forget]
    vs = [v_w_in, v_w_mem_kv, v_w_out, v_ln_g, v_ln_b, v_pool_w, v_pool_scale, v_w_kv_shared, v_b_forget]
    deltas, new_ms, new_vs = [], [], []
    for nm, w, gg, mm, vv in zip(names, ws, gs, ms, vs):
        two_d = (-1, w.shape[-1])
        d, nmm, nvv = _adamw(w.reshape(two_d), gg.reshape(two_d), mm.reshape(two_d), vv.reshape(two_d),
                             name=f"adamw_{nm}")
        deltas.append(d.reshape(w.shape))
        new_ms.append(nmm.reshape(w.shape))
        new_vs.append(nvv.reshape(w.shape))

    return (loss, dx[None], *gs, *deltas, *new_ms, *new_vs)
```

```python
import functools

import jax
import jax.numpy as jnp
from jax import lax
from jax.experimental import pallas as pl
from jax.experimental.pallas import tpu as pltpu

F32 = jnp.float32
BF16 = jnp.bfloat16

D_MODEL = 1024
D_MAIN = 1024
D_MEM = 512
D_MIX = D_MAIN + D_MEM
D_IN = 2 * D_MIX
N_MEM = 256
MEM_HEADS = 4
MEM_HEAD_DIM = 128
FOX_HEADS = 16
FOX_HEAD_DIM = 64
FOX_SCALE = 0.125
POOL_WINDOWS = (2, 4, 8, 16)
POOL_GROUP = 256
POOL_HALO = 16
ALPHA = 4.0 ** 0.25
LN_EPS = 1e-5
LANES = 128
N_CHIPS = 4

ADAM_LR = 0.001
ADAM_B1 = 0.9
ADAM_B2 = 0.999
ADAM_EPS = 1e-08
ADAM_WD = 0.01
ADAM_STEP = 10

VMEM_LIMIT = 56 * 1024 * 1024

ROWS_W_IN = 2 * D_MODEL * (D_IN // N_CHIPS) // 1024
ROWS_W_MKV = 2 * (D_MODEL // N_CHIPS) * 2 * D_MEM // 1024
ROWS_W_OUT = 2 * (D_MIX // N_CHIPS) * D_MODEL // 1024
ROWS_POOL_W = 4 * (POOL_GROUP // N_CHIPS) * POOL_GROUP // 1024
KV_COLS = 2 * D_MAIN + FOX_HEADS
KV_SHARD = KV_COLS // N_CHIPS
ROWS_W_KV = 528
ROWS_SMALL = 16
OFF_W_IN = 0
OFF_W_MKV = OFF_W_IN + ROWS_W_IN
OFF_W_OUT = OFF_W_MKV + ROWS_W_MKV
OFF_POOL_W = OFF_W_OUT + ROWS_W_OUT
OFF_W_KV = OFF_POOL_W + ROWS_POOL_W
OFF_POOL_S = OFF_W_KV + ROWS_W_KV
OFF_LN_G = OFF_POOL_S + ROWS_SMALL
OFF_LN_B = OFF_LN_G + ROWS_SMALL
OFF_BF = OFF_LN_B + ROWS_SMALL
PACK_ROWS = 3584
HALF_ROWS = PACK_ROWS // 2
PACK_TILE = 256
ROWS_PAD = PACK_ROWS - (OFF_BF + ROWS_SMALL)


def _dot(a, b):
    return jnp.dot(a, b, preferred_element_type=F32)


def _dot_nt(a, b):
    return lax.dot_general(a, b, (((1,), (1,)), ((), ())), preferred_element_type=F32)


def _dot_tn(a, b):
    return lax.dot_general(a, b, (((0,), (0,)), ((), ())), preferred_element_type=F32)


def _params(n_axes=1):
    return pltpu.CompilerParams(dimension_semantics=("arbitrary",) * n_axes,
                                vmem_limit_bytes=VMEM_LIMIT)


def _const_spec(shape):
    zeros = (0,) * len(shape)
    return pl.BlockSpec(shape, lambda *_: zeros, pipeline_mode=pl.Buffered(1))


def _split3(x):
    hi = x.astype(BF16)
    r = x - hi.astype(F32)
    mid = r.astype(BF16)
    lo = (r - mid.astype(F32)).astype(BF16)
    return hi, mid, lo


def _linear_fwd(x, w, *, tm, name):
    S, K = x.shape
    N = w.shape[1]
    nc = 512 if N % 512 == 0 else N

    def body(x_ref, w_ref, o_ref):
        xb = x_ref[...].astype(BF16)
        for n0 in range(0, N, nc):
            o_ref[:, n0:n0 + nc] = _dot(xb, w_ref[:, n0:n0 + nc]).astype(BF16)

    return pl.pallas_call(
        body, name=name, grid=(S // tm,),
        in_specs=[pl.BlockSpec((tm, K), lambda i: (i, 0)), _const_spec((K, N))],
        out_specs=pl.BlockSpec((tm, N), lambda i: (i, 0)),
        out_shape=jax.ShapeDtypeStruct((S, N), BF16),
        compiler_params=_params(),
    )(x, w)


def _kv_proj(x1, wk, wv, wf, bf_row, *, tm):
    S = x1.shape[0]

    def body(x_ref, wk_ref, wv_ref, wf_ref, b_ref, k_ref, v_ref, fl_ref, cum_ref, carry_ref):
        i = pl.program_id(0)

        @pl.when(i == 0)
        def _():
            carry_ref[...] = jnp.zeros_like(carry_ref)

        xb = x_ref[...].astype(BF16)
        for n0 in range(0, D_MAIN, 512):
            k_ref[:, n0:n0 + 512] = _dot(xb, wk_ref[:, n0:n0 + 512]).astype(BF16)
            v_ref[:, n0:n0 + 512] = _dot(xb, wv_ref[:, n0:n0 + 512]).astype(BF16)
        fl = _dot(xb, wf_ref[...]) + b_ref[...]
        fl_ref[...] = fl
        log_f = jnp.minimum(fl, 0.0) - jnp.log1p(jnp.exp(-jnp.abs(fl)))
        r = lax.broadcasted_iota(jnp.int32, (tm, tm), 0)
        c = lax.broadcasted_iota(jnp.int32, (tm, tm), 1)
        tri = jnp.where(c <= r, 1.0, 0.0).astype(BF16)
        hi, mid, lo = _split3(log_f)
        cum = (_dot(tri, hi) + _dot(tri, mid)) + _dot(tri, lo) + carry_ref[0:1, :]
        cum_ref[...] = cum
        carry_ref[0:1, :] = cum[tm - 1:tm, :]

    return pl.pallas_call(
        body, name="kv_proj", grid=(S // tm,),
        in_specs=[pl.BlockSpec((tm, D_MODEL), lambda i: (i, 0)),
                  _const_spec((D_MODEL, D_MAIN)), _const_spec((D_MODEL, D_MAIN)),
                  _const_spec((D_MODEL, LANES)), _const_spec((1, LANES))],
        out_specs=[pl.BlockSpec((tm, D_MAIN), lambda i: (i, 0)),
                   pl.BlockSpec((tm, D_MAIN), lambda i: (i, 0)),
                   pl.BlockSpec((tm, LANES), lambda i: (i, 0)),
                   pl.BlockSpec((tm, LANES), lambda i: (i, 0))],
        out_shape=[jax.ShapeDtypeStruct((S, D_MAIN), BF16), jax.ShapeDtypeStruct((S, D_MAIN), BF16),
                   jax.ShapeDtypeStruct((S, LANES), F32), jax.ShapeDtypeStruct((S, LANES), F32)],
        scratch_shapes=[pltpu.VMEM((8, LANES), F32)],
        compiler_params=_params(),
    )(x1, wk, wv, wf, bf_row)


def _gate_bwd(dcq, dck, fl, *, tm):
    S = fl.shape[0]
    n = S // tm

    def body(dcq_ref, dck_ref, fl_ref, df_ref, db_ref, carry_ref):
        i = pl.program_id(0)

        @pl.when(i == 0)
        def _():
            carry_ref[...] = jnp.zeros_like(carry_ref)
            db_ref[...] = jnp.zeros_like(db_ref)

        dcum = dcq_ref[...] - dck_ref[...]
        r = lax.broadcasted_iota(jnp.int32, (tm, tm), 0)
        c = lax.broadcasted_iota(jnp.int32, (tm, tm), 1)
        tri = jnp.where(c >= r, 1.0, 0.0).astype(BF16)
        hi, mid, lo = _split3(dcum)
        rev = (_dot(tri, hi) + _dot(tri, mid)) + _dot(tri, lo) + carry_ref[0:1, :]
        carry_ref[0:1, :] = rev[0:1, :]
        fl_v = fl_ref[...]
        df = rev * (1.0 / (1.0 + jnp.exp(fl_v)))
        df_ref[...] = df
        db_ref[...] += jnp.sum(df, axis=0, keepdims=True)

    return pl.pallas_call(
        body, name="gate_bwd", grid=(n,),
        in_specs=[pl.BlockSpec((tm, LANES), lambda i: (n - 1 - i, 0)),
                  pl.BlockSpec((tm, LANES), lambda i: (n - 1 - i, 0)),
                  pl.BlockSpec((tm, LANES), lambda i: (n - 1 - i, 0))],
        out_specs=[pl.BlockSpec((tm, LANES), lambda i: (n - 1 - i, 0)),
                   pl.BlockSpec((1, LANES), lambda i: (0, 0))],
        out_shape=[jax.ShapeDtypeStruct((S, LANES), F32), jax.ShapeDtypeStruct((1, LANES), F32)],
        scratch_shapes=[pltpu.VMEM((8, LANES), F32)],
        compiler_params=_params(),
    )(dcq, dck, fl)


def _silu_and_grad(g):
    sg = 1.0 / (1.0 + jnp.exp(-g))
    return g * sg, sg * (1.0 + g * (1.0 - sg))


def _mix(kind, mode, *, h, xres=None, mkv, w_out, ln_g, ln_b=None, pool_w=None, pool_scale=None,
         ymain=None, target=None, z=None, dy=None, tm):
    S = h.shape[0]
    n = S // tm
    pool = kind == "pool"
    bwd = mode == "bwd"
    loss_head = bwd and not pool
    rev = pool and bwd
    mem_scale = MEM_HEAD_DIM ** -0.5

    def t_of(i):
        return (n - 1 - i) if rev else i

    row = lambda i: (t_of(i), 0)
    names, arrays, specs = [], [], []

    def add(name, arr, spec):
        names.append(name)
        arrays.append(arr)
        specs.append(spec)

    add("h", h, pl.BlockSpec((tm, D_IN), row))
    if pool:
        hb = tm // POOL_HALO
        add("halo", h, pl.BlockSpec((POOL_HALO, D_MAIN), lambda i: (jnp.maximum(t_of(i) * hb - 1, 0), 0)))
        add("pool_w", pool_w, _const_spec((4, POOL_GROUP, POOL_GROUP)))
        add("pool_scale", pool_scale, _const_spec((1, D_MAIN)))
    else:
        add("ymain", ymain, pl.BlockSpec((tm, D_MAIN), row))
    add("mkv", mkv, _const_spec((N_MEM, 2 * D_MEM)))
    add("w_out", w_out, _const_spec((D_MIX, D_MODEL)))
    add("ln_g", ln_g, _const_spec((1, D_MODEL)))
    if not (pool and bwd):
        add("xres", xres, pl.BlockSpec((tm, D_MODEL), row))
        add("ln_b", ln_b, _const_spec((1, D_MODEL)))
    if loss_head:
        add("target", target, pl.BlockSpec((tm, D_MODEL), row))
    if pool and bwd:
        add("z", z, pl.BlockSpec((tm, D_MODEL), row))
        add("dy", dy, pl.BlockSpec((tm, D_MODEL), row))

    onames, oshapes, ospecs = [], [], []

    def add_out(name, shape, dtype, spec):
        onames.append(name)
        oshapes.append(jax.ShapeDtypeStruct(shape, dtype))
        ospecs.append(spec)

    const2 = lambda i: (0, 0)
    if not bwd:
        add_out("z", (S, D_MODEL), F32, pl.BlockSpec((tm, D_MODEL), row))
        add_out("xout", (S, D_MODEL), F32, pl.BlockSpec((tm, D_MODEL), row))
    else:
        add_out("dz", (S, D_MODEL), F32, pl.BlockSpec((tm, D_MODEL), row))
        add_out("dmain", (S, D_MAIN), BF16, pl.BlockSpec((tm, D_MAIN), row))
        add_out("drest", (S, D_IN - D_MAIN), BF16, pl.BlockSpec((tm, D_IN - D_MAIN), row))
        add_out("dw_out", (D_MIX, D_MODEL), F32, pl.BlockSpec((D_MIX, D_MODEL), const2))
        add_out("dmkv", (N_MEM, 2 * D_MEM), F32, pl.BlockSpec((N_MEM, 2 * D_MEM), const2))
        add_out("dln_g", (1, D_MODEL), F32, pl.BlockSpec((1, D_MODEL), const2))
        add_out("dln_b", (1, D_MODEL), F32, pl.BlockSpec((1, D_MODEL), const2))
        if pool:
            add_out("dpool_w", (4, POOL_GROUP, POOL_GROUP), F32,
                    pl.BlockSpec((4, POOL_GROUP, POOL_GROUP), lambda i: (0, 0, 0)))
            add_out("dpool_scale", (1, D_MAIN), F32, pl.BlockSpec((1, D_MAIN), const2))
        else:
            add_out("loss", (1, D_MODEL), F32, pl.BlockSpec((1, D_MODEL), const2))
            add_out("drow", (FOX_HEADS, S), F32, pl.BlockSpec((FOX_HEADS, tm), lambda i: (0, i)))

    scratch = [pltpu.VMEM((tm, D_MIX), BF16),
               pltpu.VMEM((tm, D_MEM), F32)]
    if pool:
        scratch.append(pltpu.VMEM((tm + 2 * POOL_HALO, D_MAIN), F32))
    if rev:
        scratch.append(pltpu.VMEM((POOL_HALO, D_MAIN), F32))
    n_in, n_out = len(names), len(onames)

    def body(*refs):
        R = dict(zip(names, refs[:n_in]))
        O = dict(zip(onames, refs[n_in:n_in + n_out]))
        sc = refs[n_in + n_out:]
        yc_ref, ymem_ref = sc[0], sc[1]
        ext_ref = sc[2] if pool else None
        carry_ref = sc[3] if rev else None
        i = pl.program_id(0)
        t = t_of(i)
        h_ref = R["h"]
        gamma = R["ln_g"][...]

        if bwd:
            @pl.when(i == 0)
            def _():
                for nm in ("dw_out", "dmkv", "dln_g", "dln_b", "dpool_w", "dpool_scale", "loss"):
                    if nm in O:
                        O[nm][...] = jnp.zeros_like(O[nm])
                if rev:
                    carry_ref[...] = jnp.zeros_like(carry_ref)

        if pool:
            u = h_ref[:, 0:D_MAIN].astype(F32)
            halo = R["halo"][...].astype(F32)
            ext_ref[0:POOL_HALO, :] = jnp.where(t > 0, halo, 0.0)
            ext_ref[POOL_HALO:POOL_HALO + tm, :] = u
            tpos = t * tm + lax.broadcasted_iota(jnp.int32, (tm, 1), 0)
            pms, invcs = [], []
            for gi, w in enumerate(POOL_WINDOWS):
                cs = slice(gi * POOL_GROUP, (gi + 1) * POOL_GROUP)
                acc = ext_ref[POOL_HALO:POOL_HALO + tm, cs]
                for k in range(1, w):
                    acc = acc + ext_ref[POOL_HALO - k:POOL_HALO - k + tm, cs]
                invc = 1.0 / jnp.minimum(tpos + 1, w).astype(F32)
                pm = (acc * invc - u[:, cs]).astype(BF16)
                pms.append(pm)
                invcs.append(invc)
            mixed = [_dot(pms[gi], R["pool_w"][gi]) for gi in range(4)]
            ps = R["pool_scale"][...]
            y_main = [mixed[gi] * ps[:, gi * POOL_GROUP:(gi + 1) * POOL_GROUP] for gi in range(4)]
        else:
            y_main = [R["ymain"][:, gi * 256:(gi + 1) * 256].astype(F32) for gi in range(4)]

        probs = []
        for hd in range(MEM_HEADS):
            sl = slice(D_MAIN + hd * MEM_HEAD_DIM, D_MAIN + (hd + 1) * MEM_HEAD_DIM)
            ksl = slice(hd * MEM_HEAD_DIM, (hd + 1) * MEM_HEAD_DIM)
            vsl = slice(D_MEM + hd * MEM_HEAD_DIM, D_MEM + (hd + 1) * MEM_HEAD_DIM)
            s = _dot_nt(h_ref[:, sl], R["mkv"][:, ksl]) * mem_scale
            e = jnp.exp(s - jnp.max(s, axis=1, keepdims=True))
            p = e / jnp.sum(e, axis=1, keepdims=True)
            probs.append(p)
            ymem_ref[:, ksl] = _dot(p.astype(BF16), R["mkv"][:, vsl])

        g_off = D_MIX
        gate_d = []
        for gi in range(4):
            cs = slice(gi * 256, (gi + 1) * 256)
            gm = h_ref[:, g_off + gi * 256:g_off + (gi + 1) * 256].astype(F32)
            sv, sd = _silu_and_grad(gm)
            yc_ref[:, cs] = (y_main[gi] * sv).astype(BF16)
            gate_d.append((sv, sd))
        gq = h_ref[:, g_off + D_MAIN:D_IN].astype(F32)
        svq, sdq = _silu_and_grad(gq)
        yc_ref[:, D_MAIN:D_MIX] = (ymem_ref[...] * svq).astype(BF16)

        if pool and bwd:
            zt = R["z"][...]
        else:
            o = _dot(yc_ref[...], R["w_out"][...])
            zt = ALPHA * R["xres"][...] + o
        mu = jnp.mean(zt, axis=1, keepdims=True)
        zc = zt - mu
        var = jnp.mean(zc * zc, axis=1, keepdims=True)
        rstd = lax.rsqrt(var + LN_EPS)
        xhat = zc * rstd
        if not bwd:
            O["z"][...] = zt
            O["xout"][...] = xhat * gamma + R["ln_b"][...]
            return

        if loss_head:
            xo = xhat * gamma + R["ln_b"][...]
            err = xo - R["target"][...]
            O["loss"][...] += jnp.sum(err * err, axis=0, keepdims=True)
            dyt = err * (1.0 / D_MODEL)
        else:
            dyt = R["dy"][...]

        O["dln_g"][...] += jnp.sum(dyt * xhat, axis=0, keepdims=True)
        O["dln_b"][...] += jnp.sum(dyt, axis=0, keepdims=True)
        gdy = dyt * gamma
        m1 = jnp.mean(gdy, axis=1, keepdims=True)
        m2 = jnp.mean(gdy * xhat, axis=1, keepdims=True)
        dz = rstd * (gdy - m1 - xhat * m2)
        O["dz"][...] = dz
        dzb = dz.astype(BF16)

        for n0 in range(0, D_MIX, 512):
            O["dw_out"][n0:n0 + 512, :] += _dot_tn(yc_ref[:, n0:n0 + 512], dzb)
        dyc_mem = _dot_nt(dzb, R["w_out"][D_MAIN:D_MIX, :])

        O["drest"][:, D_MEM + D_MAIN:D_MEM + D_MAIN + D_MEM] = (dyc_mem * ymem_ref[...] * sdq).astype(BF16)
        dymem = dyc_mem * svq
        for hd in range(MEM_HEADS):
            sl = slice(D_MAIN + hd * MEM_HEAD_DIM, D_MAIN + (hd + 1) * MEM_HEAD_DIM)
            ksl = slice(hd * MEM_HEAD_DIM, (hd + 1) * MEM_HEAD_DIM)
            vsl = slice(D_MEM + hd * MEM_HEAD_DIM, D_MEM + (hd + 1) * MEM_HEAD_DIM)
            p = probs[hd]
            dyb = dymem[:, ksl].astype(BF16)
            dp = _dot_nt(dyb, R["mkv"][:, vsl])
            ds = p * (dp - jnp.sum(dp * p, axis=1, keepdims=True)) * mem_scale
            dsb = ds.astype(BF16)
            O["drest"][:, ksl] = _dot(dsb, R["mkv"][:, ksl]).astype(BF16)
            O["dmkv"][:, ksl] += _dot_tn(dsb, h_ref[:, sl])
            O["dmkv"][:, vsl] += _dot_tn(p.astype(BF16), dyb)

        dmain = []
        for gi in range(4):
            cs = slice(gi * 256, (gi + 1) * 256)
            dyc_g = _dot_nt(dzb, R["w_out"][cs, :])
            sv, sd = gate_d[gi]
            O["drest"][:, D_MEM + gi * 256:D_MEM + (gi + 1) * 256] = (dyc_g * y_main[gi] * sd).astype(BF16)
            dmain.append(dyc_g * sv)

        if not pool:
            prod = []
            for gi in range(4):
                cs = slice(gi * 256, (gi + 1) * 256)
                db16 = dmain[gi].astype(BF16)
                O["dmain"][:, cs] = db16
                prod.append(db16.astype(F32) * R["ymain"][:, cs].astype(F32))
            dcol = jnp.zeros((tm, LANES), F32)
            for gi in range(4):
                dr = lax.broadcasted_iota(jnp.int32, (256, LANES), 0)
                hc = lax.broadcasted_iota(jnp.int32, (256, LANES), 1)
                sel = jnp.where(jnp.right_shift(dr, 6) + gi * 4 == hc, 1.0, 0.0).astype(BF16)
                hi, mid, lo = _split3(prod[gi])
                dcol = dcol + ((_dot(hi, sel) + _dot(mid, sel)) + _dot(lo, sel))
            O["drow"][...] = dcol.T[0:FOX_HEADS, :]
            return

        ps = R["pool_scale"][...]
        dpm_list = []
        for gi in range(4):
            cs = slice(gi * 256, (gi + 1) * 256)
            O["dpool_scale"][:, cs] += jnp.sum(dmain[gi] * mixed[gi], axis=0, keepdims=True)
            dmix = (dmain[gi] * ps[:, cs]).astype(BF16)
            O["dpool_w"][gi] += _dot_tn(pms[gi], dmix)
            dpm = _dot_nt(dmix, R["pool_w"][gi])
            dpm_list.append(dpm)
            ext_ref[0:tm, cs] = dpm * invcs[gi]
        ext_ref[tm:tm + POOL_HALO, :] = carry_ref[...]
        carry_ref[...] = ext_ref[0:POOL_HALO, :]
        for gi, w in enumerate(POOL_WINDOWS):
            cs = slice(gi * 256, (gi + 1) * 256)
            acc = ext_ref[0:tm, cs]
            for k in range(1, w):
                acc = acc + ext_ref[k:k + tm, cs]
            O["dmain"][:, cs] = (acc - dpm_list[gi]).astype(BF16)

    outs = pl.pallas_call(
        body, name=f"mix_{kind}_{mode}", grid=(n,),
        in_specs=specs, out_specs=ospecs, out_shape=oshapes,
        scratch_shapes=scratch, compiler_params=_params(),
    )(*arrays)
    return dict(zip(onames, outs))


def _lin_bwd(xin, dhs, ws, res, *, tm, name):
    S, K = xin.shape
    nj = len(dhs)
    nr = len(res)
    widths = [w.shape[1] for w in ws]
    scales = [s for _, s in res]

    def body(*refs):
        x_ref = refs[0]
        dh_refs = refs[1:1 + nj]
        w_refs = refs[1 + nj:1 + 2 * nj]
        r_refs = refs[1 + 2 * nj:1 + 2 * nj + nr]
        dx_ref = refs[1 + 2 * nj + nr]
        dw_refs = refs[2 + 2 * nj + nr:]
        i = pl.program_id(0)

        @pl.when(i == 0)
        def _():
            for dw in dw_refs:
                dw[...] = jnp.zeros_like(dw)

        xb = x_ref[...].astype(BF16)
        dx = jnp.zeros((tm, K), F32)
        for r_ref, sc in zip(r_refs, scales):
            dx = dx + sc * r_ref[...]
        for j in range(nj):
            N = widths[j]
            nc = 512 if N % 512 == 0 else N
            for n0 in range(0, N, nc):
                dhb = dh_refs[j][:, n0:n0 + nc].astype(BF16)
                dx = dx + _dot_nt(dhb, w_refs[j][:, n0:n0 + nc])
                dw_refs[j][:, n0:n0 + nc] += _dot_tn(xb, dhb)
        dx_ref[...] = dx

    in_specs = [pl.BlockSpec((tm, K), lambda i: (i, 0))]
    in_specs += [pl.BlockSpec((tm, N), lambda i: (i, 0)) for N in widths]
    in_specs += [_const_spec((K, N)) for N in widths]
    in_specs += [pl.BlockSpec((tm, K), lambda i: (i, 0)) for _ in res]
    out_specs = [pl.BlockSpec((tm, K), lambda i: (i, 0))]
    out_specs += [pl.BlockSpec((K, N), lambda i: (0, 0)) for N in widths]
    out_shape = [jax.ShapeDtypeStruct((S, K), F32)]
    out_shape += [jax.ShapeDtypeStruct((K, N), F32) for N in widths]
    outs = pl.pallas_call(
        body, name=name, grid=(S // tm,),
        in_specs=in_specs, out_specs=out_specs, out_shape=out_shape,
        compiler_params=_params(),
    )(xin, *dhs, *ws, *[r for r, _ in res])
    return outs[0], list(outs[1:])


def _wgrad(xin, dh, *, name):
    M, K = xin.shape
    N = dh.shape[1]

    def body(x_ref, dh_ref, o_ref):
        o_ref[...] = _dot_tn(x_ref[...].astype(BF16), dh_ref[...].astype(BF16))

    return pl.pallas_call(
        body, name=name, out_shape=jax.ShapeDtypeStruct((K, N), F32),
        compiler_params=pltpu.CompilerParams(vmem_limit_bytes=VMEM_LIMIT),
    )(xin, dh)


def _rows_layout(a16, nt, tt):
    S = a16.shape[1]
    a = a16.reshape(8, 2, nt, tt).transpose(0, 2, 1, 3)
    return jnp.pad(a, ((0, 0), (0, 0), (0, 6), (0, 0)))


def _fox_fwd(h1, k, v, cum, ck_rows, *, tq):
    S = k.shape[0]
    nq = S // tq
    tk = tq

    def body(q_ref, k_ref, v_ref, cum_ref, ckr_ref, o_ref, lse_ref):
        hp = pl.program_id(0)
        qi = pl.program_id(1)
        lane = lax.broadcasted_iota(jnp.int32, (tq, LANES), 1)
        half0 = lane < FOX_HEAD_DIM
        q = q_ref[...] * jnp.asarray(FOX_SCALE, BF16)
        zero = jnp.zeros_like(q)
        cumc = cum_ref[...]
        rr = lax.broadcasted_iota(jnp.int32, (tq, tk), 0)
        cc = lax.broadcasted_iota(jnp.int32, (tq, tk), 1)
        results = []
        for hh in range(2):
            qm = jnp.where(half0, q, zero) if hh == 0 else jnp.where(half0, zero, q)
            cq = jnp.sum(jnp.where(lane == 2 * hp + hh, cumc, 0.0), axis=1, keepdims=True)

            def chunk(ki, carry, masked):
                m, l, acc = carry
                k0 = pl.multiple_of(ki * tk, tk)
                kc = k_ref[pl.ds(k0, tk), :]
                vc = v_ref[pl.ds(k0, tk), :]
                ck = ckr_ref[0, ki][hh:hh + 1, :]
                s = _dot_nt(qm, kc) + (cq - ck)
                if masked:
                    s = jnp.where(cc <= rr, s, -jnp.inf)
                m_new = jnp.maximum(m, jnp.max(s, axis=1, keepdims=True))
                p = jnp.exp(s - m_new)
                alpha = jnp.exp(m - m_new)
                l = alpha * l + jnp.sum(p, axis=1, keepdims=True)
                acc = alpha * acc + _dot(p.astype(BF16), vc)
                return m_new, l, acc

            init = (jnp.full((tq, 1), -jnp.inf, F32), jnp.zeros((tq, 1), F32),
                    jnp.zeros((tq, LANES), F32))
            carry = lax.fori_loop(0, qi, lambda ki, c: chunk(ki, c, False), init)
            m, l, acc = chunk(qi, carry, True)
            results.append((acc / l, m + jnp.log(l)))
        o_ref[...] = jnp.where(half0, results[0][0], results[1][0])
        lse_cols = jnp.where(lane == 0, results[0][1], 0.0) + jnp.where(lane == 1, results[1][1], 0.0)
        lse_ref[0, 0] = lse_cols.T[0:8, :]

    return pl.pallas_call(
        body, name="fox_fwd", grid=(8, nq),
        in_specs=[pl.BlockSpec((tq, LANES), lambda hp, qi: (qi, hp)),
                  pl.BlockSpec((S, LANES), lambda hp, qi: (0, hp)),
                  pl.BlockSpec((S, LANES), lambda hp, qi: (0, hp)),
                  pl.BlockSpec((tq, LANES), lambda hp, qi: (qi, 0)),
                  pl.BlockSpec((1, nq, 8, tk), lambda hp, qi: (hp, 0, 0, 0))],
        out_specs=[pl.BlockSpec((tq, LANES), lambda hp, qi: (qi, hp)),
                   pl.BlockSpec((1, 1, 8, tq), lambda hp, qi: (hp, qi, 0, 0))],
        out_shape=[jax.ShapeDtypeStruct((S, D_MAIN), F32),
                   jax.ShapeDtypeStruct((8, nq, 8, tq), F32)],
        compiler_params=_params(2),
    )(h1, k, v, cum, ck_rows)


def _fox_bwd(h1, k, v, do, cum, cq_rows, lse_rows, d_rows, *, tq):
    S = k.shape[0]
    nq = S // tq
    tk = tq

    def body(k_ref, v_ref, q_ref, do_ref, cum_ref, cqr_ref, lser_ref, dr_ref,
             dq_ref, dk_ref, dv_ref, dck_ref, dcq_ref):
        hp = pl.program_id(0)
        kj = pl.program_id(1)

        @pl.when(kj == 0)
        def _():
            dq_ref[...] = jnp.zeros_like(dq_ref)
            dcq_ref[...] = jnp.zeros_like(dcq_ref)

        lane = lax.broadcasted_iota(jnp.int32, (tk, LANES), 1)
        half0 = lane < FOX_HEAD_DIM
        kt = k_ref[...]
        vt = v_ref[...]
        zero = jnp.zeros_like(kt)
        cumc = cum_ref[...]
        rr = lax.broadcasted_iota(jnp.int32, (tk, tq), 0)
        cc = lax.broadcasted_iota(jnp.int32, (tk, tq), 1)
        dk_tot = jnp.zeros((tk, LANES), F32)
        dv_tot = jnp.zeros((tk, LANES), F32)
        dck_cols = jnp.zeros((tk, LANES), F32)
        for hh in range(2):
            mine = half0 if hh == 0 else jnp.logical_not(half0)
            km = jnp.where(mine, kt, zero)
            vm = jnp.where(mine, vt, zero)
            ck = jnp.sum(jnp.where(lane == 2 * hp + hh, cumc, 0.0), axis=1, keepdims=True)

            def chunk(qi, carry, masked):
                dk_acc, dv_acc, dck_acc = carry
                q0 = pl.multiple_of(qi * tq, tq)
                qc = q_ref[pl.ds(q0, tq), :]
                doc = do_ref[pl.ds(q0, tq), :]
                cq = cqr_ref[0, qi][hh:hh + 1, :]
                lse = lser_ref[0, qi][hh:hh + 1, :]
                dd = dr_ref[0, qi][hh:hh + 1, :]
                st = _dot_nt(km, qc) * FOX_SCALE + (cq - ck)
                pt = jnp.exp(st - lse)
                if masked:
                    pt = jnp.where(rr <= cc, pt, 0.0)
                dpt = _dot_nt(vm, doc)
                dst = pt * (dpt - dd)
                dsb = dst.astype(BF16)
                dv_acc = dv_acc + _dot(pt.astype(BF16), doc)
                dk_acc = dk_acc + _dot(dsb, qc)
                dq_ref[pl.ds(q0, tq), :] += _dot_tn(dsb, km) * FOX_SCALE
                dck_acc = dck_acc + jnp.sum(dst, axis=1, keepdims=True)
                dcq_ref[0, qi, hh:hh + 1, :] += jnp.sum(dst, axis=0, keepdims=True)
                return dk_acc, dv_acc, dck_acc

            init = (jnp.zeros((tk, LANES), F32), jnp.zeros((tk, LANES), F32), jnp.zeros((tk, 1), F32))
            carry = chunk(kj, init, True)
            dk_acc, dv_acc, dck_acc = lax.fori_loop(kj + 1, nq, lambda qi, c: chunk(qi, c, False), carry)
            dk_tot = dk_tot + jnp.where(mine, dk_acc * FOX_SCALE, 0.0)
            dv_tot = dv_tot + jnp.where(mine, dv_acc, 0.0)
            dck_cols = dck_cols + jnp.where(lane == hh, dck_acc, 0.0)
        dk_ref[...] = dk_tot.astype(BF16)
        dv_ref[...] = dv_tot.astype(BF16)
        dck_ref[0, 0] = dck_cols.T[0:8, :]

    return pl.pallas_call(
        body, name="fox_bwd", grid=(8, nq),
        in_specs=[pl.BlockSpec((tk, LANES), lambda hp, kj: (kj, hp)),
                  pl.BlockSpec((tk, LANES), lambda hp, kj: (kj, hp)),
                  pl.BlockSpec((S, LANES), lambda hp, kj: (0, hp)),
                  pl.BlockSpec((S, LANES), lambda hp, kj: (0, hp)),
                  pl.BlockSpec((tk, LANES), lambda hp, kj: (kj, 0)),
                  pl.BlockSpec((1, nq, 8, tq), lambda hp, kj: (hp, 0, 0, 0)),
                  pl.BlockSpec((1, nq, 8, tq), lambda hp, kj: (hp, 0, 0, 0)),
                  pl.BlockSpec((1, nq, 8, tq), lambda hp, kj: (hp, 0, 0, 0))],
        out_specs=[pl.BlockSpec((S, LANES), lambda hp, kj: (0, hp)),
                   pl.BlockSpec((tk, LANES), lambda hp, kj: (kj, hp)),
                   pl.BlockSpec((tk, LANES), lambda hp, kj: (kj, hp)),
                   pl.BlockSpec((1, 1, 8, tk), lambda hp, kj: (hp, kj, 0, 0)),
                   pl.BlockSpec((1, nq, 8, tq), lambda hp, kj: (hp, 0, 0, 0))],
        out_shape=[jax.ShapeDtypeStruct((S, D_MAIN), F32),
                   jax.ShapeDtypeStruct((S, D_MAIN), BF16),
                   jax.ShapeDtypeStruct((S, D_MAIN), BF16),
                   jax.ShapeDtypeStruct((8, nq, 8, tk), F32),
                   jax.ShapeDtypeStruct((8, nq, 8, tq), F32)],
        compiler_params=_params(2),
    )(k, v, h1, do, cum, cq_rows, lse_rows, d_rows)


def _adamw(w, g, m, v, *, name):
    Rr, C = w.shape
    tr = 256 if Rr % 256 == 0 else Rr
    c1 = 1.0 / (1.0 - ADAM_B1 ** ADAM_STEP)
    c2 = 1.0 / (1.0 - ADAM_B2 ** ADAM_STEP)

    def body(w_ref, g_ref, m_ref, v_ref, d_ref, nm_ref, nv_ref):
        gv = g_ref[...]
        nm = ADAM_B1 * m_ref[...] + (1.0 - ADAM_B1) * gv
        nv = ADAM_B2 * v_ref[...] + (1.0 - ADAM_B2) * (gv * gv)
        d_ref[...] = -ADAM_LR * ((nm * c1) / (jnp.sqrt(nv * c2) + ADAM_EPS) + ADAM_WD * w_ref[...])
        nm_ref[...] = nm
        nv_ref[...] = nv

    spec = pl.BlockSpec((tr, C), lambda i: (i, 0))
    sds = jax.ShapeDtypeStruct((Rr, C), F32)
    return pl.pallas_call(
        body, name=name, grid=(Rr // tr,),
        in_specs=[spec] * 4, out_specs=[spec] * 3, out_shape=[sds] * 3,
        compiler_params=_params(),
    )(w, g, m, v)


_ANY = pl.BlockSpec(memory_space=pl.ANY)
_MESH = pl.DeviceIdType.MESH


def _place():
    x, y, c = lax.axis_index("x"), lax.axis_index("y"), lax.axis_index("c")
    return x, y, c


def _all_gather_shards(pack):
    dt = pack.dtype

    def body(p_ref, out_ref, send_sems, recv_sems):
        x, y, c = _place()
        sib = (x, y, 1 - c)
        chips = [(1 - x, y), (x, 1 - y), (1 - x, 1 - y)]
        me = 2 * x + y

        def copy(k, chip_idx, half, to, src=None):
            dst = out_ref.at[chip_idx, half]
            return pltpu.make_async_remote_copy(
                src_ref=dst if src is None else src, dst_ref=dst,
                send_sem=send_sems.at[k], recv_sem=recv_sems.at[k],
                device_id=to, device_id_type=_MESH)

        first = [copy(j, me, c, (*chip, c), src=p_ref.at[c]) for j, chip in enumerate(chips)]
        for cp in first:
            cp.start()
        passed = [copy(3 + j, 2 * chip[0] + chip[1], c, sib) for j, chip in enumerate(chips)]
        for j, chip in enumerate(chips):
            copy(j, 2 * chip[0] + chip[1], c, sib).wait_recv()
            passed[j].start()
        for j, chip in enumerate(chips):
            copy(3 + j, 2 * chip[0] + chip[1], 1 - c, sib).wait_recv()
        for cp in first + passed:
            cp.wait_send()

    return pl.pallas_call(
        body, name="all_gather_shards",
        in_specs=[_ANY], out_specs=_ANY,
        out_shape=jax.ShapeDtypeStruct((N_CHIPS, 2, HALF_ROWS, 1024), dt),
        scratch_shapes=[pltpu.SemaphoreType.DMA((6,)), pltpu.SemaphoreType.DMA((6,))],
    )(pack)


def _send_half_to_sibling(gpack):
    def body(g_ref, out_ref, send_sem, recv_sem):
        x, y, c = _place()
        sib = (x, y, 1 - c)
        cps = [pltpu.make_async_remote_copy(
            src_ref=g_ref.at[j, 1 - c], dst_ref=out_ref.at[j],
            send_sem=send_sem.at[j], recv_sem=recv_sem.at[j],
            device_id=sib, device_id_type=_MESH) for j in range(N_CHIPS)]
        for cp in cps:
            cp.start()
        for cp in cps:
            cp.wait_recv()
        for cp in cps:
            cp.wait_send()

    return pl.pallas_call(
        body, name="pair_send",
        in_specs=[_ANY], out_specs=_ANY,
        out_shape=jax.ShapeDtypeStruct((N_CHIPS, HALF_ROWS, 1024), F32),
        scratch_shapes=[pltpu.SemaphoreType.DMA((N_CHIPS,)), pltpu.SemaphoreType.DMA((N_CHIPS,))],
    )(gpack)


def _pair_sum(gpack, recv, c_arr, *, tr):
    def body(c_ref, a_ref, b_ref, o_ref):
        o_ref[...] = (a_ref[...] + b_ref[...]).astype(BF16)

    grid_spec = pltpu.PrefetchScalarGridSpec(
        num_scalar_prefetch=1, grid=(N_CHIPS, HALF_ROWS // tr),
        in_specs=[pl.BlockSpec((None, None, tr, 1024), lambda j, i, c_ref: (j, c_ref[0], i, 0)),
                  pl.BlockSpec((None, tr, 1024), lambda j, i, c_ref: (j, i, 0))],
        out_specs=pl.BlockSpec((None, tr, 1024), lambda j, i, c_ref: (j, i, 0)))
    return pl.pallas_call(
        body, name="pair_sum", grid_spec=grid_spec,
        out_shape=jax.ShapeDtypeStruct((N_CHIPS, HALF_ROWS, 1024), BF16),
        compiler_params=_params(2),
    )(c_arr, gpack, recv)


def _scatter_pieces(psum):
    def body(p_ref, out_ref, send_sems, recv_sems):
        x, y, c = _place()
        chips = [(1 - x, y), (x, 1 - y), (1 - x, 1 - y)]
        me = 2 * x + y
        cps = []
        for j, chip in enumerate(chips):
            them = 2 * chip[0] + chip[1]
            cps.append(pltpu.make_async_remote_copy(
                src_ref=p_ref.at[them], dst_ref=out_ref.at[me],
                send_sem=send_sems.at[j], recv_sem=recv_sems.at[j],
                device_id=(*chip, c), device_id_type=_MESH))
        for cp in cps:
            cp.start()
        for cp in cps:
            cp.wait_recv()
        for cp in cps:
            cp.wait_send()

    return pl.pallas_call(
        body, name="scatter_pieces",
        in_specs=[_ANY], out_specs=_ANY,
        out_shape=jax.ShapeDtypeStruct((N_CHIPS, HALF_ROWS, 1024), BF16),
        scratch_shapes=[pltpu.SemaphoreType.DMA((3,)), pltpu.SemaphoreType.DMA((3,))],
    )(psum)


def _sum_pieces(pieces, *, tr):
    def body(p_ref, o_ref):
        acc = p_ref[0].astype(F32) + p_ref[1].astype(F32)
        acc = acc + p_ref[2].astype(F32)
        o_ref[...] = acc + p_ref[3].astype(F32)

    return pl.pallas_call(
        body, name="sum_pieces", grid=(HALF_ROWS // tr,),
        in_specs=[pl.BlockSpec((N_CHIPS, tr, 1024), lambda i: (0, i, 0))],
        out_specs=pl.BlockSpec((tr, 1024), lambda i: (i, 0)),
        out_shape=jax.ShapeDtypeStruct((HALF_ROWS, 1024), F32),
        compiler_params=_params(),
    )(pieces)


def _exchange_halves(total):
    def body(t_ref, out_ref, send_sem, recv_sem):
        x, y, c = _place()
        sib = (x, y, 1 - c)
        cp = pltpu.make_async_remote_copy(
            src_ref=t_ref, dst_ref=out_ref.at[c], send_sem=send_sem, recv_sem=recv_sem,
            device_id=sib, device_id_type=_MESH)
        cp.start()
        cp.wait_recv()
        cp.wait_send()

    return pl.pallas_call(
        body, name="exchange_halves",
        in_specs=[_ANY], out_specs=_ANY,
        out_shape=jax.ShapeDtypeStruct((2, HALF_ROWS, 1024), F32),
        scratch_shapes=[pltpu.SemaphoreType.DMA, pltpu.SemaphoreType.DMA],
    )(total)


def _pad_rows(a, rows):
    return jnp.pad(a, ((0, rows - a.shape[0]), (0, 0)))


def _pack_weight_shard(w_in, w_mem_kv, w_out, pool_w, w_kv_shared, pool_scale):
    ps_bits = lax.bitcast_convert_type(pool_scale.reshape(-1), BF16).reshape(1, -1)
    ps_row = jnp.pad(ps_bits, ((0, 0), (0, 1024 - ps_bits.shape[1])))
    parts = [
        w_in.astype(BF16).reshape(ROWS_W_IN, 1024),
        w_mem_kv.astype(BF16).reshape(ROWS_W_MKV, 1024),
        w_out.astype(BF16).reshape(ROWS_W_OUT, 1024),
        pool_w.astype(BF16).reshape(ROWS_POOL_W, 1024),
        _pad_rows(w_kv_shared.astype(BF16).reshape(KV_SHARD, 1024), ROWS_W_KV),
        _pad_rows(ps_row, ROWS_SMALL),
        jnp.zeros((PACK_ROWS - OFF_LN_G, 1024), BF16),
    ]
    return jnp.concatenate(parts, axis=0).reshape(2, HALF_ROWS, 1024)


def _unpack_weights(g):
    w_in = g[:, OFF_W_IN:OFF_W_IN + ROWS_W_IN].reshape(4, 2, D_MODEL, D_IN // 4)
    w_in = w_in.transpose(1, 2, 0, 3).reshape(2, D_MODEL, D_IN)
    w_mkv = g[:, OFF_W_MKV:OFF_W_MKV + ROWS_W_MKV].reshape(4, 2, D_MODEL // 4, 2 * D_MEM)
    w_mkv = w_mkv.transpose(1, 0, 2, 3).reshape(2, D_MODEL, 2 * D_MEM)
    w_out = g[:, OFF_W_OUT:OFF_W_OUT + ROWS_W_OUT].reshape(4, 2, D_MIX // 4, D_MODEL)
    w_out = w_out.transpose(1, 0, 2, 3).reshape(2, D_MIX, D_MODEL)
    pool_w = g[:, OFF_POOL_W:OFF_POOL_W + ROWS_POOL_W].reshape(4, 4, POOL_GROUP // 4, POOL_GROUP)
    pool_w = pool_w.transpose(1, 0, 2, 3).reshape(4, POOL_GROUP, POOL_GROUP)
    w_kv = g[:, OFF_W_KV:OFF_W_KV + KV_SHARD].reshape(4, D_MODEL, KV_SHARD)
    w_kv = w_kv.transpose(1, 0, 2).reshape(D_MODEL, KV_COLS)
    ps_bits = g[:, OFF_POOL_S, 0:512].reshape(4, 256, 2)
    pool_scale = lax.bitcast_convert_type(ps_bits, F32).reshape(1, D_MAIN)
    return w_in, w_mkv, w_out, pool_w, w_kv, pool_scale


def _pack_grads(g_w_in, g_w_mkv, g_w_out, g_pool_w, g_w_kv, g_pool_scale, g_ln_g, g_ln_b, g_bf):
    def rep(a):
        a = _pad_rows(a, ROWS_SMALL)
        return jnp.broadcast_to(a[None], (4,) + a.shape)

    parts = [
        g_w_in.reshape(2, D_MODEL, 4, D_IN // 4).transpose(2, 0, 1, 3).reshape(4, ROWS_W_IN, 1024),
        g_w_mkv.reshape(2, 4, D_MODEL // 4, 2 * D_MEM).transpose(1, 0, 2, 3).reshape(4, ROWS_W_MKV, 1024),
        g_w_out.reshape(2, 4, D_MIX // 4, D_MODEL).transpose(1, 0, 2, 3).reshape(4, ROWS_W_OUT, 1024),
        g_pool_w.reshape(4, 4, POOL_GROUP // 4, POOL_GROUP).transpose(1, 0, 2, 3).reshape(4, ROWS_POOL_W, 1024),
        jnp.pad(g_w_kv.reshape(D_MODEL, 4, KV_SHARD).transpose(1, 0, 2).reshape(4, KV_SHARD, 1024),
                ((0, 0), (0, ROWS_W_KV - KV_SHARD), (0, 0))),
        jnp.pad(g_pool_scale.reshape(4, 1, 256), ((0, 0), (0, ROWS_SMALL - 1), (0, 1024 - 256))),
        rep(g_ln_g), rep(g_ln_b),
        rep(jnp.pad(g_bf.reshape(1, -1), ((0, 0), (0, 1024 - g_bf.shape[0])))),
        jnp.zeros((4, ROWS_PAD, 1024), F32),
    ]
    return jnp.concatenate(parts, axis=1).reshape(4, 2, HALF_ROWS, 1024)


def _local_step(x, mem, target, w_in, w_mkv, w_out, pool_w, pool_scale, w_kv, ln_g, ln_b, b_forget,
                *, tm=256, tq=512):
    S = x.shape[0]
    nq = S // tq
    g_rows = [ln_g[l:l + 1] for l in range(2)]
    b_rows = [ln_b[l:l + 1] for l in range(2)]
    wk, wv = w_kv[:, 0:D_MAIN], w_kv[:, D_MAIN:2 * D_MAIN]
    wf = jnp.pad(w_kv[:, 2 * D_MAIN:], ((0, 0), (0, LANES - FOX_HEADS)))
    bf_row = jnp.pad(b_forget.reshape(1, -1), ((0, 0), (0, LANES - FOX_HEADS)))

    mkv = [_linear_fwd(mem, w_mkv[l], tm=N_MEM, name=f"mem_kv{l}") for l in range(2)]

    h0 = _linear_fwd(x, w_in[0], tm=tm, name="in_proj0")
    f0 = _mix("pool", "fwd", h=h0, xres=x, mkv=mkv[0], w_out=w_out[0], ln_g=g_rows[0], ln_b=b_rows[0],
              pool_w=pool_w, pool_scale=pool_scale, tm=tm)
    z0, x1 = f0["z"], f0["xout"]
    k, v, fl, cum = _kv_proj(x1, wk, wv, wf, bf_row, tm=tm)
    h1 = _linear_fwd(x1, w_in[1], tm=tm, name="in_proj1")
    cum_rows = _rows_layout(cum[:, 0:FOX_HEADS].T, nq, tq)
    ymain1, lse_rows = _fox_fwd(h1, k, v, cum, cum_rows, tq=tq)

    b1 = _mix("fox", "bwd", h=h1, xres=x1, mkv=mkv[1], w_out=w_out[1], ln_g=g_rows[1], ln_b=b_rows[1],
              ymain=ymain1, target=target, tm=tm)
    d_rows = _rows_layout(b1["drow"], nq, tq)
    dq, dk, dv, dck_rows, dcq_rows = _fox_bwd(h1, k, v, b1["dmain"], cum, cum_rows, lse_rows, d_rows, tq=tq)

    def cols(rows):
        a = rows[:, :, 0:2, :].transpose(0, 2, 1, 3).reshape(FOX_HEADS, S).T
        return jnp.pad(a, ((0, 0), (0, LANES - FOX_HEADS)))

    df, dbf = _gate_bwd(cols(dcq_rows), cols(dck_rows), fl, tm=tm)

    dx1a, (dwu1, dwr1) = _lin_bwd(
        x1, [dq, b1["drest"]], [w_in[1][:, 0:D_MAIN], w_in[1][:, D_MAIN:]], [(b1["dz"], ALPHA)],
        tm=tm, name="in_proj1_bwd")
    dx1, (dwk, dwv, dwf) = _lin_bwd(x1, [dk, dv, df], [wk, wv, wf], [(dx1a, 1.0)], tm=tm, name="kv_proj_bwd")

    b0 = _mix("pool", "bwd", h=h0, mkv=mkv[0], w_out=w_out[0], ln_g=g_rows[0],
              pool_w=pool_w, pool_scale=pool_scale, z=z0, dy=dx1, tm=tm)
    dx, (dwu0, dwr0) = _lin_bwd(
        x, [b0["dmain"], b0["drest"]], [w_in[0][:, 0:D_MAIN], w_in[0][:, D_MAIN:]], [(b0["dz"], ALPHA)],
        tm=tm, name="in_proj0_bwd")
    dw_mkv = [_wgrad(mem, b["dmkv"], name=f"mem_kv{l}_bwd") for l, b in enumerate((b0, b1))]

    grads = dict(
        w_in=jnp.stack([jnp.concatenate([dwu0, dwr0], axis=1), jnp.concatenate([dwu1, dwr1], axis=1)]),
        w_mem_kv=jnp.stack(dw_mkv),
        w_out=jnp.stack([b0["dw_out"], b1["dw_out"]]),
        ln_g=jnp.concatenate([b0["dln_g"], b1["dln_g"]], axis=0),
        ln_b=jnp.concatenate([b0["dln_b"], b1["dln_b"]], axis=0),
        pool_w=b0["dpool_w"],
        pool_scale=b0["dpool_scale"],
        w_kv=jnp.concatenate([dwk, dwv, dwf[:, 0:FOX_HEADS]], axis=1),
        b_forget=dbf[0, 0:FOX_HEADS],
    )
    return b1["loss"], dx, grads


def kernel(x, mem, w_in, w_mem_kv, w_out, ln_g, ln_b, pool_w, pool_scale, w_kv_shared, b_forget, loss_target, m_w_in, m_w_mem_kv, m_w_out, m_ln_g, m_ln_b, m_pool_w, m_pool_scale, m_w_kv_shared, m_b_forget, v_w_in, v_w_mem_kv, v_w_out, v_ln_g, v_ln_b, v_pool_w, v_pool_scale, v_w_kv_shared, v_b_forget):
    c_arr = lax.axis_index("c").astype(jnp.int32).reshape(1)

    wpack = _pack_weight_shard(w_in, w_mem_kv, w_out, pool_w, w_kv_shared, pool_scale)
    me = 2 * lax.axis_index("x") + lax.axis_index("y")
    my_c = lax.axis_index("c")
    gathered = lax.dynamic_update_slice(_all_gather_shards(wpack), wpack[None], (me, 0, 0, 0))
    gathered = gathered.reshape(N_CHIPS, PACK_ROWS, 1024)
    fw_in, fw_mkv, fw_out, fpool_w, fw_kv, fpool_scale = _unpack_weights(gathered)

    loss_vec, dx, g = _local_step(x[0], mem[0], loss_target[0], fw_in, fw_mkv, fw_out, fpool_w,
                                  fpool_scale, fw_kv, ln_g, ln_b, b_forget)
    loss = lax.psum(0.5 / D_MODEL * jnp.sum(loss_vec), ("x", "y", "c"))

    gpack = _pack_grads(g["w_in"], g["w_mem_kv"], g["w_out"], g["pool_w"], g["w_kv"], g["pool_scale"],
                        g["ln_g"], g["ln_b"], g["b_forget"])
    from_sibling = _send_half_to_sibling(gpack)
    psum = _pair_sum(gpack, from_sibling, c_arr, tr=PACK_TILE)
    own_piece = lax.dynamic_slice(psum, (me, 0, 0), (1, HALF_ROWS, 1024))
    pieces = lax.dynamic_update_slice(_scatter_pieces(psum), own_piece, (me, 0, 0))
    total = _sum_pieces(pieces, tr=PACK_TILE)
    shard = lax.dynamic_update_slice(_exchange_halves(total), total[None], (my_c, 0, 0))
    shard = shard.reshape(PACK_ROWS, 1024)

    g_w_in = shard[OFF_W_IN:OFF_W_IN + ROWS_W_IN].reshape(w_in.shape)
    g_w_mkv = shard[OFF_W_MKV:OFF_W_MKV + ROWS_W_MKV].reshape(w_mem_kv.shape)
    g_w_out = shard[OFF_W_OUT:OFF_W_OUT + ROWS_W_OUT].reshape(w_out.shape)
    g_pool_w = shard[OFF_POOL_W:OFF_POOL_W + ROWS_POOL_W].reshape(pool_w.shape)
    g_w_kv = shard[OFF_W_KV:OFF_W_KV + KV_SHARD].reshape(w_kv_shared.shape)
    g_pool_scale = shard[OFF_POOL_S:OFF_POOL_S + 1, 0:256].reshape(pool_scale.shape)
    g_ln_g = shard[OFF_LN_G:OFF_LN_G + 2]
    g_ln_b = shard[OFF_LN_B:OFF_LN_B + 2]
    g_bf = shard[OFF_BF, 0:FOX_HEADS]

    names = ["w_in", "w_mem_kv", "w_out", "ln_g", "ln_b", "pool_w", "pool_scale", "w_kv_shared", "b_forget"]
    ws = [w_in, w_mem_kv, w_out, ln_g, ln_b, pool_w, pool_scale, w_kv_shared, b_forget]
    gs = [g_w_in, g_w_mkv, g_w_out, g_ln_g, g_ln_b, g_pool_w, g_pool_scale, g_w_kv, g_bf]
    ms = [m_w_in, m_w_mem_kv, m_w_out, m_ln_g, m_ln_b, m_pool_w, m_pool_scale, m_w_kv_shared, m_b_forget]
    vs = [v_w_in, v_w_mem_kv, v_w_out, v_ln_g, v_ln_b, v_pool_w, v_pool_scale, v_w_kv_shared, v_b_forget]
    deltas, new_ms, new_vs = [], [], []
    for nm, w, gg, mm, vv in zip(names, ws, gs, ms, vs):
        two_d = (-1, w.shape[-1])
        d, nmm, nvv = _adamw(w.reshape(two_d), gg.reshape(two_d), mm.reshape(two_d), vv.reshape(two_d),
                             name=f"adamw_{nm}")
        deltas.append(d.reshape(w.shape))
        new_ms.append(nmm.reshape(w.shape))
        new_vs.append(nvv.reshape(w.shape))

    return (loss, dx[None], *gs, *deltas, *new_ms, *new_vs)
```

```python
import functools

import jax
import jax.numpy as jnp
from jax import lax
from jax.experimental import pallas as pl
from jax.experimental.pallas import tpu as pltpu

F32 = jnp.float32
BF16 = jnp.bfloat16

D_MODEL = 1024
D_MAIN = 1024
D_MEM = 512
D_MIX = D_MAIN + D_MEM
D_IN = 2 * D_MIX
N_MEM = 256
MEM_HEADS = 4
MEM_HEAD_DIM = 128
FOX_HEADS = 16
FOX_HEAD_DIM = 64
FOX_SCALE = 0.125
POOL_WINDOWS = (2, 4, 8, 16)
POOL_GROUP = 256
POOL_HALO = 16
ALPHA = 4.0 ** 0.25
LN_EPS = 1e-5
LANES = 128
N_CHIPS = 4

ADAM_LR = 0.001
ADAM_B1 = 0.9
ADAM_B2 = 0.999
ADAM_EPS = 1e-08
ADAM_WD = 0.01
ADAM_STEP = 10

VMEM_LIMIT = 56 * 1024 * 1024

ROWS_W_IN = 2 * D_MODEL * (D_IN // N_CHIPS) // 1024
ROWS_W_MKV = 2 * (D_MODEL // N_CHIPS) * 2 * D_MEM // 1024
ROWS_W_OUT = 2 * (D_MIX // N_CHIPS) * D_MODEL // 1024
ROWS_POOL_W = 4 * (POOL_GROUP // N_CHIPS) * POOL_GROUP // 1024
KV_COLS = 2 * D_MAIN + FOX_HEADS
KV_SHARD = KV_COLS // N_CHIPS
ROWS_W_KV = 528
ROWS_SMALL = 16
OFF_W_IN = 0
OFF_W_MKV = OFF_W_IN + ROWS_W_IN
OFF_W_OUT = OFF_W_MKV + ROWS_W_MKV
OFF_POOL_W = OFF_W_OUT + ROWS_W_OUT
OFF_W_KV = OFF_POOL_W + ROWS_POOL_W
OFF_POOL_S = OFF_W_KV + ROWS_W_KV
OFF_LN_G = OFF_POOL_S + ROWS_SMALL
OFF_LN_B = OFF_LN_G + ROWS_SMALL
OFF_BF = OFF_LN_B + ROWS_SMALL
PACK_ROWS = 3584
HALF_ROWS = PACK_ROWS // 2
PACK_TILE = 256
ROWS_PAD = PACK_ROWS - (OFF_BF + ROWS_SMALL)


def _dot(a, b):
    return jnp.dot(a, b, preferred_element_type=F32)


def _dot_nt(a, b):
    return lax.dot_general(a, b, (((1,), (1,)), ((), ())), preferred_element_type=F32)


def _dot_tn(a, b):
    return lax.dot_general(a, b, (((0,), (0,)), ((), ())), preferred_element_type=F32)


def _params(n_axes=1):
    return pltpu.CompilerParams(dimension_semantics=("arbitrary",) * n_axes,
                                vmem_limit_bytes=VMEM_LIMIT)


def _const_spec(shape):
    zeros = (0,) * len(shape)
    return pl.BlockSpec(shape, lambda *_: zeros, pipeline_mode=pl.Buffered(1))


def _split3(x):
    hi = x.astype(BF16)
    r = x - hi.astype(F32)
    mid = r.astype(BF16)
    lo = (r - mid.astype(F32)).astype(BF16)
    return hi, mid, lo


def _linear_fwd(x, w, *, tm, name):
    S, K = x.shape
    N = w.shape[1]
    nc = 512 if N % 512 == 0 else N

    def body(x_ref, w_ref, o_ref):
        xb = x_ref[...].astype(BF16)
        for n0 in range(0, N, nc):
            o_ref[:, n0:n0 + nc] = _dot(xb, w_ref[:, n0:n0 + nc]).astype(BF16)

    return pl.pallas_call(
        body, name=name, grid=(S // tm,),
        in_specs=[pl.BlockSpec((tm, K), lambda i: (i, 0)), _const_spec((K, N))],
        out_specs=pl.BlockSpec((tm, N), lambda i: (i, 0)),
        out_shape=jax.ShapeDtypeStruct((S, N), BF16),
        compiler_params=_params(),
    )(x, w)


def _kv_proj(x1, wk, wv, wf, bf_row, *, tm):
    S = x1.shape[0]

    def body(x_ref, wk_ref, wv_ref, wf_ref, b_ref, k_ref, v_ref, fl_ref, cum_ref, carry_ref):
        i = pl.program_id(0)

        @pl.when(i == 0)
        def _():
            carry_ref[...] = jnp.zeros_like(carry_ref)

        xb = x_ref[...].astype(BF16)
        for n0 in range(0, D_MAIN, 512):
            k_ref[:, n0:n0 + 512] = _dot(xb, wk_ref[:, n0:n0 + 512]).astype(BF16)
            v_ref[:, n0:n0 + 512] = _dot(xb, wv_ref[:, n0:n0 + 512]).astype(BF16)
        fl = _dot(xb, wf_ref[...]) + b_ref[...]
        fl_ref[...] = fl
        log_f = jnp.minimum(fl, 0.0) - jnp.log1p(jnp.exp(-jnp.abs(fl)))
        r = lax.broadcasted_iota(jnp.int32, (tm, tm), 0)
        c = lax.broadcasted_iota(jnp.int32, (tm, tm), 1)
        tri = jnp.where(c <= r, 1.0, 0.0).astype(BF16)
        hi, mid, lo = _split3(log_f)
        cum = (_dot(tri, hi) + _dot(tri, mid)) + _dot(tri, lo) + carry_ref[0:1, :]
        cum_ref[...] = cum
        carry_ref[0:1, :] = cum[tm - 1:tm, :]

    return pl.pallas_call(
        body, name="kv_proj", grid=(S // tm,),
        in_specs=[pl.BlockSpec((tm, D_MODEL), lambda i: (i, 0)),
                  _const_spec((D_MODEL, D_MAIN)), _const_spec((D_MODEL, D_MAIN)),
                  _const_spec((D_MODEL, LANES)), _const_spec((1, LANES))],
        out_specs=[pl.BlockSpec((tm, D_MAIN), lambda i: (i, 0)),
                   pl.BlockSpec((tm, D_MAIN), lambda i: (i, 0)),
                   pl.BlockSpec((tm, LANES), lambda i: (i, 0)),
                   pl.BlockSpec((tm, LANES), lambda i: (i, 0))],
        out_shape=[jax.ShapeDtypeStruct((S, D_MAIN), BF16), jax.ShapeDtypeStruct((S, D_MAIN), BF16),
                   jax.ShapeDtypeStruct((S, LANES), F32), jax.ShapeDtypeStruct((S, LANES), F32)],
        scratch_shapes=[pltpu.VMEM((8, LANES), F32)],
        compiler_params=_params(),
    )(x1, wk, wv, wf, bf_row)


def _gate_bwd(dq_aug, dck, fl, *, tm):
    S = fl.shape[0]
    n = S // tm

    def body(dq_ref, dck_ref, fl_ref, du_ref, df_ref, db_ref, carry_ref):
        i = pl.program_id(0)

        @pl.when(i == 0)
        def _():
            carry_ref[...] = jnp.zeros_like(carry_ref)
            db_ref[...] = jnp.zeros_like(db_ref)

        lane = lax.broadcasted_iota(jnp.int32, (tm, LANES), 1)
        half0 = lane < FOX_HEAD_DIM
        dcq = jnp.zeros((tm, LANES), F32)
        for hp in range(FOX_HEADS // 2):
            b0 = dq_ref[:, 2 * hp * LANES:(2 * hp + 1) * LANES]
            b1 = dq_ref[:, (2 * hp + 1) * LANES:(2 * hp + 2) * LANES]
            du_ref[:, hp * LANES:(hp + 1) * LANES] = (
                jnp.where(half0, b0, pltpu.roll(b1, FOX_HEAD_DIM, 1)) * FOX_SCALE).astype(BF16)
            r0 = jnp.sum(jnp.where(lane == AUG_A, b0, 0.0), axis=1, keepdims=True)
            r1 = jnp.sum(jnp.where(lane == AUG_A, b1, 0.0), axis=1, keepdims=True)
            dcq = dcq + jnp.where(lane == 2 * hp, r0, 0.0) + jnp.where(lane == 2 * hp + 1, r1, 0.0)
        dcum = dcq - dck_ref[...]
        r = lax.broadcasted_iota(jnp.int32, (tm, tm), 0)
        c = lax.broadcasted_iota(jnp.int32, (tm, tm), 1)
        tri = jnp.where(c >= r, 1.0, 0.0).astype(BF16)
        hi, mid, lo = _split3(dcum)
        rev = (_dot(tri, hi) + _dot(tri, mid)) + _dot(tri, lo) + carry_ref[0:1, :]
        carry_ref[0:1, :] = rev[0:1, :]
        fl_v = fl_ref[...]
        df = rev * (1.0 / (1.0 + jnp.exp(fl_v)))
        df_ref[...] = df
        db_ref[...] += jnp.sum(df, axis=0, keepdims=True)

    return pl.pallas_call(
        body, name="gate_bwd", grid=(n,),
        in_specs=[pl.BlockSpec((tm, AUG_W), lambda i: (n - 1 - i, 0)),
                  pl.BlockSpec((tm, LANES), lambda i: (n - 1 - i, 0)),
                  pl.BlockSpec((tm, LANES), lambda i: (n - 1 - i, 0))],
        out_specs=[pl.BlockSpec((tm, D_MAIN), lambda i: (n - 1 - i, 0)),
                   pl.BlockSpec((tm, LANES), lambda i: (n - 1 - i, 0)),
                   pl.BlockSpec((1, LANES), lambda i: (0, 0))],
        out_shape=[jax.ShapeDtypeStruct((S, D_MAIN), BF16),
                   jax.ShapeDtypeStruct((S, LANES), F32), jax.ShapeDtypeStruct((1, LANES), F32)],
        scratch_shapes=[pltpu.VMEM((8, LANES), F32)],
        compiler_params=_params(),
    )(dq_aug, dck, fl)


def _silu_and_grad(g):
    sg = 1.0 / (1.0 + jnp.exp(-g))
    return g * sg, sg * (1.0 + g * (1.0 - sg))


def _mix(kind, mode, *, h, xres=None, mkv, w_out, ln_g, ln_b=None, pool_w=None, pool_scale=None,
         ymain=None, target=None, z=None, dy=None, tm):
    S = h.shape[0]
    n = S // tm
    pool = kind == "pool"
    bwd = mode == "bwd"
    loss_head = bwd and not pool
    rev = pool and bwd
    mem_scale = MEM_HEAD_DIM ** -0.5

    def t_of(i):
        return (n - 1 - i) if rev else i

    row = lambda i: (t_of(i), 0)
    names, arrays, specs = [], [], []

    def add(name, arr, spec):
        names.append(name)
        arrays.append(arr)
        specs.append(spec)

    add("h", h, pl.BlockSpec((tm, D_IN), row))
    if pool:
        hb = tm // POOL_HALO
        add("halo", h, pl.BlockSpec((POOL_HALO, D_MAIN), lambda i: (jnp.maximum(t_of(i) * hb - 1, 0), 0)))
        add("pool_w", pool_w, _const_spec((4, POOL_GROUP, POOL_GROUP)))
        add("pool_scale", pool_scale, _const_spec((1, D_MAIN)))
    else:
        add("ymain", ymain, pl.BlockSpec((tm, D_MAIN), row))
    add("mkv", mkv, _const_spec((N_MEM, 2 * D_MEM)))
    add("w_out", w_out, _const_spec((D_MIX, D_MODEL)))
    add("ln_g", ln_g, _const_spec((1, D_MODEL)))
    if not (pool and bwd):
        add("xres", xres, pl.BlockSpec((tm, D_MODEL), row))
        add("ln_b", ln_b, _const_spec((1, D_MODEL)))
    if loss_head:
        add("target", target, pl.BlockSpec((tm, D_MODEL), row))
    if pool and bwd:
        add("z", z, pl.BlockSpec((tm, D_MODEL), row))
        add("dy", dy, pl.BlockSpec((tm, D_MODEL), row))

    onames, oshapes, ospecs = [], [], []

    def add_out(name, shape, dtype, spec):
        onames.append(name)
        oshapes.append(jax.ShapeDtypeStruct(shape, dtype))
        ospecs.append(spec)

    const2 = lambda i: (0, 0)
    if not bwd:
        add_out("z", (S, D_MODEL), F32, pl.BlockSpec((tm, D_MODEL), row))
        add_out("xout", (S, D_MODEL), F32, pl.BlockSpec((tm, D_MODEL), row))
    else:
        add_out("dz", (S, D_MODEL), F32, pl.BlockSpec((tm, D_MODEL), row))
        add_out("dmain", (S, D_MAIN), BF16, pl.BlockSpec((tm, D_MAIN), row))
        add_out("drest", (S, D_IN - D_MAIN), BF16, pl.BlockSpec((tm, D_IN - D_MAIN), row))
        add_out("dw_out", (D_MIX, D_MODEL), F32, pl.BlockSpec((D_MIX, D_MODEL), const2))
        add_out("dmkv", (N_MEM, 2 * D_MEM), F32, pl.BlockSpec((N_MEM, 2 * D_MEM), const2))
        add_out("dln_g", (1, D_MODEL), F32, pl.BlockSpec((1, D_MODEL), const2))
        add_out("dln_b", (1, D_MODEL), F32, pl.BlockSpec((1, D_MODEL), const2))
        if pool:
            add_out("dpool_w", (4, POOL_GROUP, POOL_GROUP), F32,
                    pl.BlockSpec((4, POOL_GROUP, POOL_GROUP), lambda i: (0, 0, 0)))
            add_out("dpool_scale", (1, D_MAIN), F32, pl.BlockSpec((1, D_MAIN), const2))
        else:
            add_out("loss", (1, D_MODEL), F32, pl.BlockSpec((1, D_MODEL), const2))
            add_out("dcol", (S, LANES), F32, pl.BlockSpec((tm, LANES), row))

    scratch = [pltpu.VMEM((tm, D_MIX), BF16),
               pltpu.VMEM((tm, D_MEM), F32)]
    if pool:
        scratch.append(pltpu.VMEM((tm + 2 * POOL_HALO, D_MAIN), F32))
    if rev:
        scratch.append(pltpu.VMEM((POOL_HALO, D_MAIN), F32))
    n_in, n_out = len(names), len(onames)

    def body(*refs):
        R = dict(zip(names, refs[:n_in]))
        O = dict(zip(onames, refs[n_in:n_in + n_out]))
        sc = refs[n_in + n_out:]
        yc_ref, ymem_ref = sc[0], sc[1]
        ext_ref = sc[2] if pool else None
        carry_ref = sc[3] if rev else None
        i = pl.program_id(0)
        t = t_of(i)
        h_ref = R["h"]
        gamma = R["ln_g"][...]

        if bwd:
            @pl.when(i == 0)
            def _():
                for nm in ("dw_out", "dmkv", "dln_g", "dln_b", "dpool_w", "dpool_scale", "loss"):
                    if nm in O:
                        O[nm][...] = jnp.zeros_like(O[nm])
                if rev:
                    carry_ref[...] = jnp.zeros_like(carry_ref)

        if pool:
            u = h_ref[:, 0:D_MAIN].astype(F32)
            halo = R["halo"][...].astype(F32)
            ext_ref[0:POOL_HALO, :] = jnp.where(t > 0, halo, 0.0)
            ext_ref[POOL_HALO:POOL_HALO + tm, :] = u
            tpos = t * tm + lax.broadcasted_iota(jnp.int32, (tm, 1), 0)
            pms, invcs = [], []
            for gi, w in enumerate(POOL_WINDOWS):
                cs = slice(gi * POOL_GROUP, (gi + 1) * POOL_GROUP)
                acc = ext_ref[POOL_HALO:POOL_HALO + tm, cs]
                for k in range(1, w):
                    acc = acc + ext_ref[POOL_HALO - k:POOL_HALO - k + tm, cs]
                invc = 1.0 / jnp.minimum(tpos + 1, w).astype(F32)
                pm = (acc * invc - u[:, cs]).astype(BF16)
                pms.append(pm)
                invcs.append(invc)
            mixed = [_dot(pms[gi], R["pool_w"][gi]) for gi in range(4)]
            ps = R["pool_scale"][...]
            y_main = [mixed[gi] * ps[:, gi * POOL_GROUP:(gi + 1) * POOL_GROUP] for gi in range(4)]
        else:
            y_main = [R["ymain"][:, gi * 256:(gi + 1) * 256].astype(F32) for gi in range(4)]

        probs = []
        for hd in range(MEM_HEADS):
            sl = slice(D_MAIN + hd * MEM_HEAD_DIM, D_MAIN + (hd + 1) * MEM_HEAD_DIM)
            ksl = slice(hd * MEM_HEAD_DIM, (hd + 1) * MEM_HEAD_DIM)
            vsl = slice(D_MEM + hd * MEM_HEAD_DIM, D_MEM + (hd + 1) * MEM_HEAD_DIM)
            s = _dot_nt(h_ref[:, sl], R["mkv"][:, ksl]) * mem_scale
            e = jnp.exp(s - jnp.max(s, axis=1, keepdims=True))
            p = e / jnp.sum(e, axis=1, keepdims=True)
            probs.append(p)
            ymem_ref[:, ksl] = _dot(p.astype(BF16), R["mkv"][:, vsl])

        g_off = D_MIX
        gate_d = []
        for gi in range(4):
            cs = slice(gi * 256, (gi + 1) * 256)
            gm = h_ref[:, g_off + gi * 256:g_off + (gi + 1) * 256].astype(F32)
            sv, sd = _silu_and_grad(gm)
            yc_ref[:, cs] = (y_main[gi] * sv).astype(BF16)
            gate_d.append((sv, sd))
        gq = h_ref[:, g_off + D_MAIN:D_IN].astype(F32)
        svq, sdq = _silu_and_grad(gq)
        yc_ref[:, D_MAIN:D_MIX] = (ymem_ref[...] * svq).astype(BF16)

        if pool and bwd:
            zt = R["z"][...]
        else:
            o = _dot(yc_ref[...], R["w_out"][...])
            zt = ALPHA * R["xres"][...] + o
        mu = jnp.mean(zt, axis=1, keepdims=True)
        zc = zt - mu
        var = jnp.mean(zc * zc, axis=1, keepdims=True)
        rstd = lax.rsqrt(var + LN_EPS)
        xhat = zc * rstd
        if not bwd:
            O["z"][...] = zt
            O["xout"][...] = xhat * gamma + R["ln_b"][...]
            return

        if loss_head:
            xo = xhat * gamma + R["ln_b"][...]
            err = xo - R["target"][...]
            O["loss"][...] += jnp.sum(err * err, axis=0, keepdims=True)
            dyt = err * (1.0 / D_MODEL)
        else:
            dyt = R["dy"][...]

        O["dln_g"][...] += jnp.sum(dyt * xhat, axis=0, keepdims=True)
        O["dln_b"][...] += jnp.sum(dyt, axis=0, keepdims=True)
        gdy = dyt * gamma
        m1 = jnp.mean(gdy, axis=1, keepdims=True)
        m2 = jnp.mean(gdy * xhat, axis=1, keepdims=True)
        dz = rstd * (gdy - m1 - xhat * m2)
        O["dz"][...] = dz
        dzb = dz.astype(BF16)

        for n0 in range(0, D_MIX, 512):
            O["dw_out"][n0:n0 + 512, :] += _dot_tn(yc_ref[:, n0:n0 + 512], dzb)
        dyc_mem = _dot_nt(dzb, R["w_out"][D_MAIN:D_MIX, :])

        O["drest"][:, D_MEM + D_MAIN:D_MEM + D_MAIN + D_MEM] = (dyc_mem * ymem_ref[...] * sdq).astype(BF16)
        dymem = dyc_mem * svq
        for hd in range(MEM_HEADS):
            sl = slice(D_MAIN + hd * MEM_HEAD_DIM, D_MAIN + (hd + 1) * MEM_HEAD_DIM)
            ksl = slice(hd * MEM_HEAD_DIM, (hd + 1) * MEM_HEAD_DIM)
            vsl = slice(D_MEM + hd * MEM_HEAD_DIM, D_MEM + (hd + 1) * MEM_HEAD_DIM)
            p = probs[hd]
            dyb = dymem[:, ksl].astype(BF16)
            dp = _dot_nt(dyb, R["mkv"][:, vsl])
            ds = p * (dp - jnp.sum(dp * p, axis=1, keepdims=True)) * mem_scale
            dsb = ds.astype(BF16)
            O["drest"][:, ksl] = _dot(dsb, R["mkv"][:, ksl]).astype(BF16)
            O["dmkv"][:, ksl] += _dot_tn(dsb, h_ref[:, sl])
            O["dmkv"][:, vsl] += _dot_tn(p.astype(BF16), dyb)

        dmain = []
        for gi in range(4):
            cs = slice(gi * 256, (gi + 1) * 256)
            dyc_g = _dot_nt(dzb, R["w_out"][cs, :])
            sv, sd = gate_d[gi]
            O["drest"][:, D_MEM + gi * 256:D_MEM + (gi + 1) * 256] = (dyc_g * y_main[gi] * sd).astype(BF16)
            dmain.append(dyc_g * sv)

        if not pool:
            prod = []
            for gi in range(4):
                cs = slice(gi * 256, (gi + 1) * 256)
                db16 = dmain[gi].astype(BF16)
                O["dmain"][:, cs] = db16
                prod.append(db16.astype(F32) * R["ymain"][:, cs].astype(F32))
            dcol = jnp.zeros((tm, LANES), F32)
            for gi in range(4):
                dr = lax.broadcasted_iota(jnp.int32, (256, LANES), 0)
                hc = lax.broadcasted_iota(jnp.int32, (256, LANES), 1)
                sel = jnp.where(jnp.right_shift(dr, 6) + gi * 4 == hc, 1.0, 0.0).astype(BF16)
                hi, mid, lo = _split3(prod[gi])
                dcol = dcol + ((_dot(hi, sel) + _dot(mid, sel)) + _dot(lo, sel))
            O["dcol"][...] = dcol
            return

        ps = R["pool_scale"][...]
        dpm_list = []
        for gi in range(4):
            cs = slice(gi * 256, (gi + 1) * 256)
            O["dpool_scale"][:, cs] += jnp.sum(dmain[gi] * mixed[gi], axis=0, keepdims=True)
            dmix = (dmain[gi] * ps[:, cs]).astype(BF16)
            O["dpool_w"][gi] += _dot_tn(pms[gi], dmix)
            dpm = _dot_nt(dmix, R["pool_w"][gi])
            dpm_list.append(dpm)
            ext_ref[0:tm, cs] = dpm * invcs[gi]
        ext_ref[tm:tm + POOL_HALO, :] = carry_ref[...]
        carry_ref[...] = ext_ref[0:POOL_HALO, :]
        for gi, w in enumerate(POOL_WINDOWS):
            cs = slice(gi * 256, (gi + 1) * 256)
            acc = ext_ref[0:tm, cs]
            for k in range(1, w):
                acc = acc + ext_ref[k:k + tm, cs]
            O["dmain"][:, cs] = (acc - dpm_list[gi]).astype(BF16)

    outs = pl.pallas_call(
        body, name=f"mix_{kind}_{mode}", grid=(n,),
        in_specs=specs, out_specs=ospecs, out_shape=oshapes,
        scratch_shapes=scratch, compiler_params=_params(),
    )(*arrays)
    return dict(zip(onames, outs))


def _lin_bwd(xin, dhs, ws, res, *, tm, name):
    S, K = xin.shape
    nj = len(dhs)
    nr = len(res)
    widths = [w.shape[1] for w in ws]
    scales = [s for _, s in res]

    def body(*refs):
        x_ref = refs[0]
        dh_refs = refs[1:1 + nj]
        w_refs = refs[1 + nj:1 + 2 * nj]
        r_refs = refs[1 + 2 * nj:1 + 2 * nj + nr]
        dx_ref = refs[1 + 2 * nj + nr]
        dw_refs = refs[2 + 2 * nj + nr:]
        i = pl.program_id(0)

        @pl.when(i == 0)
        def _():
            for dw in dw_refs:
                dw[...] = jnp.zeros_like(dw)

        xb = x_ref[...].astype(BF16)
        dx = jnp.zeros((tm, K), F32)
        for r_ref, sc in zip(r_refs, scales):
            dx = dx + sc * r_ref[...]
        for j in range(nj):
            N = widths[j]
            nc = 512 if N % 512 == 0 else N
            for n0 in range(0, N, nc):
                dhb = dh_refs[j][:, n0:n0 + nc].astype(BF16)
                dx = dx + _dot_nt(dhb, w_refs[j][:, n0:n0 + nc])
                dw_refs[j][:, n0:n0 + nc] += _dot_tn(xb, dhb)
        dx_ref[...] = dx

    in_specs = [pl.BlockSpec((tm, K), lambda i: (i, 0))]
    in_specs += [pl.BlockSpec((tm, N), lambda i: (i, 0)) for N in widths]
    in_specs += [_const_spec((K, N)) for N in widths]
    in_specs += [pl.BlockSpec((tm, K), lambda i: (i, 0)) for _ in res]
    out_specs = [pl.BlockSpec((tm, K), lambda i: (i, 0))]
    out_specs += [pl.BlockSpec((K, N), lambda i: (0, 0)) for N in widths]
    out_shape = [jax.ShapeDtypeStruct((S, K), F32)]
    out_shape += [jax.ShapeDtypeStruct((K, N), F32) for N in widths]
    outs = pl.pallas_call(
        body, name=name, grid=(S // tm,),
        in_specs=in_specs, out_specs=out_specs, out_shape=out_shape,
        compiler_params=_params(),
    )(xin, *dhs, *ws, *[r for r, _ in res])
    return outs[0], list(outs[1:])


def _wgrad(xin, dh, *, name):
    M, K = xin.shape
    N = dh.shape[1]

    def body(x_ref, dh_ref, o_ref):
        o_ref[...] = _dot_tn(x_ref[...].astype(BF16), dh_ref[...].astype(BF16))

    return pl.pallas_call(
        body, name=name, out_shape=jax.ShapeDtypeStruct((K, N), F32),
        compiler_params=pltpu.CompilerParams(vmem_limit_bytes=VMEM_LIMIT),
    )(xin, dh)


AUG_A = FOX_HEAD_DIM
AUG_B = FOX_HEAD_DIM + 3
AUG_W = FOX_HEADS * LANES


def _placement(val_lane, ones_lane):
    r = jnp.arange(512)[:, None]
    c = jnp.arange(AUG_W)[None, :]
    head, lane = c // LANES, c % LANES
    m = jnp.zeros((512, AUG_W), jnp.bool_)
    if val_lane is not None:
        for part in range(3):
            m = m | ((r == part * LANES + head) & (lane == val_lane + part))
    if ones_lane is not None:
        m = m | ((r == 3 * LANES) & (lane >= ones_lane) & (lane < ones_lane + 3))
    return m.astype(BF16)


def _augment(x, cols, signs, *, val_lane, ones_lane, scale, tm, name):
    S = x.shape[0]
    place = _placement(val_lane if cols else None, ones_lane)
    nc = len(cols)

    def body(*refs):
        x_ref = refs[0]
        col_refs = refs[1:1 + nc]
        p_ref = refs[1 + nc]
        o_ref = refs[2 + nc]
        lane = lax.broadcasted_iota(jnp.int32, (tm, LANES), 1)
        data = lane < FOX_HEAD_DIM
        one_col = jnp.where(lane == 0, 1.0, 0.0).astype(BF16)
        if nc:
            val = signs[0] * col_refs[0][...]
            for i in range(1, nc):
                val = val + signs[i] * col_refs[i][...]
            hi, mid, lo = _split3(val)
        else:
            hi = mid = lo = jnp.zeros((tm, LANES), BF16)
        lhs = jnp.concatenate([hi, mid, lo, one_col], axis=1)
        for hp in range(FOX_HEADS // 2):
            extra = _dot(lhs, p_ref[:, 2 * hp * LANES:(2 * hp + 2) * LANES])
            blk = x_ref[:, hp * LANES:(hp + 1) * LANES].astype(F32) * scale
            o_ref[:, 2 * hp * LANES:(2 * hp + 1) * LANES] = jnp.where(data, blk, extra[:, 0:LANES]).astype(BF16)
            o_ref[:, (2 * hp + 1) * LANES:(2 * hp + 2) * LANES] = jnp.where(
                data, pltpu.roll(blk, FOX_HEAD_DIM, 1), extra[:, LANES:2 * LANES]).astype(BF16)

    in_specs = [pl.BlockSpec((tm, D_MAIN), lambda i: (i, 0))]
    in_specs += [pl.BlockSpec((tm, LANES), lambda i: (i, 0)) for _ in cols]
    in_specs += [_const_spec((512, AUG_W))]
    return pl.pallas_call(
        body, name=name, grid=(S // tm,),
        in_specs=in_specs, out_specs=pl.BlockSpec((tm, AUG_W), lambda i: (i, 0)),
        out_shape=jax.ShapeDtypeStruct((S, AUG_W), BF16),
        compiler_params=_params(),
    )(x, *cols, place)


def _cols_of_rows(rows, S):
    a = rows[:, :, 0:2, :].transpose(0, 2, 1, 3).reshape(FOX_HEADS, S).T
    return jnp.pad(a, ((0, 0), (0, LANES - FOX_HEADS)))


def _fox_fwd(qf, ka, va, *, tq):
    S = ka.shape[0]
    nq = S // tq
    tk = tq

    def body(q_ref, k_ref, v_ref, o_ref, lse_ref):
        qi = pl.program_id(1)
        lane = lax.broadcasted_iota(jnp.int32, (tq, LANES), 1)
        half0 = lane < FOX_HEAD_DIM
        rr = lax.broadcasted_iota(jnp.int32, (tq, tk), 0)
        cc = lax.broadcasted_iota(jnp.int32, (tq, tk), 1)
        sls = [slice(hh * LANES, (hh + 1) * LANES) for hh in range(2)]
        qs = [q_ref[:, sl] for sl in sls]

        def chunk(ki, carry, masked):
            k0 = pl.multiple_of(ki * tk, tk)
            out = []
            for hh in range(2):
                m, acc = carry[hh]
                s = _dot_nt(qs[hh], k_ref[pl.ds(k0, tk), sls[hh]])
                if masked:
                    s = jnp.where(cc <= rr, s, -jnp.inf)
                m_new = jnp.maximum(m, jnp.max(s, axis=1, keepdims=True))
                p = jnp.exp(s - m_new)
                acc = jnp.exp(m - m_new) * acc + _dot(p.astype(BF16), v_ref[pl.ds(k0, tk), sls[hh]])
                out.append((m_new, acc))
            return tuple(out)

        init = tuple((jnp.full((tq, 1), -jnp.inf, F32), jnp.zeros((tq, LANES), F32)) for _ in range(2))
        carry = lax.fori_loop(0, qi, lambda ki, c: chunk(ki, c, False), init)
        results = []
        for m, acc in chunk(qi, carry, True):
            l = jnp.sum(jnp.where(lane == AUG_A, acc, 0.0), axis=1, keepdims=True)
            results.append((acc / l, m + jnp.log(l)))
        o_ref[...] = jnp.where(half0, results[0][0], pltpu.roll(results[1][0], FOX_HEAD_DIM, 1))
        lse_cols = jnp.where(lane == 0, results[0][1], 0.0) + jnp.where(lane == 1, results[1][1], 0.0)
        lse_ref[0, 0] = lse_cols.T[0:8, :]

    return pl.pallas_call(
        body, name="fox_fwd", grid=(8, nq),
        in_specs=[pl.BlockSpec((tq, 2 * LANES), lambda hp, qi: (qi, hp)),
                  pl.BlockSpec((S, 2 * LANES), lambda hp, qi: (0, hp)),
                  pl.BlockSpec((S, 2 * LANES), lambda hp, qi: (0, hp))],
        out_specs=[pl.BlockSpec((tq, LANES), lambda hp, qi: (qi, hp)),
                   pl.BlockSpec((1, 1, 8, tq), lambda hp, qi: (hp, qi, 0, 0))],
        out_shape=[jax.ShapeDtypeStruct((S, D_MAIN), F32),
                   jax.ShapeDtypeStruct((8, nq, 8, tq), F32)],
        compiler_params=_params(2),
    )(qf, ka, va)


def _fox_bwd(qb, ka, va, dob, *, tq):
    S = ka.shape[0]
    nq = S // tq
    tk = tq

    def body(k_ref, v_ref, q_ref, do_ref, dq_ref, dk_ref, dv_ref, dck_ref):
        kj = pl.program_id(1)

        @pl.when(kj == 0)
        def _():
            dq_ref[...] = jnp.zeros_like(dq_ref)

        lane = lax.broadcasted_iota(jnp.int32, (tk, LANES), 1)
        half0 = lane < FOX_HEAD_DIM
        rr = lax.broadcasted_iota(jnp.int32, (tk, tq), 0)
        cc = lax.broadcasted_iota(jnp.int32, (tk, tq), 1)
        sls = [slice(hh * LANES, (hh + 1) * LANES) for hh in range(2)]
        kts = [k_ref[:, sl] for sl in sls]
        vts = [v_ref[:, sl] for sl in sls]

        def chunk(qi, carry, masked):
            q0 = pl.multiple_of(qi * tq, tq)
            out = []
            for hh in range(2):
                dk_acc, dv_acc = carry[hh]
                qc = q_ref[pl.ds(q0, tq), sls[hh]]
                doc = do_ref[pl.ds(q0, tq), sls[hh]]
                pt = jnp.exp(_dot_nt(kts[hh], qc))
                if masked:
                    pt = jnp.where(rr <= cc, pt, 0.0)
                dsb = (pt * _dot_nt(vts[hh], doc)).astype(BF16)
                dv_acc = dv_acc + _dot(pt.astype(BF16), doc)
                dk_acc = dk_acc + _dot(dsb, qc)
                dq_ref[pl.ds(q0, tq), sls[hh]] += _dot_tn(dsb, kts[hh])
                out.append((dk_acc, dv_acc))
            return tuple(out)

        init = tuple((jnp.zeros((tk, LANES), F32), jnp.zeros((tk, LANES), F32)) for _ in range(2))
        carry = chunk(kj, init, True)
        (dk0, dv0), (dk1, dv1) = lax.fori_loop(kj + 1, nq, lambda qi, c: chunk(qi, c, False), carry)
        dk_ref[...] = jnp.where(half0, dk0, pltpu.roll(dk1, FOX_HEAD_DIM, 1)).astype(BF16)
        dv_ref[...] = jnp.where(half0, dv0, pltpu.roll(dv1, FOX_HEAD_DIM, 1)).astype(BF16)
        c0 = jnp.sum(jnp.where(lane == AUG_B, dk0, 0.0), axis=1, keepdims=True)
        c1 = jnp.sum(jnp.where(lane == AUG_B, dk1, 0.0), axis=1, keepdims=True)
        dck_cols = jnp.where(lane == 0, c0, 0.0) + jnp.where(lane == 1, c1, 0.0)
        dck_ref[0, 0] = dck_cols.T[0:8, :]

    return pl.pallas_call(
        body, name="fox_bwd", grid=(8, nq),
        in_specs=[pl.BlockSpec((tk, 2 * LANES), lambda hp, kj: (kj, hp)),
                  pl.BlockSpec((tk, 2 * LANES), lambda hp, kj: (kj, hp)),
                  pl.BlockSpec((S, 2 * LANES), lambda hp, kj: (0, hp)),
                  pl.BlockSpec((S, 2 * LANES), lambda hp, kj: (0, hp))],
        out_specs=[pl.BlockSpec((S, 2 * LANES), lambda hp, kj: (0, hp)),
                   pl.BlockSpec((tk, LANES), lambda hp, kj: (kj, hp)),
                   pl.BlockSpec((tk, LANES), lambda hp, kj: (kj, hp)),
                   pl.BlockSpec((1, 1, 8, tk), lambda hp, kj: (hp, kj, 0, 0))],
        out_shape=[jax.ShapeDtypeStruct((S, AUG_W), F32),
                   jax.ShapeDtypeStruct((S, D_MAIN), BF16),
                   jax.ShapeDtypeStruct((S, D_MAIN), BF16),
                   jax.ShapeDtypeStruct((8, nq, 8, tk), F32)],
        compiler_params=_params(2),
    )(ka, va, qb, dob)


def _adamw(w, g, m, v, *, name):
    Rr, C = w.shape
    tr = 256 if Rr % 256 == 0 else Rr
    c1 = 1.0 / (1.0 - ADAM_B1 ** ADAM_STEP)
    c2 = 1.0 / (1.0 - ADAM_B2 ** ADAM_STEP)

    def body(w_ref, g_ref, m_ref, v_ref, d_ref, nm_ref, nv_ref):
        gv = g_ref[...]
        nm = ADAM_B1 * m_ref[...] + (1.0 - ADAM_B1) * gv
        nv = ADAM_B2 * v_ref[...] + (1.0 - ADAM_B2) * (gv * gv)
        d_ref[...] = -ADAM_LR * ((nm * c1) / (jnp.sqrt(nv * c2) + ADAM_EPS) + ADAM_WD * w_ref[...])
        nm_ref[...] = nm
        nv_ref[...] = nv

    spec = pl.BlockSpec((tr, C), lambda i: (i, 0))
    sds = jax.ShapeDtypeStruct((Rr, C), F32)
    return pl.pallas_call(
        body, name=name, grid=(Rr // tr,),
        in_specs=[spec] * 4, out_specs=[spec] * 3, out_shape=[sds] * 3,
        compiler_params=_params(),
    )(w, g, m, v)


_ANY = pl.BlockSpec(memory_space=pl.ANY)
_MESH = pl.DeviceIdType.MESH


def _place():
    x, y, c = lax.axis_index("x"), lax.axis_index("y"), lax.axis_index("c")
    return x, y, c


def _all_gather_shards(pack):
    dt = pack.dtype

    def body(p_ref, out_ref, send_sems, recv_sems):
        x, y, c = _place()
        sib = (x, y, 1 - c)
        chips = [(1 - x, y), (x, 1 - y), (1 - x, 1 - y)]
        me = 2 * x + y

        def copy(k, chip_idx, half, to, src=None):
            dst = out_ref.at[chip_idx, half]
            return pltpu.make_async_remote_copy(
                src_ref=dst if src is None else src, dst_ref=dst,
                send_sem=send_sems.at[k], recv_sem=recv_sems.at[k],
                device_id=to, device_id_type=_MESH)

        first = [copy(j, me, c, (*chip, c), src=p_ref.at[c]) for j, chip in enumerate(chips)]
        for cp in first:
            cp.start()
        passed = [copy(3 + j, 2 * chip[0] + chip[1], c, sib) for j, chip in enumerate(chips)]
        for j, chip in enumerate(chips):
            copy(j, 2 * chip[0] + chip[1], c, sib).wait_recv()
            passed[j].start()
        for j, chip in enumerate(chips):
            copy(3 + j, 2 * chip[0] + chip[1], 1 - c, sib).wait_recv()
        for cp in first + passed:
            cp.wait_send()

    return pl.pallas_call(
        body, name="all_gather_shards",
        in_specs=[_ANY], out_specs=_ANY,
        out_shape=jax.ShapeDtypeStruct((N_CHIPS, 2, HALF_ROWS, 1024), dt),
        scratch_shapes=[pltpu.SemaphoreType.DMA((6,)), pltpu.SemaphoreType.DMA((6,))],
    )(pack)


def _send_half_to_sibling(gpack):
    def body(g_ref, out_ref, send_sem, recv_sem):
        x, y, c = _place()
        sib = (x, y, 1 - c)
        cps = [pltpu.make_async_remote_copy(
            src_ref=g_ref.at[j, 1 - c], dst_ref=out_ref.at[j],
            send_sem=send_sem.at[j], recv_sem=recv_sem.at[j],
            device_id=sib, device_id_type=_MESH) for j in range(N_CHIPS)]
        for cp in cps:
            cp.start()
        for cp in cps:
            cp.wait_recv()
        for cp in cps:
            cp.wait_send()

    return pl.pallas_call(
        body, name="pair_send",
        in_specs=[_ANY], out_specs=_ANY,
        out_shape=jax.ShapeDtypeStruct((N_CHIPS, HALF_ROWS, 1024), F32),
        scratch_shapes=[pltpu.SemaphoreType.DMA((N_CHIPS,)), pltpu.SemaphoreType.DMA((N_CHIPS,))],
    )(gpack)


def _pair_sum(gpack, recv, c_arr, *, tr):
    def body(c_ref, a_ref, b_ref, o_ref):
        o_ref[...] = (a_ref[...] + b_ref[...]).astype(BF16)

    grid_spec = pltpu.PrefetchScalarGridSpec(
        num_scalar_prefetch=1, grid=(N_CHIPS, HALF_ROWS // tr),
        in_specs=[pl.BlockSpec((None, None, tr, 1024), lambda j, i, c_ref: (j, c_ref[0], i, 0)),
                  pl.BlockSpec((None, tr, 1024), lambda j, i, c_ref: (j, i, 0))],
        out_specs=pl.BlockSpec((None, tr, 1024), lambda j, i, c_ref: (j, i, 0)))
    return pl.pallas_call(
        body, name="pair_sum", grid_spec=grid_spec,
        out_shape=jax.ShapeDtypeStruct((N_CHIPS, HALF_ROWS, 1024), BF16),
        compiler_params=_params(2),
    )(c_arr, gpack, recv)


def _scatter_pieces(psum):
    def body(p_ref, out_ref, send_sems, recv_sems):
        x, y, c = _place()
        chips = [(1 - x, y), (x, 1 - y), (1 - x, 1 - y)]
        me = 2 * x + y
        cps = []
        for j, chip in enumerate(chips):
            them = 2 * chip[0] + chip[1]
            cps.append(pltpu.make_async_remote_copy(
                src_ref=p_ref.at[them], dst_ref=out_ref.at[me],
                send_sem=send_sems.at[j], recv_sem=recv_sems.at[j],
                device_id=(*chip, c), device_id_type=_MESH))
        for cp in cps:
            cp.start()
        for cp in cps:
            cp.wait_recv()
        for cp in cps:
            cp.wait_send()

    return pl.pallas_call(
        body, name="scatter_pieces",
        in_specs=[_ANY], out_specs=_ANY,
        out_shape=jax.ShapeDtypeStruct((N_CHIPS, HALF_ROWS, 1024), BF16),
        scratch_shapes=[pltpu.SemaphoreType.DMA((3,)), pltpu.SemaphoreType.DMA((3,))],
    )(psum)


def _sum_pieces(pieces, *, tr):
    def body(p_ref, o_ref):
        acc = p_ref[0].astype(F32) + p_ref[1].astype(F32)
        acc = acc + p_ref[2].astype(F32)
        o_ref[...] = acc + p_ref[3].astype(F32)

    return pl.pallas_call(
        body, name="sum_pieces", grid=(HALF_ROWS // tr,),
        in_specs=[pl.BlockSpec((N_CHIPS, tr, 1024), lambda i: (0, i, 0))],
        out_specs=pl.BlockSpec((tr, 1024), lambda i: (i, 0)),
        out_shape=jax.ShapeDtypeStruct((HALF_ROWS, 1024), F32),
        compiler_params=_params(),
    )(pieces)


def _exchange_halves(total):
    def body(t_ref, out_ref, send_sem, recv_sem):
        x, y, c = _place()
        sib = (x, y, 1 - c)
        cp = pltpu.make_async_remote_copy(
            src_ref=t_ref, dst_ref=out_ref.at[c], send_sem=send_sem, recv_sem=recv_sem,
            device_id=sib, device_id_type=_MESH)
        cp.start()
        cp.wait_recv()
        cp.wait_send()

    return pl.pallas_call(
        body, name="exchange_halves",
        in_specs=[_ANY], out_specs=_ANY,
        out_shape=jax.ShapeDtypeStruct((2, HALF_ROWS, 1024), F32),
        scratch_shapes=[pltpu.SemaphoreType.DMA, pltpu.SemaphoreType.DMA],
    )(total)


def _pad_rows(a, rows):
    return jnp.pad(a, ((0, rows - a.shape[0]), (0, 0)))


def _pack_weight_shard(w_in, w_mem_kv, w_out, pool_w, w_kv_shared, pool_scale):
    ps_bits = lax.bitcast_convert_type(pool_scale.reshape(-1), BF16).reshape(1, -1)
    ps_row = jnp.pad(ps_bits, ((0, 0), (0, 1024 - ps_bits.shape[1])))
    parts = [
        w_in.astype(BF16).reshape(ROWS_W_IN, 1024),
        w_mem_kv.astype(BF16).reshape(ROWS_W_MKV, 1024),
        w_out.astype(BF16).reshape(ROWS_W_OUT, 1024),
        pool_w.astype(BF16).reshape(ROWS_POOL_W, 1024),
        _pad_rows(w_kv_shared.astype(BF16).reshape(KV_SHARD, 1024), ROWS_W_KV),
        _pad_rows(ps_row, ROWS_SMALL),
        jnp.zeros((PACK_ROWS - OFF_LN_G, 1024), BF16),
    ]
    return jnp.concatenate(parts, axis=0).reshape(2, HALF_ROWS, 1024)


def _unpack_weights(g):
    w_in = g[:, OFF_W_IN:OFF_W_IN + ROWS_W_IN].reshape(4, 2, D_MODEL, D_IN // 4)
    w_in = w_in.transpose(1, 2, 0, 3).reshape(2, D_MODEL, D_IN)
    w_mkv = g[:, OFF_W_MKV:OFF_W_MKV + ROWS_W_MKV].reshape(4, 2, D_MODEL // 4, 2 * D_MEM)
    w_mkv = w_mkv.transpose(1, 0, 2, 3).reshape(2, D_MODEL, 2 * D_MEM)
    w_out = g[:, OFF_W_OUT:OFF_W_OUT + ROWS_W_OUT].reshape(4, 2, D_MIX // 4, D_MODEL)
    w_out = w_out.transpose(1, 0, 2, 3).reshape(2, D_MIX, D_MODEL)
    pool_w = g[:, OFF_POOL_W:OFF_POOL_W + ROWS_POOL_W].reshape(4, 4, POOL_GROUP // 4, POOL_GROUP)
    pool_w = pool_w.transpose(1, 0, 2, 3).reshape(4, POOL_GROUP, POOL_GROUP)
    w_kv = g[:, OFF_W_KV:OFF_W_KV + KV_SHARD].reshape(4, D_MODEL, KV_SHARD)
    w_kv = w_kv.transpose(1, 0, 2).reshape(D_MODEL, KV_COLS)
    ps_bits = g[:, OFF_POOL_S, 0:512].reshape(4, 256, 2)
    pool_scale = lax.bitcast_convert_type(ps_bits, F32).reshape(1, D_MAIN)
    return w_in, w_mkv, w_out, pool_w, w_kv, pool_scale


def _pack_grads(g_w_in, g_w_mkv, g_w_out, g_pool_w, g_w_kv, g_pool_scale, g_ln_g, g_ln_b, g_bf):
    def rep(a):
        a = _pad_rows(a, ROWS_SMALL)
        return jnp.broadcast_to(a[None], (4,) + a.shape)

    parts = [
        g_w_in.reshape(2, D_MODEL, 4, D_IN // 4).transpose(2, 0, 1, 3).reshape(4, ROWS_W_IN, 1024),
        g_w_mkv.reshape(2, 4, D_MODEL // 4, 2 * D_MEM).transpose(1, 0, 2, 3).reshape(4, ROWS_W_MKV, 1024),
        g_w_out.reshape(2, 4, D_MIX // 4, D_MODEL).transpose(1, 0, 2, 3).reshape(4, ROWS_W_OUT, 1024),
        g_pool_w.reshape(4, 4, POOL_GROUP // 4, POOL_GROUP).transpose(1, 0, 2, 3).reshape(4, ROWS_POOL_W, 1024),
        jnp.pad(g_w_kv.reshape(D_MODEL, 4, KV_SHARD).transpose(1, 0, 2).reshape(4, KV_SHARD, 1024),
                ((0, 0), (0, ROWS_W_KV - KV_SHARD), (0, 0))),
        jnp.pad(g_pool_scale.reshape(4, 1, 256), ((0, 0), (0, ROWS_SMALL - 1), (0, 1024 - 256))),
        rep(g_ln_g), rep(g_ln_b),
        rep(jnp.pad(g_bf.reshape(1, -1), ((0, 0), (0, 1024 - g_bf.shape[0])))),
        jnp.zeros((4, ROWS_PAD, 1024), F32),
    ]
    return jnp.concatenate(parts, axis=1).reshape(4, 2, HALF_ROWS, 1024)


def _local_step(x, mem, target, w_in, w_mkv, w_out, pool_w, pool_scale, w_kv, ln_g, ln_b, b_forget,
                *, tm=256, tq=512):
    S = x.shape[0]
    nq = S // tq
    g_rows = [ln_g[l:l + 1] for l in range(2)]
    b_rows = [ln_b[l:l + 1] for l in range(2)]
    wk, wv = w_kv[:, 0:D_MAIN], w_kv[:, D_MAIN:2 * D_MAIN]
    wf = jnp.pad(w_kv[:, 2 * D_MAIN:], ((0, 0), (0, LANES - FOX_HEADS)))
    bf_row = jnp.pad(b_forget.reshape(1, -1), ((0, 0), (0, LANES - FOX_HEADS)))

    mkv = [_linear_fwd(mem, w_mkv[l], tm=N_MEM, name=f"mem_kv{l}") for l in range(2)]

    h0 = _linear_fwd(x, w_in[0], tm=tm, name="in_proj0")
    f0 = _mix("pool", "fwd", h=h0, xres=x, mkv=mkv[0], w_out=w_out[0], ln_g=g_rows[0], ln_b=b_rows[0],
              pool_w=pool_w, pool_scale=pool_scale, tm=tm)
    z0, x1 = f0["z"], f0["xout"]
    k, v, fl, cum = _kv_proj(x1, wk, wv, wf, bf_row, tm=tm)
    h1 = _linear_fwd(x1, w_in[1], tm=tm, name="in_proj1")
    aug = functools.partial(_augment, tm=tm)
    ka = aug(k, [cum], [-1.0], val_lane=AUG_B, ones_lane=AUG_A, scale=1.0, name="aug_k")
    va = aug(v, [], [], val_lane=None, ones_lane=AUG_A, scale=1.0, name="aug_v")
    qf = aug(h1, [cum], [1.0], val_lane=AUG_A, ones_lane=AUG_B, scale=FOX_SCALE, name="aug_q_fwd")
    ymain1, lse_rows = _fox_fwd(qf, ka, va, tq=tq)

    b1 = _mix("fox", "bwd", h=h1, xres=x1, mkv=mkv[1], w_out=w_out[1], ln_g=g_rows[1], ln_b=b_rows[1],
              ymain=ymain1, target=target, tm=tm)
    qb = aug(h1, [cum, _cols_of_rows(lse_rows, S)], [1.0, -1.0], val_lane=AUG_A, ones_lane=AUG_B,
             scale=FOX_SCALE, name="aug_q_bwd")
    dob = aug(b1["dmain"], [b1["dcol"]], [-1.0], val_lane=AUG_A, ones_lane=None, scale=1.0, name="aug_do")
    dq_aug, dk, dv, dck_rows = _fox_bwd(qb, ka, va, dob, tq=tq)
    du1, df, dbf = _gate_bwd(dq_aug, _cols_of_rows(dck_rows, S), fl, tm=tm)

    dx1a, (dwu1, dwr1) = _lin_bwd(
        x1, [du1, b1["drest"]], [w_in[1][:, 0:D_MAIN], w_in[1][:, D_MAIN:]], [(b1["dz"], ALPHA)],
        tm=tm, name="in_proj1_bwd")
    dx1, (dwk, dwv, dwf) = _lin_bwd(x1, [dk, dv, df], [wk, wv, wf], [(dx1a, 1.0)], tm=tm, name="kv_proj_bwd")

    b0 = _mix("pool", "bwd", h=h0, mkv=mkv[0], w_out=w_out[0], ln_g=g_rows[0],
              pool_w=pool_w, pool_scale=pool_scale, z=z0, dy=dx1, tm=tm)
    dx, (dwu0, dwr0) = _lin_bwd(
        x, [b0["dmain"], b0["drest"]], [w_in[0][:, 0:D_MAIN], w_in[0][:, D_MAIN:]], [(b0["dz"], ALPHA)],
        tm=tm, name="in_proj0_bwd")
    dw_mkv = [_wgrad(mem, b["dmkv"], name=f"mem_kv{l}_bwd") for l, b in enumerate((b0, b1))]

    grads = dict(
        w_in=jnp.stack([jnp.concatenate([dwu0, dwr0], axis=1), jnp.concatenate([dwu1, dwr1], axis=1)]),
        w_mem_kv=jnp.stack(dw_mkv),
        w_out=jnp.stack([b0["dw_out"], b1["dw_out"]]),
        ln_g=jnp.concatenate([b0["dln_g"], b1["dln_g"]], axis=0),
        ln_b=jnp.concatenate([b0["dln_b"], b1["dln_b"]], axis=0),
        pool_w=b0["dpool_w"],
        pool_scale=b0["dpool_scale"],
        w_kv=jnp.concatenate([dwk, dwv, dwf[:, 0:FOX_HEADS]], axis=1),
        b_forget=dbf[0, 0:FOX_HEADS],
    )
    return b1["loss"], dx, grads


def kernel(x, mem, w_in, w_mem_kv, w_out, ln_g, ln_b, pool_w, pool_scale, w_kv_shared, b_forget, loss_target, m_w_in, m_w_mem_kv, m_w_out, m_ln_g, m_ln_b, m_pool_w, m_pool_scale, m_w_kv_shared, m_b_forget, v_w_in, v_w_mem_kv, v_w_out, v_ln_g, v_ln_b, v_pool_w, v_pool_scale, v_w_kv_shared, v_b_forget):
    c_arr = lax.axis_index("c").astype(jnp.int32).reshape(1)

    wpack = _pack_weight_shard(w_in, w_mem_kv, w_out, pool_w, w_kv_shared, pool_scale)
    me = 2 * lax.axis_index("x") + lax.axis_index("y")
    my_c = lax.axis_index("c")
    gathered = lax.dynamic_update_slice(_all_gather_shards(wpack), wpack[None], (me, 0, 0, 0))
    gathered = gathered.reshape(N_CHIPS, PACK_ROWS, 1024)
    fw_in, fw_mkv, fw_out, fpool_w, fw_kv, fpool_scale = _unpack_weights(gathered)

    loss_vec, dx, g = _local_step(x[0], mem[0], loss_target[0], fw_in, fw_mkv, fw_out, fpool_w,
                                  fpool_scale, fw_kv, ln_g, ln_b, b_forget)
    loss = lax.psum(0.5 / D_MODEL * jnp.sum(loss_vec), ("x", "y", "c"))

    gpack = _pack_grads(g["w_in"], g["w_mem_kv"], g["w_out"], g["pool_w"], g["w_kv"], g["pool_scale"],
                        g["ln_g"], g["ln_b"], g["b_forget"])
    from_sibling = _send_half_to_sibling(gpack)
    psum = _pair_sum(gpack, from_sibling, c_arr, tr=PACK_TILE)
    own_piece = lax.dynamic_slice(psum, (me, 0, 0), (1, HALF_ROWS, 1024))
    pieces = lax.dynamic_update_slice(_scatter_pieces(psum), own_piece, (me, 0, 0))
    total = _sum_pieces(pieces, tr=PACK_TILE)
    shard = lax.dynamic_update_slice(_exchange_halves(total), total[None], (my_c, 0, 0))
    shard = shard.reshape(PACK_ROWS, 1024)

    g_w_in = shard[OFF_W_IN:OFF_W_IN + ROWS_W_IN].reshape(w_in.shape)
    g_w_mkv = shard[OFF_W_MKV:OFF_W_MKV + ROWS_W_MKV].reshape(w_mem_kv.shape)
    g_w_out = shard[OFF_W_OUT:OFF_W_OUT + ROWS_W_OUT].reshape(w_out.shape)
    g_pool_w = shard[OFF_POOL_W:OFF_POOL_W + ROWS_POOL_W].reshape(pool_w.shape)
    g_w_kv = shard[OFF_W_KV:OFF_W_KV + KV_SHARD].reshape(w_kv_shared.shape)
    g_pool_scale = shard[OFF_POOL_S:OFF_POOL_S + 1, 0:256].reshape(pool_scale.shape)
    g_ln_g = shard[OFF_LN_G:OFF_LN_G + 2]
    g_ln_b = shard[OFF_LN_B:OFF_LN_B + 2]
    g_bf = shard[OFF_BF, 0:FOX_HEADS]

    names = ["w_in", "w_mem_kv", "w_out", "ln_g", "ln_b", "pool_w", "pool_scale", "w_kv_shared", "b_forget"]
    ws = [w_in, w_mem_kv, w_out, ln_g, ln_b, pool_w, pool_scale, w_kv_shared, b_forget]
    gs = [g_w_in, g_w_mkv, g_w_out, g_ln_g, g_ln_b, g_pool_w, g_pool_scale, g_w_kv, g_bf]
    ms = [m_w_in, m_w_mem_kv, m_w_out, m_ln_g, m_ln_b, m_pool_w, m_pool_scale, m_w_kv_shared, m_b_forget]
    vs = [v_w_in, v_w_mem_kv, v_w_out, v_ln_g, v_ln_b, v_pool_w, v_pool_scale, v_w_kv_shared, v_b_forget]
    deltas, new_ms, new_vs = [], [], []
    for nm, w, gg, mm, vv in zip(names, ws, gs, ms, vs):
        two_d = (-1, w.shape[-1])
        d, nmm, nvv = _adamw(w.reshape(two_d), gg.reshape(two_d), mm.reshape(two_d), vv.reshape(two_d),
                             name=f"adamw_{nm}")
        deltas.append(d.reshape(w.shape))
        new_ms.append(nmm.reshape(w.shape))
        new_vs.append(nvv.reshape(w.shape))

    return (loss, dx[None], *gs, *deltas, *new_ms, *new_vs)
```

```python
import functools

import jax
import jax.numpy as jnp
from jax import lax
from jax.experimental import pallas as pl
from jax.experimental.pallas import tpu as pltpu

F32 = jnp.float32
BF16 = jnp.bfloat16

D_MODEL = 1024
D_MAIN = 1024
D_MEM = 512
D_MIX = D_MAIN + D_MEM
D_IN = 2 * D_MIX
N_MEM = 256
MEM_HEADS = 4
MEM_HEAD_DIM = 128
FOX_HEADS = 16
FOX_HEAD_DIM = 64
FOX_SCALE = 0.125
POOL_WINDOWS = (2, 4, 8, 16)
POOL_GROUP = 256
POOL_HALO = 16
ALPHA = 4.0 ** 0.25
LN_EPS = 1e-5
LANES = 128
N_CHIPS = 4

ADAM_LR = 0.001
ADAM_B1 = 0.9
ADAM_B2 = 0.999
ADAM_EPS = 1e-08
ADAM_WD = 0.01
ADAM_STEP = 10

VMEM_LIMIT = 56 * 1024 * 1024

ROWS_W_IN = 2 * D_MODEL * (D_IN // N_CHIPS) // 1024
ROWS_W_MKV = 2 * (D_MODEL // N_CHIPS) * 2 * D_MEM // 1024
ROWS_W_OUT = 2 * (D_MIX // N_CHIPS) * D_MODEL // 1024
ROWS_POOL_W = 4 * (POOL_GROUP // N_CHIPS) * POOL_GROUP // 1024
KV_COLS = 2 * D_MAIN + FOX_HEADS
KV_SHARD = KV_COLS // N_CHIPS
ROWS_W_KV = 528
ROWS_SMALL = 16
OFF_W_IN = 0
OFF_W_MKV = OFF_W_IN + ROWS_W_IN
OFF_W_OUT = OFF_W_MKV + ROWS_W_MKV
OFF_POOL_W = OFF_W_OUT + ROWS_W_OUT
OFF_W_KV = OFF_POOL_W + ROWS_POOL_W
OFF_POOL_S = OFF_W_KV + ROWS_W_KV
OFF_LN_G = OFF_POOL_S + ROWS_SMALL
OFF_LN_B = OFF_LN_G + ROWS_SMALL
OFF_BF = OFF_LN_B + ROWS_SMALL
PACK_ROWS = 3584
HALF_ROWS = PACK_ROWS // 2
PACK_TILE = 256
ROWS_PAD = PACK_ROWS - (OFF_BF + ROWS_SMALL)


def _dot(a, b):
    return jnp.dot(a, b, preferred_element_type=F32)


def _dot_nt(a, b):
    return lax.dot_general(a, b, (((1,), (1,)), ((), ())), preferred_element_type=F32)


def _dot_tn(a, b):
    return lax.dot_general(a, b, (((0,), (0,)), ((), ())), preferred_element_type=F32)


def _params(n_axes=1):
    return pltpu.CompilerParams(dimension_semantics=("arbitrary",) * n_axes,
                                vmem_limit_bytes=VMEM_LIMIT)


def _const_spec(shape):
    zeros = (0,) * len(shape)
    return pl.BlockSpec(shape, lambda *_: zeros, pipeline_mode=pl.Buffered(1))


def _split3(x):
    hi = x.astype(BF16)
    r = x - hi.astype(F32)
    mid = r.astype(BF16)
    lo = (r - mid.astype(F32)).astype(BF16)
    return hi, mid, lo


def _linear_fwd(x, w, *, tm, name):
    S, K = x.shape
    N = w.shape[1]
    nc = 512 if N % 512 == 0 else N

    def body(x_ref, w_ref, o_ref):
        xb = x_ref[...].astype(BF16)
        for n0 in range(0, N, nc):
            o_ref[:, n0:n0 + nc] = _dot(xb, w_ref[:, n0:n0 + nc]).astype(BF16)

    return pl.pallas_call(
        body, name=name, grid=(S // tm,),
        in_specs=[pl.BlockSpec((tm, K), lambda i: (i, 0)), _const_spec((K, N))],
        out_specs=pl.BlockSpec((tm, N), lambda i: (i, 0)),
        out_shape=jax.ShapeDtypeStruct((S, N), BF16),
        compiler_params=_params(),
    )(x, w)


def _kv_proj(x1, wk, wv, wf, bf_row, *, tm):
    S = x1.shape[0]

    def body(x_ref, wk_ref, wv_ref, wf_ref, b_ref, k_ref, v_ref, fl_ref, cum_ref, carry_ref):
        i = pl.program_id(0)

        @pl.when(i == 0)
        def _():
            carry_ref[...] = jnp.zeros_like(carry_ref)

        xb = x_ref[...].astype(BF16)
        for n0 in range(0, D_MAIN, 512):
            k_ref[:, n0:n0 + 512] = _dot(xb, wk_ref[:, n0:n0 + 512]).astype(BF16)
            v_ref[:, n0:n0 + 512] = _dot(xb, wv_ref[:, n0:n0 + 512]).astype(BF16)
        fl = _dot(xb, wf_ref[...]) + b_ref[...]
        fl_ref[...] = fl
        log_f = jnp.minimum(fl, 0.0) - jnp.log1p(jnp.exp(-jnp.abs(fl)))
        r = lax.broadcasted_iota(jnp.int32, (tm, tm), 0)
        c = lax.broadcasted_iota(jnp.int32, (tm, tm), 1)
        tri = jnp.where(c <= r, 1.0, 0.0).astype(BF16)
        hi, mid, lo = _split3(log_f)
        cum = (_dot(tri, hi) + _dot(tri, mid)) + _dot(tri, lo) + carry_ref[0:1, :]
        cum_ref[...] = cum
        carry_ref[0:1, :] = cum[tm - 1:tm, :]

    return pl.pallas_call(
        body, name="kv_proj", grid=(S // tm,),
        in_specs=[pl.BlockSpec((tm, D_MODEL), lambda i: (i, 0)),
                  _const_spec((D_MODEL, D_MAIN)), _const_spec((D_MODEL, D_MAIN)),
                  _const_spec((D_MODEL, LANES)), _const_spec((1, LANES))],
        out_specs=[pl.BlockSpec((tm, D_MAIN), lambda i: (i, 0)),
                   pl.BlockSpec((tm, D_MAIN), lambda i: (i, 0)),
                   pl.BlockSpec((tm, LANES), lambda i: (i, 0)),
                   pl.BlockSpec((tm, LANES), lambda i: (i, 0))],
        out_shape=[jax.ShapeDtypeStruct((S, D_MAIN), BF16), jax.ShapeDtypeStruct((S, D_MAIN), BF16),
                   jax.ShapeDtypeStruct((S, LANES), F32), jax.ShapeDtypeStruct((S, LANES), F32)],
        scratch_shapes=[pltpu.VMEM((8, LANES), F32)],
        compiler_params=_params(),
    )(x1, wk, wv, wf, bf_row)


def _gate_bwd(dq_aug, dck, fl, *, tm):
    S = fl.shape[0]
    n = S // tm

    def body(dq_ref, dck_ref, fl_ref, du_ref, df_ref, db_ref, carry_ref):
        i = pl.program_id(0)

        @pl.when(i == 0)
        def _():
            carry_ref[...] = jnp.zeros_like(carry_ref)
            db_ref[...] = jnp.zeros_like(db_ref)

        lane = lax.broadcasted_iota(jnp.int32, (tm, LANES), 1)
        half0 = lane < FOX_HEAD_DIM
        dcq = jnp.zeros((tm, LANES), F32)
        for hp in range(FOX_HEADS // 2):
            b0 = dq_ref[:, 2 * hp * LANES:(2 * hp + 1) * LANES]
            b1 = dq_ref[:, (2 * hp + 1) * LANES:(2 * hp + 2) * LANES]
            du_ref[:, hp * LANES:(hp + 1) * LANES] = (
                jnp.where(half0, b0, pltpu.roll(b1, FOX_HEAD_DIM, 1)) * FOX_SCALE).astype(BF16)
            r0 = jnp.sum(jnp.where(lane == AUG_A, b0, 0.0), axis=1, keepdims=True)
            r1 = jnp.sum(jnp.where(lane == AUG_A, b1, 0.0), axis=1, keepdims=True)
            dcq = dcq + jnp.where(lane == 2 * hp, r0, 0.0) + jnp.where(lane == 2 * hp + 1, r1, 0.0)
        dcum = dcq - dck_ref[...]
        r = lax.broadcasted_iota(jnp.int32, (tm, tm), 0)
        c = lax.broadcasted_iota(jnp.int32, (tm, tm), 1)
        tri = jnp.where(c >= r, 1.0, 0.0).astype(BF16)
        hi, mid, lo = _split3(dcum)
        rev = (_dot(tri, hi) + _dot(tri, mid)) + _dot(tri, lo) + carry_ref[0:1, :]
        carry_ref[0:1, :] = rev[0:1, :]
        fl_v = fl_ref[...]
        df = rev * (1.0 / (1.0 + jnp.exp(fl_v)))
        df_ref[...] = df
        db_ref[...] += jnp.sum(df, axis=0, keepdims=True)

    return pl.pallas_call(
        body, name="gate_bwd", grid=(n,),
        in_specs=[pl.BlockSpec((tm, AUG_W), lambda i: (n - 1 - i, 0)),
                  pl.BlockSpec((tm, LANES), lambda i: (n - 1 - i, 0)),
                  pl.BlockSpec((tm, LANES), lambda i: (n - 1 - i, 0))],
        out_specs=[pl.BlockSpec((tm, D_MAIN), lambda i: (n - 1 - i, 0)),
                   pl.BlockSpec((tm, LANES), lambda i: (n - 1 - i, 0)),
                   pl.BlockSpec((1, LANES), lambda i: (0, 0))],
        out_shape=[jax.ShapeDtypeStruct((S, D_MAIN), BF16),
                   jax.ShapeDtypeStruct((S, LANES), F32), jax.ShapeDtypeStruct((1, LANES), F32)],
        scratch_shapes=[pltpu.VMEM((8, LANES), F32)],
        compiler_params=_params(),
    )(dq_aug, dck, fl)


def _silu_and_grad(g):
    sg = 1.0 / (1.0 + jnp.exp(-g))
    return g * sg, sg * (1.0 + g * (1.0 - sg))


def _mix(kind, mode, *, h, xres=None, mkv, w_out, ln_g, ln_b=None, pool_w=None, pool_scale=None,
         ymain=None, target=None, z=None, dy=None, tm):
    S = h.shape[0]
    n = S // tm
    pool = kind == "pool"
    bwd = mode == "bwd"
    loss_head = bwd and not pool
    rev = pool and bwd
    mem_scale = MEM_HEAD_DIM ** -0.5

    def t_of(i):
        return (n - 1 - i) if rev else i

    row = lambda i: (t_of(i), 0)
    names, arrays, specs = [], [], []

    def add(name, arr, spec):
        names.append(name)
        arrays.append(arr)
        specs.append(spec)

    add("h", h, pl.BlockSpec((tm, D_IN), row))
    if pool:
        hb = tm // POOL_HALO
        add("halo", h, pl.BlockSpec((POOL_HALO, D_MAIN), lambda i: (jnp.maximum(t_of(i) * hb - 1, 0), 0)))
        add("pool_w", pool_w, _const_spec((4, POOL_GROUP, POOL_GROUP)))
        add("pool_scale", pool_scale, _const_spec((1, D_MAIN)))
    else:
        add("ymain", ymain, pl.BlockSpec((tm, D_MAIN), row))
    add("mkv", mkv, _const_spec((N_MEM, 2 * D_MEM)))
    add("w_out", w_out, _const_spec((D_MIX, D_MODEL)))
    add("ln_g", ln_g, _const_spec((1, D_MODEL)))
    if not (pool and bwd):
        add("xres", xres, pl.BlockSpec((tm, D_MODEL), row))
        add("ln_b", ln_b, _const_spec((1, D_MODEL)))
    if loss_head:
        add("target", target, pl.BlockSpec((tm, D_MODEL), row))
    if pool and bwd:
        add("z", z, pl.BlockSpec((tm, D_MODEL), row))
        add("dy", dy, pl.BlockSpec((tm, D_MODEL), row))

    onames, oshapes, ospecs = [], [], []

    def add_out(name, shape, dtype, spec):
        onames.append(name)
        oshapes.append(jax.ShapeDtypeStruct(shape, dtype))
        ospecs.append(spec)

    const2 = lambda i: (0, 0)
    if not bwd:
        add_out("z", (S, D_MODEL), F32, pl.BlockSpec((tm, D_MODEL), row))
        add_out("xout", (S, D_MODEL), F32, pl.BlockSpec((tm, D_MODEL), row))
    else:
        add_out("dz", (S, D_MODEL), F32, pl.BlockSpec((tm, D_MODEL), row))
        add_out("dmain", (S, D_MAIN), BF16, pl.BlockSpec((tm, D_MAIN), row))
        add_out("drest", (S, D_IN - D_MAIN), BF16, pl.BlockSpec((tm, D_IN - D_MAIN), row))
        add_out("dw_out", (D_MIX, D_MODEL), F32, pl.BlockSpec((D_MIX, D_MODEL), const2))
        add_out("dmkv", (N_MEM, 2 * D_MEM), F32, pl.BlockSpec((N_MEM, 2 * D_MEM), const2))
        add_out("dln_g", (1, D_MODEL), F32, pl.BlockSpec((1, D_MODEL), const2))
        add_out("dln_b", (1, D_MODEL), F32, pl.BlockSpec((1, D_MODEL), const2))
        if pool:
            add_out("dpool_w", (4, POOL_GROUP, POOL_GROUP), F32,
                    pl.BlockSpec((4, POOL_GROUP, POOL_GROUP), lambda i: (0, 0, 0)))
            add_out("dpool_scale", (1, D_MAIN), F32, pl.BlockSpec((1, D_MAIN), const2))
        else:
            add_out("loss", (1, D_MODEL), F32, pl.BlockSpec((1, D_MODEL), const2))
            add_out("dcol", (S, LANES), F32, pl.BlockSpec((tm, LANES), row))

    scratch = [pltpu.VMEM((tm, D_MIX), BF16),
               pltpu.VMEM((tm, D_MEM), F32)]
    if pool:
        scratch.append(pltpu.VMEM((tm + 2 * POOL_HALO, D_MAIN), F32))
    if rev:
        scratch.append(pltpu.VMEM((POOL_HALO, D_MAIN), F32))
    n_in, n_out = len(names), len(onames)

    def body(*refs):
        R = dict(zip(names, refs[:n_in]))
        O = dict(zip(onames, refs[n_in:n_in + n_out]))
        sc = refs[n_in + n_out:]
        yc_ref, ymem_ref = sc[0], sc[1]
        ext_ref = sc[2] if pool else None
        carry_ref = sc[3] if rev else None
        i = pl.program_id(0)
        t = t_of(i)
        h_ref = R["h"]
        gamma = R["ln_g"][...]

        if bwd:
            @pl.when(i == 0)
            def _():
                for nm in ("dw_out", "dmkv", "dln_g", "dln_b", "dpool_w", "dpool_scale", "loss"):
                    if nm in O:
                        O[nm][...] = jnp.zeros_like(O[nm])
                if rev:
                    carry_ref[...] = jnp.zeros_like(carry_ref)

        if pool:
            u = h_ref[:, 0:D_MAIN].astype(F32)
            halo = R["halo"][...].astype(F32)
            ext_ref[0:POOL_HALO, :] = jnp.where(t > 0, halo, 0.0)
            ext_ref[POOL_HALO:POOL_HALO + tm, :] = u
            tpos = t * tm + lax.broadcasted_iota(jnp.int32, (tm, 1), 0)
            pms, invcs = [], []
            for gi, w in enumerate(POOL_WINDOWS):
                cs = slice(gi * POOL_GROUP, (gi + 1) * POOL_GROUP)
                acc = ext_ref[POOL_HALO:POOL_HALO + tm, cs]
                for k in range(1, w):
                    acc = acc + ext_ref[POOL_HALO - k:POOL_HALO - k + tm, cs]
                invc = 1.0 / jnp.minimum(tpos + 1, w).astype(F32)
                pm = (acc * invc - u[:, cs]).astype(BF16)
                pms.append(pm)
                invcs.append(invc)
            mixed = [_dot(pms[gi], R["pool_w"][gi]) for gi in range(4)]
            ps = R["pool_scale"][...]
            y_main = [mixed[gi] * ps[:, gi * POOL_GROUP:(gi + 1) * POOL_GROUP] for gi in range(4)]
        else:
            y_main = [R["ymain"][:, gi * 256:(gi + 1) * 256].astype(F32) for gi in range(4)]

        probs = []
        for hd in range(MEM_HEADS):
            sl = slice(D_MAIN + hd * MEM_HEAD_DIM, D_MAIN + (hd + 1) * MEM_HEAD_DIM)
            ksl = slice(hd * MEM_HEAD_DIM, (hd + 1) * MEM_HEAD_DIM)
            vsl = slice(D_MEM + hd * MEM_HEAD_DIM, D_MEM + (hd + 1) * MEM_HEAD_DIM)
            s = _dot_nt(h_ref[:, sl], R["mkv"][:, ksl]) * mem_scale
            e = jnp.exp(s - jnp.max(s, axis=1, keepdims=True))
            p = e / jnp.sum(e, axis=1, keepdims=True)
            probs.append(p)
            ymem_ref[:, ksl] = _dot(p.astype(BF16), R["mkv"][:, vsl])

        g_off = D_MIX
        gate_d = []
        for gi in range(4):
            cs = slice(gi * 256, (gi + 1) * 256)
            gm = h_ref[:, g_off + gi * 256:g_off + (gi + 1) * 256].astype(F32)
            sv, sd = _silu_and_grad(gm)
            yc_ref[:, cs] = (y_main[gi] * sv).astype(BF16)
            gate_d.append((sv, sd))
        gq = h_ref[:, g_off + D_MAIN:D_IN].astype(F32)
        svq, sdq = _silu_and_grad(gq)
        yc_ref[:, D_MAIN:D_MIX] = (ymem_ref[...] * svq).astype(BF16)

        if pool and bwd:
            zt = R["z"][...]
        else:
            o = _dot(yc_ref[...], R["w_out"][...])
            zt = ALPHA * R["xres"][...] + o
        mu = jnp.mean(zt, axis=1, keepdims=True)
        zc = zt - mu
        var = jnp.mean(zc * zc, axis=1, keepdims=True)
        rstd = lax.rsqrt(var + LN_EPS)
        xhat = zc * rstd
        if not bwd:
            O["z"][...] = zt
            O["xout"][...] = xhat * gamma + R["ln_b"][...]
            return

        if loss_head:
            xo = xhat * gamma + R["ln_b"][...]
            err = xo - R["target"][...]
            O["loss"][...] += jnp.sum(err * err, axis=0, keepdims=True)
            dyt = err * (1.0 / D_MODEL)
        else:
            dyt = R["dy"][...]

        O["dln_g"][...] += jnp.sum(dyt * xhat, axis=0, keepdims=True)
        O["dln_b"][...] += jnp.sum(dyt, axis=0, keepdims=True)
        gdy = dyt * gamma
        m1 = jnp.mean(gdy, axis=1, keepdims=True)
        m2 = jnp.mean(gdy * xhat, axis=1, keepdims=True)
        dz = rstd * (gdy - m1 - xhat * m2)
        O["dz"][...] = dz
        dzb = dz.astype(BF16)

        for n0 in range(0, D_MIX, 512):
            O["dw_out"][n0:n0 + 512, :] += _dot_tn(yc_ref[:, n0:n0 + 512], dzb)
        dyc_mem = _dot_nt(dzb, R["w_out"][D_MAIN:D_MIX, :])

        O["drest"][:, D_MEM + D_MAIN:D_MEM + D_MAIN + D_MEM] = (dyc_mem * ymem_ref[...] * sdq).astype(BF16)
        dymem = dyc_mem * svq
        for hd in range(MEM_HEADS):
            sl = slice(D_MAIN + hd * MEM_HEAD_DIM, D_MAIN + (hd + 1) * MEM_HEAD_DIM)
            ksl = slice(hd * MEM_HEAD_DIM, (hd + 1) * MEM_HEAD_DIM)
            vsl = slice(D_MEM + hd * MEM_HEAD_DIM, D_MEM + (hd + 1) * MEM_HEAD_DIM)
            p = probs[hd]
            dyb = dymem[:, ksl].astype(BF16)
            dp = _dot_nt(dyb, R["mkv"][:, vsl])
            ds = p * (dp - jnp.sum(dp * p, axis=1, keepdims=True)) * mem_scale
            dsb = ds.astype(BF16)
            O["drest"][:, ksl] = _dot(dsb, R["mkv"][:, ksl]).astype(BF16)
            O["dmkv"][:, ksl] += _dot_tn(dsb, h_ref[:, sl])
            O["dmkv"][:, vsl] += _dot_tn(p.astype(BF16), dyb)

        dmain = []
        for gi in range(4):
            cs = slice(gi * 256, (gi + 1) * 256)
            dyc_g = _dot_nt(dzb, R["w_out"][cs, :])
            sv, sd = gate_d[gi]
            O["drest"][:, D_MEM + gi * 256:D_MEM + (gi + 1) * 256] = (dyc_g * y_main[gi] * sd).astype(BF16)
            dmain.append(dyc_g * sv)

        if not pool:
            prod = []
            for gi in range(4):
                cs = slice(gi * 256, (gi + 1) * 256)
                db16 = dmain[gi].astype(BF16)
                O["dmain"][:, cs] = db16
                prod.append(db16.astype(F32) * R["ymain"][:, cs].astype(F32))
            dcol = jnp.zeros((tm, LANES), F32)
            for gi in range(4):
                dr = lax.broadcasted_iota(jnp.int32, (256, LANES), 0)
                hc = lax.broadcasted_iota(jnp.int32, (256, LANES), 1)
                sel = jnp.where(jnp.right_shift(dr, 6) + gi * 4 == hc, 1.0, 0.0).astype(BF16)
                hi, mid, lo = _split3(prod[gi])
                dcol = dcol + ((_dot(hi, sel) + _dot(mid, sel)) + _dot(lo, sel))
            O["dcol"][...] = dcol
            return

        ps = R["pool_scale"][...]
        dpm_list = []
        for gi in range(4):
            cs = slice(gi * 256, (gi + 1) * 256)
            O["dpool_scale"][:, cs] += jnp.sum(dmain[gi] * mixed[gi], axis=0, keepdims=True)
            dmix = (dmain[gi] * ps[:, cs]).astype(BF16)
            O["dpool_w"][gi] += _dot_tn(pms[gi], dmix)
            dpm = _dot_nt(dmix, R["pool_w"][gi])
            dpm_list.append(dpm)
            ext_ref[0:tm, cs] = dpm * invcs[gi]
        ext_ref[tm:tm + POOL_HALO, :] = carry_ref[...]
        carry_ref[...] = ext_ref[0:POOL_HALO, :]
        for gi, w in enumerate(POOL_WINDOWS):
            cs = slice(gi * 256, (gi + 1) * 256)
            acc = ext_ref[0:tm, cs]
            for k in range(1, w):
                acc = acc + ext_ref[k:k + tm, cs]
            O["dmain"][:, cs] = (acc - dpm_list[gi]).astype(BF16)

    outs = pl.pallas_call(
        body, name=f"mix_{kind}_{mode}", grid=(n,),
        in_specs=specs, out_specs=ospecs, out_shape=oshapes,
        scratch_shapes=scratch, compiler_params=_params(),
    )(*arrays)
    return dict(zip(onames, outs))


def _lin_bwd(xin, dhs, ws, res, *, tm, name):
    S, K = xin.shape
    nj = len(dhs)
    nr = len(res)
    widths = [w.shape[1] for w in ws]
    scales = [s for _, s in res]

    def body(*refs):
        x_ref = refs[0]
        dh_refs = refs[1:1 + nj]
        w_refs = refs[1 + nj:1 + 2 * nj]
        r_refs = refs[1 + 2 * nj:1 + 2 * nj + nr]
        dx_ref = refs[1 + 2 * nj + nr]
        dw_refs = refs[2 + 2 * nj + nr:]
        i = pl.program_id(0)

        @pl.when(i == 0)
        def _():
            for dw in dw_refs:
                dw[...] = jnp.zeros_like(dw)

        xb = x_ref[...].astype(BF16)
        dx = jnp.zeros((tm, K), F32)
        for r_ref, sc in zip(r_refs, scales):
            dx = dx + sc * r_ref[...]
        for j in range(nj):
            N = widths[j]
            nc = 512 if N % 512 == 0 else N
            for n0 in range(0, N, nc):
                dhb = dh_refs[j][:, n0:n0 + nc].astype(BF16)
                dx = dx + _dot_nt(dhb, w_refs[j][:, n0:n0 + nc])
                dw_refs[j][:, n0:n0 + nc] += _dot_tn(xb, dhb)
        dx_ref[...] = dx

    in_specs = [pl.BlockSpec((tm, K), lambda i: (i, 0))]
    in_specs += [pl.BlockSpec((tm, N), lambda i: (i, 0)) for N in widths]
    in_specs += [_const_spec((K, N)) for N in widths]
    in_specs += [pl.BlockSpec((tm, K), lambda i: (i, 0)) for _ in res]
    out_specs = [pl.BlockSpec((tm, K), lambda i: (i, 0))]
    out_specs += [pl.BlockSpec((K, N), lambda i: (0, 0)) for N in widths]
    out_shape = [jax.ShapeDtypeStruct((S, K), F32)]
    out_shape += [jax.ShapeDtypeStruct((K, N), F32) for N in widths]
    outs = pl.pallas_call(
        body, name=name, grid=(S // tm,),
        in_specs=in_specs, out_specs=out_specs, out_shape=out_shape,
        compiler_params=_params(),
    )(xin, *dhs, *ws, *[r for r, _ in res])
    return outs[0], list(outs[1:])


def _wgrad(xin, dh, *, name):
    M, K = xin.shape
    N = dh.shape[1]

    def body(x_ref, dh_ref, o_ref):
        o_ref[...] = _dot_tn(x_ref[...].astype(BF16), dh_ref[...].astype(BF16))

    return pl.pallas_call(
        body, name=name, out_shape=jax.ShapeDtypeStruct((K, N), F32),
        compiler_params=pltpu.CompilerParams(vmem_limit_bytes=VMEM_LIMIT),
    )(xin, dh)


AUG_A = FOX_HEAD_DIM
AUG_B = FOX_HEAD_DIM + 3
AUG_W = FOX_HEADS * LANES


def _placement(val_lane, ones_lane):
    r = jnp.arange(512)[:, None]
    c = jnp.arange(AUG_W)[None, :]
    head, lane = c // LANES, c % LANES
    m = jnp.zeros((512, AUG_W), jnp.bool_)
    if val_lane is not None:
        for part in range(3):
            m = m | ((r == part * LANES + head) & (lane == val_lane + part))
    if ones_lane is not None:
        m = m | ((r == 3 * LANES) & (lane >= ones_lane) & (lane < ones_lane + 3))
    return m.astype(BF16)


def _augment(x, cols, signs, *, val_lane, ones_lane, scale, tm, name):
    S = x.shape[0]
    place = _placement(val_lane if cols else None, ones_lane)
    nc = len(cols)

    def body(*refs):
        x_ref = refs[0]
        col_refs = refs[1:1 + nc]
        p_ref = refs[1 + nc]
        o_ref = refs[2 + nc]
        lane = lax.broadcasted_iota(jnp.int32, (tm, LANES), 1)
        data = lane < FOX_HEAD_DIM
        one_col = jnp.where(lane == 0, 1.0, 0.0).astype(BF16)
        if nc:
            val = signs[0] * col_refs[0][...]
            for i in range(1, nc):
                val = val + signs[i] * col_refs[i][...]
            hi, mid, lo = _split3(val)
        else:
            hi = mid = lo = jnp.zeros((tm, LANES), BF16)
        lhs = jnp.concatenate([hi, mid, lo, one_col], axis=1)
        for hp in range(FOX_HEADS // 2):
            extra = _dot(lhs, p_ref[:, 2 * hp * LANES:(2 * hp + 2) * LANES])
            blk = x_ref[:, hp * LANES:(hp + 1) * LANES].astype(F32) * scale
            o_ref[:, 2 * hp * LANES:(2 * hp + 1) * LANES] = jnp.where(data, blk, extra[:, 0:LANES]).astype(BF16)
            o_ref[:, (2 * hp + 1) * LANES:(2 * hp + 2) * LANES] = jnp.where(
                data, pltpu.roll(blk, FOX_HEAD_DIM, 1), extra[:, LANES:2 * LANES]).astype(BF16)

    in_specs = [pl.BlockSpec((tm, D_MAIN), lambda i: (i, 0))]
    in_specs += [pl.BlockSpec((tm, LANES), lambda i: (i, 0)) for _ in cols]
    in_specs += [_const_spec((512, AUG_W))]
    return pl.pallas_call(
        body, name=name, grid=(S // tm,),
        in_specs=in_specs, out_specs=pl.BlockSpec((tm, AUG_W), lambda i: (i, 0)),
        out_shape=jax.ShapeDtypeStruct((S, AUG_W), BF16),
        compiler_params=_params(),
    )(x, *cols, place)


def _cols_of_rows(rows, S):
    nh = FOX_HEADS // rows.shape[0]
    a = rows[:, :, 0:nh, :].transpose(0, 2, 1, 3).reshape(FOX_HEADS, S).T
    return jnp.pad(a, ((0, 0), (0, LANES - FOX_HEADS)))


def _fox_fwd(qf, ka, va, *, tq, nh=4):
    S = ka.shape[0]
    nq = S // tq
    tk = tq
    ng = FOX_HEADS // nh

    def body(q_ref, k_ref, v_ref, o_ref, lse_ref, *scratch):
        p_scr, m_scr, acc_scr = scratch[0:nh], scratch[nh:2 * nh], scratch[2 * nh:3 * nh]
        qi = pl.program_id(1)
        lane = lax.broadcasted_iota(jnp.int32, (tq, LANES), 1)
        half0 = lane < FOX_HEAD_DIM
        rr = lax.broadcasted_iota(jnp.int32, (tq, tk), 0)
        cc = lax.broadcasted_iota(jnp.int32, (tq, tk), 1)
        sls = [slice(hh * LANES, (hh + 1) * LANES) for hh in range(nh)]
        qs = [q_ref[:, sl] for sl in sls]

        for hh in range(nh):
            p_scr[hh][...] = jnp.zeros_like(p_scr[hh])
            m_scr[hh][...] = jnp.full(m_scr[hh].shape, -jnp.inf, F32)
            acc_scr[hh][...] = jnp.zeros_like(acc_scr[hh])

        def chunk(ki, masked):
            k0 = pl.multiple_of(ki * tk, tk)
            kp = pl.multiple_of(jnp.maximum(ki - 1, 0) * tk, tk)
            for hh in range(nh):
                m = m_scr[hh][...]
                s = _dot_nt(qs[hh], k_ref[pl.ds(k0, tk), sls[hh]])
                pv = _dot(p_scr[hh][...], v_ref[pl.ds(kp, tk), sls[hh]])
                if masked:
                    s = jnp.where(cc <= rr, s, -jnp.inf)
                m_new = jnp.maximum(m, jnp.max(s, axis=1, keepdims=True))
                p_scr[hh][...] = jnp.exp(s - jnp.tile(m_new, (1, tk // LANES))).astype(BF16)
                acc_scr[hh][...] = (acc_scr[hh][...] + pv) * jnp.exp(m - m_new)
                m_scr[hh][...] = m_new

        def trip(ki, c):
            chunk(ki, False)
            return c

        lax.fori_loop(0, qi, trip, 0)
        chunk(qi, True)
        kq = pl.multiple_of(qi * tk, tk)
        outs = []
        lse_cols = jnp.zeros((tq, LANES), F32)
        for hh in range(nh):
            m = m_scr[hh][...]
            acc = acc_scr[hh][...] + _dot(p_scr[hh][...], v_ref[pl.ds(kq, tk), sls[hh]])
            l = jnp.sum(jnp.where(lane == AUG_A, acc, 0.0), axis=1, keepdims=True)
            outs.append(acc / l)
            lse_cols = lse_cols + jnp.where(lane == hh, m + jnp.log(l), 0.0)
        for pr in range(nh // 2):
            o_ref[:, pr * LANES:(pr + 1) * LANES] = jnp.where(
                half0, outs[2 * pr], pltpu.roll(outs[2 * pr + 1], FOX_HEAD_DIM, 1))
        lse_ref[0, 0] = lse_cols.T[0:8, :]

    return pl.pallas_call(
        body, name="fox_fwd", grid=(ng, nq),
        in_specs=[pl.BlockSpec((tq, nh * LANES), lambda g, qi: (qi, g)),
                  pl.BlockSpec((S, nh * LANES), lambda g, qi: (0, g), pipeline_mode=pl.Buffered(1)),
                  pl.BlockSpec((S, nh * LANES), lambda g, qi: (0, g), pipeline_mode=pl.Buffered(1))],
        out_specs=[pl.BlockSpec((tq, nh * FOX_HEAD_DIM), lambda g, qi: (qi, g)),
                   pl.BlockSpec((1, 1, 8, tq), lambda g, qi: (g, qi, 0, 0))],
        out_shape=[jax.ShapeDtypeStruct((S, D_MAIN), F32),
                   jax.ShapeDtypeStruct((ng, nq, 8, tq), F32)],
        scratch_shapes=([pltpu.VMEM((tq, tk), BF16)] * nh + [pltpu.VMEM((tq, LANES), F32)] * nh
                        + [pltpu.VMEM((tq, LANES), F32)] * nh),
        compiler_params=_params(2),
    )(qf, ka, va)


def _fox_bwd(qb, ka, va, dob, *, tq):
    S = ka.shape[0]
    nq = S // tq
    tk = tq

    def body(k_ref, v_ref, q_ref, do_ref, dq_ref, dk_ref, dv_ref, dck_ref, dk_scr, dv_scr):
        kj = pl.program_id(1)

        @pl.when(kj == 0)
        def _():
            dq_ref[...] = jnp.zeros_like(dq_ref)

        lane = lax.broadcasted_iota(jnp.int32, (tk, LANES), 1)
        half0 = lane < FOX_HEAD_DIM
        rr = lax.broadcasted_iota(jnp.int32, (tk, tq), 0)
        cc = lax.broadcasted_iota(jnp.int32, (tk, tq), 1)
        sls = [slice(hh * LANES, (hh + 1) * LANES) for hh in range(2)]
        kts = [k_ref[:, sl] for sl in sls]
        vts = [v_ref[:, sl] for sl in sls]

        dk_scr[...] = jnp.zeros_like(dk_scr)
        dv_scr[...] = jnp.zeros_like(dv_scr)

        def chunk(qi, masked):
            q0 = pl.multiple_of(qi * tq, tq)
            for hh in range(2):
                qc = q_ref[pl.ds(q0, tq), sls[hh]]
                doc = do_ref[pl.ds(q0, tq), sls[hh]]
                pt = jnp.exp(_dot_nt(kts[hh], qc))
                if masked:
                    pt = jnp.where(rr <= cc, pt, 0.0)
                dsb = (pt * _dot_nt(vts[hh], doc)).astype(BF16)
                dv_scr[hh] += _dot(pt.astype(BF16), doc)
                dk_scr[hh] += _dot(dsb, qc)
                dq_ref[pl.ds(q0, tq), sls[hh]] += _dot_tn(dsb, kts[hh])

        def trip(qi, c):
            chunk(qi, False)
            return c

        chunk(kj, True)
        lax.fori_loop(kj + 1, nq, trip, 0)
        dk0, dk1 = dk_scr[0], dk_scr[1]
        dv0, dv1 = dv_scr[0], dv_scr[1]
        dk_ref[...] = jnp.where(half0, dk0, pltpu.roll(dk1, FOX_HEAD_DIM, 1)).astype(BF16)
        dv_ref[...] = jnp.where(half0, dv0, pltpu.roll(dv1, FOX_HEAD_DIM, 1)).astype(BF16)
        c0 = jnp.sum(jnp.where(lane == AUG_B, dk0, 0.0), axis=1, keepdims=True)
        c1 = jnp.sum(jnp.where(lane == AUG_B, dk1, 0.0), axis=1, keepdims=True)
        dck_cols = jnp.where(lane == 0, c0, 0.0) + jnp.where(lane == 1, c1, 0.0)
        dck_ref[0, 0] = dck_cols.T[0:8, :]

    return pl.pallas_call(
        body, name="fox_bwd", grid=(8, nq),
        in_specs=[pl.BlockSpec((tk, 2 * LANES), lambda hp, kj: (kj, hp)),
                  pl.BlockSpec((tk, 2 * LANES), lambda hp, kj: (kj, hp)),
                  pl.BlockSpec((S, 2 * LANES), lambda hp, kj: (0, hp)),
                  pl.BlockSpec((S, 2 * LANES), lambda hp, kj: (0, hp))],
        out_specs=[pl.BlockSpec((S, 2 * LANES), lambda hp, kj: (0, hp)),
                   pl.BlockSpec((tk, LANES), lambda hp, kj: (kj, hp)),
                   pl.BlockSpec((tk, LANES), lambda hp, kj: (kj, hp)),
                   pl.BlockSpec((1, 1, 8, tk), lambda hp, kj: (hp, kj, 0, 0))],
        out_shape=[jax.ShapeDtypeStruct((S, AUG_W), F32),
                   jax.ShapeDtypeStruct((S, D_MAIN), BF16),
                   jax.ShapeDtypeStruct((S, D_MAIN), BF16),
                   jax.ShapeDtypeStruct((8, nq, 8, tk), F32)],
        scratch_shapes=[pltpu.VMEM((2, tk, LANES), F32), pltpu.VMEM((2, tk, LANES), F32)],
        compiler_params=_params(2),
    )(ka, va, qb, dob)


def _adamw(w, g, m, v, *, name):
    Rr, C = w.shape
    tr = 256 if Rr % 256 == 0 else Rr
    c1 = 1.0 / (1.0 - ADAM_B1 ** ADAM_STEP)
    c2 = 1.0 / (1.0 - ADAM_B2 ** ADAM_STEP)

    def body(w_ref, g_ref, m_ref, v_ref, d_ref, nm_ref, nv_ref):
        gv = g_ref[...]
        nm = ADAM_B1 * m_ref[...] + (1.0 - ADAM_B1) * gv
        nv = ADAM_B2 * v_ref[...] + (1.0 - ADAM_B2) * (gv * gv)
        d_ref[...] = -ADAM_LR * ((nm * c1) / (jnp.sqrt(nv * c2) + ADAM_EPS) + ADAM_WD * w_ref[...])
        nm_ref[...] = nm
        nv_ref[...] = nv

    spec = pl.BlockSpec((tr, C), lambda i: (i, 0))
    sds = jax.ShapeDtypeStruct((Rr, C), F32)
    return pl.pallas_call(
        body, name=name, grid=(Rr // tr,),
        in_specs=[spec] * 4, out_specs=[spec] * 3, out_shape=[sds] * 3,
        compiler_params=_params(),
    )(w, g, m, v)


_ANY = pl.BlockSpec(memory_space=pl.ANY)
_MESH = pl.DeviceIdType.MESH


def _place():
    x, y, c = lax.axis_index("x"), lax.axis_index("y"), lax.axis_index("c")
    return x, y, c


def _all_gather_shards(pack):
    dt = pack.dtype

    def body(p_ref, out_ref, send_sems, recv_sems):
        x, y, c = _place()
        sib = (x, y, 1 - c)
        chips = [(1 - x, y), (x, 1 - y), (1 - x, 1 - y)]
        me = 2 * x + y

        def copy(k, chip_idx, half, to, src=None):
            dst = out_ref.at[chip_idx, half]
            return pltpu.make_async_remote_copy(
                src_ref=dst if src is None else src, dst_ref=dst,
                send_sem=send_sems.at[k], recv_sem=recv_sems.at[k],
                device_id=to, device_id_type=_MESH)

        first = [copy(j, me, c, (*chip, c), src=p_ref.at[c]) for j, chip in enumerate(chips)]
        for cp in first:
            cp.start()
        passed = [copy(3 + j, 2 * chip[0] + chip[1], c, sib) for j, chip in enumerate(chips)]
        for j, chip in enumerate(chips):
            copy(j, 2 * chip[0] + chip[1], c, sib).wait_recv()
            passed[j].start()
        for j, chip in enumerate(chips):
            copy(3 + j, 2 * chip[0] + chip[1], 1 - c, sib).wait_recv()
        for cp in first + passed:
            cp.wait_send()

    return pl.pallas_call(
        body, name="all_gather_shards",
        in_specs=[_ANY], out_specs=_ANY,
        out_shape=jax.ShapeDtypeStruct((N_CHIPS, 2, HALF_ROWS, 1024), dt),
        scratch_shapes=[pltpu.SemaphoreType.DMA((6,)), pltpu.SemaphoreType.DMA((6,))],
    )(pack)


def _send_half_to_sibling(gpack):
    def body(g_ref, out_ref, send_sem, recv_sem):
        x, y, c = _place()
        sib = (x, y, 1 - c)
        cps = [pltpu.make_async_remote_copy(
            src_ref=g_ref.at[j, 1 - c], dst_ref=out_ref.at[j],
            send_sem=send_sem.at[j], recv_sem=recv_sem.at[j],
            device_id=sib, device_id_type=_MESH) for j in range(N_CHIPS)]
        for cp in cps:
            cp.start()
        for cp in cps:
            cp.wait_recv()
        for cp in cps:
            cp.wait_send()

    return pl.pallas_call(
        body, name="pair_send",
        in_specs=[_ANY], out_specs=_ANY,
        out_shape=jax.ShapeDtypeStruct((N_CHIPS, HALF_ROWS, 1024), F32),
        scratch_shapes=[pltpu.SemaphoreType.DMA((N_CHIPS,)), pltpu.SemaphoreType.DMA((N_CHIPS,))],
    )(gpack)


def _pair_sum(gpack, recv, c_arr, *, tr):
    def body(c_ref, a_ref, b_ref, o_ref):
        o_ref[...] = (a_ref[...] + b_ref[...]).astype(BF16)

    grid_spec = pltpu.PrefetchScalarGridSpec(
        num_scalar_prefetch=1, grid=(N_CHIPS, HALF_ROWS // tr),
        in_specs=[pl.BlockSpec((None, None, tr, 1024), lambda j, i, c_ref: (j, c_ref[0], i, 0)),
                  pl.BlockSpec((None, tr, 1024), lambda j, i, c_ref: (j, i, 0))],
        out_specs=pl.BlockSpec((None, tr, 1024), lambda j, i, c_ref: (j, i, 0)))
    return pl.pallas_call(
        body, name="pair_sum", grid_spec=grid_spec,
        out_shape=jax.ShapeDtypeStruct((N_CHIPS, HALF_ROWS, 1024), BF16),
        compiler_params=_params(2),
    )(c_arr, gpack, recv)


def _scatter_pieces(psum):
    def body(p_ref, out_ref, send_sems, recv_sems):
        x, y, c = _place()
        chips = [(1 - x, y), (x, 1 - y), (1 - x, 1 - y)]
        me = 2 * x + y
        cps = []
        for j, chip in enumerate(chips):
            them = 2 * chip[0] + chip[1]
            cps.append(pltpu.make_async_remote_copy(
                src_ref=p_ref.at[them], dst_ref=out_ref.at[me],
                send_sem=send_sems.at[j], recv_sem=recv_sems.at[j],
                device_id=(*chip, c), device_id_type=_MESH))
        for cp in cps:
            cp.start()
        for cp in cps:
            cp.wait_recv()
        for cp in cps:
            cp.wait_send()

    return pl.pallas_call(
        body, name="scatter_pieces",
        in_specs=[_ANY], out_specs=_ANY,
        out_shape=jax.ShapeDtypeStruct((N_CHIPS, HALF_ROWS, 1024), BF16),
        scratch_shapes=[pltpu.SemaphoreType.DMA((3,)), pltpu.SemaphoreType.DMA((3,))],
    )(psum)


def _sum_pieces(pieces, *, tr):
    def body(p_ref, o_ref):
        acc = p_ref[0].astype(F32) + p_ref[1].astype(F32)
        acc = acc + p_ref[2].astype(F32)
        o_ref[...] = acc + p_ref[3].astype(F32)

    return pl.pallas_call(
        body, name="sum_pieces", grid=(HALF_ROWS // tr,),
        in_specs=[pl.BlockSpec((N_CHIPS, tr, 1024), lambda i: (0, i, 0))],
        out_specs=pl.BlockSpec((tr, 1024), lambda i: (i, 0)),
        out_shape=jax.ShapeDtypeStruct((HALF_ROWS, 1024), F32),
        compiler_params=_params(),
    )(pieces)


def _exchange_halves(total):
    def body(t_ref, out_ref, send_sem, recv_sem):
        x, y, c = _place()
        sib = (x, y, 1 - c)
        cp = pltpu.make_async_remote_copy(
            src_ref=t_ref, dst_ref=out_ref.at[c], send_sem=send_sem, recv_sem=recv_sem,
            device_id=sib, device_id_type=_MESH)
        cp.start()
        cp.wait_recv()
        cp.wait_send()

    return pl.pallas_call(
        body, name="exchange_halves",
        in_specs=[_ANY], out_specs=_ANY,
        out_shape=jax.ShapeDtypeStruct((2, HALF_ROWS, 1024), F32),
        scratch_shapes=[pltpu.SemaphoreType.DMA, pltpu.SemaphoreType.DMA],
    )(total)


def _pad_rows(a, rows):
    return jnp.pad(a, ((0, rows - a.shape[0]), (0, 0)))


def _pack_weight_shard(w_in, w_mem_kv, w_out, pool_w, w_kv_shared, pool_scale):
    ps_bits = lax.bitcast_convert_type(pool_scale.reshape(-1), BF16).reshape(1, -1)
    ps_row = jnp.pad(ps_bits, ((0, 0), (0, 1024 - ps_bits.shape[1])))
    parts = [
        w_in.astype(BF16).reshape(ROWS_W_IN, 1024),
        w_mem_kv.astype(BF16).reshape(ROWS_W_MKV, 1024),
        w_out.astype(BF16).reshape(ROWS_W_OUT, 1024),
        pool_w.astype(BF16).reshape(ROWS_POOL_W, 1024),
        _pad_rows(w_kv_shared.astype(BF16).reshape(KV_SHARD, 1024), ROWS_W_KV),
        _pad_rows(ps_row, ROWS_SMALL),
        jnp.zeros((PACK_ROWS - OFF_LN_G, 1024), BF16),
    ]
    return jnp.concatenate(parts, axis=0).reshape(2, HALF_ROWS, 1024)


def _unpack_weights(g):
    w_in = g[:, OFF_W_IN:OFF_W_IN + ROWS_W_IN].reshape(4, 2, D_MODEL, D_IN // 4)
    w_in = w_in.transpose(1, 2, 0, 3).reshape(2, D_MODEL, D_IN)
    w_mkv = g[:, OFF_W_MKV:OFF_W_MKV + ROWS_W_MKV].reshape(4, 2, D_MODEL // 4, 2 * D_MEM)
    w_mkv = w_mkv.transpose(1, 0, 2, 3).reshape(2, D_MODEL, 2 * D_MEM)
    w_out = g[:, OFF_W_OUT:OFF_W_OUT + ROWS_W_OUT].reshape(4, 2, D_MIX // 4, D_MODEL)
    w_out = w_out.transpose(1, 0, 2, 3).reshape(2, D_MIX, D_MODEL)
    pool_w = g[:, OFF_POOL_W:OFF_POOL_W + ROWS_POOL_W].reshape(4, 4, POOL_GROUP // 4, POOL_GROUP)
    pool_w = pool_w.transpose(1, 0, 2, 3).reshape(4, POOL_GROUP, POOL_GROUP)
    w_kv = g[:, OFF_W_KV:OFF_W_KV + KV_SHARD].reshape(4, D_MODEL, KV_SHARD)
    w_kv = w_kv.transpose(1, 0, 2).reshape(D_MODEL, KV_COLS)
    ps_bits = g[:, OFF_POOL_S, 0:512].reshape(4, 256, 2)
    pool_scale = lax.bitcast_convert_type(ps_bits, F32).reshape(1, D_MAIN)
    return w_in, w_mkv, w_out, pool_w, w_kv, pool_scale


def _pack_grads(g_w_in, g_w_mkv, g_w_out, g_pool_w, g_w_kv, g_pool_scale, g_ln_g, g_ln_b, g_bf):
    def rep(a):
        a = _pad_rows(a, ROWS_SMALL)
        return jnp.broadcast_to(a[None], (4,) + a.shape)

    parts = [
        g_w_in.reshape(2, D_MODEL, 4, D_IN // 4).transpose(2, 0, 1, 3).reshape(4, ROWS_W_IN, 1024),
        g_w_mkv.reshape(2, 4, D_MODEL // 4, 2 * D_MEM).transpose(1, 0, 2, 3).reshape(4, ROWS_W_MKV, 1024),
        g_w_out.reshape(2, 4, D_MIX // 4, D_MODEL).transpose(1, 0, 2, 3).reshape(4, ROWS_W_OUT, 1024),
        g_pool_w.reshape(4, 4, POOL_GROUP // 4, POOL_GROUP).transpose(1, 0, 2, 3).reshape(4, ROWS_POOL_W, 1024),
        jnp.pad(g_w_kv.reshape(D_MODEL, 4, KV_SHARD).transpose(1, 0, 2).reshape(4, KV_SHARD, 1024),
                ((0, 0), (0, ROWS_W_KV - KV_SHARD), (0, 0))),
        jnp.pad(g_pool_scale.reshape(4, 1, 256), ((0, 0), (0, ROWS_SMALL - 1), (0, 1024 - 256))),
        rep(g_ln_g), rep(g_ln_b),
        rep(jnp.pad(g_bf.reshape(1, -1), ((0, 0), (0, 1024 - g_bf.shape[0])))),
        jnp.zeros((4, ROWS_PAD, 1024), F32),
    ]
    return jnp.concatenate(parts, axis=1).reshape(4, 2, HALF_ROWS, 1024)


def _local_step(x, mem, target, w_in, w_mkv, w_out, pool_w, pool_scale, w_kv, ln_g, ln_b, b_forget,
                *, tm=256, tq=512):
    S = x.shape[0]
    nq = S // tq
    g_rows = [ln_g[l:l + 1] for l in range(2)]
    b_rows = [ln_b[l:l + 1] for l in range(2)]
    wk, wv = w_kv[:, 0:D_MAIN], w_kv[:, D_MAIN:2 * D_MAIN]
    wf = jnp.pad(w_kv[:, 2 * D_MAIN:], ((0, 0), (0, LANES - FOX_HEADS)))
    bf_row = jnp.pad(b_forget.reshape(1, -1), ((0, 0), (0, LANES - FOX_HEADS)))

    mkv = [_linear_fwd(mem, w_mkv[l], tm=N_MEM, name=f"mem_kv{l}") for l in range(2)]

    h0 = _linear_fwd(x, w_in[0], tm=tm, name="in_proj0")
    f0 = _mix("pool", "fwd", h=h0, xres=x, mkv=mkv[0], w_out=w_out[0], ln_g=g_rows[0], ln_b=b_rows[0],
              pool_w=pool_w, pool_scale=pool_scale, tm=tm)
    z0, x1 = f0["z"], f0["xout"]
    k, v, fl, cum = _kv_proj(x1, wk, wv, wf, bf_row, tm=tm)
    h1 = _linear_fwd(x1, w_in[1], tm=tm, name="in_proj1")
    aug = functools.partial(_augment, tm=tm)
    ka = aug(k, [cum], [-1.0], val_lane=AUG_B, ones_lane=AUG_A, scale=1.0, name="aug_k")
    va = aug(v, [], [], val_lane=None, ones_lane=AUG_A, scale=1.0, name="aug_v")
    qf = aug(h1, [cum], [1.0], val_lane=AUG_A, ones_lane=AUG_B, scale=FOX_SCALE, name="aug_q_fwd")
    ymain1, lse_rows = _fox_fwd(qf, ka, va, tq=tq)

    b1 = _mix("fox", "bwd", h=h1, xres=x1, mkv=mkv[1], w_out=w_out[1], ln_g=g_rows[1], ln_b=b_rows[1],
              ymain=ymain1, target=target, tm=tm)
    qb = aug(h1, [cum, _cols_of_rows(lse_rows, S)], [1.0, -1.0], val_lane=AUG_A, ones_lane=AUG_B,
             scale=FOX_SCALE, name="aug_q_bwd")
    dob = aug(b1["dmain"], [b1["dcol"]], [-1.0], val_lane=AUG_A, ones_lane=None, scale=1.0, name="aug_do")
    dq_aug, dk, dv, dck_rows = _fox_bwd(qb, ka, va, dob, tq=tq)
    du1, df, dbf = _gate_bwd(dq_aug, _cols_of_rows(dck_rows, S), fl, tm=tm)

    dx1a, (dwu1, dwr1) = _lin_bwd(
        x1, [du1, b1["drest"]], [w_in[1][:, 0:D_MAIN], w_in[1][:, D_MAIN:]], [(b1["dz"], ALPHA)],
        tm=tm, name="in_proj1_bwd")
    dx1, (dwk, dwv, dwf) = _lin_bwd(x1, [dk, dv, df], [wk, wv, wf], [(dx1a, 1.0)], tm=tm, name="kv_proj_bwd")

    b0 = _mix("pool", "bwd", h=h0, mkv=mkv[0], w_out=w_out[0], ln_g=g_rows[0],
              pool_w=pool_w, pool_scale=pool_scale, z=z0, dy=dx1, tm=tm)
    dx, (dwu0, dwr0) = _lin_bwd(
        x, [b0["dmain"], b0["drest"]], [w_in[0][:, 0:D_MAIN], w_in[0][:, D_MAIN:]], [(b0["dz"], ALPHA)],
        tm=tm, name="in_proj0_bwd")
    dw_mkv = [_wgrad(mem, b["dmkv"], name=f"mem_kv{l}_bwd") for l, b in enumerate((b0, b1))]

    grads = dict(
        w_in=jnp.stack([jnp.concatenate([dwu0, dwr0], axis=1), jnp.concatenate([dwu1, dwr1], axis=1)]),
        w_mem_kv=jnp.stack(dw_mkv),
        w_out=jnp.stack([b0["dw_out"], b1["dw_out"]]),
        ln_g=jnp.concatenate([b0["dln_g"], b1["dln_g"]], axis=0),
        ln_b=jnp.concatenate([b0["dln_b"], b1["dln_b"]], axis=0),
        pool_w=b0["dpool_w"],
        pool_scale=b0["dpool_scale"],
        w_kv=jnp.concatenate([dwk, dwv, dwf[:, 0:FOX_HEADS]], axis=1),
        b_forget=dbf[0, 0:FOX_HEADS],
    )
    return b1["loss"], dx, grads


def kernel(x, mem, w_in, w_mem_kv, w_out, ln_g, ln_b, pool_w, pool_scale, w_kv_shared, b_forget, loss_target, m_w_in, m_w_mem_kv, m_w_out, m_ln_g, m_ln_b, m_pool_w, m_pool_scale, m_w_kv_shared, m_b_forget, v_w_in, v_w_mem_kv, v_w_out, v_ln_g, v_ln_b, v_pool_w, v_pool_scale, v_w_kv_shared, v_b_forget):
    c_arr = lax.axis_index("c").astype(jnp.int32).reshape(1)

    wpack = _pack_weight_shard(w_in, w_mem_kv, w_out, pool_w, w_kv_shared, pool_scale)
    me = 2 * lax.axis_index("x") + lax.axis_index("y")
    my_c = lax.axis_index("c")
    gathered = lax.dynamic_update_slice(_all_gather_shards(wpack), wpack[None], (me, 0, 0, 0))
    gathered = gathered.reshape(N_CHIPS, PACK_ROWS, 1024)
    fw_in, fw_mkv, fw_out, fpool_w, fw_kv, fpool_scale = _unpack_weights(gathered)

    loss_vec, dx, g = _local_step(x[0], mem[0], loss_target[0], fw_in, fw_mkv, fw_out, fpool_w,
                                  fpool_scale, fw_kv, ln_g, ln_b, b_forget)
    loss = lax.psum(0.5 / D_MODEL * jnp.sum(loss_vec), ("x", "y", "c"))

    gpack = _pack_grads(g["w_in"], g["w_mem_kv"], g["w_out"], g["pool_w"], g["w_kv"], g["pool_scale"],
                        g["ln_g"], g["ln_b"], g["b_forget"])
    from_sibling = _send_half_to_sibling(gpack)
    psum = _pair_sum(gpack, from_sibling, c_arr, tr=PACK_TILE)
    own_piece = lax.dynamic_slice(psum, (me, 0, 0), (1, HALF_ROWS, 1024))
    pieces = lax.dynamic_update_slice(_scatter_pieces(psum), own_piece, (me, 0, 0))
    total = _sum_pieces(pieces, tr=PACK_TILE)
    shard = lax.dynamic_update_slice(_exchange_halves(total), total[None], (my_c, 0, 0))
    shard = shard.reshape(PACK_ROWS, 1024)

    g_w_in = shard[OFF_W_IN:OFF_W_IN + ROWS_W_IN].reshape(w_in.shape)
    g_w_mkv = shard[OFF_W_MKV:OFF_W_MKV + ROWS_W_MKV].reshape(w_mem_kv.shape)
    g_w_out = shard[OFF_W_OUT:OFF_W_OUT + ROWS_W_OUT].reshape(w_out.shape)
    g_pool_w = shard[OFF_POOL_W:OFF_POOL_W + ROWS_POOL_W].reshape(pool_w.shape)
    g_w_kv = shard[OFF_W_KV:OFF_W_KV + KV_SHARD].reshape(w_kv_shared.shape)
    g_pool_scale = shard[OFF_POOL_S:OFF_POOL_S + 1, 0:256].reshape(pool_scale.shape)
    g_ln_g = shard[OFF_LN_G:OFF_LN_G + 2]
    g_ln_b = shard[OFF_LN_B:OFF_LN_B + 2]
    g_bf = shard[OFF_BF, 0:FOX_HEADS]

    names = ["w_in", "w_mem_kv", "w_out", "ln_g", "ln_b", "pool_w", "pool_scale", "w_kv_shared", "b_forget"]
    ws = [w_in, w_mem_kv, w_out, ln_g, ln_b, pool_w, pool_scale, w_kv_shared, b_forget]
    gs = [g_w_in, g_w_mkv, g_w_out, g_ln_g, g_ln_b, g_pool_w, g_pool_scale, g_w_kv, g_bf]
    ms = [m_w_in, m_w_mem_kv, m_w_out, m_ln_g, m_ln_b, m_pool_w, m_pool_scale, m_w_kv_shared, m_b_forget]
    vs = [v_w_in, v_w_mem_kv, v_w_out, v_ln_g, v_ln_b, v_pool_w, v_pool_scale, v_w_kv_shared, v_b_forget]
    deltas, new_ms, new_vs = [], [], []
    for nm, w, gg, mm, vv in zip(names, ws, gs, ms, vs):
        two_d = (-1, w.shape[-1])
        d, nmm, nvv = _adamw(w.reshape(two_d), gg.reshape(two_d), mm.reshape(two_d), vv.reshape(two_d),
                             name=f"adamw_{nm}")
        deltas.append(d.reshape(w.shape))
        new_ms.append(nmm.reshape(w.shape))
        new_vs.append(nvv.reshape(w.shape))

    return (loss, dx[None], *gs, *deltas, *new_ms, *new_vs)
```

```python
import functools

import jax
import jax.numpy as jnp
from jax import lax
from jax.experimental import pallas as pl
from jax.experimental.pallas import tpu as pltpu

F32 = jnp.float32
BF16 = jnp.bfloat16

D_MODEL = 1024
D_MAIN = 1024
D_MEM = 512
D_MIX = D_MAIN + D_MEM
D_IN = 2 * D_MIX
N_MEM = 256
MEM_HEADS = 4
MEM_HEAD_DIM = 128
FOX_HEADS = 16
FOX_HEAD_DIM = 64
FOX_SCALE = 0.125
POOL_WINDOWS = (2, 4, 8, 16)
POOL_GROUP = 256
POOL_HALO = 16
ALPHA = 4.0 ** 0.25
LN_EPS = 1e-5
LANES = 128
N_CHIPS = 4

ADAM_LR = 0.001
ADAM_B1 = 0.9
ADAM_B2 = 0.999
ADAM_EPS = 1e-08
ADAM_WD = 0.01
ADAM_STEP = 10

VMEM_LIMIT = 56 * 1024 * 1024

ROWS_W_IN = 2 * D_MODEL * (D_IN // N_CHIPS) // 1024
ROWS_W_MKV = 2 * (D_MODEL // N_CHIPS) * 2 * D_MEM // 1024
ROWS_W_OUT = 2 * (D_MIX // N_CHIPS) * D_MODEL // 1024
ROWS_POOL_W = 4 * (POOL_GROUP // N_CHIPS) * POOL_GROUP // 1024
KV_COLS = 2 * D_MAIN + FOX_HEADS
KV_SHARD = KV_COLS // N_CHIPS
ROWS_W_KV = 528
ROWS_SMALL = 16
OFF_W_IN = 0
OFF_W_MKV = OFF_W_IN + ROWS_W_IN
OFF_W_OUT = OFF_W_MKV + ROWS_W_MKV
OFF_POOL_W = OFF_W_OUT + ROWS_W_OUT
OFF_W_KV = OFF_POOL_W + ROWS_POOL_W
OFF_POOL_S = OFF_W_KV + ROWS_W_KV
OFF_LN_G = OFF_POOL_S + ROWS_SMALL
OFF_LN_B = OFF_LN_G + ROWS_SMALL
OFF_BF = OFF_LN_B + ROWS_SMALL
PACK_ROWS = 3584
HALF_ROWS = PACK_ROWS // 2
PACK_TILE = 256
ROWS_PAD = PACK_ROWS - (OFF_BF + ROWS_SMALL)


def _dot(a, b):
    return jnp.dot(a, b, preferred_element_type=F32)


def _dot_nt(a, b):
    return lax.dot_general(a, b, (((1,), (1,)), ((), ())), preferred_element_type=F32)


def _dot_tn(a, b):
    return lax.dot_general(a, b, (((0,), (0,)), ((), ())), preferred_element_type=F32)


def _params(n_axes=1):
    return pltpu.CompilerParams(dimension_semantics=("arbitrary",) * n_axes,
                                vmem_limit_bytes=VMEM_LIMIT)


def _const_spec(shape):
    zeros = (0,) * len(shape)
    return pl.BlockSpec(shape, lambda *_: zeros, pipeline_mode=pl.Buffered(1))


def _split3(x):
    hi = x.astype(BF16)
    r = x - hi.astype(F32)
    mid = r.astype(BF16)
    lo = (r - mid.astype(F32)).astype(BF16)
    return hi, mid, lo


def _linear_fwd(x, w, *, tm, name):
    S, K = x.shape
    N = w.shape[1]
    nc = 512 if N % 512 == 0 else N

    def body(x_ref, w_ref, o_ref):
        xb = x_ref[...].astype(BF16)
        for n0 in range(0, N, nc):
            o_ref[:, n0:n0 + nc] = _dot(xb, w_ref[:, n0:n0 + nc]).astype(BF16)

    return pl.pallas_call(
        body, name=name, grid=(S // tm,),
        in_specs=[pl.BlockSpec((tm, K), lambda i: (i, 0)), _const_spec((K, N))],
        out_specs=pl.BlockSpec((tm, N), lambda i: (i, 0)),
        out_shape=jax.ShapeDtypeStruct((S, N), BF16),
        compiler_params=_params(),
    )(x, w)


def _kv_proj(x1, w_kv, bf_row, *, tm):
    S = x1.shape[0]

    def body(x_ref, w_ref, b_ref, k_ref, v_ref, fl_ref, cum_ref, carry_ref):
        i = pl.program_id(0)

        @pl.when(i == 0)
        def _():
            carry_ref[...] = jnp.zeros_like(carry_ref)

        xb = x_ref[...].astype(BF16)
        for n0 in range(0, D_MAIN, 512):
            k_ref[:, n0:n0 + 512] = _dot(xb, w_ref[:, n0:n0 + 512]).astype(BF16)
            v_ref[:, n0:n0 + 512] = _dot(xb, w_ref[:, D_MAIN + n0:D_MAIN + n0 + 512]).astype(BF16)
        fl = _dot(xb, w_ref[:, 2 * D_MAIN:2 * D_MAIN + LANES]) + b_ref[...]
        fl_ref[...] = fl
        log_f = jnp.minimum(fl, 0.0) - jnp.log1p(jnp.exp(-jnp.abs(fl)))
        r = lax.broadcasted_iota(jnp.int32, (tm, tm), 0)
        c = lax.broadcasted_iota(jnp.int32, (tm, tm), 1)
        tri = jnp.where(c <= r, 1.0, 0.0).astype(BF16)
        hi, mid, lo = _split3(log_f)
        cum = (_dot(tri, hi) + _dot(tri, mid)) + _dot(tri, lo) + carry_ref[0:1, :]
        cum_ref[...] = cum
        carry_ref[0:1, :] = cum[tm - 1:tm, :]

    return pl.pallas_call(
        body, name="kv_proj", grid=(S // tm,),
        in_specs=[pl.BlockSpec((tm, D_MODEL), lambda i: (i, 0)),
                  _const_spec((D_MODEL, 2 * D_MAIN + LANES)), _const_spec((1, LANES))],
        out_specs=[pl.BlockSpec((tm, D_MAIN), lambda i: (i, 0)),
                   pl.BlockSpec((tm, D_MAIN), lambda i: (i, 0)),
                   pl.BlockSpec((tm, LANES), lambda i: (i, 0)),
                   pl.BlockSpec((tm, LANES), lambda i: (i, 0))],
        out_shape=[jax.ShapeDtypeStruct((S, D_MAIN), BF16), jax.ShapeDtypeStruct((S, D_MAIN), BF16),
                   jax.ShapeDtypeStruct((S, LANES), F32), jax.ShapeDtypeStruct((S, LANES), F32)],
        scratch_shapes=[pltpu.VMEM((8, LANES), F32)],
        compiler_params=_params(),
    )(x1, w_kv, bf_row)


def _gate_bwd(dq_aug, dck, fl, *, tm):
    S = fl.shape[0]
    n = S // tm

    def body(dq_ref, dck_ref, fl_ref, du_ref, df_ref, db_ref, carry_ref):
        i = pl.program_id(0)

        @pl.when(i == 0)
        def _():
            carry_ref[...] = jnp.zeros_like(carry_ref)
            db_ref[...] = jnp.zeros_like(db_ref)

        lane = lax.broadcasted_iota(jnp.int32, (tm, LANES), 1)
        half0 = lane < FOX_HEAD_DIM
        dcq = jnp.zeros((tm, LANES), F32)
        for hp in range(FOX_HEADS // 2):
            b0 = dq_ref[:, 2 * hp * LANES:(2 * hp + 1) * LANES]
            b1 = dq_ref[:, (2 * hp + 1) * LANES:(2 * hp + 2) * LANES]
            du_ref[:, hp * LANES:(hp + 1) * LANES] = (
                jnp.where(half0, b0, pltpu.roll(b1, FOX_HEAD_DIM, 1)) * FOX_SCALE).astype(BF16)
            r0 = jnp.sum(jnp.where(lane == AUG_A, b0, 0.0), axis=1, keepdims=True)
            r1 = jnp.sum(jnp.where(lane == AUG_A, b1, 0.0), axis=1, keepdims=True)
            dcq = dcq + jnp.where(lane == 2 * hp, r0, 0.0) + jnp.where(lane == 2 * hp + 1, r1, 0.0)
        dcum = dcq - dck_ref[...]
        r = lax.broadcasted_iota(jnp.int32, (tm, tm), 0)
        c = lax.broadcasted_iota(jnp.int32, (tm, tm), 1)
        tri = jnp.where(c >= r, 1.0, 0.0).astype(BF16)
        hi, mid, lo = _split3(dcum)
        rev = (_dot(tri, hi) + _dot(tri, mid)) + _dot(tri, lo) + carry_ref[0:1, :]
        carry_ref[0:1, :] = rev[0:1, :]
        fl_v = fl_ref[...]
        df = rev * (1.0 / (1.0 + jnp.exp(fl_v)))
        df_ref[...] = df
        db_ref[...] += jnp.sum(df, axis=0, keepdims=True)

    return pl.pallas_call(
        body, name="gate_bwd", grid=(n,),
        in_specs=[pl.BlockSpec((tm, AUG_W), lambda i: (n - 1 - i, 0)),
                  pl.BlockSpec((tm, LANES), lambda i: (n - 1 - i, 0)),
                  pl.BlockSpec((tm, LANES), lambda i: (n - 1 - i, 0))],
        out_specs=[pl.BlockSpec((tm, D_MAIN), lambda i: (n - 1 - i, 0)),
                   pl.BlockSpec((tm, LANES), lambda i: (n - 1 - i, 0)),
                   pl.BlockSpec((1, LANES), lambda i: (0, 0))],
        out_shape=[jax.ShapeDtypeStruct((S, D_MAIN), BF16),
                   jax.ShapeDtypeStruct((S, LANES), F32), jax.ShapeDtypeStruct((1, LANES), F32)],
        scratch_shapes=[pltpu.VMEM((8, LANES), F32)],
        compiler_params=_params(),
    )(dq_aug, dck, fl)


def _silu_and_grad(g):
    sg = 1.0 / (1.0 + jnp.exp(-g))
    return g * sg, sg * (1.0 + g * (1.0 - sg))


def _mix(kind, mode, *, h, xres=None, mkv, w_out, ln_g, ln_b=None, pool_w=None, pool_scale=None,
         ymain=None, target=None, z=None, dy=None, tm):
    S = h.shape[0]
    n = S // tm
    pool = kind == "pool"
    bwd = mode == "bwd"
    loss_head = bwd and not pool
    rev = pool and bwd
    mem_scale = MEM_HEAD_DIM ** -0.5

    def t_of(i):
        return (n - 1 - i) if rev else i

    row = lambda i: (t_of(i), 0)
    names, arrays, specs = [], [], []

    def add(name, arr, spec):
        names.append(name)
        arrays.append(arr)
        specs.append(spec)

    add("h", h, pl.BlockSpec((tm, D_IN), row))
    if pool:
        hb = tm // POOL_HALO
        add("halo", h, pl.BlockSpec((POOL_HALO, D_MAIN), lambda i: (jnp.maximum(t_of(i) * hb - 1, 0), 0)))
        add("pool_w", pool_w, _const_spec((4, POOL_GROUP, POOL_GROUP)))
        add("pool_scale", pool_scale, _const_spec((1, D_MAIN)))
    else:
        add("ymain", ymain, pl.BlockSpec((tm, D_MAIN), row))
    add("mkv", mkv, _const_spec((N_MEM, 2 * D_MEM)))
    add("w_out", w_out, _const_spec((D_MIX, D_MODEL)))
    add("ln_g", ln_g, _const_spec((1, D_MODEL)))
    if not (pool and bwd):
        add("xres", xres, pl.BlockSpec((tm, D_MODEL), row))
        add("ln_b", ln_b, _const_spec((1, D_MODEL)))
    if loss_head:
        add("target", target, pl.BlockSpec((tm, D_MODEL), row))
    if pool and bwd:
        add("z", z, pl.BlockSpec((tm, D_MODEL), row))
        add("dy", dy, pl.BlockSpec((tm, D_MODEL), row))

    onames, oshapes, ospecs = [], [], []

    def add_out(name, shape, dtype, spec):
        onames.append(name)
        oshapes.append(jax.ShapeDtypeStruct(shape, dtype))
        ospecs.append(spec)

    const2 = lambda i: (0, 0)
    if not bwd:
        add_out("z", (S, D_MODEL), F32, pl.BlockSpec((tm, D_MODEL), row))
        add_out("xout", (S, D_MODEL), F32, pl.BlockSpec((tm, D_MODEL), row))
    else:
        add_out("dz", (S, D_MODEL), F32, pl.BlockSpec((tm, D_MODEL), row))
        if pool:
            add_out("dmain", (S, D_MAIN), BF16, pl.BlockSpec((tm, D_MAIN), row))
        else:
            add_out("dmain", (S, 2 * D_MAIN), BF16, pl.BlockSpec((tm, 2 * D_MAIN), row))
        add_out("drest", (S, D_IN - D_MAIN), BF16, pl.BlockSpec((tm, D_IN - D_MAIN), row))
        add_out("dw_out", (D_MIX, D_MODEL), F32, pl.BlockSpec((D_MIX, D_MODEL), const2))
        add_out("dmkv", (N_MEM, 2 * D_MEM), F32, pl.BlockSpec((N_MEM, 2 * D_MEM), const2))
        add_out("dln_g", (1, D_MODEL), F32, pl.BlockSpec((1, D_MODEL), const2))
        add_out("dln_b", (1, D_MODEL), F32, pl.BlockSpec((1, D_MODEL), const2))
        if pool:
            add_out("dpool_w", (4, POOL_GROUP, POOL_GROUP), F32,
                    pl.BlockSpec((4, POOL_GROUP, POOL_GROUP), lambda i: (0, 0, 0)))
            add_out("dpool_scale", (1, D_MAIN), F32, pl.BlockSpec((1, D_MAIN), const2))
        else:
            add_out("loss", (1, D_MODEL), F32, pl.BlockSpec((1, D_MODEL), const2))
            add_out("drow", (FOX_HEADS, S), F32, pl.BlockSpec((FOX_HEADS, tm), lambda i: (0, i)))

    scratch = [pltpu.VMEM((tm, D_MIX), BF16),
               pltpu.VMEM((tm, D_MEM), F32)]
    if pool:
        scratch.append(pltpu.VMEM((tm + 2 * POOL_HALO, D_MAIN), F32))
    if rev:
        scratch.append(pltpu.VMEM((POOL_HALO, D_MAIN), F32))
    n_in, n_out = len(names), len(onames)

    def body(*refs):
        R = dict(zip(names, refs[:n_in]))
        O = dict(zip(onames, refs[n_in:n_in + n_out]))
        sc = refs[n_in + n_out:]
        yc_ref, ymem_ref = sc[0], sc[1]
        ext_ref = sc[2] if pool else None
        carry_ref = sc[3] if rev else None
        i = pl.program_id(0)
        t = t_of(i)
        h_ref = R["h"]
        gamma = R["ln_g"][...]

        if bwd:
            @pl.when(i == 0)
            def _():
                for nm in ("dw_out", "dmkv", "dln_g", "dln_b", "dpool_w", "dpool_scale", "loss"):
                    if nm in O:
                        O[nm][...] = jnp.zeros_like(O[nm])
                if rev:
                    carry_ref[...] = jnp.zeros_like(carry_ref)

        if pool:
            u = h_ref[:, 0:D_MAIN].astype(F32)
            halo = R["halo"][...].astype(F32)
            ext_ref[0:POOL_HALO, :] = jnp.where(t > 0, halo, 0.0)
            ext_ref[POOL_HALO:POOL_HALO + tm, :] = u
            tpos = t * tm + lax.broadcasted_iota(jnp.int32, (tm, 1), 0)
            pms, invcs = [], []
            for gi, w in enumerate(POOL_WINDOWS):
                cs = slice(gi * POOL_GROUP, (gi + 1) * POOL_GROUP)
                acc = ext_ref[POOL_HALO:POOL_HALO + tm, cs]
                for k in range(1, w):
                    acc = acc + ext_ref[POOL_HALO - k:POOL_HALO - k + tm, cs]
                invc = 1.0 / jnp.minimum(tpos + 1, w).astype(F32)
                pm = (acc * invc - u[:, cs]).astype(BF16)
                pms.append(pm)
                invcs.append(invc)
            mixed = [_dot(pms[gi], R["pool_w"][gi]) for gi in range(4)]
            ps = R["pool_scale"][...]
            y_main = [mixed[gi] * ps[:, gi * POOL_GROUP:(gi + 1) * POOL_GROUP] for gi in range(4)]
        else:
            y_main = [R["ymain"][:, gi * 256:(gi + 1) * 256].astype(F32) for gi in range(4)]

        probs = []
        for hd in range(MEM_HEADS):
            sl = slice(D_MAIN + hd * MEM_HEAD_DIM, D_MAIN + (hd + 1) * MEM_HEAD_DIM)
            ksl = slice(hd * MEM_HEAD_DIM, (hd + 1) * MEM_HEAD_DIM)
            vsl = slice(D_MEM + hd * MEM_HEAD_DIM, D_MEM + (hd + 1) * MEM_HEAD_DIM)
            s = _dot_nt(h_ref[:, sl], R["mkv"][:, ksl]) * mem_scale
            e = jnp.exp(s - jnp.max(s, axis=1, keepdims=True))
            p = e / jnp.sum(e, axis=1, keepdims=True)
            probs.append(p)
            ymem_ref[:, ksl] = _dot(p.astype(BF16), R["mkv"][:, vsl])

        g_off = D_MIX
        gate_d = []
        for gi in range(4):
            cs = slice(gi * 256, (gi + 1) * 256)
            gm = h_ref[:, g_off + gi * 256:g_off + (gi + 1) * 256].astype(F32)
            sv, sd = _silu_and_grad(gm)
            yc_ref[:, cs] = (y_main[gi] * sv).astype(BF16)
            gate_d.append((sv, sd))
        gq = h_ref[:, g_off + D_MAIN:D_IN].astype(F32)
        svq, sdq = _silu_and_grad(gq)
        yc_ref[:, D_MAIN:D_MIX] = (ymem_ref[...] * svq).astype(BF16)

        if pool and bwd:
            zt = R["z"][...]
        else:
            o = _dot(yc_ref[...], R["w_out"][...])
            zt = ALPHA * R["xres"][...] + o
        mu = jnp.mean(zt, axis=1, keepdims=True)
        zc = zt - mu
        var = jnp.mean(zc * zc, axis=1, keepdims=True)
        rstd = lax.rsqrt(var + LN_EPS)
        xhat = zc * rstd
        if not bwd:
            O["z"][...] = zt
            O["xout"][...] = xhat * gamma + R["ln_b"][...]
            return

        if loss_head:
            xo = xhat * gamma + R["ln_b"][...]
            err = xo - R["target"][...]
            O["loss"][...] += jnp.sum(err * err, axis=0, keepdims=True)
            dyt = err * (1.0 / D_MODEL)
        else:
            dyt = R["dy"][...]

        O["dln_g"][...] += jnp.sum(dyt * xhat, axis=0, keepdims=True)
        O["dln_b"][...] += jnp.sum(dyt, axis=0, keepdims=True)
        gdy = dyt * gamma
        m1 = jnp.mean(gdy, axis=1, keepdims=True)
        m2 = jnp.mean(gdy * xhat, axis=1, keepdims=True)
        dz = rstd * (gdy - m1 - xhat * m2)
        O["dz"][...] = dz
        dzb = dz.astype(BF16)

        for n0 in range(0, D_MIX, 512):
            O["dw_out"][n0:n0 + 512, :] += _dot_tn(yc_ref[:, n0:n0 + 512], dzb)
        dyc_mem = _dot_nt(dzb, R["w_out"][D_MAIN:D_MIX, :])

        O["drest"][:, D_MEM + D_MAIN:D_MEM + D_MAIN + D_MEM] = (dyc_mem * ymem_ref[...] * sdq).astype(BF16)
        dymem = dyc_mem * svq
        for hd in range(MEM_HEADS):
            sl = slice(D_MAIN + hd * MEM_HEAD_DIM, D_MAIN + (hd + 1) * MEM_HEAD_DIM)
            ksl = slice(hd * MEM_HEAD_DIM, (hd + 1) * MEM_HEAD_DIM)
            vsl = slice(D_MEM + hd * MEM_HEAD_DIM, D_MEM + (hd + 1) * MEM_HEAD_DIM)
            p = probs[hd]
            dyb = dymem[:, ksl].astype(BF16)
            dp = _dot_nt(dyb, R["mkv"][:, vsl])
            ds = p * (dp - jnp.sum(dp * p, axis=1, keepdims=True)) * mem_scale
            dsb = ds.astype(BF16)
            O["drest"][:, ksl] = _dot(dsb, R["mkv"][:, ksl]).astype(BF16)
            O["dmkv"][:, ksl] += _dot_tn(dsb, h_ref[:, sl])
            O["dmkv"][:, vsl] += _dot_tn(p.astype(BF16), dyb)

        dmain = []
        for gi in range(4):
            cs = slice(gi * 256, (gi + 1) * 256)
            dyc_g = _dot_nt(dzb, R["w_out"][cs, :])
            sv, sd = gate_d[gi]
            O["drest"][:, D_MEM + gi * 256:D_MEM + (gi + 1) * 256] = (dyc_g * y_main[gi] * sd).astype(BF16)
            dmain.append(dyc_g * sv)

        if not pool:
            prod = []
            lane2 = lax.broadcasted_iota(jnp.int32, (tm, LANES), 1)
            for gi in range(4):
                cs = slice(gi * 256, (gi + 1) * 256)
                db16 = dmain[gi].astype(BF16)
                dbf = db16.astype(F32)
                prod.append(dbf * R["ymain"][:, cs].astype(F32))
                for pr in range(2):
                    blk = dbf[:, pr * LANES:(pr + 1) * LANES]
                    base = (4 * gi + 2 * pr) * LANES
                    O["dmain"][:, base:base + LANES] = jnp.where(lane2 < FOX_HEAD_DIM, blk, 0.0).astype(BF16)
                    O["dmain"][:, base + LANES:base + 2 * LANES] = jnp.where(
                        lane2 < FOX_HEAD_DIM, pltpu.roll(blk, FOX_HEAD_DIM, 1), 0.0).astype(BF16)
            dcol = jnp.zeros((tm, LANES), F32)
            for gi in range(4):
                dr = lax.broadcasted_iota(jnp.int32, (256, LANES), 0)
                hc = lax.broadcasted_iota(jnp.int32, (256, LANES), 1)
                sel = jnp.where(jnp.right_shift(dr, 6) + gi * 4 == hc, 1.0, 0.0).astype(BF16)
                hi, mid, lo = _split3(prod[gi])
                dcol = dcol + ((_dot(hi, sel) + _dot(mid, sel)) + _dot(lo, sel))
            O["drow"][...] = dcol.T[0:FOX_HEADS, :]
            return

        ps = R["pool_scale"][...]
        dpm_list = []
        for gi in range(4):
            cs = slice(gi * 256, (gi + 1) * 256)
            O["dpool_scale"][:, cs] += jnp.sum(dmain[gi] * mixed[gi], axis=0, keepdims=True)
            dmix = (dmain[gi] * ps[:, cs]).astype(BF16)
            O["dpool_w"][gi] += _dot_tn(pms[gi], dmix)
            dpm = _dot_nt(dmix, R["pool_w"][gi])
            dpm_list.append(dpm)
            ext_ref[0:tm, cs] = dpm * invcs[gi]
        ext_ref[tm:tm + POOL_HALO, :] = carry_ref[...]
        carry_ref[...] = ext_ref[0:POOL_HALO, :]
        for gi, w in enumerate(POOL_WINDOWS):
            cs = slice(gi * 256, (gi + 1) * 256)
            acc = ext_ref[0:tm, cs]
            for k in range(1, w):
                acc = acc + ext_ref[k:k + tm, cs]
            O["dmain"][:, cs] = (acc - dpm_list[gi]).astype(BF16)

    outs = pl.pallas_call(
        body, name=f"mix_{kind}_{mode}", grid=(n,),
        in_specs=specs, out_specs=ospecs, out_shape=oshapes,
        scratch_shapes=scratch, compiler_params=_params(),
    )(*arrays)
    return dict(zip(onames, outs))


def _lin_bwd(xin, dhs, w, res, *, tm, name):
    S, K = xin.shape
    N = w.shape[1]
    nj = len(dhs)
    nr = len(res)
    widths = [dh.shape[1] for dh in dhs]
    assert sum(widths) == N
    scales = [s for _, s in res]

    def body(*refs):
        x_ref = refs[0]
        dh_refs = refs[1:1 + nj]
        w_ref = refs[1 + nj]
        r_refs = refs[2 + nj:2 + nj + nr]
        dx_ref, dw_ref = refs[2 + nj + nr], refs[3 + nj + nr]
        i = pl.program_id(0)

        @pl.when(i == 0)
        def _():
            dw_ref[...] = jnp.zeros_like(dw_ref)

        xb = x_ref[...].astype(BF16)
        dx = jnp.zeros((tm, K), F32)
        for r_ref, sc in zip(r_refs, scales):
            dx = dx + sc * r_ref[...]
        off = 0
        for j in range(nj):
            nc = 512 if widths[j] % 512 == 0 else widths[j]
            for n0 in range(0, widths[j], nc):
                dhb = dh_refs[j][:, n0:n0 + nc].astype(BF16)
                dx = dx + _dot_nt(dhb, w_ref[:, off + n0:off + n0 + nc])
                dw_ref[:, off + n0:off + n0 + nc] += _dot_tn(xb, dhb)
            off += widths[j]
        dx_ref[...] = dx

    in_specs = [pl.BlockSpec((tm, K), lambda i: (i, 0))]
    in_specs += [pl.BlockSpec((tm, n), lambda i: (i, 0)) for n in widths]
    in_specs += [_const_spec((K, N))]
    in_specs += [pl.BlockSpec((tm, K), lambda i: (i, 0)) for _ in res]
    return pl.pallas_call(
        body, name=name, grid=(S // tm,),
        in_specs=in_specs,
        out_specs=[pl.BlockSpec((tm, K), lambda i: (i, 0)), pl.BlockSpec((K, N), lambda i: (0, 0))],
        out_shape=[jax.ShapeDtypeStruct((S, K), F32), jax.ShapeDtypeStruct((K, N), F32)],
        compiler_params=_params(),
    )(xin, *dhs, w, *[r for r, _ in res])


def _wgrad(xin, dh, *, name):
    M, K = xin.shape
    N = dh.shape[1]

    def body(x_ref, dh_ref, o_ref):
        o_ref[...] = _dot_tn(x_ref[...].astype(BF16), dh_ref[...].astype(BF16))

    return pl.pallas_call(
        body, name=name, out_shape=jax.ShapeDtypeStruct((K, N), F32),
        compiler_params=pltpu.CompilerParams(vmem_limit_bytes=VMEM_LIMIT),
    )(xin, dh)


AUG_A = FOX_HEAD_DIM
AUG_B = FOX_HEAD_DIM + 3
AUG_W = FOX_HEADS * LANES


def _placement(val_lane, ones_lane):
    r = jnp.arange(LANES)[:, None]
    c = jnp.arange(AUG_W)[None, :]
    head, lane = c // LANES, c % LANES
    m = jnp.zeros((LANES, AUG_W), jnp.bool_)
    if val_lane is not None:
        for part in range(3):
            m = m | ((r == part * FOX_HEADS + head) & (lane == val_lane + part))
    if ones_lane is not None:
        m = m | ((r == 3 * FOX_HEADS) & (lane >= ones_lane) & (lane < ones_lane + 3))
    return m.astype(BF16)


def _augment(x, cols, signs, *, val_lane, ones_lane, scale, tm, name):
    S = x.shape[0]
    place = _placement(val_lane if cols else None, ones_lane)
    nc = len(cols)

    def body(*refs):
        x_ref = refs[0]
        col_refs = refs[1:1 + nc]
        p_ref = refs[1 + nc]
        o_ref = refs[2 + nc]
        lane = lax.broadcasted_iota(jnp.int32, (tm, LANES), 1)
        data = lane < FOX_HEAD_DIM
        lhs = jnp.where(lane == 3 * FOX_HEADS, 1.0, 0.0)
        if nc:
            val = signs[0] * col_refs[0][...]
            for i in range(1, nc):
                val = val + signs[i] * col_refs[i][...]
            hi, mid, lo = [p.astype(F32) for p in _split3(val)]
            lhs = jnp.where(lane < FOX_HEADS, hi, jnp.where(
                lane < 2 * FOX_HEADS, pltpu.roll(mid, FOX_HEADS, 1), jnp.where(
                    lane < 3 * FOX_HEADS, pltpu.roll(lo, 2 * FOX_HEADS, 1), lhs)))
        lhs = lhs.astype(BF16)
        for hp in range(FOX_HEADS // 2):
            extra = _dot(lhs, p_ref[:, 2 * hp * LANES:(2 * hp + 2) * LANES])
            blk = x_ref[:, hp * LANES:(hp + 1) * LANES].astype(F32) * scale
            o_ref[:, 2 * hp * LANES:(2 * hp + 1) * LANES] = jnp.where(data, blk, extra[:, 0:LANES]).astype(BF16)
            o_ref[:, (2 * hp + 1) * LANES:(2 * hp + 2) * LANES] = jnp.where(
                data, pltpu.roll(blk, FOX_HEAD_DIM, 1), extra[:, LANES:2 * LANES]).astype(BF16)

    in_specs = [pl.BlockSpec((tm, D_MAIN), lambda i: (i, 0))]
    in_specs += [pl.BlockSpec((tm, LANES), lambda i: (i, 0)) for _ in cols]
    in_specs += [_const_spec((LANES, AUG_W))]
    return pl.pallas_call(
        body, name=name, grid=(S // tm,),
        in_specs=in_specs, out_specs=pl.BlockSpec((tm, AUG_W), lambda i: (i, 0)),
        out_shape=jax.ShapeDtypeStruct((S, AUG_W), BF16),
        compiler_params=_params(),
    )(x, *cols, place)


def _cols_of_rows(rows, S):
    nh = FOX_HEADS // rows.shape[0]
    a = rows[:, :, 0:nh, :].transpose(0, 2, 1, 3).reshape(FOX_HEADS, S).T
    return jnp.pad(a, ((0, 0), (0, LANES - FOX_HEADS)))


def _fox_fwd(qf, ka, va, *, tq, nh=4):
    S = ka.shape[0]
    nq = S // tq
    tk = tq
    ng = FOX_HEADS // nh

    def body(q_ref, k_ref, v_ref, o_ref, lse_ref, *scratch):
        p_scr, m_scr, acc_scr = scratch[0:nh], scratch[nh:2 * nh], scratch[2 * nh:3 * nh]
        qi = pl.program_id(1)
        lane = lax.broadcasted_iota(jnp.int32, (tq, LANES), 1)
        half0 = lane < FOX_HEAD_DIM
        rr = lax.broadcasted_iota(jnp.int32, (tq, tk), 0)
        cc = lax.broadcasted_iota(jnp.int32, (tq, tk), 1)
        sls = [slice(hh * LANES, (hh + 1) * LANES) for hh in range(nh)]
        qs = [q_ref[:, sl] for sl in sls]

        for hh in range(nh):
            p_scr[hh][...] = jnp.zeros_like(p_scr[hh])
            m_scr[hh][...] = jnp.full(m_scr[hh].shape, -jnp.inf, F32)
            acc_scr[hh][...] = jnp.zeros_like(acc_scr[hh])

        def chunk(ki, masked):
            k0 = pl.multiple_of(ki * tk, tk)
            kp = pl.multiple_of(jnp.maximum(ki - 1, 0) * tk, tk)
            for hh in range(nh):
                m = m_scr[hh][...]
                s = _dot_nt(qs[hh], k_ref[pl.ds(k0, tk), sls[hh]])
                pv = _dot(p_scr[hh][...], v_ref[pl.ds(kp, tk), sls[hh]])
                if masked:
                    s = jnp.where(cc <= rr, s, -jnp.inf)
                m_new = jnp.maximum(m, jnp.max(s, axis=1, keepdims=True))
                p_scr[hh][...] = jnp.exp(s - jnp.tile(m_new, (1, tk // LANES))).astype(BF16)
                acc_scr[hh][...] = (acc_scr[hh][...] + pv) * jnp.exp(m - m_new)
                m_scr[hh][...] = m_new

        def trip(ki, c):
            chunk(ki, False)
            return c

        lax.fori_loop(0, qi, trip, 0)
        chunk(qi, True)
        kq = pl.multiple_of(qi * tk, tk)
        outs = []
        lse_cols = jnp.zeros((tq, LANES), F32)
        for hh in range(nh):
            m = m_scr[hh][...]
            acc = acc_scr[hh][...] + _dot(p_scr[hh][...], v_ref[pl.ds(kq, tk), sls[hh]])
            l = jnp.sum(jnp.where(lane == AUG_A, acc, 0.0), axis=1, keepdims=True)
            outs.append(acc / l)
            lse_cols = lse_cols + jnp.where(lane == hh, m + jnp.log(l), 0.0)
        for pr in range(nh // 2):
            o_ref[:, pr * LANES:(pr + 1) * LANES] = jnp.where(
                half0, outs[2 * pr], pltpu.roll(outs[2 * pr + 1], FOX_HEAD_DIM, 1))
        lse_ref[0, 0] = lse_cols.T[0:8, :]

    return pl.pallas_call(
        body, name="fox_fwd", grid=(ng, nq),
        in_specs=[pl.BlockSpec((tq, nh * LANES), lambda g, qi: (qi, g)),
                  pl.BlockSpec((S, nh * LANES), lambda g, qi: (0, g), pipeline_mode=pl.Buffered(1)),
                  pl.BlockSpec((S, nh * LANES), lambda g, qi: (0, g), pipeline_mode=pl.Buffered(1))],
        out_specs=[pl.BlockSpec((tq, nh * FOX_HEAD_DIM), lambda g, qi: (qi, g)),
                   pl.BlockSpec((1, 1, 8, tq), lambda g, qi: (g, qi, 0, 0))],
        out_shape=[jax.ShapeDtypeStruct((S, D_MAIN), F32),
                   jax.ShapeDtypeStruct((ng, nq, 8, tq), F32)],
        scratch_shapes=([pltpu.VMEM((tq, tk), BF16)] * nh + [pltpu.VMEM((tq, LANES), F32)] * nh
                        + [pltpu.VMEM((tq, LANES), F32)] * nh),
        compiler_params=_params(2),
    )(qf, ka, va)


def _rows_for_pairs(a16, nt, tt):
    a = a16.reshape(8, 2, nt, tt).transpose(0, 2, 1, 3)
    return jnp.pad(a, ((0, 0), (0, 0), (0, 6), (0, 0)))


def _fox_bwd(qf, ka, va, do_aug, lse_rows, d_rows, *, tq):
    S = ka.shape[0]
    nq = S // tq
    tk = tq

    def body(k_ref, v_ref, q_ref, do_ref, lser_ref, dr_ref, dq_ref, dk_ref, dv_ref, dck_ref, dk_scr, dv_scr):
        kj = pl.program_id(1)

        @pl.when(kj == 0)
        def _():
            dq_ref[...] = jnp.zeros_like(dq_ref)

        lane = lax.broadcasted_iota(jnp.int32, (tk, LANES), 1)
        half0 = lane < FOX_HEAD_DIM
        rr = lax.broadcasted_iota(jnp.int32, (tk, tq), 0)
        cc = lax.broadcasted_iota(jnp.int32, (tk, tq), 1)
        sls = [slice(hh * LANES, (hh + 1) * LANES) for hh in range(2)]
        kts = [k_ref[:, sl] for sl in sls]
        vts = [v_ref[:, sl] for sl in sls]

        dk_scr[...] = jnp.zeros_like(dk_scr)
        dv_scr[...] = jnp.zeros_like(dv_scr)

        def chunk(qi, masked):
            q0 = pl.multiple_of(qi * tq, tq)
            for hh in range(2):
                qc = q_ref[pl.ds(q0, tq), sls[hh]]
                doc = do_ref[pl.ds(q0, tq), sls[hh]]
                lse = lser_ref[0, qi][hh:hh + 1, :]
                dd = dr_ref[0, qi][hh:hh + 1, :]
                pt = jnp.exp(_dot_nt(kts[hh], qc) - lse)
                if masked:
                    pt = jnp.where(rr <= cc, pt, 0.0)
                dsb = (pt * (_dot_nt(vts[hh], doc) - dd)).astype(BF16)
                dv_scr[hh] += _dot(pt.astype(BF16), doc)
                dk_scr[hh] += _dot(dsb, qc)
                dq_ref[pl.ds(q0, tq), sls[hh]] += _dot_tn(dsb, kts[hh])

        def trip(qi, c):
            chunk(qi, False)
            return c

        chunk(kj, True)
        lax.fori_loop(kj + 1, nq, trip, 0)
        dk0, dk1 = dk_scr[0], dk_scr[1]
        dv0, dv1 = dv_scr[0], dv_scr[1]
        dk_ref[...] = jnp.where(half0, dk0, pltpu.roll(dk1, FOX_HEAD_DIM, 1)).astype(BF16)
        dv_ref[...] = jnp.where(half0, dv0, pltpu.roll(dv1, FOX_HEAD_DIM, 1)).astype(BF16)
        c0 = jnp.sum(jnp.where(lane == AUG_B, dk0, 0.0), axis=1, keepdims=True)
        c1 = jnp.sum(jnp.where(lane == AUG_B, dk1, 0.0), axis=1, keepdims=True)
        dck_cols = jnp.where(lane == 0, c0, 0.0) + jnp.where(lane == 1, c1, 0.0)
        dck_ref[0, 0] = dck_cols.T[0:8, :]

    return pl.pallas_call(
        body, name="fox_bwd", grid=(8, nq),
        in_specs=[pl.BlockSpec((tk, 2 * LANES), lambda hp, kj: (kj, hp)),
                  pl.BlockSpec((tk, 2 * LANES), lambda hp, kj: (kj, hp)),
                  pl.BlockSpec((S, 2 * LANES), lambda hp, kj: (0, hp)),
                  pl.BlockSpec((S, 2 * LANES), lambda hp, kj: (0, hp)),
                  pl.BlockSpec((1, nq, 8, tq), lambda hp, kj: (hp, 0, 0, 0)),
                  pl.BlockSpec((1, nq, 8, tq), lambda hp, kj: (hp, 0, 0, 0))],
        out_specs=[pl.BlockSpec((S, 2 * LANES), lambda hp, kj: (0, hp)),
                   pl.BlockSpec((tk, LANES), lambda hp, kj: (kj, hp)),
                   pl.BlockSpec((tk, LANES), lambda hp, kj: (kj, hp)),
                   pl.BlockSpec((1, 1, 8, tk), lambda hp, kj: (hp, kj, 0, 0))],
        out_shape=[jax.ShapeDtypeStruct((S, AUG_W), F32),
                   jax.ShapeDtypeStruct((S, D_MAIN), BF16),
                   jax.ShapeDtypeStruct((S, D_MAIN), BF16),
                   jax.ShapeDtypeStruct((8, nq, 8, tk), F32)],
        scratch_shapes=[pltpu.VMEM((2, tk, LANES), F32), pltpu.VMEM((2, tk, LANES), F32)],
        compiler_params=_params(2),
    )(ka, va, qf, do_aug, lse_rows, d_rows)


def _adamw(w, g, m, v, *, name):
    Rr, C = w.shape
    tr = 256 if Rr % 256 == 0 else Rr
    c1 = 1.0 / (1.0 - ADAM_B1 ** ADAM_STEP)
    c2 = 1.0 / (1.0 - ADAM_B2 ** ADAM_STEP)

    def body(w_ref, g_ref, m_ref, v_ref, d_ref, nm_ref, nv_ref):
        gv = g_ref[...]
        nm = ADAM_B1 * m_ref[...] + (1.0 - ADAM_B1) * gv
        nv = ADAM_B2 * v_ref[...] + (1.0 - ADAM_B2) * (gv * gv)
        d_ref[...] = -ADAM_LR * ((nm * c1) / (jnp.sqrt(nv * c2) + ADAM_EPS) + ADAM_WD * w_ref[...])
        nm_ref[...] = nm
        nv_ref[...] = nv

    spec = pl.BlockSpec((tr, C), lambda i: (i, 0))
    sds = jax.ShapeDtypeStruct((Rr, C), F32)
    return pl.pallas_call(
        body, name=name, grid=(Rr // tr,),
        in_specs=[spec] * 4, out_specs=[spec] * 3, out_shape=[sds] * 3,
        compiler_params=_params(),
    )(w, g, m, v)


_ANY = pl.BlockSpec(memory_space=pl.ANY)
_MESH = pl.DeviceIdType.MESH


def _place():
    x, y, c = lax.axis_index("x"), lax.axis_index("y"), lax.axis_index("c")
    return x, y, c


def _all_gather_shards(pack):
    dt = pack.dtype

    def body(p_ref, out_ref, send_sems, recv_sems):
        x, y, c = _place()
        sib = (x, y, 1 - c)
        chips = [(1 - x, y), (x, 1 - y), (1 - x, 1 - y)]
        me = 2 * x + y

        def copy(k, chip_idx, half, to, src=None):
            dst = out_ref.at[chip_idx, half]
            return pltpu.make_async_remote_copy(
                src_ref=dst if src is None else src, dst_ref=dst,
                send_sem=send_sems.at[k], recv_sem=recv_sems.at[k],
                device_id=to, device_id_type=_MESH)

        first = [copy(j, me, c, (*chip, c), src=p_ref.at[c]) for j, chip in enumerate(chips)]
        for cp in first:
            cp.start()
        passed = [copy(3 + j, 2 * chip[0] + chip[1], c, sib) for j, chip in enumerate(chips)]
        for j, chip in enumerate(chips):
            copy(j, 2 * chip[0] + chip[1], c, sib).wait_recv()
            passed[j].start()
        for j, chip in enumerate(chips):
            copy(3 + j, 2 * chip[0] + chip[1], 1 - c, sib).wait_recv()
        for cp in first + passed:
            cp.wait_send()

    return pl.pallas_call(
        body, name="all_gather_shards",
        in_specs=[_ANY], out_specs=_ANY,
        out_shape=jax.ShapeDtypeStruct((N_CHIPS, 2, HALF_ROWS, 1024), dt),
        scratch_shapes=[pltpu.SemaphoreType.DMA((6,)), pltpu.SemaphoreType.DMA((6,))],
    )(pack)


def _send_half_to_sibling(gpack):
    def body(g_ref, out_ref, send_sem, recv_sem):
        x, y, c = _place()
        sib = (x, y, 1 - c)
        cps = [pltpu.make_async_remote_copy(
            src_ref=g_ref.at[j, 1 - c], dst_ref=out_ref.at[j],
            send_sem=send_sem.at[j], recv_sem=recv_sem.at[j],
            device_id=sib, device_id_type=_MESH) for j in range(N_CHIPS)]
        for cp in cps:
            cp.start()
        for cp in cps:
            cp.wait_recv()
        for cp in cps:
            cp.wait_send()

    return pl.pallas_call(
        body, name="pair_send",
        in_specs=[_ANY], out_specs=_ANY,
        out_shape=jax.ShapeDtypeStruct((N_CHIPS, HALF_ROWS, 1024), F32),
        scratch_shapes=[pltpu.SemaphoreType.DMA((N_CHIPS,)), pltpu.SemaphoreType.DMA((N_CHIPS,))],
    )(gpack)


def _pair_sum(gpack, recv, c_arr, *, tr):
    def body(c_ref, a_ref, b_ref, o_ref):
        o_ref[...] = (a_ref[...] + b_ref[...]).astype(BF16)

    grid_spec = pltpu.PrefetchScalarGridSpec(
        num_scalar_prefetch=1, grid=(N_CHIPS, HALF_ROWS // tr),
        in_specs=[pl.BlockSpec((None, None, tr, 1024), lambda j, i, c_ref: (j, c_ref[0], i, 0)),
                  pl.BlockSpec((None, tr, 1024), lambda j, i, c_ref: (j, i, 0))],
        out_specs=pl.BlockSpec((None, tr, 1024), lambda j, i, c_ref: (j, i, 0)))
    return pl.pallas_call(
        body, name="pair_sum", grid_spec=grid_spec,
        out_shape=jax.ShapeDtypeStruct((N_CHIPS, HALF_ROWS, 1024), BF16),
        compiler_params=_params(2),
    )(c_arr, gpack, recv)


def _scatter_pieces(psum):
    def body(p_ref, out_ref, send_sems, recv_sems):
        x, y, c = _place()
        chips = [(1 - x, y), (x, 1 - y), (1 - x, 1 - y)]
        me = 2 * x + y
        cps = []
        for j, chip in enumerate(chips):
            them = 2 * chip[0] + chip[1]
            cps.append(pltpu.make_async_remote_copy(
                src_ref=p_ref.at[them], dst_ref=out_ref.at[me],
                send_sem=send_sems.at[j], recv_sem=recv_sems.at[j],
                device_id=(*chip, c), device_id_type=_MESH))
        for cp in cps:
            cp.start()
        for cp in cps:
            cp.wait_recv()
        for cp in cps:
            cp.wait_send()

    return pl.pallas_call(
        body, name="scatter_pieces",
        in_specs=[_ANY], out_specs=_ANY,
        out_shape=jax.ShapeDtypeStruct((N_CHIPS, HALF_ROWS, 1024), BF16),
        scratch_shapes=[pltpu.SemaphoreType.DMA((3,)), pltpu.SemaphoreType.DMA((3,))],
    )(psum)


def _sum_pieces(pieces, *, tr):
    def body(p_ref, o_ref):
        acc = p_ref[0].astype(F32) + p_ref[1].astype(F32)
        acc = acc + p_ref[2].astype(F32)
        o_ref[...] = acc + p_ref[3].astype(F32)

    return pl.pallas_call(
        body, name="sum_pieces", grid=(HALF_ROWS // tr,),
        in_specs=[pl.BlockSpec((N_CHIPS, tr, 1024), lambda i: (0, i, 0))],
        out_specs=pl.BlockSpec((tr, 1024), lambda i: (i, 0)),
        out_shape=jax.ShapeDtypeStruct((HALF_ROWS, 1024), F32),
        compiler_params=_params(),
    )(pieces)


def _exchange_halves(total):
    def body(t_ref, out_ref, send_sem, recv_sem):
        x, y, c = _place()
        sib = (x, y, 1 - c)
        cp = pltpu.make_async_remote_copy(
            src_ref=t_ref, dst_ref=out_ref.at[c], send_sem=send_sem, recv_sem=recv_sem,
            device_id=sib, device_id_type=_MESH)
        cp.start()
        cp.wait_recv()
        cp.wait_send()

    return pl.pallas_call(
        body, name="exchange_halves",
        in_specs=[_ANY], out_specs=_ANY,
        out_shape=jax.ShapeDtypeStruct((2, HALF_ROWS, 1024), F32),
        scratch_shapes=[pltpu.SemaphoreType.DMA, pltpu.SemaphoreType.DMA],
    )(total)


def _pad_rows(a, rows):
    return jnp.pad(a, ((0, rows - a.shape[0]), (0, 0)))


def _pack_weight_shard(w_in, w_mem_kv, w_out, pool_w, w_kv_shared, pool_scale):
    ps_bits = lax.bitcast_convert_type(pool_scale.reshape(-1), BF16).reshape(1, -1)
    ps_row = jnp.pad(ps_bits, ((0, 0), (0, 1024 - ps_bits.shape[1])))
    parts = [
        w_in.astype(BF16).reshape(ROWS_W_IN, 1024),
        w_mem_kv.astype(BF16).reshape(ROWS_W_MKV, 1024),
        w_out.astype(BF16).reshape(ROWS_W_OUT, 1024),
        pool_w.astype(BF16).reshape(ROWS_POOL_W, 1024),
        _pad_rows(w_kv_shared.astype(BF16).reshape(KV_SHARD, 1024), ROWS_W_KV),
        _pad_rows(ps_row, ROWS_SMALL),
        jnp.zeros((PACK_ROWS - OFF_LN_G, 1024), BF16),
    ]
    return jnp.concatenate(parts, axis=0).reshape(2, HALF_ROWS, 1024)


def _unpack_weights(g):
    w_in = g[:, OFF_W_IN:OFF_W_IN + ROWS_W_IN].reshape(4, 2, D_MODEL, D_IN // 4)
    w_in = w_in.transpose(1, 2, 0, 3).reshape(2, D_MODEL, D_IN)
    w_mkv = g[:, OFF_W_MKV:OFF_W_MKV + ROWS_W_MKV].reshape(4, 2, D_MODEL // 4, 2 * D_MEM)
    w_mkv = w_mkv.transpose(1, 0, 2, 3).reshape(2, D_MODEL, 2 * D_MEM)
    w_out = g[:, OFF_W_OUT:OFF_W_OUT + ROWS_W_OUT].reshape(4, 2, D_MIX // 4, D_MODEL)
    w_out = w_out.transpose(1, 0, 2, 3).reshape(2, D_MIX, D_MODEL)
    pool_w = g[:, OFF_POOL_W:OFF_POOL_W + ROWS_POOL_W].reshape(4, 4, POOL_GROUP // 4, POOL_GROUP)
    pool_w = pool_w.transpose(1, 0, 2, 3).reshape(4, POOL_GROUP, POOL_GROUP)
    w_kv = g[:, OFF_W_KV:OFF_W_KV + KV_SHARD].reshape(4, D_MODEL, KV_SHARD)
    w_kv = w_kv.transpose(1, 0, 2).reshape(D_MODEL, KV_COLS)
    ps_bits = g[:, OFF_POOL_S, 0:512].reshape(4, 256, 2)
    pool_scale = lax.bitcast_convert_type(ps_bits, F32).reshape(1, D_MAIN)
    return w_in, w_mkv, w_out, pool_w, w_kv, pool_scale


def _pack_grads(g_w_in, g_w_mkv, g_w_out, g_pool_w, g_w_kv, g_pool_scale, g_ln_g, g_ln_b, g_bf):
    def rep(a):
        a = _pad_rows(a, ROWS_SMALL)
        return jnp.broadcast_to(a[None], (4,) + a.shape)

    parts = [g.reshape(D_MODEL, 4, D_IN // 4).transpose(1, 0, 2).reshape(4, ROWS_W_IN // 2, 1024) for g in g_w_in]
    parts += [g.reshape(4, ROWS_W_MKV // 2, 1024) for g in g_w_mkv]
    parts += [g.reshape(4, ROWS_W_OUT // 2, 1024) for g in g_w_out]
    parts += [
        g_pool_w.reshape(4, 4, POOL_GROUP // 4, POOL_GROUP).transpose(1, 0, 2, 3).reshape(4, ROWS_POOL_W, 1024),
        jnp.pad(g_w_kv.reshape(D_MODEL, 4, KV_SHARD).transpose(1, 0, 2).reshape(4, KV_SHARD, 1024),
                ((0, 0), (0, ROWS_W_KV - KV_SHARD), (0, 0))),
        jnp.pad(g_pool_scale.reshape(4, 1, 256), ((0, 0), (0, ROWS_SMALL - 1), (0, 1024 - 256))),
        rep(g_ln_g), rep(g_ln_b),
        rep(jnp.pad(g_bf.reshape(1, -1), ((0, 0), (0, 1024 - g_bf.shape[0])))),
        jnp.zeros((4, ROWS_PAD, 1024), F32),
    ]
    return jnp.concatenate(parts, axis=1).reshape(4, 2, HALF_ROWS, 1024)


def _local_step(x, mem, target, w_in, w_mkv, w_out, pool_w, pool_scale, w_kv, ln_g, ln_b, b_forget,
                *, tm=256, tq=512):
    S = x.shape[0]
    nq = S // tq
    g_rows = [ln_g[l:l + 1] for l in range(2)]
    b_rows = [ln_b[l:l + 1] for l in range(2)]
    w_kvp = jnp.pad(w_kv, ((0, 0), (0, LANES - FOX_HEADS)))
    bf_row = jnp.pad(b_forget.reshape(1, -1), ((0, 0), (0, LANES - FOX_HEADS)))

    mkv = [_linear_fwd(mem, w_mkv[l], tm=N_MEM, name=f"mem_kv{l}") for l in range(2)]

    h0 = _linear_fwd(x, w_in[0], tm=tm, name="in_proj0")
    f0 = _mix("pool", "fwd", h=h0, xres=x, mkv=mkv[0], w_out=w_out[0], ln_g=g_rows[0], ln_b=b_rows[0],
              pool_w=pool_w, pool_scale=pool_scale, tm=tm)
    z0, x1 = f0["z"], f0["xout"]
    k, v, fl, cum = _kv_proj(x1, w_kvp, bf_row, tm=tm)
    h1 = _linear_fwd(x1, w_in[1], tm=tm, name="in_proj1")
    aug = functools.partial(_augment, tm=tm)
    ka = aug(k, [cum], [-1.0], val_lane=AUG_B, ones_lane=AUG_A, scale=1.0, name="aug_k")
    va = aug(v, [], [], val_lane=None, ones_lane=AUG_A, scale=1.0, name="aug_v")
    qf = aug(h1, [cum], [1.0], val_lane=AUG_A, ones_lane=AUG_B, scale=FOX_SCALE, name="aug_q_fwd")
    ymain1, lse_rows = _fox_fwd(qf, ka, va, tq=tq)

    b1 = _mix("fox", "bwd", h=h1, xres=x1, mkv=mkv[1], w_out=w_out[1], ln_g=g_rows[1], ln_b=b_rows[1],
              ymain=ymain1, target=target, tm=tm)
    lse16 = lse_rows[:, :, 0:FOX_HEADS // lse_rows.shape[0], :].transpose(0, 2, 1, 3).reshape(FOX_HEADS, S)
    dq_aug, dk, dv, dck_rows = _fox_bwd(qf, ka, va, b1["dmain"], _rows_for_pairs(lse16, nq, tq),
                                        _rows_for_pairs(b1["drow"], nq, tq), tq=tq)
    du1, df, dbf = _gate_bwd(dq_aug, _cols_of_rows(dck_rows, S), fl, tm=tm)

    dx1a, dw_in1 = _lin_bwd(x1, [du1, b1["drest"]], w_in[1], [(b1["dz"], ALPHA)], tm=tm, name="in_proj1_bwd")
    dx1, dw_kvp = _lin_bwd(x1, [dk, dv, df], w_kvp, [(dx1a, 1.0)], tm=tm, name="kv_proj_bwd")

    b0 = _mix("pool", "bwd", h=h0, mkv=mkv[0], w_out=w_out[0], ln_g=g_rows[0],
              pool_w=pool_w, pool_scale=pool_scale, z=z0, dy=dx1, tm=tm)
    dx, dw_in0 = _lin_bwd(x, [b0["dmain"], b0["drest"]], w_in[0], [(b0["dz"], ALPHA)], tm=tm, name="in_proj0_bwd")
    dw_mkv = [_wgrad(mem, b["dmkv"], name=f"mem_kv{l}_bwd") for l, b in enumerate((b0, b1))]

    grads = dict(
        w_in=[dw_in0, dw_in1],
        w_mem_kv=dw_mkv,
        w_out=[b0["dw_out"], b1["dw_out"]],
        ln_g=jnp.concatenate([b0["dln_g"], b1["dln_g"]], axis=0),
        ln_b=jnp.concatenate([b0["dln_b"], b1["dln_b"]], axis=0),
        pool_w=b0["dpool_w"],
        pool_scale=b0["dpool_scale"],
        w_kv=dw_kvp[:, 0:KV_COLS],
        b_forget=dbf[0, 0:FOX_HEADS],
    )
    return b1["loss"], dx, grads


def kernel(x, mem, w_in, w_mem_kv, w_out, ln_g, ln_b, pool_w, pool_scale, w_kv_shared, b_forget, loss_target, m_w_in, m_w_mem_kv, m_w_out, m_ln_g, m_ln_b, m_pool_w, m_pool_scale, m_w_kv_shared, m_b_forget, v_w_in, v_w_mem_kv, v_w_out, v_ln_g, v_ln_b, v_pool_w, v_pool_scale, v_w_kv_shared, v_b_forget):
    c_arr = lax.axis_index("c").astype(jnp.int32).reshape(1)

    wpack = _pack_weight_shard(w_in, w_mem_kv, w_out, pool_w, w_kv_shared, pool_scale)
    me = 2 * lax.axis_index("x") + lax.axis_index("y")
    my_c = lax.axis_index("c")
    gathered = lax.dynamic_update_slice(_all_gather_shards(wpack), wpack[None], (me, 0, 0, 0))
    gathered = gathered.reshape(N_CHIPS, PACK_ROWS, 1024)
    fw_in, fw_mkv, fw_out, fpool_w, fw_kv, fpool_scale = _unpack_weights(gathered)

    loss_vec, dx, g = _local_step(x[0], mem[0], loss_target[0], fw_in, fw_mkv, fw_out, fpool_w,
                                  fpool_scale, fw_kv, ln_g, ln_b, b_forget)
    loss = lax.psum(0.5 / D_MODEL * jnp.sum(loss_vec), ("x", "y", "c"))

    gpack = _pack_grads(g["w_in"], g["w_mem_kv"], g["w_out"], g["pool_w"], g["w_kv"], g["pool_scale"],
                        g["ln_g"], g["ln_b"], g["b_forget"])
    from_sibling = _send_half_to_sibling(gpack)
    psum = _pair_sum(gpack, from_sibling, c_arr, tr=PACK_TILE)
    own_piece = lax.dynamic_slice(psum, (me, 0, 0), (1, HALF_ROWS, 1024))
    pieces = lax.dynamic_update_slice(_scatter_pieces(psum), own_piece, (me, 0, 0))
    total = _sum_pieces(pieces, tr=PACK_TILE)
    shard = lax.dynamic_update_slice(_exchange_halves(total), total[None], (my_c, 0, 0))
    shard = shard.reshape(PACK_ROWS, 1024)

    g_w_in = shard[OFF_W_IN:OFF_W_IN + ROWS_W_IN].reshape(w_in.shape)
    g_w_mkv = shard[OFF_W_MKV:OFF_W_MKV + ROWS_W_MKV].reshape(w_mem_kv.shape)
    g_w_out = shard[OFF_W_OUT:OFF_W_OUT + ROWS_W_OUT].reshape(w_out.shape)
    g_pool_w = shard[OFF_POOL_W:OFF_POOL_W + ROWS_POOL_W].reshape(pool_w.shape)
    g_w_kv = shard[OFF_W_KV:OFF_W_KV + KV_SHARD].reshape(w_kv_shared.shape)
    g_pool_scale = shard[OFF_POOL_S:OFF_POOL_S + 1, 0:256].reshape(pool_scale.shape)
    g_ln_g = shard[OFF_LN_G:OFF_LN_G + 2]
    g_ln_b = shard[OFF_LN_B:OFF_LN_B + 2]
    g_bf = shard[OFF_BF, 0:FOX_HEADS]

    names = ["w_in", "w_mem_kv", "w_out", "ln_g", "ln_b", "pool_w", "pool_scale", "w_kv_shared", "b_forget"]
    ws = [w_in, w_mem_kv, w_out, ln_g, ln_b, pool_w, pool_scale, w_kv_shared, b_forget]
    gs = [g_w_in, g_w_mkv, g_w_out, g_ln_g, g_ln_b, g_pool_w, g_pool_scale, g_w_kv, g_bf]
    ms = [m_w_in, m_w_mem_kv, m_w_out, m_ln_g, m_ln_b, m_pool_w, m_pool_scale, m_w_kv_shared, m_b_forget]
    vs = [v_w_in, v_w_mem_kv, v_w_out, v_ln_g, v_ln_b, v_pool_w, v_pool_scale, v_w_kv_shared, v_b_forget]
    deltas, new_ms, new_vs = [], [], []
    for nm, w, gg, mm, vv in zip(names, ws, gs, ms, vs):
        two_d = (-1, w.shape[-1])
        d, nmm, nvv = _adamw(w.reshape(two_d), gg.reshape(two_d), mm.reshape(two_d), vv.reshape(two_d),
                             name=f"adamw_{nm}")
        deltas.append(d.reshape(w.shape))
        new_ms.append(nmm.reshape(w.shape))
        new_vs.append(nvv.reshape(w.shape))

    return (loss, dx[None], *gs, *deltas, *new_ms, *new_vs)
```

```python
import functools

import jax
import jax.numpy as jnp
from jax import lax
from jax.experimental import pallas as pl
from jax.experimental.pallas import tpu as pltpu

F32 = jnp.float32
BF16 = jnp.bfloat16

D_MODEL = 1024
D_MAIN = 1024
D_MEM = 512
D_MIX = D_MAIN + D_MEM
D_IN = 2 * D_MIX
N_MEM = 256
MEM_HEADS = 4
MEM_HEAD_DIM = 128
FOX_HEADS = 16
FOX_HEAD_DIM = 64
FOX_SCALE = 0.125
POOL_WINDOWS = (2, 4, 8, 16)
POOL_GROUP = 256
POOL_HALO = 16
ALPHA = 4.0 ** 0.25
LN_EPS = 1e-5
LANES = 128
N_CHIPS = 4

ADAM_LR = 0.001
ADAM_B1 = 0.9
ADAM_B2 = 0.999
ADAM_EPS = 1e-08
ADAM_WD = 0.01
ADAM_STEP = 10

VMEM_LIMIT = 56 * 1024 * 1024

ROWS_W_IN = D_MODEL * (D_IN // N_CHIPS) // 1024
ROWS_W_MKV = (D_MODEL // N_CHIPS) * 2 * D_MEM // 1024
ROWS_W_OUT = (D_MIX // N_CHIPS) * D_MODEL // 1024
ROWS_POOL_W = 4 * (POOL_GROUP // N_CHIPS) * POOL_GROUP // 1024
KV_COLS = 2 * D_MAIN + FOX_HEADS
KV_SHARD = KV_COLS // N_CHIPS
ROWS_W_KV = 528
ROWS_SMALL = 16
OFF_W_IN = 0
OFF_W_MKV = OFF_W_IN + ROWS_W_IN
OFF_W_OUT = OFF_W_MKV + ROWS_W_MKV
OFF_TAIL = OFF_W_OUT + ROWS_W_OUT
OFF_POOL_W = OFF_TAIL
OFF_POOL_S = OFF_POOL_W + ROWS_POOL_W
OFF_LN_G = OFF_POOL_S + ROWS_SMALL
OFF_LN_B = OFF_LN_G + ROWS_SMALL
PACK0_ROWS = 1536
OFF_W_KV = OFF_TAIL
OFF_BF = OFF_W_KV + ROWS_W_KV
PACK1_ROWS = 2048
PACK_TILE = 256


def _dot(a, b):
    return jnp.dot(a, b, preferred_element_type=F32)


def _dot_nt(a, b):
    return lax.dot_general(a, b, (((1,), (1,)), ((), ())), preferred_element_type=F32)


def _dot_tn(a, b):
    return lax.dot_general(a, b, (((0,), (0,)), ((), ())), preferred_element_type=F32)


def _params(n_axes=1):
    return pltpu.CompilerParams(dimension_semantics=("arbitrary",) * n_axes,
                                vmem_limit_bytes=VMEM_LIMIT)


def _const_spec(shape):
    zeros = (0,) * len(shape)
    return pl.BlockSpec(shape, lambda *_: zeros, pipeline_mode=pl.Buffered(1))


def _split3(x):
    hi = x.astype(BF16)
    r = x - hi.astype(F32)
    mid = r.astype(BF16)
    lo = (r - mid.astype(F32)).astype(BF16)
    return hi, mid, lo


def _linear_fwd(x, w, *, tm, name):
    S, K = x.shape
    N = w.shape[1]
    nc = 512 if N % 512 == 0 else N

    def body(x_ref, w_ref, o_ref):
        xb = x_ref[...].astype(BF16)
        for n0 in range(0, N, nc):
            o_ref[:, n0:n0 + nc] = _dot(xb, w_ref[:, n0:n0 + nc]).astype(BF16)

    return pl.pallas_call(
        body, name=name, grid=(S // tm,),
        in_specs=[pl.BlockSpec((tm, K), lambda i: (i, 0)), _const_spec((K, N))],
        out_specs=pl.BlockSpec((tm, N), lambda i: (i, 0)),
        out_shape=jax.ShapeDtypeStruct((S, N), BF16),
        compiler_params=_params(),
    )(x, w)


def _kv_proj(x1, w_kv, bf_row, *, tm):
    S = x1.shape[0]

    def body(x_ref, w_ref, b_ref, k_ref, v_ref, fl_ref, cum_ref, carry_ref):
        i = pl.program_id(0)

        @pl.when(i == 0)
        def _():
            carry_ref[...] = jnp.zeros_like(carry_ref)

        xb = x_ref[...].astype(BF16)
        for n0 in range(0, D_MAIN, 512):
            k_ref[:, n0:n0 + 512] = _dot(xb, w_ref[:, n0:n0 + 512]).astype(BF16)
            v_ref[:, n0:n0 + 512] = _dot(xb, w_ref[:, D_MAIN + n0:D_MAIN + n0 + 512]).astype(BF16)
        fl = _dot(xb, w_ref[:, 2 * D_MAIN:2 * D_MAIN + LANES]) + b_ref[...]
        fl_ref[...] = fl
        log_f = jnp.minimum(fl, 0.0) - jnp.log1p(jnp.exp(-jnp.abs(fl)))
        r = lax.broadcasted_iota(jnp.int32, (tm, tm), 0)
        c = lax.broadcasted_iota(jnp.int32, (tm, tm), 1)
        tri = jnp.where(c <= r, 1.0, 0.0).astype(BF16)
        hi, mid, lo = _split3(log_f)
        cum = (_dot(tri, hi) + _dot(tri, mid)) + _dot(tri, lo) + carry_ref[0:1, :]
        cum_ref[...] = cum
        carry_ref[0:1, :] = cum[tm - 1:tm, :]

    return pl.pallas_call(
        body, name="kv_proj", grid=(S // tm,),
        in_specs=[pl.BlockSpec((tm, D_MODEL), lambda i: (i, 0)),
                  _const_spec((D_MODEL, 2 * D_MAIN + LANES)), _const_spec((1, LANES))],
        out_specs=[pl.BlockSpec((tm, D_MAIN), lambda i: (i, 0)),
                   pl.BlockSpec((tm, D_MAIN), lambda i: (i, 0)),
                   pl.BlockSpec((tm, LANES), lambda i: (i, 0)),
                   pl.BlockSpec((tm, LANES), lambda i: (i, 0))],
        out_shape=[jax.ShapeDtypeStruct((S, D_MAIN), BF16), jax.ShapeDtypeStruct((S, D_MAIN), BF16),
                   jax.ShapeDtypeStruct((S, LANES), F32), jax.ShapeDtypeStruct((S, LANES), F32)],
        scratch_shapes=[pltpu.VMEM((8, LANES), F32)],
        compiler_params=_params(),
    )(x1, w_kv, bf_row)


def _gate_bwd(dq_aug, dck, fl, *, tm):
    S = fl.shape[0]
    n = S // tm

    def body(dq_ref, dck_ref, fl_ref, du_ref, df_ref, db_ref, carry_ref):
        i = pl.program_id(0)

        @pl.when(i == 0)
        def _():
            carry_ref[...] = jnp.zeros_like(carry_ref)
            db_ref[...] = jnp.zeros_like(db_ref)

        lane = lax.broadcasted_iota(jnp.int32, (tm, LANES), 1)
        half0 = lane < FOX_HEAD_DIM
        dcq = jnp.zeros((tm, LANES), F32)
        for hp in range(FOX_HEADS // 2):
            b0 = dq_ref[:, 2 * hp * LANES:(2 * hp + 1) * LANES]
            b1 = dq_ref[:, (2 * hp + 1) * LANES:(2 * hp + 2) * LANES]
            du_ref[:, hp * LANES:(hp + 1) * LANES] = (
                jnp.where(half0, b0, pltpu.roll(b1, FOX_HEAD_DIM, 1)) * FOX_SCALE).astype(BF16)
            r0 = jnp.sum(jnp.where(lane == AUG_A, b0, 0.0), axis=1, keepdims=True)
            r1 = jnp.sum(jnp.where(lane == AUG_A, b1, 0.0), axis=1, keepdims=True)
            dcq = dcq + jnp.where(lane == 2 * hp, r0, 0.0) + jnp.where(lane == 2 * hp + 1, r1, 0.0)
        dcum = dcq - dck_ref[...]
        r = lax.broadcasted_iota(jnp.int32, (tm, tm), 0)
        c = lax.broadcasted_iota(jnp.int32, (tm, tm), 1)
        tri = jnp.where(c >= r, 1.0, 0.0).astype(BF16)
        hi, mid, lo = _split3(dcum)
        rev = (_dot(tri, hi) + _dot(tri, mid)) + _dot(tri, lo) + carry_ref[0:1, :]
        carry_ref[0:1, :] = rev[0:1, :]
        fl_v = fl_ref[...]
        df = rev * (1.0 / (1.0 + jnp.exp(fl_v)))
        df_ref[...] = df
        db_ref[...] += jnp.sum(df, axis=0, keepdims=True)

    return pl.pallas_call(
        body, name="gate_bwd", grid=(n,),
        in_specs=[pl.BlockSpec((tm, AUG_W), lambda i: (n - 1 - i, 0)),
                  pl.BlockSpec((tm, LANES), lambda i: (n - 1 - i, 0)),
                  pl.BlockSpec((tm, LANES), lambda i: (n - 1 - i, 0))],
        out_specs=[pl.BlockSpec((tm, D_MAIN), lambda i: (n - 1 - i, 0)),
                   pl.BlockSpec((tm, LANES), lambda i: (n - 1 - i, 0)),
                   pl.BlockSpec((1, LANES), lambda i: (0, 0))],
        out_shape=[jax.ShapeDtypeStruct((S, D_MAIN), BF16),
                   jax.ShapeDtypeStruct((S, LANES), F32), jax.ShapeDtypeStruct((1, LANES), F32)],
        scratch_shapes=[pltpu.VMEM((8, LANES), F32)],
        compiler_params=_params(),
    )(dq_aug, dck, fl)


def _silu_and_grad(g):
    sg = 1.0 / (1.0 + jnp.exp(-g))
    return g * sg, sg * (1.0 + g * (1.0 - sg))


def _mix(kind, mode, *, h, xres=None, mkv, w_out, ln_g, ln_b=None, pool_w=None, pool_scale=None,
         ymain=None, target=None, z=None, dy=None, gather=None, tm):
    S = h.shape[0]
    n = S // tm
    pool = kind == "pool"
    bwd = mode == "bwd"
    loss_head = bwd and not pool
    rev = pool and bwd
    mem_scale = MEM_HEAD_DIM ** -0.5

    def t_of(i):
        return (n - 1 - i) if rev else i

    row = lambda i: (t_of(i), 0)
    names, arrays, specs = [], [], []

    def add(name, arr, spec):
        names.append(name)
        arrays.append(arr)
        specs.append(spec)

    add("h", h, pl.BlockSpec((tm, D_IN), row))
    if pool:
        hb = tm // POOL_HALO
        add("halo", h, pl.BlockSpec((POOL_HALO, D_MAIN), lambda i: (jnp.maximum(t_of(i) * hb - 1, 0), 0)))
        add("pool_w", pool_w, _const_spec((4, POOL_GROUP, POOL_GROUP)))
        add("pool_scale", pool_scale, _const_spec((1, D_MAIN)))
    else:
        add("ymain", ymain, pl.BlockSpec((tm, D_MAIN), row))
    add("mkv", mkv, _const_spec((N_MEM, 2 * D_MEM)))
    add("w_out", w_out, _const_spec((D_MIX, D_MODEL)))
    add("ln_g", ln_g, _const_spec((1, D_MODEL)))
    if not (pool and bwd):
        add("xres", xres, pl.BlockSpec((tm, D_MODEL), row))
        add("ln_b", ln_b, _const_spec((1, D_MODEL)))
    if loss_head:
        add("target", target, pl.BlockSpec((tm, D_MODEL), row))
    if pool and bwd:
        add("z", z, pl.BlockSpec((tm, D_MODEL), row))
        add("dy", dy, pl.BlockSpec((tm, D_MODEL), row))
    if gather is not None:
        add("gather_src", gather, _ANY)

    onames, oshapes, ospecs = [], [], []

    def add_out(name, shape, dtype, spec):
        onames.append(name)
        oshapes.append(jax.ShapeDtypeStruct(shape, dtype))
        ospecs.append(spec)

    const2 = lambda i: (0, 0)
    if not bwd:
        add_out("z", (S, D_MODEL), F32, pl.BlockSpec((tm, D_MODEL), row))
        add_out("xout", (S, D_MODEL), F32, pl.BlockSpec((tm, D_MODEL), row))
    else:
        add_out("dz", (S, D_MODEL), F32, pl.BlockSpec((tm, D_MODEL), row))
        if pool:
            add_out("dmain", (S, D_MAIN), BF16, pl.BlockSpec((tm, D_MAIN), row))
        else:
            add_out("dmain", (S, 2 * D_MAIN), BF16, pl.BlockSpec((tm, 2 * D_MAIN), row))
        add_out("drest", (S, D_IN - D_MAIN), BF16, pl.BlockSpec((tm, D_IN - D_MAIN), row))
        add_out("dw_out", (D_MIX, D_MODEL), F32, pl.BlockSpec((D_MIX, D_MODEL), const2))
        add_out("dmkv", (N_MEM, 2 * D_MEM), F32, pl.BlockSpec((N_MEM, 2 * D_MEM), const2))
        add_out("dln_g", (1, D_MODEL), F32, pl.BlockSpec((1, D_MODEL), const2))
        add_out("dln_b", (1, D_MODEL), F32, pl.BlockSpec((1, D_MODEL), const2))
        if pool:
            add_out("dpool_w", (4, POOL_GROUP, POOL_GROUP), F32,
                    pl.BlockSpec((4, POOL_GROUP, POOL_GROUP), lambda i: (0, 0, 0)))
            add_out("dpool_scale", (1, D_MAIN), F32, pl.BlockSpec((1, D_MAIN), const2))
        else:
            add_out("loss", (1, D_MODEL), F32, pl.BlockSpec((1, D_MODEL), const2))
            add_out("drow", (FOX_HEADS, S), F32, pl.BlockSpec((FOX_HEADS, tm), lambda i: (0, i)))

    if gather is not None:
        add_out("gathered", (N_CHIPS,) + gather.shape, gather.dtype, _ANY)

    scratch = [pltpu.VMEM((tm, D_MIX), BF16),
               pltpu.VMEM((tm, D_MEM), F32)]
    if pool:
        scratch.append(pltpu.VMEM((tm + 2 * POOL_HALO, D_MAIN), F32))
    if rev:
        scratch.append(pltpu.VMEM((POOL_HALO, D_MAIN), F32))
    if gather is not None:
        scratch += list(_GATHER_SEMS)
    n_in, n_out = len(names), len(onames)

    def body(*refs):
        R = dict(zip(names, refs[:n_in]))
        O = dict(zip(onames, refs[n_in:n_in + n_out]))
        sc = refs[n_in + n_out:]
        yc_ref, ymem_ref = sc[0], sc[1]
        ext_ref = sc[2] if pool else None
        carry_ref = sc[3] if rev else None
        i = pl.program_id(0)
        t = t_of(i)
        h_ref = R["h"]
        gamma = R["ln_g"][...]

        if gather is not None:
            start, forward, finish = _gather_steps(R["gather_src"], O["gathered"], sc[-2], sc[-1])
            pl.when(i == 0)(start)
            pl.when(i == n - 3)(forward)
            pl.when(i == n - 1)(finish)

        if bwd:
            @pl.when(i == 0)
            def _():
                for nm in ("dw_out", "dmkv", "dln_g", "dln_b", "dpool_w", "dpool_scale", "loss"):
                    if nm in O:
                        O[nm][...] = jnp.zeros_like(O[nm])
                if rev:
                    carry_ref[...] = jnp.zeros_like(carry_ref)

        if pool:
            u = h_ref[:, 0:D_MAIN].astype(F32)
            halo = R["halo"][...].astype(F32)
            ext_ref[0:POOL_HALO, :] = jnp.where(t > 0, halo, 0.0)
            ext_ref[POOL_HALO:POOL_HALO + tm, :] = u
            tpos = t * tm + lax.broadcasted_iota(jnp.int32, (tm, 1), 0)
            pms, invcs = [], []
            for gi, w in enumerate(POOL_WINDOWS):
                cs = slice(gi * POOL_GROUP, (gi + 1) * POOL_GROUP)
                acc = ext_ref[POOL_HALO:POOL_HALO + tm, cs]
                for k in range(1, w):
                    acc = acc + ext_ref[POOL_HALO - k:POOL_HALO - k + tm, cs]
                invc = 1.0 / jnp.minimum(tpos + 1, w).astype(F32)
                pm = (acc * invc - u[:, cs]).astype(BF16)
                pms.append(pm)
                invcs.append(invc)
            mixed = [_dot(pms[gi], R["pool_w"][gi]) for gi in range(4)]
            ps = R["pool_scale"][...]
            y_main = [mixed[gi] * ps[:, gi * POOL_GROUP:(gi + 1) * POOL_GROUP] for gi in range(4)]
        else:
            y_main = [R["ymain"][:, gi * 256:(gi + 1) * 256].astype(F32) for gi in range(4)]

        probs = []
        for hd in range(MEM_HEADS):
            sl = slice(D_MAIN + hd * MEM_HEAD_DIM, D_MAIN + (hd + 1) * MEM_HEAD_DIM)
            ksl = slice(hd * MEM_HEAD_DIM, (hd + 1) * MEM_HEAD_DIM)
            vsl = slice(D_MEM + hd * MEM_HEAD_DIM, D_MEM + (hd + 1) * MEM_HEAD_DIM)
            s = _dot_nt(h_ref[:, sl], R["mkv"][:, ksl]) * mem_scale
            e = jnp.exp(s - jnp.max(s, axis=1, keepdims=True))
            p = e / jnp.sum(e, axis=1, keepdims=True)
            probs.append(p)
            ymem_ref[:, ksl] = _dot(p.astype(BF16), R["mkv"][:, vsl])

        g_off = D_MIX
        gate_d = []
        for gi in range(4):
            cs = slice(gi * 256, (gi + 1) * 256)
            gm = h_ref[:, g_off + gi * 256:g_off + (gi + 1) * 256].astype(F32)
            sv, sd = _silu_and_grad(gm)
            yc_ref[:, cs] = (y_main[gi] * sv).astype(BF16)
            gate_d.append((sv, sd))
        gq = h_ref[:, g_off + D_MAIN:D_IN].astype(F32)
        svq, sdq = _silu_and_grad(gq)
        yc_ref[:, D_MAIN:D_MIX] = (ymem_ref[...] * svq).astype(BF16)

        if pool and bwd:
            zt = R["z"][...]
        else:
            o = _dot(yc_ref[...], R["w_out"][...])
            zt = ALPHA * R["xres"][...] + o
        mu = jnp.mean(zt, axis=1, keepdims=True)
        zc = zt - mu
        var = jnp.mean(zc * zc, axis=1, keepdims=True)
        rstd = lax.rsqrt(var + LN_EPS)
        xhat = zc * rstd
        if not bwd:
            O["z"][...] = zt
            O["xout"][...] = xhat * gamma + R["ln_b"][...]
            return

        if loss_head:
            xo = xhat * gamma + R["ln_b"][...]
            err = xo - R["target"][...]
            O["loss"][...] += jnp.sum(err * err, axis=0, keepdims=True)
            dyt = err * (1.0 / D_MODEL)
        else:
            dyt = R["dy"][...]

        O["dln_g"][...] += jnp.sum(dyt * xhat, axis=0, keepdims=True)
        O["dln_b"][...] += jnp.sum(dyt, axis=0, keepdims=True)
        gdy = dyt * gamma
        m1 = jnp.mean(gdy, axis=1, keepdims=True)
        m2 = jnp.mean(gdy * xhat, axis=1, keepdims=True)
        dz = rstd * (gdy - m1 - xhat * m2)
        O["dz"][...] = dz
        dzb = dz.astype(BF16)

        for n0 in range(0, D_MIX, 512):
            O["dw_out"][n0:n0 + 512, :] += _dot_tn(yc_ref[:, n0:n0 + 512], dzb)
        dyc_mem = _dot_nt(dzb, R["w_out"][D_MAIN:D_MIX, :])

        O["drest"][:, D_MEM + D_MAIN:D_MEM + D_MAIN + D_MEM] = (dyc_mem * ymem_ref[...] * sdq).astype(BF16)
        dymem = dyc_mem * svq
        for hd in range(MEM_HEADS):
            sl = slice(D_MAIN + hd * MEM_HEAD_DIM, D_MAIN + (hd + 1) * MEM_HEAD_DIM)
            ksl = slice(hd * MEM_HEAD_DIM, (hd + 1) * MEM_HEAD_DIM)
            vsl = slice(D_MEM + hd * MEM_HEAD_DIM, D_MEM + (hd + 1) * MEM_HEAD_DIM)
            p = probs[hd]
            dyb = dymem[:, ksl].astype(BF16)
            dp = _dot_nt(dyb, R["mkv"][:, vsl])
            ds = p * (dp - jnp.sum(dp * p, axis=1, keepdims=True)) * mem_scale
            dsb = ds.astype(BF16)
            O["drest"][:, ksl] = _dot(dsb, R["mkv"][:, ksl]).astype(BF16)
            O["dmkv"][:, ksl] += _dot_tn(dsb, h_ref[:, sl])
            O["dmkv"][:, vsl] += _dot_tn(p.astype(BF16), dyb)

        dmain = []
        for gi in range(4):
            cs = slice(gi * 256, (gi + 1) * 256)
            dyc_g = _dot_nt(dzb, R["w_out"][cs, :])
            sv, sd = gate_d[gi]
            O["drest"][:, D_MEM + gi * 256:D_MEM + (gi + 1) * 256] = (dyc_g * y_main[gi] * sd).astype(BF16)
            dmain.append(dyc_g * sv)

        if not pool:
            prod = []
            lane2 = lax.broadcasted_iota(jnp.int32, (tm, LANES), 1)
            for gi in range(4):
                cs = slice(gi * 256, (gi + 1) * 256)
                db16 = dmain[gi].astype(BF16)
                dbf = db16.astype(F32)
                prod.append(dbf * R["ymain"][:, cs].astype(F32))
                for pr in range(2):
                    blk = dbf[:, pr * LANES:(pr + 1) * LANES]
                    base = (4 * gi + 2 * pr) * LANES
                    O["dmain"][:, base:base + LANES] = jnp.where(lane2 < FOX_HEAD_DIM, blk, 0.0).astype(BF16)
                    O["dmain"][:, base + LANES:base + 2 * LANES] = jnp.where(
                        lane2 < FOX_HEAD_DIM, pltpu.roll(blk, FOX_HEAD_DIM, 1), 0.0).astype(BF16)
            dcol = jnp.zeros((tm, LANES), F32)
            for gi in range(4):
                dr = lax.broadcasted_iota(jnp.int32, (256, LANES), 0)
                hc = lax.broadcasted_iota(jnp.int32, (256, LANES), 1)
                sel = jnp.where(jnp.right_shift(dr, 6) + gi * 4 == hc, 1.0, 0.0).astype(BF16)
                hi, mid, lo = _split3(prod[gi])
                dcol = dcol + ((_dot(hi, sel) + _dot(mid, sel)) + _dot(lo, sel))
            O["drow"][...] = dcol.T[0:FOX_HEADS, :]
            return

        ps = R["pool_scale"][...]
        dpm_list = []
        for gi in range(4):
            cs = slice(gi * 256, (gi + 1) * 256)
            O["dpool_scale"][:, cs] += jnp.sum(dmain[gi] * mixed[gi], axis=0, keepdims=True)
            dmix = (dmain[gi] * ps[:, cs]).astype(BF16)
            O["dpool_w"][gi] += _dot_tn(pms[gi], dmix)
            dpm = _dot_nt(dmix, R["pool_w"][gi])
            dpm_list.append(dpm)
            ext_ref[0:tm, cs] = dpm * invcs[gi]
        ext_ref[tm:tm + POOL_HALO, :] = carry_ref[...]
        carry_ref[...] = ext_ref[0:POOL_HALO, :]
        for gi, w in enumerate(POOL_WINDOWS):
            cs = slice(gi * 256, (gi + 1) * 256)
            acc = ext_ref[0:tm, cs]
            for k in range(1, w):
                acc = acc + ext_ref[k:k + tm, cs]
            O["dmain"][:, cs] = (acc - dpm_list[gi]).astype(BF16)

    outs = pl.pallas_call(
        body, name=f"mix_{kind}_{mode}", grid=(n,),
        in_specs=specs, out_specs=ospecs, out_shape=oshapes,
        scratch_shapes=scratch, compiler_params=_params(),
    )(*arrays)
    return dict(zip(onames, outs))


def _lin_bwd(xin, dhs, w, res, *, tm, name, scatter=None):
    S, K = xin.shape
    N = w.shape[1]
    nj = len(dhs)
    nr = len(res)
    n_tiles = S // tm
    widths = [dh.shape[1] for dh in dhs]
    assert sum(widths) == N
    scales = [s for _, s in res]
    n_in = 2 + nj + nr + (1 if scatter is not None else 0)

    def body(*refs):
        x_ref = refs[0]
        dh_refs = refs[1:1 + nj]
        w_ref = refs[1 + nj]
        r_refs = refs[2 + nj:2 + nj + nr]
        dx_ref, dw_ref = refs[n_in], refs[n_in + 1]
        i = pl.program_id(0)

        if scatter is not None:
            start, finish = _scatter_steps(refs[n_in - 1], refs[n_in + 2], refs[n_in + 3], refs[n_in + 4])
            pl.when(i == 0)(start)
            pl.when(i == n_tiles - 1)(finish)

        @pl.when(i == 0)
        def _():
            dw_ref[...] = jnp.zeros_like(dw_ref)

        xb = x_ref[...].astype(BF16)
        dx = jnp.zeros((tm, K), F32)
        for r_ref, sc in zip(r_refs, scales):
            dx = dx + sc * r_ref[...]
        off = 0
        for j in range(nj):
            nc = 512 if widths[j] % 512 == 0 else widths[j]
            for n0 in range(0, widths[j], nc):
                dhb = dh_refs[j][:, n0:n0 + nc].astype(BF16)
                dx = dx + _dot_nt(dhb, w_ref[:, off + n0:off + n0 + nc])
                dw_ref[:, off + n0:off + n0 + nc] += _dot_tn(xb, dhb)
            off += widths[j]
        dx_ref[...] = dx

    in_specs = [pl.BlockSpec((tm, K), lambda i: (i, 0))]
    in_specs += [pl.BlockSpec((tm, n), lambda i: (i, 0)) for n in widths]
    in_specs += [_const_spec((K, N))]
    in_specs += [pl.BlockSpec((tm, K), lambda i: (i, 0)) for _ in res]
    out_specs = [pl.BlockSpec((tm, K), lambda i: (i, 0)), pl.BlockSpec((K, N), lambda i: (0, 0))]
    out_shape = [jax.ShapeDtypeStruct((S, K), F32), jax.ShapeDtypeStruct((K, N), F32)]
    extra, scratch = [], []
    if scatter is not None:
        in_specs.append(_ANY)
        out_specs.append(_ANY)
        out_shape.append(jax.ShapeDtypeStruct(scatter.shape, scatter.dtype))
        extra, scratch = [scatter], list(_SCATTER_SEMS)
    return pl.pallas_call(
        body, name=name, grid=(n_tiles,),
        in_specs=in_specs, out_specs=out_specs, out_shape=out_shape, scratch_shapes=scratch,
        compiler_params=_params(),
    )(xin, *dhs, w, *[r for r, _ in res], *extra)


def _wgrad(xin, dh, *, name):
    M, K = xin.shape
    N = dh.shape[1]

    def body(x_ref, dh_ref, o_ref):
        o_ref[...] = _dot_tn(x_ref[...].astype(BF16), dh_ref[...].astype(BF16))

    return pl.pallas_call(
        body, name=name, out_shape=jax.ShapeDtypeStruct((K, N), F32),
        compiler_params=pltpu.CompilerParams(vmem_limit_bytes=VMEM_LIMIT),
    )(xin, dh)


AUG_A = FOX_HEAD_DIM
AUG_B = FOX_HEAD_DIM + 3
AUG_W = FOX_HEADS * LANES


def _placement(val_lane, ones_lane):
    r = jnp.arange(LANES)[:, None]
    c = jnp.arange(AUG_W)[None, :]
    head, lane = c // LANES, c % LANES
    m = jnp.zeros((LANES, AUG_W), jnp.bool_)
    if val_lane is not None:
        for part in range(3):
            m = m | ((r == part * FOX_HEADS + head) & (lane == val_lane + part))
    if ones_lane is not None:
        m = m | ((r == 3 * FOX_HEADS) & (lane >= ones_lane) & (lane < ones_lane + 3))
    return m.astype(BF16)


def _augment(x, cols, signs, *, val_lane, ones_lane, scale, tm, name):
    S = x.shape[0]
    place = _placement(val_lane if cols else None, ones_lane)
    nc = len(cols)

    def body(*refs):
        x_ref = refs[0]
        col_refs = refs[1:1 + nc]
        p_ref = refs[1 + nc]
        o_ref = refs[2 + nc]
        lane = lax.broadcasted_iota(jnp.int32, (tm, LANES), 1)
        data = lane < FOX_HEAD_DIM
        lhs = jnp.where(lane == 3 * FOX_HEADS, 1.0, 0.0)
        if nc:
            val = signs[0] * col_refs[0][...]
            for i in range(1, nc):
                val = val + signs[i] * col_refs[i][...]
            hi, mid, lo = [p.astype(F32) for p in _split3(val)]
            lhs = jnp.where(lane < FOX_HEADS, hi, jnp.where(
                lane < 2 * FOX_HEADS, pltpu.roll(mid, FOX_HEADS, 1), jnp.where(
                    lane < 3 * FOX_HEADS, pltpu.roll(lo, 2 * FOX_HEADS, 1), lhs)))
        lhs = lhs.astype(BF16)
        for hp in range(FOX_HEADS // 2):
            extra = _dot(lhs, p_ref[:, 2 * hp * LANES:(2 * hp + 2) * LANES])
            blk = x_ref[:, hp * LANES:(hp + 1) * LANES].astype(F32) * scale
            o_ref[:, 2 * hp * LANES:(2 * hp + 1) * LANES] = jnp.where(data, blk, extra[:, 0:LANES]).astype(BF16)
            o_ref[:, (2 * hp + 1) * LANES:(2 * hp + 2) * LANES] = jnp.where(
                data, pltpu.roll(blk, FOX_HEAD_DIM, 1), extra[:, LANES:2 * LANES]).astype(BF16)

    in_specs = [pl.BlockSpec((tm, D_MAIN), lambda i: (i, 0))]
    in_specs += [pl.BlockSpec((tm, LANES), lambda i: (i, 0)) for _ in cols]
    in_specs += [_const_spec((LANES, AUG_W))]
    return pl.pallas_call(
        body, name=name, grid=(S // tm,),
        in_specs=in_specs, out_specs=pl.BlockSpec((tm, AUG_W), lambda i: (i, 0)),
        out_shape=jax.ShapeDtypeStruct((S, AUG_W), BF16),
        compiler_params=_params(),
    )(x, *cols, place)


def _cols_of_rows(rows, S):
    nh = FOX_HEADS // rows.shape[0]
    a = rows[:, :, 0:nh, :].transpose(0, 2, 1, 3).reshape(FOX_HEADS, S).T
    return jnp.pad(a, ((0, 0), (0, LANES - FOX_HEADS)))


def _fox_fwd(qf, ka, va, *, tq, nh=4):
    S = ka.shape[0]
    nq = S // tq
    tk = tq
    ng = FOX_HEADS // nh

    def body(q_ref, k_ref, v_ref, o_ref, lse_ref, *scratch):
        p_scr, m_scr, acc_scr = scratch[0:nh], scratch[nh:2 * nh], scratch[2 * nh:3 * nh]
        qi = pl.program_id(1)
        lane = lax.broadcasted_iota(jnp.int32, (tq, LANES), 1)
        half0 = lane < FOX_HEAD_DIM
        rr = lax.broadcasted_iota(jnp.int32, (tq, tk), 0)
        cc = lax.broadcasted_iota(jnp.int32, (tq, tk), 1)
        sls = [slice(hh * LANES, (hh + 1) * LANES) for hh in range(nh)]
        qs = [q_ref[:, sl] for sl in sls]

        for hh in range(nh):
            p_scr[hh][...] = jnp.zeros_like(p_scr[hh])
            m_scr[hh][...] = jnp.full(m_scr[hh].shape, -jnp.inf, F32)
            acc_scr[hh][...] = jnp.zeros_like(acc_scr[hh])

        def chunk(ki, masked):
            k0 = pl.multiple_of(ki * tk, tk)
            kp = pl.multiple_of(jnp.maximum(ki - 1, 0) * tk, tk)
            for hh in range(nh):
                m = m_scr[hh][...]
                s = _dot_nt(qs[hh], k_ref[pl.ds(k0, tk), sls[hh]])
                pv = _dot(p_scr[hh][...], v_ref[pl.ds(kp, tk), sls[hh]])
                if masked:
                    s = jnp.where(cc <= rr, s, -jnp.inf)
                m_new = jnp.maximum(m, jnp.max(s, axis=1, keepdims=True))
                p_scr[hh][...] = jnp.exp(s - jnp.tile(m_new, (1, tk // LANES))).astype(BF16)
                acc_scr[hh][...] = (acc_scr[hh][...] + pv) * jnp.exp(m - m_new)
                m_scr[hh][...] = m_new

        def trip(ki, c):
            chunk(ki, False)
            return c

        lax.fori_loop(0, qi, trip, 0)
        chunk(qi, True)
        kq = pl.multiple_of(qi * tk, tk)
        outs = []
        lse_cols = jnp.zeros((tq, LANES), F32)
        for hh in range(nh):
            m = m_scr[hh][...]
            acc = acc_scr[hh][...] + _dot(p_scr[hh][...], v_ref[pl.ds(kq, tk), sls[hh]])
            l = jnp.sum(jnp.where(lane == AUG_A, acc, 0.0), axis=1, keepdims=True)
            outs.append(acc / l)
            lse_cols = lse_cols + jnp.where(lane == hh, m + jnp.log(l), 0.0)
        for pr in range(nh // 2):
            o_ref[:, pr * LANES:(pr + 1) * LANES] = jnp.where(
                half0, outs[2 * pr], pltpu.roll(outs[2 * pr + 1], FOX_HEAD_DIM, 1))
        lse_ref[0, 0] = lse_cols.T[0:8, :]

    return pl.pallas_call(
        body, name="fox_fwd", grid=(ng, nq),
        in_specs=[pl.BlockSpec((tq, nh * LANES), lambda g, qi: (qi, g)),
                  pl.BlockSpec((S, nh * LANES), lambda g, qi: (0, g), pipeline_mode=pl.Buffered(1)),
                  pl.BlockSpec((S, nh * LANES), lambda g, qi: (0, g), pipeline_mode=pl.Buffered(1))],
        out_specs=[pl.BlockSpec((tq, nh * FOX_HEAD_DIM), lambda g, qi: (qi, g)),
                   pl.BlockSpec((1, 1, 8, tq), lambda g, qi: (g, qi, 0, 0))],
        out_shape=[jax.ShapeDtypeStruct((S, D_MAIN), F32),
                   jax.ShapeDtypeStruct((ng, nq, 8, tq), F32)],
        scratch_shapes=([pltpu.VMEM((tq, tk), BF16)] * nh + [pltpu.VMEM((tq, LANES), F32)] * nh
                        + [pltpu.VMEM((tq, LANES), F32)] * nh),
        compiler_params=_params(2),
    )(qf, ka, va)


def _rows_for_pairs(a16, nt, tt):
    a = a16.reshape(8, 2, nt, tt).transpose(0, 2, 1, 3)
    return jnp.pad(a, ((0, 0), (0, 0), (0, 6), (0, 0)))


def _fox_bwd(qf, ka, va, do_aug, lse_rows, d_rows, *, tq):
    S = ka.shape[0]
    nq = S // tq
    tk = tq

    def body(k_ref, v_ref, q_ref, do_ref, lser_ref, dr_ref, dq_ref, dk_ref, dv_ref, dck_ref, dk_scr, dv_scr):
        kj = pl.program_id(1)

        @pl.when(kj == 0)
        def _():
            dq_ref[...] = jnp.zeros_like(dq_ref)

        lane = lax.broadcasted_iota(jnp.int32, (tk, LANES), 1)
        half0 = lane < FOX_HEAD_DIM
        rr = lax.broadcasted_iota(jnp.int32, (tk, tq), 0)
        cc = lax.broadcasted_iota(jnp.int32, (tk, tq), 1)
        sls = [slice(hh * LANES, (hh + 1) * LANES) for hh in range(2)]
        kts = [k_ref[:, sl] for sl in sls]
        vts = [v_ref[:, sl] for sl in sls]

        dk_scr[...] = jnp.zeros_like(dk_scr)
        dv_scr[...] = jnp.zeros_like(dv_scr)

        def chunk(qi, masked):
            q0 = pl.multiple_of(qi * tq, tq)
            for hh in range(2):
                qc = q_ref[pl.ds(q0, tq), sls[hh]]
                doc = do_ref[pl.ds(q0, tq), sls[hh]]
                lse = lser_ref[0, qi][hh:hh + 1, :]
                dd = dr_ref[0, qi][hh:hh + 1, :]
                pt = jnp.exp(_dot_nt(kts[hh], qc) - lse)
                if masked:
                    pt = jnp.where(rr <= cc, pt, 0.0)
                dsb = (pt * (_dot_nt(vts[hh], doc) - dd)).astype(BF16)
                dv_scr[hh] += _dot(pt.astype(BF16), doc)
                dk_scr[hh] += _dot(dsb, qc)
                dq_ref[pl.ds(q0, tq), sls[hh]] += _dot_tn(dsb, kts[hh])

        def trip(qi, c):
            chunk(qi, False)
            return c

        chunk(kj, True)
        lax.fori_loop(kj + 1, nq, trip, 0)
        dk0, dk1 = dk_scr[0], dk_scr[1]
        dv0, dv1 = dv_scr[0], dv_scr[1]
        dk_ref[...] = jnp.where(half0, dk0, pltpu.roll(dk1, FOX_HEAD_DIM, 1)).astype(BF16)
        dv_ref[...] = jnp.where(half0, dv0, pltpu.roll(dv1, FOX_HEAD_DIM, 1)).astype(BF16)
        c0 = jnp.sum(jnp.where(lane == AUG_B, dk0, 0.0), axis=1, keepdims=True)
        c1 = jnp.sum(jnp.where(lane == AUG_B, dk1, 0.0), axis=1, keepdims=True)
        dck_cols = jnp.where(lane == 0, c0, 0.0) + jnp.where(lane == 1, c1, 0.0)
        dck_ref[0, 0] = dck_cols.T[0:8, :]

    return pl.pallas_call(
        body, name="fox_bwd", grid=(8, nq),
        in_specs=[pl.BlockSpec((tk, 2 * LANES), lambda hp, kj: (kj, hp)),
                  pl.BlockSpec((tk, 2 * LANES), lambda hp, kj: (kj, hp)),
                  pl.BlockSpec((S, 2 * LANES), lambda hp, kj: (0, hp)),
                  pl.BlockSpec((S, 2 * LANES), lambda hp, kj: (0, hp)),
                  pl.BlockSpec((1, nq, 8, tq), lambda hp, kj: (hp, 0, 0, 0)),
                  pl.BlockSpec((1, nq, 8, tq), lambda hp, kj: (hp, 0, 0, 0))],
        out_specs=[pl.BlockSpec((S, 2 * LANES), lambda hp, kj: (0, hp)),
                   pl.BlockSpec((tk, LANES), lambda hp, kj: (kj, hp)),
                   pl.BlockSpec((tk, LANES), lambda hp, kj: (kj, hp)),
                   pl.BlockSpec((1, 1, 8, tk), lambda hp, kj: (hp, kj, 0, 0))],
        out_shape=[jax.ShapeDtypeStruct((S, AUG_W), F32),
                   jax.ShapeDtypeStruct((S, D_MAIN), BF16),
                   jax.ShapeDtypeStruct((S, D_MAIN), BF16),
                   jax.ShapeDtypeStruct((8, nq, 8, tk), F32)],
        scratch_shapes=[pltpu.VMEM((2, tk, LANES), F32), pltpu.VMEM((2, tk, LANES), F32)],
        compiler_params=_params(2),
    )(ka, va, qf, do_aug, lse_rows, d_rows)


def _adamw(w, g, m, v, *, name):
    Rr, C = w.shape
    tr = 256 if Rr % 256 == 0 else Rr
    c1 = 1.0 / (1.0 - ADAM_B1 ** ADAM_STEP)
    c2 = 1.0 / (1.0 - ADAM_B2 ** ADAM_STEP)

    def body(w_ref, g_ref, m_ref, v_ref, d_ref, nm_ref, nv_ref):
        gv = g_ref[...]
        nm = ADAM_B1 * m_ref[...] + (1.0 - ADAM_B1) * gv
        nv = ADAM_B2 * v_ref[...] + (1.0 - ADAM_B2) * (gv * gv)
        d_ref[...] = -ADAM_LR * ((nm * c1) / (jnp.sqrt(nv * c2) + ADAM_EPS) + ADAM_WD * w_ref[...])
        nm_ref[...] = nm
        nv_ref[...] = nv

    spec = pl.BlockSpec((tr, C), lambda i: (i, 0))
    sds = jax.ShapeDtypeStruct((Rr, C), F32)
    return pl.pallas_call(
        body, name=name, grid=(Rr // tr,),
        in_specs=[spec] * 4, out_specs=[spec] * 3, out_shape=[sds] * 3,
        compiler_params=_params(),
    )(w, g, m, v)


_ANY = pl.BlockSpec(memory_space=pl.ANY)
_MESH = pl.DeviceIdType.MESH


def _place():
    x, y, c = lax.axis_index("x"), lax.axis_index("y"), lax.axis_index("c")
    return x, y, c


def _gather_steps(p_ref, out_ref, send_sems, recv_sems):
    x, y, c = _place()
    sib = (x, y, 1 - c)
    chips = [(1 - x, y), (x, 1 - y), (1 - x, 1 - y)]
    idx = [2 * chip[0] + chip[1] for chip in chips]
    me = 2 * x + y

    def copy(k, chip_idx, half, to, src=None):
        dst = out_ref.at[chip_idx, half]
        return pltpu.make_async_remote_copy(
            src_ref=dst if src is None else src, dst_ref=dst,
            send_sem=send_sems.at[k], recv_sem=recv_sems.at[k],
            device_id=to, device_id_type=_MESH)

    first = [copy(j, me, c, (*chip, c), src=p_ref.at[c]) for j, chip in enumerate(chips)]
    passed = [copy(3 + j, idx[j], c, sib) for j in range(3)]

    def start():
        for cp in first:
            cp.start()

    def forward():
        for j in range(3):
            copy(j, idx[j], c, sib).wait_recv()
            passed[j].start()

    def finish():
        for j in range(3):
            copy(3 + j, idx[j], 1 - c, sib).wait_recv()
        for cp in first + passed:
            cp.wait_send()

    return start, forward, finish


_GATHER_SEMS = [pltpu.SemaphoreType.DMA((6,)), pltpu.SemaphoreType.DMA((6,))]
_SCATTER_SEMS = [pltpu.SemaphoreType.DMA((3,)), pltpu.SemaphoreType.DMA((3,))]


def _gathered_shape(pack):
    return jax.ShapeDtypeStruct((N_CHIPS,) + pack.shape, pack.dtype)


def _all_gather_shards(pack):
    def body(p_ref, out_ref, send_sems, recv_sems):
        for step in _gather_steps(p_ref, out_ref, send_sems, recv_sems):
            step()

    return pl.pallas_call(
        body, name="all_gather_shards",
        in_specs=[_ANY], out_specs=_ANY, out_shape=_gathered_shape(pack),
        scratch_shapes=list(_GATHER_SEMS),
    )(pack)


def _send_half_to_sibling(gpack, tag):
    rows = gpack.shape[2]

    def body(g_ref, out_ref, send_sem, recv_sem):
        x, y, c = _place()
        sib = (x, y, 1 - c)
        cps = [pltpu.make_async_remote_copy(
            src_ref=g_ref.at[j, 1 - c], dst_ref=out_ref.at[j],
            send_sem=send_sem.at[j], recv_sem=recv_sem.at[j],
            device_id=sib, device_id_type=_MESH) for j in range(N_CHIPS)]
        for cp in cps:
            cp.start()
        for cp in cps:
            cp.wait_recv()
        for cp in cps:
            cp.wait_send()

    return pl.pallas_call(
        body, name=f"pair_send{tag}",
        in_specs=[_ANY], out_specs=_ANY,
        out_shape=jax.ShapeDtypeStruct((N_CHIPS, rows, 1024), F32),
        scratch_shapes=[pltpu.SemaphoreType.DMA((N_CHIPS,)), pltpu.SemaphoreType.DMA((N_CHIPS,))],
    )(gpack)


def _pair_sum(gpack, recv, c_arr, tag, *, tr=PACK_TILE):
    rows = recv.shape[1]

    def body(c_ref, a_ref, b_ref, o_ref):
        o_ref[...] = (a_ref[...] + b_ref[...]).astype(BF16)

    grid_spec = pltpu.PrefetchScalarGridSpec(
        num_scalar_prefetch=1, grid=(N_CHIPS, rows // tr),
        in_specs=[pl.BlockSpec((None, None, tr, 1024), lambda j, i, c_ref: (j, c_ref[0], i, 0)),
                  pl.BlockSpec((None, tr, 1024), lambda j, i, c_ref: (j, i, 0))],
        out_specs=pl.BlockSpec((None, tr, 1024), lambda j, i, c_ref: (j, i, 0)))
    return pl.pallas_call(
        body, name=f"pair_sum{tag}", grid_spec=grid_spec,
        out_shape=jax.ShapeDtypeStruct((N_CHIPS, rows, 1024), BF16),
        compiler_params=_params(2),
    )(c_arr, gpack, recv)


def _scatter_steps(p_ref, out_ref, send_sems, recv_sems):
    x, y, c = _place()
    chips = [(1 - x, y), (x, 1 - y), (1 - x, 1 - y)]
    me = 2 * x + y
    cps = [pltpu.make_async_remote_copy(
        src_ref=p_ref.at[2 * chip[0] + chip[1]], dst_ref=out_ref.at[me],
        send_sem=send_sems.at[j], recv_sem=recv_sems.at[j],
        device_id=(*chip, c), device_id_type=_MESH) for j, chip in enumerate(chips)]

    def start():
        for cp in cps:
            cp.start()

    def finish():
        for cp in cps:
            cp.wait_recv()
        for cp in cps:
            cp.wait_send()

    return start, finish


def _scatter_pieces(psum, tag):
    def body(p_ref, out_ref, send_sems, recv_sems):
        for step in _scatter_steps(p_ref, out_ref, send_sems, recv_sems):
            step()

    return pl.pallas_call(
        body, name=f"scatter_pieces{tag}",
        in_specs=[_ANY], out_specs=_ANY,
        out_shape=jax.ShapeDtypeStruct(psum.shape, psum.dtype),
        scratch_shapes=list(_SCATTER_SEMS),
    )(psum)


def _sum_pieces(pieces, tag, *, tr=PACK_TILE):
    rows = pieces.shape[1]

    def body(p_ref, o_ref):
        acc = p_ref[0].astype(F32) + p_ref[1].astype(F32)
        acc = acc + p_ref[2].astype(F32)
        o_ref[...] = acc + p_ref[3].astype(F32)

    return pl.pallas_call(
        body, name=f"sum_pieces{tag}", grid=(rows // tr,),
        in_specs=[pl.BlockSpec((N_CHIPS, tr, 1024), lambda i: (0, i, 0))],
        out_specs=pl.BlockSpec((tr, 1024), lambda i: (i, 0)),
        out_shape=jax.ShapeDtypeStruct((rows, 1024), F32),
        compiler_params=_params(),
    )(pieces)


def _exchange_halves(totals):
    n = len(totals)

    def body(*refs):
        t_refs, out_refs, send_sem, recv_sem = refs[:n], refs[n:2 * n], refs[2 * n], refs[2 * n + 1]
        x, y, c = _place()
        cps = [pltpu.make_async_remote_copy(
            src_ref=t_refs[i], dst_ref=out_refs[i].at[c], send_sem=send_sem.at[i], recv_sem=recv_sem.at[i],
            device_id=(x, y, 1 - c), device_id_type=_MESH) for i in range(n)]
        for cp in cps:
            cp.start()
        for cp in cps:
            cp.wait_recv()
        for cp in cps:
            cp.wait_send()

    return pl.pallas_call(
        body, name="exchange_halves",
        in_specs=[_ANY] * n, out_specs=[_ANY] * n,
        out_shape=[jax.ShapeDtypeStruct((2,) + t.shape, F32) for t in totals],
        scratch_shapes=[pltpu.SemaphoreType.DMA((n,)), pltpu.SemaphoreType.DMA((n,))],
    )(*totals)


def _pad_rows(a, rows):
    return jnp.pad(a, ((0, rows - a.shape[0]), (0, 0)))


def _pack_weight_shards(w_in, w_mem_kv, w_out, pool_w, w_kv_shared, pool_scale):
    ps_bits = lax.bitcast_convert_type(pool_scale.reshape(-1), BF16).reshape(1, -1)
    ps_row = jnp.pad(ps_bits, ((0, 0), (0, 1024 - ps_bits.shape[1])))

    def common(l):
        return [w_in[l].astype(BF16).reshape(ROWS_W_IN, 1024),
                w_mem_kv[l].astype(BF16).reshape(ROWS_W_MKV, 1024),
                w_out[l].astype(BF16).reshape(ROWS_W_OUT, 1024)]

    p0 = common(0) + [pool_w.astype(BF16).reshape(ROWS_POOL_W, 1024), _pad_rows(ps_row, ROWS_SMALL),
                      jnp.zeros((PACK0_ROWS - OFF_LN_G, 1024), BF16)]
    p1 = common(1) + [_pad_rows(w_kv_shared.astype(BF16).reshape(KV_SHARD, 1024), ROWS_W_KV),
                      jnp.zeros((PACK1_ROWS - OFF_BF, 1024), BF16)]
    return (jnp.concatenate(p0, axis=0).reshape(2, PACK0_ROWS // 2, 1024),
            jnp.concatenate(p1, axis=0).reshape(2, PACK1_ROWS // 2, 1024))


def _unpack_common(g):
    w_in = g[:, OFF_W_IN:OFF_W_IN + ROWS_W_IN].reshape(4, D_MODEL, D_IN // 4)
    w_in = w_in.transpose(1, 0, 2).reshape(D_MODEL, D_IN)
    w_mkv = g[:, OFF_W_MKV:OFF_W_MKV + ROWS_W_MKV].reshape(D_MODEL, 2 * D_MEM)
    w_out = g[:, OFF_W_OUT:OFF_W_OUT + ROWS_W_OUT].reshape(D_MIX, D_MODEL)
    return w_in, w_mkv, w_out


def _unpack_weights0(g):
    pool_w = g[:, OFF_POOL_W:OFF_POOL_W + ROWS_POOL_W].reshape(4, 4, POOL_GROUP // 4, POOL_GROUP)
    pool_w = pool_w.transpose(1, 0, 2, 3).reshape(4, POOL_GROUP, POOL_GROUP)
    ps_bits = g[:, OFF_POOL_S, 0:512].reshape(4, 256, 2)
    pool_scale = lax.bitcast_convert_type(ps_bits, F32).reshape(1, D_MAIN)
    return _unpack_common(g) + (pool_w, pool_scale)


def _unpack_weights1(g):
    w_kv = g[:, OFF_W_KV:OFF_W_KV + KV_SHARD].reshape(4, D_MODEL, KV_SHARD)
    w_kv = w_kv.transpose(1, 0, 2).reshape(D_MODEL, KV_COLS)
    return _unpack_common(g) + (w_kv,)


def _replicated_rows(a):
    a = _pad_rows(a, ROWS_SMALL)
    return jnp.broadcast_to(a[None], (4,) + a.shape)


def _pack_common_grads(g_w_in, g_w_mkv, g_w_out):
    return [g_w_in.reshape(D_MODEL, 4, D_IN // 4).transpose(1, 0, 2).reshape(4, ROWS_W_IN, 1024),
            g_w_mkv.reshape(4, ROWS_W_MKV, 1024), g_w_out.reshape(4, ROWS_W_OUT, 1024)]


def _pack_grads0(g_w_in, g_w_mkv, g_w_out, g_pool_w, g_pool_scale, g_ln_g, g_ln_b):
    parts = _pack_common_grads(g_w_in, g_w_mkv, g_w_out) + [
        g_pool_w.reshape(4, 4, POOL_GROUP // 4, POOL_GROUP).transpose(1, 0, 2, 3).reshape(4, ROWS_POOL_W, 1024),
        jnp.pad(g_pool_scale.reshape(4, 1, 256), ((0, 0), (0, ROWS_SMALL - 1), (0, 1024 - 256))),
        _replicated_rows(g_ln_g), _replicated_rows(g_ln_b),
        jnp.zeros((4, PACK0_ROWS - OFF_LN_B - ROWS_SMALL, 1024), F32),
    ]
    return jnp.concatenate(parts, axis=1).reshape(4, 2, PACK0_ROWS // 2, 1024)


def _pack_grads1(g_w_in, g_w_mkv, g_w_out, g_w_kv, g_bf):
    parts = _pack_common_grads(g_w_in, g_w_mkv, g_w_out) + [
        jnp.pad(g_w_kv.reshape(D_MODEL, 4, KV_SHARD).transpose(1, 0, 2).reshape(4, KV_SHARD, 1024),
                ((0, 0), (0, ROWS_W_KV - KV_SHARD), (0, 0))),
        _replicated_rows(jnp.pad(g_bf.reshape(1, -1), ((0, 0), (0, 1024 - g_bf.shape[0])))),
        jnp.zeros((4, PACK1_ROWS - OFF_BF - ROWS_SMALL, 1024), F32),
    ]
    return jnp.concatenate(parts, axis=1).reshape(4, 2, PACK1_ROWS // 2, 1024)


def _local_step(x, mem, target, w0, w1, ln_g, ln_b, b_forget, *, tm=256, tq=512, dist=None):
    S = x.shape[0]
    nq = S // tq
    w_in0, w_mkv0, w_out0, pool_w, pool_scale = w0
    g_rows = [ln_g[l:l + 1] for l in range(2)]
    b_rows = [ln_b[l:l + 1] for l in range(2)]
    bf_row = jnp.pad(b_forget.reshape(1, -1), ((0, 0), (0, LANES - FOX_HEADS)))

    mkv0 = _linear_fwd(mem, w_mkv0, tm=N_MEM, name="mem_kv0")
    h0 = _linear_fwd(x, w_in0, tm=tm, name="in_proj0")
    f0 = _mix("pool", "fwd", h=h0, xres=x, mkv=mkv0, w_out=w_out0, ln_g=g_rows[0], ln_b=b_rows[0],
              pool_w=pool_w, pool_scale=pool_scale, gather=None if dist is None else w1, tm=tm)
    z0, x1 = f0["z"], f0["xout"]
    if dist is not None:
        gathered1 = lax.dynamic_update_slice(f0["gathered"], w1[None], (dist["me"], 0, 0, 0))
        w1 = _unpack_weights1(gathered1.reshape(N_CHIPS, PACK1_ROWS, 1024))
    w_in1, w_mkv1, w_out1, w_kv = w1
    w_in, w_out = [w_in0, w_in1], [w_out0, w_out1]
    w_kvp = jnp.pad(w_kv, ((0, 0), (0, LANES - FOX_HEADS)))
    mkv = [mkv0, _linear_fwd(mem, w_mkv1, tm=N_MEM, name="mem_kv1")]
    k, v, fl, cum = _kv_proj(x1, w_kvp, bf_row, tm=tm)
    h1 = _linear_fwd(x1, w_in[1], tm=tm, name="in_proj1")
    aug = functools.partial(_augment, tm=tm)
    ka = aug(k, [cum], [-1.0], val_lane=AUG_B, ones_lane=AUG_A, scale=1.0, name="aug_k")
    va = aug(v, [], [], val_lane=None, ones_lane=AUG_A, scale=1.0, name="aug_v")
    qf = aug(h1, [cum], [1.0], val_lane=AUG_A, ones_lane=AUG_B, scale=FOX_SCALE, name="aug_q_fwd")
    ymain1, lse_rows = _fox_fwd(qf, ka, va, tq=tq)

    b1 = _mix("fox", "bwd", h=h1, xres=x1, mkv=mkv[1], w_out=w_out[1], ln_g=g_rows[1], ln_b=b_rows[1],
              ymain=ymain1, target=target, tm=tm)
    lse16 = lse_rows[:, :, 0:FOX_HEADS // lse_rows.shape[0], :].transpose(0, 2, 1, 3).reshape(FOX_HEADS, S)
    dq_aug, dk, dv, dck_rows = _fox_bwd(qf, ka, va, b1["dmain"], _rows_for_pairs(lse16, nq, tq),
                                        _rows_for_pairs(b1["drow"], nq, tq), tq=tq)
    du1, df, dbf = _gate_bwd(dq_aug, _cols_of_rows(dck_rows, S), fl, tm=tm)

    dx1a, dw_in1 = _lin_bwd(x1, [du1, b1["drest"]], w_in[1], [(b1["dz"], ALPHA)], tm=tm, name="in_proj1_bwd")
    dx1, dw_kvp = _lin_bwd(x1, [dk, dv, df], w_kvp, [(dx1a, 1.0)], tm=tm, name="kv_proj_bwd")

    dw_mkv1 = _wgrad(mem, b1["dmkv"], name="mem_kv1_bwd")
    g_w_kv, g_bf = dw_kvp[:, 0:KV_COLS], dbf[0, 0:FOX_HEADS]

    psum1 = None
    if dist is not None:
        gpack1 = _pack_grads1(dw_in1, dw_mkv1, b1["dw_out"], g_w_kv, g_bf)
        psum1 = _pair_sum(gpack1, _send_half_to_sibling(gpack1, 1), dist["c_arr"], 1)

    b0 = _mix("pool", "bwd", h=h0, mkv=mkv[0], w_out=w_out[0], ln_g=g_rows[0],
              pool_w=pool_w, pool_scale=pool_scale, z=z0, dy=dx1, tm=tm)
    outs = _lin_bwd(x, [b0["dmain"], b0["drest"]], w_in[0], [(b0["dz"], ALPHA)], tm=tm, name="in_proj0_bwd",
                    scatter=psum1)
    dx, dw_in0 = outs[0], outs[1]
    dw_mkv0 = _wgrad(mem, b0["dmkv"], name="mem_kv0_bwd")
    g_ln_g = jnp.concatenate([b0["dln_g"], b1["dln_g"]], axis=0)
    g_ln_b = jnp.concatenate([b0["dln_b"], b1["dln_b"]], axis=0)

    if dist is None:
        grads = dict(w_in=[dw_in0, dw_in1], w_mem_kv=[dw_mkv0, dw_mkv1], w_out=[b0["dw_out"], b1["dw_out"]],
                     ln_g=g_ln_g, ln_b=g_ln_b, pool_w=b0["dpool_w"], pool_scale=b0["dpool_scale"],
                     w_kv=g_w_kv, b_forget=g_bf)
        return b1["loss"], dx, grads

    me, my_c = dist["me"], dist["my_c"]

    def with_own(pieces, psum):
        own = lax.dynamic_slice(psum, (me, 0, 0), (1,) + psum.shape[1:])
        return lax.dynamic_update_slice(pieces, own, (me, 0, 0))

    total1 = _sum_pieces(with_own(outs[2], psum1), 1)
    gpack0 = _pack_grads0(dw_in0, dw_mkv0, b0["dw_out"], b0["dpool_w"], b0["dpool_scale"], g_ln_g, g_ln_b)
    psum0 = _pair_sum(gpack0, _send_half_to_sibling(gpack0, 0), dist["c_arr"], 0)
    total0 = _sum_pieces(with_own(_scatter_pieces(psum0, 0), psum0), 0)
    halves0, halves1 = _exchange_halves([total0, total1])
    shard0 = lax.dynamic_update_slice(halves0, total0[None], (my_c, 0, 0)).reshape(PACK0_ROWS, 1024)
    shard1 = lax.dynamic_update_slice(halves1, total1[None], (my_c, 0, 0)).reshape(PACK1_ROWS, 1024)
    return b1["loss"], dx, shard0, shard1


def kernel(x, mem, w_in, w_mem_kv, w_out, ln_g, ln_b, pool_w, pool_scale, w_kv_shared, b_forget, loss_target, m_w_in, m_w_mem_kv, m_w_out, m_ln_g, m_ln_b, m_pool_w, m_pool_scale, m_w_kv_shared, m_b_forget, v_w_in, v_w_mem_kv, v_w_out, v_ln_g, v_ln_b, v_pool_w, v_pool_scale, v_w_kv_shared, v_b_forget):
    dist = dict(c_arr=lax.axis_index("c").astype(jnp.int32).reshape(1),
                me=2 * lax.axis_index("x") + lax.axis_index("y"), my_c=lax.axis_index("c"))

    wpack0, wpack1 = _pack_weight_shards(w_in, w_mem_kv, w_out, pool_w, w_kv_shared, pool_scale)
    gathered0 = lax.dynamic_update_slice(_all_gather_shards(wpack0), wpack0[None], (dist["me"], 0, 0, 0))
    w0 = _unpack_weights0(gathered0.reshape(N_CHIPS, PACK0_ROWS, 1024))

    loss_vec, dx, shard0, shard1 = _local_step(x[0], mem[0], loss_target[0], w0, wpack1, ln_g, ln_b, b_forget,
                                               dist=dist)
    loss = lax.psum(0.5 / D_MODEL * jnp.sum(loss_vec), ("x", "y", "c"))

    def per_layer(off, rows, shape):
        return jnp.concatenate([shard0[off:off + rows], shard1[off:off + rows]], axis=0).reshape(shape)

    g_w_in = per_layer(OFF_W_IN, ROWS_W_IN, w_in.shape)
    g_w_mkv = per_layer(OFF_W_MKV, ROWS_W_MKV, w_mem_kv.shape)
    g_w_out = per_layer(OFF_W_OUT, ROWS_W_OUT, w_out.shape)
    g_pool_w = shard0[OFF_POOL_W:OFF_POOL_W + ROWS_POOL_W].reshape(pool_w.shape)
    g_w_kv = shard1[OFF_W_KV:OFF_W_KV + KV_SHARD].reshape(w_kv_shared.shape)
    g_pool_scale = shard0[OFF_POOL_S:OFF_POOL_S + 1, 0:256].reshape(pool_scale.shape)
    g_ln_g = shard0[OFF_LN_G:OFF_LN_G + 2]
    g_ln_b = shard0[OFF_LN_B:OFF_LN_B + 2]
    g_bf = shard1[OFF_BF, 0:FOX_HEADS]

    names = ["w_in", "w_mem_kv", "w_out", "ln_g", "ln_b", "pool_w", "pool_scale", "w_kv_shared", "b_forget"]
    ws = [w_in, w_mem_kv, w_out, ln_g, ln_b, pool_w, pool_scale, w_kv_shared, b_forget]
    gs = [g_w_in, g_w_mkv, g_w_out, g_ln_g, g_ln_b, g_pool_w, g_pool_scale, g_w_kv, g_bf]
    ms = [m_w_in, m_w_mem_kv, m_w_out, m_ln_g, m_ln_b, m_pool_w, m_pool_scale, m_w_kv_shared, m_b_forget]
    vs = [v_w_in, v_w_mem_kv, v_w_out, v_ln_g, v_ln_b, v_pool_w, v_pool_scale, v_w_kv_shared, v_b_forget]
    deltas, new_ms, new_vs = [], [], []
    for nm, w, gg, mm, vv in zip(names, ws, gs, ms, vs):
        two_d = (-1, w.shape[-1])
        d, nmm, nvv = _adamw(w.reshape(two_d), gg.reshape(two_d), mm.reshape(two_d), vv.reshape(two_d),
                             name=f"adamw_{nm}")
        deltas.append(d.reshape(w.shape))
        new_ms.append(nmm.reshape(w.shape))
        new_vs.append(nvv.reshape(w.shape))

    return (loss, dx[None], *gs, *deltas, *new_ms, *new_vs)
```

```python
import jax
import jax.numpy as jnp
from jax import lax
from jax.experimental import pallas as pl
from jax.experimental.pallas import tpu as pltpu

F32 = jnp.float32
BF16 = jnp.bfloat16

D_MODEL = 1024
D_MAIN = 1024
D_MEM = 512
D_MIX = D_MAIN + D_MEM
D_IN = 2 * D_MIX
N_MEM = 256
MEM_HEADS = 4
MEM_HEAD_DIM = 128
FOX_HEADS = 16
FOX_HEAD_DIM = 64
FOX_SCALE = 0.125
POOL_WINDOWS = (2, 4, 8, 16)
POOL_GROUP = 256
POOL_HALO = 16
ALPHA = 4.0 ** 0.25
LN_EPS = 1e-5
LANES = 128
N_CHIPS = 4

ADAM_LR = 0.001
ADAM_B1 = 0.9
ADAM_B2 = 0.999
ADAM_EPS = 1e-08
ADAM_WD = 0.01
ADAM_STEP = 10

VMEM_LIMIT = 56 * 1024 * 1024

ROWS_W_IN = D_MODEL * (D_IN // N_CHIPS) // 1024
ROWS_W_MKV = (D_MODEL // N_CHIPS) * 2 * D_MEM // 1024
ROWS_W_OUT = (D_MIX // N_CHIPS) * D_MODEL // 1024
ROWS_POOL_W = 4 * (POOL_GROUP // N_CHIPS) * POOL_GROUP // 1024
KV_COLS = 2 * D_MAIN + FOX_HEADS
KV_SHARD = KV_COLS // N_CHIPS
ROWS_W_KV = 528
ROWS_SMALL = 16
OFF_W_IN = 0
OFF_W_MKV = OFF_W_IN + ROWS_W_IN
OFF_W_OUT = OFF_W_MKV + ROWS_W_MKV
OFF_TAIL = OFF_W_OUT + ROWS_W_OUT
OFF_POOL_W = OFF_TAIL
OFF_POOL_S = OFF_POOL_W + ROWS_POOL_W
OFF_LN_G = OFF_POOL_S + ROWS_SMALL
OFF_LN_B = OFF_LN_G + ROWS_SMALL
PACK0_ROWS = 1536
OFF_W_KV = OFF_TAIL
OFF_BF = OFF_W_KV + ROWS_W_KV
PACK1_ROWS = 2048
PACK_TILE = 256


def _dot(a, b):
    return jnp.dot(a, b, preferred_element_type=F32)


def _dot_nt(a, b):
    return lax.dot_general(a, b, (((1,), (1,)), ((), ())), preferred_element_type=F32)


def _dot_tn(a, b):
    return lax.dot_general(a, b, (((0,), (0,)), ((), ())), preferred_element_type=F32)


def _params(n_axes=1):
    return pltpu.CompilerParams(dimension_semantics=("arbitrary",) * n_axes,
                                vmem_limit_bytes=VMEM_LIMIT)


def _const_spec(shape):
    zeros = (0,) * len(shape)
    return pl.BlockSpec(shape, lambda *_: zeros, pipeline_mode=pl.Buffered(1))


def _split3(x):
    hi = x.astype(BF16)
    r = x - hi.astype(F32)
    mid = r.astype(BF16)
    lo = (r - mid.astype(F32)).astype(BF16)
    return hi, mid, lo


def _linear_fwd(x, w, *, tm, name, q_cum=None):
    S, K = x.shape
    N = w.shape[1]
    nc = 512 if N % 512 == 0 else N
    aug = q_cum is not None

    def body(*refs):
        x_ref, w_ref = refs[0], refs[1]
        o_ref = refs[4] if aug else refs[2]
        xb = x_ref[...].astype(BF16)
        if aug:
            lhs = _placement_lhs(refs[2][...], tm)
        for n0 in range(0, N, nc):
            r = _dot(xb, w_ref[:, n0:n0 + nc])
            o_ref[:, n0:n0 + nc] = r.astype(BF16)
            if aug and n0 < D_MAIN:
                _store_augmented(refs[5], r * FOX_SCALE, n0 // LANES, lhs, refs[3])

    in_specs = [pl.BlockSpec((tm, K), lambda i: (i, 0)), _const_spec((K, N))]
    out_specs = [pl.BlockSpec((tm, N), lambda i: (i, 0))]
    out_shape = [jax.ShapeDtypeStruct((S, N), BF16)]
    extra = []
    if aug:
        in_specs += [pl.BlockSpec((tm, LANES), lambda i: (i, 0)), _const_spec((LANES, AUG_W))]
        out_specs.append(pl.BlockSpec((tm, AUG_W), lambda i: (i, 0)))
        out_shape.append(jax.ShapeDtypeStruct((S, AUG_W), BF16))
        extra = [q_cum, _placement(AUG_A, AUG_B)]
    outs = pl.pallas_call(
        body, name=name, grid=(S // tm,),
        in_specs=in_specs, out_specs=out_specs, out_shape=out_shape,
        compiler_params=_params(),
    )(x, w, *extra)
    return outs if aug else outs[0]


def _kv_proj(x1, w_kv, bf_row, *, tm):
    S = x1.shape[0]

    def body(x_ref, w_ref, b_ref, pk_ref, pv_ref, k_ref, v_ref, fl_ref, cum_ref, carry_ref):
        i = pl.program_id(0)

        @pl.when(i == 0)
        def _():
            carry_ref[...] = jnp.zeros_like(carry_ref)

        xb = x_ref[...].astype(BF16)
        fl = _dot(xb, w_ref[:, 2 * D_MAIN:2 * D_MAIN + LANES]) + b_ref[...]
        fl_ref[...] = fl
        log_f = jnp.minimum(fl, 0.0) - jnp.log1p(jnp.exp(-jnp.abs(fl)))
        r = lax.broadcasted_iota(jnp.int32, (tm, tm), 0)
        c = lax.broadcasted_iota(jnp.int32, (tm, tm), 1)
        tri = jnp.where(c <= r, 1.0, 0.0).astype(BF16)
        hi, mid, lo = _split3(log_f)
        cum = (_dot(tri, hi) + _dot(tri, mid)) + _dot(tri, lo) + carry_ref[0:1, :]
        cum_ref[...] = cum
        carry_ref[0:1, :] = cum[tm - 1:tm, :]
        lhs_k, lhs_v = _placement_lhs(-cum, tm), _placement_lhs(None, tm)
        for n0 in range(0, D_MAIN, 512):
            _store_augmented(k_ref, _dot(xb, w_ref[:, n0:n0 + 512]), n0 // LANES, lhs_k, pk_ref)
            _store_augmented(v_ref, _dot(xb, w_ref[:, D_MAIN + n0:D_MAIN + n0 + 512]), n0 // LANES, lhs_v, pv_ref)

    return pl.pallas_call(
        body, name="kv_proj", grid=(S // tm,),
        in_specs=[pl.BlockSpec((tm, D_MODEL), lambda i: (i, 0)),
                  _const_spec((D_MODEL, 2 * D_MAIN + LANES)), _const_spec((1, LANES)),
                  _const_spec((LANES, AUG_W)), _const_spec((LANES, AUG_W))],
        out_specs=[pl.BlockSpec((tm, AUG_W), lambda i: (i, 0)),
                   pl.BlockSpec((tm, AUG_W), lambda i: (i, 0)),
                   pl.BlockSpec((tm, LANES), lambda i: (i, 0)),
                   pl.BlockSpec((tm, LANES), lambda i: (i, 0))],
        out_shape=[jax.ShapeDtypeStruct((S, AUG_W), BF16), jax.ShapeDtypeStruct((S, AUG_W), BF16),
                   jax.ShapeDtypeStruct((S, LANES), F32), jax.ShapeDtypeStruct((S, LANES), F32)],
        scratch_shapes=[pltpu.VMEM((8, LANES), F32)],
        compiler_params=_params(),
    )(x1, w_kv, bf_row, _placement(AUG_B, AUG_A), _placement(None, AUG_A))


def _gate_bwd(dq_aug, dck, fl, *, tm):
    S = fl.shape[0]
    n = S // tm

    def body(dq_ref, dck_ref, fl_ref, du_ref, df_ref, db_ref, carry_ref):
        i = pl.program_id(0)

        @pl.when(i == 0)
        def _():
            carry_ref[...] = jnp.zeros_like(carry_ref)
            db_ref[...] = jnp.zeros_like(db_ref)

        lane = lax.broadcasted_iota(jnp.int32, (tm, LANES), 1)
        half0 = lane < FOX_HEAD_DIM
        dcq = jnp.zeros((tm, LANES), F32)
        for hp in range(FOX_HEADS // 2):
            b0 = dq_ref[:, 2 * hp * LANES:(2 * hp + 1) * LANES]
            b1 = dq_ref[:, (2 * hp + 1) * LANES:(2 * hp + 2) * LANES]
            du_ref[:, hp * LANES:(hp + 1) * LANES] = (
                jnp.where(half0, b0, pltpu.roll(b1, FOX_HEAD_DIM, 1)) * FOX_SCALE).astype(BF16)
            r0 = jnp.sum(jnp.where(lane == AUG_A, b0, 0.0), axis=1, keepdims=True)
            r1 = jnp.sum(jnp.where(lane == AUG_A, b1, 0.0), axis=1, keepdims=True)
            dcq = dcq + jnp.where(lane == 2 * hp, r0, 0.0) + jnp.where(lane == 2 * hp + 1, r1, 0.0)
        dcum = dcq - dck_ref[...]
        r = lax.broadcasted_iota(jnp.int32, (tm, tm), 0)
        c = lax.broadcasted_iota(jnp.int32, (tm, tm), 1)
        tri = jnp.where(c >= r, 1.0, 0.0).astype(BF16)
        hi, mid, lo = _split3(dcum)
        rev = (_dot(tri, hi) + _dot(tri, mid)) + _dot(tri, lo) + carry_ref[0:1, :]
        carry_ref[0:1, :] = rev[0:1, :]
        fl_v = fl_ref[...]
        df = rev * (1.0 / (1.0 + jnp.exp(fl_v)))
        df_ref[...] = df
        db_ref[...] += jnp.sum(df, axis=0, keepdims=True)

    return pl.pallas_call(
        body, name="gate_bwd", grid=(n,),
        in_specs=[pl.BlockSpec((tm, AUG_W), lambda i: (n - 1 - i, 0)),
                  pl.BlockSpec((tm, LANES), lambda i: (n - 1 - i, 0)),
                  pl.BlockSpec((tm, LANES), lambda i: (n - 1 - i, 0))],
        out_specs=[pl.BlockSpec((tm, D_MAIN), lambda i: (n - 1 - i, 0)),
                   pl.BlockSpec((tm, LANES), lambda i: (n - 1 - i, 0)),
                   pl.BlockSpec((1, LANES), lambda i: (0, 0))],
        out_shape=[jax.ShapeDtypeStruct((S, D_MAIN), BF16),
                   jax.ShapeDtypeStruct((S, LANES), F32), jax.ShapeDtypeStruct((1, LANES), F32)],
        scratch_shapes=[pltpu.VMEM((8, LANES), F32)],
        compiler_params=_params(),
    )(dq_aug, dck, fl)


def _silu_and_grad(g):
    sg = 1.0 / (1.0 + jnp.exp(-g))
    return g * sg, sg * (1.0 + g * (1.0 - sg))


def _mix(kind, mode, *, h, xres=None, mkv, w_out, ln_g, ln_b=None, pool_w=None, pool_scale=None,
         ymain=None, target=None, z=None, dy=None, gather=None, pair_send=None, tm):
    S = h.shape[0]
    n = S // tm
    pool = kind == "pool"
    bwd = mode == "bwd"
    loss_head = bwd and not pool
    rev = pool and bwd
    mem_scale = MEM_HEAD_DIM ** -0.5

    def t_of(i):
        return (n - 1 - i) if rev else i

    row = lambda i: (t_of(i), 0)
    names, arrays, specs = [], [], []

    def add(name, arr, spec):
        names.append(name)
        arrays.append(arr)
        specs.append(spec)

    add("h", h, pl.BlockSpec((tm, D_IN), row))
    if pool:
        hb = tm // POOL_HALO
        add("halo", h, pl.BlockSpec((POOL_HALO, D_MAIN), lambda i: (jnp.maximum(t_of(i) * hb - 1, 0), 0)))
        add("pool_w", pool_w, _const_spec((4, POOL_GROUP, POOL_GROUP)))
        add("pool_scale", pool_scale, _const_spec((1, D_MAIN)))
    else:
        add("ymain", ymain, pl.BlockSpec((tm, D_MAIN), row))
    add("mkv", mkv, _const_spec((N_MEM, 2 * D_MEM)))
    add("w_out", w_out, _const_spec((D_MIX, D_MODEL)))
    add("ln_g", ln_g, _const_spec((1, D_MODEL)))
    if not (pool and bwd):
        add("xres", xres, pl.BlockSpec((tm, D_MODEL), row))
        add("ln_b", ln_b, _const_spec((1, D_MODEL)))
    if loss_head:
        add("target", target, pl.BlockSpec((tm, D_MODEL), row))
    if pool and bwd:
        add("z", z, pl.BlockSpec((tm, D_MODEL), row))
        add("dy", dy, pl.BlockSpec((tm, D_MODEL), row))
    if gather is not None:
        add("gather_src", gather, _ANY)
    if pair_send is not None:
        add("pair_src", pair_send, _ANY)

    onames, oshapes, ospecs = [], [], []

    def add_out(name, shape, dtype, spec):
        onames.append(name)
        oshapes.append(jax.ShapeDtypeStruct(shape, dtype))
        ospecs.append(spec)

    const2 = lambda i: (0, 0)
    if not bwd:
        add_out("z", (S, D_MODEL), F32, pl.BlockSpec((tm, D_MODEL), row))
        add_out("xout", (S, D_MODEL), F32, pl.BlockSpec((tm, D_MODEL), row))
    else:
        add_out("dz", (S, D_MODEL), F32, pl.BlockSpec((tm, D_MODEL), row))
        if pool:
            add_out("dmain", (S, D_MAIN), BF16, pl.BlockSpec((tm, D_MAIN), row))
        else:
            add_out("dmain", (S, 2 * D_MAIN), BF16, pl.BlockSpec((tm, 2 * D_MAIN), row))
        add_out("drest", (S, D_IN - D_MAIN), BF16, pl.BlockSpec((tm, D_IN - D_MAIN), row))
        add_out("dw_out", (D_MIX, D_MODEL), F32, pl.BlockSpec((D_MIX, D_MODEL), const2))
        add_out("dmkv", (N_MEM, 2 * D_MEM), F32, pl.BlockSpec((N_MEM, 2 * D_MEM), const2))
        add_out("dln_g", (1, D_MODEL), F32, pl.BlockSpec((1, D_MODEL), const2))
        add_out("dln_b", (1, D_MODEL), F32, pl.BlockSpec((1, D_MODEL), const2))
        if pool:
            add_out("dpool_w", (4, POOL_GROUP, POOL_GROUP), F32,
                    pl.BlockSpec((4, POOL_GROUP, POOL_GROUP), lambda i: (0, 0, 0)))
            add_out("dpool_scale", (1, D_MAIN), F32, pl.BlockSpec((1, D_MAIN), const2))
        else:
            add_out("loss", (1, D_MODEL), F32, pl.BlockSpec((1, D_MODEL), const2))
            add_out("drow", (FOX_HEADS, S), F32, pl.BlockSpec((FOX_HEADS, tm), lambda i: (0, i)))

    if gather is not None:
        add_out("gathered", (N_CHIPS,) + gather.shape, gather.dtype, _ANY)
    if pair_send is not None:
        add_out("from_sibling", (N_CHIPS,) + pair_send.shape[2:], pair_send.dtype, _ANY)

    scratch = [pltpu.VMEM((tm, D_MIX), BF16),
               pltpu.VMEM((tm, D_MEM), F32)]
    if pool:
        scratch.append(pltpu.VMEM((tm + 2 * POOL_HALO, D_MAIN), F32))
    if rev:
        scratch.append(pltpu.VMEM((POOL_HALO, D_MAIN), F32))
    if gather is not None:
        scratch += list(_GATHER_SEMS)
    if pair_send is not None:
        scratch += list(_PAIR_SEMS)
    n_in, n_out = len(names), len(onames)

    def body(*refs):
        R = dict(zip(names, refs[:n_in]))
        O = dict(zip(onames, refs[n_in:n_in + n_out]))
        sc = refs[n_in + n_out:]
        yc_ref, ymem_ref = sc[0], sc[1]
        ext_ref = sc[2] if pool else None
        carry_ref = sc[3] if rev else None
        i = pl.program_id(0)
        t = t_of(i)
        h_ref = R["h"]
        gamma = R["ln_g"][...]

        if gather is not None:
            start, forward, finish = _gather_steps(R["gather_src"], O["gathered"], sc[-2], sc[-1])
            pl.when(i == 0)(start)
            pl.when(i == n - 3)(forward)
            pl.when(i == n - 1)(finish)
        if pair_send is not None:
            start, finish = _pair_send_steps(R["pair_src"], O["from_sibling"], sc[-2], sc[-1])
            pl.when(i == 0)(start)
            pl.when(i == n - 1)(finish)

        if bwd:
            @pl.when(i == 0)
            def _():
                for nm in ("dw_out", "dmkv", "dln_g", "dln_b", "dpool_w", "dpool_scale", "loss"):
                    if nm in O:
                        O[nm][...] = jnp.zeros_like(O[nm])
                if rev:
                    carry_ref[...] = jnp.zeros_like(carry_ref)

        if pool:
            u = h_ref[:, 0:D_MAIN].astype(F32)
            halo = R["halo"][...].astype(F32)
            ext_ref[0:POOL_HALO, :] = jnp.where(t > 0, halo, 0.0)
            ext_ref[POOL_HALO:POOL_HALO + tm, :] = u
            tpos = t * tm + lax.broadcasted_iota(jnp.int32, (tm, 1), 0)
            pms, invcs = [], []
            for gi, w in enumerate(POOL_WINDOWS):
                cs = slice(gi * POOL_GROUP, (gi + 1) * POOL_GROUP)
                acc = ext_ref[POOL_HALO:POOL_HALO + tm, cs]
                for k in range(1, w):
                    acc = acc + ext_ref[POOL_HALO - k:POOL_HALO - k + tm, cs]
                invc = 1.0 / jnp.minimum(tpos + 1, w).astype(F32)
                pm = (acc * invc - u[:, cs]).astype(BF16)
                pms.append(pm)
                invcs.append(invc)
            mixed = [_dot(pms[gi], R["pool_w"][gi]) for gi in range(4)]
            ps = R["pool_scale"][...]
            y_main = [mixed[gi] * ps[:, gi * POOL_GROUP:(gi + 1) * POOL_GROUP] for gi in range(4)]
        else:
            y_main = [R["ymain"][:, gi * 256:(gi + 1) * 256].astype(F32) for gi in range(4)]

        probs = []
        for hd in range(MEM_HEADS):
            sl = slice(D_MAIN + hd * MEM_HEAD_DIM, D_MAIN + (hd + 1) * MEM_HEAD_DIM)
            ksl = slice(hd * MEM_HEAD_DIM, (hd + 1) * MEM_HEAD_DIM)
            vsl = slice(D_MEM + hd * MEM_HEAD_DIM, D_MEM + (hd + 1) * MEM_HEAD_DIM)
            s = _dot_nt(h_ref[:, sl], R["mkv"][:, ksl]) * mem_scale
            e = jnp.exp(s - jnp.max(s, axis=1, keepdims=True))
            p = e / jnp.sum(e, axis=1, keepdims=True)
            probs.append(p)
            ymem_ref[:, ksl] = _dot(p.astype(BF16), R["mkv"][:, vsl])

        g_off = D_MIX
        gate_d = []
        for gi in range(4):
            cs = slice(gi * 256, (gi + 1) * 256)
            gm = h_ref[:, g_off + gi * 256:g_off + (gi + 1) * 256].astype(F32)
            sv, sd = _silu_and_grad(gm)
            yc_ref[:, cs] = (y_main[gi] * sv).astype(BF16)
            gate_d.append((sv, sd))
        gq = h_ref[:, g_off + D_MAIN:D_IN].astype(F32)
        svq, sdq = _silu_and_grad(gq)
        yc_ref[:, D_MAIN:D_MIX] = (ymem_ref[...] * svq).astype(BF16)

        if pool and bwd:
            zt = R["z"][...]
        else:
            o = _dot(yc_ref[...], R["w_out"][...])
            zt = ALPHA * R["xres"][...] + o
        mu = jnp.mean(zt, axis=1, keepdims=True)
        zc = zt - mu
        var = jnp.mean(zc * zc, axis=1, keepdims=True)
        rstd = lax.rsqrt(var + LN_EPS)
        xhat = zc * rstd
        if not bwd:
            O["z"][...] = zt
            O["xout"][...] = xhat * gamma + R["ln_b"][...]
            return

        if loss_head:
            xo = xhat * gamma + R["ln_b"][...]
            err = xo - R["target"][...]
            O["loss"][...] += jnp.sum(err * err, axis=0, keepdims=True)
            dyt = err * (1.0 / D_MODEL)
        else:
            dyt = R["dy"][...]

        O["dln_g"][...] += jnp.sum(dyt * xhat, axis=0, keepdims=True)
        O["dln_b"][...] += jnp.sum(dyt, axis=0, keepdims=True)
        gdy = dyt * gamma
        m1 = jnp.mean(gdy, axis=1, keepdims=True)
        m2 = jnp.mean(gdy * xhat, axis=1, keepdims=True)
        dz = rstd * (gdy - m1 - xhat * m2)
        O["dz"][...] = dz
        dzb = dz.astype(BF16)

        for n0 in range(0, D_MIX, 512):
            O["dw_out"][n0:n0 + 512, :] += _dot_tn(yc_ref[:, n0:n0 + 512], dzb)
        dyc_mem = _dot_nt(dzb, R["w_out"][D_MAIN:D_MIX, :])

        O["drest"][:, D_MEM + D_MAIN:D_MEM + D_MAIN + D_MEM] = (dyc_mem * ymem_ref[...] * sdq).astype(BF16)
        dymem = dyc_mem * svq
        for hd in range(MEM_HEADS):
            sl = slice(D_MAIN + hd * MEM_HEAD_DIM, D_MAIN + (hd + 1) * MEM_HEAD_DIM)
            ksl = slice(hd * MEM_HEAD_DIM, (hd + 1) * MEM_HEAD_DIM)
            vsl = slice(D_MEM + hd * MEM_HEAD_DIM, D_MEM + (hd + 1) * MEM_HEAD_DIM)
            p = probs[hd]
            dyb = dymem[:, ksl].astype(BF16)
            dp = _dot_nt(dyb, R["mkv"][:, vsl])
            ds = p * (dp - jnp.sum(dp * p, axis=1, keepdims=True)) * mem_scale
            dsb = ds.astype(BF16)
            O["drest"][:, ksl] = _dot(dsb, R["mkv"][:, ksl]).astype(BF16)
            O["dmkv"][:, ksl] += _dot_tn(dsb, h_ref[:, sl])
            O["dmkv"][:, vsl] += _dot_tn(p.astype(BF16), dyb)

        dmain = []
        for gi in range(4):
            cs = slice(gi * 256, (gi + 1) * 256)
            dyc_g = _dot_nt(dzb, R["w_out"][cs, :])
            sv, sd = gate_d[gi]
            O["drest"][:, D_MEM + gi * 256:D_MEM + (gi + 1) * 256] = (dyc_g * y_main[gi] * sd).astype(BF16)
            dmain.append(dyc_g * sv)

        if not pool:
            prod = []
            lane2 = lax.broadcasted_iota(jnp.int32, (tm, LANES), 1)
            for gi in range(4):
                cs = slice(gi * 256, (gi + 1) * 256)
                db16 = dmain[gi].astype(BF16)
                dbf = db16.astype(F32)
                prod.append(dbf * R["ymain"][:, cs].astype(F32))
                for pr in range(2):
                    blk = dbf[:, pr * LANES:(pr + 1) * LANES]
                    base = (4 * gi + 2 * pr) * LANES
                    O["dmain"][:, base:base + LANES] = jnp.where(lane2 < FOX_HEAD_DIM, blk, 0.0).astype(BF16)
                    O["dmain"][:, base + LANES:base + 2 * LANES] = jnp.where(
                        lane2 < FOX_HEAD_DIM, pltpu.roll(blk, FOX_HEAD_DIM, 1), 0.0).astype(BF16)
            dcol = jnp.zeros((tm, LANES), F32)
            for gi in range(4):
                dr = lax.broadcasted_iota(jnp.int32, (256, LANES), 0)
                hc = lax.broadcasted_iota(jnp.int32, (256, LANES), 1)
                sel = jnp.where(jnp.right_shift(dr, 6) + gi * 4 == hc, 1.0, 0.0).astype(BF16)
                hi, mid, lo = _split3(prod[gi])
                dcol = dcol + ((_dot(hi, sel) + _dot(mid, sel)) + _dot(lo, sel))
            O["drow"][...] = dcol.T[0:FOX_HEADS, :]
            return

        ps = R["pool_scale"][...]
        dpm_list = []
        for gi in range(4):
            cs = slice(gi * 256, (gi + 1) * 256)
            O["dpool_scale"][:, cs] += jnp.sum(dmain[gi] * mixed[gi], axis=0, keepdims=True)
            dmix = (dmain[gi] * ps[:, cs]).astype(BF16)
            O["dpool_w"][gi] += _dot_tn(pms[gi], dmix)
            dpm = _dot_nt(dmix, R["pool_w"][gi])
            dpm_list.append(dpm)
            ext_ref[0:tm, cs] = dpm * invcs[gi]
        ext_ref[tm:tm + POOL_HALO, :] = carry_ref[...]
        carry_ref[...] = ext_ref[0:POOL_HALO, :]
        for gi, w in enumerate(POOL_WINDOWS):
            cs = slice(gi * 256, (gi + 1) * 256)
            acc = ext_ref[0:tm, cs]
            for k in range(1, w):
                acc = acc + ext_ref[k:k + tm, cs]
            O["dmain"][:, cs] = (acc - dpm_list[gi]).astype(BF16)

    outs = pl.pallas_call(
        body, name=f"mix_{kind}_{mode}", grid=(n,),
        in_specs=specs, out_specs=ospecs, out_shape=oshapes,
        scratch_shapes=scratch, compiler_params=_params(),
    )(*arrays)
    return dict(zip(onames, outs))


def _lin_bwd(xin, dhs, w, res, *, tm, name, scatter=None):
    S, K = xin.shape
    N = w.shape[1]
    nj = len(dhs)
    nr = len(res)
    n_tiles = S // tm
    widths = [dh.shape[1] for dh in dhs]
    assert sum(widths) == N
    scales = [s for _, s in res]
    n_in = 2 + nj + nr + (1 if scatter is not None else 0)

    def body(*refs):
        x_ref = refs[0]
        dh_refs = refs[1:1 + nj]
        w_ref = refs[1 + nj]
        r_refs = refs[2 + nj:2 + nj + nr]
        dx_ref, dw_ref = refs[n_in], refs[n_in + 1]
        i = pl.program_id(0)

        if scatter is not None:
            start, finish = _scatter_steps(refs[n_in - 1], refs[n_in + 2], refs[n_in + 3], refs[n_in + 4])
            pl.when(i == 0)(start)
            pl.when(i == n_tiles - 1)(finish)

        @pl.when(i == 0)
        def _():
            dw_ref[...] = jnp.zeros_like(dw_ref)

        xb = x_ref[...].astype(BF16)
        dx = jnp.zeros((tm, K), F32)
        for r_ref, sc in zip(r_refs, scales):
            dx = dx + sc * r_ref[...]
        off = 0
        for j in range(nj):
            nc = 512 if widths[j] % 512 == 0 else widths[j]
            for n0 in range(0, widths[j], nc):
                dhb = dh_refs[j][:, n0:n0 + nc].astype(BF16)
                dx = dx + _dot_nt(dhb, w_ref[:, off + n0:off + n0 + nc])
                dw_ref[:, off + n0:off + n0 + nc] += _dot_tn(xb, dhb)
            off += widths[j]
        dx_ref[...] = dx

    in_specs = [pl.BlockSpec((tm, K), lambda i: (i, 0))]
    in_specs += [pl.BlockSpec((tm, n), lambda i: (i, 0)) for n in widths]
    in_specs += [_const_spec((K, N))]
    in_specs += [pl.BlockSpec((tm, K), lambda i: (i, 0)) for _ in res]
    out_specs = [pl.BlockSpec((tm, K), lambda i: (i, 0)), pl.BlockSpec((K, N), lambda i: (0, 0))]
    out_shape = [jax.ShapeDtypeStruct((S, K), F32), jax.ShapeDtypeStruct((K, N), F32)]
    extra, scratch = [], []
    if scatter is not None:
        in_specs.append(_ANY)
        out_specs.append(_ANY)
        out_shape.append(jax.ShapeDtypeStruct(scatter.shape, scatter.dtype))
        extra, scratch = [scatter], list(_SCATTER_SEMS)
    return pl.pallas_call(
        body, name=name, grid=(n_tiles,),
        in_specs=in_specs, out_specs=out_specs, out_shape=out_shape, scratch_shapes=scratch,
        compiler_params=_params(),
    )(xin, *dhs, w, *[r for r, _ in res], *extra)


def _wgrad(xin, dh, *, name):
    M, K = xin.shape
    N = dh.shape[1]

    def body(x_ref, dh_ref, o_ref):
        o_ref[...] = _dot_tn(x_ref[...].astype(BF16), dh_ref[...].astype(BF16))

    return pl.pallas_call(
        body, name=name, out_shape=jax.ShapeDtypeStruct((K, N), F32),
        compiler_params=pltpu.CompilerParams(vmem_limit_bytes=VMEM_LIMIT),
    )(xin, dh)


AUG_A = FOX_HEAD_DIM
AUG_B = FOX_HEAD_DIM + 3
AUG_W = FOX_HEADS * LANES


def _placement(val_lane, ones_lane):
    r = jnp.arange(LANES)[:, None]
    c = jnp.arange(AUG_W)[None, :]
    head, lane = c // LANES, c % LANES
    m = jnp.zeros((LANES, AUG_W), jnp.bool_)
    if val_lane is not None:
        for part in range(3):
            m = m | ((r == part * FOX_HEADS + head) & (lane == val_lane + part))
    if ones_lane is not None:
        m = m | ((r == 3 * FOX_HEADS) & (lane >= ones_lane) & (lane < ones_lane + 3))
    return m.astype(BF16)


def _placement_lhs(val, tm):
    lane = lax.broadcasted_iota(jnp.int32, (tm, LANES), 1)
    lhs = jnp.where(lane == 3 * FOX_HEADS, 1.0, 0.0)
    if val is not None:
        hi, mid, lo = [p.astype(F32) for p in _split3(val)]
        lhs = jnp.where(lane < FOX_HEADS, hi, jnp.where(
            lane < 2 * FOX_HEADS, pltpu.roll(mid, FOX_HEADS, 1), jnp.where(
                lane < 3 * FOX_HEADS, pltpu.roll(lo, 2 * FOX_HEADS, 1), lhs)))
    return lhs.astype(BF16)


def _store_augmented(o_ref, data, first_pair, lhs, p_ref):
    tm = data.shape[0]
    is_data = lax.broadcasted_iota(jnp.int32, (tm, LANES), 1) < FOX_HEAD_DIM
    for j in range(data.shape[1] // LANES):
        base = 2 * (first_pair + j) * LANES
        extra = _dot(lhs, p_ref[:, base:base + 2 * LANES])
        blk = data[:, j * LANES:(j + 1) * LANES]
        o_ref[:, base:base + LANES] = jnp.where(is_data, blk, extra[:, 0:LANES]).astype(BF16)
        o_ref[:, base + LANES:base + 2 * LANES] = jnp.where(
            is_data, pltpu.roll(blk, FOX_HEAD_DIM, 1), extra[:, LANES:2 * LANES]).astype(BF16)


def _cols_of_rows(rows, S):
    nh = FOX_HEADS // rows.shape[0]
    a = rows[:, :, 0:nh, :].transpose(0, 2, 1, 3).reshape(FOX_HEADS, S).T
    return jnp.pad(a, ((0, 0), (0, LANES - FOX_HEADS)))


def _fox_fwd(qf, ka, va, *, tq, nh=4):
    S = ka.shape[0]
    nq = S // tq
    tk = tq
    ng = FOX_HEADS // nh

    def body(q_ref, k_ref, v_ref, o_ref, lse_ref, *scratch):
        p_scr, m_scr, acc_scr = scratch[0:nh], scratch[nh:2 * nh], scratch[2 * nh:3 * nh]
        qi = pl.program_id(1)
        lane = lax.broadcasted_iota(jnp.int32, (tq, LANES), 1)
        half0 = lane < FOX_HEAD_DIM
        rr = lax.broadcasted_iota(jnp.int32, (tq, tk), 0)
        cc = lax.broadcasted_iota(jnp.int32, (tq, tk), 1)
        sls = [slice(hh * LANES, (hh + 1) * LANES) for hh in range(nh)]
        qs = [q_ref[:, sl] for sl in sls]

        for hh in range(nh):
            p_scr[hh][...] = jnp.zeros_like(p_scr[hh])
            m_scr[hh][...] = jnp.full(m_scr[hh].shape, -jnp.inf, F32)
            acc_scr[hh][...] = jnp.zeros_like(acc_scr[hh])

        def chunk(ki, masked):
            k0 = pl.multiple_of(ki * tk, tk)
            kp = pl.multiple_of(jnp.maximum(ki - 1, 0) * tk, tk)
            for hh in range(nh):
                m = m_scr[hh][...]
                s = _dot_nt(qs[hh], k_ref[pl.ds(k0, tk), sls[hh]])
                pv = _dot(p_scr[hh][...], v_ref[pl.ds(kp, tk), sls[hh]])
                if masked:
                    s = jnp.where(cc <= rr, s, -jnp.inf)
                m_new = jnp.maximum(m, jnp.max(s, axis=1, keepdims=True))
                p_scr[hh][...] = jnp.exp(s - jnp.tile(m_new, (1, tk // LANES))).astype(BF16)
                acc_scr[hh][...] = (acc_scr[hh][...] + pv) * jnp.exp(m - m_new)
                m_scr[hh][...] = m_new

        def trip(ki, c):
            chunk(ki, False)
            return c

        lax.fori_loop(0, qi, trip, 0)
        chunk(qi, True)
        kq = pl.multiple_of(qi * tk, tk)
        outs = []
        lse_cols = jnp.zeros((tq, LANES), F32)
        for hh in range(nh):
            m = m_scr[hh][...]
            acc = acc_scr[hh][...] + _dot(p_scr[hh][...], v_ref[pl.ds(kq, tk), sls[hh]])
            l = jnp.sum(jnp.where(lane == AUG_A, acc, 0.0), axis=1, keepdims=True)
            outs.append(acc / l)
            lse_cols = lse_cols + jnp.where(lane == hh, m + jnp.log(l), 0.0)
        for pr in range(nh // 2):
            o_ref[:, pr * LANES:(pr + 1) * LANES] = jnp.where(
                half0, outs[2 * pr], pltpu.roll(outs[2 * pr + 1], FOX_HEAD_DIM, 1))
        lse_ref[0, 0] = lse_cols.T[0:8, :]

    return pl.pallas_call(
        body, name="fox_fwd", grid=(ng, nq),
        in_specs=[pl.BlockSpec((tq, nh * LANES), lambda g, qi: (qi, g)),
                  pl.BlockSpec((S, nh * LANES), lambda g, qi: (0, g), pipeline_mode=pl.Buffered(1)),
                  pl.BlockSpec((S, nh * LANES), lambda g, qi: (0, g), pipeline_mode=pl.Buffered(1))],
        out_specs=[pl.BlockSpec((tq, nh * FOX_HEAD_DIM), lambda g, qi: (qi, g)),
                   pl.BlockSpec((1, 1, 8, tq), lambda g, qi: (g, qi, 0, 0))],
        out_shape=[jax.ShapeDtypeStruct((S, D_MAIN), F32),
                   jax.ShapeDtypeStruct((ng, nq, 8, tq), F32)],
        scratch_shapes=([pltpu.VMEM((tq, tk), BF16)] * nh + [pltpu.VMEM((tq, LANES), F32)] * nh
                        + [pltpu.VMEM((tq, LANES), F32)] * nh),
        compiler_params=_params(2),
    )(qf, ka, va)


def _rows_for_pairs(a16, nt, tt):
    a = a16.reshape(8, 2, nt, tt).transpose(0, 2, 1, 3)
    return jnp.pad(a, ((0, 0), (0, 0), (0, 6), (0, 0)))


def _fox_bwd(qf, ka, va, do_aug, lse_rows, d_rows, *, tq):
    S = ka.shape[0]
    nq = S // tq
    tk = tq

    def body(k_ref, v_ref, q_ref, do_ref, lser_ref, dr_ref, dq_ref, dk_ref, dv_ref, dck_ref, dk_scr, dv_scr):
        kj = pl.program_id(1)

        @pl.when(kj == 0)
        def _():
            dq_ref[...] = jnp.zeros_like(dq_ref)

        lane = lax.broadcasted_iota(jnp.int32, (tk, LANES), 1)
        half0 = lane < FOX_HEAD_DIM
        rr = lax.broadcasted_iota(jnp.int32, (tk, tq), 0)
        cc = lax.broadcasted_iota(jnp.int32, (tk, tq), 1)
        sls = [slice(hh * LANES, (hh + 1) * LANES) for hh in range(2)]
        kts = [k_ref[:, sl] for sl in sls]
        vts = [v_ref[:, sl] for sl in sls]

        dk_scr[...] = jnp.zeros_like(dk_scr)
        dv_scr[...] = jnp.zeros_like(dv_scr)

        def chunk(qi, masked):
            q0 = pl.multiple_of(qi * tq, tq)
            for hh in range(2):
                qc = q_ref[pl.ds(q0, tq), sls[hh]]
                doc = do_ref[pl.ds(q0, tq), sls[hh]]
                lse = lser_ref[0, qi][hh:hh + 1, :]
                dd = dr_ref[0, qi][hh:hh + 1, :]
                pt = jnp.exp(_dot_nt(kts[hh], qc) - lse)
                if masked:
                    pt = jnp.where(rr <= cc, pt, 0.0)
                dsb = (pt * (_dot_nt(vts[hh], doc) - dd)).astype(BF16)
                dv_scr[hh] += _dot(pt.astype(BF16), doc)
                dk_scr[hh] += _dot(dsb, qc)
                dq_ref[pl.ds(q0, tq), sls[hh]] += _dot_tn(dsb, kts[hh])

        def trip(qi, c):
            chunk(qi, False)
            return c

        chunk(kj, True)
        lax.fori_loop(kj + 1, nq, trip, 0)
        dk0, dk1 = dk_scr[0], dk_scr[1]
        dv0, dv1 = dv_scr[0], dv_scr[1]
        dk_ref[...] = jnp.where(half0, dk0, pltpu.roll(dk1, FOX_HEAD_DIM, 1)).astype(BF16)
        dv_ref[...] = jnp.where(half0, dv0, pltpu.roll(dv1, FOX_HEAD_DIM, 1)).astype(BF16)
        c0 = jnp.sum(jnp.where(lane == AUG_B, dk0, 0.0), axis=1, keepdims=True)
        c1 = jnp.sum(jnp.where(lane == AUG_B, dk1, 0.0), axis=1, keepdims=True)
        dck_cols = jnp.where(lane == 0, c0, 0.0) + jnp.where(lane == 1, c1, 0.0)
        dck_ref[0, 0] = dck_cols.T[0:8, :]

    return pl.pallas_call(
        body, name="fox_bwd", grid=(8, nq),
        in_specs=[pl.BlockSpec((tk, 2 * LANES), lambda hp, kj: (kj, hp)),
                  pl.BlockSpec((tk, 2 * LANES), lambda hp, kj: (kj, hp)),
                  pl.BlockSpec((S, 2 * LANES), lambda hp, kj: (0, hp)),
                  pl.BlockSpec((S, 2 * LANES), lambda hp, kj: (0, hp)),
                  pl.BlockSpec((1, nq, 8, tq), lambda hp, kj: (hp, 0, 0, 0)),
                  pl.BlockSpec((1, nq, 8, tq), lambda hp, kj: (hp, 0, 0, 0))],
        out_specs=[pl.BlockSpec((S, 2 * LANES), lambda hp, kj: (0, hp)),
                   pl.BlockSpec((tk, LANES), lambda hp, kj: (kj, hp)),
                   pl.BlockSpec((tk, LANES), lambda hp, kj: (kj, hp)),
                   pl.BlockSpec((1, 1, 8, tk), lambda hp, kj: (hp, kj, 0, 0))],
        out_shape=[jax.ShapeDtypeStruct((S, AUG_W), F32),
                   jax.ShapeDtypeStruct((S, D_MAIN), BF16),
                   jax.ShapeDtypeStruct((S, D_MAIN), BF16),
                   jax.ShapeDtypeStruct((8, nq, 8, tk), F32)],
        scratch_shapes=[pltpu.VMEM((2, tk, LANES), F32), pltpu.VMEM((2, tk, LANES), F32)],
        compiler_params=_params(2),
    )(ka, va, qf, do_aug, lse_rows, d_rows)


def _adamw(w, g, m, v, *, name):
    Rr, C = w.shape
    tr = 256 if Rr % 256 == 0 else Rr
    c1 = 1.0 / (1.0 - ADAM_B1 ** ADAM_STEP)
    c2 = 1.0 / (1.0 - ADAM_B2 ** ADAM_STEP)

    def body(w_ref, g_ref, m_ref, v_ref, d_ref, nm_ref, nv_ref):
        gv = g_ref[...]
        nm = ADAM_B1 * m_ref[...] + (1.0 - ADAM_B1) * gv
        nv = ADAM_B2 * v_ref[...] + (1.0 - ADAM_B2) * (gv * gv)
        d_ref[...] = -ADAM_LR * ((nm * c1) / (jnp.sqrt(nv * c2) + ADAM_EPS) + ADAM_WD * w_ref[...])
        nm_ref[...] = nm
        nv_ref[...] = nv

    spec = pl.BlockSpec((tr, C), lambda i: (i, 0))
    sds = jax.ShapeDtypeStruct((Rr, C), F32)
    return pl.pallas_call(
        body, name=name, grid=(Rr // tr,),
        in_specs=[spec] * 4, out_specs=[spec] * 3, out_shape=[sds] * 3,
        compiler_params=_params(),
    )(w, g, m, v)


_ANY = pl.BlockSpec(memory_space=pl.ANY)
_MESH = pl.DeviceIdType.MESH


def _place():
    x, y, c = lax.axis_index("x"), lax.axis_index("y"), lax.axis_index("c")
    return x, y, c


def _gather_steps(p_ref, out_ref, send_sems, recv_sems):
    x, y, c = _place()
    sib = (x, y, 1 - c)
    chips = [(1 - x, y), (x, 1 - y), (1 - x, 1 - y)]
    idx = [2 * chip[0] + chip[1] for chip in chips]
    me = 2 * x + y

    def copy(k, chip_idx, half, to, src=None):
        dst = out_ref.at[chip_idx, half]
        return pltpu.make_async_remote_copy(
            src_ref=dst if src is None else src, dst_ref=dst,
            send_sem=send_sems.at[k], recv_sem=recv_sems.at[k],
            device_id=to, device_id_type=_MESH)

    first = [copy(j, me, c, (*chip, c), src=p_ref.at[c]) for j, chip in enumerate(chips)]
    passed = [copy(3 + j, idx[j], c, sib) for j in range(3)]

    def start():
        for cp in first:
            cp.start()

    def forward():
        for j in range(3):
            copy(j, idx[j], c, sib).wait_recv()
            passed[j].start()

    def finish():
        for j in range(3):
            copy(3 + j, idx[j], 1 - c, sib).wait_recv()
        for cp in first + passed:
            cp.wait_send()

    return start, forward, finish


_GATHER_SEMS = [pltpu.SemaphoreType.DMA((6,)), pltpu.SemaphoreType.DMA((6,))]
_SCATTER_SEMS = [pltpu.SemaphoreType.DMA((3,)), pltpu.SemaphoreType.DMA((3,))]


def _gathered_shape(pack):
    return jax.ShapeDtypeStruct((N_CHIPS,) + pack.shape, pack.dtype)


def _all_gather_shards(pack):
    def body(p_ref, out_ref, send_sems, recv_sems):
        for step in _gather_steps(p_ref, out_ref, send_sems, recv_sems):
            step()

    return pl.pallas_call(
        body, name="all_gather_shards",
        in_specs=[_ANY], out_specs=_ANY, out_shape=_gathered_shape(pack),
        scratch_shapes=list(_GATHER_SEMS),
    )(pack)


def _pair_send_steps(g_ref, out_ref, send_sem, recv_sem):
    x, y, c = _place()
    cps = [pltpu.make_async_remote_copy(
        src_ref=g_ref.at[j, 1 - c], dst_ref=out_ref.at[j],
        send_sem=send_sem.at[j], recv_sem=recv_sem.at[j],
        device_id=(x, y, 1 - c), device_id_type=_MESH) for j in range(N_CHIPS)]

    def start():
        for cp in cps:
            cp.start()

    def finish():
        for cp in cps:
            cp.wait_recv()
        for cp in cps:
            cp.wait_send()

    return start, finish


_PAIR_SEMS = [pltpu.SemaphoreType.DMA((N_CHIPS,)), pltpu.SemaphoreType.DMA((N_CHIPS,))]


def _from_sibling_shape(gpack):
    return jax.ShapeDtypeStruct((N_CHIPS,) + gpack.shape[2:], gpack.dtype)


def _send_half_to_sibling(gpack, tag):
    def body(g_ref, out_ref, send_sem, recv_sem):
        for step in _pair_send_steps(g_ref, out_ref, send_sem, recv_sem):
            step()

    return pl.pallas_call(
        body, name=f"pair_send{tag}",
        in_specs=[_ANY], out_specs=_ANY, out_shape=_from_sibling_shape(gpack),
        scratch_shapes=list(_PAIR_SEMS),
    )(gpack)


def _pair_sum(gpack, recv, c_arr, tag, *, tr=PACK_TILE):
    rows = recv.shape[1]

    def body(c_ref, a_ref, b_ref, o_ref):
        o_ref[...] = (a_ref[...] + b_ref[...]).astype(BF16)

    grid_spec = pltpu.PrefetchScalarGridSpec(
        num_scalar_prefetch=1, grid=(N_CHIPS, rows // tr),
        in_specs=[pl.BlockSpec((None, None, tr, 1024), lambda j, i, c_ref: (j, c_ref[0], i, 0)),
                  pl.BlockSpec((None, tr, 1024), lambda j, i, c_ref: (j, i, 0))],
        out_specs=pl.BlockSpec((None, tr, 1024), lambda j, i, c_ref: (j, i, 0)))
    return pl.pallas_call(
        body, name=f"pair_sum{tag}", grid_spec=grid_spec,
        out_shape=jax.ShapeDtypeStruct((N_CHIPS, rows, 1024), BF16),
        compiler_params=_params(2),
    )(c_arr, gpack, recv)


def _scatter_steps(p_ref, out_ref, send_sems, recv_sems):
    x, y, c = _place()
    chips = [(1 - x, y), (x, 1 - y), (1 - x, 1 - y)]
    me = 2 * x + y
    cps = [pltpu.make_async_remote_copy(
        src_ref=p_ref.at[2 * chip[0] + chip[1]], dst_ref=out_ref.at[me],
        send_sem=send_sems.at[j], recv_sem=recv_sems.at[j],
        device_id=(*chip, c), device_id_type=_MESH) for j, chip in enumerate(chips)]

    def start():
        for cp in cps:
            cp.start()

    def finish():
        for cp in cps:
            cp.wait_recv()
        for cp in cps:
            cp.wait_send()

    return start, finish


def _scatter_pieces(psum, tag):
    def body(p_ref, out_ref, send_sems, recv_sems):
        for step in _scatter_steps(p_ref, out_ref, send_sems, recv_sems):
            step()

    return pl.pallas_call(
        body, name=f"scatter_pieces{tag}",
        in_specs=[_ANY], out_specs=_ANY,
        out_shape=jax.ShapeDtypeStruct(psum.shape, psum.dtype),
        scratch_shapes=list(_SCATTER_SEMS),
    )(psum)


def _sum_pieces(pieces, tag, *, tr=PACK_TILE):
    rows = pieces.shape[1]

    def body(p_ref, o_ref):
        acc = p_ref[0].astype(F32) + p_ref[1].astype(F32)
        acc = acc + p_ref[2].astype(F32)
        o_ref[...] = acc + p_ref[3].astype(F32)

    return pl.pallas_call(
        body, name=f"sum_pieces{tag}", grid=(rows // tr,),
        in_specs=[pl.BlockSpec((N_CHIPS, tr, 1024), lambda i: (0, i, 0))],
        out_specs=pl.BlockSpec((tr, 1024), lambda i: (i, 0)),
        out_shape=jax.ShapeDtypeStruct((rows, 1024), F32),
        compiler_params=_params(),
    )(pieces)


def _exchange_halves(totals):
    n = len(totals)

    def body(*refs):
        t_refs, out_refs, send_sem, recv_sem = refs[:n], refs[n:2 * n], refs[2 * n], refs[2 * n + 1]
        x, y, c = _place()
        cps = [pltpu.make_async_remote_copy(
            src_ref=t_refs[i], dst_ref=out_refs[i].at[c], send_sem=send_sem.at[i], recv_sem=recv_sem.at[i],
            device_id=(x, y, 1 - c), device_id_type=_MESH) for i in range(n)]
        for cp in cps:
            cp.start()
        for cp in cps:
            cp.wait_recv()
        for cp in cps:
            cp.wait_send()

    return pl.pallas_call(
        body, name="exchange_halves",
        in_specs=[_ANY] * n, out_specs=[_ANY] * n,
        out_shape=[jax.ShapeDtypeStruct((2,) + t.shape, F32) for t in totals],
        scratch_shapes=[pltpu.SemaphoreType.DMA((n,)), pltpu.SemaphoreType.DMA((n,))],
    )(*totals)


def _pad_rows(a, rows):
    return jnp.pad(a, ((0, rows - a.shape[0]), (0, 0)))


def _pack_weight_shards(w_in, w_mem_kv, w_out, pool_w, w_kv_shared, pool_scale):
    ps_bits = lax.bitcast_convert_type(pool_scale.reshape(-1), BF16).reshape(1, -1)
    ps_row = jnp.pad(ps_bits, ((0, 0), (0, 1024 - ps_bits.shape[1])))

    def common(l):
        return [w_in[l].astype(BF16).reshape(ROWS_W_IN, 1024),
                w_mem_kv[l].astype(BF16).reshape(ROWS_W_MKV, 1024),
                w_out[l].astype(BF16).reshape(ROWS_W_OUT, 1024)]

    p0 = common(0) + [pool_w.astype(BF16).reshape(ROWS_POOL_W, 1024), _pad_rows(ps_row, ROWS_SMALL),
                      jnp.zeros((PACK0_ROWS - OFF_LN_G, 1024), BF16)]
    p1 = common(1) + [_pad_rows(w_kv_shared.astype(BF16).reshape(KV_SHARD, 1024), ROWS_W_KV),
                      jnp.zeros((PACK1_ROWS - OFF_BF, 1024), BF16)]
    return (jnp.concatenate(p0, axis=0).reshape(2, PACK0_ROWS // 2, 1024),
            jnp.concatenate(p1, axis=0).reshape(2, PACK1_ROWS // 2, 1024))


def _unpack_common(g):
    w_in = g[:, OFF_W_IN:OFF_W_IN + ROWS_W_IN].reshape(4, D_MODEL, D_IN // 4)
    w_in = w_in.transpose(1, 0, 2).reshape(D_MODEL, D_IN)
    w_mkv = g[:, OFF_W_MKV:OFF_W_MKV + ROWS_W_MKV].reshape(D_MODEL, 2 * D_MEM)
    w_out = g[:, OFF_W_OUT:OFF_W_OUT + ROWS_W_OUT].reshape(D_MIX, D_MODEL)
    return w_in, w_mkv, w_out


def _unpack_weights0(g):
    pool_w = g[:, OFF_POOL_W:OFF_POOL_W + ROWS_POOL_W].reshape(4, 4, POOL_GROUP // 4, POOL_GROUP)
    pool_w = pool_w.transpose(1, 0, 2, 3).reshape(4, POOL_GROUP, POOL_GROUP)
    ps_bits = g[:, OFF_POOL_S, 0:512].reshape(4, 256, 2)
    pool_scale = lax.bitcast_convert_type(ps_bits, F32).reshape(1, D_MAIN)
    return _unpack_common(g) + (pool_w, pool_scale)


def _unpack_weights1(g):
    w_kv = g[:, OFF_W_KV:OFF_W_KV + KV_SHARD].reshape(4, D_MODEL, KV_SHARD)
    w_kv = w_kv.transpose(1, 0, 2).reshape(D_MODEL, KV_COLS)
    return _unpack_common(g) + (w_kv,)


def _replicated_rows(a):
    a = _pad_rows(a, ROWS_SMALL)
    return jnp.broadcast_to(a[None], (4,) + a.shape)


def _pack_common_grads(g_w_in, g_w_mkv, g_w_out):
    return [g_w_in.reshape(D_MODEL, 4, D_IN // 4).transpose(1, 0, 2).reshape(4, ROWS_W_IN, 1024),
            g_w_mkv.reshape(4, ROWS_W_MKV, 1024), g_w_out.reshape(4, ROWS_W_OUT, 1024)]


def _pack_grads0(g_w_in, g_w_mkv, g_w_out, g_pool_w, g_pool_scale, g_ln_g, g_ln_b):
    parts = _pack_common_grads(g_w_in, g_w_mkv, g_w_out) + [
        g_pool_w.reshape(4, 4, POOL_GROUP // 4, POOL_GROUP).transpose(1, 0, 2, 3).reshape(4, ROWS_POOL_W, 1024),
        jnp.pad(g_pool_scale.reshape(4, 1, 256), ((0, 0), (0, ROWS_SMALL - 1), (0, 1024 - 256))),
        _replicated_rows(g_ln_g), _replicated_rows(g_ln_b),
        jnp.zeros((4, PACK0_ROWS - OFF_LN_B - ROWS_SMALL, 1024), F32),
    ]
    return jnp.concatenate(parts, axis=1).reshape(4, 2, PACK0_ROWS // 2, 1024)


def _pack_grads1(g_w_in, g_w_mkv, g_w_out, g_w_kv, g_bf):
    parts = _pack_common_grads(g_w_in, g_w_mkv, g_w_out) + [
        jnp.pad(g_w_kv.reshape(D_MODEL, 4, KV_SHARD).transpose(1, 0, 2).reshape(4, KV_SHARD, 1024),
                ((0, 0), (0, ROWS_W_KV - KV_SHARD), (0, 0))),
        _replicated_rows(jnp.pad(g_bf.reshape(1, -1), ((0, 0), (0, 1024 - g_bf.shape[0])))),
        jnp.zeros((4, PACK1_ROWS - OFF_BF - ROWS_SMALL, 1024), F32),
    ]
    return jnp.concatenate(parts, axis=1).reshape(4, 2, PACK1_ROWS // 2, 1024)


def _local_step(x, mem, target, w0, w1, ln_g, ln_b, b_forget, *, tm=256, tq=512, dist=None):
    S = x.shape[0]
    nq = S // tq
    w_in0, w_mkv0, w_out0, pool_w, pool_scale = w0
    g_rows = [ln_g[l:l + 1] for l in range(2)]
    b_rows = [ln_b[l:l + 1] for l in range(2)]
    bf_row = jnp.pad(b_forget.reshape(1, -1), ((0, 0), (0, LANES - FOX_HEADS)))

    mkv0 = _linear_fwd(mem, w_mkv0, tm=N_MEM, name="mem_kv0")
    h0 = _linear_fwd(x, w_in0, tm=tm, name="in_proj0")
    f0 = _mix("pool", "fwd", h=h0, xres=x, mkv=mkv0, w_out=w_out0, ln_g=g_rows[0], ln_b=b_rows[0],
              pool_w=pool_w, pool_scale=pool_scale, gather=None if dist is None else w1, tm=tm)
    z0, x1 = f0["z"], f0["xout"]
    if dist is not None:
        gathered1 = lax.dynamic_update_slice(f0["gathered"], w1[None], (dist["me"], 0, 0, 0))
        w1 = _unpack_weights1(gathered1.reshape(N_CHIPS, PACK1_ROWS, 1024))
    w_in1, w_mkv1, w_out1, w_kv = w1
    w_in, w_out = [w_in0, w_in1], [w_out0, w_out1]
    w_kvp = jnp.pad(w_kv, ((0, 0), (0, LANES - FOX_HEADS)))
    mkv = [mkv0, _linear_fwd(mem, w_mkv1, tm=N_MEM, name="mem_kv1")]
    ka, va, fl, cum = _kv_proj(x1, w_kvp, bf_row, tm=tm)
    h1, qf = _linear_fwd(x1, w_in[1], tm=tm, name="in_proj1", q_cum=cum)
    ymain1, lse_rows = _fox_fwd(qf, ka, va, tq=tq)

    b1 = _mix("fox", "bwd", h=h1, xres=x1, mkv=mkv[1], w_out=w_out[1], ln_g=g_rows[1], ln_b=b_rows[1],
              ymain=ymain1, target=target, tm=tm)
    lse16 = lse_rows[:, :, 0:FOX_HEADS // lse_rows.shape[0], :].transpose(0, 2, 1, 3).reshape(FOX_HEADS, S)
    dq_aug, dk, dv, dck_rows = _fox_bwd(qf, ka, va, b1["dmain"], _rows_for_pairs(lse16, nq, tq),
                                        _rows_for_pairs(b1["drow"], nq, tq), tq=tq)
    du1, df, dbf = _gate_bwd(dq_aug, _cols_of_rows(dck_rows, S), fl, tm=tm)

    dx1a, dw_in1 = _lin_bwd(x1, [du1, b1["drest"]], w_in[1], [(b1["dz"], ALPHA)], tm=tm, name="in_proj1_bwd")
    dx1, dw_kvp = _lin_bwd(x1, [dk, dv, df], w_kvp, [(dx1a, 1.0)], tm=tm, name="kv_proj_bwd")

    dw_mkv1 = _wgrad(mem, b1["dmkv"], name="mem_kv1_bwd")
    g_w_kv, g_bf = dw_kvp[:, 0:KV_COLS], dbf[0, 0:FOX_HEADS]

    gpack1 = None if dist is None else _pack_grads1(dw_in1, dw_mkv1, b1["dw_out"], g_w_kv, g_bf)
    b0 = _mix("pool", "bwd", h=h0, mkv=mkv[0], w_out=w_out[0], ln_g=g_rows[0],
              pool_w=pool_w, pool_scale=pool_scale, z=z0, dy=dx1, pair_send=gpack1, tm=tm)
    psum1 = None if dist is None else _pair_sum(gpack1, b0["from_sibling"], dist["c_arr"], 1)
    outs = _lin_bwd(x, [b0["dmain"], b0["drest"]], w_in[0], [(b0["dz"], ALPHA)], tm=tm, name="in_proj0_bwd",
                    scatter=psum1)
    dx, dw_in0 = outs[0], outs[1]
    dw_mkv0 = _wgrad(mem, b0["dmkv"], name="mem_kv0_bwd")
    g_ln_g = jnp.concatenate([b0["dln_g"], b1["dln_g"]], axis=0)
    g_ln_b = jnp.concatenate([b0["dln_b"], b1["dln_b"]], axis=0)

    if dist is None:
        grads = dict(w_in=[dw_in0, dw_in1], w_mem_kv=[dw_mkv0, dw_mkv1], w_out=[b0["dw_out"], b1["dw_out"]],
                     ln_g=g_ln_g, ln_b=g_ln_b, pool_w=b0["dpool_w"], pool_scale=b0["dpool_scale"],
                     w_kv=g_w_kv, b_forget=g_bf)
        return b1["loss"], dx, grads

    me, my_c = dist["me"], dist["my_c"]

    def with_own(pieces, psum):
        own = lax.dynamic_slice(psum, (me, 0, 0), (1,) + psum.shape[1:])
        return lax.dynamic_update_slice(pieces, own, (me, 0, 0))

    total1 = _sum_pieces(with_own(outs[2], psum1), 1)
    gpack0 = _pack_grads0(dw_in0, dw_mkv0, b0["dw_out"], b0["dpool_w"], b0["dpool_scale"], g_ln_g, g_ln_b)
    psum0 = _pair_sum(gpack0, _send_half_to_sibling(gpack0, 0), dist["c_arr"], 0)
    total0 = _sum_pieces(with_own(_scatter_pieces(psum0, 0), psum0), 0)
    halves0, halves1 = _exchange_halves([total0, total1])
    shard0 = lax.dynamic_update_slice(halves0, total0[None], (my_c, 0, 0)).reshape(PACK0_ROWS, 1024)
    shard1 = lax.dynamic_update_slice(halves1, total1[None], (my_c, 0, 0)).reshape(PACK1_ROWS, 1024)
    return b1["loss"], dx, shard0, shard1


def kernel(x, mem, w_in, w_mem_kv, w_out, ln_g, ln_b, pool_w, pool_scale, w_kv_shared, b_forget, loss_target, m_w_in, m_w_mem_kv, m_w_out, m_ln_g, m_ln_b, m_pool_w, m_pool_scale, m_w_kv_shared, m_b_forget, v_w_in, v_w_mem_kv, v_w_out, v_ln_g, v_ln_b, v_pool_w, v_pool_scale, v_w_kv_shared, v_b_forget):
    dist = dict(c_arr=lax.axis_index("c").astype(jnp.int32).reshape(1),
                me=2 * lax.axis_index("x") + lax.axis_index("y"), my_c=lax.axis_index("c"))

    wpack0, wpack1 = _pack_weight_shards(w_in, w_mem_kv, w_out, pool_w, w_kv_shared, pool_scale)
    gathered0 = lax.dynamic_update_slice(_all_gather_shards(wpack0), wpack0[None], (dist["me"], 0, 0, 0))
    w0 = _unpack_weights0(gathered0.reshape(N_CHIPS, PACK0_ROWS, 1024))

    loss_vec, dx, shard0, shard1 = _local_step(x[0], mem[0], loss_target[0], w0, wpack1, ln_g, ln_b, b_forget,
                                               dist=dist)
    loss = lax.psum(0.5 / D_MODEL * jnp.sum(loss_vec), ("x", "y", "c"))

    def per_layer(off, rows, shape):
        return jnp.concatenate([shard0[off:off + rows], shard1[off:off + rows]], axis=0).reshape(shape)

    g_w_in = per_layer(OFF_W_IN, ROWS_W_IN, w_in.shape)
    g_w_mkv = per_layer(OFF_W_MKV, ROWS_W_MKV, w_mem_kv.shape)
    g_w_out = per_layer(OFF_W_OUT, ROWS_W_OUT, w_out.shape)
    g_pool_w = shard0[OFF_POOL_W:OFF_POOL_W + ROWS_POOL_W].reshape(pool_w.shape)
    g_w_kv = shard1[OFF_W_KV:OFF_W_KV + KV_SHARD].reshape(w_kv_shared.shape)
    g_pool_scale = shard0[OFF_POOL_S:OFF_POOL_S + 1, 0:256].reshape(pool_scale.shape)
    g_ln_g = shard0[OFF_LN_G:OFF_LN_G + 2]
    g_ln_b = shard0[OFF_LN_B:OFF_LN_B + 2]
    g_bf = shard1[OFF_BF, 0:FOX_HEADS]

    names = ["w_in", "w_mem_kv", "w_out", "ln_g", "ln_b", "pool_w", "pool_scale", "w_kv_shared", "b_forget"]
    ws = [w_in, w_mem_kv, w_out, ln_g, ln_b, pool_w, pool_scale, w_kv_shared, b_forget]
    gs = [g_w_in, g_w_mkv, g_w_out, g_ln_g, g_ln_b, g_pool_w, g_pool_scale, g_w_kv, g_bf]
    ms = [m_w_in, m_w_mem_kv, m_w_out, m_ln_g, m_ln_b, m_pool_w, m_pool_scale, m_w_kv_shared, m_b_forget]
    vs = [v_w_in, v_w_mem_kv, v_w_out, v_ln_g, v_ln_b, v_pool_w, v_pool_scale, v_w_kv_shared, v_b_forget]
    deltas, new_ms, new_vs = [], [], []
    for nm, w, gg, mm, vv in zip(names, ws, gs, ms, vs):
        two_d = (-1, w.shape[-1])
        d, nmm, nvv = _adamw(w.reshape(two_d), gg.reshape(two_d), mm.reshape(two_d), vv.reshape(two_d),
                             name=f"adamw_{nm}")
        deltas.append(d.reshape(w.shape))
        new_ms.append(nmm.reshape(w.shape))
        new_vs.append(nvv.reshape(w.shape))

    return (loss, dx[None], *gs, *deltas, *new_ms, *new_vs)
```

```python
import jax
import jax.numpy as jnp
from jax import lax
from jax.experimental import pallas as pl
from jax.experimental.pallas import tpu as pltpu

F32 = jnp.float32
BF16 = jnp.bfloat16

D_MODEL = 1024
D_MAIN = 1024
D_MEM = 512
D_MIX = D_MAIN + D_MEM
D_IN = 2 * D_MIX
N_MEM = 256
MEM_HEADS = 4
MEM_HEAD_DIM = 128
FOX_HEADS = 16
FOX_HEAD_DIM = 64
FOX_SCALE = 0.125
POOL_WINDOWS = (2, 4, 8, 16)
POOL_GROUP = 256
POOL_HALO = 16
ALPHA = 4.0 ** 0.25
LN_EPS = 1e-5
LANES = 128
N_CHIPS = 4

ADAM_LR = 0.001
ADAM_B1 = 0.9
ADAM_B2 = 0.999
ADAM_EPS = 1e-08
ADAM_WD = 0.01
ADAM_STEP = 10

VMEM_LIMIT = 56 * 1024 * 1024

ROWS_W_MKV = (D_MODEL // N_CHIPS) * 2 * D_MEM // 1024
ROWS_W_OUT = (D_MIX // N_CHIPS) * D_MODEL // 1024
ROWS_POOL_W = 4 * (POOL_GROUP // N_CHIPS) * POOL_GROUP // 1024
KV_COLS = 2 * D_MAIN + FOX_HEADS
KV_SHARD = KV_COLS // N_CHIPS
ROWS_W_KV = 528
ROWS_SMALL = 16
OFF_W_MKV = 0
OFF_W_OUT = OFF_W_MKV + ROWS_W_MKV
OFF_TAIL = OFF_W_OUT + ROWS_W_OUT
OFF_POOL_W = OFF_TAIL
OFF_POOL_S = OFF_POOL_W + ROWS_POOL_W
OFF_LN_G = OFF_POOL_S + ROWS_SMALL
OFF_LN_B = OFF_LN_G + ROWS_SMALL
PACK0_ROWS = 768
OFF_W_KV = OFF_TAIL
OFF_BF = OFF_W_KV + ROWS_W_KV
PACK1_ROWS = 1280
PACK_TILE = 128
W_IN_SHARD = D_IN // N_CHIPS


def _dot(a, b):
    return jnp.dot(a, b, preferred_element_type=F32)


def _dot_nt(a, b):
    return lax.dot_general(a, b, (((1,), (1,)), ((), ())), preferred_element_type=F32)


def _dot_tn(a, b):
    return lax.dot_general(a, b, (((0,), (0,)), ((), ())), preferred_element_type=F32)


def _params(n_axes=1):
    return pltpu.CompilerParams(dimension_semantics=("arbitrary",) * n_axes,
                                vmem_limit_bytes=VMEM_LIMIT)


def _const_spec(shape):
    zeros = (0,) * len(shape)
    return pl.BlockSpec(shape, lambda *_: zeros, pipeline_mode=pl.Buffered(1))


def _split3(x):
    hi = x.astype(BF16)
    r = x - hi.astype(F32)
    mid = r.astype(BF16)
    lo = (r - mid.astype(F32)).astype(BF16)
    return hi, mid, lo


def _cols_and_chunk(w):
    if w.ndim == 3:
        return N_CHIPS * w.shape[2], 256
    return w.shape[1], (512 if w.shape[1] % 512 == 0 else LANES)


def _w_cols(w_ref, n0, nc):
    if len(w_ref.shape) == 3:
        per = w_ref.shape[2]
        assert n0 // per == (n0 + nc - 1) // per
        return w_ref.at[n0 // per, :, n0 % per:n0 % per + nc]
    return w_ref.at[:, n0:n0 + nc]


def _linear_fwd(x, w, *, tm, name, q_cum=None):
    S, K = x.shape
    N, nc = _cols_and_chunk(w)
    aug = q_cum is not None

    def body(*refs):
        x_ref, w_ref = refs[0], refs[1]
        o_ref = refs[4] if aug else refs[2]
        xb = x_ref[...].astype(BF16)
        if aug:
            lhs = _placement_lhs(refs[2][...], tm)
        for n0 in range(0, N, nc):
            r = _dot(xb, _w_cols(w_ref, n0, nc)[...])
            o_ref[:, n0:n0 + nc] = r.astype(BF16)
            if aug and n0 < D_MAIN:
                _store_augmented(refs[5], r * FOX_SCALE, n0 // LANES, lhs, refs[3])

    in_specs = [pl.BlockSpec((tm, K), lambda i: (i, 0)), _const_spec(w.shape)]
    out_specs = [pl.BlockSpec((tm, N), lambda i: (i, 0))]
    out_shape = [jax.ShapeDtypeStruct((S, N), BF16)]
    extra = []
    if aug:
        in_specs += [pl.BlockSpec((tm, LANES), lambda i: (i, 0)), _const_spec((LANES, AUG_W))]
        out_specs.append(pl.BlockSpec((tm, AUG_W), lambda i: (i, 0)))
        out_shape.append(jax.ShapeDtypeStruct((S, AUG_W), BF16))
        extra = [q_cum, _placement(AUG_A, AUG_B)]
    outs = pl.pallas_call(
        body, name=name, grid=(S // tm,),
        in_specs=in_specs, out_specs=out_specs, out_shape=out_shape,
        compiler_params=_params(),
    )(x, w, *extra)
    return outs if aug else outs[0]


def _kv_proj(x1, w_kv, bf_row, *, tm):
    S = x1.shape[0]

    def body(x_ref, w_ref, b_ref, pk_ref, pv_ref, k_ref, v_ref, fl_ref, cum_ref, carry_ref):
        i = pl.program_id(0)

        @pl.when(i == 0)
        def _():
            carry_ref[...] = jnp.zeros_like(carry_ref)

        xb = x_ref[...].astype(BF16)
        fl = _dot(xb, w_ref[:, 2 * D_MAIN:2 * D_MAIN + LANES]) + b_ref[...]
        fl_ref[...] = fl
        log_f = jnp.minimum(fl, 0.0) - jnp.log1p(jnp.exp(-jnp.abs(fl)))
        r = lax.broadcasted_iota(jnp.int32, (tm, tm), 0)
        c = lax.broadcasted_iota(jnp.int32, (tm, tm), 1)
        tri = jnp.where(c <= r, 1.0, 0.0).astype(BF16)
        hi, mid, lo = _split3(log_f)
        cum = (_dot(tri, hi) + _dot(tri, mid)) + _dot(tri, lo) + carry_ref[0:1, :]
        cum_ref[...] = cum
        carry_ref[0:1, :] = cum[tm - 1:tm, :]
        lhs_k, lhs_v = _placement_lhs(-cum, tm), _placement_lhs(None, tm)
        for n0 in range(0, D_MAIN, 512):
            _store_augmented(k_ref, _dot(xb, w_ref[:, n0:n0 + 512]), n0 // LANES, lhs_k, pk_ref)
            _store_augmented(v_ref, _dot(xb, w_ref[:, D_MAIN + n0:D_MAIN + n0 + 512]), n0 // LANES, lhs_v, pv_ref)

    return pl.pallas_call(
        body, name="kv_proj", grid=(S // tm,),
        in_specs=[pl.BlockSpec((tm, D_MODEL), lambda i: (i, 0)),
                  _const_spec((D_MODEL, 2 * D_MAIN + LANES)), _const_spec((1, LANES)),
                  _const_spec((LANES, AUG_W)), _const_spec((LANES, AUG_W))],
        out_specs=[pl.BlockSpec((tm, AUG_W), lambda i: (i, 0)),
                   pl.BlockSpec((tm, AUG_W), lambda i: (i, 0)),
                   pl.BlockSpec((tm, LANES), lambda i: (i, 0)),
                   pl.BlockSpec((tm, LANES), lambda i: (i, 0))],
        out_shape=[jax.ShapeDtypeStruct((S, AUG_W), BF16), jax.ShapeDtypeStruct((S, AUG_W), BF16),
                   jax.ShapeDtypeStruct((S, LANES), F32), jax.ShapeDtypeStruct((S, LANES), F32)],
        scratch_shapes=[pltpu.VMEM((8, LANES), F32)],
        compiler_params=_params(),
    )(x1, w_kv, bf_row, _placement(AUG_B, AUG_A), _placement(None, AUG_A))


def _gate_bwd(dq_aug, dck, fl, *, tm):
    S = fl.shape[0]
    n = S // tm

    def body(dq_ref, dck_ref, fl_ref, du_ref, df_ref, db_ref, carry_ref):
        i = pl.program_id(0)

        @pl.when(i == 0)
        def _():
            carry_ref[...] = jnp.zeros_like(carry_ref)
            db_ref[...] = jnp.zeros_like(db_ref)

        lane = lax.broadcasted_iota(jnp.int32, (tm, LANES), 1)
        half0 = lane < FOX_HEAD_DIM
        dcq = jnp.zeros((tm, LANES), F32)
        for hp in range(FOX_HEADS // 2):
            b0 = dq_ref[:, 2 * hp * LANES:(2 * hp + 1) * LANES]
            b1 = dq_ref[:, (2 * hp + 1) * LANES:(2 * hp + 2) * LANES]
            du_ref[:, hp * LANES:(hp + 1) * LANES] = (
                jnp.where(half0, b0, pltpu.roll(b1, FOX_HEAD_DIM, 1)) * FOX_SCALE).astype(BF16)
            r0 = jnp.sum(jnp.where(lane == AUG_A, b0, 0.0), axis=1, keepdims=True)
            r1 = jnp.sum(jnp.where(lane == AUG_A, b1, 0.0), axis=1, keepdims=True)
            dcq = dcq + jnp.where(lane == 2 * hp, r0, 0.0) + jnp.where(lane == 2 * hp + 1, r1, 0.0)
        dcum = dcq - dck_ref[...]
        r = lax.broadcasted_iota(jnp.int32, (tm, tm), 0)
        c = lax.broadcasted_iota(jnp.int32, (tm, tm), 1)
        tri = jnp.where(c >= r, 1.0, 0.0).astype(BF16)
        hi, mid, lo = _split3(dcum)
        rev = (_dot(tri, hi) + _dot(tri, mid)) + _dot(tri, lo) + carry_ref[0:1, :]
        carry_ref[0:1, :] = rev[0:1, :]
        fl_v = fl_ref[...]
        df = rev * (1.0 / (1.0 + jnp.exp(fl_v)))
        df_ref[...] = df
        db_ref[...] += jnp.sum(df, axis=0, keepdims=True)

    return pl.pallas_call(
        body, name="gate_bwd", grid=(n,),
        in_specs=[pl.BlockSpec((tm, AUG_W), lambda i: (n - 1 - i, 0)),
                  pl.BlockSpec((tm, LANES), lambda i: (n - 1 - i, 0)),
                  pl.BlockSpec((tm, LANES), lambda i: (n - 1 - i, 0))],
        out_specs=[pl.BlockSpec((tm, D_MAIN), lambda i: (n - 1 - i, 0)),
                   pl.BlockSpec((tm, LANES), lambda i: (n - 1 - i, 0)),
                   pl.BlockSpec((1, LANES), lambda i: (0, 0))],
        out_shape=[jax.ShapeDtypeStruct((S, D_MAIN), BF16),
                   jax.ShapeDtypeStruct((S, LANES), F32), jax.ShapeDtypeStruct((1, LANES), F32)],
        scratch_shapes=[pltpu.VMEM((8, LANES), F32)],
        compiler_params=_params(),
    )(dq_aug, dck, fl)


def _silu_and_grad(g):
    sg = 1.0 / (1.0 + jnp.exp(-g))
    return g * sg, sg * (1.0 + g * (1.0 - sg))


def _mix(kind, mode, *, h, xres=None, mkv, w_out, ln_g, ln_b=None, pool_w=None, pool_scale=None,
         ymain=None, target=None, z=None, dy=None, gather=(), pair_send=(), tm):
    S = h.shape[0]
    n = S // tm
    pool = kind == "pool"
    bwd = mode == "bwd"
    loss_head = bwd and not pool
    rev = pool and bwd
    mem_scale = MEM_HEAD_DIM ** -0.5

    def t_of(i):
        return (n - 1 - i) if rev else i

    row = lambda i: (t_of(i), 0)
    names, arrays, specs = [], [], []

    def add(name, arr, spec):
        names.append(name)
        arrays.append(arr)
        specs.append(spec)

    add("h", h, pl.BlockSpec((tm, D_IN), row))
    if pool:
        hb = tm // POOL_HALO
        add("halo", h, pl.BlockSpec((POOL_HALO, D_MAIN), lambda i: (jnp.maximum(t_of(i) * hb - 1, 0), 0)))
        add("pool_w", pool_w, _const_spec((4, POOL_GROUP, POOL_GROUP)))
        add("pool_scale", pool_scale, _const_spec((1, D_MAIN)))
    else:
        add("ymain", ymain, pl.BlockSpec((tm, D_MAIN), row))
    add("mkv", mkv, _const_spec((N_MEM, 2 * D_MEM)))
    add("w_out", w_out, _const_spec((D_MIX, D_MODEL)))
    add("ln_g", ln_g, _const_spec((1, D_MODEL)))
    if not (pool and bwd):
        add("xres", xres, pl.BlockSpec((tm, D_MODEL), row))
        add("ln_b", ln_b, _const_spec((1, D_MODEL)))
    if loss_head:
        add("target", target, pl.BlockSpec((tm, D_MODEL), row))
    if pool and bwd:
        add("z", z, pl.BlockSpec((tm, D_MODEL), row))
        add("dy", dy, pl.BlockSpec((tm, D_MODEL), row))
    for a, p in enumerate(gather):
        add(f"gather_src{a}", p, _ANY)
    for a, p in enumerate(pair_send):
        add(f"pair_src{a}", p, _ANY)

    onames, oshapes, ospecs = [], [], []

    def add_out(name, shape, dtype, spec):
        onames.append(name)
        oshapes.append(jax.ShapeDtypeStruct(shape, dtype))
        ospecs.append(spec)

    const2 = lambda i: (0, 0)
    if not bwd:
        add_out("z", (S, D_MODEL), F32, pl.BlockSpec((tm, D_MODEL), row))
        add_out("xout", (S, D_MODEL), F32, pl.BlockSpec((tm, D_MODEL), row))
    else:
        add_out("dz", (S, D_MODEL), F32, pl.BlockSpec((tm, D_MODEL), row))
        if pool:
            add_out("dmain", (S, D_MAIN), BF16, pl.BlockSpec((tm, D_MAIN), row))
        else:
            add_out("dmain", (S, 2 * D_MAIN), BF16, pl.BlockSpec((tm, 2 * D_MAIN), row))
        add_out("drest", (S, D_IN - D_MAIN), BF16, pl.BlockSpec((tm, D_IN - D_MAIN), row))
        add_out("dw_out", (D_MIX, D_MODEL), F32, pl.BlockSpec((D_MIX, D_MODEL), const2))
        add_out("dmkv", (N_MEM, 2 * D_MEM), F32, pl.BlockSpec((N_MEM, 2 * D_MEM), const2))
        add_out("dln_g", (1, D_MODEL), F32, pl.BlockSpec((1, D_MODEL), const2))
        add_out("dln_b", (1, D_MODEL), F32, pl.BlockSpec((1, D_MODEL), const2))
        if pool:
            add_out("dpool_w", (4, POOL_GROUP, POOL_GROUP), F32,
                    pl.BlockSpec((4, POOL_GROUP, POOL_GROUP), lambda i: (0, 0, 0)))
            add_out("dpool_scale", (1, D_MAIN), F32, pl.BlockSpec((1, D_MAIN), const2))
        else:
            add_out("loss", (1, D_MODEL), F32, pl.BlockSpec((1, D_MODEL), const2))
            add_out("drow", (FOX_HEADS, S), F32, pl.BlockSpec((FOX_HEADS, tm), lambda i: (0, i)))

    for a, p in enumerate(gather):
        add_out(f"gathered{a}", (N_CHIPS,) + p.shape, p.dtype, _ANY)
    for a, p in enumerate(pair_send):
        add_out(f"from_sibling{a}", (N_CHIPS,) + p.shape[2:], p.dtype, _ANY)

    scratch = [pltpu.VMEM((tm, D_MIX), BF16),
               pltpu.VMEM((tm, D_MEM), F32)]
    if pool:
        scratch.append(pltpu.VMEM((tm + 2 * POOL_HALO, D_MAIN), F32))
    if rev:
        scratch.append(pltpu.VMEM((POOL_HALO, D_MAIN), F32))
    assert not (gather and pair_send)
    if gather:
        scratch += _gather_sems(len(gather))
    if pair_send:
        scratch += _pair_sems(len(pair_send))
    n_in, n_out = len(names), len(onames)

    def body(*refs):
        R = dict(zip(names, refs[:n_in]))
        O = dict(zip(onames, refs[n_in:n_in + n_out]))
        sc = refs[n_in + n_out:]
        yc_ref, ymem_ref = sc[0], sc[1]
        ext_ref = sc[2] if pool else None
        carry_ref = sc[3] if rev else None
        i = pl.program_id(0)
        t = t_of(i)
        h_ref = R["h"]
        gamma = R["ln_g"][...]

        if gather:
            start, forward, finish = _gather_steps(
                [R[f"gather_src{a}"] for a in range(len(gather))],
                [O[f"gathered{a}"] for a in range(len(gather))], sc[-2], sc[-1])
            pl.when(i == 0)(start)
            pl.when(i == n - 3)(forward)
            pl.when(i == n - 1)(finish)
        if pair_send:
            start, finish = _pair_send_steps(
                [R[f"pair_src{a}"] for a in range(len(pair_send))],
                [O[f"from_sibling{a}"] for a in range(len(pair_send))], sc[-2], sc[-1])
            pl.when(i == 0)(start)
            pl.when(i == n - 1)(finish)

        if bwd:
            @pl.when(i == 0)
            def _():
                for nm in ("dw_out", "dmkv", "dln_g", "dln_b", "dpool_w", "dpool_scale", "loss"):
                    if nm in O:
                        O[nm][...] = jnp.zeros_like(O[nm])
                if rev:
                    carry_ref[...] = jnp.zeros_like(carry_ref)

        if pool:
            u = h_ref[:, 0:D_MAIN].astype(F32)
            halo = R["halo"][...].astype(F32)
            ext_ref[0:POOL_HALO, :] = jnp.where(t > 0, halo, 0.0)
            ext_ref[POOL_HALO:POOL_HALO + tm, :] = u
            tpos = t * tm + lax.broadcasted_iota(jnp.int32, (tm, 1), 0)
            pms, invcs = [], []
            for gi, w in enumerate(POOL_WINDOWS):
                cs = slice(gi * POOL_GROUP, (gi + 1) * POOL_GROUP)
                acc = ext_ref[POOL_HALO:POOL_HALO + tm, cs]
                for k in range(1, w):
                    acc = acc + ext_ref[POOL_HALO - k:POOL_HALO - k + tm, cs]
                invc = 1.0 / jnp.minimum(tpos + 1, w).astype(F32)
                pm = (acc * invc - u[:, cs]).astype(BF16)
                pms.append(pm)
                invcs.append(invc)
            mixed = [_dot(pms[gi], R["pool_w"][gi]) for gi in range(4)]
            ps = R["pool_scale"][...]
            y_main = [mixed[gi] * ps[:, gi * POOL_GROUP:(gi + 1) * POOL_GROUP] for gi in range(4)]
        else:
            y_main = [R["ymain"][:, gi * 256:(gi + 1) * 256].astype(F32) for gi in range(4)]

        probs = []
        for hd in range(MEM_HEADS):
            sl = slice(D_MAIN + hd * MEM_HEAD_DIM, D_MAIN + (hd + 1) * MEM_HEAD_DIM)
            ksl = slice(hd * MEM_HEAD_DIM, (hd + 1) * MEM_HEAD_DIM)
            vsl = slice(D_MEM + hd * MEM_HEAD_DIM, D_MEM + (hd + 1) * MEM_HEAD_DIM)
            s = _dot_nt(h_ref[:, sl], R["mkv"][:, ksl]) * mem_scale
            e = jnp.exp(s - jnp.max(s, axis=1, keepdims=True))
            p = e / jnp.sum(e, axis=1, keepdims=True)
            probs.append(p)
            ymem_ref[:, ksl] = _dot(p.astype(BF16), R["mkv"][:, vsl])

        g_off = D_MIX
        gate_d = []
        for gi in range(4):
            cs = slice(gi * 256, (gi + 1) * 256)
            gm = h_ref[:, g_off + gi * 256:g_off + (gi + 1) * 256].astype(F32)
            sv, sd = _silu_and_grad(gm)
            yc_ref[:, cs] = (y_main[gi] * sv).astype(BF16)
            gate_d.append((sv, sd))
        gq = h_ref[:, g_off + D_MAIN:D_IN].astype(F32)
        svq, sdq = _silu_and_grad(gq)
        yc_ref[:, D_MAIN:D_MIX] = (ymem_ref[...] * svq).astype(BF16)

        if pool and bwd:
            zt = R["z"][...]
        else:
            o = _dot(yc_ref[...], R["w_out"][...])
            zt = ALPHA * R["xres"][...] + o
        mu = jnp.mean(zt, axis=1, keepdims=True)
        zc = zt - mu
        var = jnp.mean(zc * zc, axis=1, keepdims=True)
        rstd = lax.rsqrt(var + LN_EPS)
        xhat = zc * rstd
        if not bwd:
            O["z"][...] = zt
            O["xout"][...] = xhat * gamma + R["ln_b"][...]
            return

        if loss_head:
            xo = xhat * gamma + R["ln_b"][...]
            err = xo - R["target"][...]
            O["loss"][...] += jnp.sum(err * err, axis=0, keepdims=True)
            dyt = err * (1.0 / D_MODEL)
        else:
            dyt = R["dy"][...]

        O["dln_g"][...] += jnp.sum(dyt * xhat, axis=0, keepdims=True)
        O["dln_b"][...] += jnp.sum(dyt, axis=0, keepdims=True)
        gdy = dyt * gamma
        m1 = jnp.mean(gdy, axis=1, keepdims=True)
        m2 = jnp.mean(gdy * xhat, axis=1, keepdims=True)
        dz = rstd * (gdy - m1 - xhat * m2)
        O["dz"][...] = dz
        dzb = dz.astype(BF16)

        for n0 in range(0, D_MIX, 512):
            O["dw_out"][n0:n0 + 512, :] += _dot_tn(yc_ref[:, n0:n0 + 512], dzb)
        dyc_mem = _dot_nt(dzb, R["w_out"][D_MAIN:D_MIX, :])

        O["drest"][:, D_MEM + D_MAIN:D_MEM + D_MAIN + D_MEM] = (dyc_mem * ymem_ref[...] * sdq).astype(BF16)
        dymem = dyc_mem * svq
        for hd in range(MEM_HEADS):
            sl = slice(D_MAIN + hd * MEM_HEAD_DIM, D_MAIN + (hd + 1) * MEM_HEAD_DIM)
            ksl = slice(hd * MEM_HEAD_DIM, (hd + 1) * MEM_HEAD_DIM)
            vsl = slice(D_MEM + hd * MEM_HEAD_DIM, D_MEM + (hd + 1) * MEM_HEAD_DIM)
            p = probs[hd]
            dyb = dymem[:, ksl].astype(BF16)
            dp = _dot_nt(dyb, R["mkv"][:, vsl])
            ds = p * (dp - jnp.sum(dp * p, axis=1, keepdims=True)) * mem_scale
            dsb = ds.astype(BF16)
            O["drest"][:, ksl] = _dot(dsb, R["mkv"][:, ksl]).astype(BF16)
            O["dmkv"][:, ksl] += _dot_tn(dsb, h_ref[:, sl])
            O["dmkv"][:, vsl] += _dot_tn(p.astype(BF16), dyb)

        dmain = []
        for gi in range(4):
            cs = slice(gi * 256, (gi + 1) * 256)
            dyc_g = _dot_nt(dzb, R["w_out"][cs, :])
            sv, sd = gate_d[gi]
            O["drest"][:, D_MEM + gi * 256:D_MEM + (gi + 1) * 256] = (dyc_g * y_main[gi] * sd).astype(BF16)
            dmain.append(dyc_g * sv)

        if not pool:
            prod = []
            lane2 = lax.broadcasted_iota(jnp.int32, (tm, LANES), 1)
            for gi in range(4):
                cs = slice(gi * 256, (gi + 1) * 256)
                db16 = dmain[gi].astype(BF16)
                dbf = db16.astype(F32)
                prod.append(dbf * R["ymain"][:, cs].astype(F32))
                for pr in range(2):
                    blk = dbf[:, pr * LANES:(pr + 1) * LANES]
                    base = (4 * gi + 2 * pr) * LANES
                    O["dmain"][:, base:base + LANES] = jnp.where(lane2 < FOX_HEAD_DIM, blk, 0.0).astype(BF16)
                    O["dmain"][:, base + LANES:base + 2 * LANES] = jnp.where(
                        lane2 < FOX_HEAD_DIM, pltpu.roll(blk, FOX_HEAD_DIM, 1), 0.0).astype(BF16)
            dcol = jnp.zeros((tm, LANES), F32)
            for gi in range(4):
                dr = lax.broadcasted_iota(jnp.int32, (256, LANES), 0)
                hc = lax.broadcasted_iota(jnp.int32, (256, LANES), 1)
                sel = jnp.where(jnp.right_shift(dr, 6) + gi * 4 == hc, 1.0, 0.0).astype(BF16)
                hi, mid, lo = _split3(prod[gi])
                dcol = dcol + ((_dot(hi, sel) + _dot(mid, sel)) + _dot(lo, sel))
            O["drow"][...] = dcol.T[0:FOX_HEADS, :]
            return

        ps = R["pool_scale"][...]
        dpm_list = []
        for gi in range(4):
            cs = slice(gi * 256, (gi + 1) * 256)
            O["dpool_scale"][:, cs] += jnp.sum(dmain[gi] * mixed[gi], axis=0, keepdims=True)
            dmix = (dmain[gi] * ps[:, cs]).astype(BF16)
            O["dpool_w"][gi] += _dot_tn(pms[gi], dmix)
            dpm = _dot_nt(dmix, R["pool_w"][gi])
            dpm_list.append(dpm)
            ext_ref[0:tm, cs] = dpm * invcs[gi]
        ext_ref[tm:tm + POOL_HALO, :] = carry_ref[...]
        carry_ref[...] = ext_ref[0:POOL_HALO, :]
        for gi, w in enumerate(POOL_WINDOWS):
            cs = slice(gi * 256, (gi + 1) * 256)
            acc = ext_ref[0:tm, cs]
            for k in range(1, w):
                acc = acc + ext_ref[k:k + tm, cs]
            O["dmain"][:, cs] = (acc - dpm_list[gi]).astype(BF16)

    outs = pl.pallas_call(
        body, name=f"mix_{kind}_{mode}", grid=(n,),
        in_specs=specs, out_specs=ospecs, out_shape=oshapes,
        scratch_shapes=scratch, compiler_params=_params(),
    )(*arrays)
    return dict(zip(onames, outs))


def _lin_bwd(xin, dhs, w, res, *, tm, name, scatter=()):
    S, K = xin.shape
    N, nc_w = _cols_and_chunk(w)
    nj, nr, ns = len(dhs), len(res), len(scatter)
    n_tiles = S // tm
    widths = [dh.shape[1] for dh in dhs]
    chunks = [nc_w if w.ndim == 3 else (512 if wd % 512 == 0 else wd) for wd in widths]
    assert sum(widths) == N and all(wd % c == 0 for wd, c in zip(widths, chunks))
    scales = [s for _, s in res]
    n_in = 2 + nj + nr + ns

    def body(*refs):
        x_ref = refs[0]
        dh_refs = refs[1:1 + nj]
        w_ref = refs[1 + nj]
        r_refs = refs[2 + nj:2 + nj + nr]
        dx_ref, dw_ref = refs[n_in], refs[n_in + 1]
        i = pl.program_id(0)

        if ns:
            start, finish = _scatter_steps(refs[n_in - ns:n_in], refs[n_in + 2:n_in + 2 + ns],
                                           refs[n_in + 2 + ns], refs[n_in + 3 + ns])
            pl.when(i == 0)(start)
            pl.when(i == n_tiles - 1)(finish)

        @pl.when(i == 0)
        def _():
            dw_ref[...] = jnp.zeros_like(dw_ref)

        xb = x_ref[...].astype(BF16)
        dx = jnp.zeros((tm, K), F32)
        for r_ref, sc in zip(r_refs, scales):
            dx = dx + sc * r_ref[...]
        off = 0
        for j in range(nj):
            nc = chunks[j]
            for n0 in range(0, widths[j], nc):
                dhb = dh_refs[j][:, n0:n0 + nc].astype(BF16)
                dx = dx + _dot_nt(dhb, _w_cols(w_ref, off + n0, nc)[...])
                _w_cols(dw_ref, off + n0, nc)[...] += _dot_tn(xb, dhb)
            off += widths[j]
        dx_ref[...] = dx

    zeros = (0,) * w.ndim
    in_specs = [pl.BlockSpec((tm, K), lambda i: (i, 0))]
    in_specs += [pl.BlockSpec((tm, n), lambda i: (i, 0)) for n in widths]
    in_specs += [_const_spec(w.shape)]
    in_specs += [pl.BlockSpec((tm, K), lambda i: (i, 0)) for _ in res]
    in_specs += [_ANY] * ns
    out_specs = [pl.BlockSpec((tm, K), lambda i: (i, 0)), pl.BlockSpec(w.shape, lambda i: zeros)] + [_ANY] * ns
    out_shape = [jax.ShapeDtypeStruct((S, K), F32), jax.ShapeDtypeStruct(w.shape, F32)]
    out_shape += [jax.ShapeDtypeStruct(p.shape, p.dtype) for p in scatter]
    return pl.pallas_call(
        body, name=name, grid=(n_tiles,),
        in_specs=in_specs, out_specs=out_specs, out_shape=out_shape,
        scratch_shapes=_scatter_sems(ns) if ns else [],
        compiler_params=_params(),
    )(xin, *dhs, w, *[r for r, _ in res], *scatter)


def _wgrad(xin, dh, *, name):
    M, K = xin.shape
    N = dh.shape[1]

    def body(x_ref, dh_ref, o_ref):
        o_ref[...] = _dot_tn(x_ref[...].astype(BF16), dh_ref[...].astype(BF16))

    return pl.pallas_call(
        body, name=name, out_shape=jax.ShapeDtypeStruct((K, N), F32),
        compiler_params=pltpu.CompilerParams(vmem_limit_bytes=VMEM_LIMIT),
    )(xin, dh)


AUG_A = FOX_HEAD_DIM
AUG_B = FOX_HEAD_DIM + 3
AUG_W = FOX_HEADS * LANES


def _placement(val_lane, ones_lane):
    r = jnp.arange(LANES)[:, None]
    c = jnp.arange(AUG_W)[None, :]
    head, lane = c // LANES, c % LANES
    m = jnp.zeros((LANES, AUG_W), jnp.bool_)
    if val_lane is not None:
        for part in range(3):
            m = m | ((r == part * FOX_HEADS + head) & (lane == val_lane + part))
    if ones_lane is not None:
        m = m | ((r == 3 * FOX_HEADS) & (lane >= ones_lane) & (lane < ones_lane + 3))
    return m.astype(BF16)


def _placement_lhs(val, tm):
    lane = lax.broadcasted_iota(jnp.int32, (tm, LANES), 1)
    lhs = jnp.where(lane == 3 * FOX_HEADS, 1.0, 0.0)
    if val is not None:
        hi, mid, lo = [p.astype(F32) for p in _split3(val)]
        lhs = jnp.where(lane < FOX_HEADS, hi, jnp.where(
            lane < 2 * FOX_HEADS, pltpu.roll(mid, FOX_HEADS, 1), jnp.where(
                lane < 3 * FOX_HEADS, pltpu.roll(lo, 2 * FOX_HEADS, 1), lhs)))
    return lhs.astype(BF16)


def _store_augmented(o_ref, data, first_pair, lhs, p_ref):
    tm = data.shape[0]
    is_data = lax.broadcasted_iota(jnp.int32, (tm, LANES), 1) < FOX_HEAD_DIM
    for j in range(data.shape[1] // LANES):
        base = 2 * (first_pair + j) * LANES
        extra = _dot(lhs, p_ref[:, base:base + 2 * LANES])
        blk = data[:, j * LANES:(j + 1) * LANES]
        o_ref[:, base:base + LANES] = jnp.where(is_data, blk, extra[:, 0:LANES]).astype(BF16)
        o_ref[:, base + LANES:base + 2 * LANES] = jnp.where(
            is_data, pltpu.roll(blk, FOX_HEAD_DIM, 1), extra[:, LANES:2 * LANES]).astype(BF16)


def _cols_of_rows(rows, S):
    nh = FOX_HEADS // rows.shape[0]
    a = rows[:, :, 0:nh, :].transpose(0, 2, 1, 3).reshape(FOX_HEADS, S).T
    return jnp.pad(a, ((0, 0), (0, LANES - FOX_HEADS)))


def _fox_fwd(qf, ka, va, *, tq, nh=4):
    S = ka.shape[0]
    nq = S // tq
    tk = tq
    ng = FOX_HEADS // nh

    def body(q_ref, k_ref, v_ref, o_ref, lse_ref, *scratch):
        p_scr, m_scr, acc_scr = scratch[0:nh], scratch[nh:2 * nh], scratch[2 * nh:3 * nh]
        qi = pl.program_id(1)
        lane = lax.broadcasted_iota(jnp.int32, (tq, LANES), 1)
        half0 = lane < FOX_HEAD_DIM
        rr = lax.broadcasted_iota(jnp.int32, (tq, tk), 0)
        cc = lax.broadcasted_iota(jnp.int32, (tq, tk), 1)
        sls = [slice(hh * LANES, (hh + 1) * LANES) for hh in range(nh)]
        qs = [q_ref[:, sl] for sl in sls]

        for hh in range(nh):
            p_scr[hh][...] = jnp.zeros_like(p_scr[hh])
            m_scr[hh][...] = jnp.full(m_scr[hh].shape, -jnp.inf, F32)
            acc_scr[hh][...] = jnp.zeros_like(acc_scr[hh])

        def chunk(ki, masked):
            k0 = pl.multiple_of(ki * tk, tk)
            kp = pl.multiple_of(jnp.maximum(ki - 1, 0) * tk, tk)
            for hh in range(nh):
                m = m_scr[hh][...]
                s = _dot_nt(qs[hh], k_ref[pl.ds(k0, tk), sls[hh]])
                pv = _dot(p_scr[hh][...], v_ref[pl.ds(kp, tk), sls[hh]])
                if masked:
                    s = jnp.where(cc <= rr, s, -jnp.inf)
                m_new = jnp.maximum(m, jnp.max(s, axis=1, keepdims=True))
                p_scr[hh][...] = jnp.exp(s - jnp.tile(m_new, (1, tk // LANES))).astype(BF16)
                acc_scr[hh][...] = (acc_scr[hh][...] + pv) * jnp.exp(m - m_new)
                m_scr[hh][...] = m_new

        def trip(ki, c):
            chunk(ki, False)
            return c

        lax.fori_loop(0, qi, trip, 0)
        chunk(qi, True)
        kq = pl.multiple_of(qi * tk, tk)
        outs = []
        lse_cols = jnp.zeros((tq, LANES), F32)
        for hh in range(nh):
            m = m_scr[hh][...]
            acc = acc_scr[hh][...] + _dot(p_scr[hh][...], v_ref[pl.ds(kq, tk), sls[hh]])
            l = jnp.sum(jnp.where(lane == AUG_A, acc, 0.0), axis=1, keepdims=True)
            outs.append(acc / l)
            lse_cols = lse_cols + jnp.where(lane == hh, m + jnp.log(l), 0.0)
        for pr in range(nh // 2):
            o_ref[:, pr * LANES:(pr + 1) * LANES] = jnp.where(
                half0, outs[2 * pr], pltpu.roll(outs[2 * pr + 1], FOX_HEAD_DIM, 1))
        lse_ref[0, 0] = lse_cols.T[0:8, :]

    return pl.pallas_call(
        body, name="fox_fwd", grid=(ng, nq),
        in_specs=[pl.BlockSpec((tq, nh * LANES), lambda g, qi: (qi, g)),
                  pl.BlockSpec((S, nh * LANES), lambda g, qi: (0, g), pipeline_mode=pl.Buffered(1)),
                  pl.BlockSpec((S, nh * LANES), lambda g, qi: (0, g), pipeline_mode=pl.Buffered(1))],
        out_specs=[pl.BlockSpec((tq, nh * FOX_HEAD_DIM), lambda g, qi: (qi, g)),
                   pl.BlockSpec((1, 1, 8, tq), lambda g, qi: (g, qi, 0, 0))],
        out_shape=[jax.ShapeDtypeStruct((S, D_MAIN), F32),
                   jax.ShapeDtypeStruct((ng, nq, 8, tq), F32)],
        scratch_shapes=([pltpu.VMEM((tq, tk), BF16)] * nh + [pltpu.VMEM((tq, LANES), F32)] * nh
                        + [pltpu.VMEM((tq, LANES), F32)] * nh),
        compiler_params=_params(2),
    )(qf, ka, va)


def _rows_for_pairs(a16, nt, tt):
    a = a16.reshape(8, 2, nt, tt).transpose(0, 2, 1, 3)
    return jnp.pad(a, ((0, 0), (0, 0), (0, 6), (0, 0)))


def _fox_bwd(qf, ka, va, do_aug, lse_rows, d_rows, *, tq):
    S = ka.shape[0]
    nq = S // tq
    tk = tq

    def body(k_ref, v_ref, q_ref, do_ref, lser_ref, dr_ref, dq_ref, dk_ref, dv_ref, dck_ref, dk_scr, dv_scr):
        kj = pl.program_id(1)

        @pl.when(kj == 0)
        def _():
            dq_ref[...] = jnp.zeros_like(dq_ref)

        lane = lax.broadcasted_iota(jnp.int32, (tk, LANES), 1)
        half0 = lane < FOX_HEAD_DIM
        rr = lax.broadcasted_iota(jnp.int32, (tk, tq), 0)
        cc = lax.broadcasted_iota(jnp.int32, (tk, tq), 1)
        sls = [slice(hh * LANES, (hh + 1) * LANES) for hh in range(2)]
        kts = [k_ref[:, sl] for sl in sls]
        vts = [v_ref[:, sl] for sl in sls]

        dk_scr[...] = jnp.zeros_like(dk_scr)
        dv_scr[...] = jnp.zeros_like(dv_scr)

        def chunk(qi, masked):
            q0 = pl.multiple_of(qi * tq, tq)
            for hh in range(2):
                qc = q_ref[pl.ds(q0, tq), sls[hh]]
                doc = do_ref[pl.ds(q0, tq), sls[hh]]
                lse = lser_ref[0, qi][hh:hh + 1, :]
                dd = dr_ref[0, qi][hh:hh + 1, :]
                pt = jnp.exp(_dot_nt(kts[hh], qc) - lse)
                if masked:
                    pt = jnp.where(rr <= cc, pt, 0.0)
                dsb = (pt * (_dot_nt(vts[hh], doc) - dd)).astype(BF16)
                dv_scr[hh] += _dot(pt.astype(BF16), doc)
                dk_scr[hh] += _dot(dsb, qc)
                dq_ref[pl.ds(q0, tq), sls[hh]] += _dot_tn(dsb, kts[hh])

        def trip(qi, c):
            chunk(qi, False)
            return c

        chunk(kj, True)
        lax.fori_loop(kj + 1, nq, trip, 0)
        dk0, dk1 = dk_scr[0], dk_scr[1]
        dv0, dv1 = dv_scr[0], dv_scr[1]
        dk_ref[...] = jnp.where(half0, dk0, pltpu.roll(dk1, FOX_HEAD_DIM, 1)).astype(BF16)
        dv_ref[...] = jnp.where(half0, dv0, pltpu.roll(dv1, FOX_HEAD_DIM, 1)).astype(BF16)
        c0 = jnp.sum(jnp.where(lane == AUG_B, dk0, 0.0), axis=1, keepdims=True)
        c1 = jnp.sum(jnp.where(lane == AUG_B, dk1, 0.0), axis=1, keepdims=True)
        dck_cols = jnp.where(lane == 0, c0, 0.0) + jnp.where(lane == 1, c1, 0.0)
        dck_ref[0, 0] = dck_cols.T[0:8, :]

    return pl.pallas_call(
        body, name="fox_bwd", grid=(8, nq),
        in_specs=[pl.BlockSpec((tk, 2 * LANES), lambda hp, kj: (kj, hp)),
                  pl.BlockSpec((tk, 2 * LANES), lambda hp, kj: (kj, hp)),
                  pl.BlockSpec((S, 2 * LANES), lambda hp, kj: (0, hp)),
                  pl.BlockSpec((S, 2 * LANES), lambda hp, kj: (0, hp)),
                  pl.BlockSpec((1, nq, 8, tq), lambda hp, kj: (hp, 0, 0, 0)),
                  pl.BlockSpec((1, nq, 8, tq), lambda hp, kj: (hp, 0, 0, 0))],
        out_specs=[pl.BlockSpec((S, 2 * LANES), lambda hp, kj: (0, hp)),
                   pl.BlockSpec((tk, LANES), lambda hp, kj: (kj, hp)),
                   pl.BlockSpec((tk, LANES), lambda hp, kj: (kj, hp)),
                   pl.BlockSpec((1, 1, 8, tk), lambda hp, kj: (hp, kj, 0, 0))],
        out_shape=[jax.ShapeDtypeStruct((S, AUG_W), F32),
                   jax.ShapeDtypeStruct((S, D_MAIN), BF16),
                   jax.ShapeDtypeStruct((S, D_MAIN), BF16),
                   jax.ShapeDtypeStruct((8, nq, 8, tk), F32)],
        scratch_shapes=[pltpu.VMEM((2, tk, LANES), F32), pltpu.VMEM((2, tk, LANES), F32)],
        compiler_params=_params(2),
    )(ka, va, qf, do_aug, lse_rows, d_rows)


def _adamw(w, g, m, v, *, name):
    Rr, C = w.shape
    tr = 256 if Rr % 256 == 0 else Rr
    c1 = 1.0 / (1.0 - ADAM_B1 ** ADAM_STEP)
    c2 = 1.0 / (1.0 - ADAM_B2 ** ADAM_STEP)

    def body(w_ref, g_ref, m_ref, v_ref, d_ref, nm_ref, nv_ref):
        gv = g_ref[...]
        nm = ADAM_B1 * m_ref[...] + (1.0 - ADAM_B1) * gv
        nv = ADAM_B2 * v_ref[...] + (1.0 - ADAM_B2) * (gv * gv)
        d_ref[...] = -ADAM_LR * ((nm * c1) / (jnp.sqrt(nv * c2) + ADAM_EPS) + ADAM_WD * w_ref[...])
        nm_ref[...] = nm
        nv_ref[...] = nv

    spec = pl.BlockSpec((tr, C), lambda i: (i, 0))
    sds = jax.ShapeDtypeStruct((Rr, C), F32)
    return pl.pallas_call(
        body, name=name, grid=(Rr // tr,),
        in_specs=[spec] * 4, out_specs=[spec] * 3, out_shape=[sds] * 3,
        compiler_params=_params(),
    )(w, g, m, v)


_ANY = pl.BlockSpec(memory_space=pl.ANY)
_MESH = pl.DeviceIdType.MESH


def _place():
    x, y, c = lax.axis_index("x"), lax.axis_index("y"), lax.axis_index("c")
    return x, y, c


def _gather_steps(p_refs, out_refs, send_sems, recv_sems):
    x, y, c = _place()
    sib = (x, y, 1 - c)
    chips = [(1 - x, y), (x, 1 - y), (1 - x, 1 - y)]
    idx = [2 * chip[0] + chip[1] for chip in chips]
    me = 2 * x + y
    na = len(p_refs)

    def copy(a, k, chip_idx, half, to, src=None):
        dst = out_refs[a].at[chip_idx, half]
        return pltpu.make_async_remote_copy(
            src_ref=dst if src is None else src, dst_ref=dst,
            send_sem=send_sems.at[6 * a + k], recv_sem=recv_sems.at[6 * a + k],
            device_id=to, device_id_type=_MESH)

    first = [copy(a, j, me, c, (*chips[j], c), src=p_refs[a].at[c]) for a in range(na) for j in range(3)]
    passed = [copy(a, 3 + j, idx[j], c, sib) for a in range(na) for j in range(3)]

    def start():
        for cp in first:
            cp.start()

    def forward():
        for a in range(na):
            for j in range(3):
                copy(a, j, idx[j], c, sib).wait_recv()
                passed[3 * a + j].start()

    def finish():
        for a in range(na):
            for j in range(3):
                copy(a, 3 + j, idx[j], 1 - c, sib).wait_recv()
        for cp in first + passed:
            cp.wait_send()

    return start, forward, finish


def _sems(n):
    return [pltpu.SemaphoreType.DMA((n,)), pltpu.SemaphoreType.DMA((n,))]


def _gather_sems(na):
    return _sems(6 * na)


def _scatter_sems(na):
    return _sems(3 * na)


def _pair_sems(na):
    return _sems(N_CHIPS * na)


def _gathered_shape(pack):
    return jax.ShapeDtypeStruct((N_CHIPS,) + pack.shape, pack.dtype)


def _from_sibling_shape(gpack):
    return jax.ShapeDtypeStruct((N_CHIPS,) + gpack.shape[2:], gpack.dtype)


def _all_gather_shards(packs):
    na = len(packs)

    def body(*refs):
        for step in _gather_steps(refs[0:na], refs[na:2 * na], refs[2 * na], refs[2 * na + 1]):
            step()

    return pl.pallas_call(
        body, name="all_gather_shards",
        in_specs=[_ANY] * na, out_specs=[_ANY] * na, out_shape=[_gathered_shape(p) for p in packs],
        scratch_shapes=_gather_sems(na),
    )(*packs)


def _pair_send_steps(g_refs, out_refs, send_sem, recv_sem):
    x, y, c = _place()
    cps = [pltpu.make_async_remote_copy(
        src_ref=g_refs[a].at[j, 1 - c], dst_ref=out_refs[a].at[j],
        send_sem=send_sem.at[N_CHIPS * a + j], recv_sem=recv_sem.at[N_CHIPS * a + j],
        device_id=(x, y, 1 - c), device_id_type=_MESH) for a in range(len(g_refs)) for j in range(N_CHIPS)]

    def start():
        for cp in cps:
            cp.start()

    def finish():
        for cp in cps:
            cp.wait_recv()
        for cp in cps:
            cp.wait_send()

    return start, finish


def _send_half_to_sibling(gpacks, tag):
    na = len(gpacks)

    def body(*refs):
        for step in _pair_send_steps(refs[0:na], refs[na:2 * na], refs[2 * na], refs[2 * na + 1]):
            step()

    return pl.pallas_call(
        body, name=f"pair_send{tag}",
        in_specs=[_ANY] * na, out_specs=[_ANY] * na, out_shape=[_from_sibling_shape(g) for g in gpacks],
        scratch_shapes=_pair_sems(na),
    )(*gpacks)


def _pair_sum(gpack, recv, c_arr, tag, *, tr=PACK_TILE):
    rows, lanes = recv.shape[1:]
    assert rows % tr == 0

    def body(c_ref, a_ref, b_ref, o_ref):
        o_ref[...] = (a_ref[...] + b_ref[...]).astype(BF16)

    grid_spec = pltpu.PrefetchScalarGridSpec(
        num_scalar_prefetch=1, grid=(N_CHIPS, rows // tr),
        in_specs=[pl.BlockSpec((None, None, tr, lanes), lambda j, i, c_ref: (j, c_ref[0], i, 0)),
                  pl.BlockSpec((None, tr, lanes), lambda j, i, c_ref: (j, i, 0))],
        out_specs=pl.BlockSpec((None, tr, lanes), lambda j, i, c_ref: (j, i, 0)))
    return pl.pallas_call(
        body, name=f"pair_sum{tag}", grid_spec=grid_spec,
        out_shape=jax.ShapeDtypeStruct((N_CHIPS, rows, lanes), BF16),
        compiler_params=_params(2),
    )(c_arr, gpack, recv)


def _scatter_steps(p_refs, out_refs, send_sems, recv_sems):
    x, y, c = _place()
    chips = [(1 - x, y), (x, 1 - y), (1 - x, 1 - y)]
    me = 2 * x + y
    cps = [pltpu.make_async_remote_copy(
        src_ref=p_refs[a].at[2 * chip[0] + chip[1]], dst_ref=out_refs[a].at[me],
        send_sem=send_sems.at[3 * a + j], recv_sem=recv_sems.at[3 * a + j],
        device_id=(*chip, c), device_id_type=_MESH) for a in range(len(p_refs)) for j, chip in enumerate(chips)]

    def start():
        for cp in cps:
            cp.start()

    def finish():
        for cp in cps:
            cp.wait_recv()
        for cp in cps:
            cp.wait_send()

    return start, finish


def _scatter_pieces(psums, tag):
    na = len(psums)

    def body(*refs):
        for step in _scatter_steps(refs[0:na], refs[na:2 * na], refs[2 * na], refs[2 * na + 1]):
            step()

    return pl.pallas_call(
        body, name=f"scatter_pieces{tag}",
        in_specs=[_ANY] * na, out_specs=[_ANY] * na,
        out_shape=[jax.ShapeDtypeStruct(p.shape, p.dtype) for p in psums],
        scratch_shapes=_scatter_sems(na),
    )(*psums)


def _sum_pieces(pieces, tag, *, tr=PACK_TILE):
    rows, lanes = pieces.shape[1:]
    assert rows % tr == 0

    def body(p_ref, o_ref):
        acc = p_ref[0].astype(F32) + p_ref[1].astype(F32)
        acc = acc + p_ref[2].astype(F32)
        o_ref[...] = acc + p_ref[3].astype(F32)

    return pl.pallas_call(
        body, name=f"sum_pieces{tag}", grid=(rows // tr,),
        in_specs=[pl.BlockSpec((N_CHIPS, tr, lanes), lambda i: (0, i, 0))],
        out_specs=pl.BlockSpec((tr, lanes), lambda i: (i, 0)),
        out_shape=jax.ShapeDtypeStruct((rows, lanes), F32),
        compiler_params=_params(),
    )(pieces)


def _exchange_halves(totals):
    n = len(totals)

    def body(*refs):
        t_refs, out_refs, send_sem, recv_sem = refs[:n], refs[n:2 * n], refs[2 * n], refs[2 * n + 1]
        x, y, c = _place()
        cps = [pltpu.make_async_remote_copy(
            src_ref=t_refs[i], dst_ref=out_refs[i].at[c], send_sem=send_sem.at[i], recv_sem=recv_sem.at[i],
            device_id=(x, y, 1 - c), device_id_type=_MESH) for i in range(n)]
        for cp in cps:
            cp.start()
        for cp in cps:
            cp.wait_recv()
        for cp in cps:
            cp.wait_send()

    return pl.pallas_call(
        body, name="exchange_halves",
        in_specs=[_ANY] * n, out_specs=[_ANY] * n,
        out_shape=[jax.ShapeDtypeStruct((2,) + t.shape, F32) for t in totals],
        scratch_shapes=[pltpu.SemaphoreType.DMA((n,)), pltpu.SemaphoreType.DMA((n,))],
    )(*totals)


def _pad_rows(a, rows):
    return jnp.pad(a, ((0, rows - a.shape[0]), (0, 0)))


def _pack_weight_shards(w_in, w_mem_kv, w_out, pool_w, w_kv_shared, pool_scale):
    ps_bits = lax.bitcast_convert_type(pool_scale.reshape(-1), BF16).reshape(1, -1)
    ps_row = jnp.pad(ps_bits, ((0, 0), (0, 1024 - ps_bits.shape[1])))

    def common(l):
        return [w_mem_kv[l].astype(BF16).reshape(ROWS_W_MKV, 1024),
                w_out[l].astype(BF16).reshape(ROWS_W_OUT, 1024)]

    p0 = common(0) + [pool_w.astype(BF16).reshape(ROWS_POOL_W, 1024), _pad_rows(ps_row, ROWS_SMALL),
                      jnp.zeros((PACK0_ROWS - OFF_LN_G, 1024), BF16)]
    p1 = common(1) + [_pad_rows(w_kv_shared.astype(BF16).reshape(KV_SHARD, 1024), ROWS_W_KV),
                      jnp.zeros((PACK1_ROWS - OFF_BF, 1024), BF16)]
    w_in_halves = w_in.astype(BF16).reshape(2, 2, D_MODEL // 2, W_IN_SHARD)
    return ([jnp.concatenate(p0, axis=0).reshape(2, PACK0_ROWS // 2, 1024), w_in_halves[0]],
            [jnp.concatenate(p1, axis=0).reshape(2, PACK1_ROWS // 2, 1024), w_in_halves[1]])


def _unpack_common(g, g_in):
    w_mkv = g[:, OFF_W_MKV:OFF_W_MKV + ROWS_W_MKV].reshape(D_MODEL, 2 * D_MEM)
    w_out = g[:, OFF_W_OUT:OFF_W_OUT + ROWS_W_OUT].reshape(D_MIX, D_MODEL)
    return g_in.reshape(N_CHIPS, D_MODEL, W_IN_SHARD), w_mkv, w_out


def _unpack_weights0(g, g_in):
    pool_w = g[:, OFF_POOL_W:OFF_POOL_W + ROWS_POOL_W].reshape(4, 4, POOL_GROUP // 4, POOL_GROUP)
    pool_w = pool_w.transpose(1, 0, 2, 3).reshape(4, POOL_GROUP, POOL_GROUP)
    ps_bits = g[:, OFF_POOL_S, 0:512].reshape(4, 256, 2)
    pool_scale = lax.bitcast_convert_type(ps_bits, F32).reshape(1, D_MAIN)
    return _unpack_common(g, g_in) + (pool_w, pool_scale)


def _unpack_weights1(g, g_in):
    w_kv = g[:, OFF_W_KV:OFF_W_KV + KV_SHARD].reshape(4, D_MODEL, KV_SHARD)
    w_kv = w_kv.transpose(1, 0, 2).reshape(D_MODEL, KV_COLS)
    return _unpack_common(g, g_in) + (w_kv,)


def _replicated_rows(a):
    a = _pad_rows(a, ROWS_SMALL)
    return jnp.broadcast_to(a[None], (4,) + a.shape)


def _pack_common_grads(g_w_mkv, g_w_out):
    return [g_w_mkv.reshape(4, ROWS_W_MKV, 1024), g_w_out.reshape(4, ROWS_W_OUT, 1024)]


def _w_in_grad_halves(g_w_in):
    return g_w_in.reshape(N_CHIPS, 2, D_MODEL // 2, W_IN_SHARD)


def _pack_grads0(g_w_mkv, g_w_out, g_pool_w, g_pool_scale, g_ln_g, g_ln_b):
    parts = _pack_common_grads(g_w_mkv, g_w_out) + [
        g_pool_w.reshape(4, 4, POOL_GROUP // 4, POOL_GROUP).transpose(1, 0, 2, 3).reshape(4, ROWS_POOL_W, 1024),
        jnp.pad(g_pool_scale.reshape(4, 1, 256), ((0, 0), (0, ROWS_SMALL - 1), (0, 1024 - 256))),
        _replicated_rows(g_ln_g), _replicated_rows(g_ln_b),
        jnp.zeros((4, PACK0_ROWS - OFF_LN_B - ROWS_SMALL, 1024), F32),
    ]
    return jnp.concatenate(parts, axis=1).reshape(4, 2, PACK0_ROWS // 2, 1024)


def _pack_grads1(g_w_mkv, g_w_out, g_w_kv, g_bf):
    parts = _pack_common_grads(g_w_mkv, g_w_out) + [
        jnp.pad(g_w_kv.reshape(D_MODEL, 4, KV_SHARD).transpose(1, 0, 2).reshape(4, KV_SHARD, 1024),
                ((0, 0), (0, ROWS_W_KV - KV_SHARD), (0, 0))),
        _replicated_rows(jnp.pad(g_bf.reshape(1, -1), ((0, 0), (0, 1024 - g_bf.shape[0])))),
        jnp.zeros((4, PACK1_ROWS - OFF_BF - ROWS_SMALL, 1024), F32),
    ]
    return jnp.concatenate(parts, axis=1).reshape(4, 2, PACK1_ROWS // 2, 1024)


def _local_step(x, mem, target, w0, w1, ln_g, ln_b, b_forget, *, tm=256, tq=512, dist=None):
    S = x.shape[0]
    nq = S // tq
    w_in0, w_mkv0, w_out0, pool_w, pool_scale = w0
    g_rows = [ln_g[l:l + 1] for l in range(2)]
    b_rows = [ln_b[l:l + 1] for l in range(2)]
    bf_row = jnp.pad(b_forget.reshape(1, -1), ((0, 0), (0, LANES - FOX_HEADS)))

    mkv0 = _linear_fwd(mem, w_mkv0, tm=N_MEM, name="mem_kv0")
    h0 = _linear_fwd(x, w_in0, tm=tm, name="in_proj0")
    f0 = _mix("pool", "fwd", h=h0, xres=x, mkv=mkv0, w_out=w_out0, ln_g=g_rows[0], ln_b=b_rows[0],
              pool_w=pool_w, pool_scale=pool_scale, gather=() if dist is None else w1, tm=tm)
    z0, x1 = f0["z"], f0["xout"]
    if dist is not None:
        g1, g1_in = [lax.dynamic_update_slice(f0[f"gathered{a}"], w1[a][None], (dist["me"], 0, 0, 0))
                     for a in range(2)]
        w1 = _unpack_weights1(g1.reshape(N_CHIPS, PACK1_ROWS, 1024), g1_in)
    w_in1, w_mkv1, w_out1, w_kv = w1
    w_in, w_out = [w_in0, w_in1], [w_out0, w_out1]
    w_kvp = jnp.pad(w_kv, ((0, 0), (0, LANES - FOX_HEADS)))
    mkv = [mkv0, _linear_fwd(mem, w_mkv1, tm=N_MEM, name="mem_kv1")]
    ka, va, fl, cum = _kv_proj(x1, w_kvp, bf_row, tm=tm)
    h1, qf = _linear_fwd(x1, w_in[1], tm=tm, name="in_proj1", q_cum=cum)
    ymain1, lse_rows = _fox_fwd(qf, ka, va, tq=tq)

    b1 = _mix("fox", "bwd", h=h1, xres=x1, mkv=mkv[1], w_out=w_out[1], ln_g=g_rows[1], ln_b=b_rows[1],
              ymain=ymain1, target=target, tm=tm)
    lse16 = lse_rows[:, :, 0:FOX_HEADS // lse_rows.shape[0], :].transpose(0, 2, 1, 3).reshape(FOX_HEADS, S)
    dq_aug, dk, dv, dck_rows = _fox_bwd(qf, ka, va, b1["dmain"], _rows_for_pairs(lse16, nq, tq),
                                        _rows_for_pairs(b1["drow"], nq, tq), tq=tq)
    du1, df, dbf = _gate_bwd(dq_aug, _cols_of_rows(dck_rows, S), fl, tm=tm)

    dx1a, dw_in1 = _lin_bwd(x1, [du1, b1["drest"]], w_in[1], [(b1["dz"], ALPHA)], tm=tm, name="in_proj1_bwd")
    dx1, dw_kvp = _lin_bwd(x1, [dk, dv, df], w_kvp, [(dx1a, 1.0)], tm=tm, name="kv_proj_bwd")

    dw_mkv1 = _wgrad(mem, b1["dmkv"], name="mem_kv1_bwd")
    g_w_kv, g_bf = dw_kvp[:, 0:KV_COLS], dbf[0, 0:FOX_HEADS]

    gpacks1, psums1 = (), ()
    if dist is not None:
        gpacks1 = [_pack_grads1(dw_mkv1, b1["dw_out"], g_w_kv, g_bf), _w_in_grad_halves(dw_in1)]
    b0 = _mix("pool", "bwd", h=h0, mkv=mkv[0], w_out=w_out[0], ln_g=g_rows[0],
              pool_w=pool_w, pool_scale=pool_scale, z=z0, dy=dx1, pair_send=gpacks1, tm=tm)
    if dist is not None:
        psums1 = [_pair_sum(g, b0[f"from_sibling{a}"], dist["c_arr"], f"1{'ab'[a]}") for a, g in enumerate(gpacks1)]
    outs = _lin_bwd(x, [b0["dmain"], b0["drest"]], w_in[0], [(b0["dz"], ALPHA)], tm=tm, name="in_proj0_bwd",
                    scatter=psums1)
    dx, dw_in0 = outs[0], outs[1]
    dw_mkv0 = _wgrad(mem, b0["dmkv"], name="mem_kv0_bwd")
    g_ln_g = jnp.concatenate([b0["dln_g"], b1["dln_g"]], axis=0)
    g_ln_b = jnp.concatenate([b0["dln_b"], b1["dln_b"]], axis=0)

    if dist is None:
        grads = dict(w_in=[dw_in0, dw_in1], w_mem_kv=[dw_mkv0, dw_mkv1], w_out=[b0["dw_out"], b1["dw_out"]],
                     ln_g=g_ln_g, ln_b=g_ln_b, pool_w=b0["dpool_w"], pool_scale=b0["dpool_scale"],
                     w_kv=g_w_kv, b_forget=g_bf)
        return b1["loss"], dx, grads

    me, my_c = dist["me"], dist["my_c"]

    def with_own(pieces, psum):
        own = lax.dynamic_slice(psum, (me, 0, 0), (1,) + psum.shape[1:])
        return lax.dynamic_update_slice(pieces, own, (me, 0, 0))

    totals1 = [_sum_pieces(with_own(outs[2 + a], p), f"1{'ab'[a]}") for a, p in enumerate(psums1)]
    gpacks0 = [_pack_grads0(dw_mkv0, b0["dw_out"], b0["dpool_w"], b0["dpool_scale"], g_ln_g, g_ln_b),
               _w_in_grad_halves(dw_in0)]
    sib0 = _send_half_to_sibling(gpacks0, 0)
    psums0 = [_pair_sum(g, sib0[a], dist["c_arr"], f"0{'ab'[a]}") for a, g in enumerate(gpacks0)]
    pieces0 = _scatter_pieces(psums0, 0)
    totals0 = [_sum_pieces(with_own(pieces0[a], p), f"0{'ab'[a]}") for a, p in enumerate(psums0)]
    totals = totals0 + totals1
    halves = _exchange_halves(totals)
    full = [lax.dynamic_update_slice(h, t[None], (my_c, 0, 0)) for h, t in zip(halves, totals)]
    shard0, shard1 = full[0].reshape(PACK0_ROWS, 1024), full[2].reshape(PACK1_ROWS, 1024)
    g_w_in = jnp.stack([full[1].reshape(D_MODEL, W_IN_SHARD), full[3].reshape(D_MODEL, W_IN_SHARD)])
    return b1["loss"], dx, shard0, shard1, g_w_in


def kernel(x, mem, w_in, w_mem_kv, w_out, ln_g, ln_b, pool_w, pool_scale, w_kv_shared, b_forget, loss_target, m_w_in, m_w_mem_kv, m_w_out, m_ln_g, m_ln_b, m_pool_w, m_pool_scale, m_w_kv_shared, m_b_forget, v_w_in, v_w_mem_kv, v_w_out, v_ln_g, v_ln_b, v_pool_w, v_pool_scale, v_w_kv_shared, v_b_forget):
    dist = dict(c_arr=lax.axis_index("c").astype(jnp.int32).reshape(1),
                me=2 * lax.axis_index("x") + lax.axis_index("y"), my_c=lax.axis_index("c"))

    wpacks0, wpacks1 = _pack_weight_shards(w_in, w_mem_kv, w_out, pool_w, w_kv_shared, pool_scale)
    g0, g0_in = [lax.dynamic_update_slice(g, p[None], (dist["me"], 0, 0, 0))
                 for g, p in zip(_all_gather_shards(wpacks0), wpacks0)]
    w0 = _unpack_weights0(g0.reshape(N_CHIPS, PACK0_ROWS, 1024), g0_in)

    loss_vec, dx, shard0, shard1, g_w_in = _local_step(x[0], mem[0], loss_target[0], w0, wpacks1, ln_g, ln_b,
                                                       b_forget, dist=dist)
    loss = lax.psum(0.5 / D_MODEL * jnp.sum(loss_vec), ("x", "y", "c"))

    def per_layer(off, rows, shape):
        return jnp.concatenate([shard0[off:off + rows], shard1[off:off + rows]], axis=0).reshape(shape)

    g_w_mkv = per_layer(OFF_W_MKV, ROWS_W_MKV, w_mem_kv.shape)
    g_w_out = per_layer(OFF_W_OUT, ROWS_W_OUT, w_out.shape)
    g_pool_w = shard0[OFF_POOL_W:OFF_POOL_W + ROWS_POOL_W].reshape(pool_w.shape)
    g_w_kv = shard1[OFF_W_KV:OFF_W_KV + KV_SHARD].reshape(w_kv_shared.shape)
    g_pool_scale = shard0[OFF_POOL_S:OFF_POOL_S + 1, 0:256].reshape(pool_scale.shape)
    g_ln_g = shard0[OFF_LN_G:OFF_LN_G + 2]
    g_ln_b = shard0[OFF_LN_B:OFF_LN_B + 2]
    g_bf = shard1[OFF_BF, 0:FOX_HEADS]

    names = ["w_in", "w_mem_kv", "w_out", "ln_g", "ln_b", "pool_w", "pool_scale", "w_kv_shared", "b_forget"]
    ws = [w_in, w_mem_kv, w_out, ln_g, ln_b, pool_w, pool_scale, w_kv_shared, b_forget]
    gs = [g_w_in, g_w_mkv, g_w_out, g_ln_g, g_ln_b, g_pool_w, g_pool_scale, g_w_kv, g_bf]
    ms = [m_w_in, m_w_mem_kv, m_w_out, m_ln_g, m_ln_b, m_pool_w, m_pool_scale, m_w_kv_shared, m_b_forget]
    vs = [v_w_in, v_w_mem_kv, v_w_out, v_ln_g, v_ln_b, v_pool_w, v_pool_scale, v_w_kv_shared, v_b_forget]
    deltas, new_ms, new_vs = [], [], []
    for nm, w, gg, mm, vv in zip(names, ws, gs, ms, vs):
        two_d = (-1, w.shape[-1])
        d, nmm, nvv = _adamw(w.reshape(two_d), gg.reshape(two_d), mm.reshape(two_d), vv.reshape(two_d),
                             name=f"adamw_{nm}")
        deltas.append(d.reshape(w.shape))
        new_ms.append(nmm.reshape(w.shape))
        new_vs.append(nvv.reshape(w.shape))

    return (loss, dx[None], *gs, *deltas, *new_ms, *new_vs)
```

```python
import jax
import jax.numpy as jnp
from jax import lax
from jax.experimental import pallas as pl
from jax.experimental.pallas import tpu as pltpu

F32 = jnp.float32
BF16 = jnp.bfloat16

D_MODEL = 1024
D_MAIN = 1024
D_MEM = 512
D_MIX = D_MAIN + D_MEM
D_IN = 2 * D_MIX
N_MEM = 256
MEM_HEADS = 4
MEM_HEAD_DIM = 128
FOX_HEADS = 16
FOX_HEAD_DIM = 64
FOX_SCALE = 0.125
POOL_WINDOWS = (2, 4, 8, 16)
POOL_GROUP = 256
POOL_HALO = 16
ALPHA = 4.0 ** 0.25
LN_EPS = 1e-5
LANES = 128
N_CHIPS = 4

ADAM_LR = 0.001
ADAM_B1 = 0.9
ADAM_B2 = 0.999
ADAM_EPS = 1e-08
ADAM_WD = 0.01
ADAM_STEP = 10

VMEM_LIMIT = 56 * 1024 * 1024

ROWS_W_MKV = (D_MODEL // N_CHIPS) * 2 * D_MEM // 1024
ROWS_W_OUT = (D_MIX // N_CHIPS) * D_MODEL // 1024
ROWS_POOL_W = 4 * (POOL_GROUP // N_CHIPS) * POOL_GROUP // 1024
KV_COLS = 2 * D_MAIN + FOX_HEADS
KV_SHARD = KV_COLS // N_CHIPS
ROWS_W_KV = 528
ROWS_SMALL = 16
OFF_W_MKV = 0
OFF_W_OUT = OFF_W_MKV + ROWS_W_MKV
OFF_TAIL = OFF_W_OUT + ROWS_W_OUT
OFF_POOL_W = OFF_TAIL
OFF_POOL_S = OFF_POOL_W + ROWS_POOL_W
OFF_LN_G = OFF_POOL_S + ROWS_SMALL
OFF_LN_B = OFF_LN_G + ROWS_SMALL
PACK0_ROWS = 768
OFF_W_KV = OFF_TAIL
OFF_BF = OFF_W_KV + ROWS_W_KV
PACK1_ROWS = 1280
PACK_TILE = 128
W_IN_SHARD = D_IN // N_CHIPS


def _dot(a, b):
    return jnp.dot(a, b, preferred_element_type=F32)


def _dot_nt(a, b):
    return lax.dot_general(a, b, (((1,), (1,)), ((), ())), preferred_element_type=F32)


def _dot_tn(a, b):
    return lax.dot_general(a, b, (((0,), (0,)), ((), ())), preferred_element_type=F32)


def _params(n_axes=1):
    return pltpu.CompilerParams(dimension_semantics=("arbitrary",) * n_axes,
                                vmem_limit_bytes=VMEM_LIMIT)


def _const_spec(shape):
    zeros = (0,) * len(shape)
    return pl.BlockSpec(shape, lambda *_: zeros, pipeline_mode=pl.Buffered(1))


def _split3(x):
    hi = x.astype(BF16)
    r = x - hi.astype(F32)
    mid = r.astype(BF16)
    lo = (r - mid.astype(F32)).astype(BF16)
    return hi, mid, lo


def _cols_and_chunk(w):
    if w.ndim == 3:
        return N_CHIPS * w.shape[2], 256
    return w.shape[1], (512 if w.shape[1] % 512 == 0 else LANES)


def _w_cols(w_ref, n0, nc):
    if len(w_ref.shape) == 3:
        per = w_ref.shape[2]
        assert n0 // per == (n0 + nc - 1) // per
        return w_ref.at[n0 // per, :, n0 % per:n0 % per + nc]
    return w_ref.at[:, n0:n0 + nc]


def _linear_fwd(x, w, *, tm, name, q_cum=None, gather=()):
    S, K = x.shape
    N, nc = _cols_and_chunk(w)
    aug = q_cum is not None
    ng = len(gather)
    n_tiles = S // tm
    assert not (aug and ng)

    def body(*refs):
        x_ref, w_ref = refs[0], refs[1]
        o_ref = refs[4] if aug else refs[2 + ng]
        if ng:
            i = pl.program_id(0)
            start, forward, finish = _gather_steps(refs[2:2 + ng], refs[3 + ng:3 + 2 * ng],
                                                   refs[3 + 2 * ng], refs[4 + 2 * ng])
            pl.when(i == 0)(start)
            pl.when(i == n_tiles - 3)(forward)
            pl.when(i == n_tiles - 1)(finish)
        xb = x_ref[...].astype(BF16)
        if aug:
            lhs = _placement_lhs(refs[2][...], tm)
        for n0 in range(0, N, nc):
            r = _dot(xb, _w_cols(w_ref, n0, nc)[...])
            o_ref[:, n0:n0 + nc] = r.astype(BF16)
            if aug and n0 < D_MAIN:
                _store_augmented(refs[5], r * FOX_SCALE, n0 // LANES, lhs, refs[3])

    in_specs = [pl.BlockSpec((tm, K), lambda i: (i, 0)), _const_spec(w.shape)]
    out_specs = [pl.BlockSpec((tm, N), lambda i: (i, 0))]
    out_shape = [jax.ShapeDtypeStruct((S, N), BF16)]
    extra = []
    if aug:
        in_specs += [pl.BlockSpec((tm, LANES), lambda i: (i, 0)), _const_spec((LANES, AUG_W))]
        out_specs.append(pl.BlockSpec((tm, AUG_W), lambda i: (i, 0)))
        out_shape.append(jax.ShapeDtypeStruct((S, AUG_W), BF16))
        extra = [q_cum, _placement(AUG_A, AUG_B)]
    if ng:
        in_specs += [_ANY] * ng
        out_specs += [_ANY] * ng
        out_shape += [_gathered_shape(p) for p in gather]
        extra = list(gather)
    outs = pl.pallas_call(
        body, name=name, grid=(n_tiles,),
        in_specs=in_specs, out_specs=out_specs, out_shape=out_shape,
        scratch_shapes=_gather_sems(ng) if ng else [],
        compiler_params=_params(),
    )(x, w, *extra)
    return outs if (aug or ng) else outs[0]


def _kv_proj(x1, w_kv, bf_row, *, tm):
    S = x1.shape[0]

    def body(x_ref, w_ref, b_ref, pk_ref, pv_ref, k_ref, v_ref, fl_ref, cum_ref, carry_ref):
        i = pl.program_id(0)

        @pl.when(i == 0)
        def _():
            carry_ref[...] = jnp.zeros_like(carry_ref)

        xb = x_ref[...].astype(BF16)
        fl = _dot(xb, w_ref[:, 2 * D_MAIN:2 * D_MAIN + LANES]) + b_ref[...]
        fl_ref[...] = fl
        log_f = jnp.minimum(fl, 0.0) - jnp.log1p(jnp.exp(-jnp.abs(fl)))
        r = lax.broadcasted_iota(jnp.int32, (tm, tm), 0)
        c = lax.broadcasted_iota(jnp.int32, (tm, tm), 1)
        tri = jnp.where(c <= r, 1.0, 0.0).astype(BF16)
        hi, mid, lo = _split3(log_f)
        cum = (_dot(tri, hi) + _dot(tri, mid)) + _dot(tri, lo) + carry_ref[0:1, :]
        cum_ref[...] = cum
        carry_ref[0:1, :] = cum[tm - 1:tm, :]
        lhs_k, lhs_v = _placement_lhs(-cum, tm), _placement_lhs(None, tm)
        for n0 in range(0, D_MAIN, 512):
            _store_augmented(k_ref, _dot(xb, w_ref[:, n0:n0 + 512]), n0 // LANES, lhs_k, pk_ref)
            _store_augmented(v_ref, _dot(xb, w_ref[:, D_MAIN + n0:D_MAIN + n0 + 512]), n0 // LANES, lhs_v, pv_ref)

    return pl.pallas_call(
        body, name="kv_proj", grid=(S // tm,),
        in_specs=[pl.BlockSpec((tm, D_MODEL), lambda i: (i, 0)),
                  _const_spec((D_MODEL, 2 * D_MAIN + LANES)), _const_spec((1, LANES)),
                  _const_spec((LANES, AUG_W)), _const_spec((LANES, AUG_W))],
        out_specs=[pl.BlockSpec((tm, AUG_W), lambda i: (i, 0)),
                   pl.BlockSpec((tm, AUG_W), lambda i: (i, 0)),
                   pl.BlockSpec((tm, LANES), lambda i: (i, 0)),
                   pl.BlockSpec((tm, LANES), lambda i: (i, 0))],
        out_shape=[jax.ShapeDtypeStruct((S, AUG_W), BF16), jax.ShapeDtypeStruct((S, AUG_W), BF16),
                   jax.ShapeDtypeStruct((S, LANES), F32), jax.ShapeDtypeStruct((S, LANES), F32)],
        scratch_shapes=[pltpu.VMEM((8, LANES), F32)],
        compiler_params=_params(),
    )(x1, w_kv, bf_row, _placement(AUG_B, AUG_A), _placement(None, AUG_A))


def _gate_bwd(dq_aug, dck, fl, *, tm):
    S = fl.shape[0]
    n = S // tm

    def body(dq_ref, dck_ref, fl_ref, du_ref, df_ref, db_ref, carry_ref):
        i = pl.program_id(0)

        @pl.when(i == 0)
        def _():
            carry_ref[...] = jnp.zeros_like(carry_ref)
            db_ref[...] = jnp.zeros_like(db_ref)

        lane = lax.broadcasted_iota(jnp.int32, (tm, LANES), 1)
        half0 = lane < FOX_HEAD_DIM
        dcq = jnp.zeros((tm, LANES), F32)
        for hp in range(FOX_HEADS // 2):
            b0 = dq_ref[:, 2 * hp * LANES:(2 * hp + 1) * LANES]
            b1 = dq_ref[:, (2 * hp + 1) * LANES:(2 * hp + 2) * LANES]
            du_ref[:, hp * LANES:(hp + 1) * LANES] = (
                jnp.where(half0, b0, pltpu.roll(b1, FOX_HEAD_DIM, 1)) * FOX_SCALE).astype(BF16)
            r0 = jnp.sum(jnp.where(lane == AUG_A, b0, 0.0), axis=1, keepdims=True)
            r1 = jnp.sum(jnp.where(lane == AUG_A, b1, 0.0), axis=1, keepdims=True)
            dcq = dcq + jnp.where(lane == 2 * hp, r0, 0.0) + jnp.where(lane == 2 * hp + 1, r1, 0.0)
        dcum = dcq - dck_ref[...]
        r = lax.broadcasted_iota(jnp.int32, (tm, tm), 0)
        c = lax.broadcasted_iota(jnp.int32, (tm, tm), 1)
        tri = jnp.where(c >= r, 1.0, 0.0).astype(BF16)
        hi, mid, lo = _split3(dcum)
        rev = (_dot(tri, hi) + _dot(tri, mid)) + _dot(tri, lo) + carry_ref[0:1, :]
        carry_ref[0:1, :] = rev[0:1, :]
        fl_v = fl_ref[...]
        df = rev * (1.0 / (1.0 + jnp.exp(fl_v)))
        df_ref[...] = df
        db_ref[...] += jnp.sum(df, axis=0, keepdims=True)

    return pl.pallas_call(
        body, name="gate_bwd", grid=(n,),
        in_specs=[pl.BlockSpec((tm, AUG_W), lambda i: (n - 1 - i, 0)),
                  pl.BlockSpec((tm, LANES), lambda i: (n - 1 - i, 0)),
                  pl.BlockSpec((tm, LANES), lambda i: (n - 1 - i, 0))],
        out_specs=[pl.BlockSpec((tm, D_MAIN), lambda i: (n - 1 - i, 0)),
                   pl.BlockSpec((tm, LANES), lambda i: (n - 1 - i, 0)),
                   pl.BlockSpec((1, LANES), lambda i: (0, 0))],
        out_shape=[jax.ShapeDtypeStruct((S, D_MAIN), BF16),
                   jax.ShapeDtypeStruct((S, LANES), F32), jax.ShapeDtypeStruct((1, LANES), F32)],
        scratch_shapes=[pltpu.VMEM((8, LANES), F32)],
        compiler_params=_params(),
    )(dq_aug, dck, fl)


def _silu_and_grad(g):
    sg = 1.0 / (1.0 + jnp.exp(-g))
    return g * sg, sg * (1.0 + g * (1.0 - sg))


def _mix(kind, mode, *, h, xres=None, mkv, w_out, ln_g, ln_b=None, pool_w=None, pool_scale=None,
         ymain=None, target=None, z=None, dy=None, gather=(), pair_send=(), tm):
    S = h.shape[0]
    n = S // tm
    pool = kind == "pool"
    bwd = mode == "bwd"
    loss_head = bwd and not pool
    rev = pool and bwd
    mem_scale = MEM_HEAD_DIM ** -0.5

    def t_of(i):
        return (n - 1 - i) if rev else i

    row = lambda i: (t_of(i), 0)
    names, arrays, specs = [], [], []

    def add(name, arr, spec):
        names.append(name)
        arrays.append(arr)
        specs.append(spec)

    add("h", h, pl.BlockSpec((tm, D_IN), row))
    if pool:
        hb = tm // POOL_HALO
        add("halo", h, pl.BlockSpec((POOL_HALO, D_MAIN), lambda i: (jnp.maximum(t_of(i) * hb - 1, 0), 0)))
        add("pool_w", pool_w, _const_spec((4, POOL_GROUP, POOL_GROUP)))
        add("pool_scale", pool_scale, _const_spec((1, D_MAIN)))
    else:
        add("ymain", ymain, pl.BlockSpec((tm, D_MAIN), row))
    add("mkv", mkv, _const_spec((N_MEM, 2 * D_MEM)))
    add("w_out", w_out, _const_spec((D_MIX, D_MODEL)))
    add("ln_g", ln_g, _const_spec((1, D_MODEL)))
    if not (pool and bwd):
        add("xres", xres, pl.BlockSpec((tm, D_MODEL), row))
        add("ln_b", ln_b, _const_spec((1, D_MODEL)))
    if loss_head:
        add("target", target, pl.BlockSpec((tm, D_MODEL), row))
    if pool and bwd:
        add("z", z, pl.BlockSpec((tm, D_MODEL), row))
        add("dy", dy, pl.BlockSpec((tm, D_MODEL), row))
    for a, p in enumerate(gather):
        add(f"gather_src{a}", p, _ANY)
    for a, p in enumerate(pair_send):
        add(f"pair_src{a}", p, _ANY)

    onames, oshapes, ospecs = [], [], []

    def add_out(name, shape, dtype, spec):
        onames.append(name)
        oshapes.append(jax.ShapeDtypeStruct(shape, dtype))
        ospecs.append(spec)

    const2 = lambda i: (0, 0)
    if not bwd:
        add_out("z", (S, D_MODEL), F32, pl.BlockSpec((tm, D_MODEL), row))
        add_out("xout", (S, D_MODEL), F32, pl.BlockSpec((tm, D_MODEL), row))
    else:
        add_out("dz", (S, D_MODEL), F32, pl.BlockSpec((tm, D_MODEL), row))
        if pool:
            add_out("dmain", (S, D_MAIN), BF16, pl.BlockSpec((tm, D_MAIN), row))
        else:
            add_out("dmain", (S, 2 * D_MAIN), BF16, pl.BlockSpec((tm, 2 * D_MAIN), row))
        add_out("drest", (S, D_IN - D_MAIN), BF16, pl.BlockSpec((tm, D_IN - D_MAIN), row))
        add_out("dw_out", (D_MIX, D_MODEL), F32, pl.BlockSpec((D_MIX, D_MODEL), const2))
        add_out("dmkv", (N_MEM, 2 * D_MEM), F32, pl.BlockSpec((N_MEM, 2 * D_MEM), const2))
        add_out("dln_g", (1, D_MODEL), F32, pl.BlockSpec((1, D_MODEL), const2))
        add_out("dln_b", (1, D_MODEL), F32, pl.BlockSpec((1, D_MODEL), const2))
        if pool:
            add_out("dpool_w", (4, POOL_GROUP, POOL_GROUP), F32,
                    pl.BlockSpec((4, POOL_GROUP, POOL_GROUP), lambda i: (0, 0, 0)))
            add_out("dpool_scale", (1, D_MAIN), F32, pl.BlockSpec((1, D_MAIN), const2))
        else:
            add_out("loss", (1, D_MODEL), F32, pl.BlockSpec((1, D_MODEL), const2))
            add_out("drow", (FOX_HEADS, S), F32, pl.BlockSpec((FOX_HEADS, tm), lambda i: (0, i)))

    for a, p in enumerate(gather):
        add_out(f"gathered{a}", (N_CHIPS,) + p.shape, p.dtype, _ANY)
    for a, p in enumerate(pair_send):
        add_out(f"from_sibling{a}", (N_CHIPS,) + p.shape[2:], p.dtype, _ANY)

    scratch = [pltpu.VMEM((tm, D_MIX), BF16),
               pltpu.VMEM((tm, D_MEM), F32)]
    if pool:
        scratch.append(pltpu.VMEM((tm + 2 * POOL_HALO, D_MAIN), F32))
    if rev:
        scratch.append(pltpu.VMEM((POOL_HALO, D_MAIN), F32))
    assert not (gather and pair_send)
    if gather:
        scratch += _gather_sems(len(gather))
    if pair_send:
        scratch += _pair_sems(len(pair_send))
    n_in, n_out = len(names), len(onames)

    def body(*refs):
        R = dict(zip(names, refs[:n_in]))
        O = dict(zip(onames, refs[n_in:n_in + n_out]))
        sc = refs[n_in + n_out:]
        yc_ref, ymem_ref = sc[0], sc[1]
        ext_ref = sc[2] if pool else None
        carry_ref = sc[3] if rev else None
        i = pl.program_id(0)
        t = t_of(i)
        h_ref = R["h"]
        gamma = R["ln_g"][...]

        if gather:
            start, forward, finish = _gather_steps(
                [R[f"gather_src{a}"] for a in range(len(gather))],
                [O[f"gathered{a}"] for a in range(len(gather))], sc[-2], sc[-1])
            pl.when(i == 0)(start)
            pl.when(i == n - 3)(forward)
            pl.when(i == n - 1)(finish)
        if pair_send:
            start, finish = _pair_send_steps(
                [R[f"pair_src{a}"] for a in range(len(pair_send))],
                [O[f"from_sibling{a}"] for a in range(len(pair_send))], sc[-2], sc[-1])
            pl.when(i == 0)(start)
            pl.when(i == n - 1)(finish)

        if bwd:
            @pl.when(i == 0)
            def _():
                for nm in ("dw_out", "dmkv", "dln_g", "dln_b", "dpool_w", "dpool_scale", "loss"):
                    if nm in O:
                        O[nm][...] = jnp.zeros_like(O[nm])
                if rev:
                    carry_ref[...] = jnp.zeros_like(carry_ref)

        if pool:
            u = h_ref[:, 0:D_MAIN].astype(F32)
            halo = R["halo"][...].astype(F32)
            ext_ref[0:POOL_HALO, :] = jnp.where(t > 0, halo, 0.0)
            ext_ref[POOL_HALO:POOL_HALO + tm, :] = u
            tpos = t * tm + lax.broadcasted_iota(jnp.int32, (tm, 1), 0)
            pms, invcs = [], []
            for gi, w in enumerate(POOL_WINDOWS):
                cs = slice(gi * POOL_GROUP, (gi + 1) * POOL_GROUP)
                acc = ext_ref[POOL_HALO:POOL_HALO + tm, cs]
                for k in range(1, w):
                    acc = acc + ext_ref[POOL_HALO - k:POOL_HALO - k + tm, cs]
                invc = 1.0 / jnp.minimum(tpos + 1, w).astype(F32)
                pm = (acc * invc - u[:, cs]).astype(BF16)
                pms.append(pm)
                invcs.append(invc)
            mixed = [_dot(pms[gi], R["pool_w"][gi]) for gi in range(4)]
            ps = R["pool_scale"][...]
            y_main = [mixed[gi] * ps[:, gi * POOL_GROUP:(gi + 1) * POOL_GROUP] for gi in range(4)]
        else:
            y_main = [R["ymain"][:, gi * 256:(gi + 1) * 256].astype(F32) for gi in range(4)]

        probs = []
        for hd in range(MEM_HEADS):
            sl = slice(D_MAIN + hd * MEM_HEAD_DIM, D_MAIN + (hd + 1) * MEM_HEAD_DIM)
            ksl = slice(hd * MEM_HEAD_DIM, (hd + 1) * MEM_HEAD_DIM)
            vsl = slice(D_MEM + hd * MEM_HEAD_DIM, D_MEM + (hd + 1) * MEM_HEAD_DIM)
            s = _dot_nt(h_ref[:, sl], R["mkv"][:, ksl]) * mem_scale
            e = jnp.exp(s - jnp.max(s, axis=1, keepdims=True))
            p = e / jnp.sum(e, axis=1, keepdims=True)
            probs.append(p)
            ymem_ref[:, ksl] = _dot(p.astype(BF16), R["mkv"][:, vsl])

        g_off = D_MIX
        gate_d = []
        for gi in range(4):
            cs = slice(gi * 256, (gi + 1) * 256)
            gm = h_ref[:, g_off + gi * 256:g_off + (gi + 1) * 256].astype(F32)
            sv, sd = _silu_and_grad(gm)
            yc_ref[:, cs] = (y_main[gi] * sv).astype(BF16)
            gate_d.append((sv, sd))
        gq = h_ref[:, g_off + D_MAIN:D_IN].astype(F32)
        svq, sdq = _silu_and_grad(gq)
        yc_ref[:, D_MAIN:D_MIX] = (ymem_ref[...] * svq).astype(BF16)

        if pool and bwd:
            zt = R["z"][...]
        else:
            o = _dot(yc_ref[...], R["w_out"][...])
            zt = ALPHA * R["xres"][...] + o
        mu = jnp.mean(zt, axis=1, keepdims=True)
        zc = zt - mu
        var = jnp.mean(zc * zc, axis=1, keepdims=True)
        rstd = lax.rsqrt(var + LN_EPS)
        xhat = zc * rstd
        if not bwd:
            O["z"][...] = zt
            O["xout"][...] = xhat * gamma + R["ln_b"][...]
            return

        if loss_head:
            xo = xhat * gamma + R["ln_b"][...]
            err = xo - R["target"][...]
            O["loss"][...] += jnp.sum(err * err, axis=0, keepdims=True)
            dyt = err * (1.0 / D_MODEL)
        else:
            dyt = R["dy"][...]

        O["dln_g"][...] += jnp.sum(dyt * xhat, axis=0, keepdims=True)
        O["dln_b"][...] += jnp.sum(dyt, axis=0, keepdims=True)
        gdy = dyt * gamma
        m1 = jnp.mean(gdy, axis=1, keepdims=True)
        m2 = jnp.mean(gdy * xhat, axis=1, keepdims=True)
        dz = rstd * (gdy - m1 - xhat * m2)
        O["dz"][...] = dz
        dzb = dz.astype(BF16)

        for n0 in range(0, D_MIX, 512):
            O["dw_out"][n0:n0 + 512, :] += _dot_tn(yc_ref[:, n0:n0 + 512], dzb)
        dyc_mem = _dot_nt(dzb, R["w_out"][D_MAIN:D_MIX, :])

        O["drest"][:, D_MEM + D_MAIN:D_MEM + D_MAIN + D_MEM] = (dyc_mem * ymem_ref[...] * sdq).astype(BF16)
        dymem = dyc_mem * svq
        for hd in range(MEM_HEADS):
            sl = slice(D_MAIN + hd * MEM_HEAD_DIM, D_MAIN + (hd + 1) * MEM_HEAD_DIM)
            ksl = slice(hd * MEM_HEAD_DIM, (hd + 1) * MEM_HEAD_DIM)
            vsl = slice(D_MEM + hd * MEM_HEAD_DIM, D_MEM + (hd + 1) * MEM_HEAD_DIM)
            p = probs[hd]
            dyb = dymem[:, ksl].astype(BF16)
            dp = _dot_nt(dyb, R["mkv"][:, vsl])
            ds = p * (dp - jnp.sum(dp * p, axis=1, keepdims=True)) * mem_scale
            dsb = ds.astype(BF16)
            O["drest"][:, ksl] = _dot(dsb, R["mkv"][:, ksl]).astype(BF16)
            O["dmkv"][:, ksl] += _dot_tn(dsb, h_ref[:, sl])
            O["dmkv"][:, vsl] += _dot_tn(p.astype(BF16), dyb)

        dmain = []
        for gi in range(4):
            cs = slice(gi * 256, (gi + 1) * 256)
            dyc_g = _dot_nt(dzb, R["w_out"][cs, :])
            sv, sd = gate_d[gi]
            O["drest"][:, D_MEM + gi * 256:D_MEM + (gi + 1) * 256] = (dyc_g * y_main[gi] * sd).astype(BF16)
            dmain.append(dyc_g * sv)

        if not pool:
            prod = []
            lane2 = lax.broadcasted_iota(jnp.int32, (tm, LANES), 1)
            for gi in range(4):
                cs = slice(gi * 256, (gi + 1) * 256)
                db16 = dmain[gi].astype(BF16)
                dbf = db16.astype(F32)
                prod.append(dbf * R["ymain"][:, cs].astype(F32))
                for pr in range(2):
                    blk = dbf[:, pr * LANES:(pr + 1) * LANES]
                    base = (4 * gi + 2 * pr) * LANES
                    O["dmain"][:, base:base + LANES] = jnp.where(lane2 < FOX_HEAD_DIM, blk, 0.0).astype(BF16)
                    O["dmain"][:, base + LANES:base + 2 * LANES] = jnp.where(
                        lane2 < FOX_HEAD_DIM, pltpu.roll(blk, FOX_HEAD_DIM, 1), 0.0).astype(BF16)
            dcol = jnp.zeros((tm, LANES), F32)
            for gi in range(4):
                dr = lax.broadcasted_iota(jnp.int32, (256, LANES), 0)
                hc = lax.broadcasted_iota(jnp.int32, (256, LANES), 1)
                sel = jnp.where(jnp.right_shift(dr, 6) + gi * 4 == hc, 1.0, 0.0).astype(BF16)
                hi, mid, lo = _split3(prod[gi])
                dcol = dcol + ((_dot(hi, sel) + _dot(mid, sel)) + _dot(lo, sel))
            O["drow"][...] = dcol.T[0:FOX_HEADS, :]
            return

        ps = R["pool_scale"][...]
        dpm_list = []
        for gi in range(4):
            cs = slice(gi * 256, (gi + 1) * 256)
            O["dpool_scale"][:, cs] += jnp.sum(dmain[gi] * mixed[gi], axis=0, keepdims=True)
            dmix = (dmain[gi] * ps[:, cs]).astype(BF16)
            O["dpool_w"][gi] += _dot_tn(pms[gi], dmix)
            dpm = _dot_nt(dmix, R["pool_w"][gi])
            dpm_list.append(dpm)
            ext_ref[0:tm, cs] = dpm * invcs[gi]
        ext_ref[tm:tm + POOL_HALO, :] = carry_ref[...]
        carry_ref[...] = ext_ref[0:POOL_HALO, :]
        for gi, w in enumerate(POOL_WINDOWS):
            cs = slice(gi * 256, (gi + 1) * 256)
            acc = ext_ref[0:tm, cs]
            for k in range(1, w):
                acc = acc + ext_ref[k:k + tm, cs]
            O["dmain"][:, cs] = (acc - dpm_list[gi]).astype(BF16)

    outs = pl.pallas_call(
        body, name=f"mix_{kind}_{mode}", grid=(n,),
        in_specs=specs, out_specs=ospecs, out_shape=oshapes,
        scratch_shapes=scratch, compiler_params=_params(),
    )(*arrays)
    return dict(zip(onames, outs))


def _lin_bwd(xin, dhs, w, res, *, tm, name, scatter=()):
    S, K = xin.shape
    N, nc_w = _cols_and_chunk(w)
    nj, nr, ns = len(dhs), len(res), len(scatter)
    n_tiles = S // tm
    widths = [dh.shape[1] for dh in dhs]
    chunks = [nc_w if w.ndim == 3 else (512 if wd % 512 == 0 else wd) for wd in widths]
    assert sum(widths) == N and all(wd % c == 0 for wd, c in zip(widths, chunks))
    scales = [s for _, s in res]
    n_in = 2 + nj + nr + ns

    def body(*refs):
        x_ref = refs[0]
        dh_refs = refs[1:1 + nj]
        w_ref = refs[1 + nj]
        r_refs = refs[2 + nj:2 + nj + nr]
        dx_ref, dw_ref = refs[n_in], refs[n_in + 1]
        i = pl.program_id(0)

        if ns:
            start, finish = _scatter_steps(refs[n_in - ns:n_in], refs[n_in + 2:n_in + 2 + ns],
                                           refs[n_in + 2 + ns], refs[n_in + 3 + ns])
            pl.when(i == 0)(start)
            pl.when(i == n_tiles - 1)(finish)

        @pl.when(i == 0)
        def _():
            dw_ref[...] = jnp.zeros_like(dw_ref)

        xb = x_ref[...].astype(BF16)
        dx = jnp.zeros((tm, K), F32)
        for r_ref, sc in zip(r_refs, scales):
            dx = dx + sc * r_ref[...]
        off = 0
        for j in range(nj):
            nc = chunks[j]
            for n0 in range(0, widths[j], nc):
                dhb = dh_refs[j][:, n0:n0 + nc].astype(BF16)
                dx = dx + _dot_nt(dhb, _w_cols(w_ref, off + n0, nc)[...])
                _w_cols(dw_ref, off + n0, nc)[...] += _dot_tn(xb, dhb)
            off += widths[j]
        dx_ref[...] = dx

    zeros = (0,) * w.ndim
    in_specs = [pl.BlockSpec((tm, K), lambda i: (i, 0))]
    in_specs += [pl.BlockSpec((tm, n), lambda i: (i, 0)) for n in widths]
    in_specs += [_const_spec(w.shape)]
    in_specs += [pl.BlockSpec((tm, K), lambda i: (i, 0)) for _ in res]
    in_specs += [_ANY] * ns
    out_specs = [pl.BlockSpec((tm, K), lambda i: (i, 0)), pl.BlockSpec(w.shape, lambda i: zeros)] + [_ANY] * ns
    out_shape = [jax.ShapeDtypeStruct((S, K), F32), jax.ShapeDtypeStruct(w.shape, F32)]
    out_shape += [jax.ShapeDtypeStruct(p.shape, p.dtype) for p in scatter]
    return pl.pallas_call(
        body, name=name, grid=(n_tiles,),
        in_specs=in_specs, out_specs=out_specs, out_shape=out_shape,
        scratch_shapes=_scatter_sems(ns) if ns else [],
        compiler_params=_params(),
    )(xin, *dhs, w, *[r for r, _ in res], *scatter)


def _wgrad(xin, dh, *, name):
    M, K = xin.shape
    N = dh.shape[1]

    def body(x_ref, dh_ref, o_ref):
        o_ref[...] = _dot_tn(x_ref[...].astype(BF16), dh_ref[...].astype(BF16))

    return pl.pallas_call(
        body, name=name, out_shape=jax.ShapeDtypeStruct((K, N), F32),
        compiler_params=pltpu.CompilerParams(vmem_limit_bytes=VMEM_LIMIT),
    )(xin, dh)


AUG_A = FOX_HEAD_DIM
AUG_B = FOX_HEAD_DIM + 3
AUG_W = FOX_HEADS * LANES


def _placement(val_lane, ones_lane):
    r = jnp.arange(LANES)[:, None]
    c = jnp.arange(AUG_W)[None, :]
    head, lane = c // LANES, c % LANES
    m = jnp.zeros((LANES, AUG_W), jnp.bool_)
    if val_lane is not None:
        for part in range(3):
            m = m | ((r == part * FOX_HEADS + head) & (lane == val_lane + part))
    if ones_lane is not None:
        m = m | ((r == 3 * FOX_HEADS) & (lane >= ones_lane) & (lane < ones_lane + 3))
    return m.astype(BF16)


def _placement_lhs(val, tm):
    lane = lax.broadcasted_iota(jnp.int32, (tm, LANES), 1)
    lhs = jnp.where(lane == 3 * FOX_HEADS, 1.0, 0.0)
    if val is not None:
        hi, mid, lo = [p.astype(F32) for p in _split3(val)]
        lhs = jnp.where(lane < FOX_HEADS, hi, jnp.where(
            lane < 2 * FOX_HEADS, pltpu.roll(mid, FOX_HEADS, 1), jnp.where(
                lane < 3 * FOX_HEADS, pltpu.roll(lo, 2 * FOX_HEADS, 1), lhs)))
    return lhs.astype(BF16)


def _store_augmented(o_ref, data, first_pair, lhs, p_ref):
    tm = data.shape[0]
    is_data = lax.broadcasted_iota(jnp.int32, (tm, LANES), 1) < FOX_HEAD_DIM
    for j in range(data.shape[1] // LANES):
        base = 2 * (first_pair + j) * LANES
        extra = _dot(lhs, p_ref[:, base:base + 2 * LANES])
        blk = data[:, j * LANES:(j + 1) * LANES]
        o_ref[:, base:base + LANES] = jnp.where(is_data, blk, extra[:, 0:LANES]).astype(BF16)
        o_ref[:, base + LANES:base + 2 * LANES] = jnp.where(
            is_data, pltpu.roll(blk, FOX_HEAD_DIM, 1), extra[:, LANES:2 * LANES]).astype(BF16)


def _cols_of_rows(rows, S):
    nh = FOX_HEADS // rows.shape[0]
    a = rows[:, :, 0:nh, :].transpose(0, 2, 1, 3).reshape(FOX_HEADS, S).T
    return jnp.pad(a, ((0, 0), (0, LANES - FOX_HEADS)))


def _fox_fwd(qf, ka, va, *, tq, nh=4):
    S = ka.shape[0]
    nq = S // tq
    tk = tq
    ng = FOX_HEADS // nh

    def body(q_ref, k_ref, v_ref, o_ref, lse_ref, *scratch):
        p_scr, m_scr, acc_scr = scratch[0:nh], scratch[nh:2 * nh], scratch[2 * nh:3 * nh]
        qi = pl.program_id(1)
        lane = lax.broadcasted_iota(jnp.int32, (tq, LANES), 1)
        half0 = lane < FOX_HEAD_DIM
        rr = lax.broadcasted_iota(jnp.int32, (tq, tk), 0)
        cc = lax.broadcasted_iota(jnp.int32, (tq, tk), 1)
        sls = [slice(hh * LANES, (hh + 1) * LANES) for hh in range(nh)]
        qs = [q_ref[:, sl] for sl in sls]

        for hh in range(nh):
            p_scr[hh][...] = jnp.zeros_like(p_scr[hh])
            m_scr[hh][...] = jnp.full(m_scr[hh].shape, -jnp.inf, F32)
            acc_scr[hh][...] = jnp.zeros_like(acc_scr[hh])

        def chunk(ki, masked):
            k0 = pl.multiple_of(ki * tk, tk)
            kp = pl.multiple_of(jnp.maximum(ki - 1, 0) * tk, tk)
            for hh in range(nh):
                m = m_scr[hh][...]
                s = _dot_nt(qs[hh], k_ref[pl.ds(k0, tk), sls[hh]])
                pv = _dot(p_scr[hh][...], v_ref[pl.ds(kp, tk), sls[hh]])
                if masked:
                    s = jnp.where(cc <= rr, s, -jnp.inf)
                m_new = jnp.maximum(m, jnp.max(s, axis=1, keepdims=True))
                p_scr[hh][...] = jnp.exp(s - jnp.tile(m_new, (1, tk // LANES))).astype(BF16)
                acc_scr[hh][...] = (acc_scr[hh][...] + pv) * jnp.exp(m - m_new)
                m_scr[hh][...] = m_new

        def trip(ki, c):
            chunk(ki, False)
            return c

        lax.fori_loop(0, qi, trip, 0)
        chunk(qi, True)
        kq = pl.multiple_of(qi * tk, tk)
        outs = []
        lse_cols = jnp.zeros((tq, LANES), F32)
        for hh in range(nh):
            m = m_scr[hh][...]
            acc = acc_scr[hh][...] + _dot(p_scr[hh][...], v_ref[pl.ds(kq, tk), sls[hh]])
            l = jnp.sum(jnp.where(lane == AUG_A, acc, 0.0), axis=1, keepdims=True)
            outs.append(acc / l)
            lse_cols = lse_cols + jnp.where(lane == hh, m + jnp.log(l), 0.0)
        for pr in range(nh // 2):
            o_ref[:, pr * LANES:(pr + 1) * LANES] = jnp.where(
                half0, outs[2 * pr], pltpu.roll(outs[2 * pr + 1], FOX_HEAD_DIM, 1))
        lse_ref[0, 0] = lse_cols.T[0:8, :]

    return pl.pallas_call(
        body, name="fox_fwd", grid=(ng, nq),
        in_specs=[pl.BlockSpec((tq, nh * LANES), lambda g, qi: (qi, g)),
                  pl.BlockSpec((S, nh * LANES), lambda g, qi: (0, g), pipeline_mode=pl.Buffered(1)),
                  pl.BlockSpec((S, nh * LANES), lambda g, qi: (0, g), pipeline_mode=pl.Buffered(1))],
        out_specs=[pl.BlockSpec((tq, nh * FOX_HEAD_DIM), lambda g, qi: (qi, g)),
                   pl.BlockSpec((1, 1, 8, tq), lambda g, qi: (g, qi, 0, 0))],
        out_shape=[jax.ShapeDtypeStruct((S, D_MAIN), F32),
                   jax.ShapeDtypeStruct((ng, nq, 8, tq), F32)],
        scratch_shapes=([pltpu.VMEM((tq, tk), BF16)] * nh + [pltpu.VMEM((tq, LANES), F32)] * nh
                        + [pltpu.VMEM((tq, LANES), F32)] * nh),
        compiler_params=_params(2),
    )(qf, ka, va)


def _rows_for_pairs(a16, nt, tt):
    a = a16.reshape(8, 2, nt, tt).transpose(0, 2, 1, 3)
    return jnp.pad(a, ((0, 0), (0, 0), (0, 6), (0, 0)))


def _fox_bwd(qf, ka, va, do_aug, lse_rows, d_rows, *, tq):
    S = ka.shape[0]
    nq = S // tq
    tk = tq

    def body(k_ref, v_ref, q_ref, do_ref, lser_ref, dr_ref, dq_ref, dk_ref, dv_ref, dck_ref, dk_scr, dv_scr):
        kj = pl.program_id(1)

        @pl.when(kj == 0)
        def _():
            dq_ref[...] = jnp.zeros_like(dq_ref)

        lane = lax.broadcasted_iota(jnp.int32, (tk, LANES), 1)
        half0 = lane < FOX_HEAD_DIM
        rr = lax.broadcasted_iota(jnp.int32, (tk, tq), 0)
        cc = lax.broadcasted_iota(jnp.int32, (tk, tq), 1)
        sls = [slice(hh * LANES, (hh + 1) * LANES) for hh in range(2)]
        kts = [k_ref[:, sl] for sl in sls]
        vts = [v_ref[:, sl] for sl in sls]

        dk_scr[...] = jnp.zeros_like(dk_scr)
        dv_scr[...] = jnp.zeros_like(dv_scr)

        def chunk(qi, masked):
            q0 = pl.multiple_of(qi * tq, tq)
            for hh in range(2):
                qc = q_ref[pl.ds(q0, tq), sls[hh]]
                doc = do_ref[pl.ds(q0, tq), sls[hh]]
                lse = lser_ref[0, qi][hh:hh + 1, :]
                dd = dr_ref[0, qi][hh:hh + 1, :]
                pt = jnp.exp(_dot_nt(kts[hh], qc) - lse)
                if masked:
                    pt = jnp.where(rr <= cc, pt, 0.0)
                dsb = (pt * (_dot_nt(vts[hh], doc) - dd)).astype(BF16)
                dv_scr[hh] += _dot(pt.astype(BF16), doc)
                dk_scr[hh] += _dot(dsb, qc)
                dq_ref[pl.ds(q0, tq), sls[hh]] += _dot_tn(dsb, kts[hh])

        def trip(qi, c):
            chunk(qi, False)
            return c

        chunk(kj, True)
        lax.fori_loop(kj + 1, nq, trip, 0)
        dk0, dk1 = dk_scr[0], dk_scr[1]
        dv0, dv1 = dv_scr[0], dv_scr[1]
        dk_ref[...] = jnp.where(half0, dk0, pltpu.roll(dk1, FOX_HEAD_DIM, 1)).astype(BF16)
        dv_ref[...] = jnp.where(half0, dv0, pltpu.roll(dv1, FOX_HEAD_DIM, 1)).astype(BF16)
        c0 = jnp.sum(jnp.where(lane == AUG_B, dk0, 0.0), axis=1, keepdims=True)
        c1 = jnp.sum(jnp.where(lane == AUG_B, dk1, 0.0), axis=1, keepdims=True)
        dck_cols = jnp.where(lane == 0, c0, 0.0) + jnp.where(lane == 1, c1, 0.0)
        dck_ref[0, 0] = dck_cols.T[0:8, :]

    return pl.pallas_call(
        body, name="fox_bwd", grid=(8, nq),
        in_specs=[pl.BlockSpec((tk, 2 * LANES), lambda hp, kj: (kj, hp)),
                  pl.BlockSpec((tk, 2 * LANES), lambda hp, kj: (kj, hp)),
                  pl.BlockSpec((S, 2 * LANES), lambda hp, kj: (0, hp)),
                  pl.BlockSpec((S, 2 * LANES), lambda hp, kj: (0, hp)),
                  pl.BlockSpec((1, nq, 8, tq), lambda hp, kj: (hp, 0, 0, 0)),
                  pl.BlockSpec((1, nq, 8, tq), lambda hp, kj: (hp, 0, 0, 0))],
        out_specs=[pl.BlockSpec((S, 2 * LANES), lambda hp, kj: (0, hp)),
                   pl.BlockSpec((tk, LANES), lambda hp, kj: (kj, hp)),
                   pl.BlockSpec((tk, LANES), lambda hp, kj: (kj, hp)),
                   pl.BlockSpec((1, 1, 8, tk), lambda hp, kj: (hp, kj, 0, 0))],
        out_shape=[jax.ShapeDtypeStruct((S, AUG_W), F32),
                   jax.ShapeDtypeStruct((S, D_MAIN), BF16),
                   jax.ShapeDtypeStruct((S, D_MAIN), BF16),
                   jax.ShapeDtypeStruct((8, nq, 8, tk), F32)],
        scratch_shapes=[pltpu.VMEM((2, tk, LANES), F32), pltpu.VMEM((2, tk, LANES), F32)],
        compiler_params=_params(2),
    )(ka, va, qf, do_aug, lse_rows, d_rows)


def _adamw(w, g, m, v, *, name):
    Rr, C = w.shape
    tr = 256 if Rr % 256 == 0 else Rr
    c1 = 1.0 / (1.0 - ADAM_B1 ** ADAM_STEP)
    c2 = 1.0 / (1.0 - ADAM_B2 ** ADAM_STEP)

    def body(w_ref, g_ref, m_ref, v_ref, d_ref, nm_ref, nv_ref):
        gv = g_ref[...]
        nm = ADAM_B1 * m_ref[...] + (1.0 - ADAM_B1) * gv
        nv = ADAM_B2 * v_ref[...] + (1.0 - ADAM_B2) * (gv * gv)
        d_ref[...] = -ADAM_LR * ((nm * c1) / (jnp.sqrt(nv * c2) + ADAM_EPS) + ADAM_WD * w_ref[...])
        nm_ref[...] = nm
        nv_ref[...] = nv

    spec = pl.BlockSpec((tr, C), lambda i: (i, 0))
    sds = jax.ShapeDtypeStruct((Rr, C), F32)
    return pl.pallas_call(
        body, name=name, grid=(Rr // tr,),
        in_specs=[spec] * 4, out_specs=[spec] * 3, out_shape=[sds] * 3,
        compiler_params=_params(),
    )(w, g, m, v)


_ANY = pl.BlockSpec(memory_space=pl.ANY)
_MESH = pl.DeviceIdType.MESH


def _place():
    x, y, c = lax.axis_index("x"), lax.axis_index("y"), lax.axis_index("c")
    return x, y, c


def _gather_steps(p_refs, out_refs, send_sems, recv_sems):
    x, y, c = _place()
    sib = (x, y, 1 - c)
    chips = [(1 - x, y), (x, 1 - y), (1 - x, 1 - y)]
    idx = [2 * chip[0] + chip[1] for chip in chips]
    me = 2 * x + y
    na = len(p_refs)

    def copy(a, k, chip_idx, half, to, src=None):
        dst = out_refs[a].at[chip_idx, half]
        return pltpu.make_async_remote_copy(
            src_ref=dst if src is None else src, dst_ref=dst,
            send_sem=send_sems.at[6 * a + k], recv_sem=recv_sems.at[6 * a + k],
            device_id=to, device_id_type=_MESH)

    first = [copy(a, j, me, c, (*chips[j], c), src=p_refs[a].at[c]) for a in range(na) for j in range(3)]
    passed = [copy(a, 3 + j, idx[j], c, sib) for a in range(na) for j in range(3)]

    def start():
        for cp in first:
            cp.start()

    def forward():
        for a in range(na):
            for j in range(3):
                copy(a, j, idx[j], c, sib).wait_recv()
                passed[3 * a + j].start()

    def finish():
        for a in range(na):
            for j in range(3):
                copy(a, 3 + j, idx[j], 1 - c, sib).wait_recv()
        for cp in first + passed:
            cp.wait_send()

    return start, forward, finish


def _sems(n):
    return [pltpu.SemaphoreType.DMA((n,)), pltpu.SemaphoreType.DMA((n,))]


def _gather_sems(na):
    return _sems(6 * na)


def _scatter_sems(na):
    return _sems(3 * na)


def _pair_sems(na):
    return _sems(N_CHIPS * na)


def _gathered_shape(pack):
    return jax.ShapeDtypeStruct((N_CHIPS,) + pack.shape, pack.dtype)


def _from_sibling_shape(gpack):
    return jax.ShapeDtypeStruct((N_CHIPS,) + gpack.shape[2:], gpack.dtype)


def _all_gather_shards(packs):
    na = len(packs)

    def body(*refs):
        for step in _gather_steps(refs[0:na], refs[na:2 * na], refs[2 * na], refs[2 * na + 1]):
            step()

    return pl.pallas_call(
        body, name="all_gather_shards",
        in_specs=[_ANY] * na, out_specs=[_ANY] * na, out_shape=[_gathered_shape(p) for p in packs],
        scratch_shapes=_gather_sems(na),
    )(*packs)


def _pair_send_steps(g_refs, out_refs, send_sem, recv_sem):
    x, y, c = _place()
    cps = [pltpu.make_async_remote_copy(
        src_ref=g_refs[a].at[j, 1 - c], dst_ref=out_refs[a].at[j],
        send_sem=send_sem.at[N_CHIPS * a + j], recv_sem=recv_sem.at[N_CHIPS * a + j],
        device_id=(x, y, 1 - c), device_id_type=_MESH) for a in range(len(g_refs)) for j in range(N_CHIPS)]

    def start():
        for cp in cps:
            cp.start()

    def finish():
        for cp in cps:
            cp.wait_recv()
        for cp in cps:
            cp.wait_send()

    return start, finish


def _send_half_to_sibling(gpacks, tag):
    na = len(gpacks)

    def body(*refs):
        for step in _pair_send_steps(refs[0:na], refs[na:2 * na], refs[2 * na], refs[2 * na + 1]):
            step()

    return pl.pallas_call(
        body, name=f"pair_send{tag}",
        in_specs=[_ANY] * na, out_specs=[_ANY] * na, out_shape=[_from_sibling_shape(g) for g in gpacks],
        scratch_shapes=_pair_sems(na),
    )(*gpacks)


def _pair_sum(gpack, recv, c_arr, tag, *, tr=PACK_TILE):
    rows, lanes = recv.shape[1:]
    assert rows % tr == 0

    def body(c_ref, a_ref, b_ref, o_ref):
        o_ref[...] = (a_ref[...] + b_ref[...]).astype(BF16)

    grid_spec = pltpu.PrefetchScalarGridSpec(
        num_scalar_prefetch=1, grid=(N_CHIPS, rows // tr),
        in_specs=[pl.BlockSpec((None, None, tr, lanes), lambda j, i, c_ref: (j, c_ref[0], i, 0)),
                  pl.BlockSpec((None, tr, lanes), lambda j, i, c_ref: (j, i, 0))],
        out_specs=pl.BlockSpec((None, tr, lanes), lambda j, i, c_ref: (j, i, 0)))
    return pl.pallas_call(
        body, name=f"pair_sum{tag}", grid_spec=grid_spec,
        out_shape=jax.ShapeDtypeStruct((N_CHIPS, rows, lanes), BF16),
        compiler_params=_params(2),
    )(c_arr, gpack, recv)


def _scatter_steps(p_refs, out_refs, send_sems, recv_sems):
    x, y, c = _place()
    chips = [(1 - x, y), (x, 1 - y), (1 - x, 1 - y)]
    me = 2 * x + y
    cps = [pltpu.make_async_remote_copy(
        src_ref=p_refs[a].at[2 * chip[0] + chip[1]], dst_ref=out_refs[a].at[me],
        send_sem=send_sems.at[3 * a + j], recv_sem=recv_sems.at[3 * a + j],
        device_id=(*chip, c), device_id_type=_MESH) for a in range(len(p_refs)) for j, chip in enumerate(chips)]

    def start():
        for cp in cps:
            cp.start()

    def finish():
        for cp in cps:
            cp.wait_recv()
        for cp in cps:
            cp.wait_send()

    return start, finish


def _share_steps(row_ref, out_ref, send_sems, recv_sems):
    x, y, c = _place()
    mine = 4 * x + 2 * y + c
    cps = []
    for k in range(1, 8):
        fx, fy, fc = (k >> 2) & 1, (k >> 1) & 1, k & 1
        peer = (x + fx - 2 * x * fx, y + fy - 2 * y * fy, c + fc - 2 * c * fc)
        cps.append(pltpu.make_async_remote_copy(
            src_ref=row_ref, dst_ref=out_ref.at[mine], send_sem=send_sems.at[k - 1], recv_sem=recv_sems.at[k - 1],
            device_id=peer, device_id_type=_MESH))

    def start():
        for cp in cps:
            cp.start()

    def finish():
        for cp in cps:
            cp.wait_recv()
        for cp in cps:
            cp.wait_send()

    return start, finish


def _scatter_pieces(psums, tag, share=None):
    na = len(psums)
    ns = 0 if share is None else 1

    def body(*refs):
        n_in = na + ns
        steps = [_scatter_steps(refs[0:na], refs[n_in:n_in + na], refs[2 * n_in], refs[2 * n_in + 1])]
        if ns:
            steps.append(_share_steps(refs[na], refs[n_in + na], refs[2 * n_in + 2], refs[2 * n_in + 3]))
        for phase in range(2):
            for st in steps:
                st[phase]()

    out_shape = [jax.ShapeDtypeStruct(p.shape, p.dtype) for p in psums]
    if ns:
        out_shape.append(jax.ShapeDtypeStruct((8,) + share.shape, share.dtype))
    return pl.pallas_call(
        body, name=f"scatter_pieces{tag}",
        in_specs=[_ANY] * (na + ns), out_specs=[_ANY] * (na + ns), out_shape=out_shape,
        scratch_shapes=_scatter_sems(na) + (_sems(7) if ns else []),
    )(*psums, *([share] if ns else []))


def _sum_pieces(pieces, tag, *, tr=PACK_TILE):
    rows, lanes = pieces.shape[1:]
    assert rows % tr == 0

    def body(p_ref, o_ref):
        acc = p_ref[0].astype(F32) + p_ref[1].astype(F32)
        acc = acc + p_ref[2].astype(F32)
        o_ref[...] = acc + p_ref[3].astype(F32)

    return pl.pallas_call(
        body, name=f"sum_pieces{tag}", grid=(rows // tr,),
        in_specs=[pl.BlockSpec((N_CHIPS, tr, lanes), lambda i: (0, i, 0))],
        out_specs=pl.BlockSpec((tr, lanes), lambda i: (i, 0)),
        out_shape=jax.ShapeDtypeStruct((rows, lanes), F32),
        compiler_params=_params(),
    )(pieces)


def _exchange_halves(totals):
    n = len(totals)

    def body(*refs):
        t_refs, out_refs, send_sem, recv_sem = refs[:n], refs[n:2 * n], refs[2 * n], refs[2 * n + 1]
        x, y, c = _place()
        cps = [pltpu.make_async_remote_copy(
            src_ref=t_refs[i], dst_ref=out_refs[i].at[c], send_sem=send_sem.at[i], recv_sem=recv_sem.at[i],
            device_id=(x, y, 1 - c), device_id_type=_MESH) for i in range(n)]
        for cp in cps:
            cp.start()
        for cp in cps:
            cp.wait_recv()
        for cp in cps:
            cp.wait_send()

    return pl.pallas_call(
        body, name="exchange_halves",
        in_specs=[_ANY] * n, out_specs=[_ANY] * n,
        out_shape=[jax.ShapeDtypeStruct((2,) + t.shape, F32) for t in totals],
        scratch_shapes=[pltpu.SemaphoreType.DMA((n,)), pltpu.SemaphoreType.DMA((n,))],
    )(*totals)


def _pad_rows(a, rows):
    return jnp.pad(a, ((0, rows - a.shape[0]), (0, 0)))


def _pack_weight_shards(w_in, w_mem_kv, w_out, pool_w, w_kv_shared, pool_scale):
    ps_bits = lax.bitcast_convert_type(pool_scale.reshape(-1), BF16).reshape(1, -1)
    ps_row = jnp.pad(ps_bits, ((0, 0), (0, 1024 - ps_bits.shape[1])))

    def common(l):
        return [w_mem_kv[l].astype(BF16).reshape(ROWS_W_MKV, 1024),
                w_out[l].astype(BF16).reshape(ROWS_W_OUT, 1024)]

    p0 = common(0) + [pool_w.astype(BF16).reshape(ROWS_POOL_W, 1024), _pad_rows(ps_row, ROWS_SMALL),
                      jnp.zeros((PACK0_ROWS - OFF_LN_G, 1024), BF16)]
    p1 = common(1) + [_pad_rows(w_kv_shared.astype(BF16).reshape(KV_SHARD, 1024), ROWS_W_KV),
                      jnp.zeros((PACK1_ROWS - OFF_BF, 1024), BF16)]
    w_in_halves = w_in.astype(BF16).reshape(2, 2, D_MODEL // 2, W_IN_SHARD)
    return ([jnp.concatenate(p0, axis=0).reshape(2, PACK0_ROWS // 2, 1024), w_in_halves[0]],
            [jnp.concatenate(p1, axis=0).reshape(2, PACK1_ROWS // 2, 1024), w_in_halves[1]])


def _unpack_w_in(g_in):
    return g_in.reshape(N_CHIPS, D_MODEL, W_IN_SHARD)


def _unpack_common(g):
    w_mkv = g[:, OFF_W_MKV:OFF_W_MKV + ROWS_W_MKV].reshape(D_MODEL, 2 * D_MEM)
    w_out = g[:, OFF_W_OUT:OFF_W_OUT + ROWS_W_OUT].reshape(D_MIX, D_MODEL)
    return w_mkv, w_out


def _unpack_weights0(g):
    pool_w = g[:, OFF_POOL_W:OFF_POOL_W + ROWS_POOL_W].reshape(4, 4, POOL_GROUP // 4, POOL_GROUP)
    pool_w = pool_w.transpose(1, 0, 2, 3).reshape(4, POOL_GROUP, POOL_GROUP)
    ps_bits = g[:, OFF_POOL_S, 0:512].reshape(4, 256, 2)
    pool_scale = lax.bitcast_convert_type(ps_bits, F32).reshape(1, D_MAIN)
    return _unpack_common(g) + (pool_w, pool_scale)


def _unpack_weights1(g):
    w_kv = g[:, OFF_W_KV:OFF_W_KV + KV_SHARD].reshape(4, D_MODEL, KV_SHARD)
    w_kv = w_kv.transpose(1, 0, 2).reshape(D_MODEL, KV_COLS)
    return _unpack_common(g) + (w_kv,)


def _replicated_rows(a):
    a = _pad_rows(a, ROWS_SMALL)
    return jnp.broadcast_to(a[None], (4,) + a.shape)


def _pack_common_grads(g_w_mkv, g_w_out):
    return [g_w_mkv.reshape(4, ROWS_W_MKV, 1024), g_w_out.reshape(4, ROWS_W_OUT, 1024)]


def _w_in_grad_halves(g_w_in):
    return g_w_in.reshape(N_CHIPS, 2, D_MODEL // 2, W_IN_SHARD)


def _pack_grads0(g_w_mkv, g_w_out, g_pool_w, g_pool_scale, g_ln_g, g_ln_b):
    parts = _pack_common_grads(g_w_mkv, g_w_out) + [
        g_pool_w.reshape(4, 4, POOL_GROUP // 4, POOL_GROUP).transpose(1, 0, 2, 3).reshape(4, ROWS_POOL_W, 1024),
        jnp.pad(g_pool_scale.reshape(4, 1, 256), ((0, 0), (0, ROWS_SMALL - 1), (0, 1024 - 256))),
        _replicated_rows(g_ln_g), _replicated_rows(g_ln_b),
        jnp.zeros((4, PACK0_ROWS - OFF_LN_B - ROWS_SMALL, 1024), F32),
    ]
    return jnp.concatenate(parts, axis=1).reshape(4, 2, PACK0_ROWS // 2, 1024)


def _pack_grads1(g_w_mkv, g_w_out, g_w_kv, g_bf):
    parts = _pack_common_grads(g_w_mkv, g_w_out) + [
        jnp.pad(g_w_kv.reshape(D_MODEL, 4, KV_SHARD).transpose(1, 0, 2).reshape(4, KV_SHARD, 1024),
                ((0, 0), (0, ROWS_W_KV - KV_SHARD), (0, 0))),
        _replicated_rows(jnp.pad(g_bf.reshape(1, -1), ((0, 0), (0, 1024 - g_bf.shape[0])))),
        jnp.zeros((4, PACK1_ROWS - OFF_BF - ROWS_SMALL, 1024), F32),
    ]
    return jnp.concatenate(parts, axis=1).reshape(4, 2, PACK1_ROWS // 2, 1024)


def _local_step(x, mem, target, w0, w1, ln_g, ln_b, b_forget, *, tm=256, tq=512, dist=None):
    S = x.shape[0]
    nq = S // tq
    g_rows = [ln_g[l:l + 1] for l in range(2)]
    b_rows = [ln_b[l:l + 1] for l in range(2)]
    bf_row = jnp.pad(b_forget.reshape(1, -1), ((0, 0), (0, LANES - FOX_HEADS)))

    def own_slot(gathered, pack):
        return lax.dynamic_update_slice(gathered, pack[None], (dist["me"], 0, 0, 0))

    if dist is None:
        w_in0, w_mkv0, w_out0, pool_w, pool_scale = w0
        h0 = _linear_fwd(x, w_in0, tm=tm, name="in_proj0")
    else:
        w_in0, pack0 = w0
        h0, g0 = _linear_fwd(x, w_in0, tm=tm, name="in_proj0", gather=[pack0])
        w_mkv0, w_out0, pool_w, pool_scale = _unpack_weights0(
            own_slot(g0, pack0).reshape(N_CHIPS, PACK0_ROWS, 1024))
    mkv0 = _linear_fwd(mem, w_mkv0, tm=N_MEM, name="mem_kv0")
    f0 = _mix("pool", "fwd", h=h0, xres=x, mkv=mkv0, w_out=w_out0, ln_g=g_rows[0], ln_b=b_rows[0],
              pool_w=pool_w, pool_scale=pool_scale, gather=() if dist is None else w1, tm=tm)
    z0, x1 = f0["z"], f0["xout"]
    if dist is not None:
        g1, g1_in = [own_slot(f0[f"gathered{a}"], w1[a]) for a in range(2)]
        w1 = (_unpack_w_in(g1_in),) + _unpack_weights1(g1.reshape(N_CHIPS, PACK1_ROWS, 1024))
    w_in1, w_mkv1, w_out1, w_kv = w1
    w_in, w_out = [w_in0, w_in1], [w_out0, w_out1]
    w_kvp = jnp.pad(w_kv, ((0, 0), (0, LANES - FOX_HEADS)))
    mkv = [mkv0, _linear_fwd(mem, w_mkv1, tm=N_MEM, name="mem_kv1")]
    ka, va, fl, cum = _kv_proj(x1, w_kvp, bf_row, tm=tm)
    h1, qf = _linear_fwd(x1, w_in[1], tm=tm, name="in_proj1", q_cum=cum)
    ymain1, lse_rows = _fox_fwd(qf, ka, va, tq=tq)

    b1 = _mix("fox", "bwd", h=h1, xres=x1, mkv=mkv[1], w_out=w_out[1], ln_g=g_rows[1], ln_b=b_rows[1],
              ymain=ymain1, target=target, tm=tm)
    lse16 = lse_rows[:, :, 0:FOX_HEADS // lse_rows.shape[0], :].transpose(0, 2, 1, 3).reshape(FOX_HEADS, S)
    dq_aug, dk, dv, dck_rows = _fox_bwd(qf, ka, va, b1["dmain"], _rows_for_pairs(lse16, nq, tq),
                                        _rows_for_pairs(b1["drow"], nq, tq), tq=tq)
    du1, df, dbf = _gate_bwd(dq_aug, _cols_of_rows(dck_rows, S), fl, tm=tm)

    dx1a, dw_in1 = _lin_bwd(x1, [du1, b1["drest"]], w_in[1], [(b1["dz"], ALPHA)], tm=tm, name="in_proj1_bwd")
    dx1, dw_kvp = _lin_bwd(x1, [dk, dv, df], w_kvp, [(dx1a, 1.0)], tm=tm, name="kv_proj_bwd")

    dw_mkv1 = _wgrad(mem, b1["dmkv"], name="mem_kv1_bwd")
    g_w_kv, g_bf = dw_kvp[:, 0:KV_COLS], dbf[0, 0:FOX_HEADS]

    gpacks1, psums1 = (), ()
    if dist is not None:
        gpacks1 = [_pack_grads1(dw_mkv1, b1["dw_out"], g_w_kv, g_bf), _w_in_grad_halves(dw_in1)]
    b0 = _mix("pool", "bwd", h=h0, mkv=mkv[0], w_out=w_out[0], ln_g=g_rows[0],
              pool_w=pool_w, pool_scale=pool_scale, z=z0, dy=dx1, pair_send=gpacks1, tm=tm)
    if dist is not None:
        psums1 = [_pair_sum(g, b0[f"from_sibling{a}"], dist["c_arr"], f"1{'ab'[a]}") for a, g in enumerate(gpacks1)]
    outs = _lin_bwd(x, [b0["dmain"], b0["drest"]], w_in[0], [(b0["dz"], ALPHA)], tm=tm, name="in_proj0_bwd",
                    scatter=psums1)
    dx, dw_in0 = outs[0], outs[1]
    dw_mkv0 = _wgrad(mem, b0["dmkv"], name="mem_kv0_bwd")
    g_ln_g = jnp.concatenate([b0["dln_g"], b1["dln_g"]], axis=0)
    g_ln_b = jnp.concatenate([b0["dln_b"], b1["dln_b"]], axis=0)

    if dist is None:
        grads = dict(w_in=[dw_in0, dw_in1], w_mem_kv=[dw_mkv0, dw_mkv1], w_out=[b0["dw_out"], b1["dw_out"]],
                     ln_g=g_ln_g, ln_b=g_ln_b, pool_w=b0["dpool_w"], pool_scale=b0["dpool_scale"],
                     w_kv=g_w_kv, b_forget=g_bf)
        return b1["loss"], dx, grads

    me, my_c = dist["me"], dist["my_c"]

    def with_own(pieces, psum):
        own = lax.dynamic_slice(psum, (me, 0, 0), (1,) + psum.shape[1:])
        return lax.dynamic_update_slice(pieces, own, (me, 0, 0))

    totals1 = [_sum_pieces(with_own(outs[2 + a], p), f"1{'ab'[a]}") for a, p in enumerate(psums1)]
    gpacks0 = [_pack_grads0(dw_mkv0, b0["dw_out"], b0["dpool_w"], b0["dpool_scale"], g_ln_g, g_ln_b),
               _w_in_grad_halves(dw_in0)]
    sib0 = _send_half_to_sibling(gpacks0, 0)
    psums0 = [_pair_sum(g, sib0[a], dist["c_arr"], f"0{'ab'[a]}") for a, g in enumerate(gpacks0)]
    loss_row = jnp.broadcast_to(0.5 / D_MODEL * jnp.sum(b1["loss"]), (8, LANES))
    pieces0 = _scatter_pieces(psums0, 0, share=loss_row)
    losses = lax.dynamic_update_slice(pieces0[2], loss_row[None], (2 * me + my_c, 0, 0))
    loss = jnp.sum(losses[:, 0, 0])
    totals0 = [_sum_pieces(with_own(pieces0[a], p), f"0{'ab'[a]}") for a, p in enumerate(psums0)]
    totals = totals0 + totals1
    halves = _exchange_halves(totals)
    full = [lax.dynamic_update_slice(h, t[None], (my_c, 0, 0)) for h, t in zip(halves, totals)]
    shard0, shard1 = full[0].reshape(PACK0_ROWS, 1024), full[2].reshape(PACK1_ROWS, 1024)
    g_w_in = jnp.stack([full[1].reshape(D_MODEL, W_IN_SHARD), full[3].reshape(D_MODEL, W_IN_SHARD)])
    return loss, dx, shard0, shard1, g_w_in


def kernel(x, mem, w_in, w_mem_kv, w_out, ln_g, ln_b, pool_w, pool_scale, w_kv_shared, b_forget, loss_target, m_w_in, m_w_mem_kv, m_w_out, m_ln_g, m_ln_b, m_pool_w, m_pool_scale, m_w_kv_shared, m_b_forget, v_w_in, v_w_mem_kv, v_w_out, v_ln_g, v_ln_b, v_pool_w, v_pool_scale, v_w_kv_shared, v_b_forget):
    dist = dict(c_arr=lax.axis_index("c").astype(jnp.int32).reshape(1),
                me=2 * lax.axis_index("x") + lax.axis_index("y"), my_c=lax.axis_index("c"))

    wpacks0, wpacks1 = _pack_weight_shards(w_in, w_mem_kv, w_out, pool_w, w_kv_shared, pool_scale)
    g0_in = lax.dynamic_update_slice(_all_gather_shards([wpacks0[1]])[0], wpacks0[1][None], (dist["me"], 0, 0, 0))
    w0 = (_unpack_w_in(g0_in), wpacks0[0])

    loss, dx, shard0, shard1, g_w_in = _local_step(x[0], mem[0], loss_target[0], w0, wpacks1, ln_g, ln_b,
                                                   b_forget, dist=dist)

    def per_layer(off, rows, shape):
        return jnp.concatenate([shard0[off:off + rows], shard1[off:off + rows]], axis=0).reshape(shape)

    g_w_mkv = per_layer(OFF_W_MKV, ROWS_W_MKV, w_mem_kv.shape)
    g_w_out = per_layer(OFF_W_OUT, ROWS_W_OUT, w_out.shape)
    g_pool_w = shard0[OFF_POOL_W:OFF_POOL_W + ROWS_POOL_W].reshape(pool_w.shape)
    g_w_kv = shard1[OFF_W_KV:OFF_W_KV + KV_SHARD].reshape(w_kv_shared.shape)
    g_pool_scale = shard0[OFF_POOL_S:OFF_POOL_S + 1, 0:256].reshape(pool_scale.shape)
    g_ln_g = shard0[OFF_LN_G:OFF_LN_G + 2]
    g_ln_b = shard0[OFF_LN_B:OFF_LN_B + 2]
    g_bf = shard1[OFF_BF, 0:FOX_HEADS]

    names = ["w_in", "w_mem_kv", "w_out", "ln_g", "ln_b", "pool_w", "pool_scale", "w_kv_shared", "b_forget"]
    ws = [w_in, w_mem_kv, w_out, ln_g, ln_b, pool_w, pool_scale, w_kv_shared, b_forget]
    gs = [g_w_in, g_w_mkv, g_w_out, g_ln_g, g_ln_b, g_pool_w, g_pool_scale, g_w_kv, g_bf]
    ms = [m_w_in, m_w_mem_kv, m_w_out, m_ln_g, m_ln_b, m_pool_w, m_pool_scale, m_w_kv_shared, m_b_forget]
    vs = [v_w_in, v_w_mem_kv, v_w_out, v_ln_g, v_ln_b, v_pool_w, v_pool_scale, v_w_kv_shared, v_b_forget]
    deltas, new_ms, new_vs = [], [], []
    for nm, w, gg, mm, vv in zip(names, ws, gs, ms, vs):
        two_d = (-1, w.shape[-1])
        d, nmm, nvv = _adamw(w.reshape(two_d), gg.reshape(two_d), mm.reshape(two_d), vv.reshape(two_d),
                             name=f"adamw_{nm}")
        deltas.append(d.reshape(w.shape))
        new_ms.append(nmm.reshape(w.shape))
        new_vs.append(nvv.reshape(w.shape))

    return (loss, dx[None], *gs, *deltas, *new_ms, *new_vs)
```

```python
import jax
import jax.numpy as jnp
from jax import lax
from jax.experimental import pallas as pl
from jax.experimental.pallas import tpu as pltpu

F32 = jnp.float32
BF16 = jnp.bfloat16

D_MODEL = 1024
D_MAIN = 1024
D_MEM = 512
D_MIX = D_MAIN + D_MEM
D_IN = 2 * D_MIX
N_MEM = 256
MEM_HEADS = 4
MEM_HEAD_DIM = 128
FOX_HEADS = 16
FOX_HEAD_DIM = 64
FOX_SCALE = 0.125
POOL_WINDOWS = (2, 4, 8, 16)
POOL_GROUP = 256
POOL_HALO = 16
ALPHA = 4.0 ** 0.25
LN_EPS = 1e-5
LANES = 128
N_CHIPS = 4

ADAM_LR = 0.001
ADAM_B1 = 0.9
ADAM_B2 = 0.999
ADAM_EPS = 1e-08
ADAM_WD = 0.01
ADAM_STEP = 10

VMEM_LIMIT = 56 * 1024 * 1024

ROWS_W_MKV = (D_MODEL // N_CHIPS) * 2 * D_MEM // 1024
ROWS_W_OUT = (D_MIX // N_CHIPS) * D_MODEL // 1024
ROWS_POOL_W = 4 * (POOL_GROUP // N_CHIPS) * POOL_GROUP // 1024
KV_COLS = 2 * D_MAIN + FOX_HEADS
KV_SHARD = KV_COLS // N_CHIPS
ROWS_W_KV = 528
ROWS_SMALL = 16
OFF_W_MKV = 0
OFF_W_OUT = OFF_W_MKV + ROWS_W_MKV
OFF_TAIL = OFF_W_OUT + ROWS_W_OUT
OFF_POOL_W = OFF_TAIL
OFF_POOL_S = OFF_POOL_W + ROWS_POOL_W
OFF_LN_G = OFF_POOL_S + ROWS_SMALL
OFF_LN_B = OFF_LN_G + ROWS_SMALL
PACK0_ROWS = 768
OFF_W_KV = OFF_TAIL
OFF_BF = OFF_W_KV + ROWS_W_KV
PACK1_ROWS = 1280
PACK_TILE = 128
W_IN_SHARD = D_IN // N_CHIPS


def _dot(a, b):
    return jnp.dot(a, b, preferred_element_type=F32)


def _dot_nt(a, b):
    return lax.dot_general(a, b, (((1,), (1,)), ((), ())), preferred_element_type=F32)


def _dot_tn(a, b):
    return lax.dot_general(a, b, (((0,), (0,)), ((), ())), preferred_element_type=F32)


def _params(n_axes=1):
    return pltpu.CompilerParams(dimension_semantics=("arbitrary",) * n_axes,
                                vmem_limit_bytes=VMEM_LIMIT)


def _const_spec(shape):
    zeros = (0,) * len(shape)
    return pl.BlockSpec(shape, lambda *_: zeros, pipeline_mode=pl.Buffered(1))


def _split3(x):
    hi = x.astype(BF16)
    r = x - hi.astype(F32)
    mid = r.astype(BF16)
    lo = (r - mid.astype(F32)).astype(BF16)
    return hi, mid, lo


def _cols_and_chunk(w):
    if w.ndim == 3:
        return N_CHIPS * w.shape[2], 256
    return w.shape[1], (512 if w.shape[1] % 512 == 0 else LANES)


def _w_cols(w_ref, n0, nc):
    if len(w_ref.shape) == 3:
        per = w_ref.shape[2]
        assert n0 // per == (n0 + nc - 1) // per
        return w_ref.at[n0 // per, :, n0 % per:n0 % per + nc]
    return w_ref.at[:, n0:n0 + nc]


def _linear_fwd(x, w, *, tm, name, q_cum=None, gather=()):
    S, K = x.shape
    N, nc = _cols_and_chunk(w)
    aug = q_cum is not None
    ng = len(gather)
    n_tiles = S // tm
    assert not (aug and ng)

    def body(*refs):
        x_ref, w_ref = refs[0], refs[1]
        o_ref = refs[4] if aug else refs[2 + ng]
        if ng:
            i = pl.program_id(0)
            start, forward, finish = _gather_steps(refs[2:2 + ng], refs[3 + ng:3 + 2 * ng],
                                                   refs[3 + 2 * ng], refs[4 + 2 * ng])
            pl.when(i == 0)(start)
            pl.when(i == n_tiles - 3)(forward)
            pl.when(i == n_tiles - 1)(finish)
        xb = x_ref[...].astype(BF16)
        if aug:
            lhs = _placement_lhs(refs[2][...], tm)
        for n0 in range(0, N, nc):
            r = _dot(xb, _w_cols(w_ref, n0, nc)[...])
            o_ref[:, n0:n0 + nc] = r.astype(BF16)
            if aug and n0 < D_MAIN:
                _store_augmented(refs[5], r * FOX_SCALE, n0 // LANES, lhs, refs[3])

    in_specs = [pl.BlockSpec((tm, K), lambda i: (i, 0)), _const_spec(w.shape)]
    out_specs = [pl.BlockSpec((tm, N), lambda i: (i, 0))]
    out_shape = [jax.ShapeDtypeStruct((S, N), BF16)]
    extra = []
    if aug:
        in_specs += [pl.BlockSpec((tm, LANES), lambda i: (i, 0)), _const_spec((LANES, AUG_W))]
        out_specs.append(pl.BlockSpec((tm, AUG_W), lambda i: (i, 0)))
        out_shape.append(jax.ShapeDtypeStruct((S, AUG_W), BF16))
        extra = [q_cum, _placement(AUG_A, (AUG_B,))]
    if ng:
        in_specs += [_ANY] * ng
        out_specs += [_ANY] * ng
        out_shape += [_gathered_shape(p) for p in gather]
        extra = list(gather)
    outs = pl.pallas_call(
        body, name=name, grid=(n_tiles,),
        in_specs=in_specs, out_specs=out_specs, out_shape=out_shape,
        scratch_shapes=_gather_sems(ng) if ng else [],
        compiler_params=_params(),
    )(x, w, *extra)
    return outs if (aug or ng) else outs[0]


def _kv_proj(x1, w_kv, bf_row, *, tm):
    S = x1.shape[0]

    def body(x_ref, w_ref, b_ref, pk_ref, pv_ref, k_ref, v_ref, fl_ref, cum_ref, carry_ref):
        i = pl.program_id(0)

        @pl.when(i == 0)
        def _():
            carry_ref[...] = jnp.zeros_like(carry_ref)

        xb = x_ref[...].astype(BF16)
        fl = _dot(xb, w_ref[:, 2 * D_MAIN:2 * D_MAIN + LANES]) + b_ref[...]
        fl_ref[...] = fl
        log_f = jnp.minimum(fl, 0.0) - jnp.log1p(jnp.exp(-jnp.abs(fl)))
        r = lax.broadcasted_iota(jnp.int32, (tm, tm), 0)
        c = lax.broadcasted_iota(jnp.int32, (tm, tm), 1)
        tri = jnp.where(c <= r, 1.0, 0.0).astype(BF16)
        hi, mid, lo = _split3(log_f)
        cum = (_dot(tri, hi) + _dot(tri, mid)) + _dot(tri, lo) + carry_ref[0:1, :]
        cum_ref[...] = cum
        carry_ref[0:1, :] = cum[tm - 1:tm, :]
        lhs_k, lhs_v = _placement_lhs(-cum, tm), _placement_lhs(None, tm)
        for n0 in range(0, D_MAIN, 512):
            _store_augmented(k_ref, _dot(xb, w_ref[:, n0:n0 + 512]), n0 // LANES, lhs_k, pk_ref)
            _store_augmented(v_ref, _dot(xb, w_ref[:, D_MAIN + n0:D_MAIN + n0 + 512]), n0 // LANES, lhs_v, pv_ref)

    return pl.pallas_call(
        body, name="kv_proj", grid=(S // tm,),
        in_specs=[pl.BlockSpec((tm, D_MODEL), lambda i: (i, 0)),
                  _const_spec((D_MODEL, 2 * D_MAIN + LANES)), _const_spec((1, LANES)),
                  _const_spec((LANES, AUG_W)), _const_spec((LANES, AUG_W))],
        out_specs=[pl.BlockSpec((tm, AUG_W), lambda i: (i, 0)),
                   pl.BlockSpec((tm, AUG_W), lambda i: (i, 0)),
                   pl.BlockSpec((tm, LANES), lambda i: (i, 0)),
                   pl.BlockSpec((tm, LANES), lambda i: (i, 0))],
        out_shape=[jax.ShapeDtypeStruct((S, AUG_W), BF16), jax.ShapeDtypeStruct((S, AUG_W), BF16),
                   jax.ShapeDtypeStruct((S, LANES), F32), jax.ShapeDtypeStruct((S, LANES), F32)],
        scratch_shapes=[pltpu.VMEM((8, LANES), F32)],
        compiler_params=_params(),
    )(x1, w_kv, bf_row, _placement(AUG_B, (AUG_A, AUG_C)), _placement(None, (AUG_A,)))


def _gate_bwd(dq_aug, dck, fl, *, tm):
    S = fl.shape[0]
    n = S // tm

    def body(dq_ref, dck_ref, fl_ref, du_ref, df_ref, db_ref, carry_ref):
        i = pl.program_id(0)

        @pl.when(i == 0)
        def _():
            carry_ref[...] = jnp.zeros_like(carry_ref)
            db_ref[...] = jnp.zeros_like(db_ref)

        lane = lax.broadcasted_iota(jnp.int32, (tm, LANES), 1)
        half0 = lane < FOX_HEAD_DIM
        dcq = jnp.zeros((tm, LANES), F32)
        for hp in range(FOX_HEADS // 2):
            b0 = dq_ref[:, 2 * hp * LANES:(2 * hp + 1) * LANES]
            b1 = dq_ref[:, (2 * hp + 1) * LANES:(2 * hp + 2) * LANES]
            du_ref[:, hp * LANES:(hp + 1) * LANES] = (
                jnp.where(half0, b0, pltpu.roll(b1, FOX_HEAD_DIM, 1)) * FOX_SCALE).astype(BF16)
            r0 = jnp.sum(jnp.where(lane == AUG_A, b0, 0.0), axis=1, keepdims=True)
            r1 = jnp.sum(jnp.where(lane == AUG_A, b1, 0.0), axis=1, keepdims=True)
            dcq = dcq + jnp.where(lane == 2 * hp, r0, 0.0) + jnp.where(lane == 2 * hp + 1, r1, 0.0)
        dcum = dcq - dck_ref[...]
        r = lax.broadcasted_iota(jnp.int32, (tm, tm), 0)
        c = lax.broadcasted_iota(jnp.int32, (tm, tm), 1)
        tri = jnp.where(c >= r, 1.0, 0.0).astype(BF16)
        hi, mid, lo = _split3(dcum)
        rev = (_dot(tri, hi) + _dot(tri, mid)) + _dot(tri, lo) + carry_ref[0:1, :]
        carry_ref[0:1, :] = rev[0:1, :]
        fl_v = fl_ref[...]
        df = rev * (1.0 / (1.0 + jnp.exp(fl_v)))
        df_ref[...] = df
        db_ref[...] += jnp.sum(df, axis=0, keepdims=True)

    return pl.pallas_call(
        body, name="gate_bwd", grid=(n,),
        in_specs=[pl.BlockSpec((tm, AUG_W), lambda i: (n - 1 - i, 0)),
                  pl.BlockSpec((tm, LANES), lambda i: (n - 1 - i, 0)),
                  pl.BlockSpec((tm, LANES), lambda i: (n - 1 - i, 0))],
        out_specs=[pl.BlockSpec((tm, D_MAIN), lambda i: (n - 1 - i, 0)),
                   pl.BlockSpec((tm, LANES), lambda i: (n - 1 - i, 0)),
                   pl.BlockSpec((1, LANES), lambda i: (0, 0))],
        out_shape=[jax.ShapeDtypeStruct((S, D_MAIN), BF16),
                   jax.ShapeDtypeStruct((S, LANES), F32), jax.ShapeDtypeStruct((1, LANES), F32)],
        scratch_shapes=[pltpu.VMEM((8, LANES), F32)],
        compiler_params=_params(),
    )(dq_aug, dck, fl)


def _silu_and_grad(g):
    sg = 1.0 / (1.0 + jnp.exp(-g))
    return g * sg, sg * (1.0 + g * (1.0 - sg))


def _mix(kind, mode, *, h, xres=None, mkv, w_out, ln_g, ln_b=None, pool_w=None, pool_scale=None,
         ymain=None, target=None, z=None, dy=None, gather=(), pair_send=(), tm):
    S = h.shape[0]
    n = S // tm
    pool = kind == "pool"
    bwd = mode == "bwd"
    loss_head = bwd and not pool
    rev = pool and bwd
    mem_scale = MEM_HEAD_DIM ** -0.5

    def t_of(i):
        return (n - 1 - i) if rev else i

    row = lambda i: (t_of(i), 0)
    names, arrays, specs = [], [], []

    def add(name, arr, spec):
        names.append(name)
        arrays.append(arr)
        specs.append(spec)

    add("h", h, pl.BlockSpec((tm, D_IN), row))
    if pool:
        hb = tm // POOL_HALO
        add("halo", h, pl.BlockSpec((POOL_HALO, D_MAIN), lambda i: (jnp.maximum(t_of(i) * hb - 1, 0), 0)))
        add("pool_w", pool_w, _const_spec((4, POOL_GROUP, POOL_GROUP)))
        add("pool_scale", pool_scale, _const_spec((1, D_MAIN)))
    else:
        add("ymain", ymain, pl.BlockSpec((tm, D_MAIN), row))
    add("mkv", mkv, _const_spec((N_MEM, 2 * D_MEM)))
    add("w_out", w_out, _const_spec((D_MIX, D_MODEL)))
    add("ln_g", ln_g, _const_spec((1, D_MODEL)))
    if not (pool and bwd):
        add("xres", xres, pl.BlockSpec((tm, D_MODEL), row))
        add("ln_b", ln_b, _const_spec((1, D_MODEL)))
    if loss_head:
        add("target", target, pl.BlockSpec((tm, D_MODEL), row))
        add("place_do", _placement(AUG_A, ()), _const_spec((LANES, AUG_W)))
    if pool and bwd:
        add("z", z, pl.BlockSpec((tm, D_MODEL), row))
        add("dy", dy, pl.BlockSpec((tm, D_MODEL), row))
    for a, p in enumerate(gather):
        add(f"gather_src{a}", p, _ANY)
    for a, p in enumerate(pair_send):
        add(f"pair_src{a}", p, _ANY)

    onames, oshapes, ospecs = [], [], []

    def add_out(name, shape, dtype, spec):
        onames.append(name)
        oshapes.append(jax.ShapeDtypeStruct(shape, dtype))
        ospecs.append(spec)

    const2 = lambda i: (0, 0)
    if not bwd:
        add_out("z", (S, D_MODEL), F32, pl.BlockSpec((tm, D_MODEL), row))
        add_out("xout", (S, D_MODEL), F32, pl.BlockSpec((tm, D_MODEL), row))
    else:
        add_out("dz", (S, D_MODEL), F32, pl.BlockSpec((tm, D_MODEL), row))
        if pool:
            add_out("dmain", (S, D_MAIN), BF16, pl.BlockSpec((tm, D_MAIN), row))
        else:
            add_out("dmain", (S, 2 * D_MAIN), BF16, pl.BlockSpec((tm, 2 * D_MAIN), row))
        add_out("drest", (S, D_IN - D_MAIN), BF16, pl.BlockSpec((tm, D_IN - D_MAIN), row))
        add_out("dw_out", (D_MIX, D_MODEL), F32, pl.BlockSpec((D_MIX, D_MODEL), const2))
        add_out("dmkv", (N_MEM, 2 * D_MEM), F32, pl.BlockSpec((N_MEM, 2 * D_MEM), const2))
        add_out("dln_g", (1, D_MODEL), F32, pl.BlockSpec((1, D_MODEL), const2))
        add_out("dln_b", (1, D_MODEL), F32, pl.BlockSpec((1, D_MODEL), const2))
        if pool:
            add_out("dpool_w", (4, POOL_GROUP, POOL_GROUP), F32,
                    pl.BlockSpec((4, POOL_GROUP, POOL_GROUP), lambda i: (0, 0, 0)))
            add_out("dpool_scale", (1, D_MAIN), F32, pl.BlockSpec((1, D_MAIN), const2))
        else:
            add_out("loss", (1, D_MODEL), F32, pl.BlockSpec((1, D_MODEL), const2))

    for a, p in enumerate(gather):
        add_out(f"gathered{a}", (N_CHIPS,) + p.shape, p.dtype, _ANY)
    for a, p in enumerate(pair_send):
        add_out(f"from_sibling{a}", (N_CHIPS,) + p.shape[2:], p.dtype, _ANY)

    scratch = [pltpu.VMEM((tm, D_MIX), BF16),
               pltpu.VMEM((tm, D_MEM), F32)]
    if pool:
        scratch.append(pltpu.VMEM((tm + 2 * POOL_HALO, D_MAIN), F32))
    if rev:
        scratch.append(pltpu.VMEM((POOL_HALO, D_MAIN), F32))
    assert not (gather and pair_send)
    if gather:
        scratch += _gather_sems(len(gather))
    if pair_send:
        scratch += _pair_sems(len(pair_send))
    n_in, n_out = len(names), len(onames)

    def body(*refs):
        R = dict(zip(names, refs[:n_in]))
        O = dict(zip(onames, refs[n_in:n_in + n_out]))
        sc = refs[n_in + n_out:]
        yc_ref, ymem_ref = sc[0], sc[1]
        ext_ref = sc[2] if pool else None
        carry_ref = sc[3] if rev else None
        i = pl.program_id(0)
        t = t_of(i)
        h_ref = R["h"]
        gamma = R["ln_g"][...]

        if gather:
            start, forward, finish = _gather_steps(
                [R[f"gather_src{a}"] for a in range(len(gather))],
                [O[f"gathered{a}"] for a in range(len(gather))], sc[-2], sc[-1])
            pl.when(i == 0)(start)
            pl.when(i == n - 3)(forward)
            pl.when(i == n - 1)(finish)
        if pair_send:
            start, finish = _pair_send_steps(
                [R[f"pair_src{a}"] for a in range(len(pair_send))],
                [O[f"from_sibling{a}"] for a in range(len(pair_send))], sc[-2], sc[-1])
            pl.when(i == 0)(start)
            pl.when(i == n - 1)(finish)

        if bwd:
            @pl.when(i == 0)
            def _():
                for nm in ("dw_out", "dmkv", "dln_g", "dln_b", "dpool_w", "dpool_scale", "loss"):
                    if nm in O:
                        O[nm][...] = jnp.zeros_like(O[nm])
                if rev:
                    carry_ref[...] = jnp.zeros_like(carry_ref)

        if pool:
            u = h_ref[:, 0:D_MAIN].astype(F32)
            halo = R["halo"][...].astype(F32)
            ext_ref[0:POOL_HALO, :] = jnp.where(t > 0, halo, 0.0)
            ext_ref[POOL_HALO:POOL_HALO + tm, :] = u
            tpos = t * tm + lax.broadcasted_iota(jnp.int32, (tm, 1), 0)
            pms, invcs = [], []
            for gi, w in enumerate(POOL_WINDOWS):
                cs = slice(gi * POOL_GROUP, (gi + 1) * POOL_GROUP)
                acc = ext_ref[POOL_HALO:POOL_HALO + tm, cs]
                for k in range(1, w):
                    acc = acc + ext_ref[POOL_HALO - k:POOL_HALO - k + tm, cs]
                invc = 1.0 / jnp.minimum(tpos + 1, w).astype(F32)
                pm = (acc * invc - u[:, cs]).astype(BF16)
                pms.append(pm)
                invcs.append(invc)
            mixed = [_dot(pms[gi], R["pool_w"][gi]) for gi in range(4)]
            ps = R["pool_scale"][...]
            y_main = [mixed[gi] * ps[:, gi * POOL_GROUP:(gi + 1) * POOL_GROUP] for gi in range(4)]
        else:
            y_main = [R["ymain"][:, gi * 256:(gi + 1) * 256].astype(F32) for gi in range(4)]

        probs = []
        for hd in range(MEM_HEADS):
            sl = slice(D_MAIN + hd * MEM_HEAD_DIM, D_MAIN + (hd + 1) * MEM_HEAD_DIM)
            ksl = slice(hd * MEM_HEAD_DIM, (hd + 1) * MEM_HEAD_DIM)
            vsl = slice(D_MEM + hd * MEM_HEAD_DIM, D_MEM + (hd + 1) * MEM_HEAD_DIM)
            s = _dot_nt(h_ref[:, sl], R["mkv"][:, ksl]) * mem_scale
            e = jnp.exp(s - jnp.max(s, axis=1, keepdims=True))
            p = e / jnp.sum(e, axis=1, keepdims=True)
            probs.append(p)
            ymem_ref[:, ksl] = _dot(p.astype(BF16), R["mkv"][:, vsl])

        g_off = D_MIX
        gate_d = []
        for gi in range(4):
            cs = slice(gi * 256, (gi + 1) * 256)
            gm = h_ref[:, g_off + gi * 256:g_off + (gi + 1) * 256].astype(F32)
            sv, sd = _silu_and_grad(gm)
            yc_ref[:, cs] = (y_main[gi] * sv).astype(BF16)
            gate_d.append((sv, sd))
        gq = h_ref[:, g_off + D_MAIN:D_IN].astype(F32)
        svq, sdq = _silu_and_grad(gq)
        yc_ref[:, D_MAIN:D_MIX] = (ymem_ref[...] * svq).astype(BF16)

        if pool and bwd:
            zt = R["z"][...]
        else:
            o = _dot(yc_ref[...], R["w_out"][...])
            zt = ALPHA * R["xres"][...] + o
        mu = jnp.mean(zt, axis=1, keepdims=True)
        zc = zt - mu
        var = jnp.mean(zc * zc, axis=1, keepdims=True)
        rstd = lax.rsqrt(var + LN_EPS)
        xhat = zc * rstd
        if not bwd:
            O["z"][...] = zt
            O["xout"][...] = xhat * gamma + R["ln_b"][...]
            return

        if loss_head:
            xo = xhat * gamma + R["ln_b"][...]
            err = xo - R["target"][...]
            O["loss"][...] += jnp.sum(err * err, axis=0, keepdims=True)
            dyt = err * (1.0 / D_MODEL)
        else:
            dyt = R["dy"][...]

        O["dln_g"][...] += jnp.sum(dyt * xhat, axis=0, keepdims=True)
        O["dln_b"][...] += jnp.sum(dyt, axis=0, keepdims=True)
        gdy = dyt * gamma
        m1 = jnp.mean(gdy, axis=1, keepdims=True)
        m2 = jnp.mean(gdy * xhat, axis=1, keepdims=True)
        dz = rstd * (gdy - m1 - xhat * m2)
        O["dz"][...] = dz
        dzb = dz.astype(BF16)

        for n0 in range(0, D_MIX, 512):
            O["dw_out"][n0:n0 + 512, :] += _dot_tn(yc_ref[:, n0:n0 + 512], dzb)
        dyc_mem = _dot_nt(dzb, R["w_out"][D_MAIN:D_MIX, :])

        O["drest"][:, D_MEM + D_MAIN:D_MEM + D_MAIN + D_MEM] = (dyc_mem * ymem_ref[...] * sdq).astype(BF16)
        dymem = dyc_mem * svq
        for hd in range(MEM_HEADS):
            sl = slice(D_MAIN + hd * MEM_HEAD_DIM, D_MAIN + (hd + 1) * MEM_HEAD_DIM)
            ksl = slice(hd * MEM_HEAD_DIM, (hd + 1) * MEM_HEAD_DIM)
            vsl = slice(D_MEM + hd * MEM_HEAD_DIM, D_MEM + (hd + 1) * MEM_HEAD_DIM)
            p = probs[hd]
            dyb = dymem[:, ksl].astype(BF16)
            dp = _dot_nt(dyb, R["mkv"][:, vsl])
            ds = p * (dp - jnp.sum(dp * p, axis=1, keepdims=True)) * mem_scale
            dsb = ds.astype(BF16)
            O["drest"][:, ksl] = _dot(dsb, R["mkv"][:, ksl]).astype(BF16)
            O["dmkv"][:, ksl] += _dot_tn(dsb, h_ref[:, sl])
            O["dmkv"][:, vsl] += _dot_tn(p.astype(BF16), dyb)

        dmain = []
        for gi in range(4):
            cs = slice(gi * 256, (gi + 1) * 256)
            dyc_g = _dot_nt(dzb, R["w_out"][cs, :])
            sv, sd = gate_d[gi]
            O["drest"][:, D_MEM + gi * 256:D_MEM + (gi + 1) * 256] = (dyc_g * y_main[gi] * sd).astype(BF16)
            dmain.append(dyc_g * sv)

        if not pool:
            dbf = [dmain[gi].astype(BF16).astype(F32) for gi in range(4)]
            dcol = jnp.zeros((tm, LANES), F32)
            for gi in range(4):
                dr = lax.broadcasted_iota(jnp.int32, (256, LANES), 0)
                hc = lax.broadcasted_iota(jnp.int32, (256, LANES), 1)
                sel = jnp.where(jnp.right_shift(dr, 6) + gi * 4 == hc, 1.0, 0.0).astype(BF16)
                hi, mid, lo = _split3(dbf[gi] * R["ymain"][:, gi * 256:(gi + 1) * 256].astype(F32))
                dcol = dcol + ((_dot(hi, sel) + _dot(mid, sel)) + _dot(lo, sel))
            lhs = _placement_lhs(-dcol, tm)
            for gi in range(4):
                _store_augmented(O["dmain"], dbf[gi], 2 * gi, lhs, R["place_do"])
            return

        ps = R["pool_scale"][...]
        dpm_list = []
        for gi in range(4):
            cs = slice(gi * 256, (gi + 1) * 256)
            O["dpool_scale"][:, cs] += jnp.sum(dmain[gi] * mixed[gi], axis=0, keepdims=True)
            dmix = (dmain[gi] * ps[:, cs]).astype(BF16)
            O["dpool_w"][gi] += _dot_tn(pms[gi], dmix)
            dpm = _dot_nt(dmix, R["pool_w"][gi])
            dpm_list.append(dpm)
            ext_ref[0:tm, cs] = dpm * invcs[gi]
        ext_ref[tm:tm + POOL_HALO, :] = carry_ref[...]
        carry_ref[...] = ext_ref[0:POOL_HALO, :]
        for gi, w in enumerate(POOL_WINDOWS):
            cs = slice(gi * 256, (gi + 1) * 256)
            acc = ext_ref[0:tm, cs]
            for k in range(1, w):
                acc = acc + ext_ref[k:k + tm, cs]
            O["dmain"][:, cs] = (acc - dpm_list[gi]).astype(BF16)

    outs = pl.pallas_call(
        body, name=f"mix_{kind}_{mode}", grid=(n,),
        in_specs=specs, out_specs=ospecs, out_shape=oshapes,
        scratch_shapes=scratch, compiler_params=_params(),
    )(*arrays)
    return dict(zip(onames, outs))


def _lin_bwd(xin, dhs, w, res, *, tm, name, scatter=()):
    S, K = xin.shape
    N, nc_w = _cols_and_chunk(w)
    nj, nr, ns = len(dhs), len(res), len(scatter)
    n_tiles = S // tm
    widths = [dh.shape[1] for dh in dhs]
    chunks = [nc_w if w.ndim == 3 else (512 if wd % 512 == 0 else wd) for wd in widths]
    assert sum(widths) == N and all(wd % c == 0 for wd, c in zip(widths, chunks))
    scales = [s for _, s in res]
    n_in = 2 + nj + nr + ns

    def body(*refs):
        x_ref = refs[0]
        dh_refs = refs[1:1 + nj]
        w_ref = refs[1 + nj]
        r_refs = refs[2 + nj:2 + nj + nr]
        dx_ref, dw_ref = refs[n_in], refs[n_in + 1]
        i = pl.program_id(0)

        if ns:
            start, finish = _scatter_steps(refs[n_in - ns:n_in], refs[n_in + 2:n_in + 2 + ns],
                                           refs[n_in + 2 + ns], refs[n_in + 3 + ns])
            pl.when(i == 0)(start)
            pl.when(i == n_tiles - 1)(finish)

        @pl.when(i == 0)
        def _():
            dw_ref[...] = jnp.zeros_like(dw_ref)

        xb = x_ref[...].astype(BF16)
        dx = jnp.zeros((tm, K), F32)
        for r_ref, sc in zip(r_refs, scales):
            dx = dx + sc * r_ref[...]
        off = 0
        for j in range(nj):
            nc = chunks[j]
            for n0 in range(0, widths[j], nc):
                dhb = dh_refs[j][:, n0:n0 + nc].astype(BF16)
                dx = dx + _dot_nt(dhb, _w_cols(w_ref, off + n0, nc)[...])
                _w_cols(dw_ref, off + n0, nc)[...] += _dot_tn(xb, dhb)
            off += widths[j]
        dx_ref[...] = dx

    zeros = (0,) * w.ndim
    in_specs = [pl.BlockSpec((tm, K), lambda i: (i, 0))]
    in_specs += [pl.BlockSpec((tm, n), lambda i: (i, 0)) for n in widths]
    in_specs += [_const_spec(w.shape)]
    in_specs += [pl.BlockSpec((tm, K), lambda i: (i, 0)) for _ in res]
    in_specs += [_ANY] * ns
    out_specs = [pl.BlockSpec((tm, K), lambda i: (i, 0)), pl.BlockSpec(w.shape, lambda i: zeros)] + [_ANY] * ns
    out_shape = [jax.ShapeDtypeStruct((S, K), F32), jax.ShapeDtypeStruct(w.shape, F32)]
    out_shape += [jax.ShapeDtypeStruct(p.shape, p.dtype) for p in scatter]
    return pl.pallas_call(
        body, name=name, grid=(n_tiles,),
        in_specs=in_specs, out_specs=out_specs, out_shape=out_shape,
        scratch_shapes=_scatter_sems(ns) if ns else [],
        compiler_params=_params(),
    )(xin, *dhs, w, *[r for r, _ in res], *scatter)


def _wgrad(xin, dh, *, name):
    M, K = xin.shape
    N = dh.shape[1]

    def body(x_ref, dh_ref, o_ref):
        o_ref[...] = _dot_tn(x_ref[...].astype(BF16), dh_ref[...].astype(BF16))

    return pl.pallas_call(
        body, name=name, out_shape=jax.ShapeDtypeStruct((K, N), F32),
        compiler_params=pltpu.CompilerParams(vmem_limit_bytes=VMEM_LIMIT),
    )(xin, dh)


AUG_A = FOX_HEAD_DIM
AUG_B = FOX_HEAD_DIM + 3
AUG_C = FOX_HEAD_DIM + 6
AUG_W = FOX_HEADS * LANES


def _placement(val_lane, ones_lanes):
    r = jnp.arange(LANES)[:, None]
    c = jnp.arange(AUG_W)[None, :]
    head, lane = c // LANES, c % LANES
    m = jnp.zeros((LANES, AUG_W), jnp.bool_)
    if val_lane is not None:
        for part in range(3):
            m = m | ((r == part * FOX_HEADS + head) & (lane == val_lane + part))
    for first in ones_lanes:
        m = m | ((r == 3 * FOX_HEADS) & (lane >= first) & (lane < first + 3))
    return m.astype(BF16)


def _placement_lhs(val, tm):
    lane = lax.broadcasted_iota(jnp.int32, (tm, LANES), 1)
    lhs = jnp.where(lane == 3 * FOX_HEADS, 1.0, 0.0)
    if val is not None:
        hi, mid, lo = [p.astype(F32) for p in _split3(val)]
        lhs = jnp.where(lane < FOX_HEADS, hi, jnp.where(
            lane < 2 * FOX_HEADS, pltpu.roll(mid, FOX_HEADS, 1), jnp.where(
                lane < 3 * FOX_HEADS, pltpu.roll(lo, 2 * FOX_HEADS, 1), lhs)))
    return lhs.astype(BF16)


def _store_augmented(o_ref, data, first_pair, lhs, p_ref):
    tm = data.shape[0]
    is_data = lax.broadcasted_iota(jnp.int32, (tm, LANES), 1) < FOX_HEAD_DIM
    for j in range(data.shape[1] // LANES):
        base = 2 * (first_pair + j) * LANES
        extra = _dot(lhs, p_ref[:, base:base + 2 * LANES])
        blk = data[:, j * LANES:(j + 1) * LANES]
        o_ref[:, base:base + LANES] = jnp.where(is_data, blk, extra[:, 0:LANES]).astype(BF16)
        o_ref[:, base + LANES:base + 2 * LANES] = jnp.where(
            is_data, pltpu.roll(blk, FOX_HEAD_DIM, 1), extra[:, LANES:2 * LANES]).astype(BF16)


def _cols_of_rows(rows, S):
    nh = FOX_HEADS // rows.shape[0]
    a = rows[:, :, 0:nh, :].transpose(0, 2, 1, 3).reshape(FOX_HEADS, S).T
    return jnp.pad(a, ((0, 0), (0, LANES - FOX_HEADS)))


def _fox_fwd(qf, ka, va, *, tq, nh=4):
    S = ka.shape[0]
    nq = S // tq
    tk = tq
    ng = FOX_HEADS // nh

    def body(q_ref, k_ref, v_ref, o_ref, qb_ref, *scratch):
        p_scr, m_scr, acc_scr = scratch[0:nh], scratch[nh:2 * nh], scratch[2 * nh:3 * nh]
        qi = pl.program_id(1)
        lane = lax.broadcasted_iota(jnp.int32, (tq, LANES), 1)
        half0 = lane < FOX_HEAD_DIM
        rr = lax.broadcasted_iota(jnp.int32, (tq, tk), 0)
        cc = lax.broadcasted_iota(jnp.int32, (tq, tk), 1)
        sls = [slice(hh * LANES, (hh + 1) * LANES) for hh in range(nh)]
        qs = [q_ref[:, sl] for sl in sls]

        for hh in range(nh):
            p_scr[hh][...] = jnp.zeros_like(p_scr[hh])
            m_scr[hh][...] = jnp.full(m_scr[hh].shape, -jnp.inf, F32)
            acc_scr[hh][...] = jnp.zeros_like(acc_scr[hh])

        def chunk(ki, masked):
            k0 = pl.multiple_of(ki * tk, tk)
            kp = pl.multiple_of(jnp.maximum(ki - 1, 0) * tk, tk)
            for hh in range(nh):
                m = m_scr[hh][...]
                s = _dot_nt(qs[hh], k_ref[pl.ds(k0, tk), sls[hh]])
                pv = _dot(p_scr[hh][...], v_ref[pl.ds(kp, tk), sls[hh]])
                if masked:
                    s = jnp.where(cc <= rr, s, -jnp.inf)
                m_new = jnp.maximum(m, jnp.max(s, axis=1, keepdims=True))
                p_scr[hh][...] = jnp.exp(s - jnp.tile(m_new, (1, tk // LANES))).astype(BF16)
                acc_scr[hh][...] = (acc_scr[hh][...] + pv) * jnp.exp(m - m_new)
                m_scr[hh][...] = m_new

        def trip(ki, c):
            chunk(ki, False)
            return c

        lax.fori_loop(0, qi, trip, 0)
        chunk(qi, True)
        kq = pl.multiple_of(qi * tk, tk)
        outs = []
        for hh in range(nh):
            m = m_scr[hh][...]
            acc = acc_scr[hh][...] + _dot(p_scr[hh][...], v_ref[pl.ds(kq, tk), sls[hh]])
            l = jnp.sum(jnp.where(lane == AUG_A, acc, 0.0), axis=1, keepdims=True)
            outs.append(acc / l)
            hi, mid, lo = _split3(-(m + jnp.log(l)))
            qb_ref[:, sls[hh]] = jnp.where(lane == AUG_C, hi, jnp.where(
                lane == AUG_C + 1, mid, jnp.where(lane == AUG_C + 2, lo, qs[hh])))
        for pr in range(nh // 2):
            o_ref[:, pr * LANES:(pr + 1) * LANES] = jnp.where(
                half0, outs[2 * pr], pltpu.roll(outs[2 * pr + 1], FOX_HEAD_DIM, 1))

    return pl.pallas_call(
        body, name="fox_fwd", grid=(ng, nq),
        in_specs=[pl.BlockSpec((tq, nh * LANES), lambda g, qi: (qi, g)),
                  pl.BlockSpec((S, nh * LANES), lambda g, qi: (0, g), pipeline_mode=pl.Buffered(1)),
                  pl.BlockSpec((S, nh * LANES), lambda g, qi: (0, g), pipeline_mode=pl.Buffered(1))],
        out_specs=[pl.BlockSpec((tq, nh * FOX_HEAD_DIM), lambda g, qi: (qi, g)),
                   pl.BlockSpec((tq, nh * LANES), lambda g, qi: (qi, g))],
        out_shape=[jax.ShapeDtypeStruct((S, D_MAIN), F32),
                   jax.ShapeDtypeStruct((S, AUG_W), BF16)],
        scratch_shapes=([pltpu.VMEM((tq, tk), BF16)] * nh + [pltpu.VMEM((tq, LANES), F32)] * nh
                        + [pltpu.VMEM((tq, LANES), F32)] * nh),
        compiler_params=_params(2),
    )(qf, ka, va)


def _fox_bwd(qb, ka, va, do_aug, *, tq):
    S = ka.shape[0]
    nq = S // tq
    tk = tq

    def body(k_ref, v_ref, q_ref, do_ref, dq_ref, dk_ref, dv_ref, dck_ref, dk_scr, dv_scr):
        kj = pl.program_id(1)

        @pl.when(kj == 0)
        def _():
            dq_ref[...] = jnp.zeros_like(dq_ref)

        lane = lax.broadcasted_iota(jnp.int32, (tk, LANES), 1)
        half0 = lane < FOX_HEAD_DIM
        rr = lax.broadcasted_iota(jnp.int32, (tk, tq), 0)
        cc = lax.broadcasted_iota(jnp.int32, (tk, tq), 1)
        sls = [slice(hh * LANES, (hh + 1) * LANES) for hh in range(2)]
        kts = [k_ref[:, sl] for sl in sls]
        vts = [v_ref[:, sl] for sl in sls]

        dk_scr[...] = jnp.zeros_like(dk_scr)
        dv_scr[...] = jnp.zeros_like(dv_scr)

        def chunk(qi, masked):
            q0 = pl.multiple_of(qi * tq, tq)
            for hh in range(2):
                qc = q_ref[pl.ds(q0, tq), sls[hh]]
                doc = do_ref[pl.ds(q0, tq), sls[hh]]
                pt = jnp.exp(_dot_nt(kts[hh], qc))
                if masked:
                    pt = jnp.where(rr <= cc, pt, 0.0)
                dsb = (pt * _dot_nt(vts[hh], doc)).astype(BF16)
                dv_scr[hh] += _dot(pt.astype(BF16), doc)
                dk_scr[hh] += _dot(dsb, qc)
                dq_ref[pl.ds(q0, tq), sls[hh]] += _dot_tn(dsb, kts[hh])

        def trip(qi, c):
            chunk(qi, False)
            return c

        chunk(kj, True)
        lax.fori_loop(kj + 1, nq, trip, 0)
        dk0, dk1 = dk_scr[0], dk_scr[1]
        dv0, dv1 = dv_scr[0], dv_scr[1]
        dk_ref[...] = jnp.where(half0, dk0, pltpu.roll(dk1, FOX_HEAD_DIM, 1)).astype(BF16)
        dv_ref[...] = jnp.where(half0, dv0, pltpu.roll(dv1, FOX_HEAD_DIM, 1)).astype(BF16)
        c0 = jnp.sum(jnp.where(lane == AUG_B, dk0, 0.0), axis=1, keepdims=True)
        c1 = jnp.sum(jnp.where(lane == AUG_B, dk1, 0.0), axis=1, keepdims=True)
        dck_cols = jnp.where(lane == 0, c0, 0.0) + jnp.where(lane == 1, c1, 0.0)
        dck_ref[0, 0] = dck_cols.T[0:8, :]

    return pl.pallas_call(
        body, name="fox_bwd", grid=(8, nq),
        in_specs=[pl.BlockSpec((tk, 2 * LANES), lambda hp, kj: (kj, hp)),
                  pl.BlockSpec((tk, 2 * LANES), lambda hp, kj: (kj, hp)),
                  pl.BlockSpec((S, 2 * LANES), lambda hp, kj: (0, hp)),
                  pl.BlockSpec((S, 2 * LANES), lambda hp, kj: (0, hp))],
        out_specs=[pl.BlockSpec((S, 2 * LANES), lambda hp, kj: (0, hp)),
                   pl.BlockSpec((tk, LANES), lambda hp, kj: (kj, hp)),
                   pl.BlockSpec((tk, LANES), lambda hp, kj: (kj, hp)),
                   pl.BlockSpec((1, 1, 8, tk), lambda hp, kj: (hp, kj, 0, 0))],
        out_shape=[jax.ShapeDtypeStruct((S, AUG_W), F32),
                   jax.ShapeDtypeStruct((S, D_MAIN), BF16),
                   jax.ShapeDtypeStruct((S, D_MAIN), BF16),
                   jax.ShapeDtypeStruct((8, nq, 8, tk), F32)],
        scratch_shapes=[pltpu.VMEM((2, tk, LANES), F32), pltpu.VMEM((2, tk, LANES), F32)],
        compiler_params=_params(2),
    )(ka, va, qb, do_aug)


def _adamw(w, g, m, v, *, name):
    Rr, C = w.shape
    tr = 256 if Rr % 256 == 0 else Rr
    c1 = 1.0 / (1.0 - ADAM_B1 ** ADAM_STEP)
    c2 = 1.0 / (1.0 - ADAM_B2 ** ADAM_STEP)

    def body(w_ref, g_ref, m_ref, v_ref, d_ref, nm_ref, nv_ref):
        gv = g_ref[...]
        nm = ADAM_B1 * m_ref[...] + (1.0 - ADAM_B1) * gv
        nv = ADAM_B2 * v_ref[...] + (1.0 - ADAM_B2) * (gv * gv)
        d_ref[...] = -ADAM_LR * ((nm * c1) / (jnp.sqrt(nv * c2) + ADAM_EPS) + ADAM_WD * w_ref[...])
        nm_ref[...] = nm
        nv_ref[...] = nv

    spec = pl.BlockSpec((tr, C), lambda i: (i, 0))
    sds = jax.ShapeDtypeStruct((Rr, C), F32)
    return pl.pallas_call(
        body, name=name, grid=(Rr // tr,),
        in_specs=[spec] * 4, out_specs=[spec] * 3, out_shape=[sds] * 3,
        compiler_params=_params(),
    )(w, g, m, v)


_ANY = pl.BlockSpec(memory_space=pl.ANY)
_MESH = pl.DeviceIdType.MESH


def _place():
    x, y, c = lax.axis_index("x"), lax.axis_index("y"), lax.axis_index("c")
    return x, y, c


def _gather_steps(p_refs, out_refs, send_sems, recv_sems):
    x, y, c = _place()
    sib = (x, y, 1 - c)
    chips = [(1 - x, y), (x, 1 - y), (1 - x, 1 - y)]
    idx = [2 * chip[0] + chip[1] for chip in chips]
    me = 2 * x + y
    na = len(p_refs)

    def copy(a, k, chip_idx, half, to, src=None):
        dst = out_refs[a].at[chip_idx, half]
        return pltpu.make_async_remote_copy(
            src_ref=dst if src is None else src, dst_ref=dst,
            send_sem=send_sems.at[6 * a + k], recv_sem=recv_sems.at[6 * a + k],
            device_id=to, device_id_type=_MESH)

    first = [copy(a, j, me, c, (*chips[j], c), src=p_refs[a].at[c]) for a in range(na) for j in range(3)]
    passed = [copy(a, 3 + j, idx[j], c, sib) for a in range(na) for j in range(3)]

    def start():
        for cp in first:
            cp.start()

    def forward():
        for a in range(na):
            for j in range(3):
                copy(a, j, idx[j], c, sib).wait_recv()
                passed[3 * a + j].start()

    def finish():
        for a in range(na):
            for j in range(3):
                copy(a, 3 + j, idx[j], 1 - c, sib).wait_recv()
        for cp in first + passed:
            cp.wait_send()

    return start, forward, finish


def _sems(n):
    return [pltpu.SemaphoreType.DMA((n,)), pltpu.SemaphoreType.DMA((n,))]


def _gather_sems(na):
    return _sems(6 * na)


def _scatter_sems(na):
    return _sems(3 * na)


def _pair_sems(na):
    return _sems(N_CHIPS * na)


def _gathered_shape(pack):
    return jax.ShapeDtypeStruct((N_CHIPS,) + pack.shape, pack.dtype)


def _from_sibling_shape(gpack):
    return jax.ShapeDtypeStruct((N_CHIPS,) + gpack.shape[2:], gpack.dtype)


def _all_gather_shards(packs):
    na = len(packs)

    def body(*refs):
        for step in _gather_steps(refs[0:na], refs[na:2 * na], refs[2 * na], refs[2 * na + 1]):
            step()

    return pl.pallas_call(
        body, name="all_gather_shards",
        in_specs=[_ANY] * na, out_specs=[_ANY] * na, out_shape=[_gathered_shape(p) for p in packs],
        scratch_shapes=_gather_sems(na),
    )(*packs)


def _pair_send_steps(g_refs, out_refs, send_sem, recv_sem):
    x, y, c = _place()
    cps = [pltpu.make_async_remote_copy(
        src_ref=g_refs[a].at[j, 1 - c], dst_ref=out_refs[a].at[j],
        send_sem=send_sem.at[N_CHIPS * a + j], recv_sem=recv_sem.at[N_CHIPS * a + j],
        device_id=(x, y, 1 - c), device_id_type=_MESH) for a in range(len(g_refs)) for j in range(N_CHIPS)]

    def start():
        for cp in cps:
            cp.start()

    def finish():
        for cp in cps:
            cp.wait_recv()
        for cp in cps:
            cp.wait_send()

    return start, finish


def _send_half_to_sibling(gpacks, tag):
    na = len(gpacks)

    def body(*refs):
        for step in _pair_send_steps(refs[0:na], refs[na:2 * na], refs[2 * na], refs[2 * na + 1]):
            step()

    return pl.pallas_call(
        body, name=f"pair_send{tag}",
        in_specs=[_ANY] * na, out_specs=[_ANY] * na, out_shape=[_from_sibling_shape(g) for g in gpacks],
        scratch_shapes=_pair_sems(na),
    )(*gpacks)


def _pair_sum(gpack, recv, c_arr, tag, *, tr=PACK_TILE):
    rows, lanes = recv.shape[1:]
    assert rows % tr == 0

    def body(c_ref, a_ref, b_ref, o_ref):
        o_ref[...] = (a_ref[...] + b_ref[...]).astype(BF16)

    grid_spec = pltpu.PrefetchScalarGridSpec(
        num_scalar_prefetch=1, grid=(N_CHIPS, rows // tr),
        in_specs=[pl.BlockSpec((None, None, tr, lanes), lambda j, i, c_ref: (j, c_ref[0], i, 0)),
                  pl.BlockSpec((None, tr, lanes), lambda j, i, c_ref: (j, i, 0))],
        out_specs=pl.BlockSpec((None, tr, lanes), lambda j, i, c_ref: (j, i, 0)))
    return pl.pallas_call(
        body, name=f"pair_sum{tag}", grid_spec=grid_spec,
        out_shape=jax.ShapeDtypeStruct((N_CHIPS, rows, lanes), BF16),
        compiler_params=_params(2),
    )(c_arr, gpack, recv)


def _scatter_steps(p_refs, out_refs, send_sems, recv_sems):
    x, y, c = _place()
    chips = [(1 - x, y), (x, 1 - y), (1 - x, 1 - y)]
    me = 2 * x + y
    cps = [pltpu.make_async_remote_copy(
        src_ref=p_refs[a].at[2 * chip[0] + chip[1]], dst_ref=out_refs[a].at[me],
        send_sem=send_sems.at[3 * a + j], recv_sem=recv_sems.at[3 * a + j],
        device_id=(*chip, c), device_id_type=_MESH) for a in range(len(p_refs)) for j, chip in enumerate(chips)]

    def start():
        for cp in cps:
            cp.start()

    def finish():
        for cp in cps:
            cp.wait_recv()
        for cp in cps:
            cp.wait_send()

    return start, finish


def _share_steps(row_ref, out_ref, send_sems, recv_sems):
    x, y, c = _place()
    mine = 4 * x + 2 * y + c
    cps = []
    for k in range(1, 8):
        fx, fy, fc = (k >> 2) & 1, (k >> 1) & 1, k & 1
        peer = (x + fx - 2 * x * fx, y + fy - 2 * y * fy, c + fc - 2 * c * fc)
        cps.append(pltpu.make_async_remote_copy(
            src_ref=row_ref, dst_ref=out_ref.at[mine], send_sem=send_sems.at[k - 1], recv_sem=recv_sems.at[k - 1],
            device_id=peer, device_id_type=_MESH))

    def start():
        for cp in cps:
            cp.start()

    def finish():
        for cp in cps:
            cp.wait_recv()
        for cp in cps:
            cp.wait_send()

    return start, finish


def _scatter_pieces(psums, tag, share=None):
    na = len(psums)
    ns = 0 if share is None else 1

    def body(*refs):
        n_in = na + ns
        steps = [_scatter_steps(refs[0:na], refs[n_in:n_in + na], refs[2 * n_in], refs[2 * n_in + 1])]
        if ns:
            steps.append(_share_steps(refs[na], refs[n_in + na], refs[2 * n_in + 2], refs[2 * n_in + 3]))
        for phase in range(2):
            for st in steps:
                st[phase]()

    out_shape = [jax.ShapeDtypeStruct(p.shape, p.dtype) for p in psums]
    if ns:
        out_shape.append(jax.ShapeDtypeStruct((8,) + share.shape, share.dtype))
    return pl.pallas_call(
        body, name=f"scatter_pieces{tag}",
        in_specs=[_ANY] * (na + ns), out_specs=[_ANY] * (na + ns), out_shape=out_shape,
        scratch_shapes=_scatter_sems(na) + (_sems(7) if ns else []),
    )(*psums, *([share] if ns else []))


def _sum_pieces(pieces, tag, *, tr=PACK_TILE):
    rows, lanes = pieces.shape[1:]
    assert rows % tr == 0

    def body(p_ref, o_ref):
        acc = p_ref[0].astype(F32) + p_ref[1].astype(F32)
        acc = acc + p_ref[2].astype(F32)
        o_ref[...] = acc + p_ref[3].astype(F32)

    return pl.pallas_call(
        body, name=f"sum_pieces{tag}", grid=(rows // tr,),
        in_specs=[pl.BlockSpec((N_CHIPS, tr, lanes), lambda i: (0, i, 0))],
        out_specs=pl.BlockSpec((tr, lanes), lambda i: (i, 0)),
        out_shape=jax.ShapeDtypeStruct((rows, lanes), F32),
        compiler_params=_params(),
    )(pieces)


def _exchange_halves(totals):
    n = len(totals)

    def body(*refs):
        t_refs, out_refs, send_sem, recv_sem = refs[:n], refs[n:2 * n], refs[2 * n], refs[2 * n + 1]
        x, y, c = _place()
        cps = [pltpu.make_async_remote_copy(
            src_ref=t_refs[i], dst_ref=out_refs[i].at[c], send_sem=send_sem.at[i], recv_sem=recv_sem.at[i],
            device_id=(x, y, 1 - c), device_id_type=_MESH) for i in range(n)]
        for cp in cps:
            cp.start()
        for cp in cps:
            cp.wait_recv()
        for cp in cps:
            cp.wait_send()

    return pl.pallas_call(
        body, name="exchange_halves",
        in_specs=[_ANY] * n, out_specs=[_ANY] * n,
        out_shape=[jax.ShapeDtypeStruct((2,) + t.shape, F32) for t in totals],
        scratch_shapes=[pltpu.SemaphoreType.DMA((n,)), pltpu.SemaphoreType.DMA((n,))],
    )(*totals)


def _pad_rows(a, rows):
    return jnp.pad(a, ((0, rows - a.shape[0]), (0, 0)))


def _pack_weight_shards(w_in, w_mem_kv, w_out, pool_w, w_kv_shared, pool_scale):
    ps_bits = lax.bitcast_convert_type(pool_scale.reshape(-1), BF16).reshape(1, -1)
    ps_row = jnp.pad(ps_bits, ((0, 0), (0, 1024 - ps_bits.shape[1])))

    def common(l):
        return [w_mem_kv[l].astype(BF16).reshape(ROWS_W_MKV, 1024),
                w_out[l].astype(BF16).reshape(ROWS_W_OUT, 1024)]

    p0 = common(0) + [pool_w.astype(BF16).reshape(ROWS_POOL_W, 1024), _pad_rows(ps_row, ROWS_SMALL),
                      jnp.zeros((PACK0_ROWS - OFF_LN_G, 1024), BF16)]
    p1 = common(1) + [_pad_rows(w_kv_shared.astype(BF16).reshape(KV_SHARD, 1024), ROWS_W_KV),
                      jnp.zeros((PACK1_ROWS - OFF_BF, 1024), BF16)]
    w_in_halves = w_in.astype(BF16).reshape(2, 2, D_MODEL // 2, W_IN_SHARD)
    return ([jnp.concatenate(p0, axis=0).reshape(2, PACK0_ROWS // 2, 1024), w_in_halves[0]],
            [jnp.concatenate(p1, axis=0).reshape(2, PACK1_ROWS // 2, 1024), w_in_halves[1]])


def _unpack_w_in(g_in):
    return g_in.reshape(N_CHIPS, D_MODEL, W_IN_SHARD)


def _unpack_common(g):
    w_mkv = g[:, OFF_W_MKV:OFF_W_MKV + ROWS_W_MKV].reshape(D_MODEL, 2 * D_MEM)
    w_out = g[:, OFF_W_OUT:OFF_W_OUT + ROWS_W_OUT].reshape(D_MIX, D_MODEL)
    return w_mkv, w_out


def _unpack_weights0(g):
    pool_w = g[:, OFF_POOL_W:OFF_POOL_W + ROWS_POOL_W].reshape(4, 4, POOL_GROUP // 4, POOL_GROUP)
    pool_w = pool_w.transpose(1, 0, 2, 3).reshape(4, POOL_GROUP, POOL_GROUP)
    ps_bits = g[:, OFF_POOL_S, 0:512].reshape(4, 256, 2)
    pool_scale = lax.bitcast_convert_type(ps_bits, F32).reshape(1, D_MAIN)
    return _unpack_common(g) + (pool_w, pool_scale)


def _unpack_weights1(g):
    w_kv = g[:, OFF_W_KV:OFF_W_KV + KV_SHARD].reshape(4, D_MODEL, KV_SHARD)
    w_kv = w_kv.transpose(1, 0, 2).reshape(D_MODEL, KV_COLS)
    return _unpack_common(g) + (w_kv,)


def _replicated_rows(a):
    a = _pad_rows(a, ROWS_SMALL)
    return jnp.broadcast_to(a[None], (4,) + a.shape)


def _pack_common_grads(g_w_mkv, g_w_out):
    return [g_w_mkv.reshape(4, ROWS_W_MKV, 1024), g_w_out.reshape(4, ROWS_W_OUT, 1024)]


def _w_in_grad_halves(g_w_in):
    return g_w_in.reshape(N_CHIPS, 2, D_MODEL // 2, W_IN_SHARD)


def _pack_grads0(g_w_mkv, g_w_out, g_pool_w, g_pool_scale, g_ln_g, g_ln_b):
    parts = _pack_common_grads(g_w_mkv, g_w_out) + [
        g_pool_w.reshape(4, 4, POOL_GROUP // 4, POOL_GROUP).transpose(1, 0, 2, 3).reshape(4, ROWS_POOL_W, 1024),
        jnp.pad(g_pool_scale.reshape(4, 1, 256), ((0, 0), (0, ROWS_SMALL - 1), (0, 1024 - 256))),
        _replicated_rows(g_ln_g), _replicated_rows(g_ln_b),
        jnp.zeros((4, PACK0_ROWS - OFF_LN_B - ROWS_SMALL, 1024), F32),
    ]
    return jnp.concatenate(parts, axis=1).reshape(4, 2, PACK0_ROWS // 2, 1024)


def _pack_grads1(g_w_mkv, g_w_out, g_w_kv, g_bf):
    parts = _pack_common_grads(g_w_mkv, g_w_out) + [
        jnp.pad(g_w_kv.reshape(D_MODEL, 4, KV_SHARD).transpose(1, 0, 2).reshape(4, KV_SHARD, 1024),
                ((0, 0), (0, ROWS_W_KV - KV_SHARD), (0, 0))),
        _replicated_rows(jnp.pad(g_bf.reshape(1, -1), ((0, 0), (0, 1024 - g_bf.shape[0])))),
        jnp.zeros((4, PACK1_ROWS - OFF_BF - ROWS_SMALL, 1024), F32),
    ]
    return jnp.concatenate(parts, axis=1).reshape(4, 2, PACK1_ROWS // 2, 1024)


def _local_step(x, mem, target, w0, w1, ln_g, ln_b, b_forget, *, tm=256, tq=512, dist=None):
    S = x.shape[0]
    g_rows = [ln_g[l:l + 1] for l in range(2)]
    b_rows = [ln_b[l:l + 1] for l in range(2)]
    bf_row = jnp.pad(b_forget.reshape(1, -1), ((0, 0), (0, LANES - FOX_HEADS)))

    def own_slot(gathered, pack):
        return lax.dynamic_update_slice(gathered, pack[None], (dist["me"], 0, 0, 0))

    if dist is None:
        w_in0, w_mkv0, w_out0, pool_w, pool_scale = w0
        h0 = _linear_fwd(x, w_in0, tm=tm, name="in_proj0")
    else:
        w_in0, pack0 = w0
        h0, g0 = _linear_fwd(x, w_in0, tm=tm, name="in_proj0", gather=[pack0])
        w_mkv0, w_out0, pool_w, pool_scale = _unpack_weights0(
            own_slot(g0, pack0).reshape(N_CHIPS, PACK0_ROWS, 1024))
    mkv0 = _linear_fwd(mem, w_mkv0, tm=N_MEM, name="mem_kv0")
    f0 = _mix("pool", "fwd", h=h0, xres=x, mkv=mkv0, w_out=w_out0, ln_g=g_rows[0], ln_b=b_rows[0],
              pool_w=pool_w, pool_scale=pool_scale, gather=() if dist is None else w1, tm=tm)
    z0, x1 = f0["z"], f0["xout"]
    if dist is not None:
        g1, g1_in = [own_slot(f0[f"gathered{a}"], w1[a]) for a in range(2)]
        w1 = (_unpack_w_in(g1_in),) + _unpack_weights1(g1.reshape(N_CHIPS, PACK1_ROWS, 1024))
    w_in1, w_mkv1, w_out1, w_kv = w1
    w_in, w_out = [w_in0, w_in1], [w_out0, w_out1]
    w_kvp = jnp.pad(w_kv, ((0, 0), (0, LANES - FOX_HEADS)))
    mkv = [mkv0, _linear_fwd(mem, w_mkv1, tm=N_MEM, name="mem_kv1")]
    ka, va, fl, cum = _kv_proj(x1, w_kvp, bf_row, tm=tm)
    h1, qf = _linear_fwd(x1, w_in[1], tm=tm, name="in_proj1", q_cum=cum)
    ymain1, qb = _fox_fwd(qf, ka, va, tq=tq)

    b1 = _mix("fox", "bwd", h=h1, xres=x1, mkv=mkv[1], w_out=w_out[1], ln_g=g_rows[1], ln_b=b_rows[1],
              ymain=ymain1, target=target, tm=tm)
    dq_aug, dk, dv, dck_rows = _fox_bwd(qb, ka, va, b1["dmain"], tq=tq)
    du1, df, dbf = _gate_bwd(dq_aug, _cols_of_rows(dck_rows, S), fl, tm=tm)

    dx1a, dw_in1 = _lin_bwd(x1, [du1, b1["drest"]], w_in[1], [(b1["dz"], ALPHA)], tm=tm, name="in_proj1_bwd")
    dx1, dw_kvp = _lin_bwd(x1, [dk, dv, df], w_kvp, [(dx1a, 1.0)], tm=tm, name="kv_proj_bwd")

    dw_mkv1 = _wgrad(mem, b1["dmkv"], name="mem_kv1_bwd")
    g_w_kv, g_bf = dw_kvp[:, 0:KV_COLS], dbf[0, 0:FOX_HEADS]

    gpacks1, psums1 = (), ()
    if dist is not None:
        gpacks1 = [_pack_grads1(dw_mkv1, b1["dw_out"], g_w_kv, g_bf), _w_in_grad_halves(dw_in1)]
    b0 = _mix("pool", "bwd", h=h0, mkv=mkv[0], w_out=w_out[0], ln_g=g_rows[0],
              pool_w=pool_w, pool_scale=pool_scale, z=z0, dy=dx1, pair_send=gpacks1, tm=tm)
    if dist is not None:
        psums1 = [_pair_sum(g, b0[f"from_sibling{a}"], dist["c_arr"], f"1{'ab'[a]}") for a, g in enumerate(gpacks1)]
    outs = _lin_bwd(x, [b0["dmain"], b0["drest"]], w_in[0], [(b0["dz"], ALPHA)], tm=tm, name="in_proj0_bwd",
                    scatter=psums1)
    dx, dw_in0 = outs[0], outs[1]
    dw_mkv0 = _wgrad(mem, b0["dmkv"], name="mem_kv0_bwd")
    g_ln_g = jnp.concatenate([b0["dln_g"], b1["dln_g"]], axis=0)
    g_ln_b = jnp.concatenate([b0["dln_b"], b1["dln_b"]], axis=0)

    if dist is None:
        grads = dict(w_in=[dw_in0, dw_in1], w_mem_kv=[dw_mkv0, dw_mkv1], w_out=[b0["dw_out"], b1["dw_out"]],
                     ln_g=g_ln_g, ln_b=g_ln_b, pool_w=b0["dpool_w"], pool_scale=b0["dpool_scale"],
                     w_kv=g_w_kv, b_forget=g_bf)
        return b1["loss"], dx, grads

    me, my_c = dist["me"], dist["my_c"]

    def with_own(pieces, psum):
        own = lax.dynamic_slice(psum, (me, 0, 0), (1,) + psum.shape[1:])
        return lax.dynamic_update_slice(pieces, own, (me, 0, 0))

    totals1 = [_sum_pieces(with_own(outs[2 + a], p), f"1{'ab'[a]}") for a, p in enumerate(psums1)]
    gpacks0 = [_pack_grads0(dw_mkv0, b0["dw_out"], b0["dpool_w"], b0["dpool_scale"], g_ln_g, g_ln_b),
               _w_in_grad_halves(dw_in0)]
    sib0 = _send_half_to_sibling(gpacks0, 0)
    psums0 = [_pair_sum(g, sib0[a], dist["c_arr"], f"0{'ab'[a]}") for a, g in enumerate(gpacks0)]
    loss_row = jnp.broadcast_to(0.5 / D_MODEL * jnp.sum(b1["loss"]), (8, LANES))
    pieces0 = _scatter_pieces(psums0, 0, share=loss_row)
    losses = lax.dynamic_update_slice(pieces0[2], loss_row[None], (2 * me + my_c, 0, 0))
    loss = jnp.sum(losses[:, 0, 0])
    totals0 = [_sum_pieces(with_own(pieces0[a], p), f"0{'ab'[a]}") for a, p in enumerate(psums0)]
    totals = totals0 + totals1
    halves = _exchange_halves(totals)
    full = [lax.dynamic_update_slice(h, t[None], (my_c, 0, 0)) for h, t in zip(halves, totals)]
    shard0, shard1 = full[0].reshape(PACK0_ROWS, 1024), full[2].reshape(PACK1_ROWS, 1024)
    g_w_in = jnp.stack([full[1].reshape(D_MODEL, W_IN_SHARD), full[3].reshape(D_MODEL, W_IN_SHARD)])
    return loss, dx, shard0, shard1, g_w_in


def kernel(x, mem, w_in, w_mem_kv, w_out, ln_g, ln_b, pool_w, pool_scale, w_kv_shared, b_forget, loss_target, m_w_in, m_w_mem_kv, m_w_out, m_ln_g, m_ln_b, m_pool_w, m_pool_scale, m_w_kv_shared, m_b_forget, v_w_in, v_w_mem_kv, v_w_out, v_ln_g, v_ln_b, v_pool_w, v_pool_scale, v_w_kv_shared, v_b_forget):
    dist = dict(c_arr=lax.axis_index("c").astype(jnp.int32).reshape(1),
                me=2 * lax.axis_index("x") + lax.axis_index("y"), my_c=lax.axis_index("c"))

    wpacks0, wpacks1 = _pack_weight_shards(w_in, w_mem_kv, w_out, pool_w, w_kv_shared, pool_scale)
    g0_in = lax.dynamic_update_slice(_all_gather_shards([wpacks0[1]])[0], wpacks0[1][None], (dist["me"], 0, 0, 0))
    w0 = (_unpack_w_in(g0_in), wpacks0[0])

    loss, dx, shard0, shard1, g_w_in = _local_step(x[0], mem[0], loss_target[0], w0, wpacks1, ln_g, ln_b,
                                                   b_forget, dist=dist)

    def per_layer(off, rows, shape):
        return jnp.concatenate([shard0[off:off + rows], shard1[off:off + rows]], axis=0).reshape(shape)

    g_w_mkv = per_layer(OFF_W_MKV, ROWS_W_MKV, w_mem_kv.shape)
    g_w_out = per_layer(OFF_W_OUT, ROWS_W_OUT, w_out.shape)
    g_pool_w = shard0[OFF_POOL_W:OFF_POOL_W + ROWS_POOL_W].reshape(pool_w.shape)
    g_w_kv = shard1[OFF_W_KV:OFF_W_KV + KV_SHARD].reshape(w_kv_shared.shape)
    g_pool_scale = shard0[OFF_POOL_S:OFF_POOL_S + 1, 0:256].reshape(pool_scale.shape)
    g_ln_g = shard0[OFF_LN_G:OFF_LN_G + 2]
    g_ln_b = shard0[OFF_LN_B:OFF_LN_B + 2]
    g_bf = shard1[OFF_BF, 0:FOX_HEADS]

    names = ["w_in", "w_mem_kv", "w_out", "ln_g", "ln_b", "pool_w", "pool_scale", "w_kv_shared", "b_forget"]
    ws = [w_in, w_mem_kv, w_out, ln_g, ln_b, pool_w, pool_scale, w_kv_shared, b_forget]
    gs = [g_w_in, g_w_mkv, g_w_out, g_ln_g, g_ln_b, g_pool_w, g_pool_scale, g_w_kv, g_bf]
    ms = [m_w_in, m_w_mem_kv, m_w_out, m_ln_g, m_ln_b, m_pool_w, m_pool_scale, m_w_kv_shared, m_b_forget]
    vs = [v_w_in, v_w_mem_kv, v_w_out, v_ln_g, v_ln_b, v_pool_w, v_pool_scale, v_w_kv_shared, v_b_forget]
    deltas, new_ms, new_vs = [], [], []
    for nm, w, gg, mm, vv in zip(names, ws, gs, ms, vs):
        two_d = (-1, w.shape[-1])
        d, nmm, nvv = _adamw(w.reshape(two_d), gg.reshape(two_d), mm.reshape(two_d), vv.reshape(two_d),
                             name=f"adamw_{nm}")
        deltas.append(d.reshape(w.shape))
        new_ms.append(nmm.reshape(w.shape))
        new_vs.append(nvv.reshape(w.shape))

    return (loss, dx[None], *gs, *deltas, *new_ms, *new_vs)
```

```python
import jax
import jax.numpy as jnp
from jax import lax
from jax.experimental import pallas as pl
from jax.experimental.pallas import tpu as pltpu

F32 = jnp.float32
BF16 = jnp.bfloat16

D_MODEL = 1024
D_MAIN = 1024
D_MEM = 512
D_MIX = D_MAIN + D_MEM
D_IN = 2 * D_MIX
N_MEM = 256
MEM_HEADS = 4
MEM_HEAD_DIM = 128
FOX_HEADS = 16
FOX_HEAD_DIM = 64
FOX_SCALE = 0.125
POOL_WINDOWS = (2, 4, 8, 16)
POOL_GROUP = 256
POOL_HALO = 16
ALPHA = 4.0 ** 0.25
LN_EPS = 1e-5
LANES = 128
N_CHIPS = 4

ADAM_LR = 0.001
ADAM_B1 = 0.9
ADAM_B2 = 0.999
ADAM_EPS = 1e-08
ADAM_WD = 0.01
ADAM_STEP = 10

V7X_VMEM_BYTES = 64 * 1024 * 1024
VMEM_LIMIT = V7X_VMEM_BYTES - 8 * 1024 * 1024

ROWS_W_MKV = (D_MODEL // N_CHIPS) * 2 * D_MEM // 1024
ROWS_W_OUT = (D_MIX // N_CHIPS) * D_MODEL // 1024
ROWS_POOL_W = 4 * (POOL_GROUP // N_CHIPS) * POOL_GROUP // 1024
KV_COLS = 2 * D_MAIN + FOX_HEADS
KV_SHARD = KV_COLS // N_CHIPS
ROWS_W_KV = 528
ROWS_SMALL = 16
OFF_W_MKV = 0
OFF_W_OUT = OFF_W_MKV + ROWS_W_MKV
OFF_TAIL = OFF_W_OUT + ROWS_W_OUT
OFF_POOL_W = OFF_TAIL
OFF_POOL_S = OFF_POOL_W + ROWS_POOL_W
OFF_LN_G = OFF_POOL_S + ROWS_SMALL
OFF_LN_B = OFF_LN_G + ROWS_SMALL
PACK0_ROWS = 768
OFF_W_KV = OFF_TAIL
OFF_BF = OFF_W_KV + ROWS_W_KV
PACK1_ROWS = 1280
PACK_TILE = 128
W_IN_SHARD = D_IN // N_CHIPS


def _dot(a, b):
    return jnp.dot(a, b, preferred_element_type=F32)


def _dot_nt(a, b):
    return lax.dot_general(a, b, (((1,), (1,)), ((), ())), preferred_element_type=F32)


def _dot_tn(a, b):
    return lax.dot_general(a, b, (((0,), (0,)), ((), ())), preferred_element_type=F32)


def _params(n_axes=1):
    return pltpu.CompilerParams(dimension_semantics=("arbitrary",) * n_axes,
                                vmem_limit_bytes=VMEM_LIMIT)


def _const_spec(shape):
    zeros = (0,) * len(shape)
    return pl.BlockSpec(shape, lambda *_: zeros, pipeline_mode=pl.Buffered(1))


def _split3(x):
    hi = x.astype(BF16)
    r = x - hi.astype(F32)
    mid = r.astype(BF16)
    lo = (r - mid.astype(F32)).astype(BF16)
    return hi, mid, lo


def _cols_and_chunk(w):
    if w.ndim == 3:
        return N_CHIPS * w.shape[2], 256
    return w.shape[1], (512 if w.shape[1] % 512 == 0 else LANES)


def _w_cols(w_ref, n0, nc):
    if len(w_ref.shape) == 3:
        per = w_ref.shape[2]
        assert n0 // per == (n0 + nc - 1) // per
        return w_ref.at[n0 // per, :, n0 % per:n0 % per + nc]
    return w_ref.at[:, n0:n0 + nc]


def _linear_fwd(x, w, *, tm, name, q_cum=None, gather=()):
    S, K = x.shape
    N, nc = _cols_and_chunk(w)
    aug = q_cum is not None
    ng = len(gather)
    n_tiles = S // tm
    assert not (aug and ng)

    def body(*refs):
        x_ref, w_ref = refs[0], refs[1]
        o_ref = refs[4] if aug else refs[2 + ng]
        if ng:
            i = pl.program_id(0)
            start, forward, finish = _gather_steps(refs[2:2 + ng], refs[3 + ng:3 + 2 * ng],
                                                   refs[3 + 2 * ng], refs[4 + 2 * ng])
            pl.when(i == 0)(start)
            pl.when(i == n_tiles - 3)(forward)
            pl.when(i == n_tiles - 1)(finish)
        xb = x_ref[...].astype(BF16)
        if aug:
            lhs = _placement_lhs(refs[2][...], tm)
        for n0 in range(0, N, nc):
            r = _dot(xb, _w_cols(w_ref, n0, nc)[...])
            o_ref[:, n0:n0 + nc] = r.astype(BF16)
            if aug and n0 < D_MAIN:
                _store_augmented(refs[5], r * FOX_SCALE, n0 // LANES, lhs, refs[3])

    in_specs = [pl.BlockSpec((tm, K), lambda i: (i, 0)), _const_spec(w.shape)]
    out_specs = [pl.BlockSpec((tm, N), lambda i: (i, 0))]
    out_shape = [jax.ShapeDtypeStruct((S, N), BF16)]
    extra = []
    if aug:
        in_specs += [pl.BlockSpec((tm, LANES), lambda i: (i, 0)), _const_spec((LANES, AUG_W))]
        out_specs.append(pl.BlockSpec((tm, AUG_W), lambda i: (i, 0)))
        out_shape.append(jax.ShapeDtypeStruct((S, AUG_W), BF16))
        extra = [q_cum, _placement(AUG_A, (AUG_B,))]
    if ng:
        in_specs += [_ANY] * ng
        out_specs += [_ANY] * ng
        out_shape += [_gathered_shape(p) for p in gather]
        extra = list(gather)
    outs = pl.pallas_call(
        body, name=name, grid=(n_tiles,),
        in_specs=in_specs, out_specs=out_specs, out_shape=out_shape,
        scratch_shapes=_gather_sems(ng) if ng else [],
        compiler_params=_params(),
    )(x, w, *extra)
    return outs if (aug or ng) else outs[0]


def _kv_proj(x1, w_kv, bf_row, *, tm):
    S = x1.shape[0]

    def body(x_ref, w_ref, b_ref, pk_ref, pv_ref, k_ref, v_ref, fl_ref, cum_ref, carry_ref):
        i = pl.program_id(0)

        @pl.when(i == 0)
        def _():
            carry_ref[...] = jnp.zeros_like(carry_ref)

        xb = x_ref[...].astype(BF16)
        fl = _dot(xb, w_ref[:, 2 * D_MAIN:2 * D_MAIN + LANES]) + b_ref[...]
        fl_ref[...] = fl
        log_f = jnp.minimum(fl, 0.0) - jnp.log1p(jnp.exp(-jnp.abs(fl)))
        r = lax.broadcasted_iota(jnp.int32, (tm, tm), 0)
        c = lax.broadcasted_iota(jnp.int32, (tm, tm), 1)
        tri = jnp.where(c <= r, 1.0, 0.0).astype(BF16)
        hi, mid, lo = _split3(log_f)
        cum = (_dot(tri, hi) + _dot(tri, mid)) + _dot(tri, lo) + carry_ref[0:1, :]
        cum_ref[...] = cum
        carry_ref[0:1, :] = cum[tm - 1:tm, :]
        lhs_k, lhs_v = _placement_lhs(-cum, tm), _placement_lhs(None, tm)
        for n0 in range(0, D_MAIN, 512):
            _store_augmented(k_ref, _dot(xb, w_ref[:, n0:n0 + 512]), n0 // LANES, lhs_k, pk_ref)
            _store_augmented(v_ref, _dot(xb, w_ref[:, D_MAIN + n0:D_MAIN + n0 + 512]), n0 // LANES, lhs_v, pv_ref)

    return pl.pallas_call(
        body, name="kv_proj", grid=(S // tm,),
        in_specs=[pl.BlockSpec((tm, D_MODEL), lambda i: (i, 0)),
                  _const_spec((D_MODEL, 2 * D_MAIN + LANES)), _const_spec((1, LANES)),
                  _const_spec((LANES, AUG_W)), _const_spec((LANES, AUG_W))],
        out_specs=[pl.BlockSpec((tm, AUG_W), lambda i: (i, 0)),
                   pl.BlockSpec((tm, AUG_W), lambda i: (i, 0)),
                   pl.BlockSpec((tm, LANES), lambda i: (i, 0)),
                   pl.BlockSpec((tm, LANES), lambda i: (i, 0))],
        out_shape=[jax.ShapeDtypeStruct((S, AUG_W), BF16), jax.ShapeDtypeStruct((S, AUG_W), BF16),
                   jax.ShapeDtypeStruct((S, LANES), F32), jax.ShapeDtypeStruct((S, LANES), F32)],
        scratch_shapes=[pltpu.VMEM((8, LANES), F32)],
        compiler_params=_params(),
    )(x1, w_kv, bf_row, _placement(AUG_B, (AUG_A, AUG_C)), _placement(None, (AUG_A,)))


def _gate_bwd(dq_aug, dck, fl, *, tm):
    S = fl.shape[0]
    n = S // tm

    def body(dq_ref, dck_ref, fl_ref, du_ref, df_ref, db_ref, carry_ref):
        i = pl.program_id(0)

        @pl.when(i == 0)
        def _():
            carry_ref[...] = jnp.zeros_like(carry_ref)
            db_ref[...] = jnp.zeros_like(db_ref)

        lane = lax.broadcasted_iota(jnp.int32, (tm, LANES), 1)
        half0 = lane < FOX_HEAD_DIM
        dcq = jnp.zeros((tm, LANES), F32)
        for hp in range(FOX_HEADS // 2):
            b0 = dq_ref[:, 2 * hp * LANES:(2 * hp + 1) * LANES]
            b1 = dq_ref[:, (2 * hp + 1) * LANES:(2 * hp + 2) * LANES]
            du_ref[:, hp * LANES:(hp + 1) * LANES] = (
                jnp.where(half0, b0, pltpu.roll(b1, FOX_HEAD_DIM, 1)) * FOX_SCALE).astype(BF16)
            r0 = jnp.sum(jnp.where(lane == AUG_A, b0, 0.0), axis=1, keepdims=True)
            r1 = jnp.sum(jnp.where(lane == AUG_A, b1, 0.0), axis=1, keepdims=True)
            dcq = dcq + jnp.where(lane == 2 * hp, r0, 0.0) + jnp.where(lane == 2 * hp + 1, r1, 0.0)
        dcum = dcq - dck_ref[...]
        r = lax.broadcasted_iota(jnp.int32, (tm, tm), 0)
        c = lax.broadcasted_iota(jnp.int32, (tm, tm), 1)
        tri = jnp.where(c >= r, 1.0, 0.0).astype(BF16)
        hi, mid, lo = _split3(dcum)
        rev = (_dot(tri, hi) + _dot(tri, mid)) + _dot(tri, lo) + carry_ref[0:1, :]
        carry_ref[0:1, :] = rev[0:1, :]
        fl_v = fl_ref[...]
        df = rev * (1.0 / (1.0 + jnp.exp(fl_v)))
        df_ref[...] = df
        db_ref[...] += jnp.sum(df, axis=0, keepdims=True)

    return pl.pallas_call(
        body, name="gate_bwd", grid=(n,),
        in_specs=[pl.BlockSpec((tm, AUG_W), lambda i: (n - 1 - i, 0)),
                  pl.BlockSpec((tm, LANES), lambda i: (n - 1 - i, 0)),
                  pl.BlockSpec((tm, LANES), lambda i: (n - 1 - i, 0))],
        out_specs=[pl.BlockSpec((tm, D_MAIN), lambda i: (n - 1 - i, 0)),
                   pl.BlockSpec((tm, LANES), lambda i: (n - 1 - i, 0)),
                   pl.BlockSpec((1, LANES), lambda i: (0, 0))],
        out_shape=[jax.ShapeDtypeStruct((S, D_MAIN), BF16),
                   jax.ShapeDtypeStruct((S, LANES), F32), jax.ShapeDtypeStruct((1, LANES), F32)],
        scratch_shapes=[pltpu.VMEM((8, LANES), F32)],
        compiler_params=_params(),
    )(dq_aug, dck, fl)


def _silu_and_grad(g):
    sg = 1.0 / (1.0 + jnp.exp(-g))
    return g * sg, sg * (1.0 + g * (1.0 - sg))


def _mix(kind, mode, *, h, xres=None, mkv, w_out, ln_g, ln_b=None, pool_w=None, pool_scale=None,
         ymain=None, target=None, z=None, dy=None, gather=(), pair_send=(), tm):
    S = h.shape[0]
    n = S // tm
    pool = kind == "pool"
    bwd = mode == "bwd"
    loss_head = bwd and not pool
    rev = pool and bwd
    mem_scale = MEM_HEAD_DIM ** -0.5

    def t_of(i):
        return (n - 1 - i) if rev else i

    row = lambda i: (t_of(i), 0)
    names, arrays, specs = [], [], []

    def add(name, arr, spec):
        names.append(name)
        arrays.append(arr)
        specs.append(spec)

    add("h", h, pl.BlockSpec((tm, D_IN), row))
    if pool:
        hb = tm // POOL_HALO
        add("halo", h, pl.BlockSpec((POOL_HALO, D_MAIN), lambda i: (jnp.maximum(t_of(i) * hb - 1, 0), 0)))
        add("pool_w", pool_w, _const_spec((4, POOL_GROUP, POOL_GROUP)))
        add("pool_scale", pool_scale, _const_spec((1, D_MAIN)))
    else:
        add("ymain", ymain, pl.BlockSpec((tm, D_MAIN), row))
    add("mkv", mkv, _const_spec((N_MEM, 2 * D_MEM)))
    add("w_out", w_out, _const_spec((D_MIX, D_MODEL)))
    add("ln_g", ln_g, _const_spec((1, D_MODEL)))
    if not (pool and bwd):
        add("xres", xres, pl.BlockSpec((tm, D_MODEL), row))
        add("ln_b", ln_b, _const_spec((1, D_MODEL)))
    if loss_head:
        add("target", target, pl.BlockSpec((tm, D_MODEL), row))
        add("place_do", _placement(AUG_A, ()), _const_spec((LANES, AUG_W)))
    if pool and bwd:
        add("z", z, pl.BlockSpec((tm, D_MODEL), row))
        add("dy", dy, pl.BlockSpec((tm, D_MODEL), row))
    for a, p in enumerate(gather):
        add(f"gather_src{a}", p, _ANY)
    for a, p in enumerate(pair_send):
        add(f"pair_src{a}", p, _ANY)

    onames, oshapes, ospecs = [], [], []

    def add_out(name, shape, dtype, spec):
        onames.append(name)
        oshapes.append(jax.ShapeDtypeStruct(shape, dtype))
        ospecs.append(spec)

    const2 = lambda i: (0, 0)
    if not bwd:
        add_out("z", (S, D_MODEL), F32, pl.BlockSpec((tm, D_MODEL), row))
        add_out("xout", (S, D_MODEL), F32, pl.BlockSpec((tm, D_MODEL), row))
    else:
        add_out("dz", (S, D_MODEL), F32, pl.BlockSpec((tm, D_MODEL), row))
        if pool:
            add_out("dmain", (S, D_MAIN), BF16, pl.BlockSpec((tm, D_MAIN), row))
        else:
            add_out("dmain", (S, 2 * D_MAIN), BF16, pl.BlockSpec((tm, 2 * D_MAIN), row))
        add_out("drest", (S, D_IN - D_MAIN), BF16, pl.BlockSpec((tm, D_IN - D_MAIN), row))
        add_out("dw_out", (D_MIX, D_MODEL), F32, pl.BlockSpec((D_MIX, D_MODEL), const2))
        add_out("dmkv", (N_MEM, 2 * D_MEM), F32, pl.BlockSpec((N_MEM, 2 * D_MEM), const2))
        add_out("dln_g", (1, D_MODEL), F32, pl.BlockSpec((1, D_MODEL), const2))
        add_out("dln_b", (1, D_MODEL), F32, pl.BlockSpec((1, D_MODEL), const2))
        if pool:
            add_out("dpool_w", (4, POOL_GROUP, POOL_GROUP), F32,
                    pl.BlockSpec((4, POOL_GROUP, POOL_GROUP), lambda i: (0, 0, 0)))
            add_out("dpool_scale", (1, D_MAIN), F32, pl.BlockSpec((1, D_MAIN), const2))
        else:
            add_out("loss", (1, D_MODEL), F32, pl.BlockSpec((1, D_MODEL), const2))

    for a, p in enumerate(gather):
        add_out(f"gathered{a}", (N_CHIPS,) + p.shape, p.dtype, _ANY)
    for a, p in enumerate(pair_send):
        add_out(f"from_sibling{a}", (N_CHIPS,) + p.shape[2:], p.dtype, _ANY)

    scratch = [pltpu.VMEM((tm, D_MIX), BF16),
               pltpu.VMEM((tm, D_MEM), F32)]
    if pool:
        scratch.append(pltpu.VMEM((tm + 2 * POOL_HALO, D_MAIN), F32))
    if rev:
        scratch.append(pltpu.VMEM((POOL_HALO, D_MAIN), F32))
    assert not (gather and pair_send)
    if gather:
        scratch += _gather_sems(len(gather))
    if pair_send:
        scratch += _pair_sems(len(pair_send))
    n_in, n_out = len(names), len(onames)

    def body(*refs):
        R = dict(zip(names, refs[:n_in]))
        O = dict(zip(onames, refs[n_in:n_in + n_out]))
        sc = refs[n_in + n_out:]
        yc_ref, ymem_ref = sc[0], sc[1]
        ext_ref = sc[2] if pool else None
        carry_ref = sc[3] if rev else None
        i = pl.program_id(0)
        t = t_of(i)
        h_ref = R["h"]
        gamma = R["ln_g"][...]

        if gather:
            start, forward, finish = _gather_steps(
                [R[f"gather_src{a}"] for a in range(len(gather))],
                [O[f"gathered{a}"] for a in range(len(gather))], sc[-2], sc[-1])
            pl.when(i == 0)(start)
            pl.when(i == n - 3)(forward)
            pl.when(i == n - 1)(finish)
        if pair_send:
            start, finish = _pair_send_steps(
                [R[f"pair_src{a}"] for a in range(len(pair_send))],
                [O[f"from_sibling{a}"] for a in range(len(pair_send))], sc[-2], sc[-1])
            pl.when(i == 0)(start)
            pl.when(i == n - 1)(finish)

        if bwd:
            @pl.when(i == 0)
            def _():
                for nm in ("dw_out", "dmkv", "dln_g", "dln_b", "dpool_w", "dpool_scale", "loss"):
                    if nm in O:
                        O[nm][...] = jnp.zeros_like(O[nm])
                if rev:
                    carry_ref[...] = jnp.zeros_like(carry_ref)

        if pool:
            u = h_ref[:, 0:D_MAIN].astype(F32)
            halo = R["halo"][...].astype(F32)
            ext_ref[0:POOL_HALO, :] = jnp.where(t > 0, halo, 0.0)
            ext_ref[POOL_HALO:POOL_HALO + tm, :] = u
            tpos = t * tm + lax.broadcasted_iota(jnp.int32, (tm, 1), 0)
            pms, invcs = [], []
            for gi, w in enumerate(POOL_WINDOWS):
                cs = slice(gi * POOL_GROUP, (gi + 1) * POOL_GROUP)
                acc = ext_ref[POOL_HALO:POOL_HALO + tm, cs]
                for k in range(1, w):
                    acc = acc + ext_ref[POOL_HALO - k:POOL_HALO - k + tm, cs]
                invc = 1.0 / jnp.minimum(tpos + 1, w).astype(F32)
                pm = (acc * invc - u[:, cs]).astype(BF16)
                pms.append(pm)
                invcs.append(invc)
            mixed = [_dot(pms[gi], R["pool_w"][gi]) for gi in range(4)]
            ps = R["pool_scale"][...]
            y_main = [mixed[gi] * ps[:, gi * POOL_GROUP:(gi + 1) * POOL_GROUP] for gi in range(4)]
        else:
            y_main = [R["ymain"][:, gi * 256:(gi + 1) * 256].astype(F32) for gi in range(4)]

        probs = []
        for hd in range(MEM_HEADS):
            sl = slice(D_MAIN + hd * MEM_HEAD_DIM, D_MAIN + (hd + 1) * MEM_HEAD_DIM)
            ksl = slice(hd * MEM_HEAD_DIM, (hd + 1) * MEM_HEAD_DIM)
            vsl = slice(D_MEM + hd * MEM_HEAD_DIM, D_MEM + (hd + 1) * MEM_HEAD_DIM)
            s = _dot_nt(h_ref[:, sl], R["mkv"][:, ksl]) * mem_scale
            e = jnp.exp(s - jnp.max(s, axis=1, keepdims=True))
            p = e / jnp.sum(e, axis=1, keepdims=True)
            probs.append(p)
            ymem_ref[:, ksl] = _dot(p.astype(BF16), R["mkv"][:, vsl])

        g_off = D_MIX
        gate_d = []
        for gi in range(4):
            cs = slice(gi * 256, (gi + 1) * 256)
            gm = h_ref[:, g_off + gi * 256:g_off + (gi + 1) * 256].astype(F32)
            sv, sd = _silu_and_grad(gm)
            yc_ref[:, cs] = (y_main[gi] * sv).astype(BF16)
            gate_d.append((sv, sd))
        gq = h_ref[:, g_off + D_MAIN:D_IN].astype(F32)
        svq, sdq = _silu_and_grad(gq)
        yc_ref[:, D_MAIN:D_MIX] = (ymem_ref[...] * svq).astype(BF16)

        if pool and bwd:
            zt = R["z"][...]
        else:
            o = _dot(yc_ref[...], R["w_out"][...])
            zt = ALPHA * R["xres"][...] + o
        mu = jnp.mean(zt, axis=1, keepdims=True)
        zc = zt - mu
        var = jnp.mean(zc * zc, axis=1, keepdims=True)
        rstd = lax.rsqrt(var + LN_EPS)
        xhat = zc * rstd
        if not bwd:
            O["z"][...] = zt
            O["xout"][...] = xhat * gamma + R["ln_b"][...]
            return

        if loss_head:
            xo = xhat * gamma + R["ln_b"][...]
            err = xo - R["target"][...]
            O["loss"][...] += jnp.sum(err * err, axis=0, keepdims=True)
            dyt = err * (1.0 / D_MODEL)
        else:
            dyt = R["dy"][...]

        O["dln_g"][...] += jnp.sum(dyt * xhat, axis=0, keepdims=True)
        O["dln_b"][...] += jnp.sum(dyt, axis=0, keepdims=True)
        gdy = dyt * gamma
        m1 = jnp.mean(gdy, axis=1, keepdims=True)
        m2 = jnp.mean(gdy * xhat, axis=1, keepdims=True)
        dz = rstd * (gdy - m1 - xhat * m2)
        O["dz"][...] = dz
        dzb = dz.astype(BF16)

        for n0 in range(0, D_MIX, 512):
            O["dw_out"][n0:n0 + 512, :] += _dot_tn(yc_ref[:, n0:n0 + 512], dzb)
        dyc_mem = _dot_nt(dzb, R["w_out"][D_MAIN:D_MIX, :])

        O["drest"][:, D_MEM + D_MAIN:D_MEM + D_MAIN + D_MEM] = (dyc_mem * ymem_ref[...] * sdq).astype(BF16)
        dymem = dyc_mem * svq
        for hd in range(MEM_HEADS):
            sl = slice(D_MAIN + hd * MEM_HEAD_DIM, D_MAIN + (hd + 1) * MEM_HEAD_DIM)
            ksl = slice(hd * MEM_HEAD_DIM, (hd + 1) * MEM_HEAD_DIM)
            vsl = slice(D_MEM + hd * MEM_HEAD_DIM, D_MEM + (hd + 1) * MEM_HEAD_DIM)
            p = probs[hd]
            dyb = dymem[:, ksl].astype(BF16)
            dp = _dot_nt(dyb, R["mkv"][:, vsl])
            ds = p * (dp - jnp.sum(dp * p, axis=1, keepdims=True)) * mem_scale
            dsb = ds.astype(BF16)
            O["drest"][:, ksl] = _dot(dsb, R["mkv"][:, ksl]).astype(BF16)
            O["dmkv"][:, ksl] += _dot_tn(dsb, h_ref[:, sl])
            O["dmkv"][:, vsl] += _dot_tn(p.astype(BF16), dyb)

        dmain = []
        for gi in range(4):
            cs = slice(gi * 256, (gi + 1) * 256)
            dyc_g = _dot_nt(dzb, R["w_out"][cs, :])
            sv, sd = gate_d[gi]
            O["drest"][:, D_MEM + gi * 256:D_MEM + (gi + 1) * 256] = (dyc_g * y_main[gi] * sd).astype(BF16)
            dmain.append(dyc_g * sv)

        if not pool:
            dbf = [dmain[gi].astype(BF16).astype(F32) for gi in range(4)]
            dcol = jnp.zeros((tm, LANES), F32)
            for gi in range(4):
                dr = lax.broadcasted_iota(jnp.int32, (256, LANES), 0)
                hc = lax.broadcasted_iota(jnp.int32, (256, LANES), 1)
                sel = jnp.where(jnp.right_shift(dr, 6) + gi * 4 == hc, 1.0, 0.0).astype(BF16)
                hi, mid, lo = _split3(dbf[gi] * R["ymain"][:, gi * 256:(gi + 1) * 256].astype(F32))
                dcol = dcol + ((_dot(hi, sel) + _dot(mid, sel)) + _dot(lo, sel))
            lhs = _placement_lhs(-dcol, tm)
            for gi in range(4):
                _store_augmented(O["dmain"], dbf[gi], 2 * gi, lhs, R["place_do"])
            return

        ps = R["pool_scale"][...]
        dpm_list = []
        for gi in range(4):
            cs = slice(gi * 256, (gi + 1) * 256)
            O["dpool_scale"][:, cs] += jnp.sum(dmain[gi] * mixed[gi], axis=0, keepdims=True)
            dmix = (dmain[gi] * ps[:, cs]).astype(BF16)
            O["dpool_w"][gi] += _dot_tn(pms[gi], dmix)
            dpm = _dot_nt(dmix, R["pool_w"][gi])
            dpm_list.append(dpm)
            ext_ref[0:tm, cs] = dpm * invcs[gi]
        ext_ref[tm:tm + POOL_HALO, :] = carry_ref[...]
        carry_ref[...] = ext_ref[0:POOL_HALO, :]
        for gi, w in enumerate(POOL_WINDOWS):
            cs = slice(gi * 256, (gi + 1) * 256)
            acc = ext_ref[0:tm, cs]
            for k in range(1, w):
                acc = acc + ext_ref[k:k + tm, cs]
            O["dmain"][:, cs] = (acc - dpm_list[gi]).astype(BF16)

    outs = pl.pallas_call(
        body, name=f"mix_{kind}_{mode}", grid=(n,),
        in_specs=specs, out_specs=ospecs, out_shape=oshapes,
        scratch_shapes=scratch, compiler_params=_params(),
    )(*arrays)
    return dict(zip(onames, outs))


def _lin_bwd(xin, dhs, w, res, *, tm, name, scatter=()):
    S, K = xin.shape
    N, nc_w = _cols_and_chunk(w)
    nj, nr, ns = len(dhs), len(res), len(scatter)
    n_tiles = S // tm
    widths = [dh.shape[1] for dh in dhs]
    chunks = [nc_w if w.ndim == 3 else (512 if wd % 512 == 0 else wd) for wd in widths]
    assert sum(widths) == N and all(wd % c == 0 for wd, c in zip(widths, chunks))
    scales = [s for _, s in res]
    n_in = 2 + nj + nr + ns

    def body(*refs):
        x_ref = refs[0]
        dh_refs = refs[1:1 + nj]
        w_ref = refs[1 + nj]
        r_refs = refs[2 + nj:2 + nj + nr]
        dx_ref, dw_ref = refs[n_in], refs[n_in + 1]
        i = pl.program_id(0)

        if ns:
            start, finish = _scatter_steps(refs[n_in - ns:n_in], refs[n_in + 2:n_in + 2 + ns],
                                           refs[n_in + 2 + ns], refs[n_in + 3 + ns])
            pl.when(i == 0)(start)
            pl.when(i == n_tiles - 1)(finish)

        @pl.when(i == 0)
        def _():
            dw_ref[...] = jnp.zeros_like(dw_ref)

        xb = x_ref[...].astype(BF16)
        dx = jnp.zeros((tm, K), F32)
        for r_ref, sc in zip(r_refs, scales):
            dx = dx + sc * r_ref[...]
        off = 0
        for j in range(nj):
            nc = chunks[j]
            for n0 in range(0, widths[j], nc):
                dhb = dh_refs[j][:, n0:n0 + nc].astype(BF16)
                dx = dx + _dot_nt(dhb, _w_cols(w_ref, off + n0, nc)[...])
                _w_cols(dw_ref, off + n0, nc)[...] += _dot_tn(xb, dhb)
            off += widths[j]
        dx_ref[...] = dx

    zeros = (0,) * w.ndim
    in_specs = [pl.BlockSpec((tm, K), lambda i: (i, 0))]
    in_specs += [pl.BlockSpec((tm, n), lambda i: (i, 0)) for n in widths]
    in_specs += [_const_spec(w.shape)]
    in_specs += [pl.BlockSpec((tm, K), lambda i: (i, 0)) for _ in res]
    in_specs += [_ANY] * ns
    out_specs = [pl.BlockSpec((tm, K), lambda i: (i, 0)), pl.BlockSpec(w.shape, lambda i: zeros)] + [_ANY] * ns
    out_shape = [jax.ShapeDtypeStruct((S, K), F32), jax.ShapeDtypeStruct(w.shape, F32)]
    out_shape += [jax.ShapeDtypeStruct(p.shape, p.dtype) for p in scatter]
    return pl.pallas_call(
        body, name=name, grid=(n_tiles,),
        in_specs=in_specs, out_specs=out_specs, out_shape=out_shape,
        scratch_shapes=_scatter_sems(ns) if ns else [],
        compiler_params=_params(),
    )(xin, *dhs, w, *[r for r, _ in res], *scatter)


def _wgrad(xin, dh, *, name):
    M, K = xin.shape
    N = dh.shape[1]

    def body(x_ref, dh_ref, o_ref):
        o_ref[...] = _dot_tn(x_ref[...].astype(BF16), dh_ref[...].astype(BF16))

    return pl.pallas_call(
        body, name=name, out_shape=jax.ShapeDtypeStruct((K, N), F32),
        compiler_params=pltpu.CompilerParams(vmem_limit_bytes=VMEM_LIMIT),
    )(xin, dh)


AUG_A = FOX_HEAD_DIM
AUG_B = FOX_HEAD_DIM + 3
AUG_C = FOX_HEAD_DIM + 6
AUG_W = FOX_HEADS * LANES


def _placement(val_lane, ones_lanes):
    r = jnp.arange(LANES)[:, None]
    c = jnp.arange(AUG_W)[None, :]
    head, lane = c // LANES, c % LANES
    m = jnp.zeros((LANES, AUG_W), jnp.bool_)
    if val_lane is not None:
        for part in range(3):
            m = m | ((r == part * FOX_HEADS + head) & (lane == val_lane + part))
    for first in ones_lanes:
        m = m | ((r == 3 * FOX_HEADS) & (lane >= first) & (lane < first + 3))
    return m.astype(BF16)


def _placement_lhs(val, tm):
    lane = lax.broadcasted_iota(jnp.int32, (tm, LANES), 1)
    lhs = jnp.where(lane == 3 * FOX_HEADS, 1.0, 0.0)
    if val is not None:
        hi, mid, lo = [p.astype(F32) for p in _split3(val)]
        lhs = jnp.where(lane < FOX_HEADS, hi, jnp.where(
            lane < 2 * FOX_HEADS, pltpu.roll(mid, FOX_HEADS, 1), jnp.where(
                lane < 3 * FOX_HEADS, pltpu.roll(lo, 2 * FOX_HEADS, 1), lhs)))
    return lhs.astype(BF16)


def _store_augmented(o_ref, data, first_pair, lhs, p_ref):
    tm = data.shape[0]
    is_data = lax.broadcasted_iota(jnp.int32, (tm, LANES), 1) < FOX_HEAD_DIM
    for j in range(data.shape[1] // LANES):
        base = 2 * (first_pair + j) * LANES
        extra = _dot(lhs, p_ref[:, base:base + 2 * LANES])
        blk = data[:, j * LANES:(j + 1) * LANES]
        o_ref[:, base:base + LANES] = jnp.where(is_data, blk, extra[:, 0:LANES]).astype(BF16)
        o_ref[:, base + LANES:base + 2 * LANES] = jnp.where(
            is_data, pltpu.roll(blk, FOX_HEAD_DIM, 1), extra[:, LANES:2 * LANES]).astype(BF16)


def _cols_of_rows(rows, S):
    nh = FOX_HEADS // rows.shape[0]
    a = rows[:, :, 0:nh, :].transpose(0, 2, 1, 3).reshape(FOX_HEADS, S).T
    return jnp.pad(a, ((0, 0), (0, LANES - FOX_HEADS)))


def _fox_fwd(qf, ka, va, *, tq, nh=4):
    S = ka.shape[0]
    nq = S // tq
    tk = tq
    ng = FOX_HEADS // nh

    def body(q_ref, k_ref, v_ref, o_ref, qb_ref, *scratch):
        p_scr, m_scr, acc_scr = scratch[0:nh], scratch[nh:2 * nh], scratch[2 * nh:3 * nh]
        qi = pl.program_id(1)
        lane = lax.broadcasted_iota(jnp.int32, (tq, LANES), 1)
        half0 = lane < FOX_HEAD_DIM
        rr = lax.broadcasted_iota(jnp.int32, (tq, tk), 0)
        cc = lax.broadcasted_iota(jnp.int32, (tq, tk), 1)
        sls = [slice(hh * LANES, (hh + 1) * LANES) for hh in range(nh)]
        qs = [q_ref[:, sl] for sl in sls]

        @pl.when(qi == 0)
        def _():
            for hh in range(nh):
                p_scr[hh][...] = jnp.zeros_like(p_scr[hh])
                m_scr[hh][...] = jnp.full(m_scr[hh].shape, -jnp.inf, F32)
                acc_scr[hh][...] = jnp.zeros_like(acc_scr[hh])

        @pl.when(qi > 0)
        def _():
            for hh in range(nh):
                s = _dot_nt(qs[hh], k_ref[0:tk, sls[hh]])
                m0 = jnp.broadcast_to(jnp.max(s, axis=1, keepdims=True), (tq, LANES))
                p_scr[hh][...] = jnp.exp(s - jnp.tile(m0, (1, tk // LANES))).astype(BF16)
                m_scr[hh][...] = m0
                acc_scr[hh][...] = jnp.zeros_like(acc_scr[hh])

        def chunk(ki, masked):
            k0 = pl.multiple_of(ki * tk, tk)
            kp = pl.multiple_of(jnp.maximum(ki - 1, 0) * tk, tk)
            for hh in range(nh):
                m = m_scr[hh][...]
                s = _dot_nt(qs[hh], k_ref[pl.ds(k0, tk), sls[hh]])
                pv = _dot(p_scr[hh][...], v_ref[pl.ds(kp, tk), sls[hh]])
                if masked:
                    s = jnp.where(cc <= rr, s, -jnp.inf)
                m_new = jnp.maximum(m, jnp.max(s, axis=1, keepdims=True))
                p_scr[hh][...] = jnp.exp(s - jnp.tile(m_new, (1, tk // LANES))).astype(BF16)
                acc_scr[hh][...] = (acc_scr[hh][...] + pv) * jnp.exp(m - m_new)
                m_scr[hh][...] = m_new

        def trip(ki, c):
            chunk(ki, False)
            return c

        lax.fori_loop(1, qi, trip, 0)
        chunk(qi, True)
        kq = pl.multiple_of(qi * tk, tk)
        outs = []
        for hh in range(nh):
            m = m_scr[hh][...]
            acc = acc_scr[hh][...] + _dot(p_scr[hh][...], v_ref[pl.ds(kq, tk), sls[hh]])
            l = jnp.sum(jnp.where(lane == AUG_A, acc, 0.0), axis=1, keepdims=True)
            outs.append(acc / l)
            hi, mid, lo = _split3(-(m + jnp.log(l)))
            qb_ref[:, sls[hh]] = jnp.where(lane == AUG_C, hi, jnp.where(
                lane == AUG_C + 1, mid, jnp.where(lane == AUG_C + 2, lo, qs[hh])))
        for pr in range(nh // 2):
            o_ref[:, pr * LANES:(pr + 1) * LANES] = jnp.where(
                half0, outs[2 * pr], pltpu.roll(outs[2 * pr + 1], FOX_HEAD_DIM, 1))

    return pl.pallas_call(
        body, name="fox_fwd", grid=(ng, nq),
        in_specs=[pl.BlockSpec((tq, nh * LANES), lambda g, qi: (qi, g)),
                  pl.BlockSpec((S, nh * LANES), lambda g, qi: (0, g), pipeline_mode=pl.Buffered(1)),
                  pl.BlockSpec((S, nh * LANES), lambda g, qi: (0, g), pipeline_mode=pl.Buffered(1))],
        out_specs=[pl.BlockSpec((tq, nh * FOX_HEAD_DIM), lambda g, qi: (qi, g)),
                   pl.BlockSpec((tq, nh * LANES), lambda g, qi: (qi, g))],
        out_shape=[jax.ShapeDtypeStruct((S, D_MAIN), F32),
                   jax.ShapeDtypeStruct((S, AUG_W), BF16)],
        scratch_shapes=([pltpu.VMEM((tq, tk), BF16)] * nh + [pltpu.VMEM((tq, LANES), F32)] * nh
                        + [pltpu.VMEM((tq, LANES), F32)] * nh),
        compiler_params=_params(2),
    )(qf, ka, va)


def _fox_bwd(qb, ka, va, do_aug, *, tq):
    S = ka.shape[0]
    nq = S // tq
    tk = tq

    def body(k_ref, v_ref, q_ref, do_ref, dq_ref, dk_ref, dv_ref, dck_ref, dk_scr, dv_scr):
        kj = pl.program_id(1)

        @pl.when(kj == 0)
        def _():
            dq_ref[...] = jnp.zeros_like(dq_ref)

        lane = lax.broadcasted_iota(jnp.int32, (tk, LANES), 1)
        half0 = lane < FOX_HEAD_DIM
        rr = lax.broadcasted_iota(jnp.int32, (tk, tq), 0)
        cc = lax.broadcasted_iota(jnp.int32, (tk, tq), 1)
        sls = [slice(hh * LANES, (hh + 1) * LANES) for hh in range(2)]
        kts = [k_ref[:, sl] for sl in sls]
        vts = [v_ref[:, sl] for sl in sls]

        dk_scr[...] = jnp.zeros_like(dk_scr)
        dv_scr[...] = jnp.zeros_like(dv_scr)

        def chunk(qi, masked):
            q0 = pl.multiple_of(qi * tq, tq)
            for hh in range(2):
                qc = q_ref[pl.ds(q0, tq), sls[hh]]
                doc = do_ref[pl.ds(q0, tq), sls[hh]]
                pt = jnp.exp(_dot_nt(kts[hh], qc))
                if masked:
                    pt = jnp.where(rr <= cc, pt, 0.0)
                dsb = (pt * _dot_nt(vts[hh], doc)).astype(BF16)
                dv_scr[hh] += _dot(pt.astype(BF16), doc)
                dk_scr[hh] += _dot(dsb, qc)
                dq_ref[pl.ds(q0, tq), sls[hh]] += _dot_tn(dsb, kts[hh])

        def trip(qi, c):
            chunk(qi, False)
            return c

        chunk(kj, True)
        lax.fori_loop(kj + 1, nq, trip, 0)
        dk0, dk1 = dk_scr[0], dk_scr[1]
        dv0, dv1 = dv_scr[0], dv_scr[1]
        dk_ref[...] = jnp.where(half0, dk0, pltpu.roll(dk1, FOX_HEAD_DIM, 1)).astype(BF16)
        dv_ref[...] = jnp.where(half0, dv0, pltpu.roll(dv1, FOX_HEAD_DIM, 1)).astype(BF16)
        c0 = jnp.sum(jnp.where(lane == AUG_B, dk0, 0.0), axis=1, keepdims=True)
        c1 = jnp.sum(jnp.where(lane == AUG_B, dk1, 0.0), axis=1, keepdims=True)
        dck_cols = jnp.where(lane == 0, c0, 0.0) + jnp.where(lane == 1, c1, 0.0)
        dck_ref[0, 0] = dck_cols.T[0:8, :]

    return pl.pallas_call(
        body, name="fox_bwd", grid=(8, nq),
        in_specs=[pl.BlockSpec((tk, 2 * LANES), lambda hp, kj: (kj, hp)),
                  pl.BlockSpec((tk, 2 * LANES), lambda hp, kj: (kj, hp)),
                  pl.BlockSpec((S, 2 * LANES), lambda hp, kj: (0, hp)),
                  pl.BlockSpec((S, 2 * LANES), lambda hp, kj: (0, hp))],
        out_specs=[pl.BlockSpec((S, 2 * LANES), lambda hp, kj: (0, hp)),
                   pl.BlockSpec((tk, LANES), lambda hp, kj: (kj, hp)),
                   pl.BlockSpec((tk, LANES), lambda hp, kj: (kj, hp)),
                   pl.BlockSpec((1, 1, 8, tk), lambda hp, kj: (hp, kj, 0, 0))],
        out_shape=[jax.ShapeDtypeStruct((S, AUG_W), F32),
                   jax.ShapeDtypeStruct((S, D_MAIN), BF16),
                   jax.ShapeDtypeStruct((S, D_MAIN), BF16),
                   jax.ShapeDtypeStruct((8, nq, 8, tk), F32)],
        scratch_shapes=[pltpu.VMEM((2, tk, LANES), F32), pltpu.VMEM((2, tk, LANES), F32)],
        compiler_params=_params(2),
    )(ka, va, qb, do_aug)


def _adamw(w, g, m, v, *, name):
    Rr, C = w.shape
    tr = 256 if Rr % 256 == 0 else Rr
    c1 = 1.0 / (1.0 - ADAM_B1 ** ADAM_STEP)
    c2 = 1.0 / (1.0 - ADAM_B2 ** ADAM_STEP)

    def body(w_ref, g_ref, m_ref, v_ref, d_ref, nm_ref, nv_ref):
        gv = g_ref[...]
        nm = ADAM_B1 * m_ref[...] + (1.0 - ADAM_B1) * gv
        nv = ADAM_B2 * v_ref[...] + (1.0 - ADAM_B2) * (gv * gv)
        d_ref[...] = -ADAM_LR * ((nm * c1) / (jnp.sqrt(nv * c2) + ADAM_EPS) + ADAM_WD * w_ref[...])
        nm_ref[...] = nm
        nv_ref[...] = nv

    spec = pl.BlockSpec((tr, C), lambda i: (i, 0))
    sds = jax.ShapeDtypeStruct((Rr, C), F32)
    return pl.pallas_call(
        body, name=name, grid=(Rr // tr,),
        in_specs=[spec] * 4, out_specs=[spec] * 3, out_shape=[sds] * 3,
        compiler_params=_params(),
    )(w, g, m, v)


_ANY = pl.BlockSpec(memory_space=pl.ANY)
_MESH = pl.DeviceIdType.MESH


def _place():
    x, y, c = lax.axis_index("x"), lax.axis_index("y"), lax.axis_index("c")
    return x, y, c


def _gather_steps(p_refs, out_refs, send_sems, recv_sems):
    x, y, c = _place()
    sib = (x, y, 1 - c)
    chips = [(1 - x, y), (x, 1 - y), (1 - x, 1 - y)]
    idx = [2 * chip[0] + chip[1] for chip in chips]
    me = 2 * x + y
    na = len(p_refs)

    def copy(a, k, chip_idx, half, to, src=None):
        dst = out_refs[a].at[chip_idx, half]
        return pltpu.make_async_remote_copy(
            src_ref=dst if src is None else src, dst_ref=dst,
            send_sem=send_sems.at[6 * a + k], recv_sem=recv_sems.at[6 * a + k],
            device_id=to, device_id_type=_MESH)

    first = [copy(a, j, me, c, (*chips[j], c), src=p_refs[a].at[c]) for a in range(na) for j in range(3)]
    passed = [copy(a, 3 + j, idx[j], c, sib) for a in range(na) for j in range(3)]

    def start():
        for cp in first:
            cp.start()

    def forward():
        for a in range(na):
            for j in range(3):
                copy(a, j, idx[j], c, sib).wait_recv()
                passed[3 * a + j].start()

    def finish():
        for a in range(na):
            for j in range(3):
                copy(a, 3 + j, idx[j], 1 - c, sib).wait_recv()
        for cp in first + passed:
            cp.wait_send()

    return start, forward, finish


def _sems(n):
    return [pltpu.SemaphoreType.DMA((n,)), pltpu.SemaphoreType.DMA((n,))]


def _gather_sems(na):
    return _sems(6 * na)


def _scatter_sems(na):
    return _sems(3 * na)


def _pair_sems(na):
    return _sems(N_CHIPS * na)


def _gathered_shape(pack):
    return jax.ShapeDtypeStruct((N_CHIPS,) + pack.shape, pack.dtype)


def _from_sibling_shape(gpack):
    return jax.ShapeDtypeStruct((N_CHIPS,) + gpack.shape[2:], gpack.dtype)


def _all_gather_shards(packs):
    na = len(packs)

    def body(*refs):
        for step in _gather_steps(refs[0:na], refs[na:2 * na], refs[2 * na], refs[2 * na + 1]):
            step()

    return pl.pallas_call(
        body, name="all_gather_shards",
        in_specs=[_ANY] * na, out_specs=[_ANY] * na, out_shape=[_gathered_shape(p) for p in packs],
        scratch_shapes=_gather_sems(na),
    )(*packs)


def _pair_send_steps(g_refs, out_refs, send_sem, recv_sem):
    x, y, c = _place()
    cps = [pltpu.make_async_remote_copy(
        src_ref=g_refs[a].at[j, 1 - c], dst_ref=out_refs[a].at[j],
        send_sem=send_sem.at[N_CHIPS * a + j], recv_sem=recv_sem.at[N_CHIPS * a + j],
        device_id=(x, y, 1 - c), device_id_type=_MESH) for a in range(len(g_refs)) for j in range(N_CHIPS)]

    def start():
        for cp in cps:
            cp.start()

    def finish():
        for cp in cps:
            cp.wait_recv()
        for cp in cps:
            cp.wait_send()

    return start, finish


def _send_half_to_sibling(gpacks, tag):
    na = len(gpacks)

    def body(*refs):
        for step in _pair_send_steps(refs[0:na], refs[na:2 * na], refs[2 * na], refs[2 * na + 1]):
            step()

    return pl.pallas_call(
        body, name=f"pair_send{tag}",
        in_specs=[_ANY] * na, out_specs=[_ANY] * na, out_shape=[_from_sibling_shape(g) for g in gpacks],
        scratch_shapes=_pair_sems(na),
    )(*gpacks)


def _pair_sum(gpack, recv, c_arr, tag, *, tr=PACK_TILE):
    rows, lanes = recv.shape[1:]
    assert rows % tr == 0

    def body(c_ref, a_ref, b_ref, o_ref):
        o_ref[...] = (a_ref[...] + b_ref[...]).astype(BF16)

    grid_spec = pltpu.PrefetchScalarGridSpec(
        num_scalar_prefetch=1, grid=(N_CHIPS, rows // tr),
        in_specs=[pl.BlockSpec((None, None, tr, lanes), lambda j, i, c_ref: (j, c_ref[0], i, 0)),
                  pl.BlockSpec((None, tr, lanes), lambda j, i, c_ref: (j, i, 0))],
        out_specs=pl.BlockSpec((None, tr, lanes), lambda j, i, c_ref: (j, i, 0)))
    return pl.pallas_call(
        body, name=f"pair_sum{tag}", grid_spec=grid_spec,
        out_shape=jax.ShapeDtypeStruct((N_CHIPS, rows, lanes), BF16),
        compiler_params=_params(2),
    )(c_arr, gpack, recv)


def _scatter_steps(p_refs, out_refs, send_sems, recv_sems):
    x, y, c = _place()
    chips = [(1 - x, y), (x, 1 - y), (1 - x, 1 - y)]
    me = 2 * x + y
    cps = [pltpu.make_async_remote_copy(
        src_ref=p_refs[a].at[2 * chip[0] + chip[1]], dst_ref=out_refs[a].at[me],
        send_sem=send_sems.at[3 * a + j], recv_sem=recv_sems.at[3 * a + j],
        device_id=(*chip, c), device_id_type=_MESH) for a in range(len(p_refs)) for j, chip in enumerate(chips)]

    def start():
        for cp in cps:
            cp.start()

    def finish():
        for cp in cps:
            cp.wait_recv()
        for cp in cps:
            cp.wait_send()

    return start, finish


def _share_steps(row_ref, out_ref, send_sems, recv_sems):
    x, y, c = _place()
    mine = 4 * x + 2 * y + c
    cps = []
    for k in range(1, 8):
        fx, fy, fc = (k >> 2) & 1, (k >> 1) & 1, k & 1
        peer = (x + fx - 2 * x * fx, y + fy - 2 * y * fy, c + fc - 2 * c * fc)
        cps.append(pltpu.make_async_remote_copy(
            src_ref=row_ref, dst_ref=out_ref.at[mine], send_sem=send_sems.at[k - 1], recv_sem=recv_sems.at[k - 1],
            device_id=peer, device_id_type=_MESH))

    def start():
        for cp in cps:
            cp.start()

    def finish():
        for cp in cps:
            cp.wait_recv()
        for cp in cps:
            cp.wait_send()

    return start, finish


def _scatter_pieces(psums, tag, share=None):
    na = len(psums)
    ns = 0 if share is None else 1

    def body(*refs):
        n_in = na + ns
        steps = [_scatter_steps(refs[0:na], refs[n_in:n_in + na], refs[2 * n_in], refs[2 * n_in + 1])]
        if ns:
            steps.append(_share_steps(refs[na], refs[n_in + na], refs[2 * n_in + 2], refs[2 * n_in + 3]))
        for phase in range(2):
            for st in steps:
                st[phase]()

    out_shape = [jax.ShapeDtypeStruct(p.shape, p.dtype) for p in psums]
    if ns:
        out_shape.append(jax.ShapeDtypeStruct((8,) + share.shape, share.dtype))
    return pl.pallas_call(
        body, name=f"scatter_pieces{tag}",
        in_specs=[_ANY] * (na + ns), out_specs=[_ANY] * (na + ns), out_shape=out_shape,
        scratch_shapes=_scatter_sems(na) + (_sems(7) if ns else []),
    )(*psums, *([share] if ns else []))


def _sum_pieces(pieces, tag, *, tr=PACK_TILE):
    rows, lanes = pieces.shape[1:]
    assert rows % tr == 0

    def body(p_ref, o_ref):
        acc = p_ref[0].astype(F32) + p_ref[1].astype(F32)
        acc = acc + p_ref[2].astype(F32)
        o_ref[...] = acc + p_ref[3].astype(F32)

    return pl.pallas_call(
        body, name=f"sum_pieces{tag}", grid=(rows // tr,),
        in_specs=[pl.BlockSpec((N_CHIPS, tr, lanes), lambda i: (0, i, 0))],
        out_specs=pl.BlockSpec((tr, lanes), lambda i: (i, 0)),
        out_shape=jax.ShapeDtypeStruct((rows, lanes), F32),
        compiler_params=_params(),
    )(pieces)


def _exchange_halves(totals):
    n = len(totals)

    def body(*refs):
        t_refs, out_refs, send_sem, recv_sem = refs[:n], refs[n:2 * n], refs[2 * n], refs[2 * n + 1]
        x, y, c = _place()
        cps = [pltpu.make_async_remote_copy(
            src_ref=t_refs[i], dst_ref=out_refs[i].at[c], send_sem=send_sem.at[i], recv_sem=recv_sem.at[i],
            device_id=(x, y, 1 - c), device_id_type=_MESH) for i in range(n)]
        for cp in cps:
            cp.start()
        for cp in cps:
            cp.wait_recv()
        for cp in cps:
            cp.wait_send()

    return pl.pallas_call(
        body, name="exchange_halves",
        in_specs=[_ANY] * n, out_specs=[_ANY] * n,
        out_shape=[jax.ShapeDtypeStruct((2,) + t.shape, F32) for t in totals],
        scratch_shapes=[pltpu.SemaphoreType.DMA((n,)), pltpu.SemaphoreType.DMA((n,))],
    )(*totals)


def _pad_rows(a, rows):
    return jnp.pad(a, ((0, rows - a.shape[0]), (0, 0)))


def _pack_weight_shards(w_in, w_mem_kv, w_out, pool_w, w_kv_shared, pool_scale):
    ps_bits = lax.bitcast_convert_type(pool_scale.reshape(-1), BF16).reshape(1, -1)
    ps_row = jnp.pad(ps_bits, ((0, 0), (0, 1024 - ps_bits.shape[1])))

    def common(l):
        return [w_mem_kv[l].astype(BF16).reshape(ROWS_W_MKV, 1024),
                w_out[l].astype(BF16).reshape(ROWS_W_OUT, 1024)]

    p0 = common(0) + [pool_w.astype(BF16).reshape(ROWS_POOL_W, 1024), _pad_rows(ps_row, ROWS_SMALL),
                      jnp.zeros((PACK0_ROWS - OFF_LN_G, 1024), BF16)]
    p1 = common(1) + [_pad_rows(w_kv_shared.astype(BF16).reshape(KV_SHARD, 1024), ROWS_W_KV),
                      jnp.zeros((PACK1_ROWS - OFF_BF, 1024), BF16)]
    w_in_halves = w_in.astype(BF16).reshape(2, 2, D_MODEL // 2, W_IN_SHARD)
    return ([jnp.concatenate(p0, axis=0).reshape(2, PACK0_ROWS // 2, 1024), w_in_halves[0]],
            [jnp.concatenate(p1, axis=0).reshape(2, PACK1_ROWS // 2, 1024), w_in_halves[1]])


def _unpack_w_in(g_in):
    return g_in.reshape(N_CHIPS, D_MODEL, W_IN_SHARD)


def _unpack_common(g):
    w_mkv = g[:, OFF_W_MKV:OFF_W_MKV + ROWS_W_MKV].reshape(D_MODEL, 2 * D_MEM)
    w_out = g[:, OFF_W_OUT:OFF_W_OUT + ROWS_W_OUT].reshape(D_MIX, D_MODEL)
    return w_mkv, w_out


def _unpack_weights0(g):
    pool_w = g[:, OFF_POOL_W:OFF_POOL_W + ROWS_POOL_W].reshape(4, 4, POOL_GROUP // 4, POOL_GROUP)
    pool_w = pool_w.transpose(1, 0, 2, 3).reshape(4, POOL_GROUP, POOL_GROUP)
    ps_bits = g[:, OFF_POOL_S, 0:512].reshape(4, 256, 2)
    pool_scale = lax.bitcast_convert_type(ps_bits, F32).reshape(1, D_MAIN)
    return _unpack_common(g) + (pool_w, pool_scale)


def _unpack_weights1(g):
    w_kv = g[:, OFF_W_KV:OFF_W_KV + KV_SHARD].reshape(4, D_MODEL, KV_SHARD)
    w_kv = w_kv.transpose(1, 0, 2).reshape(D_MODEL, KV_COLS)
    return _unpack_common(g) + (w_kv,)


def _replicated_rows(a):
    a = _pad_rows(a, ROWS_SMALL)
    return jnp.broadcast_to(a[None], (4,) + a.shape)


def _pack_common_grads(g_w_mkv, g_w_out):
    return [g_w_mkv.reshape(4, ROWS_W_MKV, 1024), g_w_out.reshape(4, ROWS_W_OUT, 1024)]


def _w_in_grad_halves(g_w_in):
    return g_w_in.reshape(N_CHIPS, 2, D_MODEL // 2, W_IN_SHARD)


def _pack_grads0(g_w_mkv, g_w_out, g_pool_w, g_pool_scale, g_ln_g, g_ln_b):
    parts = _pack_common_grads(g_w_mkv, g_w_out) + [
        g_pool_w.reshape(4, 4, POOL_GROUP // 4, POOL_GROUP).transpose(1, 0, 2, 3).reshape(4, ROWS_POOL_W, 1024),
        jnp.pad(g_pool_scale.reshape(4, 1, 256), ((0, 0), (0, ROWS_SMALL - 1), (0, 1024 - 256))),
        _replicated_rows(g_ln_g), _replicated_rows(g_ln_b),
        jnp.zeros((4, PACK0_ROWS - OFF_LN_B - ROWS_SMALL, 1024), F32),
    ]
    return jnp.concatenate(parts, axis=1).reshape(4, 2, PACK0_ROWS // 2, 1024)


def _pack_grads1(g_w_mkv, g_w_out, g_w_kv, g_bf):
    parts = _pack_common_grads(g_w_mkv, g_w_out) + [
        jnp.pad(g_w_kv.reshape(D_MODEL, 4, KV_SHARD).transpose(1, 0, 2).reshape(4, KV_SHARD, 1024),
                ((0, 0), (0, ROWS_W_KV - KV_SHARD), (0, 0))),
        _replicated_rows(jnp.pad(g_bf.reshape(1, -1), ((0, 0), (0, 1024 - g_bf.shape[0])))),
        jnp.zeros((4, PACK1_ROWS - OFF_BF - ROWS_SMALL, 1024), F32),
    ]
    return jnp.concatenate(parts, axis=1).reshape(4, 2, PACK1_ROWS // 2, 1024)


def _local_step(x, mem, target, w0, w1, ln_g, ln_b, b_forget, *, tm=256, tq=512, dist=None):
    S = x.shape[0]
    g_rows = [ln_g[l:l + 1] for l in range(2)]
    b_rows = [ln_b[l:l + 1] for l in range(2)]
    bf_row = jnp.pad(b_forget.reshape(1, -1), ((0, 0), (0, LANES - FOX_HEADS)))

    def own_slot(gathered, pack):
        return lax.dynamic_update_slice(gathered, pack[None], (dist["me"], 0, 0, 0))

    if dist is None:
        w_in0, w_mkv0, w_out0, pool_w, pool_scale = w0
        h0 = _linear_fwd(x, w_in0, tm=tm, name="in_proj0")
    else:
        w_in0, pack0 = w0
        h0, g0 = _linear_fwd(x, w_in0, tm=tm, name="in_proj0", gather=[pack0])
        w_mkv0, w_out0, pool_w, pool_scale = _unpack_weights0(
            own_slot(g0, pack0).reshape(N_CHIPS, PACK0_ROWS, 1024))
    mkv0 = _linear_fwd(mem, w_mkv0, tm=N_MEM, name="mem_kv0")
    f0 = _mix("pool", "fwd", h=h0, xres=x, mkv=mkv0, w_out=w_out0, ln_g=g_rows[0], ln_b=b_rows[0],
              pool_w=pool_w, pool_scale=pool_scale, gather=() if dist is None else w1, tm=tm)
    z0, x1 = f0["z"], f0["xout"]
    if dist is not None:
        g1, g1_in = [own_slot(f0[f"gathered{a}"], w1[a]) for a in range(2)]
        w1 = (_unpack_w_in(g1_in),) + _unpack_weights1(g1.reshape(N_CHIPS, PACK1_ROWS, 1024))
    w_in1, w_mkv1, w_out1, w_kv = w1
    w_in, w_out = [w_in0, w_in1], [w_out0, w_out1]
    w_kvp = jnp.pad(w_kv, ((0, 0), (0, LANES - FOX_HEADS)))
    mkv = [mkv0, _linear_fwd(mem, w_mkv1, tm=N_MEM, name="mem_kv1")]
    ka, va, fl, cum = _kv_proj(x1, w_kvp, bf_row, tm=tm)
    h1, qf = _linear_fwd(x1, w_in[1], tm=tm, name="in_proj1", q_cum=cum)
    ymain1, qb = _fox_fwd(qf, ka, va, tq=tq)

    b1 = _mix("fox", "bwd", h=h1, xres=x1, mkv=mkv[1], w_out=w_out[1], ln_g=g_rows[1], ln_b=b_rows[1],
              ymain=ymain1, target=target, tm=tm)
    dq_aug, dk, dv, dck_rows = _fox_bwd(qb, ka, va, b1["dmain"], tq=tq)
    du1, df, dbf = _gate_bwd(dq_aug, _cols_of_rows(dck_rows, S), fl, tm=tm)

    dx1a, dw_in1 = _lin_bwd(x1, [du1, b1["drest"]], w_in[1], [(b1["dz"], ALPHA)], tm=tm, name="in_proj1_bwd")
    dx1, dw_kvp = _lin_bwd(x1, [dk, dv, df], w_kvp, [(dx1a, 1.0)], tm=tm, name="kv_proj_bwd")

    dw_mkv1 = _wgrad(mem, b1["dmkv"], name="mem_kv1_bwd")
    g_w_kv, g_bf = dw_kvp[:, 0:KV_COLS], dbf[0, 0:FOX_HEADS]

    gpacks1, psums1 = (), ()
    if dist is not None:
        gpacks1 = [_pack_grads1(dw_mkv1, b1["dw_out"], g_w_kv, g_bf), _w_in_grad_halves(dw_in1)]
    b0 = _mix("pool", "bwd", h=h0, mkv=mkv[0], w_out=w_out[0], ln_g=g_rows[0],
              pool_w=pool_w, pool_scale=pool_scale, z=z0, dy=dx1, pair_send=gpacks1, tm=tm)
    if dist is not None:
        psums1 = [_pair_sum(g, b0[f"from_sibling{a}"], dist["c_arr"], f"1{'ab'[a]}") for a, g in enumerate(gpacks1)]
    outs = _lin_bwd(x, [b0["dmain"], b0["drest"]], w_in[0], [(b0["dz"], ALPHA)], tm=tm, name="in_proj0_bwd",
                    scatter=psums1)
    dx, dw_in0 = outs[0], outs[1]
    dw_mkv0 = _wgrad(mem, b0["dmkv"], name="mem_kv0_bwd")
    g_ln_g = jnp.concatenate([b0["dln_g"], b1["dln_g"]], axis=0)
    g_ln_b = jnp.concatenate([b0["dln_b"], b1["dln_b"]], axis=0)

    if dist is None:
        grads = dict(w_in=[dw_in0, dw_in1], w_mem_kv=[dw_mkv0, dw_mkv1], w_out=[b0["dw_out"], b1["dw_out"]],
                     ln_g=g_ln_g, ln_b=g_ln_b, pool_w=b0["dpool_w"], pool_scale=b0["dpool_scale"],
                     w_kv=g_w_kv, b_forget=g_bf)
        return b1["loss"], dx, grads

    me, my_c = dist["me"], dist["my_c"]

    def with_own(pieces, psum):
        own = lax.dynamic_slice(psum, (me, 0, 0), (1,) + psum.shape[1:])
        return lax.dynamic_update_slice(pieces, own, (me, 0, 0))

    totals1 = [_sum_pieces(with_own(outs[2 + a], p), f"1{'ab'[a]}") for a, p in enumerate(psums1)]
    gpacks0 = [_pack_grads0(dw_mkv0, b0["dw_out"], b0["dpool_w"], b0["dpool_scale"], g_ln_g, g_ln_b),
               _w_in_grad_halves(dw_in0)]
    sib0 = _send_half_to_sibling(gpacks0, 0)
    psums0 = [_pair_sum(g, sib0[a], dist["c_arr"], f"0{'ab'[a]}") for a, g in enumerate(gpacks0)]
    loss_row = jnp.broadcast_to(0.5 / D_MODEL * jnp.sum(b1["loss"]), (8, LANES))
    pieces0 = _scatter_pieces(psums0, 0, share=loss_row)
    losses = lax.dynamic_update_slice(pieces0[2], loss_row[None], (2 * me + my_c, 0, 0))
    loss = jnp.sum(losses[:, 0, 0])
    totals0 = [_sum_pieces(with_own(pieces0[a], p), f"0{'ab'[a]}") for a, p in enumerate(psums0)]
    totals = totals0 + totals1
    halves = _exchange_halves(totals)
    full = [lax.dynamic_update_slice(h, t[None], (my_c, 0, 0)) for h, t in zip(halves, totals)]
    shard0, shard1 = full[0].reshape(PACK0_ROWS, 1024), full[2].reshape(PACK1_ROWS, 1024)
    g_w_in = jnp.stack([full[1].reshape(D_MODEL, W_IN_SHARD), full[3].reshape(D_MODEL, W_IN_SHARD)])
    return loss, dx, shard0, shard1, g_w_in


def kernel(x, mem, w_in, w_mem_kv, w_out, ln_g, ln_b, pool_w, pool_scale, w_kv_shared, b_forget, loss_target, m_w_in, m_w_mem_kv, m_w_out, m_ln_g, m_ln_b, m_pool_w, m_pool_scale, m_w_kv_shared, m_b_forget, v_w_in, v_w_mem_kv, v_w_out, v_ln_g, v_ln_b, v_pool_w, v_pool_scale, v_w_kv_shared, v_b_forget):
    dist = dict(c_arr=lax.axis_index("c").astype(jnp.int32).reshape(1),
                me=2 * lax.axis_index("x") + lax.axis_index("y"), my_c=lax.axis_index("c"))

    wpacks0, wpacks1 = _pack_weight_shards(w_in, w_mem_kv, w_out, pool_w, w_kv_shared, pool_scale)
    g0_in = lax.dynamic_update_slice(_all_gather_shards([wpacks0[1]])[0], wpacks0[1][None], (dist["me"], 0, 0, 0))
    w0 = (_unpack_w_in(g0_in), wpacks0[0])

    loss, dx, shard0, shard1, g_w_in = _local_step(x[0], mem[0], loss_target[0], w0, wpacks1, ln_g, ln_b,
                                                   b_forget, dist=dist)

    def per_layer(off, rows, shape):
        return jnp.concatenate([shard0[off:off + rows], shard1[off:off + rows]], axis=0).reshape(shape)

    g_w_mkv = per_layer(OFF_W_MKV, ROWS_W_MKV, w_mem_kv.shape)
    g_w_out = per_layer(OFF_W_OUT, ROWS_W_OUT, w_out.shape)
    g_pool_w = shard0[OFF_POOL_W:OFF_POOL_W + ROWS_POOL_W].reshape(pool_w.shape)
    g_w_kv = shard1[OFF_W_KV:OFF_W_KV + KV_SHARD].reshape(w_kv_shared.shape)
    g_pool_scale = shard0[OFF_POOL_S:OFF_POOL_S + 1, 0:256].reshape(pool_scale.shape)
    g_ln_g = shard0[OFF_LN_G:OFF_LN_G + 2]
    g_ln_b = shard0[OFF_LN_B:OFF_LN_B + 2]
    g_bf = shard1[OFF_BF, 0:FOX_HEADS]

    names = ["w_in", "w_mem_kv", "w_out", "ln_g", "ln_b", "pool_w", "pool_scale", "w_kv_shared", "b_forget"]
    ws = [w_in, w_mem_kv, w_out, ln_g, ln_b, pool_w, pool_scale, w_kv_shared, b_forget]
    gs = [g_w_in, g_w_mkv, g_w_out, g_ln_g, g_ln_b, g_pool_w, g_pool_scale, g_w_kv, g_bf]
    ms = [m_w_in, m_w_mem_kv, m_w_out, m_ln_g, m_ln_b, m_pool_w, m_pool_scale, m_w_kv_shared, m_b_forget]
    vs = [v_w_in, v_w_mem_kv, v_w_out, v_ln_g, v_ln_b, v_pool_w, v_pool_scale, v_w_kv_shared, v_b_forget]
    deltas, new_ms, new_vs = [], [], []
    for nm, w, gg, mm, vv in zip(names, ws, gs, ms, vs):
        two_d = (-1, w.shape[-1])
        d, nmm, nvv = _adamw(w.reshape(two_d), gg.reshape(two_d), mm.reshape(two_d), vv.reshape(two_d),
                             name=f"adamw_{nm}")
        deltas.append(d.reshape(w.shape))
        new_ms.append(nmm.reshape(w.shape))
        new_vs.append(nvv.reshape(w.shape))

    return (loss, dx[None], *gs, *deltas, *new_ms, *new_vs)
```

```python
import jax
import jax.numpy as jnp
from jax import lax
from jax.experimental import pallas as pl
from jax.experimental.pallas import tpu as pltpu

F32 = jnp.float32
BF16 = jnp.bfloat16

D_MODEL = 1024
D_MAIN = 1024
D_MEM = 512
D_MIX = D_MAIN + D_MEM
D_IN = 2 * D_MIX
N_MEM = 256
MEM_HEADS = 4
MEM_HEAD_DIM = 128
FOX_HEADS = 16
FOX_HEAD_DIM = 64
FOX_SCALE = 0.125
POOL_WINDOWS = (2, 4, 8, 16)
POOL_GROUP = 256
POOL_HALO = 16
ALPHA = 4.0 ** 0.25
LN_EPS = 1e-5
LANES = 128
N_CHIPS = 4

ADAM_LR = 0.001
ADAM_B1 = 0.9
ADAM_B2 = 0.999
ADAM_EPS = 1e-08
ADAM_WD = 0.01
ADAM_STEP = 10

V7X_VMEM_BYTES = 64 * 1024 * 1024
VMEM_LIMIT = V7X_VMEM_BYTES - 8 * 1024 * 1024

ROWS_W_MKV = (D_MODEL // N_CHIPS) * 2 * D_MEM // 1024
ROWS_W_OUT = (D_MIX // N_CHIPS) * D_MODEL // 1024
ROWS_POOL_W = 4 * (POOL_GROUP // N_CHIPS) * POOL_GROUP // 1024
KV_COLS = 2 * D_MAIN + FOX_HEADS
KV_SHARD = KV_COLS // N_CHIPS
ROWS_W_KV = 528
ROWS_SMALL = 16
OFF_W_MKV = 0
OFF_W_OUT = OFF_W_MKV + ROWS_W_MKV
OFF_TAIL = OFF_W_OUT + ROWS_W_OUT
OFF_POOL_W = OFF_TAIL
OFF_POOL_S = OFF_POOL_W + ROWS_POOL_W
OFF_LN_G = OFF_POOL_S + ROWS_SMALL
OFF_LN_B = OFF_LN_G + ROWS_SMALL
PACK0_ROWS = 768
OFF_W_KV = OFF_TAIL
OFF_BF = OFF_W_KV + ROWS_W_KV
PACK1_ROWS = 1280
PACK_TILE = 128
W_IN_SHARD = D_IN // N_CHIPS


def _dot(a, b):
    return jnp.dot(a, b, preferred_element_type=F32)


def _dot_nt(a, b):
    return lax.dot_general(a, b, (((1,), (1,)), ((), ())), preferred_element_type=F32)


def _dot_tn(a, b):
    return lax.dot_general(a, b, (((0,), (0,)), ((), ())), preferred_element_type=F32)


def _params(n_axes=1):
    return pltpu.CompilerParams(dimension_semantics=("arbitrary",) * n_axes,
                                vmem_limit_bytes=VMEM_LIMIT)


def _const_spec(shape):
    zeros = (0,) * len(shape)
    return pl.BlockSpec(shape, lambda *_: zeros, pipeline_mode=pl.Buffered(1))


def _split3(x):
    hi = x.astype(BF16)
    r = x - hi.astype(F32)
    mid = r.astype(BF16)
    lo = (r - mid.astype(F32)).astype(BF16)
    return hi, mid, lo


def _cols_and_chunk(w):
    if w.ndim == 3:
        return N_CHIPS * w.shape[2], 256
    return w.shape[1], (512 if w.shape[1] % 512 == 0 else LANES)


def _w_cols(w_ref, n0, nc):
    if len(w_ref.shape) == 3:
        per = w_ref.shape[2]
        assert n0 // per == (n0 + nc - 1) // per
        return w_ref.at[n0 // per, :, n0 % per:n0 % per + nc]
    return w_ref.at[:, n0:n0 + nc]


def _linear_fwd(x, w, *, tm, name, q_cum=None, gather=()):
    S, K = x.shape
    N, nc = _cols_and_chunk(w)
    aug = q_cum is not None
    ng = len(gather)
    n_tiles = S // tm
    assert not (aug and ng)

    def body(*refs):
        x_ref, w_ref = refs[0], refs[1]
        o_ref = refs[4] if aug else refs[2 + ng]
        if ng:
            i = pl.program_id(0)
            start, forward, finish = _gather_steps(refs[2:2 + ng], refs[3 + ng:3 + 2 * ng],
                                                   refs[3 + 2 * ng], refs[4 + 2 * ng])
            pl.when(i == 0)(start)
            pl.when(i == max(n_tiles - 3, 0))(forward)
            pl.when(i == n_tiles - 1)(finish)
        xb = x_ref[...].astype(BF16)
        if aug:
            lhs = _placement_lhs(refs[2][...], tm)
        for n0 in range(0, N, nc):
            r = _dot(xb, _w_cols(w_ref, n0, nc)[...])
            o_ref[:, n0:n0 + nc] = r.astype(BF16)
            if aug and n0 < D_MAIN:
                _store_augmented(refs[5], r * FOX_SCALE, n0 // LANES, lhs, refs[3])

    in_specs = [pl.BlockSpec((tm, K), lambda i: (i, 0)), _const_spec(w.shape)]
    out_specs = [pl.BlockSpec((tm, N), lambda i: (i, 0))]
    out_shape = [jax.ShapeDtypeStruct((S, N), BF16)]
    extra = []
    if aug:
        in_specs += [pl.BlockSpec((tm, LANES), lambda i: (i, 0)), _const_spec((LANES, AUG_W))]
        out_specs.append(pl.BlockSpec((tm, AUG_W), lambda i: (i, 0)))
        out_shape.append(jax.ShapeDtypeStruct((S, AUG_W), BF16))
        extra = [q_cum, _placement(AUG_A, (AUG_B,))]
    if ng:
        in_specs += [_ANY] * ng
        out_specs += [_ANY] * ng
        out_shape += [_gathered_shape(p) for p in gather]
        extra = list(gather)
    outs = pl.pallas_call(
        body, name=name, grid=(n_tiles,),
        in_specs=in_specs, out_specs=out_specs, out_shape=out_shape,
        scratch_shapes=_gather_sems(ng) if ng else [],
        compiler_params=_params(),
    )(x, w, *extra)
    return outs if (aug or ng) else outs[0]


def _kv_proj(x1, w_kv, bf_row, *, tm):
    S = x1.shape[0]

    def body(x_ref, w_ref, b_ref, pk_ref, pv_ref, k_ref, v_ref, fl_ref, cum_ref, carry_ref):
        i = pl.program_id(0)

        @pl.when(i == 0)
        def _():
            carry_ref[...] = jnp.zeros_like(carry_ref)

        xb = x_ref[...].astype(BF16)
        fl = _dot(xb, w_ref[:, 2 * D_MAIN:2 * D_MAIN + LANES]) + b_ref[...]
        fl_ref[...] = fl
        log_f = jnp.minimum(fl, 0.0) - jnp.log1p(jnp.exp(-jnp.abs(fl)))
        r = lax.broadcasted_iota(jnp.int32, (tm, tm), 0)
        c = lax.broadcasted_iota(jnp.int32, (tm, tm), 1)
        tri = jnp.where(c <= r, 1.0, 0.0).astype(BF16)
        hi, mid, lo = _split3(log_f)
        cum = (_dot(tri, hi) + _dot(tri, mid)) + _dot(tri, lo) + carry_ref[0:1, :]
        cum_ref[...] = cum
        carry_ref[0:1, :] = cum[tm - 1:tm, :]
        lhs_k, lhs_v = _placement_lhs(-cum, tm), _placement_lhs(None, tm)
        for n0 in range(0, D_MAIN, 512):
            _store_augmented(k_ref, _dot(xb, w_ref[:, n0:n0 + 512]), n0 // LANES, lhs_k, pk_ref)
            _store_augmented(v_ref, _dot(xb, w_ref[:, D_MAIN + n0:D_MAIN + n0 + 512]), n0 // LANES, lhs_v, pv_ref)

    return pl.pallas_call(
        body, name="kv_proj", grid=(S // tm,),
        in_specs=[pl.BlockSpec((tm, D_MODEL), lambda i: (i, 0)),
                  _const_spec((D_MODEL, 2 * D_MAIN + LANES)), _const_spec((1, LANES)),
                  _const_spec((LANES, AUG_W)), _const_spec((LANES, AUG_W))],
        out_specs=[pl.BlockSpec((tm, AUG_W), lambda i: (i, 0)),
                   pl.BlockSpec((tm, AUG_W), lambda i: (i, 0)),
                   pl.BlockSpec((tm, LANES), lambda i: (i, 0)),
                   pl.BlockSpec((tm, LANES), lambda i: (i, 0))],
        out_shape=[jax.ShapeDtypeStruct((S, AUG_W), BF16), jax.ShapeDtypeStruct((S, AUG_W), BF16),
                   jax.ShapeDtypeStruct((S, LANES), F32), jax.ShapeDtypeStruct((S, LANES), F32)],
        scratch_shapes=[pltpu.VMEM((8, LANES), F32)],
        compiler_params=_params(),
    )(x1, w_kv, bf_row, _placement(AUG_B, (AUG_A, AUG_C)), _placement(None, (AUG_A,)))


def _gate_bwd(dq_aug, dck, fl, *, tm):
    S = fl.shape[0]
    n = S // tm

    def body(dq_ref, dck_ref, fl_ref, du_ref, df_ref, db_ref, carry_ref):
        i = pl.program_id(0)

        @pl.when(i == 0)
        def _():
            carry_ref[...] = jnp.zeros_like(carry_ref)
            db_ref[...] = jnp.zeros_like(db_ref)

        lane = lax.broadcasted_iota(jnp.int32, (tm, LANES), 1)
        half0 = lane < FOX_HEAD_DIM
        dcq = jnp.zeros((tm, LANES), F32)
        for hp in range(FOX_HEADS // 2):
            b0 = dq_ref[:, 2 * hp * LANES:(2 * hp + 1) * LANES]
            b1 = dq_ref[:, (2 * hp + 1) * LANES:(2 * hp + 2) * LANES]
            du_ref[:, hp * LANES:(hp + 1) * LANES] = (
                jnp.where(half0, b0, pltpu.roll(b1, FOX_HEAD_DIM, 1)) * FOX_SCALE).astype(BF16)
            r0 = jnp.sum(jnp.where(lane == AUG_A, b0, 0.0), axis=1, keepdims=True)
            r1 = jnp.sum(jnp.where(lane == AUG_A, b1, 0.0), axis=1, keepdims=True)
            dcq = dcq + jnp.where(lane == 2 * hp, r0, 0.0) + jnp.where(lane == 2 * hp + 1, r1, 0.0)
        dcum = dcq - dck_ref[...]
        r = lax.broadcasted_iota(jnp.int32, (tm, tm), 0)
        c = lax.broadcasted_iota(jnp.int32, (tm, tm), 1)
        tri = jnp.where(c >= r, 1.0, 0.0).astype(BF16)
        hi, mid, lo = _split3(dcum)
        rev = (_dot(tri, hi) + _dot(tri, mid)) + _dot(tri, lo) + carry_ref[0:1, :]
        carry_ref[0:1, :] = rev[0:1, :]
        fl_v = fl_ref[...]
        df = rev * (1.0 / (1.0 + jnp.exp(fl_v)))
        df_ref[...] = df
        db_ref[...] += jnp.sum(df, axis=0, keepdims=True)

    return pl.pallas_call(
        body, name="gate_bwd", grid=(n,),
        in_specs=[pl.BlockSpec((tm, AUG_W), lambda i: (n - 1 - i, 0)),
                  pl.BlockSpec((tm, LANES), lambda i: (n - 1 - i, 0)),
                  pl.BlockSpec((tm, LANES), lambda i: (n - 1 - i, 0))],
        out_specs=[pl.BlockSpec((tm, D_MAIN), lambda i: (n - 1 - i, 0)),
                   pl.BlockSpec((tm, LANES), lambda i: (n - 1 - i, 0)),
                   pl.BlockSpec((1, LANES), lambda i: (0, 0))],
        out_shape=[jax.ShapeDtypeStruct((S, D_MAIN), BF16),
                   jax.ShapeDtypeStruct((S, LANES), F32), jax.ShapeDtypeStruct((1, LANES), F32)],
        scratch_shapes=[pltpu.VMEM((8, LANES), F32)],
        compiler_params=_params(),
    )(dq_aug, dck, fl)


def _silu_and_grad(g):
    sg = 1.0 / (1.0 + jnp.exp(-g))
    return g * sg, sg * (1.0 + g * (1.0 - sg))


def _mix(kind, mode, *, h, xres=None, mkv, w_out, ln_g, ln_b=None, pool_w=None, pool_scale=None,
         ymain=None, target=None, z=None, dy=None, gather=(), pair_send=(), tm):
    S = h.shape[0]
    n = S // tm
    pool = kind == "pool"
    bwd = mode == "bwd"
    loss_head = bwd and not pool
    rev = pool and bwd
    mem_scale = MEM_HEAD_DIM ** -0.5

    def t_of(i):
        return (n - 1 - i) if rev else i

    row = lambda i: (t_of(i), 0)
    names, arrays, specs = [], [], []

    def add(name, arr, spec):
        names.append(name)
        arrays.append(arr)
        specs.append(spec)

    add("h", h, pl.BlockSpec((tm, D_IN), row))
    if pool:
        hb = tm // POOL_HALO
        add("halo", h, pl.BlockSpec((POOL_HALO, D_MAIN), lambda i: (jnp.maximum(t_of(i) * hb - 1, 0), 0)))
        add("pool_w", pool_w, _const_spec((4, POOL_GROUP, POOL_GROUP)))
        add("pool_scale", pool_scale, _const_spec((1, D_MAIN)))
    else:
        add("ymain", ymain, pl.BlockSpec((tm, D_MAIN), row))
    add("mkv", mkv, _const_spec((N_MEM, 2 * D_MEM)))
    add("w_out", w_out, _const_spec((D_MIX, D_MODEL)))
    add("ln_g", ln_g, _const_spec((1, D_MODEL)))
    if not (pool and bwd):
        add("xres", xres, pl.BlockSpec((tm, D_MODEL), row))
        add("ln_b", ln_b, _const_spec((1, D_MODEL)))
    if loss_head:
        add("target", target, pl.BlockSpec((tm, D_MODEL), row))
        add("place_do", _placement(AUG_A, ()), _const_spec((LANES, AUG_W)))
    if pool and bwd:
        add("z", z, pl.BlockSpec((tm, D_MODEL), row))
        add("dy", dy, pl.BlockSpec((tm, D_MODEL), row))
    for a, p in enumerate(gather):
        add(f"gather_src{a}", p, _ANY)
    for a, p in enumerate(pair_send):
        add(f"pair_src{a}", p, _ANY)

    onames, oshapes, ospecs = [], [], []

    def add_out(name, shape, dtype, spec):
        onames.append(name)
        oshapes.append(jax.ShapeDtypeStruct(shape, dtype))
        ospecs.append(spec)

    const2 = lambda i: (0, 0)
    if not bwd:
        add_out("z", (S, D_MODEL), F32, pl.BlockSpec((tm, D_MODEL), row))
        add_out("xout", (S, D_MODEL), F32, pl.BlockSpec((tm, D_MODEL), row))
    else:
        add_out("dz", (S, D_MODEL), F32, pl.BlockSpec((tm, D_MODEL), row))
        if pool:
            add_out("dmain", (S, D_MAIN), BF16, pl.BlockSpec((tm, D_MAIN), row))
        else:
            add_out("dmain", (S, 2 * D_MAIN), BF16, pl.BlockSpec((tm, 2 * D_MAIN), row))
        add_out("drest", (S, D_IN - D_MAIN), BF16, pl.BlockSpec((tm, D_IN - D_MAIN), row))
        add_out("dw_out", (D_MIX, D_MODEL), F32, pl.BlockSpec((D_MIX, D_MODEL), const2))
        add_out("dmkv", (N_MEM, 2 * D_MEM), F32, pl.BlockSpec((N_MEM, 2 * D_MEM), const2))
        add_out("dln_g", (1, D_MODEL), F32, pl.BlockSpec((1, D_MODEL), const2))
        add_out("dln_b", (1, D_MODEL), F32, pl.BlockSpec((1, D_MODEL), const2))
        if pool:
            add_out("dpool_w", (4, POOL_GROUP, POOL_GROUP), F32,
                    pl.BlockSpec((4, POOL_GROUP, POOL_GROUP), lambda i: (0, 0, 0)))
            add_out("dpool_scale", (1, D_MAIN), F32, pl.BlockSpec((1, D_MAIN), const2))
        else:
            add_out("loss", (1, D_MODEL), F32, pl.BlockSpec((1, D_MODEL), const2))

    for a, p in enumerate(gather):
        add_out(f"gathered{a}", (N_CHIPS,) + p.shape, p.dtype, _ANY)
    for a, p in enumerate(pair_send):
        add_out(f"from_sibling{a}", (N_CHIPS,) + p.shape[2:], p.dtype, _ANY)

    scratch = [pltpu.VMEM((tm, D_MIX), BF16),
               pltpu.VMEM((tm, D_MEM), F32)]
    if pool:
        scratch.append(pltpu.VMEM((tm + 2 * POOL_HALO, D_MAIN), F32))
    if rev:
        scratch.append(pltpu.VMEM((POOL_HALO, D_MAIN), F32))
    assert not (gather and pair_send)
    if gather:
        scratch += _gather_sems(len(gather))
    if pair_send:
        scratch += _pair_sems(len(pair_send))
    n_in, n_out = len(names), len(onames)

    def body(*refs):
        R = dict(zip(names, refs[:n_in]))
        O = dict(zip(onames, refs[n_in:n_in + n_out]))
        sc = refs[n_in + n_out:]
        yc_ref, ymem_ref = sc[0], sc[1]
        ext_ref = sc[2] if pool else None
        carry_ref = sc[3] if rev else None
        i = pl.program_id(0)
        t = t_of(i)
        h_ref = R["h"]
        gamma = R["ln_g"][...]

        if gather:
            start, forward, finish = _gather_steps(
                [R[f"gather_src{a}"] for a in range(len(gather))],
                [O[f"gathered{a}"] for a in range(len(gather))], sc[-2], sc[-1])
            pl.when(i == 0)(start)
            pl.when(i == max(n - 3, 0))(forward)
            pl.when(i == n - 1)(finish)
        if pair_send:
            start, finish = _pair_send_steps(
                [R[f"pair_src{a}"] for a in range(len(pair_send))],
                [O[f"from_sibling{a}"] for a in range(len(pair_send))], sc[-2], sc[-1])
            pl.when(i == 0)(start)
            pl.when(i == n - 1)(finish)

        if bwd:
            @pl.when(i == 0)
            def _():
                for nm in ("dw_out", "dmkv", "dln_g", "dln_b", "dpool_w", "dpool_scale", "loss"):
                    if nm in O:
                        O[nm][...] = jnp.zeros_like(O[nm])
                if rev:
                    carry_ref[...] = jnp.zeros_like(carry_ref)

        if pool:
            u = h_ref[:, 0:D_MAIN].astype(F32)
            halo = R["halo"][...].astype(F32)
            ext_ref[0:POOL_HALO, :] = jnp.where(t > 0, halo, 0.0)
            ext_ref[POOL_HALO:POOL_HALO + tm, :] = u
            tpos = t * tm + lax.broadcasted_iota(jnp.int32, (tm, 1), 0)
            pms, invcs = [], []
            for gi, w in enumerate(POOL_WINDOWS):
                cs = slice(gi * POOL_GROUP, (gi + 1) * POOL_GROUP)
                acc = ext_ref[POOL_HALO:POOL_HALO + tm, cs]
                for k in range(1, w):
                    acc = acc + ext_ref[POOL_HALO - k:POOL_HALO - k + tm, cs]
                invc = 1.0 / jnp.minimum(tpos + 1, w).astype(F32)
                pm = (acc * invc - u[:, cs]).astype(BF16)
                pms.append(pm)
                invcs.append(invc)
            mixed = [_dot(pms[gi], R["pool_w"][gi]) for gi in range(4)]
            ps = R["pool_scale"][...]
            y_main = [mixed[gi] * ps[:, gi * POOL_GROUP:(gi + 1) * POOL_GROUP] for gi in range(4)]
        else:
            y_main = [R["ymain"][:, gi * 256:(gi + 1) * 256].astype(F32) for gi in range(4)]

        probs = []
        for hd in range(MEM_HEADS):
            sl = slice(D_MAIN + hd * MEM_HEAD_DIM, D_MAIN + (hd + 1) * MEM_HEAD_DIM)
            ksl = slice(hd * MEM_HEAD_DIM, (hd + 1) * MEM_HEAD_DIM)
            vsl = slice(D_MEM + hd * MEM_HEAD_DIM, D_MEM + (hd + 1) * MEM_HEAD_DIM)
            s = _dot_nt(h_ref[:, sl], R["mkv"][:, ksl]) * mem_scale
            e = jnp.exp(s - jnp.max(s, axis=1, keepdims=True))
            p = e / jnp.sum(e, axis=1, keepdims=True)
            probs.append(p)
            ymem_ref[:, ksl] = _dot(p.astype(BF16), R["mkv"][:, vsl])

        g_off = D_MIX
        gate_d = []
        for gi in range(4):
            cs = slice(gi * 256, (gi + 1) * 256)
            gm = h_ref[:, g_off + gi * 256:g_off + (gi + 1) * 256].astype(F32)
            sv, sd = _silu_and_grad(gm)
            yc_ref[:, cs] = (y_main[gi] * sv).astype(BF16)
            gate_d.append((sv, sd))
        gq = h_ref[:, g_off + D_MAIN:D_IN].astype(F32)
        svq, sdq = _silu_and_grad(gq)
        yc_ref[:, D_MAIN:D_MIX] = (ymem_ref[...] * svq).astype(BF16)

        if pool and bwd:
            zt = R["z"][...]
        else:
            o = _dot(yc_ref[...], R["w_out"][...])
            zt = ALPHA * R["xres"][...] + o
        mu = jnp.mean(zt, axis=1, keepdims=True)
        zc = zt - mu
        var = jnp.mean(zc * zc, axis=1, keepdims=True)
        rstd = lax.rsqrt(var + LN_EPS)
        xhat = zc * rstd
        if not bwd:
            O["z"][...] = zt
            O["xout"][...] = xhat * gamma + R["ln_b"][...]
            return

        if loss_head:
            xo = xhat * gamma + R["ln_b"][...]
            err = xo - R["target"][...]
            O["loss"][...] += jnp.sum(err * err, axis=0, keepdims=True)
            dyt = err * (1.0 / D_MODEL)
        else:
            dyt = R["dy"][...]

        O["dln_g"][...] += jnp.sum(dyt * xhat, axis=0, keepdims=True)
        O["dln_b"][...] += jnp.sum(dyt, axis=0, keepdims=True)
        gdy = dyt * gamma
        m1 = jnp.mean(gdy, axis=1, keepdims=True)
        m2 = jnp.mean(gdy * xhat, axis=1, keepdims=True)
        dz = rstd * (gdy - m1 - xhat * m2)
        O["dz"][...] = dz
        dzb = dz.astype(BF16)

        for n0 in range(0, D_MIX, 512):
            O["dw_out"][n0:n0 + 512, :] += _dot_tn(yc_ref[:, n0:n0 + 512], dzb)
        dyc_mem = _dot_nt(dzb, R["w_out"][D_MAIN:D_MIX, :])

        O["drest"][:, D_MEM + D_MAIN:D_MEM + D_MAIN + D_MEM] = (dyc_mem * ymem_ref[...] * sdq).astype(BF16)
        dymem = dyc_mem * svq
        for hd in range(MEM_HEADS):
            sl = slice(D_MAIN + hd * MEM_HEAD_DIM, D_MAIN + (hd + 1) * MEM_HEAD_DIM)
            ksl = slice(hd * MEM_HEAD_DIM, (hd + 1) * MEM_HEAD_DIM)
            vsl = slice(D_MEM + hd * MEM_HEAD_DIM, D_MEM + (hd + 1) * MEM_HEAD_DIM)
            p = probs[hd]
            dyb = dymem[:, ksl].astype(BF16)
            dp = _dot_nt(dyb, R["mkv"][:, vsl])
            ds = p * (dp - jnp.sum(dp * p, axis=1, keepdims=True)) * mem_scale
            dsb = ds.astype(BF16)
            O["drest"][:, ksl] = _dot(dsb, R["mkv"][:, ksl]).astype(BF16)
            O["dmkv"][:, ksl] += _dot_tn(dsb, h_ref[:, sl])
            O["dmkv"][:, vsl] += _dot_tn(p.astype(BF16), dyb)

        dmain = []
        for gi in range(4):
            cs = slice(gi * 256, (gi + 1) * 256)
            dyc_g = _dot_nt(dzb, R["w_out"][cs, :])
            sv, sd = gate_d[gi]
            O["drest"][:, D_MEM + gi * 256:D_MEM + (gi + 1) * 256] = (dyc_g * y_main[gi] * sd).astype(BF16)
            dmain.append(dyc_g * sv)

        if not pool:
            dbf = [dmain[gi].astype(BF16).astype(F32) for gi in range(4)]
            dcol = jnp.zeros((tm, LANES), F32)
            for gi in range(4):
                dr = lax.broadcasted_iota(jnp.int32, (256, LANES), 0)
                hc = lax.broadcasted_iota(jnp.int32, (256, LANES), 1)
                sel = jnp.where(jnp.right_shift(dr, 6) + gi * 4 == hc, 1.0, 0.0).astype(BF16)
                hi, mid, lo = _split3(dbf[gi] * R["ymain"][:, gi * 256:(gi + 1) * 256].astype(F32))
                dcol = dcol + ((_dot(hi, sel) + _dot(mid, sel)) + _dot(lo, sel))
            lhs = _placement_lhs(-dcol, tm)
            for gi in range(4):
                _store_augmented(O["dmain"], dbf[gi], 2 * gi, lhs, R["place_do"])
            return

        ps = R["pool_scale"][...]
        dpm_list = []
        for gi in range(4):
            cs = slice(gi * 256, (gi + 1) * 256)
            O["dpool_scale"][:, cs] += jnp.sum(dmain[gi] * mixed[gi], axis=0, keepdims=True)
            dmix = (dmain[gi] * ps[:, cs]).astype(BF16)
            O["dpool_w"][gi] += _dot_tn(pms[gi], dmix)
            dpm = _dot_nt(dmix, R["pool_w"][gi])
            dpm_list.append(dpm)
            ext_ref[0:tm, cs] = dpm * invcs[gi]
        ext_ref[tm:tm + POOL_HALO, :] = carry_ref[...]
        carry_ref[...] = ext_ref[0:POOL_HALO, :]
        for gi, w in enumerate(POOL_WINDOWS):
            cs = slice(gi * 256, (gi + 1) * 256)
            acc = ext_ref[0:tm, cs]
            for k in range(1, w):
                acc = acc + ext_ref[k:k + tm, cs]
            O["dmain"][:, cs] = (acc - dpm_list[gi]).astype(BF16)

    outs = pl.pallas_call(
        body, name=f"mix_{kind}_{mode}", grid=(n,),
        in_specs=specs, out_specs=ospecs, out_shape=oshapes,
        scratch_shapes=scratch, compiler_params=_params(),
    )(*arrays)
    return dict(zip(onames, outs))


def _lin_bwd(xin, dhs, w, res, *, tm, name, scatter=()):
    S, K = xin.shape
    N, nc_w = _cols_and_chunk(w)
    nj, nr, ns = len(dhs), len(res), len(scatter)
    n_tiles = S // tm
    widths = [dh.shape[1] for dh in dhs]
    chunks = [nc_w if w.ndim == 3 else (512 if wd % 512 == 0 else wd) for wd in widths]
    assert sum(widths) == N and all(wd % c == 0 for wd, c in zip(widths, chunks))
    scales = [s for _, s in res]
    n_in = 2 + nj + nr + ns

    def body(*refs):
        x_ref = refs[0]
        dh_refs = refs[1:1 + nj]
        w_ref = refs[1 + nj]
        r_refs = refs[2 + nj:2 + nj + nr]
        dx_ref, dw_ref = refs[n_in], refs[n_in + 1]
        i = pl.program_id(0)

        if ns:
            start, finish = _scatter_steps(refs[n_in - ns:n_in], refs[n_in + 2:n_in + 2 + ns],
                                           refs[n_in + 2 + ns], refs[n_in + 3 + ns])
            pl.when(i == 0)(start)
            pl.when(i == n_tiles - 1)(finish)

        @pl.when(i == 0)
        def _():
            dw_ref[...] = jnp.zeros_like(dw_ref)

        xb = x_ref[...].astype(BF16)
        dx = jnp.zeros((tm, K), F32)
        for r_ref, sc in zip(r_refs, scales):
            dx = dx + sc * r_ref[...]
        off = 0
        for j in range(nj):
            nc = chunks[j]
            for n0 in range(0, widths[j], nc):
                dhb = dh_refs[j][:, n0:n0 + nc].astype(BF16)
                dx = dx + _dot_nt(dhb, _w_cols(w_ref, off + n0, nc)[...])
                _w_cols(dw_ref, off + n0, nc)[...] += _dot_tn(xb, dhb)
            off += widths[j]
        dx_ref[...] = dx

    zeros = (0,) * w.ndim
    in_specs = [pl.BlockSpec((tm, K), lambda i: (i, 0))]
    in_specs += [pl.BlockSpec((tm, n), lambda i: (i, 0)) for n in widths]
    in_specs += [_const_spec(w.shape)]
    in_specs += [pl.BlockSpec((tm, K), lambda i: (i, 0)) for _ in res]
    in_specs += [_ANY] * ns
    out_specs = [pl.BlockSpec((tm, K), lambda i: (i, 0)), pl.BlockSpec(w.shape, lambda i: zeros)] + [_ANY] * ns
    out_shape = [jax.ShapeDtypeStruct((S, K), F32), jax.ShapeDtypeStruct(w.shape, F32)]
    out_shape += [jax.ShapeDtypeStruct(p.shape, p.dtype) for p in scatter]
    return pl.pallas_call(
        body, name=name, grid=(n_tiles,),
        in_specs=in_specs, out_specs=out_specs, out_shape=out_shape,
        scratch_shapes=_scatter_sems(ns) if ns else [],
        compiler_params=_params(),
    )(xin, *dhs, w, *[r for r, _ in res], *scatter)


def _wgrad(xin, dh, *, name):
    M, K = xin.shape
    N = dh.shape[1]

    def body(x_ref, dh_ref, o_ref):
        o_ref[...] = _dot_tn(x_ref[...].astype(BF16), dh_ref[...].astype(BF16))

    return pl.pallas_call(
        body, name=name, out_shape=jax.ShapeDtypeStruct((K, N), F32),
        compiler_params=pltpu.CompilerParams(vmem_limit_bytes=VMEM_LIMIT),
    )(xin, dh)


AUG_A = FOX_HEAD_DIM
AUG_B = FOX_HEAD_DIM + 3
AUG_C = FOX_HEAD_DIM + 6
AUG_W = FOX_HEADS * LANES


def _placement(val_lane, ones_lanes):
    r = jnp.arange(LANES)[:, None]
    c = jnp.arange(AUG_W)[None, :]
    head, lane = c // LANES, c % LANES
    m = jnp.zeros((LANES, AUG_W), jnp.bool_)
    if val_lane is not None:
        for part in range(3):
            m = m | ((r == part * FOX_HEADS + head) & (lane == val_lane + part))
    for first in ones_lanes:
        m = m | ((r == 3 * FOX_HEADS) & (lane >= first) & (lane < first + 3))
    return m.astype(BF16)


def _placement_lhs(val, tm):
    lane = lax.broadcasted_iota(jnp.int32, (tm, LANES), 1)
    lhs = jnp.where(lane == 3 * FOX_HEADS, 1.0, 0.0)
    if val is not None:
        hi, mid, lo = [p.astype(F32) for p in _split3(val)]
        lhs = jnp.where(lane < FOX_HEADS, hi, jnp.where(
            lane < 2 * FOX_HEADS, pltpu.roll(mid, FOX_HEADS, 1), jnp.where(
                lane < 3 * FOX_HEADS, pltpu.roll(lo, 2 * FOX_HEADS, 1), lhs)))
    return lhs.astype(BF16)


def _store_augmented(o_ref, data, first_pair, lhs, p_ref):
    tm = data.shape[0]
    is_data = lax.broadcasted_iota(jnp.int32, (tm, LANES), 1) < FOX_HEAD_DIM
    for j in range(data.shape[1] // LANES):
        base = 2 * (first_pair + j) * LANES
        extra = _dot(lhs, p_ref[:, base:base + 2 * LANES])
        blk = data[:, j * LANES:(j + 1) * LANES]
        o_ref[:, base:base + LANES] = jnp.where(is_data, blk, extra[:, 0:LANES]).astype(BF16)
        o_ref[:, base + LANES:base + 2 * LANES] = jnp.where(
            is_data, pltpu.roll(blk, FOX_HEAD_DIM, 1), extra[:, LANES:2 * LANES]).astype(BF16)


def _cols_of_rows(rows, S):
    nh = FOX_HEADS // rows.shape[0]
    a = rows[:, :, 0:nh, :].transpose(0, 2, 1, 3).reshape(FOX_HEADS, S).T
    return jnp.pad(a, ((0, 0), (0, LANES - FOX_HEADS)))


def _fox_fwd(qf, ka, va, *, tq, nh=4):
    S = ka.shape[0]
    nq = S // tq
    tk = tq
    ng = FOX_HEADS // nh

    def body(q_ref, k_ref, v_ref, o_ref, qb_ref, *scratch):
        p_scr, m_scr, acc_scr = scratch[0:nh], scratch[nh:2 * nh], scratch[2 * nh:3 * nh]
        qi = pl.program_id(1)
        lane = lax.broadcasted_iota(jnp.int32, (tq, LANES), 1)
        half0 = lane < FOX_HEAD_DIM
        rr = lax.broadcasted_iota(jnp.int32, (tq, tk), 0)
        cc = lax.broadcasted_iota(jnp.int32, (tq, tk), 1)
        sls = [slice(hh * LANES, (hh + 1) * LANES) for hh in range(nh)]
        qs = [q_ref[:, sl] for sl in sls]

        @pl.when(qi == 0)
        def _():
            for hh in range(nh):
                p_scr[hh][...] = jnp.zeros_like(p_scr[hh])
                m_scr[hh][...] = jnp.full(m_scr[hh].shape, -jnp.inf, F32)
                acc_scr[hh][...] = jnp.zeros_like(acc_scr[hh])

        @pl.when(qi > 0)
        def _():
            for hh in range(nh):
                s = _dot_nt(qs[hh], k_ref[0:tk, sls[hh]])
                m0 = jnp.broadcast_to(jnp.max(s, axis=1, keepdims=True), (tq, LANES))
                p_scr[hh][...] = jnp.exp(s - jnp.tile(m0, (1, tk // LANES))).astype(BF16)
                m_scr[hh][...] = m0
                acc_scr[hh][...] = jnp.zeros_like(acc_scr[hh])

        def chunk(ki, masked):
            k0 = pl.multiple_of(ki * tk, tk)
            kp = pl.multiple_of(jnp.maximum(ki - 1, 0) * tk, tk)
            for hh in range(nh):
                m = m_scr[hh][...]
                s = _dot_nt(qs[hh], k_ref[pl.ds(k0, tk), sls[hh]])
                pv = _dot(p_scr[hh][...], v_ref[pl.ds(kp, tk), sls[hh]])
                if masked:
                    s = jnp.where(cc <= rr, s, -jnp.inf)
                m_new = jnp.maximum(m, jnp.max(s, axis=1, keepdims=True))
                p_scr[hh][...] = jnp.exp(s - jnp.tile(m_new, (1, tk // LANES))).astype(BF16)
                acc_scr[hh][...] = (acc_scr[hh][...] + pv) * jnp.exp(m - m_new)
                m_scr[hh][...] = m_new

        def trip(ki, c):
            chunk(ki, False)
            return c

        lax.fori_loop(1, qi, trip, 0)
        chunk(qi, True)
        kq = pl.multiple_of(qi * tk, tk)
        outs = []
        for hh in range(nh):
            m = m_scr[hh][...]
            acc = acc_scr[hh][...] + _dot(p_scr[hh][...], v_ref[pl.ds(kq, tk), sls[hh]])
            l = jnp.sum(jnp.where(lane == AUG_A, acc, 0.0), axis=1, keepdims=True)
            outs.append(acc / l)
            hi, mid, lo = _split3(-(m + jnp.log(l)))
            qb_ref[:, sls[hh]] = jnp.where(lane == AUG_C, hi, jnp.where(
                lane == AUG_C + 1, mid, jnp.where(lane == AUG_C + 2, lo, qs[hh])))
        for pr in range(nh // 2):
            o_ref[:, pr * LANES:(pr + 1) * LANES] = jnp.where(
                half0, outs[2 * pr], pltpu.roll(outs[2 * pr + 1], FOX_HEAD_DIM, 1))

    return pl.pallas_call(
        body, name="fox_fwd", grid=(ng, nq),
        in_specs=[pl.BlockSpec((tq, nh * LANES), lambda g, qi: (qi, g)),
                  pl.BlockSpec((S, nh * LANES), lambda g, qi: (0, g), pipeline_mode=pl.Buffered(1)),
                  pl.BlockSpec((S, nh * LANES), lambda g, qi: (0, g), pipeline_mode=pl.Buffered(1))],
        out_specs=[pl.BlockSpec((tq, nh * FOX_HEAD_DIM), lambda g, qi: (qi, g)),
                   pl.BlockSpec((tq, nh * LANES), lambda g, qi: (qi, g))],
        out_shape=[jax.ShapeDtypeStruct((S, D_MAIN), F32),
                   jax.ShapeDtypeStruct((S, AUG_W), BF16)],
        scratch_shapes=([pltpu.VMEM((tq, tk), BF16)] * nh + [pltpu.VMEM((tq, LANES), F32)] * nh
                        + [pltpu.VMEM((tq, LANES), F32)] * nh),
        compiler_params=_params(2),
    )(qf, ka, va)


def _fox_bwd(qb, ka, va, do_aug, *, tq):
    S = ka.shape[0]
    nq = S // tq
    tk = tq

    def body(k_ref, v_ref, q_ref, do_ref, dq_ref, dk_ref, dv_ref, dck_ref, dk_scr, dv_scr):
        kj = pl.program_id(1)

        @pl.when(kj == 0)
        def _():
            dq_ref[...] = jnp.zeros_like(dq_ref)

        lane = lax.broadcasted_iota(jnp.int32, (tk, LANES), 1)
        half0 = lane < FOX_HEAD_DIM
        rr = lax.broadcasted_iota(jnp.int32, (tk, tq), 0)
        cc = lax.broadcasted_iota(jnp.int32, (tk, tq), 1)
        sls = [slice(hh * LANES, (hh + 1) * LANES) for hh in range(2)]
        kts = [k_ref[:, sl] for sl in sls]
        vts = [v_ref[:, sl] for sl in sls]

        dk_scr[...] = jnp.zeros_like(dk_scr)
        dv_scr[...] = jnp.zeros_like(dv_scr)

        def chunk(qi, masked):
            q0 = pl.multiple_of(qi * tq, tq)
            for hh in range(2):
                qc = q_ref[pl.ds(q0, tq), sls[hh]]
                doc = do_ref[pl.ds(q0, tq), sls[hh]]
                pt = jnp.exp(_dot_nt(kts[hh], qc))
                if masked:
                    pt = jnp.where(rr <= cc, pt, 0.0)
                dsb = (pt * _dot_nt(vts[hh], doc)).astype(BF16)
                dv_scr[hh] += _dot(pt.astype(BF16), doc)
                dk_scr[hh] += _dot(dsb, qc)
                dq_ref[pl.ds(q0, tq), sls[hh]] += _dot_tn(dsb, kts[hh])

        def trip(qi, c):
            chunk(qi, False)
            return c

        chunk(kj, True)
        lax.fori_loop(kj + 1, nq, trip, 0)
        dk0, dk1 = dk_scr[0], dk_scr[1]
        dv0, dv1 = dv_scr[0], dv_scr[1]
        dk_ref[...] = jnp.where(half0, dk0, pltpu.roll(dk1, FOX_HEAD_DIM, 1)).astype(BF16)
        dv_ref[...] = jnp.where(half0, dv0, pltpu.roll(dv1, FOX_HEAD_DIM, 1)).astype(BF16)
        c0 = jnp.sum(jnp.where(lane == AUG_B, dk0, 0.0), axis=1, keepdims=True)
        c1 = jnp.sum(jnp.where(lane == AUG_B, dk1, 0.0), axis=1, keepdims=True)
        dck_cols = jnp.where(lane == 0, c0, 0.0) + jnp.where(lane == 1, c1, 0.0)
        dck_ref[0, 0] = dck_cols.T[0:8, :]

    return pl.pallas_call(
        body, name="fox_bwd", grid=(8, nq),
        in_specs=[pl.BlockSpec((tk, 2 * LANES), lambda hp, kj: (kj, hp)),
                  pl.BlockSpec((tk, 2 * LANES), lambda hp, kj: (kj, hp)),
                  pl.BlockSpec((S, 2 * LANES), lambda hp, kj: (0, hp)),
                  pl.BlockSpec((S, 2 * LANES), lambda hp, kj: (0, hp))],
        out_specs=[pl.BlockSpec((S, 2 * LANES), lambda hp, kj: (0, hp)),
                   pl.BlockSpec((tk, LANES), lambda hp, kj: (kj, hp)),
                   pl.BlockSpec((tk, LANES), lambda hp, kj: (kj, hp)),
                   pl.BlockSpec((1, 1, 8, tk), lambda hp, kj: (hp, kj, 0, 0))],
        out_shape=[jax.ShapeDtypeStruct((S, AUG_W), F32),
                   jax.ShapeDtypeStruct((S, D_MAIN), BF16),
                   jax.ShapeDtypeStruct((S, D_MAIN), BF16),
                   jax.ShapeDtypeStruct((8, nq, 8, tk), F32)],
        scratch_shapes=[pltpu.VMEM((2, tk, LANES), F32), pltpu.VMEM((2, tk, LANES), F32)],
        compiler_params=_params(2),
    )(ka, va, qb, do_aug)


def _adamw(w, g, m, v, *, name):
    Rr, C = w.shape
    tr = 256 if Rr % 256 == 0 else Rr
    c1 = 1.0 / (1.0 - ADAM_B1 ** ADAM_STEP)
    c2 = 1.0 / (1.0 - ADAM_B2 ** ADAM_STEP)

    def body(w_ref, g_ref, m_ref, v_ref, d_ref, nm_ref, nv_ref):
        gv = g_ref[...]
        nm = ADAM_B1 * m_ref[...] + (1.0 - ADAM_B1) * gv
        nv = ADAM_B2 * v_ref[...] + (1.0 - ADAM_B2) * (gv * gv)
        d_ref[...] = -ADAM_LR * ((nm * c1) / (jnp.sqrt(nv * c2) + ADAM_EPS) + ADAM_WD * w_ref[...])
        nm_ref[...] = nm
        nv_ref[...] = nv

    spec = pl.BlockSpec((tr, C), lambda i: (i, 0))
    sds = jax.ShapeDtypeStruct((Rr, C), F32)
    return pl.pallas_call(
        body, name=name, grid=(Rr // tr,),
        in_specs=[spec] * 4, out_specs=[spec] * 3, out_shape=[sds] * 3,
        compiler_params=_params(),
    )(w, g, m, v)


_ANY = pl.BlockSpec(memory_space=pl.ANY)
_MESH = pl.DeviceIdType.MESH


def _place():
    x, y, c = lax.axis_index("x"), lax.axis_index("y"), lax.axis_index("c")
    return x, y, c


def _gather_steps(p_refs, out_refs, send_sems, recv_sems):
    x, y, c = _place()
    sib = (x, y, 1 - c)
    chips = [(1 - x, y), (x, 1 - y), (1 - x, 1 - y)]
    idx = [2 * chip[0] + chip[1] for chip in chips]
    me = 2 * x + y
    na = len(p_refs)

    def copy(a, k, chip_idx, half, to, src=None):
        dst = out_refs[a].at[chip_idx, half]
        return pltpu.make_async_remote_copy(
            src_ref=dst if src is None else src, dst_ref=dst,
            send_sem=send_sems.at[6 * a + k], recv_sem=recv_sems.at[6 * a + k],
            device_id=to, device_id_type=_MESH)

    first = [copy(a, j, me, c, (*chips[j], c), src=p_refs[a].at[c]) for a in range(na) for j in range(3)]
    passed = [copy(a, 3 + j, idx[j], c, sib) for a in range(na) for j in range(3)]

    def start():
        for cp in first:
            cp.start()

    def forward():
        for a in range(na):
            for j in range(3):
                copy(a, j, idx[j], c, sib).wait_recv()
                passed[3 * a + j].start()

    def finish():
        for a in range(na):
            for j in range(3):
                copy(a, 3 + j, idx[j], 1 - c, sib).wait_recv()
        for cp in first + passed:
            cp.wait_send()

    return start, forward, finish


def _sems(n):
    return [pltpu.SemaphoreType.DMA((n,)), pltpu.SemaphoreType.DMA((n,))]


def _gather_sems(na):
    return _sems(6 * na)


def _scatter_sems(na):
    return _sems(3 * na)


def _pair_sems(na):
    return _sems(N_CHIPS * na)


def _gathered_shape(pack):
    return jax.ShapeDtypeStruct((N_CHIPS,) + pack.shape, pack.dtype)


def _from_sibling_shape(gpack):
    return jax.ShapeDtypeStruct((N_CHIPS,) + gpack.shape[2:], gpack.dtype)


def _all_gather_shards(packs):
    na = len(packs)

    def body(*refs):
        for step in _gather_steps(refs[0:na], refs[na:2 * na], refs[2 * na], refs[2 * na + 1]):
            step()

    return pl.pallas_call(
        body, name="all_gather_shards",
        in_specs=[_ANY] * na, out_specs=[_ANY] * na, out_shape=[_gathered_shape(p) for p in packs],
        scratch_shapes=_gather_sems(na),
    )(*packs)


def _pair_send_steps(g_refs, out_refs, send_sem, recv_sem):
    x, y, c = _place()
    cps = [pltpu.make_async_remote_copy(
        src_ref=g_refs[a].at[j, 1 - c], dst_ref=out_refs[a].at[j],
        send_sem=send_sem.at[N_CHIPS * a + j], recv_sem=recv_sem.at[N_CHIPS * a + j],
        device_id=(x, y, 1 - c), device_id_type=_MESH) for a in range(len(g_refs)) for j in range(N_CHIPS)]

    def start():
        for cp in cps:
            cp.start()

    def finish():
        for cp in cps:
            cp.wait_recv()
        for cp in cps:
            cp.wait_send()

    return start, finish


def _send_half_to_sibling(gpacks, tag):
    na = len(gpacks)

    def body(*refs):
        for step in _pair_send_steps(refs[0:na], refs[na:2 * na], refs[2 * na], refs[2 * na + 1]):
            step()

    return pl.pallas_call(
        body, name=f"pair_send{tag}",
        in_specs=[_ANY] * na, out_specs=[_ANY] * na, out_shape=[_from_sibling_shape(g) for g in gpacks],
        scratch_shapes=_pair_sems(na),
    )(*gpacks)


def _pair_sum(gpack, recv, c_arr, tag, *, tr=PACK_TILE):
    rows, lanes = recv.shape[1:]
    assert rows % tr == 0

    def body(c_ref, a_ref, b_ref, o_ref):
        o_ref[...] = (a_ref[...] + b_ref[...]).astype(BF16)

    grid_spec = pltpu.PrefetchScalarGridSpec(
        num_scalar_prefetch=1, grid=(N_CHIPS, rows // tr),
        in_specs=[pl.BlockSpec((None, None, tr, lanes), lambda j, i, c_ref: (j, c_ref[0], i, 0)),
                  pl.BlockSpec((None, tr, lanes), lambda j, i, c_ref: (j, i, 0))],
        out_specs=pl.BlockSpec((None, tr, lanes), lambda j, i, c_ref: (j, i, 0)))
    return pl.pallas_call(
        body, name=f"pair_sum{tag}", grid_spec=grid_spec,
        out_shape=jax.ShapeDtypeStruct((N_CHIPS, rows, lanes), BF16),
        compiler_params=_params(2),
    )(c_arr, gpack, recv)


def _scatter_steps(p_refs, out_refs, send_sems, recv_sems):
    x, y, c = _place()
    chips = [(1 - x, y), (x, 1 - y), (1 - x, 1 - y)]
    me = 2 * x + y
    cps = [pltpu.make_async_remote_copy(
        src_ref=p_refs[a].at[2 * chip[0] + chip[1]], dst_ref=out_refs[a].at[me],
        send_sem=send_sems.at[3 * a + j], recv_sem=recv_sems.at[3 * a + j],
        device_id=(*chip, c), device_id_type=_MESH) for a in range(len(p_refs)) for j, chip in enumerate(chips)]

    def start():
        for cp in cps:
            cp.start()

    def finish():
        for cp in cps:
            cp.wait_recv()
        for cp in cps:
            cp.wait_send()

    return start, finish


def _share_steps(row_ref, out_ref, send_sems, recv_sems):
    x, y, c = _place()
    mine = 4 * x + 2 * y + c
    cps = []
    for k in range(1, 8):
        fx, fy, fc = (k >> 2) & 1, (k >> 1) & 1, k & 1
        peer = (x + fx - 2 * x * fx, y + fy - 2 * y * fy, c + fc - 2 * c * fc)
        cps.append(pltpu.make_async_remote_copy(
            src_ref=row_ref, dst_ref=out_ref.at[mine], send_sem=send_sems.at[k - 1], recv_sem=recv_sems.at[k - 1],
            device_id=peer, device_id_type=_MESH))

    def start():
        for cp in cps:
            cp.start()

    def finish():
        for cp in cps:
            cp.wait_recv()
        for cp in cps:
            cp.wait_send()

    return start, finish


def _scatter_pieces(psums, tag, share=None):
    na = len(psums)
    ns = 0 if share is None else 1

    def body(*refs):
        n_in = na + ns
        steps = [_scatter_steps(refs[0:na], refs[n_in:n_in + na], refs[2 * n_in], refs[2 * n_in + 1])]
        if ns:
            steps.append(_share_steps(refs[na], refs[n_in + na], refs[2 * n_in + 2], refs[2 * n_in + 3]))
        for phase in range(2):
            for st in steps:
                st[phase]()

    out_shape = [jax.ShapeDtypeStruct(p.shape, p.dtype) for p in psums]
    if ns:
        out_shape.append(jax.ShapeDtypeStruct((8,) + share.shape, share.dtype))
    return pl.pallas_call(
        body, name=f"scatter_pieces{tag}",
        in_specs=[_ANY] * (na + ns), out_specs=[_ANY] * (na + ns), out_shape=out_shape,
        scratch_shapes=_scatter_sems(na) + (_sems(7) if ns else []),
    )(*psums, *([share] if ns else []))


def _sum_pieces(pieces, tag, *, tr=PACK_TILE):
    rows, lanes = pieces.shape[1:]
    assert rows % tr == 0

    def body(p_ref, o_ref):
        acc = p_ref[0].astype(F32) + p_ref[1].astype(F32)
        acc = acc + p_ref[2].astype(F32)
        o_ref[...] = acc + p_ref[3].astype(F32)

    return pl.pallas_call(
        body, name=f"sum_pieces{tag}", grid=(rows // tr,),
        in_specs=[pl.BlockSpec((N_CHIPS, tr, lanes), lambda i: (0, i, 0))],
        out_specs=pl.BlockSpec((tr, lanes), lambda i: (i, 0)),
        out_shape=jax.ShapeDtypeStruct((rows, lanes), F32),
        compiler_params=_params(),
    )(pieces)


def _exchange_halves(totals):
    n = len(totals)

    def body(*refs):
        t_refs, out_refs, send_sem, recv_sem = refs[:n], refs[n:2 * n], refs[2 * n], refs[2 * n + 1]
        x, y, c = _place()
        cps = [pltpu.make_async_remote_copy(
            src_ref=t_refs[i], dst_ref=out_refs[i].at[c], send_sem=send_sem.at[i], recv_sem=recv_sem.at[i],
            device_id=(x, y, 1 - c), device_id_type=_MESH) for i in range(n)]
        for cp in cps:
            cp.start()
        for cp in cps:
            cp.wait_recv()
        for cp in cps:
            cp.wait_send()

    return pl.pallas_call(
        body, name="exchange_halves",
        in_specs=[_ANY] * n, out_specs=[_ANY] * n,
        out_shape=[jax.ShapeDtypeStruct((2,) + t.shape, F32) for t in totals],
        scratch_shapes=[pltpu.SemaphoreType.DMA((n,)), pltpu.SemaphoreType.DMA((n,))],
    )(*totals)


def _pad_rows(a, rows):
    return jnp.pad(a, ((0, rows - a.shape[0]), (0, 0)))


def _pack_weight_shards(w_in, w_mem_kv, w_out, pool_w, w_kv_shared, pool_scale):
    ps_bits = lax.bitcast_convert_type(pool_scale.reshape(-1), BF16).reshape(1, -1)
    ps_row = jnp.pad(ps_bits, ((0, 0), (0, 1024 - ps_bits.shape[1])))

    def common(l):
        return [w_mem_kv[l].astype(BF16).reshape(ROWS_W_MKV, 1024),
                w_out[l].astype(BF16).reshape(ROWS_W_OUT, 1024)]

    p0 = common(0) + [pool_w.astype(BF16).reshape(ROWS_POOL_W, 1024), _pad_rows(ps_row, ROWS_SMALL),
                      jnp.zeros((PACK0_ROWS - OFF_LN_G, 1024), BF16)]
    p1 = common(1) + [_pad_rows(w_kv_shared.astype(BF16).reshape(KV_SHARD, 1024), ROWS_W_KV),
                      jnp.zeros((PACK1_ROWS - OFF_BF, 1024), BF16)]
    w_in_halves = w_in.astype(BF16).reshape(2, 2, D_MODEL // 2, W_IN_SHARD)
    return ([jnp.concatenate(p0, axis=0).reshape(2, PACK0_ROWS // 2, 1024), w_in_halves[0]],
            [jnp.concatenate(p1, axis=0).reshape(2, PACK1_ROWS // 2, 1024), w_in_halves[1]])


def _unpack_w_in(g_in):
    return g_in.reshape(N_CHIPS, D_MODEL, W_IN_SHARD)


def _unpack_common(g):
    w_mkv = g[:, OFF_W_MKV:OFF_W_MKV + ROWS_W_MKV].reshape(D_MODEL, 2 * D_MEM)
    w_out = g[:, OFF_W_OUT:OFF_W_OUT + ROWS_W_OUT].reshape(D_MIX, D_MODEL)
    return w_mkv, w_out


def _unpack_weights0(g):
    pool_w = g[:, OFF_POOL_W:OFF_POOL_W + ROWS_POOL_W].reshape(4, 4, POOL_GROUP // 4, POOL_GROUP)
    pool_w = pool_w.transpose(1, 0, 2, 3).reshape(4, POOL_GROUP, POOL_GROUP)
    ps_bits = g[:, OFF_POOL_S, 0:512].reshape(4, 256, 2)
    pool_scale = lax.bitcast_convert_type(ps_bits, F32).reshape(1, D_MAIN)
    return _unpack_common(g) + (pool_w, pool_scale)


def _unpack_weights1(g):
    w_kv = g[:, OFF_W_KV:OFF_W_KV + KV_SHARD].reshape(4, D_MODEL, KV_SHARD)
    w_kv = w_kv.transpose(1, 0, 2).reshape(D_MODEL, KV_COLS)
    return _unpack_common(g) + (w_kv,)


def _replicated_rows(a):
    a = _pad_rows(a, ROWS_SMALL)
    return jnp.broadcast_to(a[None], (4,) + a.shape)


def _pack_common_grads(g_w_mkv, g_w_out):
    return [g_w_mkv.reshape(4, ROWS_W_MKV, 1024), g_w_out.reshape(4, ROWS_W_OUT, 1024)]


def _w_in_grad_halves(g_w_in):
    return g_w_in.reshape(N_CHIPS, 2, D_MODEL // 2, W_IN_SHARD)


def _pack_grads0(g_w_mkv, g_w_out, g_pool_w, g_pool_scale, g_ln_g, g_ln_b):
    parts = _pack_common_grads(g_w_mkv, g_w_out) + [
        g_pool_w.reshape(4, 4, POOL_GROUP // 4, POOL_GROUP).transpose(1, 0, 2, 3).reshape(4, ROWS_POOL_W, 1024),
        jnp.pad(g_pool_scale.reshape(4, 1, 256), ((0, 0), (0, ROWS_SMALL - 1), (0, 1024 - 256))),
        _replicated_rows(g_ln_g), _replicated_rows(g_ln_b),
        jnp.zeros((4, PACK0_ROWS - OFF_LN_B - ROWS_SMALL, 1024), F32),
    ]
    return jnp.concatenate(parts, axis=1).reshape(4, 2, PACK0_ROWS // 2, 1024)


def _pack_grads1(g_w_mkv, g_w_out, g_w_kv, g_bf):
    parts = _pack_common_grads(g_w_mkv, g_w_out) + [
        jnp.pad(g_w_kv.reshape(D_MODEL, 4, KV_SHARD).transpose(1, 0, 2).reshape(4, KV_SHARD, 1024),
                ((0, 0), (0, ROWS_W_KV - KV_SHARD), (0, 0))),
        _replicated_rows(jnp.pad(g_bf.reshape(1, -1), ((0, 0), (0, 1024 - g_bf.shape[0])))),
        jnp.zeros((4, PACK1_ROWS - OFF_BF - ROWS_SMALL, 1024), F32),
    ]
    return jnp.concatenate(parts, axis=1).reshape(4, 2, PACK1_ROWS // 2, 1024)


def _local_step(x, mem, target, w0, w1, ln_g, ln_b, b_forget, *, tm=256, tq=512, dist=None):
    S = x.shape[0]
    g_rows = [ln_g[l:l + 1] for l in range(2)]
    b_rows = [ln_b[l:l + 1] for l in range(2)]
    bf_row = jnp.pad(b_forget.reshape(1, -1), ((0, 0), (0, LANES - FOX_HEADS)))

    def own_slot(gathered, pack):
        return lax.dynamic_update_slice(gathered, pack[None], (dist["me"], 0, 0, 0))

    if dist is None:
        w_in0, w_mkv0, w_out0, pool_w, pool_scale = w0
        h0 = _linear_fwd(x, w_in0, tm=2 * tm, name="in_proj0")
    else:
        w_in0, pack0 = w0
        h0, g0 = _linear_fwd(x, w_in0, tm=2 * tm, name="in_proj0", gather=[pack0])
        w_mkv0, w_out0, pool_w, pool_scale = _unpack_weights0(
            own_slot(g0, pack0).reshape(N_CHIPS, PACK0_ROWS, 1024))
    mkv0 = _linear_fwd(mem, w_mkv0, tm=N_MEM, name="mem_kv0")
    f0 = _mix("pool", "fwd", h=h0, xres=x, mkv=mkv0, w_out=w_out0, ln_g=g_rows[0], ln_b=b_rows[0],
              pool_w=pool_w, pool_scale=pool_scale, gather=() if dist is None else w1, tm=tm)
    z0, x1 = f0["z"], f0["xout"]
    if dist is not None:
        g1, g1_in = [own_slot(f0[f"gathered{a}"], w1[a]) for a in range(2)]
        w1 = (_unpack_w_in(g1_in),) + _unpack_weights1(g1.reshape(N_CHIPS, PACK1_ROWS, 1024))
    w_in1, w_mkv1, w_out1, w_kv = w1
    w_in, w_out = [w_in0, w_in1], [w_out0, w_out1]
    w_kvp = jnp.pad(w_kv, ((0, 0), (0, LANES - FOX_HEADS)))
    mkv = [mkv0, _linear_fwd(mem, w_mkv1, tm=N_MEM, name="mem_kv1")]
    ka, va, fl, cum = _kv_proj(x1, w_kvp, bf_row, tm=tm)
    h1, qf = _linear_fwd(x1, w_in[1], tm=2 * tm, name="in_proj1", q_cum=cum)
    ymain1, qb = _fox_fwd(qf, ka, va, tq=tq)

    b1 = _mix("fox", "bwd", h=h1, xres=x1, mkv=mkv[1], w_out=w_out[1], ln_g=g_rows[1], ln_b=b_rows[1],
              ymain=ymain1, target=target, tm=tm)
    dq_aug, dk, dv, dck_rows = _fox_bwd(qb, ka, va, b1["dmain"], tq=tq)
    du1, df, dbf = _gate_bwd(dq_aug, _cols_of_rows(dck_rows, S), fl, tm=2 * tm)

    dx1a, dw_in1 = _lin_bwd(x1, [du1, b1["drest"]], w_in[1], [(b1["dz"], ALPHA)], tm=tm, name="in_proj1_bwd")
    dx1, dw_kvp = _lin_bwd(x1, [dk, dv, df], w_kvp, [(dx1a, 1.0)], tm=tm, name="kv_proj_bwd")

    dw_mkv1 = _wgrad(mem, b1["dmkv"], name="mem_kv1_bwd")
    g_w_kv, g_bf = dw_kvp[:, 0:KV_COLS], dbf[0, 0:FOX_HEADS]

    gpacks1, psums1 = (), ()
    if dist is not None:
        gpacks1 = [_pack_grads1(dw_mkv1, b1["dw_out"], g_w_kv, g_bf), _w_in_grad_halves(dw_in1)]
    b0 = _mix("pool", "bwd", h=h0, mkv=mkv[0], w_out=w_out[0], ln_g=g_rows[0],
              pool_w=pool_w, pool_scale=pool_scale, z=z0, dy=dx1, pair_send=gpacks1, tm=tm)
    if dist is not None:
        psums1 = [_pair_sum(g, b0[f"from_sibling{a}"], dist["c_arr"], f"1{'ab'[a]}") for a, g in enumerate(gpacks1)]
    outs = _lin_bwd(x, [b0["dmain"], b0["drest"]], w_in[0], [(b0["dz"], ALPHA)], tm=tm, name="in_proj0_bwd",
                    scatter=psums1)
    dx, dw_in0 = outs[0], outs[1]
    dw_mkv0 = _wgrad(mem, b0["dmkv"], name="mem_kv0_bwd")
    g_ln_g = jnp.concatenate([b0["dln_g"], b1["dln_g"]], axis=0)
    g_ln_b = jnp.concatenate([b0["dln_b"], b1["dln_b"]], axis=0)

    if dist is None:
        grads = dict(w_in=[dw_in0, dw_in1], w_mem_kv=[dw_mkv0, dw_mkv1], w_out=[b0["dw_out"], b1["dw_out"]],
                     ln_g=g_ln_g, ln_b=g_ln_b, pool_w=b0["dpool_w"], pool_scale=b0["dpool_scale"],
                     w_kv=g_w_kv, b_forget=g_bf)
        return b1["loss"], dx, grads

    me, my_c = dist["me"], dist["my_c"]

    def with_own(pieces, psum):
        own = lax.dynamic_slice(psum, (me, 0, 0), (1,) + psum.shape[1:])
        return lax.dynamic_update_slice(pieces, own, (me, 0, 0))

    totals1 = [_sum_pieces(with_own(outs[2 + a], p), f"1{'ab'[a]}") for a, p in enumerate(psums1)]
    gpacks0 = [_pack_grads0(dw_mkv0, b0["dw_out"], b0["dpool_w"], b0["dpool_scale"], g_ln_g, g_ln_b),
               _w_in_grad_halves(dw_in0)]
    sib0 = _send_half_to_sibling(gpacks0, 0)
    psums0 = [_pair_sum(g, sib0[a], dist["c_arr"], f"0{'ab'[a]}") for a, g in enumerate(gpacks0)]
    loss_row = jnp.broadcast_to(0.5 / D_MODEL * jnp.sum(b1["loss"]), (8, LANES))
    pieces0 = _scatter_pieces(psums0, 0, share=loss_row)
    losses = lax.dynamic_update_slice(pieces0[2], loss_row[None], (2 * me + my_c, 0, 0))
    loss = jnp.sum(losses[:, 0, 0])
    totals0 = [_sum_pieces(with_own(pieces0[a], p), f"0{'ab'[a]}") for a, p in enumerate(psums0)]
    totals = totals0 + totals1
    halves = _exchange_halves(totals)
    full = [lax.dynamic_update_slice(h, t[None], (my_c, 0, 0)) for h, t in zip(halves, totals)]
    shard0, shard1 = full[0].reshape(PACK0_ROWS, 1024), full[2].reshape(PACK1_ROWS, 1024)
    g_w_in = jnp.stack([full[1].reshape(D_MODEL, W_IN_SHARD), full[3].reshape(D_MODEL, W_IN_SHARD)])
    return loss, dx, shard0, shard1, g_w_in


def kernel(x, mem, w_in, w_mem_kv, w_out, ln_g, ln_b, pool_w, pool_scale, w_kv_shared, b_forget, loss_target, m_w_in, m_w_mem_kv, m_w_out, m_ln_g, m_ln_b, m_pool_w, m_pool_scale, m_w_kv_shared, m_b_forget, v_w_in, v_w_mem_kv, v_w_out, v_ln_g, v_ln_b, v_pool_w, v_pool_scale, v_w_kv_shared, v_b_forget):
    dist = dict(c_arr=lax.axis_index("c").astype(jnp.int32).reshape(1),
                me=2 * lax.axis_index("x") + lax.axis_index("y"), my_c=lax.axis_index("c"))

    wpacks0, wpacks1 = _pack_weight_shards(w_in, w_mem_kv, w_out, pool_w, w_kv_shared, pool_scale)
    g0_in = lax.dynamic_update_slice(_all_gather_shards([wpacks0[1]])[0], wpacks0[1][None], (dist["me"], 0, 0, 0))
    w0 = (_unpack_w_in(g0_in), wpacks0[0])

    loss, dx, shard0, shard1, g_w_in = _local_step(x[0], mem[0], loss_target[0], w0, wpacks1, ln_g, ln_b,
                                                   b_forget, dist=dist)

    def per_layer(off, rows, shape):
        return jnp.concatenate([shard0[off:off + rows], shard1[off:off + rows]], axis=0).reshape(shape)

    g_w_mkv = per_layer(OFF_W_MKV, ROWS_W_MKV, w_mem_kv.shape)
    g_w_out = per_layer(OFF_W_OUT, ROWS_W_OUT, w_out.shape)
    g_pool_w = shard0[OFF_POOL_W:OFF_POOL_W + ROWS_POOL_W].reshape(pool_w.shape)
    g_w_kv = shard1[OFF_W_KV:OFF_W_KV + KV_SHARD].reshape(w_kv_shared.shape)
    g_pool_scale = shard0[OFF_POOL_S:OFF_POOL_S + 1, 0:256].reshape(pool_scale.shape)
    g_ln_g = shard0[OFF_LN_G:OFF_LN_G + 2]
    g_ln_b = shard0[OFF_LN_B:OFF_LN_B + 2]
    g_bf = shard1[OFF_BF, 0:FOX_HEADS]

    names = ["w_in", "w_mem_kv", "w_out", "ln_g", "ln_b", "pool_w", "pool_scale", "w_kv_shared", "b_forget"]
    ws = [w_in, w_mem_kv, w_out, ln_g, ln_b, pool_w, pool_scale, w_kv_shared, b_forget]
    gs = [g_w_in, g_w_mkv, g_w_out, g_ln_g, g_ln_b, g_pool_w, g_pool_scale, g_w_kv, g_bf]
    ms = [m_w_in, m_w_mem_kv, m_w_out, m_ln_g, m_ln_b, m_pool_w, m_pool_scale, m_w_kv_shared, m_b_forget]
    vs = [v_w_in, v_w_mem_kv, v_w_out, v_ln_g, v_ln_b, v_pool_w, v_pool_scale, v_w_kv_shared, v_b_forget]
    deltas, new_ms, new_vs = [], [], []
    for nm, w, gg, mm, vv in zip(names, ws, gs, ms, vs):
        two_d = (-1, w.shape[-1])
        d, nmm, nvv = _adamw(w.reshape(two_d), gg.reshape(two_d), mm.reshape(two_d), vv.reshape(two_d),
                             name=f"adamw_{nm}")
        deltas.append(d.reshape(w.shape))
        new_ms.append(nmm.reshape(w.shape))
        new_vs.append(nvv.reshape(w.shape))

    return (loss, dx[None], *gs, *deltas, *new_ms, *new_vs)
```

```python
import jax
import jax.numpy as jnp
from jax import lax
from jax.experimental import pallas as pl
from jax.experimental.pallas import tpu as pltpu

F32 = jnp.float32
BF16 = jnp.bfloat16

D_MODEL = 1024
D_MAIN = 1024
D_MEM = 512
D_MIX = D_MAIN + D_MEM
D_IN = 2 * D_MIX
N_MEM = 256
MEM_HEADS = 4
MEM_HEAD_DIM = 128
FOX_HEADS = 16
FOX_HEAD_DIM = 64
FOX_SCALE = 0.125
POOL_WINDOWS = (2, 4, 8, 16)
POOL_GROUP = 256
POOL_HALO = 16
ALPHA = 4.0 ** 0.25
LN_EPS = 1e-5
LANES = 128
N_CHIPS = 4

ADAM_LR = 0.001
ADAM_B1 = 0.9
ADAM_B2 = 0.999
ADAM_EPS = 1e-08
ADAM_WD = 0.01
ADAM_STEP = 10

V7X_VMEM_BYTES = 64 * 1024 * 1024
VMEM_LIMIT = V7X_VMEM_BYTES - 8 * 1024 * 1024

ROWS_W_MKV = (D_MODEL // N_CHIPS) * 2 * D_MEM // 1024
ROWS_W_OUT = (D_MIX // N_CHIPS) * D_MODEL // 1024
ROWS_POOL_W = 4 * (POOL_GROUP // N_CHIPS) * POOL_GROUP // 1024
KV_COLS = 2 * D_MAIN + FOX_HEADS
KV_SHARD = KV_COLS // N_CHIPS
ROWS_W_KV = 528
ROWS_SMALL = 16
OFF_W_MKV = 0
OFF_W_OUT = OFF_W_MKV + ROWS_W_MKV
OFF_TAIL = OFF_W_OUT + ROWS_W_OUT
OFF_POOL_W = OFF_TAIL
OFF_POOL_S = OFF_POOL_W + ROWS_POOL_W
OFF_LN_G = OFF_POOL_S + ROWS_SMALL
OFF_LN_B = OFF_LN_G + ROWS_SMALL
PACK0_ROWS = 768
OFF_W_KV = OFF_TAIL
OFF_BF = OFF_W_KV + ROWS_W_KV
PACK1_ROWS = 1280
PACK_TILE = 128
W_IN_SHARD = D_IN // N_CHIPS


def _dot(a, b):
    return jnp.dot(a, b, preferred_element_type=F32)


def _dot_nt(a, b):
    return lax.dot_general(a, b, (((1,), (1,)), ((), ())), preferred_element_type=F32)


def _dot_tn(a, b):
    return lax.dot_general(a, b, (((0,), (0,)), ((), ())), preferred_element_type=F32)


def _params(n_axes=1):
    return pltpu.CompilerParams(dimension_semantics=("arbitrary",) * n_axes,
                                vmem_limit_bytes=VMEM_LIMIT)


def _const_spec(shape):
    zeros = (0,) * len(shape)
    return pl.BlockSpec(shape, lambda *_: zeros, pipeline_mode=pl.Buffered(1))


def _split3(x):
    hi = x.astype(BF16)
    r = x - hi.astype(F32)
    mid = r.astype(BF16)
    lo = (r - mid.astype(F32)).astype(BF16)
    return hi, mid, lo


def _cols_and_chunk(w):
    if w.ndim == 3:
        return N_CHIPS * w.shape[2], 256
    return w.shape[1], (512 if w.shape[1] % 512 == 0 else LANES)


def _w_cols(w_ref, n0, nc):
    if len(w_ref.shape) == 3:
        per = w_ref.shape[2]
        assert n0 // per == (n0 + nc - 1) // per
        return w_ref.at[n0 // per, :, n0 % per:n0 % per + nc]
    return w_ref.at[:, n0:n0 + nc]


def _linear_fwd(x, w, *, tm, name, q_cum=None, gather=()):
    S, K = x.shape
    N, nc = _cols_and_chunk(w)
    aug = q_cum is not None
    ng = len(gather)
    n_tiles = S // tm
    assert not (aug and ng)

    def body(*refs):
        x_ref, w_ref = refs[0], refs[1]
        o_ref = refs[4] if aug else refs[2 + ng]
        if ng:
            i = pl.program_id(0)
            start, forward, finish = _gather_steps(refs[2:2 + ng], refs[3 + ng:3 + 2 * ng],
                                                   refs[3 + 2 * ng], refs[4 + 2 * ng])
            pl.when(i == 0)(start)
            pl.when(i == max(n_tiles - 3, 0))(forward)
            pl.when(i == n_tiles - 1)(finish)
        xb = x_ref[...].astype(BF16)
        if aug:
            lhs = _placement_lhs(refs[2][...], tm)
        for n0 in range(0, N, nc):
            r = _dot(xb, _w_cols(w_ref, n0, nc)[...])
            o_ref[:, n0:n0 + nc] = r.astype(BF16)
            if aug and n0 < D_MAIN:
                _store_augmented(refs[5], r * FOX_SCALE, n0 // LANES, lhs, refs[3])

    in_specs = [pl.BlockSpec((tm, K), lambda i: (i, 0)), _const_spec(w.shape)]
    out_specs = [pl.BlockSpec((tm, N), lambda i: (i, 0))]
    out_shape = [jax.ShapeDtypeStruct((S, N), BF16)]
    extra = []
    if aug:
        in_specs += [pl.BlockSpec((tm, LANES), lambda i: (i, 0)), _const_spec((LANES, AUG_W))]
        out_specs.append(pl.BlockSpec((tm, AUG_W), lambda i: (i, 0)))
        out_shape.append(jax.ShapeDtypeStruct((S, AUG_W), BF16))
        extra = [q_cum, _placement(AUG_A, (AUG_B,))]
    if ng:
        in_specs += [_ANY] * ng
        out_specs += [_ANY] * ng
        out_shape += [_gathered_shape(p) for p in gather]
        extra = list(gather)
    outs = pl.pallas_call(
        body, name=name, grid=(n_tiles,),
        in_specs=in_specs, out_specs=out_specs, out_shape=out_shape,
        scratch_shapes=_gather_sems(ng) if ng else [],
        compiler_params=_params(),
    )(x, w, *extra)
    return outs if (aug or ng) else outs[0]


def _kv_proj(x1, w_kv, bf_row, *, tm):
    S = x1.shape[0]

    def body(x_ref, w_ref, b_ref, pk_ref, pv_ref, k_ref, v_ref, fl_ref, cum_ref, carry_ref):
        i = pl.program_id(0)

        @pl.when(i == 0)
        def _():
            carry_ref[...] = jnp.zeros_like(carry_ref)

        xb = x_ref[...].astype(BF16)
        fl = _dot(xb, w_ref[:, 2 * D_MAIN:2 * D_MAIN + LANES]) + b_ref[...]
        fl_ref[...] = fl
        log_f = jnp.minimum(fl, 0.0) - jnp.log1p(jnp.exp(-jnp.abs(fl)))
        r = lax.broadcasted_iota(jnp.int32, (tm, tm), 0)
        c = lax.broadcasted_iota(jnp.int32, (tm, tm), 1)
        tri = jnp.where(c <= r, 1.0, 0.0).astype(BF16)
        hi, mid, lo = _split3(log_f)
        cum = (_dot(tri, hi) + _dot(tri, mid)) + _dot(tri, lo) + carry_ref[0:1, :]
        cum_ref[...] = cum
        carry_ref[0:1, :] = cum[tm - 1:tm, :]
        lhs_k, lhs_v = _placement_lhs(-cum, tm), _placement_lhs(None, tm)
        for n0 in range(0, D_MAIN, 512):
            _store_augmented(k_ref, _dot(xb, w_ref[:, n0:n0 + 512]), n0 // LANES, lhs_k, pk_ref)
            _store_augmented(v_ref, _dot(xb, w_ref[:, D_MAIN + n0:D_MAIN + n0 + 512]), n0 // LANES, lhs_v, pv_ref)

    return pl.pallas_call(
        body, name="kv_proj", grid=(S // tm,),
        in_specs=[pl.BlockSpec((tm, D_MODEL), lambda i: (i, 0)),
                  _const_spec((D_MODEL, 2 * D_MAIN + LANES)), _const_spec((1, LANES)),
                  _const_spec((LANES, AUG_W)), _const_spec((LANES, AUG_W))],
        out_specs=[pl.BlockSpec((tm, AUG_W), lambda i: (i, 0)),
                   pl.BlockSpec((tm, AUG_W), lambda i: (i, 0)),
                   pl.BlockSpec((tm, LANES), lambda i: (i, 0)),
                   pl.BlockSpec((tm, LANES), lambda i: (i, 0))],
        out_shape=[jax.ShapeDtypeStruct((S, AUG_W), BF16), jax.ShapeDtypeStruct((S, AUG_W), BF16),
                   jax.ShapeDtypeStruct((S, LANES), F32), jax.ShapeDtypeStruct((S, LANES), F32)],
        scratch_shapes=[pltpu.VMEM((8, LANES), F32)],
        compiler_params=_params(),
    )(x1, w_kv, bf_row, _placement(AUG_B, (AUG_A, AUG_C)), _placement(None, (AUG_A,)))


def _gate_bwd(dq_aug, dck, fl, *, tm):
    S = fl.shape[0]
    n = S // tm

    def body(dq_ref, dck_ref, fl_ref, du_ref, df_ref, db_ref, carry_ref):
        i = pl.program_id(0)

        @pl.when(i == 0)
        def _():
            carry_ref[...] = jnp.zeros_like(carry_ref)
            db_ref[...] = jnp.zeros_like(db_ref)

        lane = lax.broadcasted_iota(jnp.int32, (tm, LANES), 1)
        half0 = lane < FOX_HEAD_DIM
        dcq = jnp.zeros((tm, LANES), F32)
        for hp in range(FOX_HEADS // 2):
            b0 = dq_ref[:, 2 * hp * LANES:(2 * hp + 1) * LANES]
            b1 = dq_ref[:, (2 * hp + 1) * LANES:(2 * hp + 2) * LANES]
            du_ref[:, hp * LANES:(hp + 1) * LANES] = (
                jnp.where(half0, b0, pltpu.roll(b1, FOX_HEAD_DIM, 1)) * FOX_SCALE).astype(BF16)
            r0 = jnp.sum(jnp.where(lane == AUG_A, b0, 0.0), axis=1, keepdims=True)
            r1 = jnp.sum(jnp.where(lane == AUG_A, b1, 0.0), axis=1, keepdims=True)
            dcq = dcq + jnp.where(lane == 2 * hp, r0, 0.0) + jnp.where(lane == 2 * hp + 1, r1, 0.0)
        dcum = dcq - dck_ref[...]
        r = lax.broadcasted_iota(jnp.int32, (tm, tm), 0)
        c = lax.broadcasted_iota(jnp.int32, (tm, tm), 1)
        tri = jnp.where(c >= r, 1.0, 0.0).astype(BF16)
        hi, mid, lo = _split3(dcum)
        rev = (_dot(tri, hi) + _dot(tri, mid)) + _dot(tri, lo) + carry_ref[0:1, :]
        carry_ref[0:1, :] = rev[0:1, :]
        fl_v = fl_ref[...]
        df = rev * (1.0 / (1.0 + jnp.exp(fl_v)))
        df_ref[...] = df
        db_ref[...] += jnp.sum(df, axis=0, keepdims=True)

    return pl.pallas_call(
        body, name="gate_bwd", grid=(n,),
        in_specs=[pl.BlockSpec((tm, AUG_W), lambda i: (n - 1 - i, 0)),
                  pl.BlockSpec((tm, LANES), lambda i: (n - 1 - i, 0)),
                  pl.BlockSpec((tm, LANES), lambda i: (n - 1 - i, 0))],
        out_specs=[pl.BlockSpec((tm, D_MAIN), lambda i: (n - 1 - i, 0)),
                   pl.BlockSpec((tm, LANES), lambda i: (n - 1 - i, 0)),
                   pl.BlockSpec((1, LANES), lambda i: (0, 0))],
        out_shape=[jax.ShapeDtypeStruct((S, D_MAIN), BF16),
                   jax.ShapeDtypeStruct((S, LANES), F32), jax.ShapeDtypeStruct((1, LANES), F32)],
        scratch_shapes=[pltpu.VMEM((8, LANES), F32)],
        compiler_params=_params(),
    )(dq_aug, dck, fl)


def _silu_and_grad(g):
    sg = 1.0 / (1.0 + jnp.exp(-g))
    return g * sg, sg * (1.0 + g * (1.0 - sg))


def _mix(kind, mode, *, h, xres=None, mkv, w_out, ln_g, ln_b=None, pool_w=None, pool_scale=None,
         ymain=None, target=None, z=None, dy=None, gather=(), pair_send=(), tm):
    S = h.shape[0]
    n = S // tm
    pool = kind == "pool"
    bwd = mode == "bwd"
    loss_head = bwd and not pool
    rev = pool and bwd
    mem_scale = MEM_HEAD_DIM ** -0.5

    def t_of(i):
        return (n - 1 - i) if rev else i

    row = lambda i: (t_of(i), 0)
    names, arrays, specs = [], [], []

    def add(name, arr, spec):
        names.append(name)
        arrays.append(arr)
        specs.append(spec)

    add("h", h, pl.BlockSpec((tm, D_IN), row))
    if pool:
        hb = tm // POOL_HALO
        add("halo", h, pl.BlockSpec((POOL_HALO, D_MAIN), lambda i: (jnp.maximum(t_of(i) * hb - 1, 0), 0)))
        add("pool_w", pool_w, _const_spec((4, POOL_GROUP, POOL_GROUP)))
        add("pool_scale", pool_scale, _const_spec((1, D_MAIN)))
    else:
        add("ymain", ymain, pl.BlockSpec((tm, D_MAIN), row))
    add("mkv", mkv, _const_spec((N_MEM, 2 * D_MEM)))
    add("w_out", w_out, _const_spec((D_MIX, D_MODEL)))
    add("ln_g", ln_g, _const_spec((1, D_MODEL)))
    if not (pool and bwd):
        add("xres", xres, pl.BlockSpec((tm, D_MODEL), row))
        add("ln_b", ln_b, _const_spec((1, D_MODEL)))
    if loss_head:
        add("target", target, pl.BlockSpec((tm, D_MODEL), row))
        add("place_do", _placement(AUG_A, ()), _const_spec((LANES, AUG_W)))
    if pool and bwd:
        add("z", z, pl.BlockSpec((tm, D_MODEL), row))
        add("dy", dy, pl.BlockSpec((tm, D_MODEL), row))
    for a, p in enumerate(gather):
        add(f"gather_src{a}", p, _ANY)
    for a, p in enumerate(pair_send):
        add(f"pair_src{a}", p, _ANY)

    onames, oshapes, ospecs = [], [], []

    def add_out(name, shape, dtype, spec):
        onames.append(name)
        oshapes.append(jax.ShapeDtypeStruct(shape, dtype))
        ospecs.append(spec)

    const2 = lambda i: (0, 0)
    if not bwd:
        add_out("z", (S, D_MODEL), F32, pl.BlockSpec((tm, D_MODEL), row))
        add_out("xout", (S, D_MODEL), F32, pl.BlockSpec((tm, D_MODEL), row))
    else:
        add_out("dz", (S, D_MODEL), F32, pl.BlockSpec((tm, D_MODEL), row))
        if pool:
            add_out("dmain", (S, D_MAIN), BF16, pl.BlockSpec((tm, D_MAIN), row))
        else:
            add_out("dmain", (S, 2 * D_MAIN), BF16, pl.BlockSpec((tm, 2 * D_MAIN), row))
        add_out("drest", (S, D_IN - D_MAIN), BF16, pl.BlockSpec((tm, D_IN - D_MAIN), row))
        add_out("dw_out", (D_MIX, D_MODEL), F32, pl.BlockSpec((D_MIX, D_MODEL), const2))
        add_out("dmkv", (N_MEM, 2 * D_MEM), F32, pl.BlockSpec((N_MEM, 2 * D_MEM), const2))
        add_out("dln_g", (1, D_MODEL), F32, pl.BlockSpec((1, D_MODEL), const2))
        add_out("dln_b", (1, D_MODEL), F32, pl.BlockSpec((1, D_MODEL), const2))
        if pool:
            add_out("dpool_w", (4, POOL_GROUP, POOL_GROUP), F32,
                    pl.BlockSpec((4, POOL_GROUP, POOL_GROUP), lambda i: (0, 0, 0)))
            add_out("dpool_scale", (1, D_MAIN), F32, pl.BlockSpec((1, D_MAIN), const2))
        else:
            add_out("loss", (1, D_MODEL), F32, pl.BlockSpec((1, D_MODEL), const2))

    for a, p in enumerate(gather):
        add_out(f"gathered{a}", (N_CHIPS,) + p.shape, p.dtype, _ANY)
    for a, p in enumerate(pair_send):
        add_out(f"from_sibling{a}", (N_CHIPS,) + p.shape[2:], p.dtype, _ANY)

    scratch = [pltpu.VMEM((tm, D_MIX), BF16),
               pltpu.VMEM((tm, D_MEM), F32)]
    if pool:
        scratch.append(pltpu.VMEM((tm + 2 * POOL_HALO, D_MAIN), F32))
    if rev:
        scratch.append(pltpu.VMEM((POOL_HALO, D_MAIN), F32))
    assert not (gather and pair_send)
    if gather:
        scratch += _gather_sems(len(gather))
    if pair_send:
        scratch += _pair_sems(len(pair_send))
    n_in, n_out = len(names), len(onames)

    def body(*refs):
        R = dict(zip(names, refs[:n_in]))
        O = dict(zip(onames, refs[n_in:n_in + n_out]))
        sc = refs[n_in + n_out:]
        yc_ref, ymem_ref = sc[0], sc[1]
        ext_ref = sc[2] if pool else None
        carry_ref = sc[3] if rev else None
        i = pl.program_id(0)
        t = t_of(i)
        h_ref = R["h"]
        gamma = R["ln_g"][...]

        if gather:
            start, forward, finish = _gather_steps(
                [R[f"gather_src{a}"] for a in range(len(gather))],
                [O[f"gathered{a}"] for a in range(len(gather))], sc[-2], sc[-1])
            pl.when(i == 0)(start)
            pl.when(i == max(n - 3, 0))(forward)
            pl.when(i == n - 1)(finish)
        if pair_send:
            start, finish = _pair_send_steps(
                [R[f"pair_src{a}"] for a in range(len(pair_send))],
                [O[f"from_sibling{a}"] for a in range(len(pair_send))], sc[-2], sc[-1])
            pl.when(i == 0)(start)
            pl.when(i == n - 1)(finish)

        if bwd:
            @pl.when(i == 0)
            def _():
                for nm in ("dw_out", "dmkv", "dln_g", "dln_b", "dpool_w", "dpool_scale", "loss"):
                    if nm in O:
                        O[nm][...] = jnp.zeros_like(O[nm])
                if rev:
                    carry_ref[...] = jnp.zeros_like(carry_ref)

        if pool:
            u = h_ref[:, 0:D_MAIN].astype(F32)
            halo = R["halo"][...].astype(F32)
            ext_ref[0:POOL_HALO, :] = jnp.where(t > 0, halo, 0.0)
            ext_ref[POOL_HALO:POOL_HALO + tm, :] = u
            tpos = t * tm + lax.broadcasted_iota(jnp.int32, (tm, 1), 0)
            pms, invcs = [], []
            for gi, w in enumerate(POOL_WINDOWS):
                cs = slice(gi * POOL_GROUP, (gi + 1) * POOL_GROUP)
                acc = ext_ref[POOL_HALO:POOL_HALO + tm, cs]
                for k in range(1, w):
                    acc = acc + ext_ref[POOL_HALO - k:POOL_HALO - k + tm, cs]
                invc = 1.0 / jnp.minimum(tpos + 1, w).astype(F32)
                pm = (acc * invc - u[:, cs]).astype(BF16)
                pms.append(pm)
                invcs.append(invc)
            mixed = [_dot(pms[gi], R["pool_w"][gi]) for gi in range(4)]
            ps = R["pool_scale"][...]
            y_main = [mixed[gi] * ps[:, gi * POOL_GROUP:(gi + 1) * POOL_GROUP] for gi in range(4)]
        else:
            y_main = [R["ymain"][:, gi * 256:(gi + 1) * 256].astype(F32) for gi in range(4)]

        probs = []
        for hd in range(MEM_HEADS):
            sl = slice(D_MAIN + hd * MEM_HEAD_DIM, D_MAIN + (hd + 1) * MEM_HEAD_DIM)
            ksl = slice(hd * MEM_HEAD_DIM, (hd + 1) * MEM_HEAD_DIM)
            vsl = slice(D_MEM + hd * MEM_HEAD_DIM, D_MEM + (hd + 1) * MEM_HEAD_DIM)
            s = _dot_nt(h_ref[:, sl], R["mkv"][:, ksl]) * mem_scale
            e = jnp.exp(s - jnp.max(s, axis=1, keepdims=True))
            p = e / jnp.sum(e, axis=1, keepdims=True)
            probs.append(p)
            ymem_ref[:, ksl] = _dot(p.astype(BF16), R["mkv"][:, vsl])

        g_off = D_MIX
        gate_d = []
        for gi in range(4):
            cs = slice(gi * 256, (gi + 1) * 256)
            gm = h_ref[:, g_off + gi * 256:g_off + (gi + 1) * 256].astype(F32)
            sv, sd = _silu_and_grad(gm)
            yc_ref[:, cs] = (y_main[gi] * sv).astype(BF16)
            gate_d.append((sv, sd))
        gq = h_ref[:, g_off + D_MAIN:D_IN].astype(F32)
        svq, sdq = _silu_and_grad(gq)
        yc_ref[:, D_MAIN:D_MIX] = (ymem_ref[...] * svq).astype(BF16)

        if pool and bwd:
            zt = R["z"][...]
        else:
            o = _dot(yc_ref[...], R["w_out"][...])
            zt = ALPHA * R["xres"][...] + o
        mu = jnp.mean(zt, axis=1, keepdims=True)
        zc = zt - mu
        var = jnp.mean(zc * zc, axis=1, keepdims=True)
        rstd = lax.rsqrt(var + LN_EPS)
        xhat = zc * rstd
        if not bwd:
            O["z"][...] = zt
            O["xout"][...] = xhat * gamma + R["ln_b"][...]
            return

        if loss_head:
            xo = xhat * gamma + R["ln_b"][...]
            err = xo - R["target"][...]
            O["loss"][...] += jnp.sum(err * err, axis=0, keepdims=True)
            dyt = err * (1.0 / D_MODEL)
        else:
            dyt = R["dy"][...]

        O["dln_g"][...] += jnp.sum(dyt * xhat, axis=0, keepdims=True)
        O["dln_b"][...] += jnp.sum(dyt, axis=0, keepdims=True)
        gdy = dyt * gamma
        m1 = jnp.mean(gdy, axis=1, keepdims=True)
        m2 = jnp.mean(gdy * xhat, axis=1, keepdims=True)
        dz = rstd * (gdy - m1 - xhat * m2)
        O["dz"][...] = dz
        dzb = dz.astype(BF16)

        for n0 in range(0, D_MIX, 512):
            O["dw_out"][n0:n0 + 512, :] += _dot_tn(yc_ref[:, n0:n0 + 512], dzb)
        dyc_mem = _dot_nt(dzb, R["w_out"][D_MAIN:D_MIX, :])

        O["drest"][:, D_MEM + D_MAIN:D_MEM + D_MAIN + D_MEM] = (dyc_mem * ymem_ref[...] * sdq).astype(BF16)
        dymem = dyc_mem * svq
        for hd in range(MEM_HEADS):
            sl = slice(D_MAIN + hd * MEM_HEAD_DIM, D_MAIN + (hd + 1) * MEM_HEAD_DIM)
            ksl = slice(hd * MEM_HEAD_DIM, (hd + 1) * MEM_HEAD_DIM)
            vsl = slice(D_MEM + hd * MEM_HEAD_DIM, D_MEM + (hd + 1) * MEM_HEAD_DIM)
            p = probs[hd]
            dyb = dymem[:, ksl].astype(BF16)
            dp = _dot_nt(dyb, R["mkv"][:, vsl])
            ds = p * (dp - jnp.sum(dp * p, axis=1, keepdims=True)) * mem_scale
            dsb = ds.astype(BF16)
            O["drest"][:, ksl] = _dot(dsb, R["mkv"][:, ksl]).astype(BF16)
            O["dmkv"][:, ksl] += _dot_tn(dsb, h_ref[:, sl])
            O["dmkv"][:, vsl] += _dot_tn(p.astype(BF16), dyb)

        dmain = []
        for gi in range(4):
            cs = slice(gi * 256, (gi + 1) * 256)
            dyc_g = _dot_nt(dzb, R["w_out"][cs, :])
            sv, sd = gate_d[gi]
            O["drest"][:, D_MEM + gi * 256:D_MEM + (gi + 1) * 256] = (dyc_g * y_main[gi] * sd).astype(BF16)
            dmain.append(dyc_g * sv)

        if not pool:
            dbf = [dmain[gi].astype(BF16).astype(F32) for gi in range(4)]
            dcol = jnp.zeros((tm, LANES), F32)
            for gi in range(4):
                dr = lax.broadcasted_iota(jnp.int32, (256, LANES), 0)
                hc = lax.broadcasted_iota(jnp.int32, (256, LANES), 1)
                sel = jnp.where(jnp.right_shift(dr, 6) + gi * 4 == hc, 1.0, 0.0).astype(BF16)
                hi, mid, lo = _split3(dbf[gi] * R["ymain"][:, gi * 256:(gi + 1) * 256].astype(F32))
                dcol = dcol + ((_dot(hi, sel) + _dot(mid, sel)) + _dot(lo, sel))
            lhs = _placement_lhs(-dcol, tm)
            for gi in range(4):
                _store_augmented(O["dmain"], dbf[gi], 2 * gi, lhs, R["place_do"])
            return

        ps = R["pool_scale"][...]
        dpm_list = []
        for gi in range(4):
            cs = slice(gi * 256, (gi + 1) * 256)
            O["dpool_scale"][:, cs] += jnp.sum(dmain[gi] * mixed[gi], axis=0, keepdims=True)
            dmix = (dmain[gi] * ps[:, cs]).astype(BF16)
            O["dpool_w"][gi] += _dot_tn(pms[gi], dmix)
            dpm = _dot_nt(dmix, R["pool_w"][gi])
            dpm_list.append(dpm)
            ext_ref[0:tm, cs] = dpm * invcs[gi]
        ext_ref[tm:tm + POOL_HALO, :] = carry_ref[...]
        carry_ref[...] = ext_ref[0:POOL_HALO, :]
        for gi, w in enumerate(POOL_WINDOWS):
            cs = slice(gi * 256, (gi + 1) * 256)
            acc = ext_ref[0:tm, cs]
            for k in range(1, w):
                acc = acc + ext_ref[k:k + tm, cs]
            O["dmain"][:, cs] = (acc - dpm_list[gi]).astype(BF16)

    outs = pl.pallas_call(
        body, name=f"mix_{kind}_{mode}", grid=(n,),
        in_specs=specs, out_specs=ospecs, out_shape=oshapes,
        scratch_shapes=scratch, compiler_params=_params(),
    )(*arrays)
    return dict(zip(onames, outs))


def _lin_bwd(xin, dhs, w, res, *, tm, name, scatter=()):
    S, K = xin.shape
    N, nc_w = _cols_and_chunk(w)
    nj, nr, ns = len(dhs), len(res), len(scatter)
    n_tiles = S // tm
    widths = [dh.shape[1] for dh in dhs]
    chunks = [nc_w if w.ndim == 3 else (512 if wd % 512 == 0 else wd) for wd in widths]
    assert sum(widths) == N and all(wd % c == 0 for wd, c in zip(widths, chunks))
    scales = [s for _, s in res]
    n_in = 2 + nj + nr + ns

    def body(*refs):
        x_ref = refs[0]
        dh_refs = refs[1:1 + nj]
        w_ref = refs[1 + nj]
        r_refs = refs[2 + nj:2 + nj + nr]
        dx_ref, dw_ref = refs[n_in], refs[n_in + 1]
        i = pl.program_id(0)

        if ns:
            start, finish = _scatter_steps(refs[n_in - ns:n_in], refs[n_in + 2:n_in + 2 + ns],
                                           refs[n_in + 2 + ns], refs[n_in + 3 + ns])
            pl.when(i == 0)(start)
            pl.when(i == n_tiles - 1)(finish)

        @pl.when(i == 0)
        def _():
            dw_ref[...] = jnp.zeros_like(dw_ref)

        xb = x_ref[...].astype(BF16)
        dx = jnp.zeros((tm, K), F32)
        for r_ref, sc in zip(r_refs, scales):
            dx = dx + sc * r_ref[...]
        off = 0
        for j in range(nj):
            nc = chunks[j]
            for n0 in range(0, widths[j], nc):
                dhb = dh_refs[j][:, n0:n0 + nc].astype(BF16)
                dx = dx + _dot_nt(dhb, _w_cols(w_ref, off + n0, nc)[...])
                _w_cols(dw_ref, off + n0, nc)[...] += _dot_tn(xb, dhb)
            off += widths[j]
        dx_ref[...] = dx

    zeros = (0,) * w.ndim
    in_specs = [pl.BlockSpec((tm, K), lambda i: (i, 0))]
    in_specs += [pl.BlockSpec((tm, n), lambda i: (i, 0)) for n in widths]
    in_specs += [_const_spec(w.shape)]
    in_specs += [pl.BlockSpec((tm, K), lambda i: (i, 0)) for _ in res]
    in_specs += [_ANY] * ns
    out_specs = [pl.BlockSpec((tm, K), lambda i: (i, 0)), pl.BlockSpec(w.shape, lambda i: zeros)] + [_ANY] * ns
    out_shape = [jax.ShapeDtypeStruct((S, K), F32), jax.ShapeDtypeStruct(w.shape, F32)]
    out_shape += [jax.ShapeDtypeStruct(p.shape, p.dtype) for p in scatter]
    return pl.pallas_call(
        body, name=name, grid=(n_tiles,),
        in_specs=in_specs, out_specs=out_specs, out_shape=out_shape,
        scratch_shapes=_scatter_sems(ns) if ns else [],
        compiler_params=_params(),
    )(xin, *dhs, w, *[r for r, _ in res], *scatter)


def _wgrad(xin, dh, *, name):
    M, K = xin.shape
    N = dh.shape[1]

    def body(x_ref, dh_ref, o_ref):
        o_ref[...] = _dot_tn(x_ref[...].astype(BF16), dh_ref[...].astype(BF16))

    return pl.pallas_call(
        body, name=name, out_shape=jax.ShapeDtypeStruct((K, N), F32),
        compiler_params=pltpu.CompilerParams(vmem_limit_bytes=VMEM_LIMIT),
    )(xin, dh)


AUG_A = FOX_HEAD_DIM
AUG_B = FOX_HEAD_DIM + 3
AUG_C = FOX_HEAD_DIM + 6
AUG_W = FOX_HEADS * LANES


def _placement(val_lane, ones_lanes):
    r = jnp.arange(LANES)[:, None]
    c = jnp.arange(AUG_W)[None, :]
    head, lane = c // LANES, c % LANES
    m = jnp.zeros((LANES, AUG_W), jnp.bool_)
    if val_lane is not None:
        for part in range(3):
            m = m | ((r == part * FOX_HEADS + head) & (lane == val_lane + part))
    for first in ones_lanes:
        m = m | ((r == 3 * FOX_HEADS) & (lane >= first) & (lane < first + 3))
    return m.astype(BF16)


def _placement_lhs(val, tm):
    lane = lax.broadcasted_iota(jnp.int32, (tm, LANES), 1)
    lhs = jnp.where(lane == 3 * FOX_HEADS, 1.0, 0.0)
    if val is not None:
        hi, mid, lo = [p.astype(F32) for p in _split3(val)]
        lhs = jnp.where(lane < FOX_HEADS, hi, jnp.where(
            lane < 2 * FOX_HEADS, pltpu.roll(mid, FOX_HEADS, 1), jnp.where(
                lane < 3 * FOX_HEADS, pltpu.roll(lo, 2 * FOX_HEADS, 1), lhs)))
    return lhs.astype(BF16)


def _store_augmented(o_ref, data, first_pair, lhs, p_ref):
    tm = data.shape[0]
    is_data = lax.broadcasted_iota(jnp.int32, (tm, LANES), 1) < FOX_HEAD_DIM
    for j in range(data.shape[1] // LANES):
        base = 2 * (first_pair + j) * LANES
        extra = _dot(lhs, p_ref[:, base:base + 2 * LANES])
        blk = data[:, j * LANES:(j + 1) * LANES]
        o_ref[:, base:base + LANES] = jnp.where(is_data, blk, extra[:, 0:LANES]).astype(BF16)
        o_ref[:, base + LANES:base + 2 * LANES] = jnp.where(
            is_data, pltpu.roll(blk, FOX_HEAD_DIM, 1), extra[:, LANES:2 * LANES]).astype(BF16)


def _cols_of_rows(rows, S):
    nh = FOX_HEADS // rows.shape[0]
    a = rows[:, :, 0:nh, :].transpose(0, 2, 1, 3).reshape(FOX_HEADS, S).T
    return jnp.pad(a, ((0, 0), (0, LANES - FOX_HEADS)))


def _fox_fwd(qf, ka, va, *, tq, nh=4):
    S = ka.shape[0]
    nq = S // tq
    tk = tq
    ng = FOX_HEADS // nh

    def body(q_ref, k_ref, v_ref, o_ref, qb_ref, *scratch):
        p_scr, m_scr, acc_scr = scratch[0:nh], scratch[nh:2 * nh], scratch[2 * nh:3 * nh]
        qi = pl.program_id(1)
        lane = lax.broadcasted_iota(jnp.int32, (tq, LANES), 1)
        half0 = lane < FOX_HEAD_DIM
        rr = lax.broadcasted_iota(jnp.int32, (tq, tk), 0)
        cc = lax.broadcasted_iota(jnp.int32, (tq, tk), 1)
        sls = [slice(hh * LANES, (hh + 1) * LANES) for hh in range(nh)]
        qs = [q_ref[:, sl] for sl in sls]

        @pl.when(qi == 0)
        def _():
            for hh in range(nh):
                p_scr[hh][...] = jnp.zeros_like(p_scr[hh])
                m_scr[hh][...] = jnp.full(m_scr[hh].shape, -jnp.inf, F32)
                acc_scr[hh][...] = jnp.zeros_like(acc_scr[hh])

        @pl.when(qi > 0)
        def _():
            for hh in range(nh):
                s = _dot_nt(qs[hh], k_ref[0:tk, sls[hh]])
                m0 = jnp.broadcast_to(jnp.max(s, axis=1, keepdims=True), (tq, LANES))
                p_scr[hh][...] = jnp.exp(s - jnp.tile(m0, (1, tk // LANES))).astype(BF16)
                m_scr[hh][...] = m0
                acc_scr[hh][...] = jnp.zeros_like(acc_scr[hh])

        def chunk(ki, masked):
            k0 = pl.multiple_of(ki * tk, tk)
            kp = pl.multiple_of(jnp.maximum(ki - 1, 0) * tk, tk)
            for hh in range(nh):
                m = m_scr[hh][...]
                s = _dot_nt(qs[hh], k_ref[pl.ds(k0, tk), sls[hh]])
                pv = _dot(p_scr[hh][...], v_ref[pl.ds(kp, tk), sls[hh]])
                if masked:
                    s = jnp.where(cc <= rr, s, -jnp.inf)
                m_new = jnp.maximum(m, jnp.max(s, axis=1, keepdims=True))
                p_scr[hh][...] = jnp.exp(s - jnp.tile(m_new, (1, tk // LANES))).astype(BF16)
                acc_scr[hh][...] = (acc_scr[hh][...] + pv) * jnp.exp(m - m_new)
                m_scr[hh][...] = m_new

        def trip(ki, c):
            chunk(ki, False)
            return c

        lax.fori_loop(1, qi, trip, 0)
        chunk(qi, True)
        kq = pl.multiple_of(qi * tk, tk)
        outs = []
        for hh in range(nh):
            m = m_scr[hh][...]
            acc = acc_scr[hh][...] + _dot(p_scr[hh][...], v_ref[pl.ds(kq, tk), sls[hh]])
            l = jnp.sum(jnp.where(lane == AUG_A, acc, 0.0), axis=1, keepdims=True)
            outs.append(acc / l)
            hi, mid, lo = _split3(-(m + jnp.log(l)))
            qb_ref[:, sls[hh]] = jnp.where(lane == AUG_C, hi, jnp.where(
                lane == AUG_C + 1, mid, jnp.where(lane == AUG_C + 2, lo, qs[hh])))
        for pr in range(nh // 2):
            o_ref[:, pr * LANES:(pr + 1) * LANES] = jnp.where(
                half0, outs[2 * pr], pltpu.roll(outs[2 * pr + 1], FOX_HEAD_DIM, 1))

    return pl.pallas_call(
        body, name="fox_fwd", grid=(ng, nq),
        in_specs=[pl.BlockSpec((tq, nh * LANES), lambda g, qi: (qi, g)),
                  pl.BlockSpec((S, nh * LANES), lambda g, qi: (0, g), pipeline_mode=pl.Buffered(1)),
                  pl.BlockSpec((S, nh * LANES), lambda g, qi: (0, g), pipeline_mode=pl.Buffered(1))],
        out_specs=[pl.BlockSpec((tq, nh * FOX_HEAD_DIM), lambda g, qi: (qi, g)),
                   pl.BlockSpec((tq, nh * LANES), lambda g, qi: (qi, g))],
        out_shape=[jax.ShapeDtypeStruct((S, D_MAIN), F32),
                   jax.ShapeDtypeStruct((S, AUG_W), BF16)],
        scratch_shapes=([pltpu.VMEM((tq, tk), BF16)] * nh + [pltpu.VMEM((tq, LANES), F32)] * nh
                        + [pltpu.VMEM((tq, LANES), F32)] * nh),
        compiler_params=_params(2),
    )(qf, ka, va)


def _fox_bwd(qb, ka, va, do_aug, *, tq):
    S = ka.shape[0]
    nq = S // tq
    tk = tq

    def body(k_ref, v_ref, q_ref, do_ref, dq_ref, dk_ref, dv_ref, dck_ref, dk_scr, dv_scr):
        kj = pl.program_id(1)

        @pl.when(kj == 0)
        def _():
            dq_ref[...] = jnp.zeros_like(dq_ref)

        lane = lax.broadcasted_iota(jnp.int32, (tk, LANES), 1)
        half0 = lane < FOX_HEAD_DIM
        rr = lax.broadcasted_iota(jnp.int32, (tk, tq), 0)
        cc = lax.broadcasted_iota(jnp.int32, (tk, tq), 1)
        sls = [slice(hh * LANES, (hh + 1) * LANES) for hh in range(2)]
        kts = [k_ref[:, sl] for sl in sls]
        vts = [v_ref[:, sl] for sl in sls]

        dk_scr[...] = jnp.zeros_like(dk_scr)
        dv_scr[...] = jnp.zeros_like(dv_scr)

        def chunk(qi, masked):
            q0 = pl.multiple_of(qi * tq, tq)
            for hh in range(2):
                qc = q_ref[pl.ds(q0, tq), sls[hh]]
                doc = do_ref[pl.ds(q0, tq), sls[hh]]
                pt = jnp.exp(_dot_nt(kts[hh], qc))
                if masked:
                    pt = jnp.where(rr <= cc, pt, 0.0)
                dsb = (pt * _dot_nt(vts[hh], doc)).astype(BF16)
                dv_scr[hh] += _dot(pt.astype(BF16), doc)
                dk_scr[hh] += _dot(dsb, qc)
                dq_ref[pl.ds(q0, tq), sls[hh]] += _dot_tn(dsb, kts[hh])

        def trip(qi, c):
            chunk(qi, False)
            return c

        chunk(kj, True)
        lax.fori_loop(kj + 1, nq, trip, 0)
        dk0, dk1 = dk_scr[0], dk_scr[1]
        dv0, dv1 = dv_scr[0], dv_scr[1]
        dk_ref[...] = jnp.where(half0, dk0, pltpu.roll(dk1, FOX_HEAD_DIM, 1)).astype(BF16)
        dv_ref[...] = jnp.where(half0, dv0, pltpu.roll(dv1, FOX_HEAD_DIM, 1)).astype(BF16)
        c0 = jnp.sum(jnp.where(lane == AUG_B, dk0, 0.0), axis=1, keepdims=True)
        c1 = jnp.sum(jnp.where(lane == AUG_B, dk1, 0.0), axis=1, keepdims=True)
        dck_cols = jnp.where(lane == 0, c0, 0.0) + jnp.where(lane == 1, c1, 0.0)
        dck_ref[0, 0] = dck_cols.T[0:8, :]

    return pl.pallas_call(
        body, name="fox_bwd", grid=(8, nq),
        in_specs=[pl.BlockSpec((tk, 2 * LANES), lambda hp, kj: (kj, hp)),
                  pl.BlockSpec((tk, 2 * LANES), lambda hp, kj: (kj, hp)),
                  pl.BlockSpec((S, 2 * LANES), lambda hp, kj: (0, hp)),
                  pl.BlockSpec((S, 2 * LANES), lambda hp, kj: (0, hp))],
        out_specs=[pl.BlockSpec((S, 2 * LANES), lambda hp, kj: (0, hp)),
                   pl.BlockSpec((tk, LANES), lambda hp, kj: (kj, hp)),
                   pl.BlockSpec((tk, LANES), lambda hp, kj: (kj, hp)),
                   pl.BlockSpec((1, 1, 8, tk), lambda hp, kj: (hp, kj, 0, 0))],
        out_shape=[jax.ShapeDtypeStruct((S, AUG_W), F32),
                   jax.ShapeDtypeStruct((S, D_MAIN), BF16),
                   jax.ShapeDtypeStruct((S, D_MAIN), BF16),
                   jax.ShapeDtypeStruct((8, nq, 8, tk), F32)],
        scratch_shapes=[pltpu.VMEM((2, tk, LANES), F32), pltpu.VMEM((2, tk, LANES), F32)],
        compiler_params=_params(2),
    )(ka, va, qb, do_aug)


def _adamw(w, g, m, v, *, name):
    Rr, C = w.shape
    tr = 256 if Rr % 256 == 0 else Rr
    c1 = 1.0 / (1.0 - ADAM_B1 ** ADAM_STEP)
    c2 = 1.0 / (1.0 - ADAM_B2 ** ADAM_STEP)

    def body(w_ref, g_ref, m_ref, v_ref, d_ref, nm_ref, nv_ref):
        gv = g_ref[...]
        nm = ADAM_B1 * m_ref[...] + (1.0 - ADAM_B1) * gv
        nv = ADAM_B2 * v_ref[...] + (1.0 - ADAM_B2) * (gv * gv)
        d_ref[...] = -ADAM_LR * ((nm * c1) / (jnp.sqrt(nv * c2) + ADAM_EPS) + ADAM_WD * w_ref[...])
        nm_ref[...] = nm
        nv_ref[...] = nv

    spec = pl.BlockSpec((tr, C), lambda i: (i, 0))
    sds = jax.ShapeDtypeStruct((Rr, C), F32)
    return pl.pallas_call(
        body, name=name, grid=(Rr // tr,),
        in_specs=[spec] * 4, out_specs=[spec] * 3, out_shape=[sds] * 3,
        compiler_params=_params(),
    )(w, g, m, v)


_ANY = pl.BlockSpec(memory_space=pl.ANY)
_MESH = pl.DeviceIdType.MESH


def _place():
    x, y, c = lax.axis_index("x"), lax.axis_index("y"), lax.axis_index("c")
    return x, y, c


def _gather_steps(p_refs, out_refs, send_sems, recv_sems):
    x, y, c = _place()
    sib = (x, y, 1 - c)
    chips = [(1 - x, y), (x, 1 - y), (1 - x, 1 - y)]
    idx = [2 * chip[0] + chip[1] for chip in chips]
    me = 2 * x + y
    na = len(p_refs)

    def copy(a, k, chip_idx, half, to, src=None):
        dst = out_refs[a].at[chip_idx, half]
        return pltpu.make_async_remote_copy(
            src_ref=dst if src is None else src, dst_ref=dst,
            send_sem=send_sems.at[6 * a + k], recv_sem=recv_sems.at[6 * a + k],
            device_id=to, device_id_type=_MESH)

    first = [copy(a, j, me, c, (*chips[j], c), src=p_refs[a].at[c]) for a in range(na) for j in range(3)]
    passed = [copy(a, 3 + j, idx[j], c, sib) for a in range(na) for j in range(3)]

    def start():
        for cp in first:
            cp.start()

    def forward():
        for a in range(na):
            for j in range(3):
                copy(a, j, idx[j], c, sib).wait_recv()
                passed[3 * a + j].start()

    def finish():
        for a in range(na):
            for j in range(3):
                copy(a, 3 + j, idx[j], 1 - c, sib).wait_recv()
        for cp in first + passed:
            cp.wait_send()

    return start, forward, finish


def _sems(n):
    return [pltpu.SemaphoreType.DMA((n,)), pltpu.SemaphoreType.DMA((n,))]


def _gather_sems(na):
    return _sems(6 * na)


def _scatter_sems(na):
    return _sems(3 * na)


def _pair_sems(na):
    return _sems(N_CHIPS * na)


def _gathered_shape(pack):
    return jax.ShapeDtypeStruct((N_CHIPS,) + pack.shape, pack.dtype)


def _from_sibling_shape(gpack):
    return jax.ShapeDtypeStruct((N_CHIPS,) + gpack.shape[2:], gpack.dtype)


def _all_gather_shards(packs):
    na = len(packs)

    def body(*refs):
        for step in _gather_steps(refs[0:na], refs[na:2 * na], refs[2 * na], refs[2 * na + 1]):
            step()

    return pl.pallas_call(
        body, name="all_gather_shards",
        in_specs=[_ANY] * na, out_specs=[_ANY] * na, out_shape=[_gathered_shape(p) for p in packs],
        scratch_shapes=_gather_sems(na),
    )(*packs)


def _pair_send_steps(g_refs, out_refs, send_sem, recv_sem):
    x, y, c = _place()
    cps = [pltpu.make_async_remote_copy(
        src_ref=g_refs[a].at[j, 1 - c], dst_ref=out_refs[a].at[j],
        send_sem=send_sem.at[N_CHIPS * a + j], recv_sem=recv_sem.at[N_CHIPS * a + j],
        device_id=(x, y, 1 - c), device_id_type=_MESH) for a in range(len(g_refs)) for j in range(N_CHIPS)]

    def start():
        for cp in cps:
            cp.start()

    def finish():
        for cp in cps:
            cp.wait_recv()
        for cp in cps:
            cp.wait_send()

    return start, finish


def _send_half_to_sibling(gpacks, tag):
    na = len(gpacks)

    def body(*refs):
        for step in _pair_send_steps(refs[0:na], refs[na:2 * na], refs[2 * na], refs[2 * na + 1]):
            step()

    return pl.pallas_call(
        body, name=f"pair_send{tag}",
        in_specs=[_ANY] * na, out_specs=[_ANY] * na, out_shape=[_from_sibling_shape(g) for g in gpacks],
        scratch_shapes=_pair_sems(na),
    )(*gpacks)


def _pair_sum(gpack, recv, c_arr, tag, *, tr=PACK_TILE):
    rows, lanes = recv.shape[1:]
    assert rows % tr == 0

    def body(c_ref, a_ref, b_ref, o_ref):
        o_ref[...] = (a_ref[...] + b_ref[...]).astype(BF16)

    grid_spec = pltpu.PrefetchScalarGridSpec(
        num_scalar_prefetch=1, grid=(N_CHIPS, rows // tr),
        in_specs=[pl.BlockSpec((None, None, tr, lanes), lambda j, i, c_ref: (j, c_ref[0], i, 0)),
                  pl.BlockSpec((None, tr, lanes), lambda j, i, c_ref: (j, i, 0))],
        out_specs=pl.BlockSpec((None, tr, lanes), lambda j, i, c_ref: (j, i, 0)))
    return pl.pallas_call(
        body, name=f"pair_sum{tag}", grid_spec=grid_spec,
        out_shape=jax.ShapeDtypeStruct((N_CHIPS, rows, lanes), BF16),
        compiler_params=_params(2),
    )(c_arr, gpack, recv)


def _scatter_steps(p_refs, out_refs, send_sems, recv_sems):
    x, y, c = _place()
    chips = [(1 - x, y), (x, 1 - y), (1 - x, 1 - y)]
    me = 2 * x + y
    cps = [pltpu.make_async_remote_copy(
        src_ref=p_refs[a].at[2 * chip[0] + chip[1]], dst_ref=out_refs[a].at[me],
        send_sem=send_sems.at[3 * a + j], recv_sem=recv_sems.at[3 * a + j],
        device_id=(*chip, c), device_id_type=_MESH) for a in range(len(p_refs)) for j, chip in enumerate(chips)]

    def start():
        for cp in cps:
            cp.start()

    def finish():
        for cp in cps:
            cp.wait_recv()
        for cp in cps:
            cp.wait_send()

    return start, finish


def _share_steps(row_ref, out_ref, send_sems, recv_sems):
    x, y, c = _place()
    mine = 4 * x + 2 * y + c
    cps = []
    for k in range(1, 8):
        fx, fy, fc = (k >> 2) & 1, (k >> 1) & 1, k & 1
        peer = (x + fx - 2 * x * fx, y + fy - 2 * y * fy, c + fc - 2 * c * fc)
        cps.append(pltpu.make_async_remote_copy(
            src_ref=row_ref, dst_ref=out_ref.at[mine], send_sem=send_sems.at[k - 1], recv_sem=recv_sems.at[k - 1],
            device_id=peer, device_id_type=_MESH))

    def start():
        for cp in cps:
            cp.start()

    def finish():
        for cp in cps:
            cp.wait_recv()
        for cp in cps:
            cp.wait_send()

    return start, finish


def _scatter_pieces(psums, tag, share=None):
    na = len(psums)
    ns = 0 if share is None else 1

    def body(*refs):
        n_in = na + ns
        steps = [_scatter_steps(refs[0:na], refs[n_in:n_in + na], refs[2 * n_in], refs[2 * n_in + 1])]
        if ns:
            steps.append(_share_steps(refs[na], refs[n_in + na], refs[2 * n_in + 2], refs[2 * n_in + 3]))
        for phase in range(2):
            for st in steps:
                st[phase]()

    out_shape = [jax.ShapeDtypeStruct(p.shape, p.dtype) for p in psums]
    if ns:
        out_shape.append(jax.ShapeDtypeStruct((8,) + share.shape, share.dtype))
    return pl.pallas_call(
        body, name=f"scatter_pieces{tag}",
        in_specs=[_ANY] * (na + ns), out_specs=[_ANY] * (na + ns), out_shape=out_shape,
        scratch_shapes=_scatter_sems(na) + (_sems(7) if ns else []),
    )(*psums, *([share] if ns else []))


def _sum_pieces(pieces, tag, *, tr=PACK_TILE):
    rows, lanes = pieces.shape[1:]
    assert rows % tr == 0

    def body(p_ref, o_ref):
        acc = p_ref[0].astype(F32) + p_ref[1].astype(F32)
        acc = acc + p_ref[2].astype(F32)
        o_ref[...] = acc + p_ref[3].astype(F32)

    return pl.pallas_call(
        body, name=f"sum_pieces{tag}", grid=(rows // tr,),
        in_specs=[pl.BlockSpec((N_CHIPS, tr, lanes), lambda i: (0, i, 0))],
        out_specs=pl.BlockSpec((tr, lanes), lambda i: (i, 0)),
        out_shape=jax.ShapeDtypeStruct((rows, lanes), F32),
        compiler_params=_params(),
    )(pieces)


def _exchange_halves(totals):
    n = len(totals)

    def body(*refs):
        t_refs, out_refs, send_sem, recv_sem = refs[:n], refs[n:2 * n], refs[2 * n], refs[2 * n + 1]
        x, y, c = _place()
        cps = [pltpu.make_async_remote_copy(
            src_ref=t_refs[i], dst_ref=out_refs[i].at[c], send_sem=send_sem.at[i], recv_sem=recv_sem.at[i],
            device_id=(x, y, 1 - c), device_id_type=_MESH) for i in range(n)]
        for cp in cps:
            cp.start()
        for cp in cps:
            cp.wait_recv()
        for cp in cps:
            cp.wait_send()

    return pl.pallas_call(
        body, name="exchange_halves",
        in_specs=[_ANY] * n, out_specs=[_ANY] * n,
        out_shape=[jax.ShapeDtypeStruct((2,) + t.shape, F32) for t in totals],
        scratch_shapes=[pltpu.SemaphoreType.DMA((n,)), pltpu.SemaphoreType.DMA((n,))],
    )(*totals)


def _pad_rows(a, rows):
    return jnp.pad(a, ((0, rows - a.shape[0]), (0, 0)))


def _pack_weight_shards(w_in, w_mem_kv, w_out, pool_w, w_kv_shared, pool_scale):
    ps_bits = lax.bitcast_convert_type(pool_scale.reshape(-1), BF16).reshape(1, -1)
    ps_row = jnp.pad(ps_bits, ((0, 0), (0, 1024 - ps_bits.shape[1])))

    def common(l):
        return [w_mem_kv[l].astype(BF16).reshape(ROWS_W_MKV, 1024),
                w_out[l].astype(BF16).reshape(ROWS_W_OUT, 1024)]

    p0 = common(0) + [pool_w.astype(BF16).reshape(ROWS_POOL_W, 1024), _pad_rows(ps_row, ROWS_SMALL),
                      jnp.zeros((PACK0_ROWS - OFF_LN_G, 1024), BF16)]
    p1 = common(1) + [_pad_rows(w_kv_shared.astype(BF16).reshape(KV_SHARD, 1024), ROWS_W_KV),
                      jnp.zeros((PACK1_ROWS - OFF_BF, 1024), BF16)]
    w_in_halves = w_in.astype(BF16).reshape(2, 2, D_MODEL // 2, W_IN_SHARD)
    return ([jnp.concatenate(p0, axis=0).reshape(2, PACK0_ROWS // 2, 1024), w_in_halves[0]],
            [jnp.concatenate(p1, axis=0).reshape(2, PACK1_ROWS // 2, 1024), w_in_halves[1]])


def _unpack_w_in(g_in):
    return g_in.reshape(N_CHIPS, D_MODEL, W_IN_SHARD)


def _unpack_common(g):
    w_mkv = g[:, OFF_W_MKV:OFF_W_MKV + ROWS_W_MKV].reshape(D_MODEL, 2 * D_MEM)
    w_out = g[:, OFF_W_OUT:OFF_W_OUT + ROWS_W_OUT].reshape(D_MIX, D_MODEL)
    return w_mkv, w_out


def _unpack_weights0(g):
    pool_w = g[:, OFF_POOL_W:OFF_POOL_W + ROWS_POOL_W].reshape(4, 4, POOL_GROUP // 4, POOL_GROUP)
    pool_w = pool_w.transpose(1, 0, 2, 3).reshape(4, POOL_GROUP, POOL_GROUP)
    ps_bits = g[:, OFF_POOL_S, 0:512].reshape(4, 256, 2)
    pool_scale = lax.bitcast_convert_type(ps_bits, F32).reshape(1, D_MAIN)
    return _unpack_common(g) + (pool_w, pool_scale)


def _unpack_weights1(g):
    w_kv = g[:, OFF_W_KV:OFF_W_KV + KV_SHARD].reshape(4, D_MODEL, KV_SHARD)
    w_kv = w_kv.transpose(1, 0, 2).reshape(D_MODEL, KV_COLS)
    return _unpack_common(g) + (w_kv,)


def _replicated_rows(a):
    a = _pad_rows(a, ROWS_SMALL)
    return jnp.broadcast_to(a[None], (4,) + a.shape)


def _pack_common_grads(g_w_mkv, g_w_out):
    return [g_w_mkv.reshape(4, ROWS_W_MKV, 1024), g_w_out.reshape(4, ROWS_W_OUT, 1024)]


def _w_in_grad_halves(g_w_in):
    return g_w_in.reshape(N_CHIPS, 2, D_MODEL // 2, W_IN_SHARD)


def _pack_grads0(g_w_mkv, g_w_out, g_pool_w, g_pool_scale, g_ln_g, g_ln_b):
    parts = _pack_common_grads(g_w_mkv, g_w_out) + [
        g_pool_w.reshape(4, 4, POOL_GROUP // 4, POOL_GROUP).transpose(1, 0, 2, 3).reshape(4, ROWS_POOL_W, 1024),
        jnp.pad(g_pool_scale.reshape(4, 1, 256), ((0, 0), (0, ROWS_SMALL - 1), (0, 1024 - 256))),
        _replicated_rows(g_ln_g), _replicated_rows(g_ln_b),
        jnp.zeros((4, PACK0_ROWS - OFF_LN_B - ROWS_SMALL, 1024), F32),
    ]
    return jnp.concatenate(parts, axis=1).reshape(4, 2, PACK0_ROWS // 2, 1024)


def _pack_grads1(g_w_mkv, g_w_out, g_w_kv, g_bf):
    parts = _pack_common_grads(g_w_mkv, g_w_out) + [
        jnp.pad(g_w_kv.reshape(D_MODEL, 4, KV_SHARD).transpose(1, 0, 2).reshape(4, KV_SHARD, 1024),
                ((0, 0), (0, ROWS_W_KV - KV_SHARD), (0, 0))),
        _replicated_rows(jnp.pad(g_bf.reshape(1, -1), ((0, 0), (0, 1024 - g_bf.shape[0])))),
        jnp.zeros((4, PACK1_ROWS - OFF_BF - ROWS_SMALL, 1024), F32),
    ]
    return jnp.concatenate(parts, axis=1).reshape(4, 2, PACK1_ROWS // 2, 1024)


def _local_step(x, mem, target, w0, w1, ln_g, ln_b, b_forget, *, tm=256, tq=512, dist=None):
    S = x.shape[0]
    g_rows = [ln_g[l:l + 1] for l in range(2)]
    b_rows = [ln_b[l:l + 1] for l in range(2)]
    bf_row = jnp.pad(b_forget.reshape(1, -1), ((0, 0), (0, LANES - FOX_HEADS)))

    def own_slot(gathered, pack):
        return lax.dynamic_update_slice(gathered, pack[None], (dist["me"], 0, 0, 0))

    if dist is None:
        w_in0, w_mkv0, w_out0, pool_w, pool_scale = w0
        h0 = _linear_fwd(x, w_in0, tm=2 * tm, name="in_proj0")
    else:
        w_in0, pack0 = w0
        h0, g0 = _linear_fwd(x, w_in0, tm=2 * tm, name="in_proj0", gather=[pack0])
        w_mkv0, w_out0, pool_w, pool_scale = _unpack_weights0(
            own_slot(g0, pack0).reshape(N_CHIPS, PACK0_ROWS, 1024))
    mkv0 = _linear_fwd(mem, w_mkv0, tm=N_MEM, name="mem_kv0")
    f0 = _mix("pool", "fwd", h=h0, xres=x, mkv=mkv0, w_out=w_out0, ln_g=g_rows[0], ln_b=b_rows[0],
              pool_w=pool_w, pool_scale=pool_scale, gather=() if dist is None else w1, tm=2 * tm)
    z0, x1 = f0["z"], f0["xout"]
    if dist is not None:
        g1, g1_in = [own_slot(f0[f"gathered{a}"], w1[a]) for a in range(2)]
        w1 = (_unpack_w_in(g1_in),) + _unpack_weights1(g1.reshape(N_CHIPS, PACK1_ROWS, 1024))
    w_in1, w_mkv1, w_out1, w_kv = w1
    w_in, w_out = [w_in0, w_in1], [w_out0, w_out1]
    w_kvp = jnp.pad(w_kv, ((0, 0), (0, LANES - FOX_HEADS)))
    mkv = [mkv0, _linear_fwd(mem, w_mkv1, tm=N_MEM, name="mem_kv1")]
    ka, va, fl, cum = _kv_proj(x1, w_kvp, bf_row, tm=2 * tm)
    h1, qf = _linear_fwd(x1, w_in[1], tm=2 * tm, name="in_proj1", q_cum=cum)
    ymain1, qb = _fox_fwd(qf, ka, va, tq=tq)

    b1 = _mix("fox", "bwd", h=h1, xres=x1, mkv=mkv[1], w_out=w_out[1], ln_g=g_rows[1], ln_b=b_rows[1],
              ymain=ymain1, target=target, tm=tm)
    dq_aug, dk, dv, dck_rows = _fox_bwd(qb, ka, va, b1["dmain"], tq=tq)
    du1, df, dbf = _gate_bwd(dq_aug, _cols_of_rows(dck_rows, S), fl, tm=2 * tm)

    dx1a, dw_in1 = _lin_bwd(x1, [du1, b1["drest"]], w_in[1], [(b1["dz"], ALPHA)], tm=tm, name="in_proj1_bwd")
    dx1, dw_kvp = _lin_bwd(x1, [dk, dv, df], w_kvp, [(dx1a, 1.0)], tm=2 * tm, name="kv_proj_bwd")

    dw_mkv1 = _wgrad(mem, b1["dmkv"], name="mem_kv1_bwd")
    g_w_kv, g_bf = dw_kvp[:, 0:KV_COLS], dbf[0, 0:FOX_HEADS]

    gpacks1, psums1 = (), ()
    if dist is not None:
        gpacks1 = [_pack_grads1(dw_mkv1, b1["dw_out"], g_w_kv, g_bf), _w_in_grad_halves(dw_in1)]
    b0 = _mix("pool", "bwd", h=h0, mkv=mkv[0], w_out=w_out[0], ln_g=g_rows[0],
              pool_w=pool_w, pool_scale=pool_scale, z=z0, dy=dx1, pair_send=gpacks1, tm=tm)
    if dist is not None:
        psums1 = [_pair_sum(g, b0[f"from_sibling{a}"], dist["c_arr"], f"1{'ab'[a]}") for a, g in enumerate(gpacks1)]
    outs = _lin_bwd(x, [b0["dmain"], b0["drest"]], w_in[0], [(b0["dz"], ALPHA)], tm=tm, name="in_proj0_bwd",
                    scatter=psums1)
    dx, dw_in0 = outs[0], outs[1]
    dw_mkv0 = _wgrad(mem, b0["dmkv"], name="mem_kv0_bwd")
    g_ln_g = jnp.concatenate([b0["dln_g"], b1["dln_g"]], axis=0)
    g_ln_b = jnp.concatenate([b0["dln_b"], b1["dln_b"]], axis=0)

    if dist is None:
        grads = dict(w_in=[dw_in0, dw_in1], w_mem_kv=[dw_mkv0, dw_mkv1], w_out=[b0["dw_out"], b1["dw_out"]],
                     ln_g=g_ln_g, ln_b=g_ln_b, pool_w=b0["dpool_w"], pool_scale=b0["dpool_scale"],
                     w_kv=g_w_kv, b_forget=g_bf)
        return b1["loss"], dx, grads

    me, my_c = dist["me"], dist["my_c"]

    def with_own(pieces, psum):
        own = lax.dynamic_slice(psum, (me, 0, 0), (1,) + psum.shape[1:])
        return lax.dynamic_update_slice(pieces, own, (me, 0, 0))

    totals1 = [_sum_pieces(with_own(outs[2 + a], p), f"1{'ab'[a]}") for a, p in enumerate(psums1)]
    gpacks0 = [_pack_grads0(dw_mkv0, b0["dw_out"], b0["dpool_w"], b0["dpool_scale"], g_ln_g, g_ln_b),
               _w_in_grad_halves(dw_in0)]
    sib0 = _send_half_to_sibling(gpacks0, 0)
    psums0 = [_pair_sum(g, sib0[a], dist["c_arr"], f"0{'ab'[a]}") for a, g in enumerate(gpacks0)]
    loss_row = jnp.broadcast_to(0.5 / D_MODEL * jnp.sum(b1["loss"]), (8, LANES))
    pieces0 = _scatter_pieces(psums0, 0, share=loss_row)
    losses = lax.dynamic_update_slice(pieces0[2], loss_row[None], (2 * me + my_c, 0, 0))
    loss = jnp.sum(losses[:, 0, 0])
    totals0 = [_sum_pieces(with_own(pieces0[a], p), f"0{'ab'[a]}") for a, p in enumerate(psums0)]
    totals = totals0 + totals1
    halves = _exchange_halves(totals)
    full = [lax.dynamic_update_slice(h, t[None], (my_c, 0, 0)) for h, t in zip(halves, totals)]
    shard0, shard1 = full[0].reshape(PACK0_ROWS, 1024), full[2].reshape(PACK1_ROWS, 1024)
    g_w_in = jnp.stack([full[1].reshape(D_MODEL, W_IN_SHARD), full[3].reshape(D_MODEL, W_IN_SHARD)])
    return loss, dx, shard0, shard1, g_w_in


def kernel(x, mem, w_in, w_mem_kv, w_out, ln_g, ln_b, pool_w, pool_scale, w_kv_shared, b_forget, loss_target, m_w_in, m_w_mem_kv, m_w_out, m_ln_g, m_ln_b, m_pool_w, m_pool_scale, m_w_kv_shared, m_b_forget, v_w_in, v_w_mem_kv, v_w_out, v_ln_g, v_ln_b, v_pool_w, v_pool_scale, v_w_kv_shared, v_b_forget):
    dist = dict(c_arr=lax.axis_index("c").astype(jnp.int32).reshape(1),
                me=2 * lax.axis_index("x") + lax.axis_index("y"), my_c=lax.axis_index("c"))

    wpacks0, wpacks1 = _pack_weight_shards(w_in, w_mem_kv, w_out, pool_w, w_kv_shared, pool_scale)
    g0_in = lax.dynamic_update_slice(_all_gather_shards([wpacks0[1]])[0], wpacks0[1][None], (dist["me"], 0, 0, 0))
    w0 = (_unpack_w_in(g0_in), wpacks0[0])

    loss, dx, shard0, shard1, g_w_in = _local_step(x[0], mem[0], loss_target[0], w0, wpacks1, ln_g, ln_b,
                                                   b_forget, dist=dist)

    def per_layer(off, rows, shape):
        return jnp.concatenate([shard0[off:off + rows], shard1[off:off + rows]], axis=0).reshape(shape)

    g_w_mkv = per_layer(OFF_W_MKV, ROWS_W_MKV, w_mem_kv.shape)
    g_w_out = per_layer(OFF_W_OUT, ROWS_W_OUT, w_out.shape)
    g_pool_w = shard0[OFF_POOL_W:OFF_POOL_W + ROWS_POOL_W].reshape(pool_w.shape)
    g_w_kv = shard1[OFF_W_KV:OFF_W_KV + KV_SHARD].reshape(w_kv_shared.shape)
    g_pool_scale = shard0[OFF_POOL_S:OFF_POOL_S + 1, 0:256].reshape(pool_scale.shape)
    g_ln_g = shard0[OFF_LN_G:OFF_LN_G + 2]
    g_ln_b = shard0[OFF_LN_B:OFF_LN_B + 2]
    g_bf = shard1[OFF_BF, 0:FOX_HEADS]

    names = ["w_in", "w_mem_kv", "w_out", "ln_g", "ln_b", "pool_w", "pool_scale", "w_kv_shared", "b_forget"]
    ws = [w_in, w_mem_kv, w_out, ln_g, ln_b, pool_w, pool_scale, w_kv_shared, b_forget]
    gs = [g_w_in, g_w_mkv, g_w_out, g_ln_g, g_ln_b, g_pool_w, g_pool_scale, g_w_kv, g_bf]
    ms = [m_w_in, m_w_mem_kv, m_w_out, m_ln_g, m_ln_b, m_pool_w, m_pool_scale, m_w_kv_shared, m_b_forget]
    vs = [v_w_in, v_w_mem_kv, v_w_out, v_ln_g, v_ln_b, v_pool_w, v_pool_scale, v_w_kv_shared, v_b_forget]
    deltas, new_ms, new_vs = [], [], []
    for nm, w, gg, mm, vv in zip(names, ws, gs, ms, vs):
        two_d = (-1, w.shape[-1])
        d, nmm, nvv = _adamw(w.reshape(two_d), gg.reshape(two_d), mm.reshape(two_d), vv.reshape(two_d),
                             name=f"adamw_{nm}")
        deltas.append(d.reshape(w.shape))
        new_ms.append(nmm.reshape(w.shape))
        new_vs.append(nvv.reshape(w.shape))

    return (loss, dx[None], *gs, *deltas, *new_ms, *new_vs)
```

```python
import jax
import jax.numpy as jnp
from jax import lax
from jax.experimental import pallas as pl
from jax.experimental.pallas import tpu as pltpu

F32 = jnp.float32
BF16 = jnp.bfloat16

D_MODEL = 1024
D_MAIN = 1024
D_MEM = 512
D_MIX = D_MAIN + D_MEM
D_IN = 2 * D_MIX
N_MEM = 256
MEM_HEADS = 4
MEM_HEAD_DIM = 128
FOX_HEADS = 16
FOX_HEAD_DIM = 64
FOX_SCALE = 0.125
POOL_WINDOWS = (2, 4, 8, 16)
POOL_GROUP = 256
POOL_HALO = 16
ALPHA = 4.0 ** 0.25
LN_EPS = 1e-5
LANES = 128
N_CHIPS = 4

ADAM_LR = 0.001
ADAM_B1 = 0.9
ADAM_B2 = 0.999
ADAM_EPS = 1e-08
ADAM_WD = 0.01
ADAM_STEP = 10

V7X_VMEM_BYTES = 64 * 1024 * 1024
VMEM_LIMIT = V7X_VMEM_BYTES - 8 * 1024 * 1024

ROWS_W_MKV = (D_MODEL // N_CHIPS) * 2 * D_MEM // 1024
ROWS_W_OUT = (D_MIX // N_CHIPS) * D_MODEL // 1024
ROWS_POOL_W = 4 * (POOL_GROUP // N_CHIPS) * POOL_GROUP // 1024
KV_COLS = 2 * D_MAIN + FOX_HEADS
KV_SHARD = KV_COLS // N_CHIPS
ROWS_W_KV = 528
ROWS_SMALL = 16
OFF_W_MKV = 0
OFF_W_OUT = OFF_W_MKV + ROWS_W_MKV
OFF_TAIL = OFF_W_OUT + ROWS_W_OUT
OFF_POOL_W = OFF_TAIL
OFF_POOL_S = OFF_POOL_W + ROWS_POOL_W
OFF_LN_G = OFF_POOL_S + ROWS_SMALL
OFF_LN_B = OFF_LN_G + ROWS_SMALL
PACK0_ROWS = 768
OFF_W_KV = OFF_TAIL
OFF_BF = OFF_W_KV + ROWS_W_KV
PACK1_ROWS = 1280
PACK_TILE = 128
W_IN_SHARD = D_IN // N_CHIPS


def _dot(a, b):
    return jnp.dot(a, b, preferred_element_type=F32)


def _dot_nt(a, b):
    return lax.dot_general(a, b, (((1,), (1,)), ((), ())), preferred_element_type=F32)


def _dot_tn(a, b):
    return lax.dot_general(a, b, (((0,), (0,)), ((), ())), preferred_element_type=F32)


def _params(n_axes=1):
    return pltpu.CompilerParams(dimension_semantics=("arbitrary",) * n_axes,
                                vmem_limit_bytes=VMEM_LIMIT)


def _const_spec(shape):
    zeros = (0,) * len(shape)
    return pl.BlockSpec(shape, lambda *_: zeros, pipeline_mode=pl.Buffered(1))


def _split3(x):
    hi = x.astype(BF16)
    r = x - hi.astype(F32)
    mid = r.astype(BF16)
    lo = (r - mid.astype(F32)).astype(BF16)
    return hi, mid, lo


def _cols_and_chunk(w):
    if w.ndim == 3:
        return N_CHIPS * w.shape[2], 256
    return w.shape[1], (512 if w.shape[1] % 512 == 0 else LANES)


def _w_cols(w_ref, n0, nc):
    if len(w_ref.shape) == 3:
        per = w_ref.shape[2]
        assert n0 // per == (n0 + nc - 1) // per
        return w_ref.at[n0 // per, :, n0 % per:n0 % per + nc]
    return w_ref.at[:, n0:n0 + nc]


def _linear_fwd(x, w, *, tm, name, q_cum=None, gather=()):
    S, K = x.shape
    N, nc = _cols_and_chunk(w)
    aug = q_cum is not None
    ng = len(gather)
    n_tiles = S // tm
    assert not (aug and ng)

    def body(*refs):
        x_ref, w_ref = refs[0], refs[1]
        o_ref = refs[4] if aug else refs[2 + ng]
        if ng:
            i = pl.program_id(0)
            start, forward, finish = _gather_steps(refs[2:2 + ng], refs[3 + ng:3 + 2 * ng],
                                                   refs[3 + 2 * ng], refs[4 + 2 * ng])
            pl.when(i == 0)(start)
            pl.when(i == max(n_tiles - 3, 0))(forward)
            pl.when(i == n_tiles - 1)(finish)
        xb = x_ref[...].astype(BF16)
        if aug:
            lhs = _placement_lhs(refs[2][...], tm)
        for n0 in range(0, N, nc):
            r = _dot(xb, _w_cols(w_ref, n0, nc)[...])
            o_ref[:, n0:n0 + nc] = r.astype(BF16)
            if aug and n0 < D_MAIN:
                _store_augmented(refs[5], r * FOX_SCALE, n0 // LANES, lhs, refs[3])

    in_specs = [pl.BlockSpec((tm, K), lambda i: (i, 0)), _const_spec(w.shape)]
    out_specs = [pl.BlockSpec((tm, N), lambda i: (i, 0))]
    out_shape = [jax.ShapeDtypeStruct((S, N), BF16)]
    extra = []
    if aug:
        in_specs += [pl.BlockSpec((tm, LANES), lambda i: (i, 0)), _const_spec((LANES, AUG_W))]
        out_specs.append(pl.BlockSpec((tm, AUG_W), lambda i: (i, 0)))
        out_shape.append(jax.ShapeDtypeStruct((S, AUG_W), BF16))
        extra = [q_cum, _placement(AUG_A, (AUG_B,))]
    if ng:
        in_specs += [_ANY] * ng
        out_specs += [_ANY] * ng
        out_shape += [_gathered_shape(p) for p in gather]
        extra = list(gather)
    outs = pl.pallas_call(
        body, name=name, grid=(n_tiles,),
        in_specs=in_specs, out_specs=out_specs, out_shape=out_shape,
        scratch_shapes=_gather_sems(ng) if ng else [],
        compiler_params=_params(),
    )(x, w, *extra)
    return outs if (aug or ng) else outs[0]


def _kv_proj(x1, w_kv, bf_row, *, tm):
    S = x1.shape[0]

    def body(x_ref, w_ref, b_ref, pk_ref, pv_ref, k_ref, v_ref, fl_ref, cum_ref, carry_ref):
        i = pl.program_id(0)

        @pl.when(i == 0)
        def _():
            carry_ref[...] = jnp.zeros_like(carry_ref)

        xb = x_ref[...].astype(BF16)
        fl = _dot(xb, w_ref[:, 2 * D_MAIN:2 * D_MAIN + LANES]) + b_ref[...]
        fl_ref[...] = fl
        log_f = jnp.minimum(fl, 0.0) - jnp.log1p(jnp.exp(-jnp.abs(fl)))
        r = lax.broadcasted_iota(jnp.int32, (tm, tm), 0)
        c = lax.broadcasted_iota(jnp.int32, (tm, tm), 1)
        tri = jnp.where(c <= r, 1.0, 0.0).astype(BF16)
        hi, mid, lo = _split3(log_f)
        cum = (_dot(tri, hi) + _dot(tri, mid)) + _dot(tri, lo) + carry_ref[0:1, :]
        cum_ref[...] = cum
        carry_ref[0:1, :] = cum[tm - 1:tm, :]
        lhs_k, lhs_v = _placement_lhs(-cum, tm), _placement_lhs(None, tm)
        for n0 in range(0, D_MAIN, 512):
            _store_augmented(k_ref, _dot(xb, w_ref[:, n0:n0 + 512]), n0 // LANES, lhs_k, pk_ref)
            _store_augmented(v_ref, _dot(xb, w_ref[:, D_MAIN + n0:D_MAIN + n0 + 512]), n0 // LANES, lhs_v, pv_ref)

    return pl.pallas_call(
        body, name="kv_proj", grid=(S // tm,),
        in_specs=[pl.BlockSpec((tm, D_MODEL), lambda i: (i, 0)),
                  _const_spec((D_MODEL, 2 * D_MAIN + LANES)), _const_spec((1, LANES)),
                  _const_spec((LANES, AUG_W)), _const_spec((LANES, AUG_W))],
        out_specs=[pl.BlockSpec((tm, AUG_W), lambda i: (i, 0)),
                   pl.BlockSpec((tm, AUG_W), lambda i: (i, 0)),
                   pl.BlockSpec((tm, LANES), lambda i: (i, 0)),
                   pl.BlockSpec((tm, LANES), lambda i: (i, 0))],
        out_shape=[jax.ShapeDtypeStruct((S, AUG_W), BF16), jax.ShapeDtypeStruct((S, AUG_W), BF16),
                   jax.ShapeDtypeStruct((S, LANES), F32), jax.ShapeDtypeStruct((S, LANES), F32)],
        scratch_shapes=[pltpu.VMEM((8, LANES), F32)],
        compiler_params=_params(),
    )(x1, w_kv, bf_row, _placement(AUG_B, (AUG_A, AUG_C)), _placement(None, (AUG_A,)))


def _gate_bwd(dq_aug, dck, fl, *, tm):
    S = fl.shape[0]
    n = S // tm

    def body(dq_ref, dck_ref, fl_ref, du_ref, df_ref, db_ref, carry_ref):
        i = pl.program_id(0)

        @pl.when(i == 0)
        def _():
            carry_ref[...] = jnp.zeros_like(carry_ref)
            db_ref[...] = jnp.zeros_like(db_ref)

        lane = lax.broadcasted_iota(jnp.int32, (tm, LANES), 1)
        half0 = lane < FOX_HEAD_DIM
        dcq = jnp.zeros((tm, LANES), F32)
        for hp in range(FOX_HEADS // 2):
            b0 = dq_ref[:, 2 * hp * LANES:(2 * hp + 1) * LANES]
            b1 = dq_ref[:, (2 * hp + 1) * LANES:(2 * hp + 2) * LANES]
            du_ref[:, hp * LANES:(hp + 1) * LANES] = (
                jnp.where(half0, b0, pltpu.roll(b1, FOX_HEAD_DIM, 1)) * FOX_SCALE).astype(BF16)
            r0 = jnp.sum(jnp.where(lane == AUG_A, b0, 0.0), axis=1, keepdims=True)
            r1 = jnp.sum(jnp.where(lane == AUG_A, b1, 0.0), axis=1, keepdims=True)
            dcq = dcq + jnp.where(lane == 2 * hp, r0, 0.0) + jnp.where(lane == 2 * hp + 1, r1, 0.0)
        dcum = dcq - dck_ref[...]
        r = lax.broadcasted_iota(jnp.int32, (tm, tm), 0)
        c = lax.broadcasted_iota(jnp.int32, (tm, tm), 1)
        tri = jnp.where(c >= r, 1.0, 0.0).astype(BF16)
        hi, mid, lo = _split3(dcum)
        rev = (_dot(tri, hi) + _dot(tri, mid)) + _dot(tri, lo) + carry_ref[0:1, :]
        carry_ref[0:1, :] = rev[0:1, :]
        fl_v = fl_ref[...]
        df = rev * (1.0 / (1.0 + jnp.exp(fl_v)))
        df_ref[...] = df
        db_ref[...] += jnp.sum(df, axis=0, keepdims=True)

    return pl.pallas_call(
        body, name="gate_bwd", grid=(n,),
        in_specs=[pl.BlockSpec((tm, AUG_W), lambda i: (n - 1 - i, 0)),
                  pl.BlockSpec((tm, LANES), lambda i: (n - 1 - i, 0)),
                  pl.BlockSpec((tm, LANES), lambda i: (n - 1 - i, 0))],
        out_specs=[pl.BlockSpec((tm, D_MAIN), lambda i: (n - 1 - i, 0)),
                   pl.BlockSpec((tm, LANES), lambda i: (n - 1 - i, 0)),
                   pl.BlockSpec((1, LANES), lambda i: (0, 0))],
        out_shape=[jax.ShapeDtypeStruct((S, D_MAIN), BF16),
                   jax.ShapeDtypeStruct((S, LANES), F32), jax.ShapeDtypeStruct((1, LANES), F32)],
        scratch_shapes=[pltpu.VMEM((8, LANES), F32)],
        compiler_params=_params(),
    )(dq_aug, dck, fl)


def _silu_and_grad(g):
    sg = 1.0 / (1.0 + jnp.exp(-g))
    return g * sg, sg * (1.0 + g * (1.0 - sg))


def _mix(kind, mode, *, h, xres=None, mkv, w_out, ln_g, ln_b=None, pool_w=None, pool_scale=None,
         ymain=None, target=None, z=None, dy=None, gather=(), pair_send=(), tm):
    S = h.shape[0]
    n = S // tm
    pool = kind == "pool"
    bwd = mode == "bwd"
    loss_head = bwd and not pool
    rev = pool and bwd
    mem_scale = MEM_HEAD_DIM ** -0.5

    def t_of(i):
        return (n - 1 - i) if rev else i

    row = lambda i: (t_of(i), 0)
    names, arrays, specs = [], [], []

    def add(name, arr, spec):
        names.append(name)
        arrays.append(arr)
        specs.append(spec)

    add("h", h, pl.BlockSpec((tm, D_IN), row))
    if pool:
        hb = tm // POOL_HALO
        add("halo", h, pl.BlockSpec((POOL_HALO, D_MAIN), lambda i: (jnp.maximum(t_of(i) * hb - 1, 0), 0)))
        add("pool_w", pool_w, _const_spec((4, POOL_GROUP, POOL_GROUP)))
        add("pool_scale", pool_scale, _const_spec((1, D_MAIN)))
    else:
        add("ymain", ymain, pl.BlockSpec((tm, D_MAIN), row))
    add("mkv", mkv, _const_spec((N_MEM, 2 * D_MEM)))
    add("w_out", w_out, _const_spec((D_MIX, D_MODEL)))
    add("ln_g", ln_g, _const_spec((1, D_MODEL)))
    if not (pool and bwd):
        add("xres", xres, pl.BlockSpec((tm, D_MODEL), row))
        add("ln_b", ln_b, _const_spec((1, D_MODEL)))
    if loss_head:
        add("target", target, pl.BlockSpec((tm, D_MODEL), row))
        add("place_do", _placement(AUG_A, ()), _const_spec((LANES, AUG_W)))
    if pool and bwd:
        add("z", z, pl.BlockSpec((tm, D_MODEL), row))
        add("dy", dy, pl.BlockSpec((tm, D_MODEL), row))
    for a, p in enumerate(gather):
        add(f"gather_src{a}", p, _ANY)
    for a, p in enumerate(pair_send):
        add(f"pair_src{a}", p, _ANY)

    onames, oshapes, ospecs = [], [], []

    def add_out(name, shape, dtype, spec):
        onames.append(name)
        oshapes.append(jax.ShapeDtypeStruct(shape, dtype))
        ospecs.append(spec)

    const2 = lambda i: (0, 0)
    if not bwd:
        add_out("z", (S, D_MODEL), F32, pl.BlockSpec((tm, D_MODEL), row))
        add_out("xout", (S, D_MODEL), F32, pl.BlockSpec((tm, D_MODEL), row))
    else:
        add_out("dz", (S, D_MODEL), F32, pl.BlockSpec((tm, D_MODEL), row))
        if pool:
            add_out("dmain", (S, D_MAIN), BF16, pl.BlockSpec((tm, D_MAIN), row))
        else:
            add_out("dmain", (S, 2 * D_MAIN), BF16, pl.BlockSpec((tm, 2 * D_MAIN), row))
        add_out("drest", (S, D_IN - D_MAIN), BF16, pl.BlockSpec((tm, D_IN - D_MAIN), row))
        add_out("dw_out", (D_MIX, D_MODEL), F32, pl.BlockSpec((D_MIX, D_MODEL), const2))
        add_out("dmkv", (N_MEM, 2 * D_MEM), F32, pl.BlockSpec((N_MEM, 2 * D_MEM), const2))
        add_out("dln_g", (1, D_MODEL), F32, pl.BlockSpec((1, D_MODEL), const2))
        add_out("dln_b", (1, D_MODEL), F32, pl.BlockSpec((1, D_MODEL), const2))
        if pool:
            add_out("dpool_w", (4, POOL_GROUP, POOL_GROUP), F32,
                    pl.BlockSpec((4, POOL_GROUP, POOL_GROUP), lambda i: (0, 0, 0)))
            add_out("dpool_scale", (1, D_MAIN), F32, pl.BlockSpec((1, D_MAIN), const2))
        else:
            add_out("loss", (1, D_MODEL), F32, pl.BlockSpec((1, D_MODEL), const2))

    for a, p in enumerate(gather):
        add_out(f"gathered{a}", (N_CHIPS,) + p.shape, p.dtype, _ANY)
    for a, p in enumerate(pair_send):
        add_out(f"from_sibling{a}", (N_CHIPS,) + p.shape[2:], p.dtype, _ANY)

    scratch = [pltpu.VMEM((tm, D_MIX), BF16),
               pltpu.VMEM((tm, D_MEM), F32)]
    if pool:
        scratch.append(pltpu.VMEM((tm + 2 * POOL_HALO, D_MAIN), F32))
    if rev:
        scratch.append(pltpu.VMEM((POOL_HALO, D_MAIN), F32))
    assert not (gather and pair_send)
    if gather:
        scratch += _gather_sems(len(gather))
    if pair_send:
        scratch += _pair_sems(len(pair_send))
    n_in, n_out = len(names), len(onames)

    def body(*refs):
        R = dict(zip(names, refs[:n_in]))
        O = dict(zip(onames, refs[n_in:n_in + n_out]))
        sc = refs[n_in + n_out:]
        yc_ref, ymem_ref = sc[0], sc[1]
        ext_ref = sc[2] if pool else None
        carry_ref = sc[3] if rev else None
        i = pl.program_id(0)
        t = t_of(i)
        h_ref = R["h"]
        gamma = R["ln_g"][...]

        if gather:
            start, forward, finish = _gather_steps(
                [R[f"gather_src{a}"] for a in range(len(gather))],
                [O[f"gathered{a}"] for a in range(len(gather))], sc[-2], sc[-1])
            pl.when(i == 0)(start)
            pl.when(i == max(n - 3, 0))(forward)
            pl.when(i == n - 1)(finish)
        if pair_send:
            start, finish = _pair_send_steps(
                [R[f"pair_src{a}"] for a in range(len(pair_send))],
                [O[f"from_sibling{a}"] for a in range(len(pair_send))], sc[-2], sc[-1])
            pl.when(i == 0)(start)
            pl.when(i == n - 1)(finish)

        if bwd:
            @pl.when(i == 0)
            def _():
                for nm in ("dw_out", "dmkv", "dln_g", "dln_b", "dpool_w", "dpool_scale", "loss"):
                    if nm in O:
                        O[nm][...] = jnp.zeros_like(O[nm])
                if rev:
                    carry_ref[...] = jnp.zeros_like(carry_ref)

        if pool:
            u = h_ref[:, 0:D_MAIN].astype(F32)
            halo = R["halo"][...].astype(F32)
            ext_ref[0:POOL_HALO, :] = jnp.where(t > 0, halo, 0.0)
            ext_ref[POOL_HALO:POOL_HALO + tm, :] = u
            tpos = t * tm + lax.broadcasted_iota(jnp.int32, (tm, 1), 0)
            pms, invcs = [], []
            for gi, w in enumerate(POOL_WINDOWS):
                cs = slice(gi * POOL_GROUP, (gi + 1) * POOL_GROUP)
                acc = ext_ref[POOL_HALO:POOL_HALO + tm, cs]
                for k in range(1, w):
                    acc = acc + ext_ref[POOL_HALO - k:POOL_HALO - k + tm, cs]
                invc = 1.0 / jnp.minimum(tpos + 1, w).astype(F32)
                pm = (acc * invc - u[:, cs]).astype(BF16)
                pms.append(pm)
                invcs.append(invc)
            mixed = [_dot(pms[gi], R["pool_w"][gi]) for gi in range(4)]
            ps = R["pool_scale"][...]
            y_main = [mixed[gi] * ps[:, gi * POOL_GROUP:(gi + 1) * POOL_GROUP] for gi in range(4)]
        else:
            y_main = [R["ymain"][:, gi * 256:(gi + 1) * 256].astype(F32) for gi in range(4)]

        probs = []
        for hd in range(MEM_HEADS):
            sl = slice(D_MAIN + hd * MEM_HEAD_DIM, D_MAIN + (hd + 1) * MEM_HEAD_DIM)
            ksl = slice(hd * MEM_HEAD_DIM, (hd + 1) * MEM_HEAD_DIM)
            vsl = slice(D_MEM + hd * MEM_HEAD_DIM, D_MEM + (hd + 1) * MEM_HEAD_DIM)
            s = _dot_nt(h_ref[:, sl], R["mkv"][:, ksl]) * mem_scale
            e = jnp.exp(s - jnp.max(s, axis=1, keepdims=True))
            p = e / jnp.sum(e, axis=1, keepdims=True)
            probs.append(p)
            ymem_ref[:, ksl] = _dot(p.astype(BF16), R["mkv"][:, vsl])

        g_off = D_MIX
        gate_d = []
        for gi in range(4):
            cs = slice(gi * 256, (gi + 1) * 256)
            gm = h_ref[:, g_off + gi * 256:g_off + (gi + 1) * 256].astype(F32)
            sv, sd = _silu_and_grad(gm)
            yc_ref[:, cs] = (y_main[gi] * sv).astype(BF16)
            gate_d.append((sv, sd))
        gq = h_ref[:, g_off + D_MAIN:D_IN].astype(F32)
        svq, sdq = _silu_and_grad(gq)
        yc_ref[:, D_MAIN:D_MIX] = (ymem_ref[...] * svq).astype(BF16)

        if pool and bwd:
            zt = R["z"][...]
        else:
            o = _dot(yc_ref[...], R["w_out"][...])
            zt = ALPHA * R["xres"][...] + o
        mu = jnp.mean(zt, axis=1, keepdims=True)
        zc = zt - mu
        var = jnp.mean(zc * zc, axis=1, keepdims=True)
        rstd = lax.rsqrt(var + LN_EPS)
        xhat = zc * rstd
        if not bwd:
            O["z"][...] = zt
            O["xout"][...] = xhat * gamma + R["ln_b"][...]
            return

        if loss_head:
            xo = xhat * gamma + R["ln_b"][...]
            err = xo - R["target"][...]
            O["loss"][...] += jnp.sum(err * err, axis=0, keepdims=True)
            dyt = err * (1.0 / D_MODEL)
        else:
            dyt = R["dy"][...]

        O["dln_g"][...] += jnp.sum(dyt * xhat, axis=0, keepdims=True)
        O["dln_b"][...] += jnp.sum(dyt, axis=0, keepdims=True)
        gdy = dyt * gamma
        m1 = jnp.mean(gdy, axis=1, keepdims=True)
        m2 = jnp.mean(gdy * xhat, axis=1, keepdims=True)
        dz = rstd * (gdy - m1 - xhat * m2)
        O["dz"][...] = dz
        dzb = dz.astype(BF16)

        for n0 in range(0, D_MIX, 512):
            O["dw_out"][n0:n0 + 512, :] += _dot_tn(yc_ref[:, n0:n0 + 512], dzb)
        dyc_mem = _dot_nt(dzb, R["w_out"][D_MAIN:D_MIX, :])

        O["drest"][:, D_MEM + D_MAIN:D_MEM + D_MAIN + D_MEM] = (dyc_mem * ymem_ref[...] * sdq).astype(BF16)
        dymem = dyc_mem * svq
        for hd in range(MEM_HEADS):
            sl = slice(D_MAIN + hd * MEM_HEAD_DIM, D_MAIN + (hd + 1) * MEM_HEAD_DIM)
            ksl = slice(hd * MEM_HEAD_DIM, (hd + 1) * MEM_HEAD_DIM)
            vsl = slice(D_MEM + hd * MEM_HEAD_DIM, D_MEM + (hd + 1) * MEM_HEAD_DIM)
            p = probs[hd]
            dyb = dymem[:, ksl].astype(BF16)
            dp = _dot_nt(dyb, R["mkv"][:, vsl])
            ds = p * (dp - jnp.sum(dp * p, axis=1, keepdims=True)) * mem_scale
            dsb = ds.astype(BF16)
            O["drest"][:, ksl] = _dot(dsb, R["mkv"][:, ksl]).astype(BF16)
            O["dmkv"][:, ksl] += _dot_tn(dsb, h_ref[:, sl])
            O["dmkv"][:, vsl] += _dot_tn(p.astype(BF16), dyb)

        dmain = []
        for gi in range(4):
            cs = slice(gi * 256, (gi + 1) * 256)
            dyc_g = _dot_nt(dzb, R["w_out"][cs, :])
            sv, sd = gate_d[gi]
            O["drest"][:, D_MEM + gi * 256:D_MEM + (gi + 1) * 256] = (dyc_g * y_main[gi] * sd).astype(BF16)
            dmain.append(dyc_g * sv)

        if not pool:
            dbf = [dmain[gi].astype(BF16).astype(F32) for gi in range(4)]
            dcol = jnp.zeros((tm, LANES), F32)
            for gi in range(4):
                dr = lax.broadcasted_iota(jnp.int32, (256, LANES), 0)
                hc = lax.broadcasted_iota(jnp.int32, (256, LANES), 1)
                sel = jnp.where(jnp.right_shift(dr, 6) + gi * 4 == hc, 1.0, 0.0).astype(BF16)
                hi, mid, lo = _split3(dbf[gi] * R["ymain"][:, gi * 256:(gi + 1) * 256].astype(F32))
                dcol = dcol + ((_dot(hi, sel) + _dot(mid, sel)) + _dot(lo, sel))
            lhs = _placement_lhs(-dcol, tm)
            for gi in range(4):
                _store_augmented(O["dmain"], dbf[gi], 2 * gi, lhs, R["place_do"])
            return

        ps = R["pool_scale"][...]
        dpm_list = []
        for gi in range(4):
            cs = slice(gi * 256, (gi + 1) * 256)
            O["dpool_scale"][:, cs] += jnp.sum(dmain[gi] * mixed[gi], axis=0, keepdims=True)
            dmix = (dmain[gi] * ps[:, cs]).astype(BF16)
            O["dpool_w"][gi] += _dot_tn(pms[gi], dmix)
            dpm = _dot_nt(dmix, R["pool_w"][gi])
            dpm_list.append(dpm)
            ext_ref[0:tm, cs] = dpm * invcs[gi]
        ext_ref[tm:tm + POOL_HALO, :] = carry_ref[...]
        carry_ref[...] = ext_ref[0:POOL_HALO, :]
        for gi, w in enumerate(POOL_WINDOWS):
            cs = slice(gi * 256, (gi + 1) * 256)
            acc = ext_ref[0:tm, cs]
            for k in range(1, w):
                acc = acc + ext_ref[k:k + tm, cs]
            O["dmain"][:, cs] = (acc - dpm_list[gi]).astype(BF16)

    outs = pl.pallas_call(
        body, name=f"mix_{kind}_{mode}", grid=(n,),
        in_specs=specs, out_specs=ospecs, out_shape=oshapes,
        scratch_shapes=scratch, compiler_params=_params(),
    )(*arrays)
    return dict(zip(onames, outs))


def _lin_bwd(xin, dhs, w, res, *, tm, name, scatter=()):
    S, K = xin.shape
    N, nc_w = _cols_and_chunk(w)
    nj, nr, ns = len(dhs), len(res), len(scatter)
    n_tiles = S // tm
    widths = [dh.shape[1] for dh in dhs]
    chunks = [nc_w if w.ndim == 3 else (512 if wd % 512 == 0 else wd) for wd in widths]
    assert sum(widths) == N and all(wd % c == 0 for wd, c in zip(widths, chunks))
    scales = [s for _, s in res]
    n_in = 2 + nj + nr + ns

    def body(*refs):
        x_ref = refs[0]
        dh_refs = refs[1:1 + nj]
        w_ref = refs[1 + nj]
        r_refs = refs[2 + nj:2 + nj + nr]
        dx_ref, dw_ref = refs[n_in], refs[n_in + 1]
        i = pl.program_id(0)

        if ns:
            start, finish = _scatter_steps(refs[n_in - ns:n_in], refs[n_in + 2:n_in + 2 + ns],
                                           refs[n_in + 2 + ns], refs[n_in + 3 + ns])
            pl.when(i == 0)(start)
            pl.when(i == n_tiles - 1)(finish)

        @pl.when(i == 0)
        def _():
            dw_ref[...] = jnp.zeros_like(dw_ref)

        xb = x_ref[...].astype(BF16)
        dx = jnp.zeros((tm, K), F32)
        for r_ref, sc in zip(r_refs, scales):
            dx = dx + sc * r_ref[...]
        off = 0
        for j in range(nj):
            nc = chunks[j]
            for n0 in range(0, widths[j], nc):
                dhb = dh_refs[j][:, n0:n0 + nc].astype(BF16)
                dx = dx + _dot_nt(dhb, _w_cols(w_ref, off + n0, nc)[...])
                _w_cols(dw_ref, off + n0, nc)[...] += _dot_tn(xb, dhb)
            off += widths[j]
        dx_ref[...] = dx

    zeros = (0,) * w.ndim
    in_specs = [pl.BlockSpec((tm, K), lambda i: (i, 0))]
    in_specs += [pl.BlockSpec((tm, n), lambda i: (i, 0)) for n in widths]
    in_specs += [_const_spec(w.shape)]
    in_specs += [pl.BlockSpec((tm, K), lambda i: (i, 0)) for _ in res]
    in_specs += [_ANY] * ns
    out_specs = [pl.BlockSpec((tm, K), lambda i: (i, 0)), pl.BlockSpec(w.shape, lambda i: zeros)] + [_ANY] * ns
    out_shape = [jax.ShapeDtypeStruct((S, K), F32), jax.ShapeDtypeStruct(w.shape, F32)]
    out_shape += [jax.ShapeDtypeStruct(p.shape, p.dtype) for p in scatter]
    return pl.pallas_call(
        body, name=name, grid=(n_tiles,),
        in_specs=in_specs, out_specs=out_specs, out_shape=out_shape,
        scratch_shapes=_scatter_sems(ns) if ns else [],
        compiler_params=_params(),
    )(xin, *dhs, w, *[r for r, _ in res], *scatter)


def _wgrad(xin, dh, *, name):
    M, K = xin.shape
    N = dh.shape[1]

    def body(x_ref, dh_ref, o_ref):
        o_ref[...] = _dot_tn(x_ref[...].astype(BF16), dh_ref[...].astype(BF16))

    return pl.pallas_call(
        body, name=name, out_shape=jax.ShapeDtypeStruct((K, N), F32),
        compiler_params=pltpu.CompilerParams(vmem_limit_bytes=VMEM_LIMIT),
    )(xin, dh)


AUG_A = FOX_HEAD_DIM
AUG_B = FOX_HEAD_DIM + 3
AUG_C = FOX_HEAD_DIM + 6
AUG_W = FOX_HEADS * LANES


def _placement(val_lane, ones_lanes):
    r = jnp.arange(LANES)[:, None]
    c = jnp.arange(AUG_W)[None, :]
    head, lane = c // LANES, c % LANES
    m = jnp.zeros((LANES, AUG_W), jnp.bool_)
    if val_lane is not None:
        for part in range(3):
            m = m | ((r == part * FOX_HEADS + head) & (lane == val_lane + part))
    for first in ones_lanes:
        m = m | ((r == 3 * FOX_HEADS) & (lane >= first) & (lane < first + 3))
    return m.astype(BF16)


def _placement_lhs(val, tm):
    lane = lax.broadcasted_iota(jnp.int32, (tm, LANES), 1)
    lhs = jnp.where(lane == 3 * FOX_HEADS, 1.0, 0.0)
    if val is not None:
        hi, mid, lo = [p.astype(F32) for p in _split3(val)]
        lhs = jnp.where(lane < FOX_HEADS, hi, jnp.where(
            lane < 2 * FOX_HEADS, pltpu.roll(mid, FOX_HEADS, 1), jnp.where(
                lane < 3 * FOX_HEADS, pltpu.roll(lo, 2 * FOX_HEADS, 1), lhs)))
    return lhs.astype(BF16)


def _store_augmented(o_ref, data, first_pair, lhs, p_ref):
    tm = data.shape[0]
    is_data = lax.broadcasted_iota(jnp.int32, (tm, LANES), 1) < FOX_HEAD_DIM
    for j in range(data.shape[1] // LANES):
        base = 2 * (first_pair + j) * LANES
        extra = _dot(lhs, p_ref[:, base:base + 2 * LANES])
        blk = data[:, j * LANES:(j + 1) * LANES]
        o_ref[:, base:base + LANES] = jnp.where(is_data, blk, extra[:, 0:LANES]).astype(BF16)
        o_ref[:, base + LANES:base + 2 * LANES] = jnp.where(
            is_data, pltpu.roll(blk, FOX_HEAD_DIM, 1), extra[:, LANES:2 * LANES]).astype(BF16)


def _cols_of_rows(rows, S):
    nh = FOX_HEADS // rows.shape[0]
    a = rows[:, :, 0:nh, :].transpose(0, 2, 1, 3).reshape(FOX_HEADS, S).T
    return jnp.pad(a, ((0, 0), (0, LANES - FOX_HEADS)))


def _fox_fwd(qf, ka, va, *, tq, nh=4):
    S = ka.shape[0]
    nq = S // tq
    tk = tq
    ng = FOX_HEADS // nh

    def body(q_ref, k_ref, v_ref, o_ref, qb_ref, *scratch):
        p_scr, m_scr, acc_scr = scratch[0:nh], scratch[nh:2 * nh], scratch[2 * nh:3 * nh]
        qi = pl.program_id(1)
        lane = lax.broadcasted_iota(jnp.int32, (tq, LANES), 1)
        half0 = lane < FOX_HEAD_DIM
        rr = lax.broadcasted_iota(jnp.int32, (tq, tk), 0)
        cc = lax.broadcasted_iota(jnp.int32, (tq, tk), 1)
        sls = [slice(hh * LANES, (hh + 1) * LANES) for hh in range(nh)]
        qs = [q_ref[:, sl] for sl in sls]

        @pl.when(qi == 0)
        def _():
            for hh in range(nh):
                p_scr[hh][...] = jnp.zeros_like(p_scr[hh])
                m_scr[hh][...] = jnp.full(m_scr[hh].shape, -jnp.inf, F32)
                acc_scr[hh][...] = jnp.zeros_like(acc_scr[hh])

        @pl.when(qi > 0)
        def _():
            for hh in range(nh):
                s = _dot_nt(qs[hh], k_ref[0:tk, sls[hh]])
                m0 = jnp.broadcast_to(jnp.max(s, axis=1, keepdims=True), (tq, LANES))
                p_scr[hh][...] = jnp.exp(s - jnp.tile(m0, (1, tk // LANES))).astype(BF16)
                m_scr[hh][...] = m0
                acc_scr[hh][...] = jnp.zeros_like(acc_scr[hh])

        def chunk(ki, masked):
            k0 = pl.multiple_of(ki * tk, tk)
            kp = pl.multiple_of(jnp.maximum(ki - 1, 0) * tk, tk)
            for hh in range(nh):
                m = m_scr[hh][...]
                s = _dot_nt(qs[hh], k_ref[pl.ds(k0, tk), sls[hh]])
                pv = _dot(p_scr[hh][...], v_ref[pl.ds(kp, tk), sls[hh]])
                if masked:
                    s = jnp.where(cc <= rr, s, -jnp.inf)
                m_new = jnp.maximum(m, jnp.max(s, axis=1, keepdims=True))
                p_scr[hh][...] = jnp.exp(s - jnp.tile(m_new, (1, tk // LANES))).astype(BF16)
                acc_scr[hh][...] = (acc_scr[hh][...] + pv) * jnp.exp(m - m_new)
                m_scr[hh][...] = m_new

        def trip(ki, c):
            chunk(ki, False)
            return c

        lax.fori_loop(1, qi, trip, 0)
        chunk(qi, True)
        kq = pl.multiple_of(qi * tk, tk)
        outs = []
        for hh in range(nh):
            m = m_scr[hh][...]
            acc = acc_scr[hh][...] + _dot(p_scr[hh][...], v_ref[pl.ds(kq, tk), sls[hh]])
            l = jnp.sum(jnp.where(lane == AUG_A, acc, 0.0), axis=1, keepdims=True)
            outs.append(acc / l)
            hi, mid, lo = _split3(-(m + jnp.log(l)))
            qb_ref[:, sls[hh]] = jnp.where(lane == AUG_C, hi, jnp.where(
                lane == AUG_C + 1, mid, jnp.where(lane == AUG_C + 2, lo, qs[hh])))
        for pr in range(nh // 2):
            o_ref[:, pr * LANES:(pr + 1) * LANES] = jnp.where(
                half0, outs[2 * pr], pltpu.roll(outs[2 * pr + 1], FOX_HEAD_DIM, 1))

    return pl.pallas_call(
        body, name="fox_fwd", grid=(ng, nq),
        in_specs=[pl.BlockSpec((tq, nh * LANES), lambda g, qi: (qi, g)),
                  pl.BlockSpec((S, nh * LANES), lambda g, qi: (0, g), pipeline_mode=pl.Buffered(1)),
                  pl.BlockSpec((S, nh * LANES), lambda g, qi: (0, g), pipeline_mode=pl.Buffered(1))],
        out_specs=[pl.BlockSpec((tq, nh * FOX_HEAD_DIM), lambda g, qi: (qi, g)),
                   pl.BlockSpec((tq, nh * LANES), lambda g, qi: (qi, g))],
        out_shape=[jax.ShapeDtypeStruct((S, D_MAIN), F32),
                   jax.ShapeDtypeStruct((S, AUG_W), BF16)],
        scratch_shapes=([pltpu.VMEM((tq, tk), BF16)] * nh + [pltpu.VMEM((tq, LANES), F32)] * nh
                        + [pltpu.VMEM((tq, LANES), F32)] * nh),
        compiler_params=_params(2),
    )(qf, ka, va)


def _fox_bwd(qb, ka, va, do_aug, *, tq):
    S = ka.shape[0]
    nq = S // tq
    tk = tq

    def body(k_ref, v_ref, q_ref, do_ref, dq_ref, dk_ref, dv_ref, dck_ref, dk_scr, dv_scr):
        kj = pl.program_id(1)

        @pl.when(kj == 0)
        def _():
            dq_ref[...] = jnp.zeros_like(dq_ref)

        lane = lax.broadcasted_iota(jnp.int32, (tk, LANES), 1)
        half0 = lane < FOX_HEAD_DIM
        rr = lax.broadcasted_iota(jnp.int32, (tk, tq), 0)
        cc = lax.broadcasted_iota(jnp.int32, (tk, tq), 1)
        sls = [slice(hh * LANES, (hh + 1) * LANES) for hh in range(2)]
        kts = [k_ref[:, sl] for sl in sls]
        vts = [v_ref[:, sl] for sl in sls]

        dk_scr[...] = jnp.zeros_like(dk_scr)
        dv_scr[...] = jnp.zeros_like(dv_scr)

        def chunk(qi, masked):
            q0 = pl.multiple_of(qi * tq, tq)
            for hh in range(2):
                qc = q_ref[pl.ds(q0, tq), sls[hh]]
                doc = do_ref[pl.ds(q0, tq), sls[hh]]
                pt = jnp.exp(_dot_nt(kts[hh], qc))
                if masked:
                    pt = jnp.where(rr <= cc, pt, 0.0)
                dsb = (pt * _dot_nt(vts[hh], doc)).astype(BF16)
                dv_scr[hh] += _dot(pt.astype(BF16), doc)
                dk_scr[hh] += _dot(dsb, qc)
                dq_ref[pl.ds(q0, tq), sls[hh]] += _dot_tn(dsb, kts[hh])

        def trip(qi, c):
            chunk(qi, False)
            return c

        chunk(kj, True)
        lax.fori_loop(kj + 1, nq, trip, 0)
        dk0, dk1 = dk_scr[0], dk_scr[1]
        dv0, dv1 = dv_scr[0], dv_scr[1]
        dk_ref[...] = jnp.where(half0, dk0, pltpu.roll(dk1, FOX_HEAD_DIM, 1)).astype(BF16)
        dv_ref[...] = jnp.where(half0, dv0, pltpu.roll(dv1, FOX_HEAD_DIM, 1)).astype(BF16)
        c0 = jnp.sum(jnp.where(lane == AUG_B, dk0, 0.0), axis=1, keepdims=True)
        c1 = jnp.sum(jnp.where(lane == AUG_B, dk1, 0.0), axis=1, keepdims=True)
        dck_cols = jnp.where(lane == 0, c0, 0.0) + jnp.where(lane == 1, c1, 0.0)
        dck_ref[0, 0] = dck_cols.T[0:8, :]

    return pl.pallas_call(
        body, name="fox_bwd", grid=(8, nq),
        in_specs=[pl.BlockSpec((tk, 2 * LANES), lambda hp, kj: (kj, hp)),
                  pl.BlockSpec((tk, 2 * LANES), lambda hp, kj: (kj, hp)),
                  pl.BlockSpec((S, 2 * LANES), lambda hp, kj: (0, hp)),
                  pl.BlockSpec((S, 2 * LANES), lambda hp, kj: (0, hp))],
        out_specs=[pl.BlockSpec((S, 2 * LANES), lambda hp, kj: (0, hp)),
                   pl.BlockSpec((tk, LANES), lambda hp, kj: (kj, hp)),
                   pl.BlockSpec((tk, LANES), lambda hp, kj: (kj, hp)),
                   pl.BlockSpec((1, 1, 8, tk), lambda hp, kj: (hp, kj, 0, 0))],
        out_shape=[jax.ShapeDtypeStruct((S, AUG_W), F32),
                   jax.ShapeDtypeStruct((S, D_MAIN), BF16),
                   jax.ShapeDtypeStruct((S, D_MAIN), BF16),
                   jax.ShapeDtypeStruct((8, nq, 8, tk), F32)],
        scratch_shapes=[pltpu.VMEM((2, tk, LANES), F32), pltpu.VMEM((2, tk, LANES), F32)],
        compiler_params=_params(2),
    )(ka, va, qb, do_aug)


def _adamw(w, g, m, v, *, name):
    Rr, C = w.shape
    tr = 256 if Rr % 256 == 0 else Rr
    c1 = 1.0 / (1.0 - ADAM_B1 ** ADAM_STEP)
    c2 = 1.0 / (1.0 - ADAM_B2 ** ADAM_STEP)

    def body(w_ref, g_ref, m_ref, v_ref, d_ref, nm_ref, nv_ref):
        gv = g_ref[...]
        nm = ADAM_B1 * m_ref[...] + (1.0 - ADAM_B1) * gv
        nv = ADAM_B2 * v_ref[...] + (1.0 - ADAM_B2) * (gv * gv)
        d_ref[...] = -ADAM_LR * ((nm * c1) / (jnp.sqrt(nv * c2) + ADAM_EPS) + ADAM_WD * w_ref[...])
        nm_ref[...] = nm
        nv_ref[...] = nv

    spec = pl.BlockSpec((tr, C), lambda i: (i, 0))
    sds = jax.ShapeDtypeStruct((Rr, C), F32)
    return pl.pallas_call(
        body, name=name, grid=(Rr // tr,),
        in_specs=[spec] * 4, out_specs=[spec] * 3, out_shape=[sds] * 3,
        compiler_params=_params(),
    )(w, g, m, v)


_ANY = pl.BlockSpec(memory_space=pl.ANY)
_MESH = pl.DeviceIdType.MESH


def _place():
    x, y, c = lax.axis_index("x"), lax.axis_index("y"), lax.axis_index("c")
    return x, y, c


def _gather_steps(p_refs, out_refs, send_sems, recv_sems):
    x, y, c = _place()
    sib = (x, y, 1 - c)
    chips = [(1 - x, y), (x, 1 - y), (1 - x, 1 - y)]
    idx = [2 * chip[0] + chip[1] for chip in chips]
    me = 2 * x + y
    na = len(p_refs)

    def copy(a, k, chip_idx, half, to, src=None):
        dst = out_refs[a].at[chip_idx, half]
        return pltpu.make_async_remote_copy(
            src_ref=dst if src is None else src, dst_ref=dst,
            send_sem=send_sems.at[6 * a + k], recv_sem=recv_sems.at[6 * a + k],
            device_id=to, device_id_type=_MESH)

    first = [copy(a, j, me, c, (*chips[j], c), src=p_refs[a].at[c]) for a in range(na) for j in range(3)]
    passed = [copy(a, 3 + j, idx[j], c, sib) for a in range(na) for j in range(3)]

    def start():
        for cp in first:
            cp.start()

    def forward():
        for a in range(na):
            for j in range(3):
                copy(a, j, idx[j], c, sib).wait_recv()
                passed[3 * a + j].start()

    def finish():
        for a in range(na):
            for j in range(3):
                copy(a, 3 + j, idx[j], 1 - c, sib).wait_recv()
        for cp in first + passed:
            cp.wait_send()

    return start, forward, finish


def _sems(n):
    return [pltpu.SemaphoreType.DMA((n,)), pltpu.SemaphoreType.DMA((n,))]


def _gather_sems(na):
    return _sems(6 * na)


def _scatter_sems(na):
    return _sems(3 * na)


def _pair_sems(na):
    return _sems(N_CHIPS * na)


def _gathered_shape(pack):
    return jax.ShapeDtypeStruct((N_CHIPS,) + pack.shape, pack.dtype)


def _from_sibling_shape(gpack):
    return jax.ShapeDtypeStruct((N_CHIPS,) + gpack.shape[2:], gpack.dtype)


def _all_gather_shards(packs):
    na = len(packs)

    def body(*refs):
        for step in _gather_steps(refs[0:na], refs[na:2 * na], refs[2 * na], refs[2 * na + 1]):
            step()

    return pl.pallas_call(
        body, name="all_gather_shards",
        in_specs=[_ANY] * na, out_specs=[_ANY] * na, out_shape=[_gathered_shape(p) for p in packs],
        scratch_shapes=_gather_sems(na),
    )(*packs)


def _pair_send_steps(g_refs, out_refs, send_sem, recv_sem):
    x, y, c = _place()
    cps = [pltpu.make_async_remote_copy(
        src_ref=g_refs[a].at[j, 1 - c], dst_ref=out_refs[a].at[j],
        send_sem=send_sem.at[N_CHIPS * a + j], recv_sem=recv_sem.at[N_CHIPS * a + j],
        device_id=(x, y, 1 - c), device_id_type=_MESH) for a in range(len(g_refs)) for j in range(N_CHIPS)]

    def start():
        for cp in cps:
            cp.start()

    def finish():
        for cp in cps:
            cp.wait_recv()
        for cp in cps:
            cp.wait_send()

    return start, finish


def _send_half_to_sibling(gpacks, tag):
    na = len(gpacks)

    def body(*refs):
        for step in _pair_send_steps(refs[0:na], refs[na:2 * na], refs[2 * na], refs[2 * na + 1]):
            step()

    return pl.pallas_call(
        body, name=f"pair_send{tag}",
        in_specs=[_ANY] * na, out_specs=[_ANY] * na, out_shape=[_from_sibling_shape(g) for g in gpacks],
        scratch_shapes=_pair_sems(na),
    )(*gpacks)


def _pair_sum(gpack, recv, c_arr, tag, *, tr=PACK_TILE):
    rows, lanes = recv.shape[1:]
    assert rows % tr == 0

    def body(c_ref, a_ref, b_ref, o_ref):
        o_ref[...] = (a_ref[...] + b_ref[...]).astype(BF16)

    grid_spec = pltpu.PrefetchScalarGridSpec(
        num_scalar_prefetch=1, grid=(N_CHIPS, rows // tr),
        in_specs=[pl.BlockSpec((None, None, tr, lanes), lambda j, i, c_ref: (j, c_ref[0], i, 0)),
                  pl.BlockSpec((None, tr, lanes), lambda j, i, c_ref: (j, i, 0))],
        out_specs=pl.BlockSpec((None, tr, lanes), lambda j, i, c_ref: (j, i, 0)))
    return pl.pallas_call(
        body, name=f"pair_sum{tag}", grid_spec=grid_spec,
        out_shape=jax.ShapeDtypeStruct((N_CHIPS, rows, lanes), BF16),
        compiler_params=_params(2),
    )(c_arr, gpack, recv)


def _scatter_steps(p_refs, out_refs, send_sems, recv_sems):
    x, y, c = _place()
    chips = [(1 - x, y), (x, 1 - y), (1 - x, 1 - y)]
    me = 2 * x + y
    cps = [pltpu.make_async_remote_copy(
        src_ref=p_refs[a].at[2 * chip[0] + chip[1]], dst_ref=out_refs[a].at[me],
        send_sem=send_sems.at[3 * a + j], recv_sem=recv_sems.at[3 * a + j],
        device_id=(*chip, c), device_id_type=_MESH) for a in range(len(p_refs)) for j, chip in enumerate(chips)]

    def start():
        for cp in cps:
            cp.start()

    def finish():
        for cp in cps:
            cp.wait_recv()
        for cp in cps:
            cp.wait_send()

    return start, finish


def _share_steps(row_ref, out_ref, send_sems, recv_sems):
    x, y, c = _place()
    mine = 4 * x + 2 * y + c
    cps = []
    for k in range(1, 8):
        fx, fy, fc = (k >> 2) & 1, (k >> 1) & 1, k & 1
        peer = (x + fx - 2 * x * fx, y + fy - 2 * y * fy, c + fc - 2 * c * fc)
        cps.append(pltpu.make_async_remote_copy(
            src_ref=row_ref, dst_ref=out_ref.at[mine], send_sem=send_sems.at[k - 1], recv_sem=recv_sems.at[k - 1],
            device_id=peer, device_id_type=_MESH))

    def start():
        for cp in cps:
            cp.start()

    def finish():
        for cp in cps:
            cp.wait_recv()
        for cp in cps:
            cp.wait_send()

    return start, finish


def _scatter_pieces(psums, tag, share=None):
    na = len(psums)
    ns = 0 if share is None else 1

    def body(*refs):
        n_in = na + ns
        steps = [_scatter_steps(refs[0:na], refs[n_in:n_in + na], refs[2 * n_in], refs[2 * n_in + 1])]
        if ns:
            steps.append(_share_steps(refs[na], refs[n_in + na], refs[2 * n_in + 2], refs[2 * n_in + 3]))
        for phase in range(2):
            for st in steps:
                st[phase]()

    out_shape = [jax.ShapeDtypeStruct(p.shape, p.dtype) for p in psums]
    if ns:
        out_shape.append(jax.ShapeDtypeStruct((8,) + share.shape, share.dtype))
    return pl.pallas_call(
        body, name=f"scatter_pieces{tag}",
        in_specs=[_ANY] * (na + ns), out_specs=[_ANY] * (na + ns), out_shape=out_shape,
        scratch_shapes=_scatter_sems(na) + (_sems(7) if ns else []),
    )(*psums, *([share] if ns else []))


def _sum_pieces(pieces, tag, *, tr=PACK_TILE):
    rows, lanes = pieces.shape[1:]
    assert rows % tr == 0

    def body(p_ref, o_ref):
        acc = p_ref[0].astype(F32) + p_ref[1].astype(F32)
        acc = acc + p_ref[2].astype(F32)
        o_ref[...] = acc + p_ref[3].astype(F32)

    return pl.pallas_call(
        body, name=f"sum_pieces{tag}", grid=(rows // tr,),
        in_specs=[pl.BlockSpec((N_CHIPS, tr, lanes), lambda i: (0, i, 0))],
        out_specs=pl.BlockSpec((tr, lanes), lambda i: (i, 0)),
        out_shape=jax.ShapeDtypeStruct((rows, lanes), F32),
        compiler_params=_params(),
    )(pieces)


def _exchange_halves(totals):
    n = len(totals)

    def body(*refs):
        t_refs, out_refs, send_sem, recv_sem = refs[:n], refs[n:2 * n], refs[2 * n], refs[2 * n + 1]
        x, y, c = _place()
        cps = [pltpu.make_async_remote_copy(
            src_ref=t_refs[i], dst_ref=out_refs[i].at[c], send_sem=send_sem.at[i], recv_sem=recv_sem.at[i],
            device_id=(x, y, 1 - c), device_id_type=_MESH) for i in range(n)]
        for cp in cps:
            cp.start()
        for cp in cps:
            cp.wait_recv()
        for cp in cps:
            cp.wait_send()

    return pl.pallas_call(
        body, name="exchange_halves",
        in_specs=[_ANY] * n, out_specs=[_ANY] * n,
        out_shape=[jax.ShapeDtypeStruct((2,) + t.shape, F32) for t in totals],
        scratch_shapes=[pltpu.SemaphoreType.DMA((n,)), pltpu.SemaphoreType.DMA((n,))],
    )(*totals)


def _pad_rows(a, rows):
    return jnp.pad(a, ((0, rows - a.shape[0]), (0, 0)))


def _pack_weight_shards(w_in, w_mem_kv, w_out, pool_w, w_kv_shared, pool_scale):
    ps_bits = lax.bitcast_convert_type(pool_scale.reshape(-1), BF16).reshape(1, -1)
    ps_row = jnp.pad(ps_bits, ((0, 0), (0, 1024 - ps_bits.shape[1])))

    def common(l):
        return [w_mem_kv[l].astype(BF16).reshape(ROWS_W_MKV, 1024),
                w_out[l].astype(BF16).reshape(ROWS_W_OUT, 1024)]

    p0 = common(0) + [pool_w.astype(BF16).reshape(ROWS_POOL_W, 1024), _pad_rows(ps_row, ROWS_SMALL),
                      jnp.zeros((PACK0_ROWS - OFF_LN_G, 1024), BF16)]
    p1 = common(1) + [_pad_rows(w_kv_shared.astype(BF16).reshape(KV_SHARD, 1024), ROWS_W_KV),
                      jnp.zeros((PACK1_ROWS - OFF_BF, 1024), BF16)]
    w_in_halves = w_in.astype(BF16).reshape(2, 2, D_MODEL // 2, W_IN_SHARD)
    return ([jnp.concatenate(p0, axis=0).reshape(2, PACK0_ROWS // 2, 1024), w_in_halves[0]],
            [jnp.concatenate(p1, axis=0).reshape(2, PACK1_ROWS // 2, 1024), w_in_halves[1]])


def _unpack_w_in(g_in):
    return g_in.reshape(N_CHIPS, D_MODEL, W_IN_SHARD)


def _unpack_common(g):
    w_mkv = g[:, OFF_W_MKV:OFF_W_MKV + ROWS_W_MKV].reshape(D_MODEL, 2 * D_MEM)
    w_out = g[:, OFF_W_OUT:OFF_W_OUT + ROWS_W_OUT].reshape(D_MIX, D_MODEL)
    return w_mkv, w_out


def _unpack_weights0(g):
    pool_w = g[:, OFF_POOL_W:OFF_POOL_W + ROWS_POOL_W].reshape(4, 4, POOL_GROUP // 4, POOL_GROUP)
    pool_w = pool_w.transpose(1, 0, 2, 3).reshape(4, POOL_GROUP, POOL_GROUP)
    ps_bits = g[:, OFF_POOL_S, 0:512].reshape(4, 256, 2)
    pool_scale = lax.bitcast_convert_type(ps_bits, F32).reshape(1, D_MAIN)
    return _unpack_common(g) + (pool_w, pool_scale)


def _unpack_weights1(g):
    w_kv = g[:, OFF_W_KV:OFF_W_KV + KV_SHARD].reshape(4, D_MODEL, KV_SHARD)
    w_kv = w_kv.transpose(1, 0, 2).reshape(D_MODEL, KV_COLS)
    return _unpack_common(g) + (w_kv,)


def _replicated_rows(a):
    a = _pad_rows(a, ROWS_SMALL)
    return jnp.broadcast_to(a[None], (4,) + a.shape)


def _pack_common_grads(g_w_mkv, g_w_out):
    return [g_w_mkv.reshape(4, ROWS_W_MKV, 1024), g_w_out.reshape(4, ROWS_W_OUT, 1024)]


def _w_in_grad_halves(g_w_in):
    return g_w_in.reshape(N_CHIPS, 2, D_MODEL // 2, W_IN_SHARD)


def _pack_grads0(g_w_mkv, g_w_out, g_pool_w, g_pool_scale, g_ln_g, g_ln_b):
    parts = _pack_common_grads(g_w_mkv, g_w_out) + [
        g_pool_w.reshape(4, 4, POOL_GROUP // 4, POOL_GROUP).transpose(1, 0, 2, 3).reshape(4, ROWS_POOL_W, 1024),
        jnp.pad(g_pool_scale.reshape(4, 1, 256), ((0, 0), (0, ROWS_SMALL - 1), (0, 1024 - 256))),
        _replicated_rows(g_ln_g), _replicated_rows(g_ln_b),
        jnp.zeros((4, PACK0_ROWS - OFF_LN_B - ROWS_SMALL, 1024), F32),
    ]
    return jnp.concatenate(parts, axis=1).reshape(4, 2, PACK0_ROWS // 2, 1024)


def _pack_grads1(g_w_mkv, g_w_out, g_w_kv, g_bf):
    parts = _pack_common_grads(g_w_mkv, g_w_out) + [
        jnp.pad(g_w_kv.reshape(D_MODEL, 4, KV_SHARD).transpose(1, 0, 2).reshape(4, KV_SHARD, 1024),
                ((0, 0), (0, ROWS_W_KV - KV_SHARD), (0, 0))),
        _replicated_rows(jnp.pad(g_bf.reshape(1, -1), ((0, 0), (0, 1024 - g_bf.shape[0])))),
        jnp.zeros((4, PACK1_ROWS - OFF_BF - ROWS_SMALL, 1024), F32),
    ]
    return jnp.concatenate(parts, axis=1).reshape(4, 2, PACK1_ROWS // 2, 1024)


def _local_step(x, mem, target, w0, w1, ln_g, ln_b, b_forget, *, tm=256, tq=512, dist=None):
    S = x.shape[0]
    g_rows = [ln_g[l:l + 1] for l in range(2)]
    b_rows = [ln_b[l:l + 1] for l in range(2)]
    bf_row = jnp.pad(b_forget.reshape(1, -1), ((0, 0), (0, LANES - FOX_HEADS)))

    def own_slot(gathered, pack):
        return lax.dynamic_update_slice(gathered, pack[None], (dist["me"], 0, 0, 0))

    if dist is None:
        w_in0, w_mkv0, w_out0, pool_w, pool_scale = w0
        h0 = _linear_fwd(x, w_in0, tm=2 * tm, name="in_proj0")
    else:
        w_in0, pack0 = w0
        h0, g0 = _linear_fwd(x, w_in0, tm=2 * tm, name="in_proj0", gather=[pack0])
        w_mkv0, w_out0, pool_w, pool_scale = _unpack_weights0(
            own_slot(g0, pack0).reshape(N_CHIPS, PACK0_ROWS, 1024))
    mkv0 = _linear_fwd(mem, w_mkv0, tm=N_MEM, name="mem_kv0")
    f0 = _mix("pool", "fwd", h=h0, xres=x, mkv=mkv0, w_out=w_out0, ln_g=g_rows[0], ln_b=b_rows[0],
              pool_w=pool_w, pool_scale=pool_scale, gather=() if dist is None else w1, tm=tm)
    z0, x1 = f0["z"], f0["xout"]
    if dist is not None:
        g1, g1_in = [own_slot(f0[f"gathered{a}"], w1[a]) for a in range(2)]
        w1 = (_unpack_w_in(g1_in),) + _unpack_weights1(g1.reshape(N_CHIPS, PACK1_ROWS, 1024))
    w_in1, w_mkv1, w_out1, w_kv = w1
    w_in, w_out = [w_in0, w_in1], [w_out0, w_out1]
    w_kvp = jnp.pad(w_kv, ((0, 0), (0, LANES - FOX_HEADS)))
    mkv = [mkv0, _linear_fwd(mem, w_mkv1, tm=N_MEM, name="mem_kv1")]
    ka, va, fl, cum = _kv_proj(x1, w_kvp, bf_row, tm=2 * tm)
    h1, qf = _linear_fwd(x1, w_in[1], tm=2 * tm, name="in_proj1", q_cum=cum)
    ymain1, qb = _fox_fwd(qf, ka, va, tq=tq)

    b1 = _mix("fox", "bwd", h=h1, xres=x1, mkv=mkv[1], w_out=w_out[1], ln_g=g_rows[1], ln_b=b_rows[1],
              ymain=ymain1, target=target, tm=tm)
    dq_aug, dk, dv, dck_rows = _fox_bwd(qb, ka, va, b1["dmain"], tq=tq)
    du1, df, dbf = _gate_bwd(dq_aug, _cols_of_rows(dck_rows, S), fl, tm=2 * tm)

    dx1a, dw_in1 = _lin_bwd(x1, [du1, b1["drest"]], w_in[1], [(b1["dz"], ALPHA)], tm=tm, name="in_proj1_bwd")
    dx1, dw_kvp = _lin_bwd(x1, [dk, dv, df], w_kvp, [(dx1a, 1.0)], tm=2 * tm, name="kv_proj_bwd")

    dw_mkv1 = _wgrad(mem, b1["dmkv"], name="mem_kv1_bwd")
    g_w_kv, g_bf = dw_kvp[:, 0:KV_COLS], dbf[0, 0:FOX_HEADS]

    gpacks1, psums1 = (), ()
    if dist is not None:
        gpacks1 = [_pack_grads1(dw_mkv1, b1["dw_out"], g_w_kv, g_bf), _w_in_grad_halves(dw_in1)]
    b0 = _mix("pool", "bwd", h=h0, mkv=mkv[0], w_out=w_out[0], ln_g=g_rows[0],
              pool_w=pool_w, pool_scale=pool_scale, z=z0, dy=dx1, pair_send=gpacks1, tm=tm)
    if dist is not None:
        psums1 = [_pair_sum(g, b0[f"from_sibling{a}"], dist["c_arr"], f"1{'ab'[a]}") for a, g in enumerate(gpacks1)]
    outs = _lin_bwd(x, [b0["dmain"], b0["drest"]], w_in[0], [(b0["dz"], ALPHA)], tm=tm, name="in_proj0_bwd",
                    scatter=psums1)
    dx, dw_in0 = outs[0], outs[1]
    dw_mkv0 = _wgrad(mem, b0["dmkv"], name="mem_kv0_bwd")
    g_ln_g = jnp.concatenate([b0["dln_g"], b1["dln_g"]], axis=0)
    g_ln_b = jnp.concatenate([b0["dln_b"], b1["dln_b"]], axis=0)

    if dist is None:
        grads = dict(w_in=[dw_in0, dw_in1], w_mem_kv=[dw_mkv0, dw_mkv1], w_out=[b0["dw_out"], b1["dw_out"]],
                     ln_g=g_ln_g, ln_b=g_ln_b, pool_w=b0["dpool_w"], pool_scale=b0["dpool_scale"],
                     w_kv=g_w_kv, b_forget=g_bf)
        return b1["loss"], dx, grads

    me, my_c = dist["me"], dist["my_c"]

    def with_own(pieces, psum):
        own = lax.dynamic_slice(psum, (me, 0, 0), (1,) + psum.shape[1:])
        return lax.dynamic_update_slice(pieces, own, (me, 0, 0))

    totals1 = [_sum_pieces(with_own(outs[2 + a], p), f"1{'ab'[a]}") for a, p in enumerate(psums1)]
    gpacks0 = [_pack_grads0(dw_mkv0, b0["dw_out"], b0["dpool_w"], b0["dpool_scale"], g_ln_g, g_ln_b),
               _w_in_grad_halves(dw_in0)]
    sib0 = _send_half_to_sibling(gpacks0, 0)
    psums0 = [_pair_sum(g, sib0[a], dist["c_arr"], f"0{'ab'[a]}") for a, g in enumerate(gpacks0)]
    loss_row = jnp.broadcast_to(0.5 / D_MODEL * jnp.sum(b1["loss"]), (8, LANES))
    pieces0 = _scatter_pieces(psums0, 0, share=loss_row)
    losses = lax.dynamic_update_slice(pieces0[2], loss_row[None], (2 * me + my_c, 0, 0))
    loss = jnp.sum(losses[:, 0, 0])
    totals0 = [_sum_pieces(with_own(pieces0[a], p), f"0{'ab'[a]}") for a, p in enumerate(psums0)]
    totals = totals0 + totals1
    halves = _exchange_halves(totals)
    full = [lax.dynamic_update_slice(h, t[None], (my_c, 0, 0)) for h, t in zip(halves, totals)]
    shard0, shard1 = full[0].reshape(PACK0_ROWS, 1024), full[2].reshape(PACK1_ROWS, 1024)
    g_w_in = jnp.stack([full[1].reshape(D_MODEL, W_IN_SHARD), full[3].reshape(D_MODEL, W_IN_SHARD)])
    return loss, dx, shard0, shard1, g_w_in


def kernel(x, mem, w_in, w_mem_kv, w_out, ln_g, ln_b, pool_w, pool_scale, w_kv_shared, b_forget, loss_target, m_w_in, m_w_mem_kv, m_w_out, m_ln_g, m_ln_b, m_pool_w, m_pool_scale, m_w_kv_shared, m_b_forget, v_w_in, v_w_mem_kv, v_w_out, v_ln_g, v_ln_b, v_pool_w, v_pool_scale, v_w_kv_shared, v_b_forget):
    dist = dict(c_arr=lax.axis_index("c").astype(jnp.int32).reshape(1),
                me=2 * lax.axis_index("x") + lax.axis_index("y"), my_c=lax.axis_index("c"))

    wpacks0, wpacks1 = _pack_weight_shards(w_in, w_mem_kv, w_out, pool_w, w_kv_shared, pool_scale)
    g0_in = lax.dynamic_update_slice(_all_gather_shards([wpacks0[1]])[0], wpacks0[1][None], (dist["me"], 0, 0, 0))
    w0 = (_unpack_w_in(g0_in), wpacks0[0])

    loss, dx, shard0, shard1, g_w_in = _local_step(x[0], mem[0], loss_target[0], w0, wpacks1, ln_g, ln_b,
                                                   b_forget, dist=dist)

    def per_layer(off, rows, shape):
        return jnp.concatenate([shard0[off:off + rows], shard1[off:off + rows]], axis=0).reshape(shape)

    g_w_mkv = per_layer(OFF_W_MKV, ROWS_W_MKV, w_mem_kv.shape)
    g_w_out = per_layer(OFF_W_OUT, ROWS_W_OUT, w_out.shape)
    g_pool_w = shard0[OFF_POOL_W:OFF_POOL_W + ROWS_POOL_W].reshape(pool_w.shape)
    g_w_kv = shard1[OFF_W_KV:OFF_W_KV + KV_SHARD].reshape(w_kv_shared.shape)
    g_pool_scale = shard0[OFF_POOL_S:OFF_POOL_S + 1, 0:256].reshape(pool_scale.shape)
    g_ln_g = shard0[OFF_LN_G:OFF_LN_G + 2]
    g_ln_b = shard0[OFF_LN_B:OFF_LN_B + 2]
    g_bf = shard1[OFF_BF, 0:FOX_HEADS]

    names = ["w_in", "w_mem_kv", "w_out", "ln_g", "ln_b", "pool_w", "pool_scale", "w_kv_shared", "b_forget"]
    ws = [w_in, w_mem_kv, w_out, ln_g, ln_b, pool_w, pool_scale, w_kv_shared, b_forget]
    gs = [g_w_in, g_w_mkv, g_w_out, g_ln_g, g_ln_b, g_pool_w, g_pool_scale, g_w_kv, g_bf]
    ms = [m_w_in, m_w_mem_kv, m_w_out, m_ln_g, m_ln_b, m_pool_w, m_pool_scale, m_w_kv_shared, m_b_forget]
    vs = [v_w_in, v_w_mem_kv, v_w_out, v_ln_g, v_ln_b, v_pool_w, v_pool_scale, v_w_kv_shared, v_b_forget]
    deltas, new_ms, new_vs = [], [], []
    for nm, w, gg, mm, vv in zip(names, ws, gs, ms, vs):
        two_d = (-1, w.shape[-1])
        d, nmm, nvv = _adamw(w.reshape(two_d), gg.reshape(two_d), mm.reshape(two_d), vv.reshape(two_d),
                             name=f"adamw_{nm}")
        deltas.append(d.reshape(w.shape))
        new_ms.append(nmm.reshape(w.shape))
        new_vs.append(nvv.reshape(w.shape))

    return (loss, dx[None], *gs, *deltas, *new_ms, *new_vs)
```
